```python
import jax
import jax.numpy as jnp
from jax import lax
import numpy as np

D_MODEL = 1024
BATCH = 32
SEQ = 256
DEPTH = 2
DEC_BATCH = 4
DEC_SEQ = 1024
PAST_LEN = 256

GRID_W = 64
HEAD_DIM = 64
ATT_HEADS = 8
ATT_KV = 2
ATT_GROUP = ATT_HEADS // ATT_KV
WIN_HEADS = 8
WIN_KV = 2
WIN_GROUP = WIN_HEADS // WIN_KV
WINDOW = 128
Q_BLOCK = 128
ML_HEADS = 4
ML_DK = 64
ML_DV = 128
ML_CHUNK = 128
BRANCH_W = 512
N_BRANCH = 3
ROPE_THETA = 10000.0
AXIS_DIM = HEAD_DIM // 2
LN_EPS = 1e-6
RMS_EPS = 1e-6
F32 = jnp.float32

SPLIT_SIZES = (
    ATT_HEADS * HEAD_DIM, ATT_KV * HEAD_DIM, ATT_KV * HEAD_DIM, BRANCH_W,
    ML_HEADS * ML_DK, ML_HEADS * ML_DK, ML_HEADS * ML_DV, ML_HEADS * ML_DV,
    4 * ML_HEADS, BRANCH_W,
    WIN_HEADS * HEAD_DIM, WIN_KV * HEAD_DIM, WIN_KV * HEAD_DIM, BRANCH_W,
    N_BRANCH * D_MODEL,
)
N_IN = sum(SPLIT_SIZES)

kernel_name = 'hybrid_diffusion_prefix_trunk'


def layer_norm(x):
    xf = x.astype(F32)
    mu = jnp.mean(xf, axis=-1, keepdims=True)
    var = jnp.mean(jnp.square(xf - mu), axis=-1, keepdims=True)
    return (xf - mu) * lax.rsqrt(var + LN_EPS)


def rms_norm(x, g):
    xf = x.astype(F32)
    y = xf * lax.rsqrt(jnp.mean(jnp.square(xf), axis=-1, keepdims=True) + RMS_EPS)
    return (y * g.astype(F32)).astype(x.dtype)


def split_columns(p):
    idx = [int(s) for s in np.cumsum(SPLIT_SIZES)[:-1]]
    return jnp.split(p, idx, axis=-1)


def q_heads(x, n_kv, n_group):
    B, T, _ = x.shape
    return x.reshape(B, T, n_kv, n_group, HEAD_DIM).transpose(0, 2, 3, 1, 4)


def kv_heads(x, n_kv):
    B, T, _ = x.shape
    return x.reshape(B, T, n_kv, HEAD_DIM).transpose(0, 2, 1, 3)


def merge_heads(o):
    B, KV, G, T, HD = o.shape
    return o.transpose(0, 3, 1, 2, 4).reshape(B, T, KV * G * HD)


def axial_rope(T, dtype):
    rows = T // GRID_W
    row = jnp.repeat(jnp.arange(rows, dtype=F32), GRID_W)
    col = jnp.tile(jnp.arange(GRID_W, dtype=F32), rows)
    inv = ROPE_THETA ** (-jnp.arange(0, AXIS_DIM, 2, dtype=F32) / AXIS_DIM)
    ar = row[:, None] * inv
    ac = col[:, None] * inv
    ang = jnp.concatenate([ar, ar, ac, ac], axis=-1)
    return jnp.cos(ang).astype(dtype), jnp.sin(ang).astype(dtype)


def apply_rope(x, cos, sin):
    half = AXIS_DIM // 2

    def rot(h):
        return jnp.concatenate([-h[..., half:], h[..., :half]], axis=-1)

    xr = jnp.concatenate([rot(x[..., :AXIS_DIM]), rot(x[..., AXIS_DIM:])], axis=-1)
    return x * cos + xr * sin


def softmax_with_sink(s, sink):
    s = s.astype(F32)
    if sink is None:
        return jax.nn.softmax(s, axis=-1)
    sk = sink.astype(F32).reshape((1,) + sink.shape + (1,) * (s.ndim - 3))
    mx = jnp.maximum(jnp.max(s, axis=-1, keepdims=True), sk)
    e = jnp.exp(s - mx)
    return e / (jnp.sum(e, axis=-1, keepdims=True) + jnp.exp(sk - mx))


def attend_blocks(q, k, v, sink):
    B, KV, G, T, HD = q.shape
    nb = T // Q_BLOCK
    qb = jnp.moveaxis(q.reshape(B, KV, G, nb, Q_BLOCK, HD), 3, 0)

    def one_block(qi):
        s = jnp.einsum('bkgqd,bksd->bkgqs', qi, k).astype(F32) * (HD ** -0.5)
        p = softmax_with_sink(s, sink)
        return jnp.einsum('bkgqs,bksd->bkgqd', p.astype(v.dtype), v)

    o = lax.map(one_block, qb)
    return jnp.moveaxis(o, 0, 3).reshape(B, KV, G, T, HD)


def attend_band(q, k, v, kc, vc, sink):
    B, KV, G, T, HD = q.shape
    nb = T // Q_BLOCK
    pad = ((0, 0), (0, 0), (Q_BLOCK, Q_BLOCK), (0, 0))
    kp = jnp.pad(k, pad).reshape(B, KV, nb + 2, Q_BLOCK, HD)
    vp = jnp.pad(v, pad).reshape(B, KV, nb + 2, Q_BLOCK, HD)
    kband = jnp.concatenate([kp[:, :, :-2], kp[:, :, 1:-1], kp[:, :, 2:]], axis=3)
    vband = jnp.concatenate([vp[:, :, :-2], vp[:, :, 1:-1], vp[:, :, 2:]], axis=3)
    qb = q.reshape(B, KV, G, nb, Q_BLOCK, HD)
    scale = HD ** -0.5
    s_band = jnp.einsum('bkgnqd,bknsd->bkgnqs', qb, kband).astype(F32) * scale
    s_ctx = jnp.einsum('bkgnqd,bksd->bkgnqs', qb, kc).astype(F32) * scale
    blk = jnp.arange(nb)[:, None, None] * Q_BLOCK
    qpos = blk + jnp.arange(Q_BLOCK)[None, :, None]
    kpos = blk - Q_BLOCK + jnp.arange(3 * Q_BLOCK)[None, None, :]
    valid = (jnp.abs(kpos - qpos) <= WINDOW) & (kpos >= 0) & (kpos < T)
    s_band = jnp.where(valid, s_band, -jnp.inf)
    p = softmax_with_sink(jnp.concatenate([s_band, s_ctx], axis=-1), sink).astype(v.dtype)
    nband = 3 * Q_BLOCK
    o = (jnp.einsum('bkgnqs,bknsd->bkgnqd', p[..., :nband], vband)
         + jnp.einsum('bkgnqs,bksd->bkgnqd', p[..., nband:], vc))
    return o.reshape(B, KV, G, T, HD)


def mlstm_chunkwise(q, k, v, i_pre, f_log, C0, n0, m0):
    B, H, T, _ = q.shape
    L = ML_CHUNK
    N = T // L

    def chunks(a):
        return jnp.moveaxis(a.reshape((B, H, N, L) + a.shape[3:]), 2, 0)

    lower = jnp.tril(jnp.ones((L, L), dtype=bool))

    def step(carry, xs):
        C, n, m = carry
        qb, kb, vb, ib, fb = xs
        b = jnp.cumsum(fb, axis=-1)
        a = b + m[..., None]
        D = jnp.where(lower, b[..., :, None] - b[..., None, :] + ib[..., None, :], -jnp.inf)
        mt = jnp.maximum(a, jnp.max(D, axis=-1))
        s = jnp.einsum('bhtd,bhsd->bhts', qb, kb) * jnp.exp(D - mt[..., None])
        w_inter = jnp.exp(a - mt)
        num = (jnp.einsum('bhts,bhsv->bhtv', s, vb)
               + w_inter[..., None] * jnp.einsum('bhvd,bhtd->bhtv', C, qb))
        den = jnp.sum(s, axis=-1) + w_inter * jnp.einsum('bhd,bhtd->bht', n, qb)
        h = num / jnp.maximum(jnp.abs(den), jnp.exp(-mt))[..., None]
        b_last = b[..., -1]
        g = b_last[..., None] - b + ib
        m_new = jnp.maximum(b_last + m, jnp.max(g, axis=-1))
        ws = jnp.exp(g - m_new[..., None])
        wc = jnp.exp(b_last + m - m_new)
        C_new = wc[..., None, None] * C + jnp.einsum('bhs,bhsv,bhsd->bhvd', ws, vb, kb)
        n_new = wc[..., None] * n + jnp.einsum('bhs,bhsd->bhd', ws, kb)
        return (C_new, n_new, m_new), h

    init = (C0.astype(F32), n0.astype(F32), m0.astype(F32))
    (C, n, m), hs = lax.scan(step, init, tuple(chunks(a) for a in (q, k, v, i_pre, f_log)))
    h = jnp.moveaxis(hs, 0, 2).reshape(B, H, T, v.shape[-1])
    return h, C, n, m


def mlstm_bidir(q, k, v, o, gates, gate_bias, C0, n0, m0):
    B, T, _ = q.shape

    def heads(a, d):
        return a.astype(F32).reshape(B, T, ML_HEADS, d).transpose(0, 2, 1, 3)

    qh = heads(q, ML_DK)
    kh = heads(k, ML_DK) * (ML_DK ** -0.5)
    vh = heads(v, ML_DV)
    g = (gates.astype(F32).reshape(B, T, 4, ML_HEADS) + gate_bias.astype(F32)).transpose(0, 2, 3, 1)
    i_f, f_f = g[:, 0], jax.nn.log_sigmoid(g[:, 1])
    i_b, f_b = g[:, 2], jax.nn.log_sigmoid(g[:, 3])
    h_f, Cf, nf, mf = mlstm_chunkwise(qh, kh, vh, i_f, f_f, C0[:, 0], n0[:, 0], m0[:, 0])

    def rev(a):
        return jnp.flip(a, axis=2)

    h_b, Cb, nb, mb = mlstm_chunkwise(rev(qh), rev(kh), rev(vh), rev(i_b), rev(f_b),
                                      C0[:, 1], n0[:, 1], m0[:, 1])
    h = (h_f + rev(h_b)).transpose(0, 2, 1, 3).reshape(B, T, ML_HEADS * ML_DV)
    y = jax.nn.sigmoid(o.astype(F32)) * h
    return (y.astype(q.dtype), jnp.stack([Cf, Cb], axis=1), jnp.stack([nf, nb], axis=1),
            jnp.stack([mf, mb], axis=1))


def modulate_project(x, mod, w_in):
    shift, scale, gate = jnp.split(mod, 3, axis=-1)
    u = (layer_norm(x) * (1.0 + scale) + shift).astype(x.dtype)
    return split_columns(u @ w_in), gate


def merge_residual(x, ys, zs, g_merge, gate, w_branch, w_out, ln_g, ln_b, alpha):
    gates = jnp.split(jax.nn.sigmoid(g_merge), N_BRANCH, axis=-1)
    merged = sum(gb * ((yb * jax.nn.silu(zb)) @ w_branch[i])
                 for i, (yb, zb, gb) in enumerate(zip(ys, zs, gates)))
    h = alpha * x + gate * (merged @ w_out)
    return (layer_norm(h) * ln_g + ln_b).astype(x.dtype)


def context_layer(x, mod, lp, alpha):
    w_in, qk_gain, sink, gate_bias, w_branch, w_out, ln_g, ln_b = lp
    parts, gate = modulate_project(x, mod, w_in)
    qa, ka, va, za, qm, km, vm, om, gm, zm, qw, kw, vw, zw, g_merge = parts
    B = x.shape[0]
    qa = rms_norm(q_heads(qa, ATT_KV, ATT_GROUP), qk_gain[0])
    ka = rms_norm(kv_heads(ka, ATT_KV), qk_gain[1])
    va = kv_heads(va, ATT_KV)
    ya = merge_heads(attend_blocks(qa, ka, va, None))
    qw = q_heads(qw, WIN_KV, WIN_GROUP)
    kw = kv_heads(kw, WIN_KV)
    vw = kv_heads(vw, WIN_KV)
    yw = merge_heads(attend_blocks(qw, kw, vw, sink))
    C0 = jnp.zeros((B, 2, ML_HEADS, ML_DV, ML_DK), F32)
    n0 = jnp.zeros((B, 2, ML_HEADS, ML_DK), F32)
    m0 = jnp.zeros((B, 2, ML_HEADS), F32)
    ym, Cm, nm, mm = mlstm_bidir(qm, km, vm, om, gm, gate_bias, C0, n0, m0)
    x_new = merge_residual(x, (ya, ym, yw), (za, zm, zw), g_merge, gate, w_branch, w_out, ln_g, ln_b, alpha)
    return x_new, (ka, va, kw, vw, Cm, nm, mm)


def latent_layer(x, mod, ctx, lp, alpha):
    w_in, qk_gain, sink, gate_bias, w_branch, w_out, ln_g, ln_b = lp
    ck_a, cv_a, ck_w, cv_w, C0, n0, m0 = ctx
    parts, gate = modulate_project(x, mod, w_in)
    qa, ka, va, za, qm, km, vm, om, gm, zm, qw, kw, vw, zw, g_merge = parts
    cos, sin = axial_rope(x.shape[1], x.dtype)
    qa = apply_rope(rms_norm(q_heads(qa, ATT_KV, ATT_GROUP), qk_gain[0]), cos, sin)
    ka = apply_rope(rms_norm(kv_heads(ka, ATT_KV), qk_gain[1]), cos, sin)
    va = kv_heads(va, ATT_KV)
    k_all = jnp.concatenate([ka, ck_a.astype(ka.dtype)], axis=2)
    v_all = jnp.concatenate([va, cv_a.astype(va.dtype)], axis=2)
    ya = merge_heads(attend_blocks(qa, k_all, v_all, None))
    qw = apply_rope(q_heads(qw, WIN_KV, WIN_GROUP), cos, sin)
    kw = apply_rope(kv_heads(kw, WIN_KV), cos, sin)
    vw = kv_heads(vw, WIN_KV)
    yw = merge_heads(attend_band(qw, kw, vw, ck_w.astype(kw.dtype), cv_w.astype(vw.dtype), sink))
    ym, _, _, _ = mlstm_bidir(qm, km, vm, om, gm, gate_bias, C0, n0, m0)
    return merge_residual(x, (ya, ym, yw), (za, zm, zw), g_merge, gate, w_branch, w_out, ln_g, ln_b, alpha)


def setup_inputs(seed: int = 0) -> dict:
    key = jax.random.key(seed)
    ks = jax.random.split(key, 21)
    D = D_MODEL
    beta = (8.0 * DEPTH) ** -0.25

    def nrm(k, shape, scale=1.0):
        return jax.random.normal(k, shape, F32) * scale

    gate_bias_base = jnp.array([0.0, 3.0, 0.0, 3.0], F32)[None, :, None]
    return {
        'x_prompt': nrm(ks[0], (BATCH, SEQ, D)),
        'x_sample': nrm(ks[1], (DEC_BATCH, DEC_SEQ, D)),
        'cache_attn_k': nrm(ks[2], (DEC_BATCH, DEPTH, ATT_KV, PAST_LEN, HEAD_DIM)),
        'cache_attn_v': nrm(ks[3], (DEC_BATCH, DEPTH, ATT_KV, PAST_LEN, HEAD_DIM)),
        'cache_win_k': nrm(ks[4], (DEC_BATCH, DEPTH, WIN_KV, PAST_LEN, HEAD_DIM)),
        'cache_win_v': nrm(ks[5], (DEC_BATCH, DEPTH, WIN_KV, PAST_LEN, HEAD_DIM)),
        'state_mlstm_C': nrm(ks[6], (DEC_BATCH, DEPTH, 2, ML_HEADS, ML_DV, ML_DK), 0.1),
        'state_mlstm_n': nrm(ks[7], (DEC_BATCH, DEPTH, 2, ML_HEADS, ML_DK), 0.3),
        'state_mlstm_m': nrm(ks[8], (DEC_BATCH, DEPTH, 2, ML_HEADS)),
        'c': nrm(ks[9], (DEC_BATCH, D)),
        'c_ctx': nrm(ks[10], (D,)),
        'w_mod': nrm(ks[11], (DEPTH, D, 3 * D), 0.5 * D ** -0.5),
        'b_mod': nrm(ks[12], (DEPTH, 3 * D), 0.02),
        'w_in': nrm(ks[13], (DEPTH, D, N_IN), D ** -0.5),
        'qk_gain': 1.0 + nrm(ks[14], (DEPTH, 2, HEAD_DIM), 0.02),
        'sink_logit': nrm(ks[15], (DEPTH, WIN_KV, WIN_GROUP), 0.5),
        'mlstm_gate_bias': gate_bias_base + nrm(ks[16], (DEPTH, 4, ML_HEADS), 0.1),
        'w_branch': nrm(ks[17], (DEPTH, N_BRANCH, BRANCH_W, D), BRANCH_W ** -0.5),
        'w_out': nrm(ks[18], (DEPTH, D, D), beta * D ** -0.5),
        'ln_gain': 1.0 + nrm(ks[19], (DEPTH, D), 0.02),
        'ln_bias': nrm(ks[20], (DEPTH, D), 0.02),
    }


def reference(x_prompt, x_sample, cache_attn_k, cache_attn_v, cache_win_k, cache_win_v,
              state_mlstm_C, state_mlstm_n, state_mlstm_m, c, c_ctx, w_mod, b_mod, w_in,
              qk_gain, sink_logit, mlstm_gate_bias, w_branch, w_out, ln_gain, ln_bias):
    alpha = (2.0 * DEPTH) ** 0.25
    xp = x_prompt
    xs = x_sample
    ak, av, wk, wv, sC, sn, sm = [], [], [], [], [], [], []
    for l in range(DEPTH):
        lp = (w_in[l], qk_gain[l], sink_logit[l], mlstm_gate_bias[l], w_branch[l], w_out[l],
              ln_gain[l], ln_bias[l])
        mod_ctx = jax.nn.silu(c_ctx) @ w_mod[l] + b_mod[l]
        xp, ctx_t = context_layer(xp, mod_ctx, lp, alpha)
        ak.append(ctx_t[0]); av.append(ctx_t[1]); wk.append(ctx_t[2]); wv.append(ctx_t[3])
        sC.append(ctx_t[4]); sn.append(ctx_t[5]); sm.append(ctx_t[6])
        mod_lat = (jax.nn.silu(c) @ w_mod[l] + b_mod[l])[:, None, :]
        ctx_cache = (cache_attn_k[:, l], cache_attn_v[:, l], cache_win_k[:, l], cache_win_v[:, l],
                     state_mlstm_C[:, l], state_mlstm_n[:, l], state_mlstm_m[:, l])
        xs = latent_layer(xs, mod_lat, ctx_cache, lp, alpha)
    new_attn_k = jnp.stack(ak, axis=1)
    new_attn_v = jnp.stack(av, axis=1)
    new_win_k = jnp.stack(wk, axis=1)
    new_win_v = jnp.stack(wv, axis=1)
    new_mlstm_C = jnp.stack(sC, axis=1)
    new_mlstm_n = jnp.stack(sn, axis=1)
    new_mlstm_m = jnp.stack(sm, axis=1)
    return (xp, xs, new_attn_k, new_attn_v, new_win_k, new_win_v, new_mlstm_C, new_mlstm_n, new_mlstm_m)
```

```python
import functools

import jax
import jax.numpy as jnp
import numpy as np
from jax import lax
from jax.experimental import pallas as pl
from jax.experimental.pallas import tpu as pltpu

F32 = jnp.float32
BF16 = jnp.bfloat16

D_MODEL = 1024
DEPTH = 2
PAST_LEN = 256
GRID_W = 64
HEAD_DIM = 64
N_HEADS = 8
N_KV = 2
N_GROUP = N_HEADS // N_KV
WINDOW = 128
ML_HEADS = 4
ML_DK = 64
ML_DV = 128
CHUNK = 128
BRANCH_W = 512
ROPE_THETA = 10000.0
AXIS_DIM = HEAD_DIM // 2
LN_EPS = 1e-6
RMS_EPS = 1e-6
ALPHA = (2.0 * DEPTH) ** 0.25
LANES = 128
ROW_BLOCK = 256
VMEM_LIMIT_BYTES = 60 * 1024 * 1024

_SIZES = (512, 128, 128, 512, 256, 256, 512, 512, 16, 512, 512, 128, 128, 512, 3072)
_OFF = np.concatenate([[0], np.cumsum(_SIZES)])
(_QA, _KA, _VA, _ZA, _QM, _KM, _VM, _OM, _GM, _ZM, _QW, _KW, _VW, _ZW, _GMERGE) = [
    (int(_OFF[i]), int(_OFF[i + 1])) for i in range(len(_SIZES))]

C_QA, C_QW, C_KA, C_KW, C_VA, C_VW = 0, 512, 1024, 1152, 1280, 1408
C_QM, C_KM, C_VM, C_OM, C_GI, C_GF = 1536, 1792, 2048, 2560, 3072, 3200
W1_COLS = 3328
C_ZA, C_ZM, C_ZW, C_G = 0, 512, 1024, 1536
W2_COLS = 4608


def _dot(a, b):
    return jnp.dot(a, b, preferred_element_type=F32)


def _dot_tb(a, b):
    return lax.dot_general(a, b, (((1,), (1,)), ((), ())), preferred_element_type=F32)


def _split(a):
    hi = a.astype(BF16)
    lo = (a - hi.astype(F32)).astype(BF16)
    return hi, lo


def _layer_norm(x):
    mu = jnp.mean(x, axis=-1, keepdims=True)
    xc = x - mu
    var = jnp.mean(xc * xc, axis=-1, keepdims=True)
    return xc * lax.rsqrt(var + LN_EPS)


def _log_sigmoid(x):
    return jnp.minimum(x, 0.0) - jnp.log(1.0 + jnp.exp(-jnp.abs(x)))


def _sigmoid(x):
    return jax.nn.sigmoid(x)


def _mod_kernel(c_ref, w_ref, b_ref, o_ref):
    c = c_ref[...]
    s = c * _sigmoid(c)
    s_hi, s_lo = _split(s)
    w = w_ref[...]
    w_hi, w_lo = _split(w)
    o_ref[...] = _dot(s_hi, w_hi) + _dot(s_lo, w_hi) + _dot(s_hi, w_lo) + b_ref[...]


def _modulation(cond, w_mod, b_mod):
    rows = cond.shape[0]
    tn = 1024
    return pl.pallas_call(
        _mod_kernel,
        grid=(DEPTH, 3 * D_MODEL // tn),
        in_specs=[
            pl.BlockSpec((rows, D_MODEL), lambda l, j: (0, 0)),
            pl.BlockSpec((None, D_MODEL, tn), lambda l, j: (l, 0, j)),
            pl.BlockSpec((None, 1, tn), lambda l, j: (l, 0, j)),
        ],
        out_specs=pl.BlockSpec((None, rows, tn), lambda l, j: (l, 0, j)),
        out_shape=jax.ShapeDtypeStruct((DEPTH, rows, 3 * D_MODEL), F32),
        compiler_params=pltpu.CompilerParams(dimension_semantics=("arbitrary", "arbitrary")),
        name="adaln_modulation",
    )(cond, w_mod, b_mod.reshape(DEPTH, 1, 3 * D_MODEL))


def _rope(x, cos, sa, sb):
    return x * cos + pltpu.roll(x, LANES - AXIS_DIM // 2, 1) * sa + pltpu.roll(x, AXIS_DIM // 2, 1) * sb


def _group_rms(x, bd, gain):
    hi, lo = _split(x * x)
    ms = (_dot(hi, bd) + _dot(lo, bd)) * (1.0 / HEAD_DIM)
    return x * lax.rsqrt(ms + RMS_EPS) * gain


def _softmax_attend(qs, parts, sink_col):
    scores = []
    for k, _, valid in parts:
        s = _dot_tb(qs, k)
        if valid is not None:
            s = jnp.where(valid, s, -jnp.inf)
        scores.append(s)
    mx = functools.reduce(jnp.maximum, [jnp.max(s, axis=-1, keepdims=True) for s in scores])
    if sink_col is not None:
        mx = jnp.maximum(mx, sink_col)
    den = None
    acc = None
    for s, (_, v, _) in zip(scores, parts):
        e = jnp.exp(s - mx)
        d = jnp.sum(e, axis=-1, keepdims=True)
        o = _dot(e.astype(BF16), v)
        den = d if den is None else den + d
        acc = o if acc is None else acc + o
    if sink_col is not None:
        den = den + jnp.exp(sink_col - mx)
    return acc / den


def _layer_kernel(lat, T, layer, *refs):
    refs = list(refs)
    x_ref, mod_ref, w1_ref, w2_ref, wb_ref, wo_ref = refs[:6]
    g0_ref, g1_ref, gbi_ref, gbf_ref, lng_ref, lnb_ref = refs[6:12]
    bd_ref, tril_ref, triu_ref, sink_ref = refs[12:16]
    pos = 16
    if lat:
        cos_ref, sa_ref, sb_ref = refs[pos:pos + 3]
        cka_ref, cva_ref, ckw_ref, cvw_ref, c0_ref, n0_ref, m0_ref = refs[pos + 3:pos + 10]
        pos += 10
        y_ref = refs[pos]
        pos += 1
    else:
        y_ref, ka_o, va_o, kw_o, vw_o, c_o, n_o, m_o = refs[pos:pos + 8]
        pos += 8
    (u_s, qa_s, qw_s, ka_s, va_s, kw_s, vw_s, qm_s, km_s, vm_s, om_s, gi_s, gf_s,
     ya_s, yw_s, hm_s, c_s, n_s, m_s) = refs[pos:pos + 19]
    pos += 19
    if lat:
        kwc_s, vwc_s = refs[pos:pos + 2]

    n_rows = T // ROW_BLOCK
    n_chunks = T // CHUNK
    shift = mod_ref[0:1, :]
    scale = mod_ref[1:2, :]
    gate = mod_ref[2:3, :]

    def phase1(r0):
        R = ROW_BLOCK
        rows = pl.ds(r0, R)
        x = x_ref[rows, :]
        u = (_layer_norm(x) * (1.0 + scale) + shift).astype(BF16)
        u_s[rows, :] = u

        def proj(c0, width):
            return _dot(u, w1_ref[:, c0:c0 + width])

        if lat:
            cos = cos_ref[rows, :]
            sa = sa_ref[rows, :]
            sb = sb_ref[rows, :]

        def rope_wide(v):
            if not lat:
                return v
            slabs = [_rope(v[:, j:j + LANES], cos, sa, sb) for j in range(0, v.shape[1], LANES)]
            return slabs[0] if len(slabs) == 1 else jnp.concatenate(slabs, axis=1)

        qscale = HEAD_DIM ** -0.5
        qa = rope_wide(_group_rms(proj(C_QA, 512), bd_ref[...], g0_ref[...])) * qscale
        for h in range(N_HEADS):
            qa_s[h, rows, :] = qa[:, h * HEAD_DIM:(h + 1) * HEAD_DIM].astype(BF16)
        ka_n = _group_rms(proj(C_KA, 128), bd_ref[0:LANES, 0:LANES], g1_ref[...])
        ka = rope_wide(ka_n)
        va = proj(C_VA, 128)
        qw = rope_wide(proj(C_QW, 512)) * qscale
        for h in range(N_HEADS):
            qw_s[h, rows, :] = qw[:, h * HEAD_DIM:(h + 1) * HEAD_DIM].astype(BF16)
        kw_raw = proj(C_KW, 128)
        kw = rope_wide(kw_raw)
        vw = proj(C_VW, 128)
        for kv in range(N_KV):
            sl = slice(kv * HEAD_DIM, (kv + 1) * HEAD_DIM)
            ka_s[kv, rows, :] = ka[:, sl].astype(BF16)
            va_s[kv, rows, :] = va[:, sl].astype(BF16)
            kw_s[kv, rows, :] = kw[:, sl].astype(BF16)
            vw_s[kv, rows, :] = vw[:, sl].astype(BF16)
            if not lat:
                ka_o[kv, rows, :] = ka_n[:, sl]
                va_o[kv, rows, :] = va[:, sl]
                kw_o[kv, rows, :] = kw_raw[:, sl]
                vw_o[kv, rows, :] = vw[:, sl]
        qm = proj(C_QM, 256)
        km = proj(C_KM, 256) * (ML_DK ** -0.5)
        for h in range(ML_HEADS):
            sl = slice(h * ML_DK, (h + 1) * ML_DK)
            qm_s[h, rows, :] = qm[:, sl].astype(BF16)
            km_s[h, rows, :] = km[:, sl].astype(BF16)
        vm_s[rows, :] = proj(C_VM, 512).astype(BF16)
        om_s[rows, :] = proj(C_OM, 512)
        gi_s[rows, :] = proj(C_GI, LANES) + gbi_ref[...]
        gf_s[rows, :] = proj(C_GF, LANES) + gbf_ref[...]

    if n_rows == 1:
        phase1(0)
    else:
        def body1(i, carry):
            phase1(pl.multiple_of(i * ROW_BLOCK, ROW_BLOCK))
            return carry
        lax.fori_loop(0, n_rows, body1, 0)

    if lat:
        for kv in range(N_KV):
            ka_s[kv, T:T + PAST_LEN, :] = cka_ref[kv].astype(BF16)
            va_s[kv, T:T + PAST_LEN, :] = cva_ref[kv].astype(BF16)
            kwc_s[kv] = ckw_ref[kv].astype(BF16)
            vwc_s[kv] = cvw_ref[kv].astype(BF16)

    def sink_column(kv, tq):
        vals = [jnp.full((tq, 1), sink_ref[layer * N_HEADS + kv * N_GROUP + g], F32)
                for g in range(N_GROUP)]
        return jnp.concatenate(vals, axis=0)

    def store_heads(dst, q0, tq, kv, o):
        for g in range(N_GROUP):
            h = kv * N_GROUP + g
            dst[pl.ds(q0, tq), h * HEAD_DIM:(h + 1) * HEAD_DIM] = o[g * tq:(g + 1) * tq, :]

    def attn_full(q0, tq):
        for kv in range(N_KV):
            qs = qa_s[kv * N_GROUP:(kv + 1) * N_GROUP, pl.ds(q0, tq), :].reshape(N_GROUP * tq, HEAD_DIM)
            o = _softmax_attend(qs, [(ka_s[kv], va_s[kv], None)], None)
            store_heads(ya_s, q0, tq, kv, o)

    def attn_ctx_sink(q0, tq):
        for kv in range(N_KV):
            qs = qw_s[kv * N_GROUP:(kv + 1) * N_GROUP, pl.ds(q0, tq), :].reshape(N_GROUP * tq, HEAD_DIM)
            o = _softmax_attend(qs, [(kw_s[kv], vw_s[kv], None)], sink_column(kv, tq))
            store_heads(yw_s, q0, tq, kv, o)

    def attn_band(n):
        tq = CHUNK
        band = 3 * CHUNK
        q0 = pl.multiple_of(n * tq, tq)
        w0 = pl.multiple_of(jnp.clip((n - 1) * tq, 0, T - band), tq)
        rows = lax.broadcasted_iota(jnp.int32, (N_GROUP * tq, band), 0)
        cols = lax.broadcasted_iota(jnp.int32, (N_GROUP * tq, band), 1)
        qpos = q0 + (rows & (tq - 1))
        kpos = w0 + cols
        valid = jnp.abs(kpos - qpos) <= WINDOW
        for kv in range(N_KV):
            qs = qw_s[kv * N_GROUP:(kv + 1) * N_GROUP, pl.ds(q0, tq), :].reshape(N_GROUP * tq, HEAD_DIM)
            parts = [(kw_s[kv, pl.ds(w0, band), :], vw_s[kv, pl.ds(w0, band), :], valid),
                     (kwc_s[kv], vwc_s[kv], None)]
            o = _softmax_attend(qs, parts, sink_column(kv, tq))
            store_heads(yw_s, q0, tq, kv, o)

    if lat:
        def body2(n, carry):
            attn_full(pl.multiple_of(n * CHUNK, CHUNK), CHUNK)
            attn_band(n)
            return carry
        lax.fori_loop(0, n_chunks, body2, 0)
    else:
        attn_full(0, T)
        attn_ctx_sink(0, T)

    hm_s[...] = jnp.zeros_like(hm_s)
    for d in range(2):
        for h in range(ML_HEADS):
            idx = d * ML_HEADS + h
            if lat:
                c_s[idx] = c0_ref[d, h]
                n_s[idx:idx + 1, :] = n0_ref[d, h:h + 1, :]
                m_s[idx:idx + 1, :] = jnp.broadcast_to(m0_ref[d:d + 1, h:h + 1], (1, LANES))
            else:
                c_s[idx] = jnp.zeros((ML_DV, ML_DK), F32)
                n_s[idx:idx + 1, :] = jnp.zeros((1, ML_DK), F32)
                m_s[idx:idx + 1, :] = jnp.zeros((1, LANES), F32)

    L = CHUNK
    t_idx = lax.broadcasted_iota(jnp.int32, (L, L), 0)
    s_idx = lax.broadcasted_iota(jnp.int32, (L, L), 1)
    lane = lax.broadcasted_iota(jnp.int32, (L, LANES), 1)

    def mlstm_chunk(c):
        for d in range(2):
            cc = c if d == 0 else n_chunks - 1 - c
            r0 = pl.multiple_of(cc * L, L)
            rows = pl.ds(r0, L)
            gi = gi_s[rows, :]
            f_hi, f_lo = _split(_log_sigmoid(gf_s[rows, :]))
            tri = tril_ref[...] if d == 0 else triu_ref[...]
            cum = _dot(tri, f_hi) + _dot(tri, f_lo)
            r_t = jnp.transpose(gi - cum)
            causal = (s_idx <= t_idx) if d == 0 else (s_idx >= t_idx)
            for h in range(ML_HEADS):
                idx = d * ML_HEADS + h
                b_col = jnp.sum(jnp.where(lane == idx, cum, 0.0), axis=-1, keepdims=True)
                i_col = jnp.sum(jnp.where(lane == idx, gi, 0.0), axis=-1, keepdims=True)
                r_row = r_t[idx:idx + 1, :]
                m_prev = m_s[idx:idx + 1, 0:1]
                b_last = b_col[L - 1:L, :] if d == 0 else b_col[0:1, :]
                a_col = b_col + m_prev
                dm = jnp.where(causal, b_col + r_row, -jnp.inf)
                mt = jnp.maximum(a_col, jnp.max(dm, axis=-1, keepdims=True))
                q = qm_s[h, rows, :]
                k = km_s[h, rows, :]
                v = vm_s[rows, h * ML_DV:(h + 1) * ML_DV]
                s = _dot_tb(q, k) * jnp.exp(dm - mt)
                w_inter = jnp.exp(a_col - mt)
                c_prev = c_s[idx]
                n_prev = n_s[idx:idx + 1, :]
                num = _dot(s.astype(BF16), v) + w_inter * _dot_tb(q, c_prev.astype(BF16))
                den = (jnp.sum(s, axis=-1, keepdims=True)
                       + w_inter * jnp.sum(q.astype(F32) * n_prev, axis=-1, keepdims=True))
                hh = num / jnp.maximum(jnp.abs(den), jnp.exp(-mt))
                hm_s[rows, h * ML_DV:(h + 1) * ML_DV] += hh
                g_col = b_last - b_col + i_col
                m_new = jnp.maximum(b_last + m_prev, jnp.max(g_col, axis=0, keepdims=True))
                ws = jnp.exp(g_col - m_new)
                wc = jnp.exp(b_last + m_prev - m_new)
                wv_t = jnp.transpose(ws * v.astype(F32)).astype(BF16)
                c_s[idx] = wc * c_prev + _dot(wv_t, k)
                n_s[idx:idx + 1, :] = wc * n_prev + jnp.sum(ws * k.astype(F32), axis=0, keepdims=True)
                m_s[idx:idx + 1, :] = jnp.broadcast_to(m_new, (1, LANES))

    def body3(c, carry):
        mlstm_chunk(c)
        return carry
    lax.fori_loop(0, n_chunks, body3, 0)

    if not lat:
        for d in range(2):
            for h in range(ML_HEADS):
                idx = d * ML_HEADS + h
                c_o[d, h] = c_s[idx]
                n_o[d, h:h + 1, :] = n_s[idx:idx + 1, :]
                m_o[d:d + 1, h:h + 1] = m_s[idx:idx + 1, 0:1]

    def phase3(r0):
        rows = pl.ds(r0, ROW_BLOCK)
        u = u_s[rows, :]

        def proj(c0, width):
            return _dot(u, w2_ref[:, c0:c0 + width])

        ys = (ya_s[rows, :],
              _sigmoid(om_s[rows, :]) * hm_s[rows, :],
              yw_s[rows, :])
        merged = None
        for b, (yb, zc) in enumerate(zip(ys, (C_ZA, C_ZM, C_ZW))):
            z = proj(zc, BRANCH_W)
            t = (yb * (z * _sigmoid(z))).astype(BF16)
            pb = _sigmoid(proj(C_G + b * D_MODEL, D_MODEL)) * _dot(t, wb_ref[b])
            merged = pb if merged is None else merged + pb
        o = _dot(merged.astype(BF16), wo_ref[...])
        hres = ALPHA * x_ref[rows, :] + gate * o
        y_ref[rows, :] = _layer_norm(hres) * lng_ref[...] + lnb_ref[...]

    if n_rows == 1:
        phase3(0)
    else:
        def body4(i, carry):
            phase3(pl.multiple_of(i * ROW_BLOCK, ROW_BLOCK))
            return carry
        lax.fori_loop(0, n_rows, body4, 0)


def _const_spec(shape):
    nd = len(shape)
    return pl.BlockSpec(shape, lambda b: (0,) * nd, pipeline_mode=pl.Buffered(1))


def _layer_call(lat, layer, x, mod, weights, consts, extra):
    B, T, _ = x.shape
    w1, w2, wb, wo, g0, g1, gbi, gbf, lng, lnb = weights
    bd, tril, triu, sink = consts
    S = T + PAST_LEN if lat else T

    seq_spec = pl.BlockSpec((None, T, D_MODEL), lambda b: (b, 0, 0), pipeline_mode=pl.Buffered(1))
    if lat:
        mod_spec = pl.BlockSpec((None, None, 3, D_MODEL), lambda b: (layer, b + 1, 0, 0))
    else:
        mod_spec = pl.BlockSpec((None, None, 3, D_MODEL), lambda b: (layer, 0, 0, 0))
    in_specs = [seq_spec, mod_spec]
    in_specs += [_const_spec(a.shape) for a in (w1, w2, wb, wo, g0, g1, gbi, gbf, lng, lnb, bd, tril, triu)]
    in_specs.append(pl.BlockSpec(memory_space=pltpu.SMEM))
    args = [x, mod, w1, w2, wb, wo, g0, g1, gbi, gbf, lng, lnb, bd, tril, triu, sink]

    if lat:
        cos, sa, sb, cka, cva, ckw, cvw, sc, sn, sm = extra
        in_specs += [_const_spec(cos.shape)] * 3
        kv_spec = pl.BlockSpec((None, None, N_KV, PAST_LEN, HEAD_DIM), lambda b: (b, layer, 0, 0, 0))
        in_specs += [kv_spec] * 4
        in_specs += [
            pl.BlockSpec((None, None, 2, ML_HEADS, ML_DV, ML_DK), lambda b: (b, layer, 0, 0, 0, 0)),
            pl.BlockSpec((None, None, 2, ML_HEADS, ML_DK), lambda b: (b, layer, 0, 0, 0)),
            pl.BlockSpec((None, None, 2, ML_HEADS), lambda b: (b, layer, 0, 0)),
        ]
        args += [cos, sa, sb, cka, cva, ckw, cvw, sc, sn, sm]
        out_specs = seq_spec
        out_shape = jax.ShapeDtypeStruct((B, T, D_MODEL), F32)
    else:
        kv_out = pl.BlockSpec((None, N_KV, T, HEAD_DIM), lambda b: (b, 0, 0, 0))
        out_specs = [
            seq_spec, kv_out, kv_out, kv_out, kv_out,
            pl.BlockSpec((None, 2, ML_HEADS, ML_DV, ML_DK), lambda b: (b, 0, 0, 0, 0)),
            pl.BlockSpec((None, 2, ML_HEADS, ML_DK), lambda b: (b, 0, 0, 0)),
            pl.BlockSpec((None, 2, ML_HEADS), lambda b: (b, 0, 0)),
        ]
        kv_shape = jax.ShapeDtypeStruct((B, N_KV, T, HEAD_DIM), F32)
        out_shape = [
            jax.ShapeDtypeStruct((B, T, D_MODEL), F32), kv_shape, kv_shape, kv_shape, kv_shape,
            jax.ShapeDtypeStruct((B, 2, ML_HEADS, ML_DV, ML_DK), F32),
            jax.ShapeDtypeStruct((B, 2, ML_HEADS, ML_DK), F32),
            jax.ShapeDtypeStruct((B, 2, ML_HEADS), F32),
        ]

    scratch = [
        pltpu.VMEM((T, D_MODEL), BF16),
        pltpu.VMEM((N_HEADS, T, HEAD_DIM), BF16),
        pltpu.VMEM((N_HEADS, T, HEAD_DIM), BF16),
        pltpu.VMEM((N_KV, S, HEAD_DIM), BF16),
        pltpu.VMEM((N_KV, S, HEAD_DIM), BF16),
        pltpu.VMEM((N_KV, T, HEAD_DIM), BF16),
        pltpu.VMEM((N_KV, T, HEAD_DIM), BF16),
        pltpu.VMEM((ML_HEADS, T, ML_DK), BF16),
        pltpu.VMEM((ML_HEADS, T, ML_DK), BF16),
        pltpu.VMEM((T, ML_HEADS * ML_DV), BF16),
        pltpu.VMEM((T, ML_HEADS * ML_DV), F32),
        pltpu.VMEM((T, LANES), F32),
        pltpu.VMEM((T, LANES), F32),
        pltpu.VMEM((T, BRANCH_W), F32),
        pltpu.VMEM((T, BRANCH_W), F32),
        pltpu.VMEM((T, ML_HEADS * ML_DV), F32),
        pltpu.VMEM((2 * ML_HEADS, ML_DV, ML_DK), F32),
        pltpu.VMEM((2 * ML_HEADS, ML_DK), F32),
        pltpu.VMEM((2 * ML_HEADS, LANES), F32),
    ]
    if lat:
        scratch += [pltpu.VMEM((N_KV, PAST_LEN, HEAD_DIM), BF16)] * 2

    return pl.pallas_call(
        functools.partial(_layer_kernel, lat, T, layer),
        grid=(B,),
        in_specs=in_specs,
        out_specs=out_specs,
        out_shape=out_shape,
        scratch_shapes=scratch,
        compiler_params=pltpu.CompilerParams(
            dimension_semantics=("arbitrary",), vmem_limit_bytes=VMEM_LIMIT_BYTES),
        name=("latent_layer" if lat else "context_layer") + str(layer),
    )(*args)


def _rope_tables(T):
    rows = T // GRID_W
    row = jnp.repeat(jnp.arange(rows, dtype=F32), GRID_W)
    col = jnp.tile(jnp.arange(GRID_W, dtype=F32), rows)
    inv = ROPE_THETA ** (-jnp.arange(0, AXIS_DIM, 2, dtype=F32) / AXIS_DIM)
    ar = row[:, None] * inv
    ac = col[:, None] * inv
    ang = jnp.concatenate([ar, ar, ac, ac], axis=-1)
    ang = jnp.concatenate([ang, ang], axis=-1)
    first_half = (jnp.arange(LANES) % AXIS_DIM) < (AXIS_DIM // 2)
    cos = jnp.cos(ang)
    sin = jnp.sin(ang)
    sa = jnp.where(first_half[None, :], -sin, 0.0)
    sb = jnp.where(first_half[None, :], 0.0, sin)
    return cos, sa, sb


def kernel(x_prompt, x_sample, cache_attn_k, cache_attn_v, cache_win_k, cache_win_v, state_mlstm_C,
           state_mlstm_n, state_mlstm_m, c, c_ctx, w_mod, b_mod, w_in, qk_gain, sink_logit,
           mlstm_gate_bias, w_branch, w_out, ln_gain, ln_bias):
    dec_b = x_sample.shape[0]
    assert dec_b + 1 <= 8
    cond = jnp.concatenate([c_ctx[None, :], c, jnp.zeros((8 - 1 - dec_b, D_MODEL), F32)], axis=0)
    mod = _modulation(cond, w_mod, b_mod).reshape(DEPTH, 8, 3, D_MODEL)

    def cols(rng):
        return w_in[:, :, rng[0]:rng[1]]

    zpad = jnp.zeros((DEPTH, D_MODEL, LANES - 2 * ML_HEADS), F32)
    gm = cols(_GM).reshape(DEPTH, D_MODEL, 4, ML_HEADS)
    gi_w = jnp.concatenate([gm[:, :, 0], gm[:, :, 2], zpad], axis=-1)
    gf_w = jnp.concatenate([gm[:, :, 1], gm[:, :, 3], zpad], axis=-1)
    w1 = jnp.concatenate([cols(_QA), cols(_QW), cols(_KA), cols(_KW), cols(_VA), cols(_VW),
                          cols(_QM), cols(_KM), cols(_VM), cols(_OM), gi_w, gf_w], axis=-1).astype(BF16)
    w2 = jnp.concatenate([cols(_ZA), cols(_ZM), cols(_ZW), cols(_GMERGE)], axis=-1).astype(BF16)
    wb = w_branch.astype(BF16)
    wo = w_out.astype(BF16)
    g0 = jnp.tile(qk_gain[:, 0:1, :], (1, 1, N_HEADS))
    g1 = jnp.tile(qk_gain[:, 1:2, :], (1, 1, N_KV))
    bpad = jnp.zeros((DEPTH, 1, LANES - 2 * ML_HEADS), F32)
    gb = mlstm_gate_bias
    gbi = jnp.concatenate([gb[:, 0:1, :], gb[:, 2:3, :], bpad], axis=-1)
    gbf = jnp.concatenate([gb[:, 1:2, :], gb[:, 3:4, :], bpad], axis=-1)
    lng = ln_gain[:, None, :]
    lnb = ln_bias[:, None, :]

    grp = np.arange(N_HEADS * HEAD_DIM) // HEAD_DIM
    bd = jnp.asarray(grp[:, None] == grp[None, :], dtype=BF16)
    ti = np.arange(CHUNK)
    tril = jnp.asarray(ti[None, :] <= ti[:, None], dtype=BF16)
    triu = jnp.asarray(ti[None, :] >= ti[:, None], dtype=BF16)
    sink = sink_logit.reshape(-1)
    consts = (bd, tril, triu, sink)
    cos, sa, sb = _rope_tables(x_sample.shape[1])

    xp, xs = x_prompt, x_sample
    ctx_outs = []
    for l in range(DEPTH):
        weights = (w1[l], w2[l], wb[l], wo[l], g0[l], g1[l], gbi[l], gbf[l], lng[l], lnb[l])
        outs = _layer_call(False, l, xp, mod, weights, consts, None)
        xp = outs[0]
        ctx_outs.append(outs[1:])
        extra = (cos, sa, sb, cache_attn_k, cache_attn_v, cache_win_k, cache_win_v,
                 state_mlstm_C, state_mlstm_n, state_mlstm_m)
        xs = _layer_call(True, l, xs, mod, weights, consts, extra)
    stacked = [jnp.stack([ctx_outs[l][i] for l in range(DEPTH)], axis=1) for i in range(7)]
    return (xp, xs, *stacked)
```

```python
import functools

import jax
import jax.numpy as jnp
import numpy as np
from jax import lax
from jax.experimental import pallas as pl
from jax.experimental.pallas import tpu as pltpu

F32 = jnp.float32
BF16 = jnp.bfloat16

D_MODEL = 1024
DEPTH = 2
PAST_LEN = 256
GRID_W = 64
HEAD_DIM = 64
N_HEADS = 8
N_KV = 2
N_GROUP = N_HEADS // N_KV
WINDOW = 128
ML_HEADS = 4
ML_DK = 64
ML_DV = 128
CHUNK = 128
BRANCH_W = 512
ROPE_THETA = 10000.0
AXIS_DIM = HEAD_DIM // 2
LN_EPS = 1e-6
RMS_EPS = 1e-6
ALPHA = (2.0 * DEPTH) ** 0.25
LANES = 128
BF16_ROWS = 16
ROW_BLOCK = 256
VMEM_LIMIT_BYTES = 60 * 1024 * 1024
VT_ROWS = HEAD_DIM + BF16_ROWS
ST_ROWS = ML_DV + BF16_ROWS

_SIZES = (512, 128, 128, 512, 256, 256, 512, 512, 128, 512, 512, 128, 128, 512, 3072)
_OFF = [int(v) for v in np.concatenate([[0], np.cumsum(_SIZES)])]
(C_QA, C_KA, C_VA, C_ZA, C_QM, C_KM, C_VM, C_OM, C_GT, C_ZM, C_QW, C_KW, C_VW, C_ZW, C_GMERGE) = _OFF[:-1]
W_COLS = _OFF[-1]
GATE_COL = 2816
N_GATES = 4 * ML_HEADS


def _dot(a, b):
    return jnp.dot(a, b, preferred_element_type=F32)


def _dot_tb(a, b):
    return lax.dot_general(a, b, (((1,), (1,)), ((), ())), preferred_element_type=F32)


def _split(a):
    hi = a.astype(BF16)
    lo = (a - hi.astype(F32)).astype(BF16)
    return hi, lo


def _split3(a):
    hi = a.astype(BF16)
    r = a - hi.astype(F32)
    mid = r.astype(BF16)
    lo = (r - mid.astype(F32)).astype(BF16)
    return hi, mid, lo


def _layer_norm(x):
    mu = jnp.mean(x, axis=-1, keepdims=True)
    xc = x - mu
    var = jnp.mean(xc * xc, axis=-1, keepdims=True)
    return xc * lax.rsqrt(var + LN_EPS)


def _log_sigmoid(x):
    return jnp.minimum(x, 0.0) - jnp.log(1.0 + jnp.exp(-jnp.abs(x)))


def _sigmoid(x):
    return jax.nn.sigmoid(x)


def _mod_kernel(c_ref, w_ref, b_ref, o_ref):
    c = c_ref[...]
    s = c * _sigmoid(c)
    s_hi, s_lo = _split(s)
    w = w_ref[...]
    w_hi, w_lo = _split(w)
    o_ref[...] = _dot(s_hi, w_hi) + _dot(s_lo, w_hi) + _dot(s_hi, w_lo) + b_ref[...]


def _modulation(cond, w_mod, b_mod):
    rows = cond.shape[0]
    tn = 1024
    return pl.pallas_call(
        _mod_kernel,
        grid=(DEPTH, 3 * D_MODEL // tn),
        in_specs=[
            pl.BlockSpec((rows, D_MODEL), lambda l, j: (0, 0)),
            pl.BlockSpec((None, D_MODEL, tn), lambda l, j: (l, 0, j)),
            pl.BlockSpec((None, 1, tn), lambda l, j: (l, 0, j)),
        ],
        out_specs=pl.BlockSpec((None, rows, tn), lambda l, j: (l, 0, j)),
        out_shape=jax.ShapeDtypeStruct((DEPTH, rows, 3 * D_MODEL), F32),
        compiler_params=pltpu.CompilerParams(dimension_semantics=("arbitrary", "arbitrary")),
        name="adaln_modulation",
    )(cond, w_mod, b_mod.reshape(DEPTH, 1, 3 * D_MODEL))


def _rope(x, cos, sa, sb):
    return x * cos + pltpu.roll(x, LANES - AXIS_DIM // 2, 1) * sa + pltpu.roll(x, AXIS_DIM // 2, 1) * sb


def _group_rms(x, bd, gain):
    hi, lo = _split(x * x)
    ms = (_dot(hi, bd) + _dot(lo, bd)) * (1.0 / HEAD_DIM)
    return x * lax.rsqrt(ms + RMS_EPS) * gain


def _ones_row_tile():
    r = lax.broadcasted_iota(jnp.int32, (BF16_ROWS, LANES), 0)
    return jnp.where(r == 0, 1.0, 0.0).astype(BF16)


def _attend_t(qs, parts, sink_row):
    scores = []
    for k, _, valid in parts:
        s = _dot_tb(k, qs)
        if valid is not None:
            s = jnp.where(valid, s, -jnp.inf)
        scores.append(s)
    mx = functools.reduce(jnp.maximum, [jnp.max(s, axis=0, keepdims=True) for s in scores])
    if sink_row is not None:
        mx = jnp.maximum(mx, sink_row)
    tot = None
    for s, (_, v_t, _) in zip(scores, parts):
        o = _dot(v_t, jnp.exp(s - mx).astype(BF16))
        tot = o if tot is None else tot + o
    den = tot[HEAD_DIM:HEAD_DIM + 1, :]
    if sink_row is not None:
        den = den + jnp.exp(sink_row - mx)
    return tot[0:HEAD_DIM, :] / den


def _layer_kernel(lat, T, layer, *refs):
    refs = list(refs)
    x_ref, mod_ref, w_ref, wb_ref, wo_ref = refs[:5]
    g0_ref, g1_ref, gb_ref, lng_ref, lnb_ref = refs[5:10]
    bd_ref, tril_ref, triu_ref, sel_ref, eye_ref, sink_ref = refs[10:16]
    pos = 16
    if lat:
        cos_ref, sa_ref, sb_ref = refs[pos:pos + 3]
        cka_ref, cva_ref, ckw_ref, cvw_ref, c0_ref, n0_ref, m0_ref = refs[pos + 3:pos + 10]
        pos += 10
        y_ref = refs[pos]
        pos += 1
    else:
        y_ref, ka_o, va_o, kw_o, vw_o, c_o, n_o, m_o = refs[pos:pos + 8]
        pos += 8
    (u_s, qa_s, qw_s, ka_s, vat_s, kw_s, vwt_s, qm_s, km_s, vmt_s, om_s, g_s,
     ya_s, yw_s, hmt_s, c_s, m_s) = refs[pos:pos + 17]
    pos += 17
    if lat:
        kwc_s, vwct_s = refs[pos:pos + 2]

    n_rows = T // ROW_BLOCK
    n_chunks = T // CHUNK
    chunks_per_block = ROW_BLOCK // CHUNK
    shift = mod_ref[0:1, :]
    scale = mod_ref[1:2, :]
    gate = mod_ref[2:3, :]
    ones_tile = _ones_row_tile()

    def phase1(i):
        R = ROW_BLOCK
        r0 = i * R if isinstance(i, int) else pl.multiple_of(i * R, R)
        rows = pl.ds(r0, R)
        x = x_ref[rows, :]
        u = (_layer_norm(x) * (1.0 + scale) + shift).astype(BF16)
        u_s[rows, :] = u

        def proj(c0, width):
            return _dot(u, w_ref[:, c0:c0 + width])

        if lat:
            cos = cos_ref[rows, :]
            sa = sa_ref[rows, :]
            sb = sb_ref[rows, :]

        def rope_wide(v):
            if not lat:
                return v
            slabs = [_rope(v[:, j:j + LANES], cos, sa, sb) for j in range(0, v.shape[1], LANES)]
            return slabs[0] if len(slabs) == 1 else jnp.concatenate(slabs, axis=1)

        qscale = HEAD_DIM ** -0.5
        qa = rope_wide(_group_rms(proj(C_QA, 512), bd_ref[...], g0_ref[...])) * qscale
        for h in range(N_HEADS):
            qa_s[h, rows, :] = qa[:, h * HEAD_DIM:(h + 1) * HEAD_DIM].astype(BF16)
        ka_n = _group_rms(proj(C_KA, LANES), bd_ref[0:LANES, 0:LANES], g1_ref[...])
        ka = rope_wide(ka_n)
        va = proj(C_VA, LANES)
        qw = rope_wide(proj(C_QW, 512)) * qscale
        for h in range(N_HEADS):
            qw_s[h, rows, :] = qw[:, h * HEAD_DIM:(h + 1) * HEAD_DIM].astype(BF16)
        kw_raw = proj(C_KW, LANES)
        kw = rope_wide(kw_raw)
        vw = proj(C_VW, LANES)
        for kv in range(N_KV):
            sl = slice(kv * HEAD_DIM, (kv + 1) * HEAD_DIM)
            ka_s[kv, rows, :] = ka[:, sl].astype(BF16)
            kw_s[kv, rows, :] = kw[:, sl].astype(BF16)
            if not lat:
                ka_o[kv, rows, :] = ka_n[:, sl]
                va_o[kv, rows, :] = va[:, sl]
                kw_o[kv, rows, :] = kw_raw[:, sl]
                vw_o[kv, rows, :] = vw[:, sl]
        qm = proj(C_QM, 256)
        km = proj(C_KM, 256) * (ML_DK ** -0.5)
        for h in range(ML_HEADS):
            sl = slice(h * ML_DK, (h + 1) * ML_DK)
            qm_s[h, rows, :] = qm[:, sl].astype(BF16)
            km_s[h, rows, :] = km[:, sl].astype(BF16)
        vm = proj(C_VM, 512)
        om_s[rows, :] = proj(C_OM, 512)
        g_s[rows, :] = proj(C_GT, LANES) + gb_ref[...]
        for j in range(chunks_per_block):
            cj = i * chunks_per_block + j
            cr = slice(j * CHUNK, (j + 1) * CHUNK)
            va_t = jnp.transpose(va[cr, :]).astype(BF16)
            vw_t = jnp.transpose(vw[cr, :]).astype(BF16)
            for kv in range(N_KV):
                vat_s[cj, kv, 0:HEAD_DIM, :] = va_t[kv * HEAD_DIM:(kv + 1) * HEAD_DIM, :]
                vat_s[cj, kv, HEAD_DIM:VT_ROWS, :] = ones_tile
                vwt_s[cj, kv, 0:HEAD_DIM, :] = vw_t[kv * HEAD_DIM:(kv + 1) * HEAD_DIM, :]
                vwt_s[cj, kv, HEAD_DIM:VT_ROWS, :] = ones_tile
            for h in range(ML_HEADS):
                vmt_s[cj, h, 0:ML_DV, :] = jnp.transpose(vm[cr, h * ML_DV:(h + 1) * ML_DV]).astype(BF16)
                vmt_s[cj, h, ML_DV:ST_ROWS, :] = ones_tile

    if n_rows == 1:
        phase1(0)
    else:
        def body1(i, carry):
            phase1(i)
            return carry
        lax.fori_loop(0, n_rows, body1, 0)

    if lat:
        eye = eye_ref[...]
        for kv in range(N_KV):
            ka_s[kv, T:T + PAST_LEN, :] = cka_ref[kv].astype(BF16)
            kwc_s[kv] = ckw_ref[kv].astype(BF16)
            cva_t = _dot_tb(eye, cva_ref[kv].astype(BF16)).astype(BF16)
            cvw_t = _dot_tb(eye, cvw_ref[kv].astype(BF16)).astype(BF16)
            for j in range(PAST_LEN // CHUNK):
                vat_s[n_chunks + j, kv, 0:HEAD_DIM, :] = cva_t[:, j * CHUNK:(j + 1) * CHUNK]
                vat_s[n_chunks + j, kv, HEAD_DIM:VT_ROWS, :] = ones_tile
                vwct_s[kv, 0:HEAD_DIM, j * CHUNK:(j + 1) * CHUNK] = cvw_t[:, j * CHUNK:(j + 1) * CHUNK]
                vwct_s[kv, HEAD_DIM:VT_ROWS, j * CHUNK:(j + 1) * CHUNK] = ones_tile

    def sink_row(kv, tq):
        col = lax.broadcasted_iota(jnp.int32, (1, N_GROUP * tq), 1)
        row = jnp.full((1, N_GROUP * tq), sink_ref[layer * N_HEADS + kv * N_GROUP], F32)
        for g in range(1, N_GROUP):
            row = jnp.where(col >= g * tq, sink_ref[layer * N_HEADS + kv * N_GROUP + g], row)
        return row

    def load_q(q_s, kv, q0, tq):
        return q_s[kv * N_GROUP:(kv + 1) * N_GROUP, pl.ds(q0, tq), :].reshape(N_GROUP * tq, HEAD_DIM)

    def store_heads(dst, q0, tq, kv, o_t):
        for p in range(N_GROUP // 2):
            blk = jnp.concatenate([o_t[:, (2 * p) * tq:(2 * p + 1) * tq],
                                   o_t[:, (2 * p + 1) * tq:(2 * p + 2) * tq]], axis=0)
            c0 = (kv * N_GROUP + 2 * p) * HEAD_DIM
            dst[pl.ds(q0, tq), c0:c0 + LANES] = jnp.transpose(blk)

    def values_t(ref, kv, n):
        return jnp.concatenate([ref[j, kv] for j in range(n)], axis=1)

    def attn_full(q0, tq):
        n_key_chunks = ka_s.shape[1] // CHUNK
        for kv in range(N_KV):
            o_t = _attend_t(load_q(qa_s, kv, q0, tq),
                            [(ka_s[kv], values_t(vat_s, kv, n_key_chunks), None)], None)
            store_heads(ya_s, q0, tq, kv, o_t)

    def attn_ctx_sink(q0, tq):
        for kv in range(N_KV):
            o_t = _attend_t(load_q(qw_s, kv, q0, tq),
                            [(kw_s[kv], values_t(vwt_s, kv, n_chunks), None)], sink_row(kv, tq))
            store_heads(yw_s, q0, tq, kv, o_t)

    def attn_band(n):
        tq = CHUNK
        band_chunks = 3
        band = band_chunks * CHUNK
        q0 = pl.multiple_of(n * tq, tq)
        j0 = jnp.clip(n - 1, 0, n_chunks - band_chunks)
        w0 = pl.multiple_of(j0 * CHUNK, CHUNK)
        kpos = w0 + lax.broadcasted_iota(jnp.int32, (band, N_GROUP * tq), 0)
        qpos = q0 + (lax.broadcasted_iota(jnp.int32, (band, N_GROUP * tq), 1) & (tq - 1))
        valid = jnp.abs(kpos - qpos) <= WINDOW
        for kv in range(N_KV):
            v_band = jnp.concatenate([vwt_s[j0 + j, kv] for j in range(band_chunks)], axis=1)
            parts = [(kw_s[kv, pl.ds(w0, band), :], v_band, valid),
                     (kwc_s[kv], vwct_s[kv], None)]
            o_t = _attend_t(load_q(qw_s, kv, q0, tq), parts, sink_row(kv, tq))
            store_heads(yw_s, q0, tq, kv, o_t)

    if lat:
        def body2(n, carry):
            attn_full(pl.multiple_of(n * CHUNK, CHUNK), CHUNK)
            attn_band(n)
            return carry
        lax.fori_loop(0, n_chunks, body2, 0)
    else:
        attn_full(0, T)
        attn_ctx_sink(0, T)

    hmt_s[...] = jnp.zeros_like(hmt_s)
    for d in range(2):
        for h in range(ML_HEADS):
            idx = d * ML_HEADS + h
            c_s[idx, ML_DV:ST_ROWS, :] = jnp.zeros((BF16_ROWS, ML_DK), F32)
            if lat:
                c_s[idx, 0:ML_DV, :] = c0_ref[d, h]
                c_s[idx, ML_DV:ML_DV + 1, :] = n0_ref[d, h:h + 1, :]
                m_s[idx:idx + 1, :] = jnp.broadcast_to(m0_ref[d:d + 1, h:h + 1], (1, LANES))
            else:
                c_s[idx, 0:ML_DV, :] = jnp.zeros((ML_DV, ML_DK), F32)
                m_s[idx:idx + 1, :] = jnp.zeros((1, LANES), F32)

    L = CHUNK
    s_idx = lax.broadcasted_iota(jnp.int32, (L, L), 0)
    t_idx = lax.broadcasted_iota(jnp.int32, (L, L), 1)
    ones_ll = jnp.ones((L, L), BF16)

    def mlstm_chunk(c):
        for d in range(2):
            cc = c if d == 0 else n_chunks - 1 - c
            rows = pl.ds(pl.multiple_of(cc * L, L), L)
            g = g_s[rows, :]
            f_hi, f_lo = _split(_log_sigmoid(g))
            tri_c = tril_ref[...] if d == 0 else triu_ref[...]
            tri_r = triu_ref[...] if d == 0 else tril_ref[...]
            cum = _dot(tri_c, f_hi) + _dot(tri_c, f_lo)
            r = g - pltpu.roll(cum, LANES - ML_HEADS, 1)
            r1, r2, r3 = _split3(r)
            sel = sel_ref[d]
            rb = _dot(r1, sel) + _dot(r2, sel) + _dot(r3, sel)
            g_t = jnp.transpose(g)[0:N_GATES, :]
            ft_hi, ft_lo = _split(_log_sigmoid(g_t))
            tr = jnp.concatenate([tri_r, ones_ll], axis=1)
            ct = _dot(ft_hi, tr) + _dot(ft_lo, tr)
            valid = (s_idx <= t_idx) if d == 0 else (s_idx >= t_idx)
            for h in range(ML_HEADS):
                idx = d * ML_HEADS + h
                gi = 2 * ML_HEADS * d + h
                gf = gi + ML_HEADS
                b_row = ct[gf:gf + 1, 0:L]
                b_last = ct[gf:gf + 1, L:2 * L]
                i_row = g_t[gi:gi + 1, :]
                m_prev = m_s[idx:idx + 1, :]
                a_row = b_row + m_prev
                dm = jnp.where(valid, b_row + rb[:, h * L:(h + 1) * L], -jnp.inf)
                mt = jnp.maximum(a_row, jnp.max(dm, axis=0, keepdims=True))
                q = qm_s[h, rows, :]
                k = km_s[h, rows, :]
                v_t = vmt_s[cc, h]
                s_t = _dot_tb(k, q) * jnp.exp(dm - mt)
                w_inter = jnp.exp(a_row - mt)
                state = c_s[idx]
                tot = _dot(v_t, s_t.astype(BF16)) + w_inter * _dot_tb(state.astype(BF16), q)
                h_t = tot[0:ML_DV, :] / jnp.maximum(jnp.abs(tot[ML_DV:ML_DV + 1, :]), jnp.exp(-mt))
                hmt_s[cc, h * ML_DV:(h + 1) * ML_DV, :] += h_t
                g_row = b_last - b_row + i_row
                m_new = jnp.maximum(b_last + m_prev, jnp.max(g_row, axis=-1, keepdims=True))
                ws = jnp.exp(g_row - m_new)
                wc = jnp.exp(b_last + m_prev - m_new)
                wv = (v_t.astype(F32) * ws).astype(BF16)
                c_s[idx] = wc[:, 0:ML_DK] * state + _dot(wv, k)
                m_s[idx:idx + 1, :] = m_new

    def body3(c, carry):
        mlstm_chunk(c)
        return carry
    lax.fori_loop(0, n_chunks, body3, 0)

    if not lat:
        for d in range(2):
            for h in range(ML_HEADS):
                idx = d * ML_HEADS + h
                c_o[d, h] = c_s[idx, 0:ML_DV, :]
                n_o[d, h:h + 1, :] = c_s[idx, ML_DV:ML_DV + 1, :]
                m_o[d:d + 1, h:h + 1] = m_s[idx:idx + 1, 0:1]

    def phase3(i):
        r0 = i * ROW_BLOCK if isinstance(i, int) else pl.multiple_of(i * ROW_BLOCK, ROW_BLOCK)
        rows = pl.ds(r0, ROW_BLOCK)
        u = u_s[rows, :]

        def proj(c0, width):
            return _dot(u, w_ref[:, c0:c0 + width])

        hm = jnp.concatenate(
            [jnp.concatenate([jnp.transpose(hmt_s[i * chunks_per_block + j, h * ML_DV:(h + 1) * ML_DV, :])
                              for h in range(ML_HEADS)], axis=1)
             for j in range(chunks_per_block)], axis=0)
        ys = (ya_s[rows, :], _sigmoid(om_s[rows, :]) * hm, yw_s[rows, :])
        merged = None
        for b, (yb, zc) in enumerate(zip(ys, (C_ZA, C_ZM, C_ZW))):
            z = proj(zc, BRANCH_W)
            t = (yb * (z * _sigmoid(z))).astype(BF16)
            pb = _sigmoid(proj(C_GMERGE + b * D_MODEL, D_MODEL)) * _dot(t, wb_ref[b])
            merged = pb if merged is None else merged + pb
        o = _dot(merged.astype(BF16), wo_ref[...])
        hres = ALPHA * x_ref[rows, :] + gate * o
        y_ref[rows, :] = _layer_norm(hres) * lng_ref[...] + lnb_ref[...]

    if n_rows == 1:
        phase3(0)
    else:
        def body4(i, carry):
            phase3(i)
            return carry
        lax.fori_loop(0, n_rows, body4, 0)


def _const_spec(shape):
    nd = len(shape)
    return pl.BlockSpec(shape, lambda b: (0,) * nd, pipeline_mode=pl.Buffered(1))


def _layer_call(lat, layer, x, mod, weights, consts, extra):
    B, T, _ = x.shape
    S = T + PAST_LEN if lat else T
    n_chunks = T // CHUNK

    seq_spec = pl.BlockSpec((None, T, D_MODEL), lambda b: (b, 0, 0), pipeline_mode=pl.Buffered(1))
    if lat:
        mod_spec = pl.BlockSpec((None, None, 3, D_MODEL), lambda b: (layer, b + 1, 0, 0))
    else:
        mod_spec = pl.BlockSpec((None, None, 3, D_MODEL), lambda b: (layer, 0, 0, 0))
    *vmem_consts, sink = consts
    in_specs = [seq_spec, mod_spec]
    in_specs += [_const_spec(a.shape) for a in (*weights, *vmem_consts)]
    in_specs.append(pl.BlockSpec(memory_space=pltpu.SMEM))
    args = [x, mod, *weights, *vmem_consts, sink]

    if lat:
        cos, sa, sb, cka, cva, ckw, cvw, sc, sn, sm = extra
        in_specs += [_const_spec(cos.shape)] * 3
        kv_spec = pl.BlockSpec((None, None, N_KV, PAST_LEN, HEAD_DIM), lambda b: (b, layer, 0, 0, 0))
        in_specs += [kv_spec] * 4
        in_specs += [
            pl.BlockSpec((None, None, 2, ML_HEADS, ML_DV, ML_DK), lambda b: (b, layer, 0, 0, 0, 0)),
            pl.BlockSpec((None, None, 2, ML_HEADS, ML_DK), lambda b: (b, layer, 0, 0, 0)),
            pl.BlockSpec((None, None, 2, ML_HEADS), lambda b: (b, layer, 0, 0)),
        ]
        args += [cos, sa, sb, cka, cva, ckw, cvw, sc, sn, sm]
        out_specs = seq_spec
        out_shape = jax.ShapeDtypeStruct((B, T, D_MODEL), F32)
    else:
        kv_out = pl.BlockSpec((None, N_KV, T, HEAD_DIM), lambda b: (b, 0, 0, 0))
        out_specs = [
            seq_spec, kv_out, kv_out, kv_out, kv_out,
            pl.BlockSpec((None, 2, ML_HEADS, ML_DV, ML_DK), lambda b: (b, 0, 0, 0, 0)),
            pl.BlockSpec((None, 2, ML_HEADS, ML_DK), lambda b: (b, 0, 0, 0)),
            pl.BlockSpec((None, 2, ML_HEADS), lambda b: (b, 0, 0)),
        ]
        kv_shape = jax.ShapeDtypeStruct((B, N_KV, T, HEAD_DIM), F32)
        out_shape = [
            jax.ShapeDtypeStruct((B, T, D_MODEL), F32), kv_shape, kv_shape, kv_shape, kv_shape,
            jax.ShapeDtypeStruct((B, 2, ML_HEADS, ML_DV, ML_DK), F32),
            jax.ShapeDtypeStruct((B, 2, ML_HEADS, ML_DK), F32),
            jax.ShapeDtypeStruct((B, 2, ML_HEADS), F32),
        ]

    scratch = [
        pltpu.VMEM((T, D_MODEL), BF16),
        pltpu.VMEM((N_HEADS, T, HEAD_DIM), BF16),
        pltpu.VMEM((N_HEADS, T, HEAD_DIM), BF16),
        pltpu.VMEM((N_KV, S, HEAD_DIM), BF16),
        pltpu.VMEM((S // CHUNK, N_KV, VT_ROWS, CHUNK), BF16),
        pltpu.VMEM((N_KV, T, HEAD_DIM), BF16),
        pltpu.VMEM((n_chunks, N_KV, VT_ROWS, CHUNK), BF16),
        pltpu.VMEM((ML_HEADS, T, ML_DK), BF16),
        pltpu.VMEM((ML_HEADS, T, ML_DK), BF16),
        pltpu.VMEM((n_chunks, ML_HEADS, ST_ROWS, CHUNK), BF16),
        pltpu.VMEM((T, ML_HEADS * ML_DV), F32),
        pltpu.VMEM((T, LANES), F32),
        pltpu.VMEM((T, BRANCH_W), F32),
        pltpu.VMEM((T, BRANCH_W), F32),
        pltpu.VMEM((n_chunks, ML_HEADS * ML_DV, CHUNK), F32),
        pltpu.VMEM((2 * ML_HEADS, ST_ROWS, ML_DK), F32),
        pltpu.VMEM((2 * ML_HEADS, LANES), F32),
    ]
    if lat:
        scratch += [pltpu.VMEM((N_KV, PAST_LEN, HEAD_DIM), BF16),
                    pltpu.VMEM((N_KV, VT_ROWS, PAST_LEN), BF16)]

    return pl.pallas_call(
        functools.partial(_layer_kernel, lat, T, layer),
        grid=(B,),
        in_specs=in_specs,
        out_specs=out_specs,
        out_shape=out_shape,
        scratch_shapes=scratch,
        compiler_params=pltpu.CompilerParams(
            dimension_semantics=("arbitrary",), vmem_limit_bytes=VMEM_LIMIT_BYTES),
        name=("latent_layer" if lat else "context_layer") + str(layer),
    )(*args)


def _rope_tables(T):
    rows = T // GRID_W
    row = jnp.repeat(jnp.arange(rows, dtype=F32), GRID_W)
    col = jnp.tile(jnp.arange(GRID_W, dtype=F32), rows)
    inv = ROPE_THETA ** (-jnp.arange(0, AXIS_DIM, 2, dtype=F32) / AXIS_DIM)
    ar = row[:, None] * inv
    ac = col[:, None] * inv
    ang = jnp.concatenate([ar, ar, ac, ac], axis=-1)
    ang = jnp.concatenate([ang, ang], axis=-1)
    first_half = (jnp.arange(LANES) % AXIS_DIM) < (AXIS_DIM // 2)
    cos = jnp.cos(ang)
    sin = jnp.sin(ang)
    sa = jnp.where(first_half[None, :], -sin, 0.0)
    sb = jnp.where(first_half[None, :], 0.0, sin)
    return cos, sa, sb


def _constants(sink_logit):
    grp = np.arange(N_HEADS * HEAD_DIM) // HEAD_DIM
    bd = jnp.asarray(grp[:, None] == grp[None, :], dtype=BF16)
    ti = np.arange(CHUNK)
    tril = jnp.asarray(ti[None, :] <= ti[:, None], dtype=BF16)
    triu = jnp.asarray(ti[None, :] >= ti[:, None], dtype=BF16)
    lane = np.arange(LANES)[:, None]
    blk = (np.arange(ML_HEADS * CHUNK) // CHUNK)[None, :]
    sel = jnp.asarray(np.stack([lane == 2 * ML_HEADS * d + blk for d in range(2)]), dtype=BF16)
    eye = jnp.asarray(np.eye(HEAD_DIM), dtype=BF16)
    return bd, tril, triu, sel, eye, sink_logit.reshape(-1)


def kernel(x_prompt, x_sample, cache_attn_k, cache_attn_v, cache_win_k, cache_win_v, state_mlstm_C,
           state_mlstm_n, state_mlstm_m, c, c_ctx, w_mod, b_mod, w_in, qk_gain, sink_logit,
           mlstm_gate_bias, w_branch, w_out, ln_gain, ln_bias):
    dec_b = x_sample.shape[0]
    assert dec_b + 1 <= 8
    cond = jnp.concatenate([c_ctx[None, :], c, jnp.zeros((8 - 1 - dec_b, D_MODEL), F32)], axis=0)
    mod = _modulation(cond, w_mod, b_mod).reshape(DEPTH, 8, 3, D_MODEL)

    consts = _constants(sink_logit)
    cos, sa, sb = _rope_tables(x_sample.shape[1])
    gate_end = GATE_COL + N_GATES
    col_pad = jnp.zeros((D_MODEL, LANES - N_GATES), BF16)
    lane_pad = jnp.zeros((1, LANES - N_GATES), F32)

    xp, xs = x_prompt, x_sample
    ctx_outs = []
    for l in range(DEPTH):
        w = jnp.concatenate([w_in[l, :, :gate_end].astype(BF16), col_pad,
                             w_in[l, :, gate_end:].astype(BF16)], axis=1)
        weights = (
            w, w_branch[l].astype(BF16), w_out[l].astype(BF16),
            jnp.tile(qk_gain[l, 0:1, :], (1, N_HEADS)),
            jnp.tile(qk_gain[l, 1:2, :], (1, N_KV)),
            jnp.concatenate([mlstm_gate_bias[l].reshape(1, N_GATES), lane_pad], axis=1),
            ln_gain[l][None, :], ln_bias[l][None, :],
        )
        outs = _layer_call(False, l, xp, mod, weights, consts, None)
        xp = outs[0]
        ctx_outs.append(outs[1:])
        extra = (cos, sa, sb, cache_attn_k, cache_attn_v, cache_win_k, cache_win_v,
                 state_mlstm_C, state_mlstm_n, state_mlstm_m)
        xs = _layer_call(True, l, xs, mod, weights, consts, extra)
    stacked = [jnp.stack([ctx_outs[l][i] for l in range(DEPTH)], axis=1) for i in range(7)]
    return (xp, xs, *stacked)
```

```python
import functools

import jax
import jax.numpy as jnp
import numpy as np
from jax import lax
from jax.experimental import pallas as pl
from jax.experimental.pallas import tpu as pltpu

F32 = jnp.float32
BF16 = jnp.bfloat16

D_MODEL = 1024
DEPTH = 2
PAST_LEN = 256
GRID_W = 64
HEAD_DIM = 64
N_HEADS = 8
N_KV = 2
N_GROUP = N_HEADS // N_KV
WINDOW = 128
ML_HEADS = 4
ML_DK = 64
ML_DV = 128
CHUNK = 128
BRANCH_W = 512
ROPE_THETA = 10000.0
AXIS_DIM = HEAD_DIM // 2
LN_EPS = 1e-6
RMS_EPS = 1e-6
ALPHA = (2.0 * DEPTH) ** 0.25
LANES = 128
BF16_ROWS = 16
ROW_BLOCK = 256
VMEM_LIMIT_BYTES = 60 * 1024 * 1024
VT_ROWS = HEAD_DIM + BF16_ROWS
ST_ROWS = ML_DV + BF16_ROWS

_SIZES = (512, 128, 128, 512, 256, 256, 512, 512, 128, 512, 512, 128, 128, 512, 3072)
_OFF = [int(v) for v in np.concatenate([[0], np.cumsum(_SIZES)])]
(C_QA, C_KA, C_VA, C_ZA, C_QM, C_KM, C_VM, C_OM, C_GT, C_ZM, C_QW, C_KW, C_VW, C_ZW, C_GMERGE) = _OFF[:-1]
W_COLS = _OFF[-1]
GATE_COL = 2816
N_GATES = 4 * ML_HEADS


def _dot(a, b):
    return jnp.dot(a, b, preferred_element_type=F32)


def _dot_tb(a, b):
    return lax.dot_general(a, b, (((1,), (1,)), ((), ())), preferred_element_type=F32)


def _split(a):
    hi = a.astype(BF16)
    lo = (a - hi.astype(F32)).astype(BF16)
    return hi, lo


def _split3(a):
    hi = a.astype(BF16)
    r = a - hi.astype(F32)
    mid = r.astype(BF16)
    lo = (r - mid.astype(F32)).astype(BF16)
    return hi, mid, lo


def _layer_norm(x):
    mu = jnp.mean(x, axis=-1, keepdims=True)
    xc = x - mu
    var = jnp.mean(xc * xc, axis=-1, keepdims=True)
    return xc * lax.rsqrt(var + LN_EPS)


def _log_sigmoid(x):
    return jnp.minimum(x, 0.0) - jnp.log(1.0 + jnp.exp(-jnp.abs(x)))


def _sigmoid(x):
    return jax.nn.sigmoid(x)


def _mod_kernel(c_ref, w_ref, b_ref, o_ref):
    c = c_ref[...]
    s = c * _sigmoid(c)
    s_hi, s_lo = _split(s)
    w = w_ref[...]
    w_hi, w_lo = _split(w)
    o_ref[...] = _dot(s_hi, w_hi) + _dot(s_lo, w_hi) + _dot(s_hi, w_lo) + b_ref[...]


def _modulation(cond, w_mod, b_mod):
    rows = cond.shape[0]
    tn = 1024
    return pl.pallas_call(
        _mod_kernel,
        grid=(DEPTH, 3 * D_MODEL // tn),
        in_specs=[
            pl.BlockSpec((rows, D_MODEL), lambda l, j: (0, 0)),
            pl.BlockSpec((None, D_MODEL, tn), lambda l, j: (l, 0, j)),
            pl.BlockSpec((None, 1, tn), lambda l, j: (l, 0, j)),
        ],
        out_specs=pl.BlockSpec((None, rows, tn), lambda l, j: (l, 0, j)),
        out_shape=jax.ShapeDtypeStruct((DEPTH, rows, 3 * D_MODEL), F32),
        compiler_params=pltpu.CompilerParams(dimension_semantics=("arbitrary", "arbitrary")),
        name="adaln_modulation",
    )(cond, w_mod, b_mod.reshape(DEPTH, 1, 3 * D_MODEL))


def _rope(x, cos, sa, sb):
    return x * cos + pltpu.roll(x, LANES - AXIS_DIM // 2, 1) * sa + pltpu.roll(x, AXIS_DIM // 2, 1) * sb


def _group_rms(x, bd, gain):
    hi, lo = _split(x * x)
    ms = (_dot(hi, bd) + _dot(lo, bd)) * (1.0 / HEAD_DIM)
    return x * lax.rsqrt(ms + RMS_EPS) * gain


def _ones_row_tile():
    r = lax.broadcasted_iota(jnp.int32, (BF16_ROWS, LANES), 0)
    return jnp.where(r == 0, 1.0, 0.0).astype(BF16)


def _attend_t(qs, parts, sink_row):
    scores = []
    for k, _, valid in parts:
        s = _dot_tb(k, qs)
        if valid is not None:
            s = jnp.where(valid, s, -jnp.inf)
        scores.append(s)
    mx = functools.reduce(jnp.maximum, [jnp.max(s, axis=0, keepdims=True) for s in scores])
    if sink_row is not None:
        mx = jnp.maximum(mx, sink_row)
    tot = None
    for s, (_, v_t, _) in zip(scores, parts):
        o = _dot(v_t, jnp.exp(s - mx).astype(BF16))
        tot = o if tot is None else tot + o
    den = tot[HEAD_DIM:HEAD_DIM + 1, :]
    if sink_row is not None:
        den = den + jnp.exp(sink_row - mx)
    return tot[0:HEAD_DIM, :] / den


def _layer_kernel(lat, T, layer, *refs):
    refs = list(refs)
    x_ref, mod_ref, w_ref, wb_ref, wo_ref = refs[:5]
    g0_ref, g1_ref, gb_ref, lng_ref, lnb_ref = refs[5:10]
    bd_ref, tril_ref, triu_ref, sel_ref, eye_ref, sink_ref = refs[10:16]
    pos = 16
    if lat:
        cos_ref, sa_ref, sb_ref = refs[pos:pos + 3]
        cka_ref, cva_ref, ckw_ref, cvw_ref, c0_ref, n0_ref, m0_ref = refs[pos + 3:pos + 10]
        pos += 10
        y_ref = refs[pos]
        pos += 1
    else:
        y_ref, ka_o, va_o, kw_o, vw_o, c_o, n_o, m_o = refs[pos:pos + 8]
        pos += 8
    (u_s, qa_s, qw_s, ka_s, vat_s, kw_s, vwt_s, qm_s, km_s, vmt_s, om_s, g_s,
     ya_s, yw_s, hmt_s, c_s, m_s) = refs[pos:pos + 17]
    pos += 17
    if lat:
        kwc_s, vwct_s = refs[pos:pos + 2]

    n_rows = T // ROW_BLOCK
    n_chunks = T // CHUNK
    chunks_per_block = ROW_BLOCK // CHUNK
    shift = mod_ref[0:1, :]
    scale = mod_ref[1:2, :]
    gate = mod_ref[2:3, :]
    ones_tile = _ones_row_tile()

    def phase1(i):
        R = ROW_BLOCK
        r0 = i * R if isinstance(i, int) else pl.multiple_of(i * R, R)
        rows = pl.ds(r0, R)
        x = x_ref[rows, :]
        u = (_layer_norm(x) * (1.0 + scale) + shift).astype(BF16)
        u_s[rows, :] = u

        def proj(c0, width):
            return _dot(u, w_ref[:, c0:c0 + width])

        if lat:
            cos = cos_ref[rows, :]
            sa = sa_ref[rows, :]
            sb = sb_ref[rows, :]

        def rope_wide(v):
            if not lat:
                return v
            slabs = [_rope(v[:, j:j + LANES], cos, sa, sb) for j in range(0, v.shape[1], LANES)]
            return slabs[0] if len(slabs) == 1 else jnp.concatenate(slabs, axis=1)

        qscale = HEAD_DIM ** -0.5
        qa = rope_wide(_group_rms(proj(C_QA, 512), bd_ref[...], g0_ref[...])) * qscale
        for h in range(N_HEADS):
            qa_s[h, rows, :] = qa[:, h * HEAD_DIM:(h + 1) * HEAD_DIM].astype(BF16)
        ka_n = _group_rms(proj(C_KA, LANES), bd_ref[0:LANES, 0:LANES], g1_ref[...])
        ka = rope_wide(ka_n)
        va = proj(C_VA, LANES)
        qw = rope_wide(proj(C_QW, 512)) * qscale
        for h in range(N_HEADS):
            qw_s[h, rows, :] = qw[:, h * HEAD_DIM:(h + 1) * HEAD_DIM].astype(BF16)
        kw_raw = proj(C_KW, LANES)
        kw = rope_wide(kw_raw)
        vw = proj(C_VW, LANES)
        for kv in range(N_KV):
            sl = slice(kv * HEAD_DIM, (kv + 1) * HEAD_DIM)
            ka_s[kv, rows, :] = ka[:, sl].astype(BF16)
            kw_s[kv, rows, :] = kw[:, sl].astype(BF16)
            if not lat:
                ka_o[kv, rows, :] = ka_n[:, sl]
                va_o[kv, rows, :] = va[:, sl]
                kw_o[kv, rows, :] = kw_raw[:, sl]
                vw_o[kv, rows, :] = vw[:, sl]
        qm_s[rows, :] = proj(C_QM, 256).astype(BF16)
        km_s[rows, :] = (proj(C_KM, 256) * (ML_DK ** -0.5)).astype(BF16)
        vm = proj(C_VM, 512)
        om_s[rows, :] = proj(C_OM, 512)
        g_s[rows, :] = proj(C_GT, LANES) + gb_ref[...]
        for j in range(chunks_per_block):
            cj = i * chunks_per_block + j
            cr = slice(j * CHUNK, (j + 1) * CHUNK)
            va_t = jnp.transpose(va[cr, :]).astype(BF16)
            vw_t = jnp.transpose(vw[cr, :]).astype(BF16)
            for kv in range(N_KV):
                vat_s[cj, kv, 0:HEAD_DIM, :] = va_t[kv * HEAD_DIM:(kv + 1) * HEAD_DIM, :]
                vat_s[cj, kv, HEAD_DIM:VT_ROWS, :] = ones_tile
                vwt_s[cj, kv, 0:HEAD_DIM, :] = vw_t[kv * HEAD_DIM:(kv + 1) * HEAD_DIM, :]
                vwt_s[cj, kv, HEAD_DIM:VT_ROWS, :] = ones_tile
            for h in range(ML_HEADS):
                vmt_s[cj, h, 0:ML_DV, :] = jnp.transpose(vm[cr, h * ML_DV:(h + 1) * ML_DV]).astype(BF16)
                vmt_s[cj, h, ML_DV:ST_ROWS, :] = ones_tile

    if n_rows == 1:
        phase1(0)
    else:
        def body1(i, carry):
            phase1(i)
            return carry
        lax.fori_loop(0, n_rows, body1, 0)

    if lat:
        eye = eye_ref[...]
        for kv in range(N_KV):
            ka_s[kv, T:T + PAST_LEN, :] = cka_ref[kv].astype(BF16)
            kwc_s[kv] = ckw_ref[kv].astype(BF16)
            cva_t = _dot_tb(eye, cva_ref[kv].astype(BF16)).astype(BF16)
            cvw_t = _dot_tb(eye, cvw_ref[kv].astype(BF16)).astype(BF16)
            for j in range(PAST_LEN // CHUNK):
                vat_s[n_chunks + j, kv, 0:HEAD_DIM, :] = cva_t[:, j * CHUNK:(j + 1) * CHUNK]
                vat_s[n_chunks + j, kv, HEAD_DIM:VT_ROWS, :] = ones_tile
                vwct_s[kv, 0:HEAD_DIM, j * CHUNK:(j + 1) * CHUNK] = cvw_t[:, j * CHUNK:(j + 1) * CHUNK]
                vwct_s[kv, HEAD_DIM:VT_ROWS, j * CHUNK:(j + 1) * CHUNK] = ones_tile

    def sink_row(kv, tq):
        col = lax.broadcasted_iota(jnp.int32, (1, N_GROUP * tq), 1)
        row = jnp.full((1, N_GROUP * tq), sink_ref[layer * N_HEADS + kv * N_GROUP], F32)
        for g in range(1, N_GROUP):
            row = jnp.where(col >= g * tq, sink_ref[layer * N_HEADS + kv * N_GROUP + g], row)
        return row

    def load_q(q_s, kv, q0, tq):
        return q_s[kv * N_GROUP:(kv + 1) * N_GROUP, pl.ds(q0, tq), :].reshape(N_GROUP * tq, HEAD_DIM)

    def store_heads(dst, q0, tq, kv, o_t):
        for p in range(N_GROUP // 2):
            blk = jnp.concatenate([o_t[:, (2 * p) * tq:(2 * p + 1) * tq],
                                   o_t[:, (2 * p + 1) * tq:(2 * p + 2) * tq]], axis=0)
            c0 = (kv * N_GROUP + 2 * p) * HEAD_DIM
            dst[pl.ds(q0, tq), c0:c0 + LANES] = jnp.transpose(blk)

    def values_t(ref, kv, n):
        return jnp.concatenate([ref[j, kv] for j in range(n)], axis=1)

    def attn_full(q0, tq):
        n_key_chunks = ka_s.shape[1] // CHUNK
        for kv in range(N_KV):
            o_t = _attend_t(load_q(qa_s, kv, q0, tq),
                            [(ka_s[kv], values_t(vat_s, kv, n_key_chunks), None)], None)
            store_heads(ya_s, q0, tq, kv, o_t)

    def attn_ctx_sink(q0, tq):
        for kv in range(N_KV):
            o_t = _attend_t(load_q(qw_s, kv, q0, tq),
                            [(kw_s[kv], values_t(vwt_s, kv, n_chunks), None)], sink_row(kv, tq))
            store_heads(yw_s, q0, tq, kv, o_t)

    def attn_band(n):
        tq = CHUNK
        band_chunks = 3
        band = band_chunks * CHUNK
        q0 = pl.multiple_of(n * tq, tq)
        j0 = jnp.clip(n - 1, 0, n_chunks - band_chunks)
        w0 = pl.multiple_of(j0 * CHUNK, CHUNK)
        kpos = w0 + lax.broadcasted_iota(jnp.int32, (band, N_GROUP * tq), 0)
        qpos = q0 + (lax.broadcasted_iota(jnp.int32, (band, N_GROUP * tq), 1) & (tq - 1))
        valid = jnp.abs(kpos - qpos) <= WINDOW
        for kv in range(N_KV):
            v_band = jnp.concatenate([vwt_s[j0 + j, kv] for j in range(band_chunks)], axis=1)
            parts = [(kw_s[kv, pl.ds(w0, band), :], v_band, valid),
                     (kwc_s[kv], vwct_s[kv], None)]
            o_t = _attend_t(load_q(qw_s, kv, q0, tq), parts, sink_row(kv, tq))
            store_heads(yw_s, q0, tq, kv, o_t)

    if lat:
        def body2(n, carry):
            attn_full(pl.multiple_of(n * CHUNK, CHUNK), CHUNK)
            attn_band(n)
            return carry
        lax.fori_loop(0, n_chunks, body2, 0)
    else:
        attn_full(0, T)
        attn_ctx_sink(0, T)

    n_pairs = ML_HEADS // 2
    hmt_s[...] = jnp.zeros_like(hmt_s)
    for d in range(2):
        for h in range(ML_HEADS):
            idx = d * n_pairs + h // 2
            hl = slice((h % 2) * ML_DK, (h % 2 + 1) * ML_DK)
            c_s[idx, ML_DV:ST_ROWS, hl] = jnp.zeros((BF16_ROWS, ML_DK), F32)
            if lat:
                c_s[idx, 0:ML_DV, hl] = c0_ref[d, h]
                c_s[idx, ML_DV:ML_DV + 1, hl] = n0_ref[d, h:h + 1, :]
                m_s[d * ML_HEADS + h:d * ML_HEADS + h + 1, :] = jnp.broadcast_to(
                    m0_ref[d:d + 1, h:h + 1], (1, LANES))
            else:
                c_s[idx, 0:ML_DV, hl] = jnp.zeros((ML_DV, ML_DK), F32)
                m_s[d * ML_HEADS + h:d * ML_HEADS + h + 1, :] = jnp.zeros((1, LANES), F32)

    L = CHUNK
    s_idx = lax.broadcasted_iota(jnp.int32, (L, ML_HEADS * L), 0)
    t_idx = lax.broadcasted_iota(jnp.int32, (L, ML_HEADS * L), 1) & (L - 1)
    lane_row = lax.broadcasted_iota(jnp.int32, (1, LANES), 1)
    low_half = lax.broadcasted_iota(jnp.int32, (L, LANES), 1) < ML_DK
    ones_ll = jnp.ones((L, L), BF16)
    zeros_ll = jnp.zeros((L, L), BF16)

    def heads_row(src, r0, c0=0):
        return jnp.concatenate([src[r0 + h:r0 + h + 1, c0:c0 + L] for h in range(ML_HEADS)], axis=1)

    def block_diag_rows(x):
        zero = jnp.zeros_like(x)
        return jnp.concatenate([jnp.where(low_half, x, zero), jnp.where(low_half, zero, x)], axis=0)

    def mlstm_chunk(c):
        for d in range(2):
            cc = c if d == 0 else n_chunks - 1 - c
            r0 = cc * L if isinstance(cc, int) else pl.multiple_of(cc * L, L)
            rows = pl.ds(r0, L)
            g = g_s[rows, :]
            f_hi, f_lo = _split(_log_sigmoid(g))
            tri_c = tril_ref[...] if d == 0 else triu_ref[...]
            tri_r = triu_ref[...] if d == 0 else tril_ref[...]
            cum = _dot(tri_c, f_hi) + _dot(tri_c, f_lo)
            r = g - pltpu.roll(cum, LANES - ML_HEADS, 1)
            r1, r2, r3 = _split3(r)
            sel = sel_ref[d]
            rb = _dot(r1, sel) + _dot(r2, sel) + _dot(r3, sel)
            g_t = jnp.transpose(g)[0:N_GATES, :]
            ft_hi, ft_lo = _split(_log_sigmoid(g_t))
            tr = jnp.concatenate([tri_r, ones_ll], axis=1)
            ct = _dot(ft_hi, tr) + _dot(ft_lo, tr)
            valid = (s_idx <= t_idx) if d == 0 else (s_idx >= t_idx)
            gi0 = 2 * ML_HEADS * d
            gf0 = gi0 + ML_HEADS
            b_row = heads_row(ct, gf0)
            b_last = heads_row(ct, gf0, L)
            i_row = heads_row(g_t, gi0)
            m_prev = heads_row(m_s, d * ML_HEADS)
            a_row = b_row + m_prev
            dm = jnp.where(valid, b_row + rb, -jnp.inf)
            mt = jnp.maximum(a_row, jnp.max(dm, axis=0, keepdims=True))
            p = jnp.exp(dm - mt)
            w_inter = jnp.exp(a_row - mt)
            floor = jnp.exp(-mt)
            g_row = b_last - b_row + i_row
            g_max = jnp.concatenate(
                [jnp.broadcast_to(jnp.max(g_row[:, h * L:(h + 1) * L], axis=-1, keepdims=True), (1, L))
                 for h in range(ML_HEADS)], axis=1)
            m_new = jnp.maximum(b_last + m_prev, g_max)
            ws = jnp.exp(g_row - m_new)
            wc = jnp.exp(b_last + m_prev - m_new)
            for h in range(ML_HEADS):
                m_s[d * ML_HEADS + h:d * ML_HEADS + h + 1, :] = m_new[:, h * L:(h + 1) * L]
            for pr in range(n_pairs):
                idx = d * n_pairs + pr
                lanes = slice(pr * LANES, (pr + 1) * LANES)
                cols = slice(pr * 2 * L, (pr + 1) * 2 * L)
                k_pair = km_s[rows, lanes]
                q_bd = block_diag_rows(qm_s[rows, lanes])
                k_bd = block_diag_rows(k_pair)
                s_t = (_dot_tb(k_pair, q_bd) * p[:, cols]).astype(BF16)
                s_bd = jnp.concatenate(
                    [jnp.concatenate([s_t[:, 0:L], zeros_ll], axis=1),
                     jnp.concatenate([zeros_ll, s_t[:, L:2 * L]], axis=1)], axis=0)
                v_t = jnp.concatenate([vmt_s[cc, 2 * pr], vmt_s[cc, 2 * pr + 1]], axis=1)
                state = c_s[idx]
                tot = _dot(v_t, s_bd) + w_inter[:, cols] * _dot_tb(state.astype(BF16), q_bd)
                h_t = tot[0:ML_DV, :] / jnp.maximum(jnp.abs(tot[ML_DV:ML_DV + 1, :]), floor[:, cols])
                for e in range(2):
                    hr = slice((2 * pr + e) * ML_DV, (2 * pr + e + 1) * ML_DV)
                    hmt_s[cc, hr, :] += h_t[:, e * L:(e + 1) * L]
                wv = (v_t.astype(F32) * ws[:, cols]).astype(BF16)
                wc_pair = jnp.where(lane_row < ML_DK, wc[:, 2 * pr * L:(2 * pr + 1) * L],
                                    wc[:, (2 * pr + 1) * L:(2 * pr + 2) * L])
                c_s[idx] = wc_pair * state + _dot(wv, k_bd)

    if lat:
        def body3(c, carry):
            mlstm_chunk(c)
            return carry
        lax.fori_loop(0, n_chunks, body3, 0)
    else:
        for c in range(n_chunks):
            mlstm_chunk(c)

    if not lat:
        for d in range(2):
            for h in range(ML_HEADS):
                idx = d * n_pairs + h // 2
                hl = slice((h % 2) * ML_DK, (h % 2 + 1) * ML_DK)
                c_o[d, h] = c_s[idx, 0:ML_DV, hl]
                n_o[d, h:h + 1, :] = c_s[idx, ML_DV:ML_DV + 1, hl]
                m_o[d:d + 1, h:h + 1] = m_s[d * ML_HEADS + h:d * ML_HEADS + h + 1, 0:1]

    def phase3(i):
        r0 = i * ROW_BLOCK if isinstance(i, int) else pl.multiple_of(i * ROW_BLOCK, ROW_BLOCK)
        rows = pl.ds(r0, ROW_BLOCK)
        u = u_s[rows, :]

        def proj(c0, width):
            return _dot(u, w_ref[:, c0:c0 + width])

        hm = jnp.concatenate(
            [jnp.concatenate([jnp.transpose(hmt_s[i * chunks_per_block + j, h * ML_DV:(h + 1) * ML_DV, :])
                              for h in range(ML_HEADS)], axis=1)
             for j in range(chunks_per_block)], axis=0)
        ys = (ya_s[rows, :], _sigmoid(om_s[rows, :]) * hm, yw_s[rows, :])
        merged = None
        for b, (yb, zc) in enumerate(zip(ys, (C_ZA, C_ZM, C_ZW))):
            z = proj(zc, BRANCH_W)
            t = (yb * (z * _sigmoid(z))).astype(BF16)
            pb = _sigmoid(proj(C_GMERGE + b * D_MODEL, D_MODEL)) * _dot(t, wb_ref[b])
            merged = pb if merged is None else merged + pb
        o = _dot(merged.astype(BF16), wo_ref[...])
        hres = ALPHA * x_ref[rows, :] + gate * o
        y_ref[rows, :] = _layer_norm(hres) * lng_ref[...] + lnb_ref[...]

    if n_rows == 1:
        phase3(0)
    else:
        def body4(i, carry):
            phase3(i)
            return carry
        lax.fori_loop(0, n_rows, body4, 0)


def _const_spec(shape):
    nd = len(shape)
    return pl.BlockSpec(shape, lambda b: (0,) * nd, pipeline_mode=pl.Buffered(1))


def _layer_call(lat, layer, x, mod, weights, consts, extra):
    B, T, _ = x.shape
    S = T + PAST_LEN if lat else T
    n_chunks = T // CHUNK

    if lat:
        seq_spec = pl.BlockSpec((None, T, D_MODEL), lambda b: (b, 0, 0), pipeline_mode=pl.Buffered(1))
    else:
        seq_spec = pl.BlockSpec((None, T, D_MODEL), lambda b: (b, 0, 0))
    if lat:
        mod_spec = pl.BlockSpec((None, None, 3, D_MODEL), lambda b: (layer, b + 1, 0, 0))
    else:
        mod_spec = pl.BlockSpec((None, None, 3, D_MODEL), lambda b: (layer, 0, 0, 0))
    *vmem_consts, sink = consts
    in_specs = [seq_spec, mod_spec]
    in_specs += [_const_spec(a.shape) for a in (*weights, *vmem_consts)]
    in_specs.append(pl.BlockSpec(memory_space=pltpu.SMEM))
    args = [x, mod, *weights, *vmem_consts, sink]

    if lat:
        cos, sa, sb, cka, cva, ckw, cvw, sc, sn, sm = extra
        in_specs += [_const_spec(cos.shape)] * 3
        kv_spec = pl.BlockSpec((None, None, N_KV, PAST_LEN, HEAD_DIM), lambda b: (b, layer, 0, 0, 0))
        in_specs += [kv_spec] * 4
        in_specs += [
            pl.BlockSpec((None, None, 2, ML_HEADS, ML_DV, ML_DK), lambda b: (b, layer, 0, 0, 0, 0)),
            pl.BlockSpec((None, None, 2, ML_HEADS, ML_DK), lambda b: (b, layer, 0, 0, 0)),
            pl.BlockSpec((None, None, 2, ML_HEADS), lambda b: (b, layer, 0, 0)),
        ]
        args += [cos, sa, sb, cka, cva, ckw, cvw, sc, sn, sm]
        out_specs = seq_spec
        out_shape = jax.ShapeDtypeStruct((B, T, D_MODEL), F32)
    else:
        kv_out = pl.BlockSpec((None, N_KV, T, HEAD_DIM), lambda b: (b, 0, 0, 0))
        out_specs = [
            seq_spec, kv_out, kv_out, kv_out, kv_out,
            pl.BlockSpec((None, 2, ML_HEADS, ML_DV, ML_DK), lambda b: (b, 0, 0, 0, 0)),
            pl.BlockSpec((None, 2, ML_HEADS, ML_DK), lambda b: (b, 0, 0, 0)),
            pl.BlockSpec((None, 2, ML_HEADS), lambda b: (b, 0, 0)),
        ]
        kv_shape = jax.ShapeDtypeStruct((B, N_KV, T, HEAD_DIM), F32)
        out_shape = [
            jax.ShapeDtypeStruct((B, T, D_MODEL), F32), kv_shape, kv_shape, kv_shape, kv_shape,
            jax.ShapeDtypeStruct((B, 2, ML_HEADS, ML_DV, ML_DK), F32),
            jax.ShapeDtypeStruct((B, 2, ML_HEADS, ML_DK), F32),
            jax.ShapeDtypeStruct((B, 2, ML_HEADS), F32),
        ]

    scratch = [
        pltpu.VMEM((T, D_MODEL), BF16),
        pltpu.VMEM((N_HEADS, T, HEAD_DIM), BF16),
        pltpu.VMEM((N_HEADS, T, HEAD_DIM), BF16),
        pltpu.VMEM((N_KV, S, HEAD_DIM), BF16),
        pltpu.VMEM((S // CHUNK, N_KV, VT_ROWS, CHUNK), BF16),
        pltpu.VMEM((N_KV, T, HEAD_DIM), BF16),
        pltpu.VMEM((n_chunks, N_KV, VT_ROWS, CHUNK), BF16),
        pltpu.VMEM((T, ML_HEADS * ML_DK), BF16),
        pltpu.VMEM((T, ML_HEADS * ML_DK), BF16),
        pltpu.VMEM((n_chunks, ML_HEADS, ST_ROWS, CHUNK), BF16),
        pltpu.VMEM((T, ML_HEADS * ML_DV), F32),
        pltpu.VMEM((T, LANES), F32),
        pltpu.VMEM((T, BRANCH_W), F32),
        pltpu.VMEM((T, BRANCH_W), F32),
        pltpu.VMEM((n_chunks, ML_HEADS * ML_DV, CHUNK), F32),
        pltpu.VMEM((ML_HEADS, ST_ROWS, 2 * ML_DK), F32),
        pltpu.VMEM((2 * ML_HEADS, LANES), F32),
    ]
    if lat:
        scratch += [pltpu.VMEM((N_KV, PAST_LEN, HEAD_DIM), BF16),
                    pltpu.VMEM((N_KV, VT_ROWS, PAST_LEN), BF16)]

    return pl.pallas_call(
        functools.partial(_layer_kernel, lat, T, layer),
        grid=(B,),
        in_specs=in_specs,
        out_specs=out_specs,
        out_shape=out_shape,
        scratch_shapes=scratch,
        compiler_params=pltpu.CompilerParams(
            dimension_semantics=("arbitrary",), vmem_limit_bytes=VMEM_LIMIT_BYTES),
        name=("latent_layer" if lat else "context_layer") + str(layer),
    )(*args)


def _rope_tables(T):
    rows = T // GRID_W
    row = jnp.repeat(jnp.arange(rows, dtype=F32), GRID_W)
    col = jnp.tile(jnp.arange(GRID_W, dtype=F32), rows)
    inv = ROPE_THETA ** (-jnp.arange(0, AXIS_DIM, 2, dtype=F32) / AXIS_DIM)
    ar = row[:, None] * inv
    ac = col[:, None] * inv
    ang = jnp.concatenate([ar, ar, ac, ac], axis=-1)
    ang = jnp.concatenate([ang, ang], axis=-1)
    first_half = (jnp.arange(LANES) % AXIS_DIM) < (AXIS_DIM // 2)
    cos = jnp.cos(ang)
    sin = jnp.sin(ang)
    sa = jnp.where(first_half[None, :], -sin, 0.0)
    sb = jnp.where(first_half[None, :], 0.0, sin)
    return cos, sa, sb


def _constants(sink_logit):
    grp = np.arange(N_HEADS * HEAD_DIM) // HEAD_DIM
    bd = jnp.asarray(grp[:, None] == grp[None, :], dtype=BF16)
    ti = np.arange(CHUNK)
    tril = jnp.asarray(ti[None, :] <= ti[:, None], dtype=BF16)
    triu = jnp.asarray(ti[None, :] >= ti[:, None], dtype=BF16)
    lane = np.arange(LANES)[:, None]
    blk = (np.arange(ML_HEADS * CHUNK) // CHUNK)[None, :]
    sel = jnp.asarray(np.stack([lane == 2 * ML_HEADS * d + blk for d in range(2)]), dtype=BF16)
    eye = jnp.asarray(np.eye(HEAD_DIM), dtype=BF16)
    return bd, tril, triu, sel, eye, sink_logit.reshape(-1)


def kernel(x_prompt, x_sample, cache_attn_k, cache_attn_v, cache_win_k, cache_win_v, state_mlstm_C,
           state_mlstm_n, state_mlstm_m, c, c_ctx, w_mod, b_mod, w_in, qk_gain, sink_logit,
           mlstm_gate_bias, w_branch, w_out, ln_gain, ln_bias):
    dec_b = x_sample.shape[0]
    assert dec_b + 1 <= 8
    cond = jnp.concatenate([c_ctx[None, :], c, jnp.zeros((8 - 1 - dec_b, D_MODEL), F32)], axis=0)
    mod = _modulation(cond, w_mod, b_mod).reshape(DEPTH, 8, 3, D_MODEL)

    consts = _constants(sink_logit)
    cos, sa, sb = _rope_tables(x_sample.shape[1])
    gate_end = GATE_COL + N_GATES
    col_pad = jnp.zeros((D_MODEL, LANES - N_GATES), BF16)
    lane_pad = jnp.zeros((1, LANES - N_GATES), F32)

    xp, xs = x_prompt, x_sample
    ctx_outs = []
    for l in range(DEPTH):
        w = jnp.concatenate([w_in[l, :, :gate_end].astype(BF16), col_pad,
                             w_in[l, :, gate_end:].astype(BF16)], axis=1)
        weights = (
            w, w_branch[l].astype(BF16), w_out[l].astype(BF16),
            jnp.tile(qk_gain[l, 0:1, :], (1, N_HEADS)),
            jnp.tile(qk_gain[l, 1:2, :], (1, N_KV)),
            jnp.concatenate([mlstm_gate_bias[l].reshape(1, N_GATES), lane_pad], axis=1),
            ln_gain[l][None, :], ln_bias[l][None, :],
        )
        outs = _layer_call(False, l, xp, mod, weights, consts, None)
        xp = outs[0]
        ctx_outs.append(outs[1:])
        extra = (cos, sa, sb, cache_attn_k, cache_attn_v, cache_win_k, cache_win_v,
                 state_mlstm_C, state_mlstm_n, state_mlstm_m)
        xs = _layer_call(True, l, xs, mod, weights, consts, extra)
    stacked = [jnp.stack([ctx_outs[l][i] for l in range(DEPTH)], axis=1) for i in range(7)]
    return (xp, xs, *stacked)
```

```python
import functools

import jax
import jax.numpy as jnp
import numpy as np
from jax import lax
from jax.experimental import pallas as pl
from jax.experimental.pallas import tpu as pltpu

F32 = jnp.float32
BF16 = jnp.bfloat16

D_MODEL = 1024
DEPTH = 2
PAST_LEN = 256
GRID_W = 64
HEAD_DIM = 64
N_HEADS = 8
N_KV = 2
N_GROUP = N_HEADS // N_KV
WINDOW = 128
ML_HEADS = 4
ML_DK = 64
ML_DV = 128
CHUNK = 128
BRANCH_W = 512
ROPE_THETA = 10000.0
AXIS_DIM = HEAD_DIM // 2
LN_EPS = 1e-6
RMS_EPS = 1e-6
ALPHA = (2.0 * DEPTH) ** 0.25
LANES = 128
BF16_ROWS = 16
ROW_BLOCK = 256
CTX_SEQS_PER_STEP = 2
VMEM_LIMIT_BYTES = 60 * 1024 * 1024
VT_ROWS = HEAD_DIM + BF16_ROWS
ST_ROWS = ML_DV + BF16_ROWS

_SIZES = (512, 128, 128, 512, 256, 256, 512, 512, 128, 512, 512, 128, 128, 512, 3072)
_OFF = [int(v) for v in np.concatenate([[0], np.cumsum(_SIZES)])]
(C_QA, C_KA, C_VA, C_ZA, C_QM, C_KM, C_VM, C_OM, C_GT, C_ZM, C_QW, C_KW, C_VW, C_ZW, C_GMERGE) = _OFF[:-1]
W_COLS = _OFF[-1]
GATE_COL = 2816
N_GATES = 4 * ML_HEADS


def _dot(a, b):
    return jnp.dot(a, b, preferred_element_type=F32)


def _dot_tb(a, b):
    return lax.dot_general(a, b, (((1,), (1,)), ((), ())), preferred_element_type=F32)


def _split(a):
    hi = a.astype(BF16)
    lo = (a - hi.astype(F32)).astype(BF16)
    return hi, lo


def _split3(a):
    hi = a.astype(BF16)
    r = a - hi.astype(F32)
    mid = r.astype(BF16)
    lo = (r - mid.astype(F32)).astype(BF16)
    return hi, mid, lo


def _layer_norm(x):
    mu = jnp.mean(x, axis=-1, keepdims=True)
    xc = x - mu
    var = jnp.mean(xc * xc, axis=-1, keepdims=True)
    return xc * lax.rsqrt(var + LN_EPS)


def _log_sigmoid(x):
    return jnp.minimum(x, 0.0) - jnp.log(1.0 + jnp.exp(-jnp.abs(x)))


def _sigmoid(x):
    return jax.nn.sigmoid(x)


def _mod_kernel(c_ref, w_ref, b_ref, o_ref):
    c = c_ref[...]
    s = c * _sigmoid(c)
    s_hi, s_lo = _split(s)
    w = w_ref[...]
    w_hi, w_lo = _split(w)
    o_ref[...] = _dot(s_hi, w_hi) + _dot(s_lo, w_hi) + _dot(s_hi, w_lo) + b_ref[...]


def _modulation(cond, w_mod, b_mod):
    rows = cond.shape[0]
    tn = 1024
    return pl.pallas_call(
        _mod_kernel,
        grid=(DEPTH, 3 * D_MODEL // tn),
        in_specs=[
            pl.BlockSpec((rows, D_MODEL), lambda l, j: (0, 0)),
            pl.BlockSpec((None, D_MODEL, tn), lambda l, j: (l, 0, j)),
            pl.BlockSpec((None, 1, tn), lambda l, j: (l, 0, j)),
        ],
        out_specs=pl.BlockSpec((None, rows, tn), lambda l, j: (l, 0, j)),
        out_shape=jax.ShapeDtypeStruct((DEPTH, rows, 3 * D_MODEL), F32),
        compiler_params=pltpu.CompilerParams(dimension_semantics=("arbitrary", "arbitrary")),
        name="adaln_modulation",
    )(cond, w_mod, b_mod.reshape(DEPTH, 1, 3 * D_MODEL))


def _rope(x, cos, sa, sb):
    return x * cos + pltpu.roll(x, LANES - AXIS_DIM // 2, 1) * sa + pltpu.roll(x, AXIS_DIM // 2, 1) * sb


def _group_rms(x, bd, gain):
    hi, lo = _split(x * x)
    ms = (_dot(hi, bd) + _dot(lo, bd)) * (1.0 / HEAD_DIM)
    return x * lax.rsqrt(ms + RMS_EPS) * gain


def _ones_row_tile():
    r = lax.broadcasted_iota(jnp.int32, (BF16_ROWS, LANES), 0)
    return jnp.where(r == 0, 1.0, 0.0).astype(BF16)


def _attend_t(qs, parts, sink_row):
    scores = []
    for k, _, valid in parts:
        s = _dot_tb(k, qs)
        if valid is not None:
            s = jnp.where(valid, s, -jnp.inf)
        scores.append(s)
    mx = functools.reduce(jnp.maximum, [jnp.max(s, axis=0, keepdims=True) for s in scores])
    if sink_row is not None:
        mx = jnp.maximum(mx, sink_row)
    tot = None
    for s, (_, v_t, _) in zip(scores, parts):
        o = _dot(v_t, jnp.exp(s - mx).astype(BF16))
        tot = o if tot is None else tot + o
    den = tot[HEAD_DIM:HEAD_DIM + 1, :]
    if sink_row is not None:
        den = den + jnp.exp(sink_row - mx)
    return tot[0:HEAD_DIM, :] / den


def _layer_kernel(lat, T, layer, nseq, *refs):
    if nseq == 1:
        _sequence(lat, T, layer, *refs)
        return
    n_shared = 16
    for sq in range(nseq):
        per_seq = [refs[0].at[sq]] + list(refs[1:n_shared]) + [r.at[sq] for r in refs[n_shared:]]
        _sequence(lat, T, layer, *per_seq)


def _sequence(lat, T, layer, *refs):
    refs = list(refs)
    x_ref, mod_ref, w_ref, wb_ref, wo_ref = refs[:5]
    g0_ref, g1_ref, gb_ref, lng_ref, lnb_ref = refs[5:10]
    bd_ref, tril_ref, triu_ref, sel_ref, eye_ref, sink_ref = refs[10:16]
    pos = 16
    if lat:
        cos_ref, sa_ref, sb_ref = refs[pos:pos + 3]
        cka_ref, cva_ref, ckw_ref, cvw_ref, c0_ref, n0_ref, m0_ref = refs[pos + 3:pos + 10]
        pos += 10
        y_ref = refs[pos]
        pos += 1
    else:
        y_ref, ka_o, va_o, kw_o, vw_o, c_o, n_o, m_o = refs[pos:pos + 8]
        pos += 8
    (u_s, qa_s, qw_s, ka_s, vat_s, kw_s, vwt_s, qm_s, km_s, vmt_s, om_s, g_s,
     ya_s, yw_s, hmt_s, c_s, m_s) = refs[pos:pos + 17]
    pos += 17
    if lat:
        kwc_s, vwct_s = refs[pos:pos + 2]

    n_rows = T // ROW_BLOCK
    n_chunks = T // CHUNK
    chunks_per_block = ROW_BLOCK // CHUNK
    shift = mod_ref[0:1, :]
    scale = mod_ref[1:2, :]
    gate = mod_ref[2:3, :]
    ones_tile = _ones_row_tile()

    def phase1(i):
        R = ROW_BLOCK
        r0 = i * R if isinstance(i, int) else pl.multiple_of(i * R, R)
        rows = pl.ds(r0, R)
        x = x_ref[rows, :]
        u = (_layer_norm(x) * (1.0 + scale) + shift).astype(BF16)
        u_s[rows, :] = u

        def proj(c0, width):
            return _dot(u, w_ref[:, c0:c0 + width])

        if lat:
            cos = cos_ref[rows, :]
            sa = sa_ref[rows, :]
            sb = sb_ref[rows, :]

        def rope_wide(v):
            if not lat:
                return v
            slabs = [_rope(v[:, j:j + LANES], cos, sa, sb) for j in range(0, v.shape[1], LANES)]
            return slabs[0] if len(slabs) == 1 else jnp.concatenate(slabs, axis=1)

        qscale = HEAD_DIM ** -0.5
        qa = rope_wide(_group_rms(proj(C_QA, 512), bd_ref[...], g0_ref[...])) * qscale
        for h in range(N_HEADS):
            qa_s[h, rows, :] = qa[:, h * HEAD_DIM:(h + 1) * HEAD_DIM].astype(BF16)
        ka_n = _group_rms(proj(C_KA, LANES), bd_ref[0:LANES, 0:LANES], g1_ref[...])
        ka = rope_wide(ka_n)
        va = proj(C_VA, LANES)
        qw = rope_wide(proj(C_QW, 512)) * qscale
        for h in range(N_HEADS):
            qw_s[h, rows, :] = qw[:, h * HEAD_DIM:(h + 1) * HEAD_DIM].astype(BF16)
        kw_raw = proj(C_KW, LANES)
        kw = rope_wide(kw_raw)
        vw = proj(C_VW, LANES)
        for kv in range(N_KV):
            sl = slice(kv * HEAD_DIM, (kv + 1) * HEAD_DIM)
            ka_s[kv, rows, :] = ka[:, sl].astype(BF16)
            kw_s[kv, rows, :] = kw[:, sl].astype(BF16)
            if not lat:
                ka_o[kv, rows, :] = ka_n[:, sl]
                va_o[kv, rows, :] = va[:, sl]
                kw_o[kv, rows, :] = kw_raw[:, sl]
                vw_o[kv, rows, :] = vw[:, sl]
        qm_s[rows, :] = proj(C_QM, 256).astype(BF16)
        km_s[rows, :] = (proj(C_KM, 256) * (ML_DK ** -0.5)).astype(BF16)
        vm = proj(C_VM, 512)
        om_s[rows, :] = proj(C_OM, 512)
        g_s[rows, :] = proj(C_GT, LANES) + gb_ref[...]
        for j in range(chunks_per_block):
            cj = i * chunks_per_block + j
            cr = slice(j * CHUNK, (j + 1) * CHUNK)
            va_t = jnp.transpose(va[cr, :]).astype(BF16)
            vw_t = jnp.transpose(vw[cr, :]).astype(BF16)
            for kv in range(N_KV):
                vat_s[cj, kv, 0:HEAD_DIM, :] = va_t[kv * HEAD_DIM:(kv + 1) * HEAD_DIM, :]
                vat_s[cj, kv, HEAD_DIM:VT_ROWS, :] = ones_tile
                vwt_s[cj, kv, 0:HEAD_DIM, :] = vw_t[kv * HEAD_DIM:(kv + 1) * HEAD_DIM, :]
                vwt_s[cj, kv, HEAD_DIM:VT_ROWS, :] = ones_tile
            for h in range(ML_HEADS):
                vmt_s[cj, h, 0:ML_DV, :] = jnp.transpose(vm[cr, h * ML_DV:(h + 1) * ML_DV]).astype(BF16)
                vmt_s[cj, h, ML_DV:ST_ROWS, :] = ones_tile

    if n_rows == 1:
        phase1(0)
    else:
        def body1(i, carry):
            phase1(i)
            return carry
        lax.fori_loop(0, n_rows, body1, 0)

    if lat:
        eye = eye_ref[...]
        for kv in range(N_KV):
            ka_s[kv, T:T + PAST_LEN, :] = cka_ref[kv].astype(BF16)
            kwc_s[kv] = ckw_ref[kv].astype(BF16)
            cva_t = _dot_tb(eye, cva_ref[kv].astype(BF16)).astype(BF16)
            cvw_t = _dot_tb(eye, cvw_ref[kv].astype(BF16)).astype(BF16)
            for j in range(PAST_LEN // CHUNK):
                vat_s[n_chunks + j, kv, 0:HEAD_DIM, :] = cva_t[:, j * CHUNK:(j + 1) * CHUNK]
                vat_s[n_chunks + j, kv, HEAD_DIM:VT_ROWS, :] = ones_tile
                vwct_s[kv, 0:HEAD_DIM, j * CHUNK:(j + 1) * CHUNK] = cvw_t[:, j * CHUNK:(j + 1) * CHUNK]
                vwct_s[kv, HEAD_DIM:VT_ROWS, j * CHUNK:(j + 1) * CHUNK] = ones_tile

    def sink_row(kv, tq):
        col = lax.broadcasted_iota(jnp.int32, (1, N_GROUP * tq), 1)
        row = jnp.full((1, N_GROUP * tq), sink_ref[layer * N_HEADS + kv * N_GROUP], F32)
        for g in range(1, N_GROUP):
            row = jnp.where(col >= g * tq, sink_ref[layer * N_HEADS + kv * N_GROUP + g], row)
        return row

    def load_q(q_s, kv, q0, tq):
        return q_s[kv * N_GROUP:(kv + 1) * N_GROUP, pl.ds(q0, tq), :].reshape(N_GROUP * tq, HEAD_DIM)

    def store_heads(dst, q0, tq, kv, o_t):
        for p in range(N_GROUP // 2):
            blk = jnp.concatenate([o_t[:, (2 * p) * tq:(2 * p + 1) * tq],
                                   o_t[:, (2 * p + 1) * tq:(2 * p + 2) * tq]], axis=0)
            c0 = (kv * N_GROUP + 2 * p) * HEAD_DIM
            dst[pl.ds(q0, tq), c0:c0 + LANES] = jnp.transpose(blk)

    def values_t(ref, kv, n):
        return jnp.concatenate([ref[j, kv] for j in range(n)], axis=1)

    def attn_full(q0, tq):
        n_key_chunks = ka_s.shape[1] // CHUNK
        for kv in range(N_KV):
            o_t = _attend_t(load_q(qa_s, kv, q0, tq),
                            [(ka_s[kv], values_t(vat_s, kv, n_key_chunks), None)], None)
            store_heads(ya_s, q0, tq, kv, o_t)

    def attn_ctx_sink(q0, tq):
        for kv in range(N_KV):
            o_t = _attend_t(load_q(qw_s, kv, q0, tq),
                            [(kw_s[kv], values_t(vwt_s, kv, n_chunks), None)], sink_row(kv, tq))
            store_heads(yw_s, q0, tq, kv, o_t)

    def attn_band(n):
        tq = CHUNK
        band_chunks = 3
        band = band_chunks * CHUNK
        q0 = pl.multiple_of(n * tq, tq)
        j0 = jnp.clip(n - 1, 0, n_chunks - band_chunks)
        w0 = pl.multiple_of(j0 * CHUNK, CHUNK)
        kpos = w0 + lax.broadcasted_iota(jnp.int32, (band, N_GROUP * tq), 0)
        qpos = q0 + (lax.broadcasted_iota(jnp.int32, (band, N_GROUP * tq), 1) & (tq - 1))
        valid = jnp.abs(kpos - qpos) <= WINDOW
        for kv in range(N_KV):
            v_band = jnp.concatenate([vwt_s[j0 + j, kv] for j in range(band_chunks)], axis=1)
            parts = [(kw_s[kv, pl.ds(w0, band), :], v_band, valid),
                     (kwc_s[kv], vwct_s[kv], None)]
            o_t = _attend_t(load_q(qw_s, kv, q0, tq), parts, sink_row(kv, tq))
            store_heads(yw_s, q0, tq, kv, o_t)

    if lat:
        def body2(n, carry):
            attn_full(pl.multiple_of(n * CHUNK, CHUNK), CHUNK)
            attn_band(n)
            return carry
        lax.fori_loop(0, n_chunks, body2, 0)
    else:
        attn_full(0, T)
        attn_ctx_sink(0, T)

    n_pairs = ML_HEADS // 2
    hmt_s[...] = jnp.zeros_like(hmt_s)
    for d in range(2):
        for h in range(ML_HEADS):
            idx = d * n_pairs + h // 2
            hl = slice((h % 2) * ML_DK, (h % 2 + 1) * ML_DK)
            c_s[idx, ML_DV:ST_ROWS, hl] = jnp.zeros((BF16_ROWS, ML_DK), F32)
            if lat:
                c_s[idx, 0:ML_DV, hl] = c0_ref[d, h]
                c_s[idx, ML_DV:ML_DV + 1, hl] = n0_ref[d, h:h + 1, :]
                m_s[d * ML_HEADS + h:d * ML_HEADS + h + 1, :] = jnp.broadcast_to(
                    m0_ref[d:d + 1, h:h + 1], (1, LANES))
            else:
                c_s[idx, 0:ML_DV, hl] = jnp.zeros((ML_DV, ML_DK), F32)
                m_s[d * ML_HEADS + h:d * ML_HEADS + h + 1, :] = jnp.zeros((1, LANES), F32)

    L = CHUNK
    s_idx = lax.broadcasted_iota(jnp.int32, (L, ML_HEADS * L), 0)
    t_idx = lax.broadcasted_iota(jnp.int32, (L, ML_HEADS * L), 1) & (L - 1)
    lane_row = lax.broadcasted_iota(jnp.int32, (1, LANES), 1)
    low_half = lax.broadcasted_iota(jnp.int32, (L, LANES), 1) < ML_DK
    ones_ll = jnp.ones((L, L), BF16)
    zeros_ll = jnp.zeros((L, L), BF16)

    def heads_row(src, r0, c0=0):
        return jnp.concatenate([src[r0 + h:r0 + h + 1, c0:c0 + L] for h in range(ML_HEADS)], axis=1)

    def block_diag_rows(x):
        zero = jnp.zeros_like(x)
        return jnp.concatenate([jnp.where(low_half, x, zero), jnp.where(low_half, zero, x)], axis=0)

    def mlstm_chunk(c):
        for d in range(2):
            cc = c if d == 0 else n_chunks - 1 - c
            r0 = cc * L if isinstance(cc, int) else pl.multiple_of(cc * L, L)
            rows = pl.ds(r0, L)
            g = g_s[rows, :]
            f_hi, f_lo = _split(_log_sigmoid(g))
            tri_c = tril_ref[...] if d == 0 else triu_ref[...]
            tri_r = triu_ref[...] if d == 0 else tril_ref[...]
            cum = _dot(tri_c, f_hi) + _dot(tri_c, f_lo)
            r = g - pltpu.roll(cum, LANES - ML_HEADS, 1)
            r1, r2, r3 = _split3(r)
            sel = sel_ref[d]
            rb = _dot(r1, sel) + _dot(r2, sel) + _dot(r3, sel)
            g_t = jnp.transpose(g)[0:N_GATES, :]
            ft_hi, ft_lo = _split(_log_sigmoid(g_t))
            tr = jnp.concatenate([tri_r, ones_ll], axis=1)
            ct = _dot(ft_hi, tr) + _dot(ft_lo, tr)
            valid = (s_idx <= t_idx) if d == 0 else (s_idx >= t_idx)
            gi0 = 2 * ML_HEADS * d
            gf0 = gi0 + ML_HEADS
            b_row = heads_row(ct, gf0)
            b_last = heads_row(ct, gf0, L)
            i_row = heads_row(g_t, gi0)
            m_prev = heads_row(m_s, d * ML_HEADS)
            a_row = b_row + m_prev
            dm = jnp.where(valid, b_row + rb, -jnp.inf)
            mt = jnp.maximum(a_row, jnp.max(dm, axis=0, keepdims=True))
            p = jnp.exp(dm - mt)
            w_inter = jnp.exp(a_row - mt)
            floor = jnp.exp(-mt)
            g_row = b_last - b_row + i_row
            g_max = jnp.concatenate(
                [jnp.broadcast_to(jnp.max(g_row[:, h * L:(h + 1) * L], axis=-1, keepdims=True), (1, L))
                 for h in range(ML_HEADS)], axis=1)
            m_new = jnp.maximum(b_last + m_prev, g_max)
            ws = jnp.exp(g_row - m_new)
            wc = jnp.exp(b_last + m_prev - m_new)
            for h in range(ML_HEADS):
                m_s[d * ML_HEADS + h:d * ML_HEADS + h + 1, :] = m_new[:, h * L:(h + 1) * L]
            for pr in range(n_pairs):
                idx = d * n_pairs + pr
                lanes = slice(pr * LANES, (pr + 1) * LANES)
                cols = slice(pr * 2 * L, (pr + 1) * 2 * L)
                k_pair = km_s[rows, lanes]
                q_bd = block_diag_rows(qm_s[rows, lanes])
                k_bd = block_diag_rows(k_pair)
                s_t = (_dot_tb(k_pair, q_bd) * p[:, cols]).astype(BF16)
                s_bd = jnp.concatenate(
                    [jnp.concatenate([s_t[:, 0:L], zeros_ll], axis=1),
                     jnp.concatenate([zeros_ll, s_t[:, L:2 * L]], axis=1)], axis=0)
                v_t = jnp.concatenate([vmt_s[cc, 2 * pr], vmt_s[cc, 2 * pr + 1]], axis=1)
                state = c_s[idx]
                tot = _dot(v_t, s_bd) + w_inter[:, cols] * _dot_tb(state.astype(BF16), q_bd)
                h_t = tot[0:ML_DV, :] / jnp.maximum(jnp.abs(tot[ML_DV:ML_DV + 1, :]), floor[:, cols])
                for e in range(2):
                    hr = slice((2 * pr + e) * ML_DV, (2 * pr + e + 1) * ML_DV)
                    hmt_s[cc, hr, :] += h_t[:, e * L:(e + 1) * L]
                wv = (v_t.astype(F32) * ws[:, cols]).astype(BF16)
                wc_pair = jnp.where(lane_row < ML_DK, wc[:, 2 * pr * L:(2 * pr + 1) * L],
                                    wc[:, (2 * pr + 1) * L:(2 * pr + 2) * L])
                c_s[idx] = wc_pair * state + _dot(wv, k_bd)

    if lat:
        def body3(c, carry):
            mlstm_chunk(c)
            return carry
        lax.fori_loop(0, n_chunks, body3, 0)
    else:
        for c in range(n_chunks):
            mlstm_chunk(c)

    if not lat:
        for d in range(2):
            for h in range(ML_HEADS):
                idx = d * n_pairs + h // 2
                hl = slice((h % 2) * ML_DK, (h % 2 + 1) * ML_DK)
                c_o[d, h] = c_s[idx, 0:ML_DV, hl]
                n_o[d, h:h + 1, :] = c_s[idx, ML_DV:ML_DV + 1, hl]
                m_o[d:d + 1, h:h + 1] = m_s[d * ML_HEADS + h:d * ML_HEADS + h + 1, 0:1]

    def phase3(i):
        r0 = i * ROW_BLOCK if isinstance(i, int) else pl.multiple_of(i * ROW_BLOCK, ROW_BLOCK)
        rows = pl.ds(r0, ROW_BLOCK)
        u = u_s[rows, :]

        def proj(c0, width):
            return _dot(u, w_ref[:, c0:c0 + width])

        hm = jnp.concatenate(
            [jnp.concatenate([jnp.transpose(hmt_s[i * chunks_per_block + j, h * ML_DV:(h + 1) * ML_DV, :])
                              for h in range(ML_HEADS)], axis=1)
             for j in range(chunks_per_block)], axis=0)
        ys = (ya_s[rows, :], _sigmoid(om_s[rows, :]) * hm, yw_s[rows, :])
        merged = None
        for b, (yb, zc) in enumerate(zip(ys, (C_ZA, C_ZM, C_ZW))):
            z = proj(zc, BRANCH_W)
            t = (yb * (z * _sigmoid(z))).astype(BF16)
            pb = _sigmoid(proj(C_GMERGE + b * D_MODEL, D_MODEL)) * _dot(t, wb_ref[b])
            merged = pb if merged is None else merged + pb
        o = _dot(merged.astype(BF16), wo_ref[...])
        hres = ALPHA * x_ref[rows, :] + gate * o
        y_ref[rows, :] = _layer_norm(hres) * lng_ref[...] + lnb_ref[...]

    if n_rows == 1:
        phase3(0)
    else:
        def body4(i, carry):
            phase3(i)
            return carry
        lax.fori_loop(0, n_rows, body4, 0)


def _const_spec(shape):
    nd = len(shape)
    return pl.BlockSpec(shape, lambda b: (0,) * nd, pipeline_mode=pl.Buffered(1))


def _layer_call(lat, layer, x, mod, weights, consts, extra):
    B, T, _ = x.shape
    S = T + PAST_LEN if lat else T
    n_chunks = T // CHUNK
    nseq = 1 if lat else CTX_SEQS_PER_STEP
    lead = None if nseq == 1 else nseq

    if lat:
        seq_spec = pl.BlockSpec((None, T, D_MODEL), lambda b: (b, 0, 0), pipeline_mode=pl.Buffered(1))
    else:
        seq_spec = pl.BlockSpec((lead, T, D_MODEL), lambda b: (b, 0, 0))
    if lat:
        mod_spec = pl.BlockSpec((None, None, 3, D_MODEL), lambda b: (layer, b + 1, 0, 0))
    else:
        mod_spec = pl.BlockSpec((None, None, 3, D_MODEL), lambda b: (layer, 0, 0, 0))
    *vmem_consts, sink = consts
    in_specs = [seq_spec, mod_spec]
    in_specs += [_const_spec(a.shape) for a in (*weights, *vmem_consts)]
    in_specs.append(pl.BlockSpec(memory_space=pltpu.SMEM))
    args = [x, mod, *weights, *vmem_consts, sink]

    if lat:
        cos, sa, sb, cka, cva, ckw, cvw, sc, sn, sm = extra
        in_specs += [_const_spec(cos.shape)] * 3
        kv_spec = pl.BlockSpec((None, None, N_KV, PAST_LEN, HEAD_DIM), lambda b: (b, layer, 0, 0, 0))
        in_specs += [kv_spec] * 4
        in_specs += [
            pl.BlockSpec((None, None, 2, ML_HEADS, ML_DV, ML_DK), lambda b: (b, layer, 0, 0, 0, 0)),
            pl.BlockSpec((None, None, 2, ML_HEADS, ML_DK), lambda b: (b, layer, 0, 0, 0)),
            pl.BlockSpec((None, None, 2, ML_HEADS), lambda b: (b, layer, 0, 0)),
        ]
        args += [cos, sa, sb, cka, cva, ckw, cvw, sc, sn, sm]
        out_specs = seq_spec
        out_shape = jax.ShapeDtypeStruct((B, T, D_MODEL), F32)
    else:
        kv_out = pl.BlockSpec((lead, N_KV, T, HEAD_DIM), lambda b: (b, 0, 0, 0))
        out_specs = [
            seq_spec, kv_out, kv_out, kv_out, kv_out,
            pl.BlockSpec((lead, 2, ML_HEADS, ML_DV, ML_DK), lambda b: (b, 0, 0, 0, 0)),
            pl.BlockSpec((lead, 2, ML_HEADS, ML_DK), lambda b: (b, 0, 0, 0)),
            pl.BlockSpec((lead, 2, ML_HEADS), lambda b: (b, 0, 0)),
        ]
        kv_shape = jax.ShapeDtypeStruct((B, N_KV, T, HEAD_DIM), F32)
        out_shape = [
            jax.ShapeDtypeStruct((B, T, D_MODEL), F32), kv_shape, kv_shape, kv_shape, kv_shape,
            jax.ShapeDtypeStruct((B, 2, ML_HEADS, ML_DV, ML_DK), F32),
            jax.ShapeDtypeStruct((B, 2, ML_HEADS, ML_DK), F32),
            jax.ShapeDtypeStruct((B, 2, ML_HEADS), F32),
        ]

    scratch = [
        ((T, D_MODEL), BF16),
        ((N_HEADS, T, HEAD_DIM), BF16),
        ((N_HEADS, T, HEAD_DIM), BF16),
        ((N_KV, S, HEAD_DIM), BF16),
        ((S // CHUNK, N_KV, VT_ROWS, CHUNK), BF16),
        ((N_KV, T, HEAD_DIM), BF16),
        ((n_chunks, N_KV, VT_ROWS, CHUNK), BF16),
        ((T, ML_HEADS * ML_DK), BF16),
        ((T, ML_HEADS * ML_DK), BF16),
        ((n_chunks, ML_HEADS, ST_ROWS, CHUNK), BF16),
        ((T, ML_HEADS * ML_DV), F32),
        ((T, LANES), F32),
        ((T, BRANCH_W), F32),
        ((T, BRANCH_W), F32),
        ((n_chunks, ML_HEADS * ML_DV, CHUNK), F32),
        ((ML_HEADS, ST_ROWS, 2 * ML_DK), F32),
        ((2 * ML_HEADS, LANES), F32),
    ]
    if lat:
        scratch += [((N_KV, PAST_LEN, HEAD_DIM), BF16),
                    ((N_KV, VT_ROWS, PAST_LEN), BF16)]
    if nseq > 1:
        scratch = [((nseq,) + shape, dtype) for shape, dtype in scratch]
    scratch = [pltpu.VMEM(shape, dtype) for shape, dtype in scratch]

    return pl.pallas_call(
        functools.partial(_layer_kernel, lat, T, layer, nseq),
        grid=(B // nseq,),
        in_specs=in_specs,
        out_specs=out_specs,
        out_shape=out_shape,
        scratch_shapes=scratch,
        compiler_params=pltpu.CompilerParams(
            dimension_semantics=("arbitrary",), vmem_limit_bytes=VMEM_LIMIT_BYTES),
        name=("latent_layer" if lat else "context_layer") + str(layer),
    )(*args)


def _rope_tables(T):
    rows = T // GRID_W
    row = jnp.repeat(jnp.arange(rows, dtype=F32), GRID_W)
    col = jnp.tile(jnp.arange(GRID_W, dtype=F32), rows)
    inv = ROPE_THETA ** (-jnp.arange(0, AXIS_DIM, 2, dtype=F32) / AXIS_DIM)
    ar = row[:, None] * inv
    ac = col[:, None] * inv
    ang = jnp.concatenate([ar, ar, ac, ac], axis=-1)
    ang = jnp.concatenate([ang, ang], axis=-1)
    first_half = (jnp.arange(LANES) % AXIS_DIM) < (AXIS_DIM // 2)
    cos = jnp.cos(ang)
    sin = jnp.sin(ang)
    sa = jnp.where(first_half[None, :], -sin, 0.0)
    sb = jnp.where(first_half[None, :], 0.0, sin)
    return cos, sa, sb


def _constants(sink_logit):
    grp = np.arange(N_HEADS * HEAD_DIM) // HEAD_DIM
    bd = jnp.asarray(grp[:, None] == grp[None, :], dtype=BF16)
    ti = np.arange(CHUNK)
    tril = jnp.asarray(ti[None, :] <= ti[:, None], dtype=BF16)
    triu = jnp.asarray(ti[None, :] >= ti[:, None], dtype=BF16)
    lane = np.arange(LANES)[:, None]
    blk = (np.arange(ML_HEADS * CHUNK) // CHUNK)[None, :]
    sel = jnp.asarray(np.stack([lane == 2 * ML_HEADS * d + blk for d in range(2)]), dtype=BF16)
    eye = jnp.asarray(np.eye(HEAD_DIM), dtype=BF16)
    return bd, tril, triu, sel, eye, sink_logit.reshape(-1)


def kernel(x_prompt, x_sample, cache_attn_k, cache_attn_v, cache_win_k, cache_win_v, state_mlstm_C,
           state_mlstm_n, state_mlstm_m, c, c_ctx, w_mod, b_mod, w_in, qk_gain, sink_logit,
           mlstm_gate_bias, w_branch, w_out, ln_gain, ln_bias):
    dec_b = x_sample.shape[0]
    assert dec_b + 1 <= 8
    cond = jnp.concatenate([c_ctx[None, :], c, jnp.zeros((8 - 1 - dec_b, D_MODEL), F32)], axis=0)
    mod = _modulation(cond, w_mod, b_mod).reshape(DEPTH, 8, 3, D_MODEL)

    consts = _constants(sink_logit)
    cos, sa, sb = _rope_tables(x_sample.shape[1])
    gate_end = GATE_COL + N_GATES
    col_pad = jnp.zeros((D_MODEL, LANES - N_GATES), BF16)
    lane_pad = jnp.zeros((1, LANES - N_GATES), F32)

    xp, xs = x_prompt, x_sample
    ctx_outs = []
    for l in range(DEPTH):
        w = jnp.concatenate([w_in[l, :, :gate_end].astype(BF16), col_pad,
                             w_in[l, :, gate_end:].astype(BF16)], axis=1)
        weights = (
            w, w_branch[l].astype(BF16), w_out[l].astype(BF16),
            jnp.tile(qk_gain[l, 0:1, :], (1, N_HEADS)),
            jnp.tile(qk_gain[l, 1:2, :], (1, N_KV)),
            jnp.concatenate([mlstm_gate_bias[l].reshape(1, N_GATES), lane_pad], axis=1),
            ln_gain[l][None, :], ln_bias[l][None, :],
        )
        outs = _layer_call(False, l, xp, mod, weights, consts, None)
        xp = outs[0]
        ctx_outs.append(outs[1:])
        extra = (cos, sa, sb, cache_attn_k, cache_attn_v, cache_win_k, cache_win_v,
                 state_mlstm_C, state_mlstm_n, state_mlstm_m)
        xs = _layer_call(True, l, xs, mod, weights, consts, extra)
    stacked = [jnp.stack([ctx_outs[l][i] for l in range(DEPTH)], axis=1) for i in range(7)]
    return (xp, xs, *stacked)
```

```python
import functools

import jax
import jax.numpy as jnp
import numpy as np
from jax import lax
from jax.experimental import pallas as pl
from jax.experimental.pallas import tpu as pltpu

F32 = jnp.float32
BF16 = jnp.bfloat16

D_MODEL = 1024
DEPTH = 2
PAST_LEN = 256
GRID_W = 64
HEAD_DIM = 64
N_HEADS = 8
N_KV = 2
N_GROUP = N_HEADS // N_KV
WINDOW = 128
ML_HEADS = 4
ML_DK = 64
ML_DV = 128
CHUNK = 128
BRANCH_W = 512
ROPE_THETA = 10000.0
AXIS_DIM = HEAD_DIM // 2
LN_EPS = 1e-6
RMS_EPS = 1e-6
ALPHA = (2.0 * DEPTH) ** 0.25
LANES = 128
BF16_ROWS = 16
ROW_BLOCK = 256
CTX_SEQS_PER_STEP = 1
VMEM_LIMIT_BYTES = 60 * 1024 * 1024
VT_ROWS = HEAD_DIM + BF16_ROWS
ST_ROWS = ML_DV + BF16_ROWS

_SIZES = (512, 128, 128, 512, 256, 256, 512, 512, 128, 512, 512, 128, 128, 512, 3072)
_OFF = [int(v) for v in np.concatenate([[0], np.cumsum(_SIZES)])]
(C_QA, C_KA, C_VA, C_ZA, C_QM, C_KM, C_VM, C_OM, C_GT, C_ZM, C_QW, C_KW, C_VW, C_ZW, C_GMERGE) = _OFF[:-1]
W_COLS = _OFF[-1]
GATE_COL = 2816
N_GATES = 4 * ML_HEADS


def _dot(a, b):
    return jnp.dot(a, b, preferred_element_type=F32)


def _dot_tb(a, b):
    return lax.dot_general(a, b, (((1,), (1,)), ((), ())), preferred_element_type=F32)


def _split(a):
    hi = a.astype(BF16)
    lo = (a - hi.astype(F32)).astype(BF16)
    return hi, lo


def _split3(a):
    hi = a.astype(BF16)
    r = a - hi.astype(F32)
    mid = r.astype(BF16)
    lo = (r - mid.astype(F32)).astype(BF16)
    return hi, mid, lo


def _layer_norm(x):
    mu = jnp.mean(x, axis=-1, keepdims=True)
    xc = x - mu
    var = jnp.mean(xc * xc, axis=-1, keepdims=True)
    return xc * lax.rsqrt(var + LN_EPS)


def _log_sigmoid(x):
    return jnp.minimum(x, 0.0) - jnp.log(1.0 + jnp.exp(-jnp.abs(x)))


def _sigmoid(x):
    return jax.nn.sigmoid(x)


def _mod_kernel(c_ref, w_ref, b_ref, o_ref):
    c = c_ref[...]
    s = c * _sigmoid(c)
    s_hi, s_lo = _split(s)
    w = w_ref[...]
    w_hi, w_lo = _split(w)
    o_ref[...] = _dot(s_hi, w_hi) + _dot(s_lo, w_hi) + _dot(s_hi, w_lo) + b_ref[...]


def _modulation(cond, w_mod, b_mod):
    rows = cond.shape[0]
    tn = 1024
    return pl.pallas_call(
        _mod_kernel,
        grid=(DEPTH, 3 * D_MODEL // tn),
        in_specs=[
            pl.BlockSpec((rows, D_MODEL), lambda l, j: (0, 0)),
            pl.BlockSpec((None, D_MODEL, tn), lambda l, j: (l, 0, j)),
            pl.BlockSpec((None, 1, tn), lambda l, j: (l, 0, j)),
        ],
        out_specs=pl.BlockSpec((None, rows, tn), lambda l, j: (l, 0, j)),
        out_shape=jax.ShapeDtypeStruct((DEPTH, rows, 3 * D_MODEL), F32),
        compiler_params=pltpu.CompilerParams(dimension_semantics=("arbitrary", "arbitrary")),
        name="adaln_modulation",
    )(cond, w_mod, b_mod.reshape(DEPTH, 1, 3 * D_MODEL))


PREP_ROWS = 128


def _prep_kernel(w_ref, o_ref):
    gate_end = GATE_COL + N_GATES
    o_ref[:, 0:GATE_COL] = w_ref[:, 0:GATE_COL].astype(BF16)
    lane = lax.broadcasted_iota(jnp.int32, (PREP_ROWS, LANES), 1)
    o_ref[:, GATE_COL:GATE_COL + LANES] = jnp.where(
        lane < N_GATES, w_ref[:, GATE_COL:GATE_COL + LANES], 0.0).astype(BF16)
    o_ref[:, GATE_COL + LANES:W_COLS] = w_ref[:, gate_end:].astype(BF16)


def _prep_in_weight(w_in):
    n_in = w_in.shape[-1]
    return pl.pallas_call(
        _prep_kernel,
        grid=(DEPTH, D_MODEL // PREP_ROWS),
        in_specs=[pl.BlockSpec((None, PREP_ROWS, n_in), lambda l, i: (l, i, 0))],
        out_specs=pl.BlockSpec((None, PREP_ROWS, W_COLS), lambda l, i: (l, i, 0)),
        out_shape=jax.ShapeDtypeStruct((DEPTH, D_MODEL, W_COLS), BF16),
        compiler_params=pltpu.CompilerParams(dimension_semantics=("arbitrary", "arbitrary")),
        name="in_weight_prep",
    )(w_in)


def _rope(x, cos, sa, sb):
    return x * cos + pltpu.roll(x, LANES - AXIS_DIM // 2, 1) * sa + pltpu.roll(x, AXIS_DIM // 2, 1) * sb


def _group_rms(x, bd, gain):
    hi, lo = _split(x * x)
    ms = (_dot(hi, bd) + _dot(lo, bd)) * (1.0 / HEAD_DIM)
    return x * lax.rsqrt(ms + RMS_EPS) * gain


def _ones_row_tile():
    r = lax.broadcasted_iota(jnp.int32, (BF16_ROWS, LANES), 0)
    return jnp.where(r == 0, 1.0, 0.0).astype(BF16)


def _attend_t(qs, parts, sink_row):
    scores = []
    for k, _, valid in parts:
        s = _dot_tb(k, qs)
        if valid is not None:
            s = jnp.where(valid, s, -jnp.inf)
        scores.append(s)
    mx = functools.reduce(jnp.maximum, [jnp.max(s, axis=0, keepdims=True) for s in scores])
    if sink_row is not None:
        mx = jnp.maximum(mx, sink_row)
    tot = None
    for s, (_, v_t, _) in zip(scores, parts):
        o = _dot(v_t, jnp.exp(s - mx).astype(BF16))
        tot = o if tot is None else tot + o
    den = tot[HEAD_DIM:HEAD_DIM + 1, :]
    if sink_row is not None:
        den = den + jnp.exp(sink_row - mx)
    return tot[0:HEAD_DIM, :] / den


def _layer_kernel(lat, T, layer, nseq, *refs):
    if nseq == 1:
        _sequence(lat, T, layer, *refs)
        return
    n_shared = 16
    for sq in range(nseq):
        per_seq = [refs[0].at[sq]] + list(refs[1:n_shared]) + [r.at[sq] for r in refs[n_shared:]]
        _sequence(lat, T, layer, *per_seq)


def _sequence(lat, T, layer, *refs):
    refs = list(refs)
    x_ref, mod_ref, w_ref, wb_ref, wo_ref = refs[:5]
    g0_ref, g1_ref, gb_ref, lng_ref, lnb_ref = refs[5:10]
    bd_ref, tril_ref, triu_ref, sel_ref, eye_ref, sink_ref = refs[10:16]
    pos = 16
    if lat:
        cos_ref, sa_ref, sb_ref = refs[pos:pos + 3]
        cka_ref, cva_ref, ckw_ref, cvw_ref, c0_ref, n0_ref, m0_ref = refs[pos + 3:pos + 10]
        pos += 10
        y_ref = refs[pos]
        pos += 1
    else:
        y_ref, ka_o, va_o, kw_o, vw_o, c_o, n_o, m_o = refs[pos:pos + 8]
        pos += 8
    (u_s, qa_s, qw_s, ka_s, vat_s, kw_s, vwt_s, qm_s, km_s, vmt_s, om_s, g_s,
     ya_s, yw_s, hmt_s, c_s, m_s) = refs[pos:pos + 17]
    pos += 17
    if lat:
        kwc_s, vwct_s = refs[pos:pos + 2]

    n_rows = T // ROW_BLOCK
    n_chunks = T // CHUNK
    chunks_per_block = ROW_BLOCK // CHUNK
    shift = mod_ref[0:1, :]
    scale = mod_ref[1:2, :]
    gate = mod_ref[2:3, :]
    ones_tile = _ones_row_tile()

    def phase1(i):
        R = ROW_BLOCK
        r0 = i * R if isinstance(i, int) else pl.multiple_of(i * R, R)
        rows = pl.ds(r0, R)
        x = x_ref[rows, :]
        u = (_layer_norm(x) * (1.0 + scale) + shift).astype(BF16)
        u_s[rows, :] = u

        def proj(c0, width):
            return _dot(u, w_ref[:, c0:c0 + width])

        if lat:
            cos = cos_ref[rows, :]
            sa = sa_ref[rows, :]
            sb = sb_ref[rows, :]

        def rope_wide(v):
            if not lat:
                return v
            slabs = [_rope(v[:, j:j + LANES], cos, sa, sb) for j in range(0, v.shape[1], LANES)]
            return slabs[0] if len(slabs) == 1 else jnp.concatenate(slabs, axis=1)

        qscale = HEAD_DIM ** -0.5
        qa = rope_wide(_group_rms(proj(C_QA, 512), bd_ref[...], g0_ref[...])) * qscale
        for h in range(N_HEADS):
            qa_s[h, rows, :] = qa[:, h * HEAD_DIM:(h + 1) * HEAD_DIM].astype(BF16)
        ka_n = _group_rms(proj(C_KA, LANES), bd_ref[0:LANES, 0:LANES], g1_ref[...])
        ka = rope_wide(ka_n)
        va = proj(C_VA, LANES)
        qw = rope_wide(proj(C_QW, 512)) * qscale
        for h in range(N_HEADS):
            qw_s[h, rows, :] = qw[:, h * HEAD_DIM:(h + 1) * HEAD_DIM].astype(BF16)
        kw_raw = proj(C_KW, LANES)
        kw = rope_wide(kw_raw)
        vw = proj(C_VW, LANES)
        for kv in range(N_KV):
            sl = slice(kv * HEAD_DIM, (kv + 1) * HEAD_DIM)
            ka_s[kv, rows, :] = ka[:, sl].astype(BF16)
            kw_s[kv, rows, :] = kw[:, sl].astype(BF16)
            if not lat:
                ka_o[kv, rows, :] = ka_n[:, sl]
                va_o[kv, rows, :] = va[:, sl]
                kw_o[kv, rows, :] = kw_raw[:, sl]
                vw_o[kv, rows, :] = vw[:, sl]
        qm_s[rows, :] = proj(C_QM, 256).astype(BF16)
        km_s[rows, :] = (proj(C_KM, 256) * (ML_DK ** -0.5)).astype(BF16)
        vm = proj(C_VM, 512)
        om_s[rows, :] = proj(C_OM, 512)
        g_s[rows, :] = proj(C_GT, LANES) + gb_ref[...]
        for j in range(chunks_per_block):
            cj = i * chunks_per_block + j
            cr = slice(j * CHUNK, (j + 1) * CHUNK)
            va_t = jnp.transpose(va[cr, :]).astype(BF16)
            vw_t = jnp.transpose(vw[cr, :]).astype(BF16)
            for kv in range(N_KV):
                vat_s[cj, kv, 0:HEAD_DIM, :] = va_t[kv * HEAD_DIM:(kv + 1) * HEAD_DIM, :]
                vat_s[cj, kv, HEAD_DIM:VT_ROWS, :] = ones_tile
                vwt_s[cj, kv, 0:HEAD_DIM, :] = vw_t[kv * HEAD_DIM:(kv + 1) * HEAD_DIM, :]
                vwt_s[cj, kv, HEAD_DIM:VT_ROWS, :] = ones_tile
            for h in range(ML_HEADS):
                vmt_s[cj, h, 0:ML_DV, :] = jnp.transpose(vm[cr, h * ML_DV:(h + 1) * ML_DV]).astype(BF16)
                vmt_s[cj, h, ML_DV:ST_ROWS, :] = ones_tile

    if n_rows == 1:
        phase1(0)
    else:
        def body1(i, carry):
            phase1(i)
            return carry
        lax.fori_loop(0, n_rows, body1, 0)

    if lat:
        eye = eye_ref[...]
        for kv in range(N_KV):
            ka_s[kv, T:T + PAST_LEN, :] = cka_ref[kv].astype(BF16)
            kwc_s[kv] = ckw_ref[kv].astype(BF16)
            cva_t = _dot_tb(eye, cva_ref[kv].astype(BF16)).astype(BF16)
            cvw_t = _dot_tb(eye, cvw_ref[kv].astype(BF16)).astype(BF16)
            for j in range(PAST_LEN // CHUNK):
                vat_s[n_chunks + j, kv, 0:HEAD_DIM, :] = cva_t[:, j * CHUNK:(j + 1) * CHUNK]
                vat_s[n_chunks + j, kv, HEAD_DIM:VT_ROWS, :] = ones_tile
                vwct_s[kv, 0:HEAD_DIM, j * CHUNK:(j + 1) * CHUNK] = cvw_t[:, j * CHUNK:(j + 1) * CHUNK]
                vwct_s[kv, HEAD_DIM:VT_ROWS, j * CHUNK:(j + 1) * CHUNK] = ones_tile

    def sink_row(kv, tq):
        col = lax.broadcasted_iota(jnp.int32, (1, N_GROUP * tq), 1)
        row = jnp.full((1, N_GROUP * tq), sink_ref[layer * N_HEADS + kv * N_GROUP], F32)
        for g in range(1, N_GROUP):
            row = jnp.where(col >= g * tq, sink_ref[layer * N_HEADS + kv * N_GROUP + g], row)
        return row

    def load_q(q_s, kv, q0, tq):
        return q_s[kv * N_GROUP:(kv + 1) * N_GROUP, pl.ds(q0, tq), :].reshape(N_GROUP * tq, HEAD_DIM)

    def store_heads(dst, q0, tq, kv, o_t):
        for p in range(N_GROUP // 2):
            blk = jnp.concatenate([o_t[:, (2 * p) * tq:(2 * p + 1) * tq],
                                   o_t[:, (2 * p + 1) * tq:(2 * p + 2) * tq]], axis=0)
            c0 = (kv * N_GROUP + 2 * p) * HEAD_DIM
            dst[pl.ds(q0, tq), c0:c0 + LANES] = jnp.transpose(blk)

    def values_t(ref, kv, n):
        return jnp.concatenate([ref[j, kv] for j in range(n)], axis=1)

    def attn_full(q0, tq):
        n_key_chunks = ka_s.shape[1] // CHUNK
        for kv in range(N_KV):
            o_t = _attend_t(load_q(qa_s, kv, q0, tq),
                            [(ka_s[kv], values_t(vat_s, kv, n_key_chunks), None)], None)
            store_heads(ya_s, q0, tq, kv, o_t)

    def attn_ctx_sink(q0, tq):
        for kv in range(N_KV):
            o_t = _attend_t(load_q(qw_s, kv, q0, tq),
                            [(kw_s[kv], values_t(vwt_s, kv, n_chunks), None)], sink_row(kv, tq))
            store_heads(yw_s, q0, tq, kv, o_t)

    def attn_band(n):
        tq = CHUNK
        band_chunks = 3
        band = band_chunks * CHUNK
        q0 = pl.multiple_of(n * tq, tq)
        j0 = jnp.clip(n - 1, 0, n_chunks - band_chunks)
        w0 = pl.multiple_of(j0 * CHUNK, CHUNK)
        kpos = w0 + lax.broadcasted_iota(jnp.int32, (band, N_GROUP * tq), 0)
        qpos = q0 + (lax.broadcasted_iota(jnp.int32, (band, N_GROUP * tq), 1) & (tq - 1))
        valid = jnp.abs(kpos - qpos) <= WINDOW
        for kv in range(N_KV):
            v_band = jnp.concatenate([vwt_s[j0 + j, kv] for j in range(band_chunks)], axis=1)
            parts = [(kw_s[kv, pl.ds(w0, band), :], v_band, valid),
                     (kwc_s[kv], vwct_s[kv], None)]
            o_t = _attend_t(load_q(qw_s, kv, q0, tq), parts, sink_row(kv, tq))
            store_heads(yw_s, q0, tq, kv, o_t)

    if lat:
        def body2(n, carry):
            attn_full(pl.multiple_of(n * CHUNK, CHUNK), CHUNK)
            attn_band(n)
            return carry
        lax.fori_loop(0, n_chunks, body2, 0)
    else:
        attn_full(0, T)
        attn_ctx_sink(0, T)

    n_pairs = ML_HEADS // 2
    hmt_s[...] = jnp.zeros_like(hmt_s)
    for d in range(2):
        for h in range(ML_HEADS):
            idx = d * n_pairs + h // 2
            hl = slice((h % 2) * ML_DK, (h % 2 + 1) * ML_DK)
            c_s[idx, ML_DV:ST_ROWS, hl] = jnp.zeros((BF16_ROWS, ML_DK), F32)
            if lat:
                c_s[idx, 0:ML_DV, hl] = c0_ref[d, h]
                c_s[idx, ML_DV:ML_DV + 1, hl] = n0_ref[d, h:h + 1, :]
                m_s[d * ML_HEADS + h:d * ML_HEADS + h + 1, :] = jnp.broadcast_to(
                    m0_ref[d:d + 1, h:h + 1], (1, LANES))
            else:
                c_s[idx, 0:ML_DV, hl] = jnp.zeros((ML_DV, ML_DK), F32)
                m_s[d * ML_HEADS + h:d * ML_HEADS + h + 1, :] = jnp.zeros((1, LANES), F32)

    L = CHUNK
    s_idx = lax.broadcasted_iota(jnp.int32, (L, ML_HEADS * L), 0)
    t_idx = lax.broadcasted_iota(jnp.int32, (L, ML_HEADS * L), 1) & (L - 1)
    lane_row = lax.broadcasted_iota(jnp.int32, (1, LANES), 1)
    low_half = lax.broadcasted_iota(jnp.int32, (L, LANES), 1) < ML_DK
    ones_ll = jnp.ones((L, L), BF16)
    zeros_ll = jnp.zeros((L, L), BF16)

    def heads_row(src, r0, c0=0):
        return jnp.concatenate([src[r0 + h:r0 + h + 1, c0:c0 + L] for h in range(ML_HEADS)], axis=1)

    def block_diag_rows(x):
        zero = jnp.zeros_like(x)
        return jnp.concatenate([jnp.where(low_half, x, zero), jnp.where(low_half, zero, x)], axis=0)

    def mlstm_chunk(c):
        for d in range(2):
            cc = c if d == 0 else n_chunks - 1 - c
            r0 = cc * L if isinstance(cc, int) else pl.multiple_of(cc * L, L)
            rows = pl.ds(r0, L)
            g = g_s[rows, :]
            f_hi, f_lo = _split(_log_sigmoid(g))
            tri_c = tril_ref[...] if d == 0 else triu_ref[...]
            tri_r = triu_ref[...] if d == 0 else tril_ref[...]
            cum = _dot(tri_c, f_hi) + _dot(tri_c, f_lo)
            r = g - pltpu.roll(cum, LANES - ML_HEADS, 1)
            r1, r2, r3 = _split3(r)
            sel = sel_ref[d]
            rb = _dot(r1, sel) + _dot(r2, sel) + _dot(r3, sel)
            g_t = jnp.transpose(g)[0:N_GATES, :]
            ft_hi, ft_lo = _split(_log_sigmoid(g_t))
            tr = jnp.concatenate([tri_r, ones_ll], axis=1)
            ct = _dot(ft_hi, tr) + _dot(ft_lo, tr)
            valid = (s_idx <= t_idx) if d == 0 else (s_idx >= t_idx)
            gi0 = 2 * ML_HEADS * d
            gf0 = gi0 + ML_HEADS
            b_row = heads_row(ct, gf0)
            b_last = heads_row(ct, gf0, L)
            i_row = heads_row(g_t, gi0)
            m_prev = heads_row(m_s, d * ML_HEADS)
            a_row = b_row + m_prev
            dm = jnp.where(valid, b_row + rb, -jnp.inf)
            mt = jnp.maximum(a_row, jnp.max(dm, axis=0, keepdims=True))
            p = jnp.exp(dm - mt)
            w_inter = jnp.exp(a_row - mt)
            floor = jnp.exp(-mt)
            g_row = b_last - b_row + i_row
            g_max = jnp.concatenate(
                [jnp.broadcast_to(jnp.max(g_row[:, h * L:(h + 1) * L], axis=-1, keepdims=True), (1, L))
                 for h in range(ML_HEADS)], axis=1)
            m_new = jnp.maximum(b_last + m_prev, g_max)
            ws = jnp.exp(g_row - m_new)
            wc = jnp.exp(b_last + m_prev - m_new)
            for h in range(ML_HEADS):
                m_s[d * ML_HEADS + h:d * ML_HEADS + h + 1, :] = m_new[:, h * L:(h + 1) * L]
            for pr in range(n_pairs):
                idx = d * n_pairs + pr
                lanes = slice(pr * LANES, (pr + 1) * LANES)
                cols = slice(pr * 2 * L, (pr + 1) * 2 * L)
                k_pair = km_s[rows, lanes]
                q_bd = block_diag_rows(qm_s[rows, lanes])
                k_bd = block_diag_rows(k_pair)
                s_t = (_dot_tb(k_pair, q_bd) * p[:, cols]).astype(BF16)
                s_bd = jnp.concatenate(
                    [jnp.concatenate([s_t[:, 0:L], zeros_ll], axis=1),
                     jnp.concatenate([zeros_ll, s_t[:, L:2 * L]], axis=1)], axis=0)
                v_t = jnp.concatenate([vmt_s[cc, 2 * pr], vmt_s[cc, 2 * pr + 1]], axis=1)
                state = c_s[idx]
                tot = _dot(v_t, s_bd) + w_inter[:, cols] * _dot_tb(state.astype(BF16), q_bd)
                h_t = tot[0:ML_DV, :] / jnp.maximum(jnp.abs(tot[ML_DV:ML_DV + 1, :]), floor[:, cols])
                for e in range(2):
                    hr = slice((2 * pr + e) * ML_DV, (2 * pr + e + 1) * ML_DV)
                    hmt_s[cc, hr, :] += h_t[:, e * L:(e + 1) * L]
                wv = (v_t.astype(F32) * ws[:, cols]).astype(BF16)
                wc_pair = jnp.where(lane_row < ML_DK, wc[:, 2 * pr * L:(2 * pr + 1) * L],
                                    wc[:, (2 * pr + 1) * L:(2 * pr + 2) * L])
                c_s[idx] = wc_pair * state + _dot(wv, k_bd)

    if lat:
        def body3(c, carry):
            mlstm_chunk(c)
            return carry
        lax.fori_loop(0, n_chunks, body3, 0)
    else:
        for c in range(n_chunks):
            mlstm_chunk(c)

    if not lat:
        for d in range(2):
            for h in range(ML_HEADS):
                idx = d * n_pairs + h // 2
                hl = slice((h % 2) * ML_DK, (h % 2 + 1) * ML_DK)
                c_o[d, h] = c_s[idx, 0:ML_DV, hl]
                n_o[d, h:h + 1, :] = c_s[idx, ML_DV:ML_DV + 1, hl]
                m_o[d:d + 1, h:h + 1] = m_s[d * ML_HEADS + h:d * ML_HEADS + h + 1, 0:1]

    def phase3(i):
        r0 = i * ROW_BLOCK if isinstance(i, int) else pl.multiple_of(i * ROW_BLOCK, ROW_BLOCK)
        rows = pl.ds(r0, ROW_BLOCK)
        u = u_s[rows, :]

        def proj(c0, width):
            return _dot(u, w_ref[:, c0:c0 + width])

        hm = jnp.concatenate(
            [jnp.concatenate([jnp.transpose(hmt_s[i * chunks_per_block + j, h * ML_DV:(h + 1) * ML_DV, :])
                              for h in range(ML_HEADS)], axis=1)
             for j in range(chunks_per_block)], axis=0)
        ys = (ya_s[rows, :], _sigmoid(om_s[rows, :]) * hm, yw_s[rows, :])
        merged = None
        for b, (yb, zc) in enumerate(zip(ys, (C_ZA, C_ZM, C_ZW))):
            z = proj(zc, BRANCH_W)
            t = (yb * (z * _sigmoid(z))).astype(BF16)
            pb = _sigmoid(proj(C_GMERGE + b * D_MODEL, D_MODEL)) * _dot(t, wb_ref[b])
            merged = pb if merged is None else merged + pb
        o = _dot(merged.astype(BF16), wo_ref[...])
        hres = ALPHA * x_ref[rows, :] + gate * o
        y_ref[rows, :] = _layer_norm(hres) * lng_ref[...] + lnb_ref[...]

    if n_rows == 1:
        phase3(0)
    else:
        def body4(i, carry):
            phase3(i)
            return carry
        lax.fori_loop(0, n_rows, body4, 0)


def _const_spec(shape):
    nd = len(shape)
    return pl.BlockSpec(shape, lambda b: (0,) * nd, pipeline_mode=pl.Buffered(1))


def _layer_spec(shape, layer):
    nd = len(shape) - 1
    return pl.BlockSpec((None,) + tuple(shape[1:]), lambda b: (layer,) + (0,) * nd,
                        pipeline_mode=pl.Buffered(1))


def _layer_call(lat, layer, x, mod, weights, consts, extra):
    B, T, _ = x.shape
    S = T + PAST_LEN if lat else T
    n_chunks = T // CHUNK
    nseq = 1 if lat else CTX_SEQS_PER_STEP
    lead = None if nseq == 1 else nseq

    if lat:
        seq_spec = pl.BlockSpec((None, T, D_MODEL), lambda b: (b, 0, 0), pipeline_mode=pl.Buffered(1))
    else:
        seq_spec = pl.BlockSpec((lead, T, D_MODEL), lambda b: (b, 0, 0))
    if lat:
        mod_spec = pl.BlockSpec((None, None, 3, D_MODEL), lambda b: (layer, b + 1, 0, 0))
    else:
        mod_spec = pl.BlockSpec((None, None, 3, D_MODEL), lambda b: (layer, 0, 0, 0))
    *vmem_consts, sink = consts
    in_specs = [seq_spec, mod_spec]
    in_specs += [_layer_spec(a.shape, layer) for a in weights]
    in_specs += [_const_spec(a.shape) for a in vmem_consts]
    in_specs.append(pl.BlockSpec(memory_space=pltpu.SMEM))
    args = [x, mod, *weights, *vmem_consts, sink]

    if lat:
        cos, sa, sb, cka, cva, ckw, cvw, sc, sn, sm = extra
        in_specs += [_const_spec(cos.shape)] * 3
        kv_spec = pl.BlockSpec((None, None, N_KV, PAST_LEN, HEAD_DIM), lambda b: (b, layer, 0, 0, 0))
        in_specs += [kv_spec] * 4
        in_specs += [
            pl.BlockSpec((None, None, 2, ML_HEADS, ML_DV, ML_DK), lambda b: (b, layer, 0, 0, 0, 0)),
            pl.BlockSpec((None, None, 2, ML_HEADS, ML_DK), lambda b: (b, layer, 0, 0, 0)),
            pl.BlockSpec((None, None, 2, ML_HEADS), lambda b: (b, layer, 0, 0)),
        ]
        args += [cos, sa, sb, cka, cva, ckw, cvw, sc, sn, sm]
        out_specs = seq_spec
        out_shape = jax.ShapeDtypeStruct((B, T, D_MODEL), F32)
    else:
        kv_out = pl.BlockSpec((lead, N_KV, T, HEAD_DIM), lambda b: (b, 0, 0, 0))
        out_specs = [
            seq_spec, kv_out, kv_out, kv_out, kv_out,
            pl.BlockSpec((lead, 2, ML_HEADS, ML_DV, ML_DK), lambda b: (b, 0, 0, 0, 0)),
            pl.BlockSpec((lead, 2, ML_HEADS, ML_DK), lambda b: (b, 0, 0, 0)),
            pl.BlockSpec((lead, 2, ML_HEADS), lambda b: (b, 0, 0)),
        ]
        kv_shape = jax.ShapeDtypeStruct((B, N_KV, T, HEAD_DIM), F32)
        out_shape = [
            jax.ShapeDtypeStruct((B, T, D_MODEL), F32), kv_shape, kv_shape, kv_shape, kv_shape,
            jax.ShapeDtypeStruct((B, 2, ML_HEADS, ML_DV, ML_DK), F32),
            jax.ShapeDtypeStruct((B, 2, ML_HEADS, ML_DK), F32),
            jax.ShapeDtypeStruct((B, 2, ML_HEADS), F32),
        ]

    scratch = [
        ((T, D_MODEL), BF16),
        ((N_HEADS, T, HEAD_DIM), BF16),
        ((N_HEADS, T, HEAD_DIM), BF16),
        ((N_KV, S, HEAD_DIM), BF16),
        ((S // CHUNK, N_KV, VT_ROWS, CHUNK), BF16),
        ((N_KV, T, HEAD_DIM), BF16),
        ((n_chunks, N_KV, VT_ROWS, CHUNK), BF16),
        ((T, ML_HEADS * ML_DK), BF16),
        ((T, ML_HEADS * ML_DK), BF16),
        ((n_chunks, ML_HEADS, ST_ROWS, CHUNK), BF16),
        ((T, ML_HEADS * ML_DV), F32),
        ((T, LANES), F32),
        ((T, BRANCH_W), F32),
        ((T, BRANCH_W), F32),
        ((n_chunks, ML_HEADS * ML_DV, CHUNK), F32),
        ((ML_HEADS, ST_ROWS, 2 * ML_DK), F32),
        ((2 * ML_HEADS, LANES), F32),
    ]
    if lat:
        scratch += [((N_KV, PAST_LEN, HEAD_DIM), BF16),
                    ((N_KV, VT_ROWS, PAST_LEN), BF16)]
    if nseq > 1:
        scratch = [((nseq,) + shape, dtype) for shape, dtype in scratch]
    scratch = [pltpu.VMEM(shape, dtype) for shape, dtype in scratch]

    return pl.pallas_call(
        functools.partial(_layer_kernel, lat, T, layer, nseq),
        grid=(B // nseq,),
        in_specs=in_specs,
        out_specs=out_specs,
        out_shape=out_shape,
        scratch_shapes=scratch,
        compiler_params=pltpu.CompilerParams(
            dimension_semantics=("arbitrary",), vmem_limit_bytes=VMEM_LIMIT_BYTES),
        name=("latent_layer" if lat else "context_layer") + str(layer),
    )(*args)


def _rope_tables(T):
    rows = T // GRID_W
    row = jnp.repeat(jnp.arange(rows, dtype=F32), GRID_W)
    col = jnp.tile(jnp.arange(GRID_W, dtype=F32), rows)
    inv = ROPE_THETA ** (-jnp.arange(0, AXIS_DIM, 2, dtype=F32) / AXIS_DIM)
    ar = row[:, None] * inv
    ac = col[:, None] * inv
    ang = jnp.concatenate([ar, ar, ac, ac], axis=-1)
    ang = jnp.concatenate([ang, ang], axis=-1)
    first_half = (jnp.arange(LANES) % AXIS_DIM) < (AXIS_DIM // 2)
    cos = jnp.cos(ang)
    sin = jnp.sin(ang)
    sa = jnp.where(first_half[None, :], -sin, 0.0)
    sb = jnp.where(first_half[None, :], 0.0, sin)
    return cos, sa, sb


def _constants(sink_logit):
    grp = np.arange(N_HEADS * HEAD_DIM) // HEAD_DIM
    bd = jnp.asarray(grp[:, None] == grp[None, :], dtype=BF16)
    ti = np.arange(CHUNK)
    tril = jnp.asarray(ti[None, :] <= ti[:, None], dtype=BF16)
    triu = jnp.asarray(ti[None, :] >= ti[:, None], dtype=BF16)
    lane = np.arange(LANES)[:, None]
    blk = (np.arange(ML_HEADS * CHUNK) // CHUNK)[None, :]
    sel = jnp.asarray(np.stack([lane == 2 * ML_HEADS * d + blk for d in range(2)]), dtype=BF16)
    eye = jnp.asarray(np.eye(HEAD_DIM), dtype=BF16)
    return bd, tril, triu, sel, eye, sink_logit.reshape(-1)


def kernel(x_prompt, x_sample, cache_attn_k, cache_attn_v, cache_win_k, cache_win_v, state_mlstm_C,
           state_mlstm_n, state_mlstm_m, c, c_ctx, w_mod, b_mod, w_in, qk_gain, sink_logit,
           mlstm_gate_bias, w_branch, w_out, ln_gain, ln_bias):
    dec_b = x_sample.shape[0]
    assert dec_b + 1 <= 8
    cond = jnp.concatenate([c_ctx[None, :], c, jnp.zeros((8 - 1 - dec_b, D_MODEL), F32)], axis=0)
    mod = _modulation(cond, w_mod, b_mod).reshape(DEPTH, 8, 3, D_MODEL)

    consts = _constants(sink_logit)
    cos, sa, sb = _rope_tables(x_sample.shape[1])
    lane_pad = jnp.zeros((DEPTH, 1, LANES - N_GATES), F32)
    weights = (
        _prep_in_weight(w_in), w_branch.astype(BF16), w_out.astype(BF16),
        jnp.tile(qk_gain[:, 0:1, :], (1, 1, N_HEADS)),
        jnp.tile(qk_gain[:, 1:2, :], (1, 1, N_KV)),
        jnp.concatenate([mlstm_gate_bias.reshape(DEPTH, 1, N_GATES), lane_pad], axis=2),
        ln_gain[:, None, :], ln_bias[:, None, :],
    )

    xp, xs = x_prompt, x_sample
    ctx_outs = []
    for l in range(DEPTH):
        outs = _layer_call(False, l, xp, mod, weights, consts, None)
        xp = outs[0]
        ctx_outs.append(outs[1:])
        extra = (cos, sa, sb, cache_attn_k, cache_attn_v, cache_win_k, cache_win_v,
                 state_mlstm_C, state_mlstm_n, state_mlstm_m)
        xs = _layer_call(True, l, xs, mod, weights, consts, extra)
    stacked = [jnp.stack([ctx_outs[l][i] for l in range(DEPTH)], axis=1) for i in range(7)]
    return (xp, xs, *stacked)
```

```python
import functools

import jax
import jax.numpy as jnp
import numpy as np
from jax import lax
from jax.experimental import pallas as pl
from jax.experimental.pallas import tpu as pltpu

F32 = jnp.float32
BF16 = jnp.bfloat16

D_MODEL = 1024
DEPTH = 2
PAST_LEN = 256
GRID_W = 64
HEAD_DIM = 64
N_HEADS = 8
N_KV = 2
N_GROUP = N_HEADS // N_KV
WINDOW = 128
ML_HEADS = 4
ML_DK = 64
ML_DV = 128
CHUNK = 128
BRANCH_W = 512
ROPE_THETA = 10000.0
AXIS_DIM = HEAD_DIM // 2
LN_EPS = 1e-6
RMS_EPS = 1e-6
ALPHA = (2.0 * DEPTH) ** 0.25
LANES = 128
BF16_ROWS = 16
ROW_BLOCK = 256
VMEM_LIMIT_BYTES = 60 * 1024 * 1024
VT_ROWS = HEAD_DIM + BF16_ROWS
ST_ROWS = ML_DV + BF16_ROWS

_SIZES = (512, 128, 128, 512, 256, 256, 512, 512, 128, 512, 512, 128, 128, 512, 3072)
_OFF = [int(v) for v in np.concatenate([[0], np.cumsum(_SIZES)])]
(C_QA, C_KA, C_VA, C_ZA, C_QM, C_KM, C_VM, C_OM, C_GT, C_ZM, C_QW, C_KW, C_VW, C_ZW, C_GMERGE) = _OFF[:-1]
W_COLS = _OFF[-1]
GATE_COL = 2816
N_GATES = 4 * ML_HEADS


def _dot(a, b):
    return jnp.dot(a, b, preferred_element_type=F32)


def _dot_tb(a, b):
    return lax.dot_general(a, b, (((1,), (1,)), ((), ())), preferred_element_type=F32)


def _split(a):
    hi = a.astype(BF16)
    lo = (a - hi.astype(F32)).astype(BF16)
    return hi, lo


def _split3(a):
    hi = a.astype(BF16)
    r = a - hi.astype(F32)
    mid = r.astype(BF16)
    lo = (r - mid.astype(F32)).astype(BF16)
    return hi, mid, lo


def _layer_norm(x):
    mu = jnp.mean(x, axis=-1, keepdims=True)
    xc = x - mu
    var = jnp.mean(xc * xc, axis=-1, keepdims=True)
    return xc * lax.rsqrt(var + LN_EPS)


def _log_sigmoid(x):
    return jnp.minimum(x, 0.0) - jnp.log(1.0 + jnp.exp(-jnp.abs(x)))


def _sigmoid(x):
    return jax.nn.sigmoid(x)


def _mod_kernel(c_ref, w_ref, b_ref, o_ref):
    c = c_ref[...]
    s = c * _sigmoid(c)
    s_hi, s_lo = _split(s)
    w = w_ref[...]
    w_hi, w_lo = _split(w)
    o_ref[...] = _dot(s_hi, w_hi) + _dot(s_lo, w_hi) + _dot(s_hi, w_lo) + b_ref[...]


def _modulation(cond, w_mod, b_mod):
    rows = cond.shape[0]
    tn = 1024
    return pl.pallas_call(
        _mod_kernel,
        grid=(DEPTH, 3 * D_MODEL // tn),
        in_specs=[
            pl.BlockSpec((rows, D_MODEL), lambda l, j: (0, 0)),
            pl.BlockSpec((None, D_MODEL, tn), lambda l, j: (l, 0, j)),
            pl.BlockSpec((None, 1, tn), lambda l, j: (l, 0, j)),
        ],
        out_specs=pl.BlockSpec((None, rows, tn), lambda l, j: (l, 0, j)),
        out_shape=jax.ShapeDtypeStruct((DEPTH, rows, 3 * D_MODEL), F32),
        compiler_params=pltpu.CompilerParams(dimension_semantics=("arbitrary", "arbitrary")),
        name="adaln_modulation",
    )(cond, w_mod, b_mod.reshape(DEPTH, 1, 3 * D_MODEL))


PREP_K = 256


def _prep_kernel(wt_ref, o_ref):
    lane = lax.broadcasted_iota(jnp.int32, (PREP_K, LANES), 1)
    for c0 in range(0, W_COLS, LANES):
        r0 = c0 if c0 <= GATE_COL else c0 - (LANES - N_GATES)
        slab = jnp.transpose(wt_ref[r0:r0 + LANES, :])
        if c0 == GATE_COL:
            slab = jnp.where(lane < N_GATES, slab, 0.0)
        o_ref[:, c0:c0 + LANES] = slab.astype(BF16)


def _prep_in_weight(w_in_t):
    n_in = w_in_t.shape[1]
    return pl.pallas_call(
        _prep_kernel,
        grid=(DEPTH, D_MODEL // PREP_K),
        in_specs=[pl.BlockSpec((None, n_in, PREP_K), lambda l, i: (l, 0, i))],
        out_specs=pl.BlockSpec((None, PREP_K, W_COLS), lambda l, i: (l, i, 0)),
        out_shape=jax.ShapeDtypeStruct((DEPTH, D_MODEL, W_COLS), BF16),
        compiler_params=pltpu.CompilerParams(
            dimension_semantics=("arbitrary", "arbitrary"), vmem_limit_bytes=VMEM_LIMIT_BYTES),
        name="in_weight_prep",
    )(w_in_t)


def _rope(x, cos, sa, sb):
    return x * cos + pltpu.roll(x, LANES - AXIS_DIM // 2, 1) * sa + pltpu.roll(x, AXIS_DIM // 2, 1) * sb


def _group_rms(x, bd, gain):
    hi, lo = _split(x * x)
    ms = (_dot(hi, bd) + _dot(lo, bd)) * (1.0 / HEAD_DIM)
    return x * lax.rsqrt(ms + RMS_EPS) * gain


def _ones_row_tile():
    r = lax.broadcasted_iota(jnp.int32, (BF16_ROWS, LANES), 0)
    return jnp.where(r == 0, 1.0, 0.0).astype(BF16)


def _attend_t(qs, parts, sink_row):
    scores = []
    for k, _, valid in parts:
        s = _dot_tb(k, qs)
        if valid is not None:
            s = jnp.where(valid, s, -jnp.inf)
        scores.append(s)
    mx = functools.reduce(jnp.maximum, [jnp.max(s, axis=0, keepdims=True) for s in scores])
    if sink_row is not None:
        mx = jnp.maximum(mx, sink_row)
    tot = None
    for s, (_, v_t, _) in zip(scores, parts):
        o = _dot(v_t, jnp.exp(s - mx).astype(BF16))
        tot = o if tot is None else tot + o
    den = tot[HEAD_DIM:HEAD_DIM + 1, :]
    if sink_row is not None:
        den = den + jnp.exp(sink_row - mx)
    return tot[0:HEAD_DIM, :] / den


def _layer_kernel(lat, T, layer, *refs):
    refs = list(refs)
    x_ref, mod_ref, w_ref, wb_ref, wo_ref = refs[:5]
    g0_ref, g1_ref, gb_ref, lng_ref, lnb_ref = refs[5:10]
    bd_ref, tril_ref, triu_ref, sel_ref, eye_ref, sink_ref = refs[10:16]
    pos = 16
    if lat:
        cos_ref, sa_ref, sb_ref = refs[pos:pos + 3]
        cka_ref, cva_ref, ckw_ref, cvw_ref, c0_ref, n0_ref, m0_ref = refs[pos + 3:pos + 10]
        pos += 10
        y_ref = refs[pos]
        pos += 1
    else:
        pos += 5
        y_ref, ka_o, va_o, kw_o, vw_o, c_o, n_o, m_o = refs[pos:pos + 8]
        pos += 8
    (u_s, qa_s, qw_s, ka_s, vat_s, kw_s, vwt_s, qm_s, km_s, vmt_s, om_s, g_s,
     ya_s, yw_s, hmt_s, c_s, m_s) = refs[pos:pos + 17]
    pos += 17
    if lat:
        kwc_s, vwct_s = refs[pos:pos + 2]

    n_rows = T // ROW_BLOCK
    n_chunks = T // CHUNK
    chunks_per_block = ROW_BLOCK // CHUNK
    shift = mod_ref[0:1, :]
    scale = mod_ref[1:2, :]
    gate = mod_ref[2:3, :]
    ones_tile = _ones_row_tile()

    def phase1(i):
        R = ROW_BLOCK
        r0 = i * R if isinstance(i, int) else pl.multiple_of(i * R, R)
        rows = pl.ds(r0, R)
        x = x_ref[rows, :]
        u = (_layer_norm(x) * (1.0 + scale) + shift).astype(BF16)
        u_s[rows, :] = u

        def proj(c0, width):
            return _dot(u, w_ref[:, c0:c0 + width])

        if lat:
            cos = cos_ref[rows, :]
            sa = sa_ref[rows, :]
            sb = sb_ref[rows, :]

        def rope_wide(v):
            if not lat:
                return v
            slabs = [_rope(v[:, j:j + LANES], cos, sa, sb) for j in range(0, v.shape[1], LANES)]
            return slabs[0] if len(slabs) == 1 else jnp.concatenate(slabs, axis=1)

        qscale = HEAD_DIM ** -0.5
        qa = rope_wide(_group_rms(proj(C_QA, 512), bd_ref[...], g0_ref[...])) * qscale
        for h in range(N_HEADS):
            qa_s[h, rows, :] = qa[:, h * HEAD_DIM:(h + 1) * HEAD_DIM].astype(BF16)
        ka_n = _group_rms(proj(C_KA, LANES), bd_ref[0:LANES, 0:LANES], g1_ref[...])
        ka = rope_wide(ka_n)
        va = proj(C_VA, LANES)
        qw = rope_wide(proj(C_QW, 512)) * qscale
        for h in range(N_HEADS):
            qw_s[h, rows, :] = qw[:, h * HEAD_DIM:(h + 1) * HEAD_DIM].astype(BF16)
        kw_raw = proj(C_KW, LANES)
        kw = rope_wide(kw_raw)
        vw = proj(C_VW, LANES)
        for kv in range(N_KV):
            sl = slice(kv * HEAD_DIM, (kv + 1) * HEAD_DIM)
            ka_s[kv, rows, :] = ka[:, sl].astype(BF16)
            kw_s[kv, rows, :] = kw[:, sl].astype(BF16)
        qm_s[rows, :] = proj(C_QM, 256).astype(BF16)
        km_s[rows, :] = (proj(C_KM, 256) * (ML_DK ** -0.5)).astype(BF16)
        vm = proj(C_VM, 512)
        om_s[rows, :] = proj(C_OM, 512)
        g_s[rows, :] = proj(C_GT, LANES) + gb_ref[...]
        for j in range(chunks_per_block):
            cj = i * chunks_per_block + j
            cr = slice(j * CHUNK, (j + 1) * CHUNK)
            va_t = jnp.transpose(va[cr, :])
            vw_t = jnp.transpose(vw[cr, :])
            for kv in range(N_KV):
                hd = slice(kv * HEAD_DIM, (kv + 1) * HEAD_DIM)
                vat_s[cj, kv, 0:HEAD_DIM, :] = va_t[hd, :].astype(BF16)
                vat_s[cj, kv, HEAD_DIM:VT_ROWS, :] = ones_tile
                vwt_s[cj, kv, 0:HEAD_DIM, :] = vw_t[hd, :].astype(BF16)
                vwt_s[cj, kv, HEAD_DIM:VT_ROWS, :] = ones_tile
            if not lat:
                tc = slice(i * ROW_BLOCK + j * CHUNK, i * ROW_BLOCK + (j + 1) * CHUNK)
                ka_t = jnp.transpose(ka_n[cr, :])
                kw_t = jnp.transpose(kw_raw[cr, :])
                for kv in range(N_KV):
                    hd = slice(kv * HEAD_DIM, (kv + 1) * HEAD_DIM)
                    ka_o[kv, :, tc] = ka_t[hd, :]
                    va_o[kv, :, tc] = va_t[hd, :]
                    kw_o[kv, :, tc] = kw_t[hd, :]
                    vw_o[kv, :, tc] = vw_t[hd, :]
            for h in range(ML_HEADS):
                vmt_s[cj, h, 0:ML_DV, :] = jnp.transpose(vm[cr, h * ML_DV:(h + 1) * ML_DV]).astype(BF16)
                vmt_s[cj, h, ML_DV:ST_ROWS, :] = ones_tile

    if n_rows == 1:
        phase1(0)
    else:
        def body1(i, carry):
            phase1(i)
            return carry
        lax.fori_loop(0, n_rows, body1, 0)

    if lat:
        eye = eye_ref[...]
        for kv in range(N_KV):
            ka_s[kv, T:T + PAST_LEN, :] = cka_ref[kv].astype(BF16)
            kwc_s[kv] = ckw_ref[kv].astype(BF16)
            cva_t = _dot_tb(eye, cva_ref[kv].astype(BF16)).astype(BF16)
            cvw_t = _dot_tb(eye, cvw_ref[kv].astype(BF16)).astype(BF16)
            for j in range(PAST_LEN // CHUNK):
                vat_s[n_chunks + j, kv, 0:HEAD_DIM, :] = cva_t[:, j * CHUNK:(j + 1) * CHUNK]
                vat_s[n_chunks + j, kv, HEAD_DIM:VT_ROWS, :] = ones_tile
                vwct_s[kv, 0:HEAD_DIM, j * CHUNK:(j + 1) * CHUNK] = cvw_t[:, j * CHUNK:(j + 1) * CHUNK]
                vwct_s[kv, HEAD_DIM:VT_ROWS, j * CHUNK:(j + 1) * CHUNK] = ones_tile

    def sink_row(kv, tq):
        col = lax.broadcasted_iota(jnp.int32, (1, N_GROUP * tq), 1)
        row = jnp.full((1, N_GROUP * tq), sink_ref[layer * N_HEADS + kv * N_GROUP], F32)
        for g in range(1, N_GROUP):
            row = jnp.where(col >= g * tq, sink_ref[layer * N_HEADS + kv * N_GROUP + g], row)
        return row

    def load_q(q_s, kv, q0, tq):
        return q_s[kv * N_GROUP:(kv + 1) * N_GROUP, pl.ds(q0, tq), :].reshape(N_GROUP * tq, HEAD_DIM)

    def store_heads(dst, q0, tq, kv, o_t):
        for p in range(N_GROUP // 2):
            blk = jnp.concatenate([o_t[:, (2 * p) * tq:(2 * p + 1) * tq],
                                   o_t[:, (2 * p + 1) * tq:(2 * p + 2) * tq]], axis=0)
            c0 = (kv * N_GROUP + 2 * p) * HEAD_DIM
            dst[pl.ds(q0, tq), c0:c0 + LANES] = jnp.transpose(blk)

    def values_t(ref, kv, n):
        return jnp.concatenate([ref[j, kv] for j in range(n)], axis=1)

    def attn_full(q0, tq):
        n_key_chunks = ka_s.shape[1] // CHUNK
        for kv in range(N_KV):
            o_t = _attend_t(load_q(qa_s, kv, q0, tq),
                            [(ka_s[kv], values_t(vat_s, kv, n_key_chunks), None)], None)
            store_heads(ya_s, q0, tq, kv, o_t)

    def attn_ctx_sink(q0, tq):
        for kv in range(N_KV):
            o_t = _attend_t(load_q(qw_s, kv, q0, tq),
                            [(kw_s[kv], values_t(vwt_s, kv, n_chunks), None)], sink_row(kv, tq))
            store_heads(yw_s, q0, tq, kv, o_t)

    def attn_band(n):
        tq = CHUNK
        band_chunks = 3
        band = band_chunks * CHUNK
        q0 = pl.multiple_of(n * tq, tq)
        j0 = jnp.clip(n - 1, 0, n_chunks - band_chunks)
        w0 = pl.multiple_of(j0 * CHUNK, CHUNK)
        kpos = w0 + lax.broadcasted_iota(jnp.int32, (band, N_GROUP * tq), 0)
        qpos = q0 + (lax.broadcasted_iota(jnp.int32, (band, N_GROUP * tq), 1) & (tq - 1))
        valid = jnp.abs(kpos - qpos) <= WINDOW
        for kv in range(N_KV):
            v_band = jnp.concatenate([vwt_s[j0 + j, kv] for j in range(band_chunks)], axis=1)
            parts = [(kw_s[kv, pl.ds(w0, band), :], v_band, valid),
                     (kwc_s[kv], vwct_s[kv], None)]
            o_t = _attend_t(load_q(qw_s, kv, q0, tq), parts, sink_row(kv, tq))
            store_heads(yw_s, q0, tq, kv, o_t)

    if lat:
        def body2(n, carry):
            attn_full(pl.multiple_of(n * CHUNK, CHUNK), CHUNK)
            attn_band(n)
            return carry
        lax.fori_loop(0, n_chunks, body2, 0)
    else:
        attn_full(0, T)
        attn_ctx_sink(0, T)

    n_pairs = ML_HEADS // 2
    hmt_s[...] = jnp.zeros_like(hmt_s)
    for d in range(2):
        for h in range(ML_HEADS):
            idx = d * n_pairs + h // 2
            hl = slice((h % 2) * ML_DK, (h % 2 + 1) * ML_DK)
            c_s[idx, ML_DV:ST_ROWS, hl] = jnp.zeros((BF16_ROWS, ML_DK), F32)
            if lat:
                c_s[idx, 0:ML_DV, hl] = c0_ref[d, h]
                c_s[idx, ML_DV:ML_DV + 1, hl] = n0_ref[d, h:h + 1, :]
                m_s[d * ML_HEADS + h:d * ML_HEADS + h + 1, :] = jnp.broadcast_to(
                    m0_ref[d:d + 1, h:h + 1], (1, LANES))
            else:
                c_s[idx, 0:ML_DV, hl] = jnp.zeros((ML_DV, ML_DK), F32)
                m_s[d * ML_HEADS + h:d * ML_HEADS + h + 1, :] = jnp.zeros((1, LANES), F32)

    L = CHUNK
    s_idx = lax.broadcasted_iota(jnp.int32, (L, ML_HEADS * L), 0)
    t_idx = lax.broadcasted_iota(jnp.int32, (L, ML_HEADS * L), 1) & (L - 1)
    lane_row = lax.broadcasted_iota(jnp.int32, (1, LANES), 1)
    low_half = lax.broadcasted_iota(jnp.int32, (L, LANES), 1) < ML_DK
    ones_ll = jnp.ones((L, L), BF16)
    zeros_ll = jnp.zeros((L, L), BF16)

    def heads_row(src, r0, c0=0):
        return jnp.concatenate([src[r0 + h:r0 + h + 1, c0:c0 + L] for h in range(ML_HEADS)], axis=1)

    def block_diag_rows(x):
        zero = jnp.zeros_like(x)
        return jnp.concatenate([jnp.where(low_half, x, zero), jnp.where(low_half, zero, x)], axis=0)

    def mlstm_chunk(c):
        for d in range(2):
            cc = c if d == 0 else n_chunks - 1 - c
            r0 = cc * L if isinstance(cc, int) else pl.multiple_of(cc * L, L)
            rows = pl.ds(r0, L)
            g = g_s[rows, :]
            f_hi, f_lo = _split(_log_sigmoid(g))
            tri_c = tril_ref[...] if d == 0 else triu_ref[...]
            tri_r = triu_ref[...] if d == 0 else tril_ref[...]
            cum = _dot(tri_c, f_hi) + _dot(tri_c, f_lo)
            r = g - pltpu.roll(cum, LANES - ML_HEADS, 1)
            r1, r2, r3 = _split3(r)
            sel = sel_ref[d]
            rb = _dot(r1, sel) + _dot(r2, sel) + _dot(r3, sel)
            g_t = jnp.transpose(g)[0:N_GATES, :]
            ft_hi, ft_lo = _split(_log_sigmoid(g_t))
            tr = jnp.concatenate([tri_r, ones_ll], axis=1)
            ct = _dot(ft_hi, tr) + _dot(ft_lo, tr)
            valid = (s_idx <= t_idx) if d == 0 else (s_idx >= t_idx)
            gi0 = 2 * ML_HEADS * d
            gf0 = gi0 + ML_HEADS
            b_row = heads_row(ct, gf0)
            b_last = heads_row(ct, gf0, L)
            i_row = heads_row(g_t, gi0)
            m_prev = heads_row(m_s, d * ML_HEADS)
            a_row = b_row + m_prev
            dm = jnp.where(valid, b_row + rb, -jnp.inf)
            mt = jnp.maximum(a_row, jnp.max(dm, axis=0, keepdims=True))
            p = jnp.exp(dm - mt)
            w_inter = jnp.exp(a_row - mt)
            floor = jnp.exp(-mt)
            g_row = b_last - b_row + i_row
            g_max = jnp.concatenate(
                [jnp.broadcast_to(jnp.max(g_row[:, h * L:(h + 1) * L], axis=-1, keepdims=True), (1, L))
                 for h in range(ML_HEADS)], axis=1)
            m_new = jnp.maximum(b_last + m_prev, g_max)
            ws = jnp.exp(g_row - m_new)
            wc = jnp.exp(b_last + m_prev - m_new)
            for h in range(ML_HEADS):
                m_s[d * ML_HEADS + h:d * ML_HEADS + h + 1, :] = m_new[:, h * L:(h + 1) * L]
            for pr in range(n_pairs):
                idx = d * n_pairs + pr
                lanes = slice(pr * LANES, (pr + 1) * LANES)
                cols = slice(pr * 2 * L, (pr + 1) * 2 * L)
                k_pair = km_s[rows, lanes]
                q_bd = block_diag_rows(qm_s[rows, lanes])
                k_bd = block_diag_rows(k_pair)
                s_t = (_dot_tb(k_pair, q_bd) * p[:, cols]).astype(BF16)
                s_bd = jnp.concatenate(
                    [jnp.concatenate([s_t[:, 0:L], zeros_ll], axis=1),
                     jnp.concatenate([zeros_ll, s_t[:, L:2 * L]], axis=1)], axis=0)
                v_t = jnp.concatenate([vmt_s[cc, 2 * pr], vmt_s[cc, 2 * pr + 1]], axis=1)
                state = c_s[idx]
                tot = _dot(v_t, s_bd) + w_inter[:, cols] * _dot_tb(state.astype(BF16), q_bd)
                h_t = tot[0:ML_DV, :] / jnp.maximum(jnp.abs(tot[ML_DV:ML_DV + 1, :]), floor[:, cols])
                for e in range(2):
                    hr = slice((2 * pr + e) * ML_DV, (2 * pr + e + 1) * ML_DV)
                    hmt_s[cc, hr, :] += h_t[:, e * L:(e + 1) * L]
                wv = (v_t.astype(F32) * ws[:, cols]).astype(BF16)
                wc_pair = jnp.where(lane_row < ML_DK, wc[:, 2 * pr * L:(2 * pr + 1) * L],
                                    wc[:, (2 * pr + 1) * L:(2 * pr + 2) * L])
                c_s[idx] = wc_pair * state + _dot(wv, k_bd)

    if lat:
        def body3(c, carry):
            mlstm_chunk(c)
            return carry
        lax.fori_loop(0, n_chunks, body3, 0)
    else:
        for c in range(n_chunks):
            mlstm_chunk(c)

    if not lat:
        for d in range(2):
            for pr in range(n_pairs):
                c_t = jnp.transpose(c_s[d * n_pairs + pr, 0:ML_DV, :])
                for e in range(2):
                    c_o[d, 2 * pr + e] = c_t[e * ML_DK:(e + 1) * ML_DK, :]
            for h in range(ML_HEADS):
                idx = d * n_pairs + h // 2
                hl = slice((h % 2) * ML_DK, (h % 2 + 1) * ML_DK)
                n_o[d, h:h + 1, :] = c_s[idx, ML_DV:ML_DV + 1, hl]
                m_o[d:d + 1, h:h + 1] = m_s[d * ML_HEADS + h:d * ML_HEADS + h + 1, 0:1]

    def phase3(i):
        r0 = i * ROW_BLOCK if isinstance(i, int) else pl.multiple_of(i * ROW_BLOCK, ROW_BLOCK)
        rows = pl.ds(r0, ROW_BLOCK)
        u = u_s[rows, :]

        def proj(c0, width):
            return _dot(u, w_ref[:, c0:c0 + width])

        hm = jnp.concatenate(
            [jnp.concatenate([jnp.transpose(hmt_s[i * chunks_per_block + j, h * ML_DV:(h + 1) * ML_DV, :])
                              for h in range(ML_HEADS)], axis=1)
             for j in range(chunks_per_block)], axis=0)
        ys = (ya_s[rows, :], _sigmoid(om_s[rows, :]) * hm, yw_s[rows, :])
        merged = None
        for b, (yb, zc) in enumerate(zip(ys, (C_ZA, C_ZM, C_ZW))):
            z = proj(zc, BRANCH_W)
            t = (yb * (z * _sigmoid(z))).astype(BF16)
            pb = _sigmoid(proj(C_GMERGE + b * D_MODEL, D_MODEL)) * _dot(t, wb_ref[b])
            merged = pb if merged is None else merged + pb
        o = _dot(merged.astype(BF16), wo_ref[...])
        hres = ALPHA * x_ref[rows, :] + gate * o
        y_ref[rows, :] = _layer_norm(hres) * lng_ref[...] + lnb_ref[...]

    if n_rows == 1:
        phase3(0)
    else:
        def body4(i, carry):
            phase3(i)
            return carry
        lax.fori_loop(0, n_rows, body4, 0)


def _const_spec(shape):
    nd = len(shape)
    return pl.BlockSpec(shape, lambda b: (0,) * nd, pipeline_mode=pl.Buffered(1))


def _layer_spec(shape, layer):
    nd = len(shape) - 1
    return pl.BlockSpec((None,) + tuple(shape[1:]), lambda b: (layer,) + (0,) * nd,
                        pipeline_mode=pl.Buffered(1))


def _layer_call(lat, layer, x, mod, weights, consts, extra):
    B, T, _ = x.shape
    S = T + PAST_LEN if lat else T
    n_chunks = T // CHUNK

    if lat:
        seq_spec = pl.BlockSpec((None, T, D_MODEL), lambda b: (b, 0, 0), pipeline_mode=pl.Buffered(1))
    else:
        seq_spec = pl.BlockSpec((None, T, D_MODEL), lambda b: (b, 0, 0))
    if lat:
        mod_spec = pl.BlockSpec((None, None, 3, D_MODEL), lambda b: (layer, b + 1, 0, 0))
    else:
        mod_spec = pl.BlockSpec((None, None, 3, D_MODEL), lambda b: (layer, 0, 0, 0))
    *vmem_consts, sink = consts
    in_specs = [seq_spec, mod_spec]
    in_specs += [_layer_spec(a.shape, layer) for a in weights]
    in_specs += [_const_spec(a.shape) for a in vmem_consts]
    in_specs.append(pl.BlockSpec(memory_space=pltpu.SMEM))
    args = [x, mod, *weights, *vmem_consts, sink]

    if lat:
        cos, sa, sb, cka, cva, ckw, cvw, sc, sn, sm = extra
        in_specs += [_const_spec(cos.shape)] * 3
        kv_spec = pl.BlockSpec((None, None, N_KV, PAST_LEN, HEAD_DIM), lambda b: (b, layer, 0, 0, 0))
        in_specs += [kv_spec] * 4
        in_specs += [
            pl.BlockSpec((None, None, 2, ML_HEADS, ML_DV, ML_DK), lambda b: (b, layer, 0, 0, 0, 0)),
            pl.BlockSpec((None, None, 2, ML_HEADS, ML_DK), lambda b: (b, layer, 0, 0, 0)),
            pl.BlockSpec((None, None, 2, ML_HEADS), lambda b: (b, layer, 0, 0)),
        ]
        args += [cos, sa, sb, cka, cva, ckw, cvw, sc, sn, sm]
        out_specs = seq_spec
        out_shape = jax.ShapeDtypeStruct((B, T, D_MODEL), F32)
        aliases = {}
    else:
        first_acc = len(args)
        in_specs += [pl.BlockSpec(memory_space=pl.ANY)] * len(extra)
        args += list(extra)
        aliases = {first_acc + i: 1 + i for i in range(len(extra))}
        kv_out = pl.BlockSpec((None, None, N_KV, HEAD_DIM, T), lambda b: (b, layer, 0, 0, 0))
        out_specs = [
            seq_spec, kv_out, kv_out, kv_out, kv_out,
            pl.BlockSpec((None, None, 2, ML_HEADS, ML_DK, ML_DV), lambda b: (b, layer, 0, 0, 0, 0)),
            pl.BlockSpec((None, 2, ML_HEADS, ML_DK), lambda b: (b, 0, 0, 0)),
            pl.BlockSpec((None, 2, ML_HEADS), lambda b: (b, 0, 0)),
        ]
        out_shape = [jax.ShapeDtypeStruct((B, T, D_MODEL), F32)]
        out_shape += [jax.ShapeDtypeStruct(a.shape, a.dtype) for a in extra]
        out_shape += [
            jax.ShapeDtypeStruct((B, 2, ML_HEADS, ML_DK), F32),
            jax.ShapeDtypeStruct((B, 2, ML_HEADS), F32),
        ]

    scratch = [
        ((T, D_MODEL), BF16),
        ((N_HEADS, T, HEAD_DIM), BF16),
        ((N_HEADS, T, HEAD_DIM), BF16),
        ((N_KV, S, HEAD_DIM), BF16),
        ((S // CHUNK, N_KV, VT_ROWS, CHUNK), BF16),
        ((N_KV, T, HEAD_DIM), BF16),
        ((n_chunks, N_KV, VT_ROWS, CHUNK), BF16),
        ((T, ML_HEADS * ML_DK), BF16),
        ((T, ML_HEADS * ML_DK), BF16),
        ((n_chunks, ML_HEADS, ST_ROWS, CHUNK), BF16),
        ((T, ML_HEADS * ML_DV), F32),
        ((T, LANES), F32),
        ((T, BRANCH_W), F32),
        ((T, BRANCH_W), F32),
        ((n_chunks, ML_HEADS * ML_DV, CHUNK), F32),
        ((ML_HEADS, ST_ROWS, 2 * ML_DK), F32),
        ((2 * ML_HEADS, LANES), F32),
    ]
    if lat:
        scratch += [((N_KV, PAST_LEN, HEAD_DIM), BF16),
                    ((N_KV, VT_ROWS, PAST_LEN), BF16)]
    scratch = [pltpu.VMEM(shape, dtype) for shape, dtype in scratch]

    return pl.pallas_call(
        functools.partial(_layer_kernel, lat, T, layer),
        grid=(B,),
        in_specs=in_specs,
        out_specs=out_specs,
        out_shape=out_shape,
        scratch_shapes=scratch,
        input_output_aliases=aliases,
        compiler_params=pltpu.CompilerParams(
            dimension_semantics=("arbitrary",), vmem_limit_bytes=VMEM_LIMIT_BYTES),
        name=("latent_layer" if lat else "context_layer") + str(layer),
    )(*args)


def _rope_tables(T):
    rows = T // GRID_W
    row = jnp.repeat(jnp.arange(rows, dtype=F32), GRID_W)
    col = jnp.tile(jnp.arange(GRID_W, dtype=F32), rows)
    inv = ROPE_THETA ** (-jnp.arange(0, AXIS_DIM, 2, dtype=F32) / AXIS_DIM)
    ar = row[:, None] * inv
    ac = col[:, None] * inv
    ang = jnp.concatenate([ar, ar, ac, ac], axis=-1)
    ang = jnp.concatenate([ang, ang], axis=-1)
    first_half = (jnp.arange(LANES) % AXIS_DIM) < (AXIS_DIM // 2)
    cos = jnp.cos(ang)
    sin = jnp.sin(ang)
    sa = jnp.where(first_half[None, :], -sin, 0.0)
    sb = jnp.where(first_half[None, :], 0.0, sin)
    return cos, sa, sb


def _constants(sink_logit):
    grp = np.arange(N_HEADS * HEAD_DIM) // HEAD_DIM
    bd = jnp.asarray(grp[:, None] == grp[None, :], dtype=BF16)
    ti = np.arange(CHUNK)
    tril = jnp.asarray(ti[None, :] <= ti[:, None], dtype=BF16)
    triu = jnp.asarray(ti[None, :] >= ti[:, None], dtype=BF16)
    lane = np.arange(LANES)[:, None]
    blk = (np.arange(ML_HEADS * CHUNK) // CHUNK)[None, :]
    sel = jnp.asarray(np.stack([lane == 2 * ML_HEADS * d + blk for d in range(2)]), dtype=BF16)
    eye = jnp.asarray(np.eye(HEAD_DIM), dtype=BF16)
    return bd, tril, triu, sel, eye, sink_logit.reshape(-1)


def kernel(x_prompt, x_sample, cache_attn_k, cache_attn_v, cache_win_k, cache_win_v, state_mlstm_C,
           state_mlstm_n, state_mlstm_m, c, c_ctx, w_mod, b_mod, w_in, qk_gain, sink_logit,
           mlstm_gate_bias, w_branch, w_out, ln_gain, ln_bias):
    dec_b = x_sample.shape[0]
    assert dec_b + 1 <= 8
    cond = jnp.concatenate([c_ctx[None, :], c, jnp.zeros((8 - 1 - dec_b, D_MODEL), F32)], axis=0)
    mod = _modulation(cond, w_mod, b_mod).reshape(DEPTH, 8, 3, D_MODEL)

    consts = _constants(sink_logit)
    cos, sa, sb = _rope_tables(x_sample.shape[1])
    lane_pad = jnp.zeros((DEPTH, 1, LANES - N_GATES), F32)
    weights = (
        _prep_in_weight(jnp.swapaxes(w_in, 1, 2)), w_branch.astype(BF16), w_out.astype(BF16),
        jnp.tile(qk_gain[:, 0:1, :], (1, 1, N_HEADS)),
        jnp.tile(qk_gain[:, 1:2, :], (1, 1, N_KV)),
        jnp.concatenate([mlstm_gate_bias.reshape(DEPTH, 1, N_GATES), lane_pad], axis=2),
        ln_gain[:, None, :], ln_bias[:, None, :],
    )

    B, T, _ = x_prompt.shape
    kv_t = jnp.zeros((B, DEPTH, N_KV, HEAD_DIM, T), F32)
    ctx = (kv_t, kv_t, kv_t, kv_t, jnp.zeros((B, DEPTH, 2, ML_HEADS, ML_DK, ML_DV), F32))
    xp, xs = x_prompt, x_sample
    small = []
    for l in range(DEPTH):
        outs = _layer_call(False, l, xp, mod, weights, consts, ctx)
        xp = outs[0]
        ctx = tuple(outs[1:6])
        small.append(outs[6:])
        extra = (cos, sa, sb, cache_attn_k, cache_attn_v, cache_win_k, cache_win_v,
                 state_mlstm_C, state_mlstm_n, state_mlstm_m)
        xs = _layer_call(True, l, xs, mod, weights, consts, extra)
    new_n, new_m = [jnp.stack([small[l][i] for l in range(DEPTH)], axis=1) for i in range(2)]
    return (xp, xs, *[jnp.swapaxes(a, -1, -2) for a in ctx], new_n, new_m)
```

```python
import functools

import jax
import jax.numpy as jnp
import numpy as np
from jax import lax
from jax.experimental import pallas as pl
from jax.experimental.pallas import tpu as pltpu

F32 = jnp.float32
BF16 = jnp.bfloat16

D_MODEL = 1024
DEPTH = 2
PAST_LEN = 256
GRID_W = 64
HEAD_DIM = 64
N_HEADS = 8
N_KV = 2
N_GROUP = N_HEADS // N_KV
WINDOW = 128
ML_HEADS = 4
ML_DK = 64
ML_DV = 128
CHUNK = 128
BRANCH_W = 512
ROPE_THETA = 10000.0
AXIS_DIM = HEAD_DIM // 2
LN_EPS = 1e-6
RMS_EPS = 1e-6
ALPHA = (2.0 * DEPTH) ** 0.25
LOG2E = float(np.log2(np.e))
LANES = 128
BF16_ROWS = 16
ROW_BLOCK = 256
VMEM_LIMIT_BYTES = 60 * 1024 * 1024
VT_ROWS = HEAD_DIM + BF16_ROWS
ST_ROWS = ML_DV + BF16_ROWS

_SIZES = (512, 128, 128, 512, 256, 256, 512, 512, 128, 512, 512, 128, 128, 512, 3072)
_OFF = [int(v) for v in np.concatenate([[0], np.cumsum(_SIZES)])]
(C_QA, C_KA, C_VA, C_ZA, C_QM, C_KM, C_VM, C_OM, C_GT, C_ZM, C_QW, C_KW, C_VW, C_ZW, C_GMERGE) = _OFF[:-1]
W_COLS = _OFF[-1]
GATE_COL = 2816
N_GATES = 4 * ML_HEADS


def _dot(a, b):
    return jnp.dot(a, b, preferred_element_type=F32)


def _dot_tb(a, b):
    return lax.dot_general(a, b, (((1,), (1,)), ((), ())), preferred_element_type=F32)


def _split(a):
    hi = a.astype(BF16)
    lo = (a - hi.astype(F32)).astype(BF16)
    return hi, lo


def _split3(a):
    hi = a.astype(BF16)
    r = a - hi.astype(F32)
    mid = r.astype(BF16)
    lo = (r - mid.astype(F32)).astype(BF16)
    return hi, mid, lo


def _layer_norm(x):
    mu = jnp.mean(x, axis=-1, keepdims=True)
    xc = x - mu
    var = jnp.mean(xc * xc, axis=-1, keepdims=True)
    return xc * lax.rsqrt(var + LN_EPS)


def _log_sigmoid(x):
    return jnp.minimum(x, 0.0) - jnp.log(1.0 + jnp.exp(-jnp.abs(x)))


def _sigmoid(x):
    return jax.nn.sigmoid(x)


def _mod_kernel(c_ref, w_ref, b_ref, o_ref):
    c = c_ref[...]
    s = c * _sigmoid(c)
    s_hi, s_lo = _split(s)
    w = w_ref[...]
    w_hi, w_lo = _split(w)
    o_ref[...] = _dot(s_hi, w_hi) + _dot(s_lo, w_hi) + _dot(s_hi, w_lo) + b_ref[...]


def _modulation(cond, w_mod, b_mod):
    rows = cond.shape[0]
    tn = 1024
    return pl.pallas_call(
        _mod_kernel,
        grid=(DEPTH, 3 * D_MODEL // tn),
        in_specs=[
            pl.BlockSpec((rows, D_MODEL), lambda l, j: (0, 0)),
            pl.BlockSpec((None, D_MODEL, tn), lambda l, j: (l, 0, j)),
            pl.BlockSpec((None, 1, tn), lambda l, j: (l, 0, j)),
        ],
        out_specs=pl.BlockSpec((None, rows, tn), lambda l, j: (l, 0, j)),
        out_shape=jax.ShapeDtypeStruct((DEPTH, rows, 3 * D_MODEL), F32),
        compiler_params=pltpu.CompilerParams(dimension_semantics=("arbitrary", "arbitrary")),
        name="adaln_modulation",
    )(cond, w_mod, b_mod.reshape(DEPTH, 1, 3 * D_MODEL))


PREP_K = 256


def _prep_kernel(wt_ref, o_ref):
    lane = lax.broadcasted_iota(jnp.int32, (PREP_K, LANES), 1)
    for c0 in range(0, W_COLS, LANES):
        r0 = c0 if c0 <= GATE_COL else c0 - (LANES - N_GATES)
        slab = jnp.transpose(wt_ref[r0:r0 + LANES, :])
        if c0 == GATE_COL:
            slab = jnp.where(lane < N_GATES, slab, 0.0)
        o_ref[:, c0:c0 + LANES] = slab.astype(BF16)


def _prep_in_weight(w_in_t):
    n_in = w_in_t.shape[1]
    return pl.pallas_call(
        _prep_kernel,
        grid=(DEPTH, D_MODEL // PREP_K),
        in_specs=[pl.BlockSpec((None, n_in, PREP_K), lambda l, i: (l, 0, i))],
        out_specs=pl.BlockSpec((None, PREP_K, W_COLS), lambda l, i: (l, i, 0)),
        out_shape=jax.ShapeDtypeStruct((DEPTH, D_MODEL, W_COLS), BF16),
        compiler_params=pltpu.CompilerParams(
            dimension_semantics=("arbitrary", "arbitrary"), vmem_limit_bytes=VMEM_LIMIT_BYTES),
        name="in_weight_prep",
    )(w_in_t)


def _rope(x, cos, sa, sb):
    return x * cos + pltpu.roll(x, LANES - AXIS_DIM // 2, 1) * sa + pltpu.roll(x, AXIS_DIM // 2, 1) * sb


def _group_rms(x, bd, gain):
    hi, lo = _split(x * x)
    ms = (_dot(hi, bd) + _dot(lo, bd)) * (1.0 / HEAD_DIM)
    return x * lax.rsqrt(ms + RMS_EPS) * gain


def _ones_row_tile():
    r = lax.broadcasted_iota(jnp.int32, (BF16_ROWS, LANES), 0)
    return jnp.where(r == 0, 1.0, 0.0).astype(BF16)


def _attend_chains(chains):
    state = []
    for qs, tiles, sink_row in chains:
        m_cols = qs.shape[0]
        if sink_row is None:
            m = jnp.full((1, m_cols), -jnp.inf, F32)
            acc = jnp.zeros((VT_ROWS, m_cols), F32)
        else:
            m = sink_row
            r = lax.broadcasted_iota(jnp.int32, (VT_ROWS, m_cols), 0)
            acc = jnp.where(r == HEAD_DIM, 1.0, 0.0)
        cur = tiles[0]()
        state.append([m, acc, cur, _dot_tb(cur[0], qs)])
    for t in range(max(len(tiles) for _, tiles, _ in chains)):
        for st, (qs, tiles, _) in zip(state, chains):
            if t >= len(tiles):
                continue
            m, acc, (_, v_t, valid), s = st
            if valid is not None:
                s = jnp.where(valid, s, -jnp.inf)
            if t + 1 < len(tiles):
                nxt = tiles[t + 1]()
                st[2], st[3] = nxt, _dot_tb(nxt[0], qs)
            m_new = jnp.maximum(m, jnp.max(s, axis=0, keepdims=True))
            p = jnp.exp2(s - m_new).astype(BF16)
            st[1] = acc * jnp.exp2(m - m_new) + _dot(v_t, p)
            st[0] = m_new
    return [st[1][0:HEAD_DIM, :] / st[1][HEAD_DIM:HEAD_DIM + 1, :] for st in state]


def _layer_kernel(lat, T, layer, *refs):
    refs = list(refs)
    x_ref, mod_ref, w_ref, wb_ref, wo_ref = refs[:5]
    g0_ref, g1_ref, gb_ref, lng_ref, lnb_ref = refs[5:10]
    bd_ref, tril_ref, triu_ref, sel_ref, eye_ref, sink_ref = refs[10:16]
    pos = 16
    if lat:
        cos_ref, sa_ref, sb_ref = refs[pos:pos + 3]
        cka_ref, cva_ref, ckw_ref, cvw_ref, c0_ref, n0_ref, m0_ref = refs[pos + 3:pos + 10]
        pos += 10
        y_ref = refs[pos]
        pos += 1
    else:
        pos += 5
        y_ref, ka_o, va_o, kw_o, vw_o, c_o, n_o, m_o = refs[pos:pos + 8]
        pos += 8
    (u_s, qa_s, qw_s, ka_s, vat_s, kw_s, vwt_s, qm_s, km_s, vmt_s, om_s, g_s,
     ya_s, yw_s, hmt_s, c_s, m_s) = refs[pos:pos + 17]
    pos += 17
    if lat:
        kwc_s, vwct_s = refs[pos:pos + 2]

    n_rows = T // ROW_BLOCK
    n_chunks = T // CHUNK
    chunks_per_block = ROW_BLOCK // CHUNK
    shift = mod_ref[0:1, :]
    scale = mod_ref[1:2, :]
    gate = mod_ref[2:3, :]
    ones_tile = _ones_row_tile()

    def phase1(i):
        R = ROW_BLOCK
        r0 = i * R if isinstance(i, int) else pl.multiple_of(i * R, R)
        rows = pl.ds(r0, R)
        x = x_ref[rows, :]
        u = (_layer_norm(x) * (1.0 + scale) + shift).astype(BF16)
        u_s[rows, :] = u

        def proj(c0, width):
            return _dot(u, w_ref[:, c0:c0 + width])

        if lat:
            cos = cos_ref[rows, :]
            sa = sa_ref[rows, :]
            sb = sb_ref[rows, :]

        def rope_wide(v):
            if not lat:
                return v
            slabs = [_rope(v[:, j:j + LANES], cos, sa, sb) for j in range(0, v.shape[1], LANES)]
            return slabs[0] if len(slabs) == 1 else jnp.concatenate(slabs, axis=1)

        qscale = LOG2E * HEAD_DIM ** -0.5
        qa = rope_wide(_group_rms(proj(C_QA, 512), bd_ref[...], g0_ref[...])) * qscale
        for h in range(N_HEADS):
            qa_s[h, rows, :] = qa[:, h * HEAD_DIM:(h + 1) * HEAD_DIM].astype(BF16)
        ka_n = _group_rms(proj(C_KA, LANES), bd_ref[0:LANES, 0:LANES], g1_ref[...])
        ka = rope_wide(ka_n)
        va = proj(C_VA, LANES)
        qw = rope_wide(proj(C_QW, 512)) * qscale
        for h in range(N_HEADS):
            qw_s[h, rows, :] = qw[:, h * HEAD_DIM:(h + 1) * HEAD_DIM].astype(BF16)
        kw_raw = proj(C_KW, LANES)
        kw = rope_wide(kw_raw)
        vw = proj(C_VW, LANES)
        for kv in range(N_KV):
            sl = slice(kv * HEAD_DIM, (kv + 1) * HEAD_DIM)
            ka_s[kv, rows, :] = ka[:, sl].astype(BF16)
            kw_s[kv, rows, :] = kw[:, sl].astype(BF16)
        qm_s[rows, :] = proj(C_QM, 256).astype(BF16)
        km_s[rows, :] = (proj(C_KM, 256) * (ML_DK ** -0.5)).astype(BF16)
        vm = proj(C_VM, 512)
        om_s[rows, :] = proj(C_OM, 512)
        g_s[rows, :] = proj(C_GT, LANES) + gb_ref[...]
        for j in range(chunks_per_block):
            cj = i * chunks_per_block + j
            cr = slice(j * CHUNK, (j + 1) * CHUNK)
            va_t = jnp.transpose(va[cr, :])
            vw_t = jnp.transpose(vw[cr, :])
            for kv in range(N_KV):
                hd = slice(kv * HEAD_DIM, (kv + 1) * HEAD_DIM)
                vat_s[cj, kv, 0:HEAD_DIM, :] = va_t[hd, :].astype(BF16)
                vat_s[cj, kv, HEAD_DIM:VT_ROWS, :] = ones_tile
                vwt_s[cj, kv, 0:HEAD_DIM, :] = vw_t[hd, :].astype(BF16)
                vwt_s[cj, kv, HEAD_DIM:VT_ROWS, :] = ones_tile
            if not lat:
                tc = slice(i * ROW_BLOCK + j * CHUNK, i * ROW_BLOCK + (j + 1) * CHUNK)
                ka_t = jnp.transpose(ka_n[cr, :])
                kw_t = jnp.transpose(kw_raw[cr, :])
                for kv in range(N_KV):
                    hd = slice(kv * HEAD_DIM, (kv + 1) * HEAD_DIM)
                    ka_o[kv, :, tc] = ka_t[hd, :]
                    va_o[kv, :, tc] = va_t[hd, :]
                    kw_o[kv, :, tc] = kw_t[hd, :]
                    vw_o[kv, :, tc] = vw_t[hd, :]
            for h in range(ML_HEADS):
                vmt_s[cj, h, 0:ML_DV, :] = jnp.transpose(vm[cr, h * ML_DV:(h + 1) * ML_DV]).astype(BF16)
                vmt_s[cj, h, ML_DV:ST_ROWS, :] = ones_tile

    if n_rows == 1:
        phase1(0)
    else:
        def body1(i, carry):
            phase1(i)
            return carry
        lax.fori_loop(0, n_rows, body1, 0)

    if lat:
        eye = eye_ref[...]
        for kv in range(N_KV):
            ka_s[kv, T:T + PAST_LEN, :] = cka_ref[kv].astype(BF16)
            kwc_s[kv] = ckw_ref[kv].astype(BF16)
            cva_t = _dot_tb(eye, cva_ref[kv].astype(BF16)).astype(BF16)
            cvw_t = _dot_tb(eye, cvw_ref[kv].astype(BF16)).astype(BF16)
            for j in range(PAST_LEN // CHUNK):
                vat_s[n_chunks + j, kv, 0:HEAD_DIM, :] = cva_t[:, j * CHUNK:(j + 1) * CHUNK]
                vat_s[n_chunks + j, kv, HEAD_DIM:VT_ROWS, :] = ones_tile
                vwct_s[kv, 0:HEAD_DIM, j * CHUNK:(j + 1) * CHUNK] = cvw_t[:, j * CHUNK:(j + 1) * CHUNK]
                vwct_s[kv, HEAD_DIM:VT_ROWS, j * CHUNK:(j + 1) * CHUNK] = ones_tile

    tq = CHUNK
    m_cols = N_GROUP * tq

    def sink_row(kv):
        col = lax.broadcasted_iota(jnp.int32, (1, m_cols), 1)
        row = jnp.full((1, m_cols), sink_ref[layer * N_HEADS + kv * N_GROUP], F32)
        for g in range(1, N_GROUP):
            row = jnp.where(col >= g * tq, sink_ref[layer * N_HEADS + kv * N_GROUP + g], row)
        return row * LOG2E

    def load_q(q_s, kv, q0):
        return q_s[kv * N_GROUP:(kv + 1) * N_GROUP, pl.ds(q0, tq), :].reshape(m_cols, HEAD_DIM)

    def store_heads(dst, q0, kv, o_t):
        for p in range(N_GROUP // 2):
            blk = jnp.concatenate([o_t[:, (2 * p) * tq:(2 * p + 1) * tq],
                                   o_t[:, (2 * p + 1) * tq:(2 * p + 2) * tq]], axis=0)
            c0 = (kv * N_GROUP + 2 * p) * HEAD_DIM
            dst[pl.ds(q0, tq), c0:c0 + LANES] = jnp.transpose(blk)

    def key_tiles(k_ref, vt_ref, kv, n_key_chunks, chunks_per_tile):
        def tile(c0):
            def load():
                k = k_ref[kv, c0 * CHUNK:(c0 + chunks_per_tile) * CHUNK, :]
                v_t = [vt_ref[c0 + j, kv] for j in range(chunks_per_tile)]
                return k, (v_t[0] if len(v_t) == 1 else jnp.concatenate(v_t, axis=1)), None
            return load
        return [tile(c0) for c0 in range(0, n_key_chunks, chunks_per_tile)]

    def band_tiles(kv, n, q0):
        j0 = jnp.clip(n - 1, 0, n_chunks - 3)
        w0 = pl.multiple_of(j0 * CHUNK, CHUNK)

        def band(c0, nc):
            def load():
                rows = nc * CHUNK
                kpos = w0 + c0 * CHUNK + lax.broadcasted_iota(jnp.int32, (rows, m_cols), 0)
                qpos = q0 + (lax.broadcasted_iota(jnp.int32, (rows, m_cols), 1) & (tq - 1))
                v_t = [vwt_s[j0 + c0 + j, kv] for j in range(nc)]
                return (kw_s[kv, pl.ds(w0 + c0 * CHUNK, rows), :],
                        v_t[0] if nc == 1 else jnp.concatenate(v_t, axis=1),
                        jnp.abs(kpos - qpos) <= WINDOW)
            return load
        return [band(0, 2), band(2, 1), lambda: (kwc_s[kv], vwct_s[kv], None)]

    def attend(n, q0, chunks_per_tile):
        chains, dsts = [], []
        n_key_chunks = ka_s.shape[1] // CHUNK
        for kv in range(N_KV):
            chains.append((load_q(qa_s, kv, q0), key_tiles(ka_s, vat_s, kv, n_key_chunks, chunks_per_tile), None))
            dsts.append((ya_s, kv))
        for kv in range(N_KV):
            tiles = band_tiles(kv, n, q0) if lat else key_tiles(kw_s, vwt_s, kv, n_chunks, chunks_per_tile)
            chains.append((load_q(qw_s, kv, q0), tiles, sink_row(kv)))
            dsts.append((yw_s, kv))
        for (dst, kv), o_t in zip(dsts, _attend_chains(chains)):
            store_heads(dst, q0, kv, o_t)

    if lat:
        def body2(n, carry):
            attend(n, pl.multiple_of(n * tq, tq), 2)
            return carry
        lax.fori_loop(0, n_chunks, body2, 0)
    else:
        for n in range(T // tq):
            attend(n, n * tq, 1)

    n_pairs = ML_HEADS // 2
    hmt_s[...] = jnp.zeros_like(hmt_s)
    for d in range(2):
        for h in range(ML_HEADS):
            idx = d * n_pairs + h // 2
            hl = slice((h % 2) * ML_DK, (h % 2 + 1) * ML_DK)
            c_s[idx, ML_DV:ST_ROWS, hl] = jnp.zeros((BF16_ROWS, ML_DK), F32)
            if lat:
                c_s[idx, 0:ML_DV, hl] = c0_ref[d, h]
                c_s[idx, ML_DV:ML_DV + 1, hl] = n0_ref[d, h:h + 1, :]
                m_s[d * ML_HEADS + h:d * ML_HEADS + h + 1, :] = jnp.broadcast_to(
                    m0_ref[d:d + 1, h:h + 1], (1, LANES))
            else:
                c_s[idx, 0:ML_DV, hl] = jnp.zeros((ML_DV, ML_DK), F32)
                m_s[d * ML_HEADS + h:d * ML_HEADS + h + 1, :] = jnp.zeros((1, LANES), F32)

    L = CHUNK
    s_idx = lax.broadcasted_iota(jnp.int32, (L, ML_HEADS * L), 0)
    t_idx = lax.broadcasted_iota(jnp.int32, (L, ML_HEADS * L), 1) & (L - 1)
    lane_row = lax.broadcasted_iota(jnp.int32, (1, LANES), 1)
    low_half = lax.broadcasted_iota(jnp.int32, (L, LANES), 1) < ML_DK
    ones_ll = jnp.ones((L, L), BF16)
    zeros_ll = jnp.zeros((L, L), BF16)

    def heads_row(src, r0, c0=0):
        return jnp.concatenate([src[r0 + h:r0 + h + 1, c0:c0 + L] for h in range(ML_HEADS)], axis=1)

    def block_diag_rows(x):
        zero = jnp.zeros_like(x)
        return jnp.concatenate([jnp.where(low_half, x, zero), jnp.where(low_half, zero, x)], axis=0)

    def mlstm_chunk(c):
        for d in range(2):
            cc = c if d == 0 else n_chunks - 1 - c
            r0 = cc * L if isinstance(cc, int) else pl.multiple_of(cc * L, L)
            rows = pl.ds(r0, L)
            g = g_s[rows, :]
            f_hi, f_lo = _split(_log_sigmoid(g))
            tri_c = tril_ref[...] if d == 0 else triu_ref[...]
            tri_r = triu_ref[...] if d == 0 else tril_ref[...]
            cum = _dot(tri_c, f_hi) + _dot(tri_c, f_lo)
            r = g - pltpu.roll(cum, LANES - ML_HEADS, 1)
            r1, r2, r3 = _split3(r)
            sel = sel_ref[d]
            rb = _dot(r1, sel) + _dot(r2, sel) + _dot(r3, sel)
            g_t = jnp.transpose(g)[0:N_GATES, :]
            ft_hi, ft_lo = _split(_log_sigmoid(g_t))
            tr = jnp.concatenate([tri_r, ones_ll], axis=1)
            ct = _dot(ft_hi, tr) + _dot(ft_lo, tr)
            valid = (s_idx <= t_idx) if d == 0 else (s_idx >= t_idx)
            gi0 = 2 * ML_HEADS * d
            gf0 = gi0 + ML_HEADS
            b_row = heads_row(ct, gf0)
            b_last = heads_row(ct, gf0, L)
            i_row = heads_row(g_t, gi0)
            m_prev = heads_row(m_s, d * ML_HEADS)
            a_row = b_row + m_prev
            dm = jnp.where(valid, b_row + rb, -jnp.inf)
            mt = jnp.maximum(a_row, jnp.max(dm, axis=0, keepdims=True))
            p = jnp.exp(dm - mt)
            w_inter = jnp.exp(a_row - mt)
            floor = jnp.exp(-mt)
            g_row = b_last - b_row + i_row
            g_max = jnp.concatenate(
                [jnp.broadcast_to(jnp.max(g_row[:, h * L:(h + 1) * L], axis=-1, keepdims=True), (1, L))
                 for h in range(ML_HEADS)], axis=1)
            m_new = jnp.maximum(b_last + m_prev, g_max)
            ws = jnp.exp(g_row - m_new)
            wc = jnp.exp(b_last + m_prev - m_new)
            for h in range(ML_HEADS):
                m_s[d * ML_HEADS + h:d * ML_HEADS + h + 1, :] = m_new[:, h * L:(h + 1) * L]
            for pr in range(n_pairs):
                idx = d * n_pairs + pr
                lanes = slice(pr * LANES, (pr + 1) * LANES)
                cols = slice(pr * 2 * L, (pr + 1) * 2 * L)
                k_pair = km_s[rows, lanes]
                q_bd = block_diag_rows(qm_s[rows, lanes])
                k_bd = block_diag_rows(k_pair)
                s_t = (_dot_tb(k_pair, q_bd) * p[:, cols]).astype(BF16)
                s_bd = jnp.concatenate(
                    [jnp.concatenate([s_t[:, 0:L], zeros_ll], axis=1),
                     jnp.concatenate([zeros_ll, s_t[:, L:2 * L]], axis=1)], axis=0)
                v_t = jnp.concatenate([vmt_s[cc, 2 * pr], vmt_s[cc, 2 * pr + 1]], axis=1)
                state = c_s[idx]
                tot = _dot(v_t, s_bd) + w_inter[:, cols] * _dot_tb(state.astype(BF16), q_bd)
                h_t = tot[0:ML_DV, :] / jnp.maximum(jnp.abs(tot[ML_DV:ML_DV + 1, :]), floor[:, cols])
                for e in range(2):
                    hr = slice((2 * pr + e) * ML_DV, (2 * pr + e + 1) * ML_DV)
                    hmt_s[cc, hr, :] += h_t[:, e * L:(e + 1) * L]
                wv = (v_t.astype(F32) * ws[:, cols]).astype(BF16)
                wc_pair = jnp.where(lane_row < ML_DK, wc[:, 2 * pr * L:(2 * pr + 1) * L],
                                    wc[:, (2 * pr + 1) * L:(2 * pr + 2) * L])
                c_s[idx] = wc_pair * state + _dot(wv, k_bd)

    if lat:
        def body3(c, carry):
            mlstm_chunk(c)
            return carry
        lax.fori_loop(0, n_chunks, body3, 0)
    else:
        for c in range(n_chunks):
            mlstm_chunk(c)

    if not lat:
        for d in range(2):
            for pr in range(n_pairs):
                c_t = jnp.transpose(c_s[d * n_pairs + pr, 0:ML_DV, :])
                for e in range(2):
                    c_o[d, 2 * pr + e] = c_t[e * ML_DK:(e + 1) * ML_DK, :]
            for h in range(ML_HEADS):
                idx = d * n_pairs + h // 2
                hl = slice((h % 2) * ML_DK, (h % 2 + 1) * ML_DK)
                n_o[d, h:h + 1, :] = c_s[idx, ML_DV:ML_DV + 1, hl]
                m_o[d:d + 1, h:h + 1] = m_s[d * ML_HEADS + h:d * ML_HEADS + h + 1, 0:1]

    def phase3(i):
        r0 = i * ROW_BLOCK if isinstance(i, int) else pl.multiple_of(i * ROW_BLOCK, ROW_BLOCK)
        rows = pl.ds(r0, ROW_BLOCK)
        u = u_s[rows, :]

        def proj(c0, width):
            return _dot(u, w_ref[:, c0:c0 + width])

        hm = jnp.concatenate(
            [jnp.concatenate([jnp.transpose(hmt_s[i * chunks_per_block + j, h * ML_DV:(h + 1) * ML_DV, :])
                              for h in range(ML_HEADS)], axis=1)
             for j in range(chunks_per_block)], axis=0)
        ys = (ya_s[rows, :], _sigmoid(om_s[rows, :]) * hm, yw_s[rows, :])
        merged = None
        for b, (yb, zc) in enumerate(zip(ys, (C_ZA, C_ZM, C_ZW))):
            z = proj(zc, BRANCH_W)
            t = (yb * (z * _sigmoid(z))).astype(BF16)
            pb = _sigmoid(proj(C_GMERGE + b * D_MODEL, D_MODEL)) * _dot(t, wb_ref[b])
            merged = pb if merged is None else merged + pb
        o = _dot(merged.astype(BF16), wo_ref[...])
        hres = ALPHA * x_ref[rows, :] + gate * o
        y_ref[rows, :] = _layer_norm(hres) * lng_ref[...] + lnb_ref[...]

    if n_rows == 1:
        phase3(0)
    else:
        def body4(i, carry):
            phase3(i)
            return carry
        lax.fori_loop(0, n_rows, body4, 0)


def _const_spec(shape):
    nd = len(shape)
    return pl.BlockSpec(shape, lambda b: (0,) * nd, pipeline_mode=pl.Buffered(1))


def _layer_spec(shape, layer):
    nd = len(shape) - 1
    return pl.BlockSpec((None,) + tuple(shape[1:]), lambda b: (layer,) + (0,) * nd,
                        pipeline_mode=pl.Buffered(1))


def _layer_call(lat, layer, x, mod, weights, consts, extra):
    B, T, _ = x.shape
    S = T + PAST_LEN if lat else T
    n_chunks = T // CHUNK

    if lat:
        seq_spec = pl.BlockSpec((None, T, D_MODEL), lambda b: (b, 0, 0), pipeline_mode=pl.Buffered(1))
    else:
        seq_spec = pl.BlockSpec((None, T, D_MODEL), lambda b: (b, 0, 0))
    if lat:
        mod_spec = pl.BlockSpec((None, None, 3, D_MODEL), lambda b: (layer, b + 1, 0, 0))
    else:
        mod_spec = pl.BlockSpec((None, None, 3, D_MODEL), lambda b: (layer, 0, 0, 0))
    *vmem_consts, sink = consts
    in_specs = [seq_spec, mod_spec]
    in_specs += [_layer_spec(a.shape, layer) for a in weights]
    in_specs += [_const_spec(a.shape) for a in vmem_consts]
    in_specs.append(pl.BlockSpec(memory_space=pltpu.SMEM))
    args = [x, mod, *weights, *vmem_consts, sink]

    if lat:
        cos, sa, sb, cka, cva, ckw, cvw, sc, sn, sm = extra
        in_specs += [_const_spec(cos.shape)] * 3
        kv_spec = pl.BlockSpec((None, None, N_KV, PAST_LEN, HEAD_DIM), lambda b: (b, layer, 0, 0, 0))
        in_specs += [kv_spec] * 4
        in_specs += [
            pl.BlockSpec((None, None, 2, ML_HEADS, ML_DV, ML_DK), lambda b: (b, layer, 0, 0, 0, 0)),
            pl.BlockSpec((None, None, 2, ML_HEADS, ML_DK), lambda b: (b, layer, 0, 0, 0)),
            pl.BlockSpec((None, None, 2, ML_HEADS), lambda b: (b, layer, 0, 0)),
        ]
        args += [cos, sa, sb, cka, cva, ckw, cvw, sc, sn, sm]
        out_specs = seq_spec
        out_shape = jax.ShapeDtypeStruct((B, T, D_MODEL), F32)
        aliases = {}
    else:
        first_acc = len(args)
        in_specs += [pl.BlockSpec(memory_space=pl.ANY)] * len(extra)
        args += list(extra)
        aliases = {first_acc + i: 1 + i for i in range(len(extra))}
        kv_out = pl.BlockSpec((None, None, N_KV, HEAD_DIM, T), lambda b: (b, layer, 0, 0, 0))
        out_specs = [
            seq_spec, kv_out, kv_out, kv_out, kv_out,
            pl.BlockSpec((None, None, 2, ML_HEADS, ML_DK, ML_DV), lambda b: (b, layer, 0, 0, 0, 0)),
            pl.BlockSpec((None, 2, ML_HEADS, ML_DK), lambda b: (b, 0, 0, 0)),
            pl.BlockSpec((None, 2, ML_HEADS), lambda b: (b, 0, 0)),
        ]
        out_shape = [jax.ShapeDtypeStruct((B, T, D_MODEL), F32)]
        out_shape += [jax.ShapeDtypeStruct(a.shape, a.dtype) for a in extra]
        out_shape += [
            jax.ShapeDtypeStruct((B, 2, ML_HEADS, ML_DK), F32),
            jax.ShapeDtypeStruct((B, 2, ML_HEADS), F32),
        ]

    scratch = [
        ((T, D_MODEL), BF16),
        ((N_HEADS, T, HEAD_DIM), BF16),
        ((N_HEADS, T, HEAD_DIM), BF16),
        ((N_KV, S, HEAD_DIM), BF16),
        ((S // CHUNK, N_KV, VT_ROWS, CHUNK), BF16),
        ((N_KV, T, HEAD_DIM), BF16),
        ((n_chunks, N_KV, VT_ROWS, CHUNK), BF16),
        ((T, ML_HEADS * ML_DK), BF16),
        ((T, ML_HEADS * ML_DK), BF16),
        ((n_chunks, ML_HEADS, ST_ROWS, CHUNK), BF16),
        ((T, ML_HEADS * ML_DV), F32),
        ((T, LANES), F32),
        ((T, BRANCH_W), F32),
        ((T, BRANCH_W), F32),
        ((n_chunks, ML_HEADS * ML_DV, CHUNK), F32),
        ((ML_HEADS, ST_ROWS, 2 * ML_DK), F32),
        ((2 * ML_HEADS, LANES), F32),
    ]
    if lat:
        scratch += [((N_KV, PAST_LEN, HEAD_DIM), BF16),
                    ((N_KV, VT_ROWS, PAST_LEN), BF16)]
    scratch = [pltpu.VMEM(shape, dtype) for shape, dtype in scratch]

    return pl.pallas_call(
        functools.partial(_layer_kernel, lat, T, layer),
        grid=(B,),
        in_specs=in_specs,
        out_specs=out_specs,
        out_shape=out_shape,
        scratch_shapes=scratch,
        input_output_aliases=aliases,
        compiler_params=pltpu.CompilerParams(
            dimension_semantics=("arbitrary",), vmem_limit_bytes=VMEM_LIMIT_BYTES),
        name=("latent_layer" if lat else "context_layer") + str(layer),
    )(*args)


def _rope_tables(T):
    rows = T // GRID_W
    row = jnp.repeat(jnp.arange(rows, dtype=F32), GRID_W)
    col = jnp.tile(jnp.arange(GRID_W, dtype=F32), rows)
    inv = ROPE_THETA ** (-jnp.arange(0, AXIS_DIM, 2, dtype=F32) / AXIS_DIM)
    ar = row[:, None] * inv
    ac = col[:, None] * inv
    ang = jnp.concatenate([ar, ar, ac, ac], axis=-1)
    ang = jnp.concatenate([ang, ang], axis=-1)
    first_half = (jnp.arange(LANES) % AXIS_DIM) < (AXIS_DIM // 2)
    cos = jnp.cos(ang)
    sin = jnp.sin(ang)
    sa = jnp.where(first_half[None, :], -sin, 0.0)
    sb = jnp.where(first_half[None, :], 0.0, sin)
    return cos, sa, sb


def _constants(sink_logit):
    grp = np.arange(N_HEADS * HEAD_DIM) // HEAD_DIM
    bd = jnp.asarray(grp[:, None] == grp[None, :], dtype=BF16)
    ti = np.arange(CHUNK)
    tril = jnp.asarray(ti[None, :] <= ti[:, None], dtype=BF16)
    triu = jnp.asarray(ti[None, :] >= ti[:, None], dtype=BF16)
    lane = np.arange(LANES)[:, None]
    blk = (np.arange(ML_HEADS * CHUNK) // CHUNK)[None, :]
    sel = jnp.asarray(np.stack([lane == 2 * ML_HEADS * d + blk for d in range(2)]), dtype=BF16)
    eye = jnp.asarray(np.eye(HEAD_DIM), dtype=BF16)
    return bd, tril, triu, sel, eye, sink_logit.reshape(-1)


def kernel(x_prompt, x_sample, cache_attn_k, cache_attn_v, cache_win_k, cache_win_v, state_mlstm_C,
           state_mlstm_n, state_mlstm_m, c, c_ctx, w_mod, b_mod, w_in, qk_gain, sink_logit,
           mlstm_gate_bias, w_branch, w_out, ln_gain, ln_bias):
    dec_b = x_sample.shape[0]
    assert dec_b + 1 <= 8
    cond = jnp.concatenate([c_ctx[None, :], c, jnp.zeros((8 - 1 - dec_b, D_MODEL), F32)], axis=0)
    mod = _modulation(cond, w_mod, b_mod).reshape(DEPTH, 8, 3, D_MODEL)

    consts = _constants(sink_logit)
    cos, sa, sb = _rope_tables(x_sample.shape[1])
    lane_pad = jnp.zeros((DEPTH, 1, LANES - N_GATES), F32)
    weights = (
        _prep_in_weight(jnp.swapaxes(w_in, 1, 2)), w_branch.astype(BF16), w_out.astype(BF16),
        jnp.tile(qk_gain[:, 0:1, :], (1, 1, N_HEADS)),
        jnp.tile(qk_gain[:, 1:2, :], (1, 1, N_KV)),
        jnp.concatenate([mlstm_gate_bias.reshape(DEPTH, 1, N_GATES), lane_pad], axis=2),
        ln_gain[:, None, :], ln_bias[:, None, :],
    )

    B, T, _ = x_prompt.shape
    kv_t = jnp.zeros((B, DEPTH, N_KV, HEAD_DIM, T), F32)
    ctx = (kv_t, kv_t, kv_t, kv_t, jnp.zeros((B, DEPTH, 2, ML_HEADS, ML_DK, ML_DV), F32))
    xp, xs = x_prompt, x_sample
    small = []
    for l in range(DEPTH):
        outs = _layer_call(False, l, xp, mod, weights, consts, ctx)
        xp = outs[0]
        ctx = tuple(outs[1:6])
        small.append(outs[6:])
        extra = (cos, sa, sb, cache_attn_k, cache_attn_v, cache_win_k, cache_win_v,
                 state_mlstm_C, state_mlstm_n, state_mlstm_m)
        xs = _layer_call(True, l, xs, mod, weights, consts, extra)
    new_n, new_m = [jnp.stack([small[l][i] for l in range(DEPTH)], axis=1) for i in range(2)]
    return (xp, xs, *[jnp.swapaxes(a, -1, -2) for a in ctx], new_n, new_m)
```

```python
import functools

import jax
import jax.numpy as jnp
import numpy as np
from jax import lax
from jax.experimental import pallas as pl
from jax.experimental.pallas import tpu as pltpu

F32 = jnp.float32
BF16 = jnp.bfloat16

D_MODEL = 1024
DEPTH = 2
PAST_LEN = 256
GRID_W = 64
HEAD_DIM = 64
N_HEADS = 8
N_KV = 2
N_GROUP = N_HEADS // N_KV
WINDOW = 128
ML_HEADS = 4
ML_DK = 64
ML_DV = 128
CHUNK = 128
BRANCH_W = 512
ROPE_THETA = 10000.0
AXIS_DIM = HEAD_DIM // 2
LN_EPS = 1e-6
RMS_EPS = 1e-6
ALPHA = (2.0 * DEPTH) ** 0.25
LOG2E = float(np.log2(np.e))
LANES = 128
BF16_ROWS = 16
ROW_BLOCK = 256
VMEM_LIMIT_BYTES = 60 * 1024 * 1024
VT_ROWS = HEAD_DIM + BF16_ROWS
ST_ROWS = ML_DV + BF16_ROWS

_SIZES = (512, 128, 128, 512, 256, 256, 512, 512, 128, 512, 512, 128, 128, 512, 3072)
_OFF = [int(v) for v in np.concatenate([[0], np.cumsum(_SIZES)])]
(C_QA, C_KA, C_VA, C_ZA, C_QM, C_KM, C_VM, C_OM, C_GT, C_ZM, C_QW, C_KW, C_VW, C_ZW, C_GMERGE) = _OFF[:-1]
W_COLS = _OFF[-1]
GATE_COL = 2816
N_GATES = 4 * ML_HEADS


def _dot(a, b):
    return jnp.dot(a, b, preferred_element_type=F32)


def _dot_tb(a, b):
    return lax.dot_general(a, b, (((1,), (1,)), ((), ())), preferred_element_type=F32)


def _split(a):
    hi = a.astype(BF16)
    lo = (a - hi.astype(F32)).astype(BF16)
    return hi, lo


def _split3(a):
    hi = a.astype(BF16)
    r = a - hi.astype(F32)
    mid = r.astype(BF16)
    lo = (r - mid.astype(F32)).astype(BF16)
    return hi, mid, lo


def _layer_norm(x):
    mu = jnp.mean(x, axis=-1, keepdims=True)
    xc = x - mu
    var = jnp.mean(xc * xc, axis=-1, keepdims=True)
    return xc * lax.rsqrt(var + LN_EPS)


def _log_sigmoid(x):
    return jnp.minimum(x, 0.0) - jnp.log(1.0 + jnp.exp(-jnp.abs(x)))


def _sigmoid(x):
    return jax.nn.sigmoid(x)


def _mod_kernel(c_ref, w_ref, b_ref, o_ref):
    c = c_ref[...]
    s = c * _sigmoid(c)
    s_hi, s_lo = _split(s)
    w = w_ref[...]
    w_hi, w_lo = _split(w)
    o_ref[...] = _dot(s_hi, w_hi) + _dot(s_lo, w_hi) + _dot(s_hi, w_lo) + b_ref[...]


def _modulation(cond, w_mod, b_mod):
    rows = cond.shape[0]
    tn = 1024
    return pl.pallas_call(
        _mod_kernel,
        grid=(DEPTH, 3 * D_MODEL // tn),
        in_specs=[
            pl.BlockSpec((rows, D_MODEL), lambda l, j: (0, 0)),
            pl.BlockSpec((None, D_MODEL, tn), lambda l, j: (l, 0, j)),
            pl.BlockSpec((None, 1, tn), lambda l, j: (l, 0, j)),
        ],
        out_specs=pl.BlockSpec((None, rows, tn), lambda l, j: (l, 0, j)),
        out_shape=jax.ShapeDtypeStruct((DEPTH, rows, 3 * D_MODEL), F32),
        compiler_params=pltpu.CompilerParams(dimension_semantics=("arbitrary", "arbitrary")),
        name="adaln_modulation",
    )(cond, w_mod, b_mod.reshape(DEPTH, 1, 3 * D_MODEL))


PREP_K = 256


def _prep_kernel(wt_ref, o_ref):
    lane = lax.broadcasted_iota(jnp.int32, (PREP_K, LANES), 1)
    for c0 in range(0, W_COLS, LANES):
        r0 = c0 if c0 <= GATE_COL else c0 - (LANES - N_GATES)
        slab = jnp.transpose(wt_ref[r0:r0 + LANES, :])
        if c0 == GATE_COL:
            slab = jnp.where(lane < N_GATES, slab, 0.0)
        o_ref[:, c0:c0 + LANES] = slab.astype(BF16)


def _prep_in_weight(w_in_t):
    n_in = w_in_t.shape[1]
    return pl.pallas_call(
        _prep_kernel,
        grid=(DEPTH, D_MODEL // PREP_K),
        in_specs=[pl.BlockSpec((None, n_in, PREP_K), lambda l, i: (l, 0, i))],
        out_specs=pl.BlockSpec((None, PREP_K, W_COLS), lambda l, i: (l, i, 0)),
        out_shape=jax.ShapeDtypeStruct((DEPTH, D_MODEL, W_COLS), BF16),
        compiler_params=pltpu.CompilerParams(
            dimension_semantics=("arbitrary", "arbitrary"), vmem_limit_bytes=VMEM_LIMIT_BYTES),
        name="in_weight_prep",
    )(w_in_t)


def _rope(x, cos, sa, sb):
    return x * cos + pltpu.roll(x, LANES - AXIS_DIM // 2, 1) * sa + pltpu.roll(x, AXIS_DIM // 2, 1) * sb


def _group_rms(x, bd, gain):
    hi, lo = _split(x * x)
    ms = (_dot(hi, bd) + _dot(lo, bd)) * (1.0 / HEAD_DIM)
    return x * lax.rsqrt(ms + RMS_EPS) * gain


def _ones_row_tile():
    r = lax.broadcasted_iota(jnp.int32, (BF16_ROWS, LANES), 0)
    return jnp.where(r == 0, 1.0, 0.0).astype(BF16)


def _attend_chains(chains):
    state = []
    for qs, tiles, sink_row in chains:
        m_cols = qs.shape[0]
        if sink_row is None:
            m = jnp.full((1, m_cols), -jnp.inf, F32)
            acc = jnp.zeros((VT_ROWS, m_cols), F32)
        else:
            m = sink_row
            r = lax.broadcasted_iota(jnp.int32, (VT_ROWS, m_cols), 0)
            acc = jnp.where(r == HEAD_DIM, 1.0, 0.0)
        cur = tiles[0]()
        state.append([m, acc, cur, _dot_tb(cur[0], qs)])
    for t in range(max(len(tiles) for _, tiles, _ in chains)):
        for st, (qs, tiles, _) in zip(state, chains):
            if t >= len(tiles):
                continue
            m, acc, (_, v_t, valid), s = st
            if valid is not None:
                s = jnp.where(valid, s, -jnp.inf)
            if t + 1 < len(tiles):
                nxt = tiles[t + 1]()
                st[2], st[3] = nxt, _dot_tb(nxt[0], qs)
            m_new = jnp.maximum(m, jnp.max(s, axis=0, keepdims=True))
            p = jnp.exp2(s - m_new).astype(BF16)
            st[1] = acc * jnp.exp2(m - m_new) + _dot(v_t, p)
            st[0] = m_new
    return [st[1][0:HEAD_DIM, :] / st[1][HEAD_DIM:HEAD_DIM + 1, :] for st in state]


def _layer_kernel(lat, T, layer, *refs):
    refs = list(refs)
    x_ref, mod_ref, w_ref, wb_ref, wo_ref = refs[:5]
    g0_ref, g1_ref, gb_ref, lng_ref, lnb_ref = refs[5:10]
    bd_ref, tril_ref, triu_ref, sel_ref, eye_ref, sink_ref = refs[10:16]
    pos = 16
    if lat:
        cos_ref, sa_ref, sb_ref = refs[pos:pos + 3]
        cka_ref, cva_ref, ckw_ref, cvw_ref, c0_ref, n0_ref, m0_ref = refs[pos + 3:pos + 10]
        pos += 10
        y_ref = refs[pos]
        pos += 1
    else:
        pos += 5
        y_ref, ka_o, va_o, kw_o, vw_o, c_o, n_o, m_o = refs[pos:pos + 8]
        pos += 8
    (u_s, qa_s, qw_s, ka_s, vat_s, kw_s, vwt_s, qm_s, km_s, vmt_s, om_s, g_s,
     ya_s, yw_s, hmt_s, c_s, m_s) = refs[pos:pos + 17]
    pos += 17
    if lat:
        kwc_s, vwct_s = refs[pos:pos + 2]

    n_rows = T // ROW_BLOCK
    n_chunks = T // CHUNK
    chunks_per_block = ROW_BLOCK // CHUNK
    shift = mod_ref[0:1, :]
    scale = mod_ref[1:2, :]
    gate = mod_ref[2:3, :]
    ones_tile = _ones_row_tile()

    def phase1(i):
        R = ROW_BLOCK
        r0 = i * R if isinstance(i, int) else pl.multiple_of(i * R, R)
        rows = pl.ds(r0, R)
        x = x_ref[rows, :]
        u = (_layer_norm(x) * (1.0 + scale) + shift).astype(BF16)
        u_s[rows, :] = u

        def proj(c0, width):
            return _dot(u, w_ref[:, c0:c0 + width])

        if lat:
            cos = cos_ref[rows, :]
            sa = sa_ref[rows, :]
            sb = sb_ref[rows, :]

        def rope_wide(v):
            if not lat:
                return v
            slabs = [_rope(v[:, j:j + LANES], cos, sa, sb) for j in range(0, v.shape[1], LANES)]
            return slabs[0] if len(slabs) == 1 else jnp.concatenate(slabs, axis=1)

        qscale = LOG2E * HEAD_DIM ** -0.5
        qa = rope_wide(_group_rms(proj(C_QA, 512), bd_ref[...], g0_ref[...])) * qscale
        for h in range(N_HEADS):
            qa_s[h, rows, :] = qa[:, h * HEAD_DIM:(h + 1) * HEAD_DIM].astype(BF16)
        ka_n = _group_rms(proj(C_KA, LANES), bd_ref[0:LANES, 0:LANES], g1_ref[...])
        ka = rope_wide(ka_n)
        va = proj(C_VA, LANES)
        qw = rope_wide(proj(C_QW, 512)) * qscale
        for h in range(N_HEADS):
            qw_s[h, rows, :] = qw[:, h * HEAD_DIM:(h + 1) * HEAD_DIM].astype(BF16)
        kw_raw = proj(C_KW, LANES)
        kw = rope_wide(kw_raw)
        vw = proj(C_VW, LANES)
        for kv in range(N_KV):
            sl = slice(kv * HEAD_DIM, (kv + 1) * HEAD_DIM)
            ka_s[kv, rows, :] = ka[:, sl].astype(BF16)
            kw_s[kv, rows, :] = kw[:, sl].astype(BF16)
        qm_s[rows, :] = proj(C_QM, 256).astype(BF16)
        km_s[rows, :] = (proj(C_KM, 256) * (ML_DK ** -0.5)).astype(BF16)
        vm = proj(C_VM, 512)
        om_s[rows, :] = proj(C_OM, 512)
        g_s[rows, :] = proj(C_GT, LANES) + gb_ref[...]
        for j in range(chunks_per_block):
            cj = i * chunks_per_block + j
            cr = slice(j * CHUNK, (j + 1) * CHUNK)
            va_t = jnp.transpose(va[cr, :])
            vw_t = jnp.transpose(vw[cr, :])
            for kv in range(N_KV):
                hd = slice(kv * HEAD_DIM, (kv + 1) * HEAD_DIM)
                vat_s[cj, kv, 0:HEAD_DIM, :] = va_t[hd, :].astype(BF16)
                vat_s[cj, kv, HEAD_DIM:VT_ROWS, :] = ones_tile
                vwt_s[cj, kv, 0:HEAD_DIM, :] = vw_t[hd, :].astype(BF16)
                vwt_s[cj, kv, HEAD_DIM:VT_ROWS, :] = ones_tile
            if not lat:
                tc = slice(i * ROW_BLOCK + j * CHUNK, i * ROW_BLOCK + (j + 1) * CHUNK)
                ka_t = jnp.transpose(ka_n[cr, :])
                kw_t = jnp.transpose(kw_raw[cr, :])
                for kv in range(N_KV):
                    hd = slice(kv * HEAD_DIM, (kv + 1) * HEAD_DIM)
                    ka_o[kv, :, tc] = ka_t[hd, :]
                    va_o[kv, :, tc] = va_t[hd, :]
                    kw_o[kv, :, tc] = kw_t[hd, :]
                    vw_o[kv, :, tc] = vw_t[hd, :]
            for h in range(ML_HEADS):
                vmt_s[cj, h, 0:ML_DV, :] = jnp.transpose(vm[cr, h * ML_DV:(h + 1) * ML_DV]).astype(BF16)
                vmt_s[cj, h, ML_DV:ST_ROWS, :] = ones_tile

    if n_rows == 1:
        phase1(0)
    else:
        def body1(i, carry):
            phase1(i)
            return carry
        lax.fori_loop(0, n_rows, body1, 0)

    if lat:
        eye = eye_ref[...]
        for kv in range(N_KV):
            ka_s[kv, T:T + PAST_LEN, :] = cka_ref[kv].astype(BF16)
            kwc_s[kv] = ckw_ref[kv].astype(BF16)
            cva_t = _dot_tb(eye, cva_ref[kv].astype(BF16)).astype(BF16)
            cvw_t = _dot_tb(eye, cvw_ref[kv].astype(BF16)).astype(BF16)
            for j in range(PAST_LEN // CHUNK):
                vat_s[n_chunks + j, kv, 0:HEAD_DIM, :] = cva_t[:, j * CHUNK:(j + 1) * CHUNK]
                vat_s[n_chunks + j, kv, HEAD_DIM:VT_ROWS, :] = ones_tile
                vwct_s[kv, 0:HEAD_DIM, j * CHUNK:(j + 1) * CHUNK] = cvw_t[:, j * CHUNK:(j + 1) * CHUNK]
                vwct_s[kv, HEAD_DIM:VT_ROWS, j * CHUNK:(j + 1) * CHUNK] = ones_tile

    tq = CHUNK
    m_cols = N_GROUP * tq

    def sink_row(kv):
        col = lax.broadcasted_iota(jnp.int32, (1, m_cols), 1)
        row = jnp.full((1, m_cols), sink_ref[layer * N_HEADS + kv * N_GROUP], F32)
        for g in range(1, N_GROUP):
            row = jnp.where(col >= g * tq, sink_ref[layer * N_HEADS + kv * N_GROUP + g], row)
        return row * LOG2E

    def load_q(q_s, kv, q0):
        return q_s[kv * N_GROUP:(kv + 1) * N_GROUP, pl.ds(q0, tq), :].reshape(m_cols, HEAD_DIM)

    def store_heads(dst, q0, kv, o_t):
        for p in range(N_GROUP // 2):
            blk = jnp.concatenate([o_t[:, (2 * p) * tq:(2 * p + 1) * tq],
                                   o_t[:, (2 * p + 1) * tq:(2 * p + 2) * tq]], axis=0)
            c0 = (kv * N_GROUP + 2 * p) * HEAD_DIM
            dst[pl.ds(q0, tq), c0:c0 + LANES] = jnp.transpose(blk)

    def key_tiles(k_ref, vt_ref, kv, n_key_chunks, chunks_per_tile):
        def tile(c0):
            def load():
                k = k_ref[kv, c0 * CHUNK:(c0 + chunks_per_tile) * CHUNK, :]
                v_t = [vt_ref[c0 + j, kv] for j in range(chunks_per_tile)]
                return k, (v_t[0] if len(v_t) == 1 else jnp.concatenate(v_t, axis=1)), None
            return load
        return [tile(c0) for c0 in range(0, n_key_chunks, chunks_per_tile)]

    def band_tiles(kv, n, q0):
        j0 = jnp.clip(n - 1, 0, n_chunks - 3)
        w0 = pl.multiple_of(j0 * CHUNK, CHUNK)

        def band(c0, nc):
            def load():
                rows = nc * CHUNK
                kpos = w0 + c0 * CHUNK + lax.broadcasted_iota(jnp.int32, (rows, m_cols), 0)
                qpos = q0 + (lax.broadcasted_iota(jnp.int32, (rows, m_cols), 1) & (tq - 1))
                v_t = [vwt_s[j0 + c0 + j, kv] for j in range(nc)]
                return (kw_s[kv, pl.ds(w0 + c0 * CHUNK, rows), :],
                        v_t[0] if nc == 1 else jnp.concatenate(v_t, axis=1),
                        jnp.abs(kpos - qpos) <= WINDOW)
            return load
        return [band(0, 2), band(2, 1), lambda: (kwc_s[kv], vwct_s[kv], None)]

    def attend(n, q0, chunks_per_tile):
        chains, dsts = [], []
        n_key_chunks = ka_s.shape[1] // CHUNK
        for kv in range(N_KV):
            chains.append((load_q(qa_s, kv, q0), key_tiles(ka_s, vat_s, kv, n_key_chunks, chunks_per_tile), None))
            dsts.append((ya_s, kv))
        for kv in range(N_KV):
            tiles = band_tiles(kv, n, q0) if lat else key_tiles(kw_s, vwt_s, kv, n_chunks, chunks_per_tile)
            chains.append((load_q(qw_s, kv, q0), tiles, sink_row(kv)))
            dsts.append((yw_s, kv))
        for (dst, kv), o_t in zip(dsts, _attend_chains(chains)):
            store_heads(dst, q0, kv, o_t)

    if lat:
        def body2(n, carry):
            attend(n, pl.multiple_of(n * tq, tq), 2)
            return carry
        lax.fori_loop(0, n_chunks, body2, 0)
    else:
        for n in range(T // tq):
            attend(n, n * tq, 1)

    n_pairs = ML_HEADS // 2
    hmt_s[...] = jnp.zeros_like(hmt_s)
    for d in range(2):
        for h in range(ML_HEADS):
            idx = d * n_pairs + h // 2
            hl = slice((h % 2) * ML_DK, (h % 2 + 1) * ML_DK)
            c_s[idx, ML_DV:ST_ROWS, hl] = jnp.zeros((BF16_ROWS, ML_DK), F32)
            if lat:
                c_s[idx, 0:ML_DV, hl] = c0_ref[d, h]
                c_s[idx, ML_DV:ML_DV + 1, hl] = n0_ref[d, h:h + 1, :]
                m_s[d * ML_HEADS + h:d * ML_HEADS + h + 1, :] = jnp.broadcast_to(
                    m0_ref[d:d + 1, h:h + 1], (1, LANES))
            else:
                c_s[idx, 0:ML_DV, hl] = jnp.zeros((ML_DV, ML_DK), F32)
                m_s[d * ML_HEADS + h:d * ML_HEADS + h + 1, :] = jnp.zeros((1, LANES), F32)

    L = CHUNK
    s_idx = lax.broadcasted_iota(jnp.int32, (L, ML_HEADS * L), 0)
    t_idx = lax.broadcasted_iota(jnp.int32, (L, ML_HEADS * L), 1) & (L - 1)
    lane_row = lax.broadcasted_iota(jnp.int32, (1, LANES), 1)
    low_half = lax.broadcasted_iota(jnp.int32, (L, LANES), 1) < ML_DK
    ones_ll = jnp.ones((L, L), BF16)
    zeros_ll = jnp.zeros((L, L), BF16)

    def heads_row(src, r0, c0=0):
        return jnp.concatenate([src[r0 + h:r0 + h + 1, c0:c0 + L] for h in range(ML_HEADS)], axis=1)

    def block_diag_rows(x):
        zero = jnp.zeros_like(x)
        return jnp.concatenate([jnp.where(low_half, x, zero), jnp.where(low_half, zero, x)], axis=0)

    def mlstm_streams(streams):
        m_state = {d: heads_row(m_s, d * ML_HEADS) for d in sorted({d for d, _ in streams})}
        c_state = {d: [c_s[d * n_pairs + pr] for pr in range(n_pairs)] for d in m_state}
        work = []
        for d, cc in streams:
            r0 = cc * L if isinstance(cc, int) else pl.multiple_of(cc * L, L)
            rows = pl.ds(r0, L)
            g = g_s[rows, :]
            f_hi, f_lo = _split(_log_sigmoid(g))
            tri_c = tril_ref[...] if d == 0 else triu_ref[...]
            tri_r = triu_ref[...] if d == 0 else tril_ref[...]
            cum = _dot(tri_c, f_hi) + _dot(tri_c, f_lo)
            r = g - pltpu.roll(cum, LANES - ML_HEADS, 1)
            r1, r2, r3 = _split3(r)
            sel = sel_ref[d]
            rb = _dot(r1, sel) + _dot(r2, sel) + _dot(r3, sel)
            g_t = jnp.transpose(g)[0:N_GATES, :]
            ft_hi, ft_lo = _split(_log_sigmoid(g_t))
            tr = jnp.concatenate([tri_r, ones_ll], axis=1)
            ct = _dot(ft_hi, tr) + _dot(ft_lo, tr)
            work.append(dict(d=d, cc=cc, rows=rows, rb=rb, g_t=g_t, ct=ct))
        for w in work:
            d = w["d"]
            valid = (s_idx <= t_idx) if d == 0 else (s_idx >= t_idx)
            gi0 = 2 * ML_HEADS * d
            gf0 = gi0 + ML_HEADS
            b_row = heads_row(w["ct"], gf0)
            b_last = heads_row(w["ct"], gf0, L)
            i_row = heads_row(w["g_t"], gi0)
            m_prev = m_state[d]
            a_row = b_row + m_prev
            dm = jnp.where(valid, b_row + w["rb"], -jnp.inf)
            mt = jnp.maximum(a_row, jnp.max(dm, axis=0, keepdims=True))
            w["p"] = jnp.exp(dm - mt)
            w["w_inter"] = jnp.exp(a_row - mt)
            w["floor"] = jnp.exp(-mt)
            g_row = b_last - b_row + i_row
            g_max = jnp.concatenate(
                [jnp.broadcast_to(jnp.max(g_row[:, h * L:(h + 1) * L], axis=-1, keepdims=True), (1, L))
                 for h in range(ML_HEADS)], axis=1)
            m_new = jnp.maximum(b_last + m_prev, g_max)
            w["ws"] = jnp.exp(g_row - m_new)
            w["wc"] = jnp.exp(b_last + m_prev - m_new)
            m_state[d] = m_new
        for w in work:
            rows, cc = w["rows"], w["cc"]
            w["pairs"] = []
            for pr in range(n_pairs):
                lanes = slice(pr * LANES, (pr + 1) * LANES)
                cols = slice(pr * 2 * L, (pr + 1) * 2 * L)
                k_pair = km_s[rows, lanes]
                q_bd = block_diag_rows(qm_s[rows, lanes])
                k_bd = block_diag_rows(k_pair)
                s_t = (_dot_tb(k_pair, q_bd) * w["p"][:, cols]).astype(BF16)
                s_bd = jnp.concatenate(
                    [jnp.concatenate([s_t[:, 0:L], zeros_ll], axis=1),
                     jnp.concatenate([zeros_ll, s_t[:, L:2 * L]], axis=1)], axis=0)
                v_t = jnp.concatenate([vmt_s[cc, 2 * pr], vmt_s[cc, 2 * pr + 1]], axis=1)
                intra = _dot(v_t, s_bd)
                wv = (v_t.astype(F32) * w["ws"][:, cols]).astype(BF16)
                w["pairs"].append((q_bd, intra, _dot(wv, k_bd)))
        for w in work:
            d, cc = w["d"], w["cc"]
            for pr, (q_bd, intra, update) in enumerate(w["pairs"]):
                cols = slice(pr * 2 * L, (pr + 1) * 2 * L)
                state = c_state[d][pr]
                tot = intra + w["w_inter"][:, cols] * _dot_tb(state.astype(BF16), q_bd)
                h_t = tot[0:ML_DV, :] / jnp.maximum(jnp.abs(tot[ML_DV:ML_DV + 1, :]), w["floor"][:, cols])
                for e in range(2):
                    hr = slice((2 * pr + e) * ML_DV, (2 * pr + e + 1) * ML_DV)
                    hmt_s[cc, hr, :] += h_t[:, e * L:(e + 1) * L]
                wc = w["wc"]
                wc_pair = jnp.where(lane_row < ML_DK, wc[:, 2 * pr * L:(2 * pr + 1) * L],
                                    wc[:, (2 * pr + 1) * L:(2 * pr + 2) * L])
                c_state[d][pr] = wc_pair * state + update
        for d in m_state:
            for h in range(ML_HEADS):
                m_s[d * ML_HEADS + h:d * ML_HEADS + h + 1, :] = m_state[d][:, h * L:(h + 1) * L]
            for pr in range(n_pairs):
                c_s[d * n_pairs + pr] = c_state[d][pr]

    if lat:
        def body3(i, carry):
            mlstm_streams([(d, c if d == 0 else n_chunks - 1 - c) for c in (2 * i, 2 * i + 1) for d in range(2)])
            return carry
        lax.fori_loop(0, n_chunks // 2, body3, 0)
    else:
        mlstm_streams([(d, c if d == 0 else n_chunks - 1 - c) for c in range(n_chunks) for d in range(2)])

    if not lat:
        for d in range(2):
            for pr in range(n_pairs):
                c_t = jnp.transpose(c_s[d * n_pairs + pr, 0:ML_DV, :])
                for e in range(2):
                    c_o[d, 2 * pr + e] = c_t[e * ML_DK:(e + 1) * ML_DK, :]
            for h in range(ML_HEADS):
                idx = d * n_pairs + h // 2
                hl = slice((h % 2) * ML_DK, (h % 2 + 1) * ML_DK)
                n_o[d, h:h + 1, :] = c_s[idx, ML_DV:ML_DV + 1, hl]
                m_o[d:d + 1, h:h + 1] = m_s[d * ML_HEADS + h:d * ML_HEADS + h + 1, 0:1]

    def phase3(i):
        r0 = i * ROW_BLOCK if isinstance(i, int) else pl.multiple_of(i * ROW_BLOCK, ROW_BLOCK)
        rows = pl.ds(r0, ROW_BLOCK)
        u = u_s[rows, :]

        def proj(c0, width):
            return _dot(u, w_ref[:, c0:c0 + width])

        hm = jnp.concatenate(
            [jnp.concatenate([jnp.transpose(hmt_s[i * chunks_per_block + j, h * ML_DV:(h + 1) * ML_DV, :])
                              for h in range(ML_HEADS)], axis=1)
             for j in range(chunks_per_block)], axis=0)
        ys = (ya_s[rows, :], _sigmoid(om_s[rows, :]) * hm, yw_s[rows, :])
        merged = None
        for b, (yb, zc) in enumerate(zip(ys, (C_ZA, C_ZM, C_ZW))):
            z = proj(zc, BRANCH_W)
            t = (yb * (z * _sigmoid(z))).astype(BF16)
            pb = _sigmoid(proj(C_GMERGE + b * D_MODEL, D_MODEL)) * _dot(t, wb_ref[b])
            merged = pb if merged is None else merged + pb
        o = _dot(merged.astype(BF16), wo_ref[...])
        hres = ALPHA * x_ref[rows, :] + gate * o
        y_ref[rows, :] = _layer_norm(hres) * lng_ref[...] + lnb_ref[...]

    if n_rows == 1:
        phase3(0)
    else:
        def body4(i, carry):
            phase3(i)
            return carry
        lax.fori_loop(0, n_rows, body4, 0)


def _const_spec(shape):
    nd = len(shape)
    return pl.BlockSpec(shape, lambda b: (0,) * nd, pipeline_mode=pl.Buffered(1))


def _layer_spec(shape, layer):
    nd = len(shape) - 1
    return pl.BlockSpec((None,) + tuple(shape[1:]), lambda b: (layer,) + (0,) * nd,
                        pipeline_mode=pl.Buffered(1))


def _layer_call(lat, layer, x, mod, weights, consts, extra):
    B, T, _ = x.shape
    S = T + PAST_LEN if lat else T
    n_chunks = T // CHUNK

    if lat:
        seq_spec = pl.BlockSpec((None, T, D_MODEL), lambda b: (b, 0, 0), pipeline_mode=pl.Buffered(1))
    else:
        seq_spec = pl.BlockSpec((None, T, D_MODEL), lambda b: (b, 0, 0))
    if lat:
        mod_spec = pl.BlockSpec((None, None, 3, D_MODEL), lambda b: (layer, b + 1, 0, 0))
    else:
        mod_spec = pl.BlockSpec((None, None, 3, D_MODEL), lambda b: (layer, 0, 0, 0))
    *vmem_consts, sink = consts
    in_specs = [seq_spec, mod_spec]
    in_specs += [_layer_spec(a.shape, layer) for a in weights]
    in_specs += [_const_spec(a.shape) for a in vmem_consts]
    in_specs.append(pl.BlockSpec(memory_space=pltpu.SMEM))
    args = [x, mod, *weights, *vmem_consts, sink]

    if lat:
        cos, sa, sb, cka, cva, ckw, cvw, sc, sn, sm = extra
        in_specs += [_const_spec(cos.shape)] * 3
        kv_spec = pl.BlockSpec((None, None, N_KV, PAST_LEN, HEAD_DIM), lambda b: (b, layer, 0, 0, 0))
        in_specs += [kv_spec] * 4
        in_specs += [
            pl.BlockSpec((None, None, 2, ML_HEADS, ML_DV, ML_DK), lambda b: (b, layer, 0, 0, 0, 0)),
            pl.BlockSpec((None, None, 2, ML_HEADS, ML_DK), lambda b: (b, layer, 0, 0, 0)),
            pl.BlockSpec((None, None, 2, ML_HEADS), lambda b: (b, layer, 0, 0)),
        ]
        args += [cos, sa, sb, cka, cva, ckw, cvw, sc, sn, sm]
        out_specs = seq_spec
        out_shape = jax.ShapeDtypeStruct((B, T, D_MODEL), F32)
        aliases = {}
    else:
        first_acc = len(args)
        in_specs += [pl.BlockSpec(memory_space=pl.ANY)] * len(extra)
        args += list(extra)
        aliases = {first_acc + i: 1 + i for i in range(len(extra))}
        kv_out = pl.BlockSpec((None, None, N_KV, HEAD_DIM, T), lambda b: (b, layer, 0, 0, 0))
        out_specs = [
            seq_spec, kv_out, kv_out, kv_out, kv_out,
            pl.BlockSpec((None, None, 2, ML_HEADS, ML_DK, ML_DV), lambda b: (b, layer, 0, 0, 0, 0)),
            pl.BlockSpec((None, 2, ML_HEADS, ML_DK), lambda b: (b, 0, 0, 0)),
            pl.BlockSpec((None, 2, ML_HEADS), lambda b: (b, 0, 0)),
        ]
        out_shape = [jax.ShapeDtypeStruct((B, T, D_MODEL), F32)]
        out_shape += [jax.ShapeDtypeStruct(a.shape, a.dtype) for a in extra]
        out_shape += [
            jax.ShapeDtypeStruct((B, 2, ML_HEADS, ML_DK), F32),
            jax.ShapeDtypeStruct((B, 2, ML_HEADS), F32),
        ]

    scratch = [
        ((T, D_MODEL), BF16),
        ((N_HEADS, T, HEAD_DIM), BF16),
        ((N_HEADS, T, HEAD_DIM), BF16),
        ((N_KV, S, HEAD_DIM), BF16),
        ((S // CHUNK, N_KV, VT_ROWS, CHUNK), BF16),
        ((N_KV, T, HEAD_DIM), BF16),
        ((n_chunks, N_KV, VT_ROWS, CHUNK), BF16),
        ((T, ML_HEADS * ML_DK), BF16),
        ((T, ML_HEADS * ML_DK), BF16),
        ((n_chunks, ML_HEADS, ST_ROWS, CHUNK), BF16),
        ((T, ML_HEADS * ML_DV), F32),
        ((T, LANES), F32),
        ((T, BRANCH_W), F32),
        ((T, BRANCH_W), F32),
        ((n_chunks, ML_HEADS * ML_DV, CHUNK), F32),
        ((ML_HEADS, ST_ROWS, 2 * ML_DK), F32),
        ((2 * ML_HEADS, LANES), F32),
    ]
    if lat:
        scratch += [((N_KV, PAST_LEN, HEAD_DIM), BF16),
                    ((N_KV, VT_ROWS, PAST_LEN), BF16)]
    scratch = [pltpu.VMEM(shape, dtype) for shape, dtype in scratch]

    return pl.pallas_call(
        functools.partial(_layer_kernel, lat, T, layer),
        grid=(B,),
        in_specs=in_specs,
        out_specs=out_specs,
        out_shape=out_shape,
        scratch_shapes=scratch,
        input_output_aliases=aliases,
        compiler_params=pltpu.CompilerParams(
            dimension_semantics=("arbitrary",), vmem_limit_bytes=VMEM_LIMIT_BYTES),
        name=("latent_layer" if lat else "context_layer") + str(layer),
    )(*args)


def _rope_tables(T):
    rows = T // GRID_W
    row = jnp.repeat(jnp.arange(rows, dtype=F32), GRID_W)
    col = jnp.tile(jnp.arange(GRID_W, dtype=F32), rows)
    inv = ROPE_THETA ** (-jnp.arange(0, AXIS_DIM, 2, dtype=F32) / AXIS_DIM)
    ar = row[:, None] * inv
    ac = col[:, None] * inv
    ang = jnp.concatenate([ar, ar, ac, ac], axis=-1)
    ang = jnp.concatenate([ang, ang], axis=-1)
    first_half = (jnp.arange(LANES) % AXIS_DIM) < (AXIS_DIM // 2)
    cos = jnp.cos(ang)
    sin = jnp.sin(ang)
    sa = jnp.where(first_half[None, :], -sin, 0.0)
    sb = jnp.where(first_half[None, :], 0.0, sin)
    return cos, sa, sb


def _constants(sink_logit):
    grp = np.arange(N_HEADS * HEAD_DIM) // HEAD_DIM
    bd = jnp.asarray(grp[:, None] == grp[None, :], dtype=BF16)
    ti = np.arange(CHUNK)
    tril = jnp.asarray(ti[None, :] <= ti[:, None], dtype=BF16)
    triu = jnp.asarray(ti[None, :] >= ti[:, None], dtype=BF16)
    lane = np.arange(LANES)[:, None]
    blk = (np.arange(ML_HEADS * CHUNK) // CHUNK)[None, :]
    sel = jnp.asarray(np.stack([lane == 2 * ML_HEADS * d + blk for d in range(2)]), dtype=BF16)
    eye = jnp.asarray(np.eye(HEAD_DIM), dtype=BF16)
    return bd, tril, triu, sel, eye, sink_logit.reshape(-1)


def kernel(x_prompt, x_sample, cache_attn_k, cache_attn_v, cache_win_k, cache_win_v, state_mlstm_C,
           state_mlstm_n, state_mlstm_m, c, c_ctx, w_mod, b_mod, w_in, qk_gain, sink_logit,
           mlstm_gate_bias, w_branch, w_out, ln_gain, ln_bias):
    dec_b = x_sample.shape[0]
    assert dec_b + 1 <= 8
    cond = jnp.concatenate([c_ctx[None, :], c, jnp.zeros((8 - 1 - dec_b, D_MODEL), F32)], axis=0)
    mod = _modulation(cond, w_mod, b_mod).reshape(DEPTH, 8, 3, D_MODEL)

    consts = _constants(sink_logit)
    cos, sa, sb = _rope_tables(x_sample.shape[1])
    lane_pad = jnp.zeros((DEPTH, 1, LANES - N_GATES), F32)
    weights = (
        _prep_in_weight(jnp.swapaxes(w_in, 1, 2)), w_branch.astype(BF16), w_out.astype(BF16),
        jnp.tile(qk_gain[:, 0:1, :], (1, 1, N_HEADS)),
        jnp.tile(qk_gain[:, 1:2, :], (1, 1, N_KV)),
        jnp.concatenate([mlstm_gate_bias.reshape(DEPTH, 1, N_GATES), lane_pad], axis=2),
        ln_gain[:, None, :], ln_bias[:, None, :],
    )

    B, T, _ = x_prompt.shape
    kv_t = jnp.zeros((B, DEPTH, N_KV, HEAD_DIM, T), F32)
    ctx = (kv_t, kv_t, kv_t, kv_t, jnp.zeros((B, DEPTH, 2, ML_HEADS, ML_DK, ML_DV), F32))
    xp, xs = x_prompt, x_sample
    small = []
    for l in range(DEPTH):
        outs = _layer_call(False, l, xp, mod, weights, consts, ctx)
        xp = outs[0]
        ctx = tuple(outs[1:6])
        small.append(outs[6:])
        extra = (cos, sa, sb, cache_attn_k, cache_attn_v, cache_win_k, cache_win_v,
                 state_mlstm_C, state_mlstm_n, state_mlstm_m)
        xs = _layer_call(True, l, xs, mod, weights, consts, extra)
    new_n, new_m = [jnp.stack([small[l][i] for l in range(DEPTH)], axis=1) for i in range(2)]
    return (xp, xs, *[jnp.swapaxes(a, -1, -2) for a in ctx], new_n, new_m)
```

```python
import functools

import jax
import jax.numpy as jnp
import numpy as np
from jax import lax
from jax.experimental import pallas as pl
from jax.experimental.pallas import tpu as pltpu

F32 = jnp.float32
BF16 = jnp.bfloat16

D_MODEL = 1024
DEPTH = 2
PAST_LEN = 256
GRID_W = 64
HEAD_DIM = 64
N_HEADS = 8
N_KV = 2
N_GROUP = N_HEADS // N_KV
WINDOW = 128
ML_HEADS = 4
ML_DK = 64
ML_DV = 128
CHUNK = 128
BRANCH_W = 512
ROPE_THETA = 10000.0
AXIS_DIM = HEAD_DIM // 2
LN_EPS = 1e-6
RMS_EPS = 1e-6
ALPHA = (2.0 * DEPTH) ** 0.25
LOG2E = float(np.log2(np.e))
LANES = 128
BF16_ROWS = 16
ROW_BLOCK = 256
VMEM_LIMIT_BYTES = 60 * 1024 * 1024
VT_ROWS = HEAD_DIM + BF16_ROWS
ST_ROWS = ML_DV + BF16_ROWS

_SIZES = (512, 128, 128, 512, 256, 256, 512, 512, 128, 512, 512, 128, 128, 512, 3072)
_OFF = [int(v) for v in np.concatenate([[0], np.cumsum(_SIZES)])]
(C_QA, C_KA, C_VA, C_ZA, C_QM, C_KM, C_VM, C_OM, C_GT, C_ZM, C_QW, C_KW, C_VW, C_ZW, C_GMERGE) = _OFF[:-1]
W_COLS = _OFF[-1]
GATE_COL = 2816
N_GATES = 4 * ML_HEADS


def _dot(a, b):
    return jnp.dot(a, b, preferred_element_type=F32)


def _dot_tb(a, b):
    return lax.dot_general(a, b, (((1,), (1,)), ((), ())), preferred_element_type=F32)


def _split(a):
    hi = a.astype(BF16)
    lo = (a - hi.astype(F32)).astype(BF16)
    return hi, lo


def _split3(a):
    hi = a.astype(BF16)
    r = a - hi.astype(F32)
    mid = r.astype(BF16)
    lo = (r - mid.astype(F32)).astype(BF16)
    return hi, mid, lo


def _layer_norm(x):
    mu = jnp.mean(x, axis=-1, keepdims=True)
    xc = x - mu
    var = jnp.mean(xc * xc, axis=-1, keepdims=True)
    return xc * lax.rsqrt(var + LN_EPS)


def _log_sigmoid(x):
    return jnp.minimum(x, 0.0) - jnp.log(1.0 + jnp.exp(-jnp.abs(x)))


def _sigmoid(x):
    return jax.nn.sigmoid(x)


def _mod_kernel(c_ref, w_ref, b_ref, o_ref):
    c = c_ref[...]
    s = c * _sigmoid(c)
    s_hi, s_lo = _split(s)
    w = w_ref[...]
    w_hi, w_lo = _split(w)
    o_ref[...] = _dot(s_hi, w_hi) + _dot(s_lo, w_hi) + _dot(s_hi, w_lo) + b_ref[...]


def _modulation(cond, w_mod, b_mod):
    rows = cond.shape[0]
    tn = 1024
    return pl.pallas_call(
        _mod_kernel,
        grid=(DEPTH, 3 * D_MODEL // tn),
        in_specs=[
            pl.BlockSpec((rows, D_MODEL), lambda l, j: (0, 0)),
            pl.BlockSpec((None, D_MODEL, tn), lambda l, j: (l, 0, j)),
            pl.BlockSpec((None, 1, tn), lambda l, j: (l, 0, j)),
        ],
        out_specs=pl.BlockSpec((None, rows, tn), lambda l, j: (l, 0, j)),
        out_shape=jax.ShapeDtypeStruct((DEPTH, rows, 3 * D_MODEL), F32),
        compiler_params=pltpu.CompilerParams(dimension_semantics=("arbitrary", "arbitrary")),
        name="adaln_modulation",
    )(cond, w_mod, b_mod.reshape(DEPTH, 1, 3 * D_MODEL))


PREP_K = 256


def _prep_kernel(wt_ref, o_ref):
    lane = lax.broadcasted_iota(jnp.int32, (PREP_K, LANES), 1)
    for c0 in range(0, W_COLS, LANES):
        r0 = c0 if c0 <= GATE_COL else c0 - (LANES - N_GATES)
        slab = jnp.transpose(wt_ref[r0:r0 + LANES, :])
        if c0 == GATE_COL:
            slab = jnp.where(lane < N_GATES, slab, 0.0)
        o_ref[:, c0:c0 + LANES] = slab.astype(BF16)


def _prep_in_weight(w_in_t):
    n_in = w_in_t.shape[1]
    return pl.pallas_call(
        _prep_kernel,
        grid=(DEPTH, D_MODEL // PREP_K),
        in_specs=[pl.BlockSpec((None, n_in, PREP_K), lambda l, i: (l, 0, i))],
        out_specs=pl.BlockSpec((None, PREP_K, W_COLS), lambda l, i: (l, i, 0)),
        out_shape=jax.ShapeDtypeStruct((DEPTH, D_MODEL, W_COLS), BF16),
        compiler_params=pltpu.CompilerParams(
            dimension_semantics=("arbitrary", "arbitrary"), vmem_limit_bytes=VMEM_LIMIT_BYTES),
        name="in_weight_prep",
    )(w_in_t)


def _rope(x, cos, sa, sb):
    return x * cos + pltpu.roll(x, LANES - AXIS_DIM // 2, 1) * sa + pltpu.roll(x, AXIS_DIM // 2, 1) * sb


def _group_rms(x, bd, gain):
    hi, lo = _split(x * x)
    ms = (_dot(hi, bd) + _dot(lo, bd)) * (1.0 / HEAD_DIM)
    return x * lax.rsqrt(ms + RMS_EPS) * gain


def _ones_row_tile():
    r = lax.broadcasted_iota(jnp.int32, (BF16_ROWS, LANES), 0)
    return jnp.where(r == 0, 1.0, 0.0).astype(BF16)


def _attend_chains(chains, results):
    state = []
    for qs, tiles, sink_row in chains:
        m_cols = qs.shape[0]
        if sink_row is None:
            m = jnp.full((1, m_cols), -jnp.inf, F32)
            acc = jnp.zeros((VT_ROWS, m_cols), F32)
        else:
            m = sink_row
            r = lax.broadcasted_iota(jnp.int32, (VT_ROWS, m_cols), 0)
            acc = jnp.where(r == HEAD_DIM, 1.0, 0.0)
        cur = tiles[0]()
        state.append([m, acc, cur, _dot_tb(cur[0], qs)])
    yield
    for t in range(max(len(tiles) for _, tiles, _ in chains)):
        for st, (qs, tiles, _) in zip(state, chains):
            if t >= len(tiles):
                continue
            m, acc, (_, v_t, valid), s = st
            if valid is not None:
                s = jnp.where(valid, s, -jnp.inf)
            if t + 1 < len(tiles):
                nxt = tiles[t + 1]()
                st[2], st[3] = nxt, _dot_tb(nxt[0], qs)
            m_new = jnp.maximum(m, jnp.max(s, axis=0, keepdims=True))
            p = jnp.exp2(s - m_new).astype(BF16)
            st[1] = acc * jnp.exp2(m - m_new) + _dot(v_t, p)
            st[0] = m_new
        yield
    results.extend(st[1][0:HEAD_DIM, :] / st[1][HEAD_DIM:HEAD_DIM + 1, :] for st in state)


def _interleave(weighted):
    live = [[g, n] for g, n in weighted]
    while live:
        for item in list(live):
            for _ in range(item[1]):
                try:
                    next(item[0])
                except StopIteration:
                    live.remove(item)
                    break


def _layer_kernel(lat, T, layer, *refs):
    refs = list(refs)
    x_ref, mod_ref, w_ref, wb_ref, wo_ref = refs[:5]
    g0_ref, g1_ref, gb_ref, lng_ref, lnb_ref = refs[5:10]
    bd_ref, tril_ref, triu_ref, sel_ref, eye_ref, sink_ref = refs[10:16]
    pos = 16
    if lat:
        cos_ref, sa_ref, sb_ref = refs[pos:pos + 3]
        cka_ref, cva_ref, ckw_ref, cvw_ref, c0_ref, n0_ref, m0_ref = refs[pos + 3:pos + 10]
        pos += 10
        y_ref = refs[pos]
        pos += 1
    else:
        pos += 5
        y_ref, ka_o, va_o, kw_o, vw_o, c_o, n_o, m_o = refs[pos:pos + 8]
        pos += 8
    (u_s, qa_s, qw_s, ka_s, vat_s, kw_s, vwt_s, qm_s, km_s, vmt_s, om_s, g_s,
     ya_s, yw_s, hmt_s, c_s, m_s) = refs[pos:pos + 17]
    pos += 17
    if lat:
        kwc_s, vwct_s = refs[pos:pos + 2]

    n_rows = T // ROW_BLOCK
    n_chunks = T // CHUNK
    chunks_per_block = ROW_BLOCK // CHUNK
    shift = mod_ref[0:1, :]
    scale = mod_ref[1:2, :]
    gate = mod_ref[2:3, :]
    ones_tile = _ones_row_tile()

    def phase1(i):
        R = ROW_BLOCK
        r0 = i * R if isinstance(i, int) else pl.multiple_of(i * R, R)
        rows = pl.ds(r0, R)
        x = x_ref[rows, :]
        u = (_layer_norm(x) * (1.0 + scale) + shift).astype(BF16)
        u_s[rows, :] = u

        def proj(c0, width):
            return _dot(u, w_ref[:, c0:c0 + width])

        if lat:
            cos = cos_ref[rows, :]
            sa = sa_ref[rows, :]
            sb = sb_ref[rows, :]

        def rope_wide(v):
            if not lat:
                return v
            slabs = [_rope(v[:, j:j + LANES], cos, sa, sb) for j in range(0, v.shape[1], LANES)]
            return slabs[0] if len(slabs) == 1 else jnp.concatenate(slabs, axis=1)

        qscale = LOG2E * HEAD_DIM ** -0.5
        qa = rope_wide(_group_rms(proj(C_QA, 512), bd_ref[...], g0_ref[...])) * qscale
        for h in range(N_HEADS):
            qa_s[h, rows, :] = qa[:, h * HEAD_DIM:(h + 1) * HEAD_DIM].astype(BF16)
        ka_n = _group_rms(proj(C_KA, LANES), bd_ref[0:LANES, 0:LANES], g1_ref[...])
        ka = rope_wide(ka_n)
        va = proj(C_VA, LANES)
        qw = rope_wide(proj(C_QW, 512)) * qscale
        for h in range(N_HEADS):
            qw_s[h, rows, :] = qw[:, h * HEAD_DIM:(h + 1) * HEAD_DIM].astype(BF16)
        kw_raw = proj(C_KW, LANES)
        kw = rope_wide(kw_raw)
        vw = proj(C_VW, LANES)
        for kv in range(N_KV):
            sl = slice(kv * HEAD_DIM, (kv + 1) * HEAD_DIM)
            ka_s[kv, rows, :] = ka[:, sl].astype(BF16)
            kw_s[kv, rows, :] = kw[:, sl].astype(BF16)
        qm_s[rows, :] = proj(C_QM, 256).astype(BF16)
        km_s[rows, :] = (proj(C_KM, 256) * (ML_DK ** -0.5)).astype(BF16)
        vm = proj(C_VM, 512)
        om_s[rows, :] = proj(C_OM, 512)
        g_s[rows, :] = proj(C_GT, LANES) + gb_ref[...]
        for j in range(chunks_per_block):
            cj = i * chunks_per_block + j
            cr = slice(j * CHUNK, (j + 1) * CHUNK)
            va_t = jnp.transpose(va[cr, :])
            vw_t = jnp.transpose(vw[cr, :])
            for kv in range(N_KV):
                hd = slice(kv * HEAD_DIM, (kv + 1) * HEAD_DIM)
                vat_s[cj, kv, 0:HEAD_DIM, :] = va_t[hd, :].astype(BF16)
                vat_s[cj, kv, HEAD_DIM:VT_ROWS, :] = ones_tile
                vwt_s[cj, kv, 0:HEAD_DIM, :] = vw_t[hd, :].astype(BF16)
                vwt_s[cj, kv, HEAD_DIM:VT_ROWS, :] = ones_tile
            if not lat:
                tc = slice(i * ROW_BLOCK + j * CHUNK, i * ROW_BLOCK + (j + 1) * CHUNK)
                ka_t = jnp.transpose(ka_n[cr, :])
                kw_t = jnp.transpose(kw_raw[cr, :])
                for kv in range(N_KV):
                    hd = slice(kv * HEAD_DIM, (kv + 1) * HEAD_DIM)
                    ka_o[kv, :, tc] = ka_t[hd, :]
                    va_o[kv, :, tc] = va_t[hd, :]
                    kw_o[kv, :, tc] = kw_t[hd, :]
                    vw_o[kv, :, tc] = vw_t[hd, :]
            for h in range(ML_HEADS):
                vmt_s[cj, h, 0:ML_DV, :] = jnp.transpose(vm[cr, h * ML_DV:(h + 1) * ML_DV]).astype(BF16)
                vmt_s[cj, h, ML_DV:ST_ROWS, :] = ones_tile

    if n_rows == 1:
        phase1(0)
    else:
        def body1(i, carry):
            phase1(i)
            return carry
        lax.fori_loop(0, n_rows, body1, 0)

    if lat:
        eye = eye_ref[...]
        for kv in range(N_KV):
            ka_s[kv, T:T + PAST_LEN, :] = cka_ref[kv].astype(BF16)
            kwc_s[kv] = ckw_ref[kv].astype(BF16)
            cva_t = _dot_tb(eye, cva_ref[kv].astype(BF16)).astype(BF16)
            cvw_t = _dot_tb(eye, cvw_ref[kv].astype(BF16)).astype(BF16)
            for j in range(PAST_LEN // CHUNK):
                vat_s[n_chunks + j, kv, 0:HEAD_DIM, :] = cva_t[:, j * CHUNK:(j + 1) * CHUNK]
                vat_s[n_chunks + j, kv, HEAD_DIM:VT_ROWS, :] = ones_tile
                vwct_s[kv, 0:HEAD_DIM, j * CHUNK:(j + 1) * CHUNK] = cvw_t[:, j * CHUNK:(j + 1) * CHUNK]
                vwct_s[kv, HEAD_DIM:VT_ROWS, j * CHUNK:(j + 1) * CHUNK] = ones_tile

    tq = CHUNK
    m_cols = N_GROUP * tq

    def sink_row(kv):
        col = lax.broadcasted_iota(jnp.int32, (1, m_cols), 1)
        row = jnp.full((1, m_cols), sink_ref[layer * N_HEADS + kv * N_GROUP], F32)
        for g in range(1, N_GROUP):
            row = jnp.where(col >= g * tq, sink_ref[layer * N_HEADS + kv * N_GROUP + g], row)
        return row * LOG2E

    def load_q(q_s, kv, q0):
        return q_s[kv * N_GROUP:(kv + 1) * N_GROUP, pl.ds(q0, tq), :].reshape(m_cols, HEAD_DIM)

    def store_heads(dst, q0, kv, o_t):
        for p in range(N_GROUP // 2):
            blk = jnp.concatenate([o_t[:, (2 * p) * tq:(2 * p + 1) * tq],
                                   o_t[:, (2 * p + 1) * tq:(2 * p + 2) * tq]], axis=0)
            c0 = (kv * N_GROUP + 2 * p) * HEAD_DIM
            dst[pl.ds(q0, tq), c0:c0 + LANES] = jnp.transpose(blk)

    def key_tiles(k_ref, vt_ref, kv, n_key_chunks, chunks_per_tile):
        def tile(c0):
            def load():
                k = k_ref[kv, c0 * CHUNK:(c0 + chunks_per_tile) * CHUNK, :]
                v_t = [vt_ref[c0 + j, kv] for j in range(chunks_per_tile)]
                return k, (v_t[0] if len(v_t) == 1 else jnp.concatenate(v_t, axis=1)), None
            return load
        return [tile(c0) for c0 in range(0, n_key_chunks, chunks_per_tile)]

    def band_tiles(kv, n, q0):
        j0 = jnp.clip(n - 1, 0, n_chunks - 3)
        w0 = pl.multiple_of(j0 * CHUNK, CHUNK)

        def band(c0, nc):
            def load():
                rows = nc * CHUNK
                kpos = w0 + c0 * CHUNK + lax.broadcasted_iota(jnp.int32, (rows, m_cols), 0)
                qpos = q0 + (lax.broadcasted_iota(jnp.int32, (rows, m_cols), 1) & (tq - 1))
                v_t = [vwt_s[j0 + c0 + j, kv] for j in range(nc)]
                return (kw_s[kv, pl.ds(w0 + c0 * CHUNK, rows), :],
                        v_t[0] if nc == 1 else jnp.concatenate(v_t, axis=1),
                        jnp.abs(kpos - qpos) <= WINDOW)
            return load
        return [band(0, 2), band(2, 1), lambda: (kwc_s[kv], vwct_s[kv], None)]

    def attend(n, q0, chunks_per_tile):
        chains, dsts = [], []
        n_key_chunks = ka_s.shape[1] // CHUNK
        for kv in range(N_KV):
            chains.append((load_q(qa_s, kv, q0), key_tiles(ka_s, vat_s, kv, n_key_chunks, chunks_per_tile), None))
            dsts.append((ya_s, kv))
        for kv in range(N_KV):
            tiles = band_tiles(kv, n, q0) if lat else key_tiles(kw_s, vwt_s, kv, n_chunks, chunks_per_tile)
            chains.append((load_q(qw_s, kv, q0), tiles, sink_row(kv)))
            dsts.append((yw_s, kv))
        results = []
        yield from _attend_chains(chains, results)
        for (dst, kv), o_t in zip(dsts, results):
            store_heads(dst, q0, kv, o_t)

    n_pairs = ML_HEADS // 2
    hmt_s[...] = jnp.zeros_like(hmt_s)
    for d in range(2):
        for h in range(ML_HEADS):
            idx = d * n_pairs + h // 2
            hl = slice((h % 2) * ML_DK, (h % 2 + 1) * ML_DK)
            c_s[idx, ML_DV:ST_ROWS, hl] = jnp.zeros((BF16_ROWS, ML_DK), F32)
            if lat:
                c_s[idx, 0:ML_DV, hl] = c0_ref[d, h]
                c_s[idx, ML_DV:ML_DV + 1, hl] = n0_ref[d, h:h + 1, :]
                m_s[d * ML_HEADS + h:d * ML_HEADS + h + 1, :] = jnp.broadcast_to(
                    m0_ref[d:d + 1, h:h + 1], (1, LANES))
            else:
                c_s[idx, 0:ML_DV, hl] = jnp.zeros((ML_DV, ML_DK), F32)
                m_s[d * ML_HEADS + h:d * ML_HEADS + h + 1, :] = jnp.zeros((1, LANES), F32)

    L = CHUNK
    s_idx = lax.broadcasted_iota(jnp.int32, (L, ML_HEADS * L), 0)
    t_idx = lax.broadcasted_iota(jnp.int32, (L, ML_HEADS * L), 1) & (L - 1)
    lane_row = lax.broadcasted_iota(jnp.int32, (1, LANES), 1)
    low_half = lax.broadcasted_iota(jnp.int32, (L, LANES), 1) < ML_DK
    ones_ll = jnp.ones((L, L), BF16)
    zeros_ll = jnp.zeros((L, L), BF16)

    def heads_row(src, r0, c0=0):
        return jnp.concatenate([src[r0 + h:r0 + h + 1, c0:c0 + L] for h in range(ML_HEADS)], axis=1)

    def block_diag_rows(x):
        zero = jnp.zeros_like(x)
        return jnp.concatenate([jnp.where(low_half, x, zero), jnp.where(low_half, zero, x)], axis=0)

    def mlstm_streams(streams):
        m_state = {d: heads_row(m_s, d * ML_HEADS) for d in sorted({d for d, _ in streams})}
        c_state = {d: [c_s[d * n_pairs + pr] for pr in range(n_pairs)] for d in m_state}
        work = []
        for d, cc in streams:
            r0 = cc * L if isinstance(cc, int) else pl.multiple_of(cc * L, L)
            rows = pl.ds(r0, L)
            g = g_s[rows, :]
            f_hi, f_lo = _split(_log_sigmoid(g))
            tri_c = tril_ref[...] if d == 0 else triu_ref[...]
            tri_r = triu_ref[...] if d == 0 else tril_ref[...]
            cum = _dot(tri_c, f_hi) + _dot(tri_c, f_lo)
            r = g - pltpu.roll(cum, LANES - ML_HEADS, 1)
            r1, r2, r3 = _split3(r)
            sel = sel_ref[d]
            rb = _dot(r1, sel) + _dot(r2, sel) + _dot(r3, sel)
            g_t = jnp.transpose(g)[0:N_GATES, :]
            ft_hi, ft_lo = _split(_log_sigmoid(g_t))
            tr = jnp.concatenate([tri_r, ones_ll], axis=1)
            ct = _dot(ft_hi, tr) + _dot(ft_lo, tr)
            work.append(dict(d=d, cc=cc, rows=rows, rb=rb, g_t=g_t, ct=ct))
            yield
        for w in work:
            d = w["d"]
            valid = (s_idx <= t_idx) if d == 0 else (s_idx >= t_idx)
            gi0 = 2 * ML_HEADS * d
            gf0 = gi0 + ML_HEADS
            b_row = heads_row(w["ct"], gf0)
            b_last = heads_row(w["ct"], gf0, L)
            i_row = heads_row(w["g_t"], gi0)
            m_prev = m_state[d]
            a_row = b_row + m_prev
            dm = jnp.where(valid, b_row + w["rb"], -jnp.inf)
            mt = jnp.maximum(a_row, jnp.max(dm, axis=0, keepdims=True))
            w["p"] = jnp.exp(dm - mt)
            w["w_inter"] = jnp.exp(a_row - mt)
            w["floor"] = jnp.exp(-mt)
            g_row = b_last - b_row + i_row
            g_max = jnp.concatenate(
                [jnp.broadcast_to(jnp.max(g_row[:, h * L:(h + 1) * L], axis=-1, keepdims=True), (1, L))
                 for h in range(ML_HEADS)], axis=1)
            m_new = jnp.maximum(b_last + m_prev, g_max)
            w["ws"] = jnp.exp(g_row - m_new)
            w["wc"] = jnp.exp(b_last + m_prev - m_new)
            m_state[d] = m_new
            yield
        for w in work:
            rows, cc = w["rows"], w["cc"]
            w["pairs"] = []
            for pr in range(n_pairs):
                lanes = slice(pr * LANES, (pr + 1) * LANES)
                cols = slice(pr * 2 * L, (pr + 1) * 2 * L)
                k_pair = km_s[rows, lanes]
                q_bd = block_diag_rows(qm_s[rows, lanes])
                k_bd = block_diag_rows(k_pair)
                s_t = (_dot_tb(k_pair, q_bd) * w["p"][:, cols]).astype(BF16)
                s_bd = jnp.concatenate(
                    [jnp.concatenate([s_t[:, 0:L], zeros_ll], axis=1),
                     jnp.concatenate([zeros_ll, s_t[:, L:2 * L]], axis=1)], axis=0)
                v_t = jnp.concatenate([vmt_s[cc, 2 * pr], vmt_s[cc, 2 * pr + 1]], axis=1)
                intra = _dot(v_t, s_bd)
                wv = (v_t.astype(F32) * w["ws"][:, cols]).astype(BF16)
                w["pairs"].append((q_bd, intra, _dot(wv, k_bd)))
            yield
        for w in work:
            d, cc = w["d"], w["cc"]
            for pr, (q_bd, intra, update) in enumerate(w["pairs"]):
                cols = slice(pr * 2 * L, (pr + 1) * 2 * L)
                state = c_state[d][pr]
                tot = intra + w["w_inter"][:, cols] * _dot_tb(state.astype(BF16), q_bd)
                h_t = tot[0:ML_DV, :] / jnp.maximum(jnp.abs(tot[ML_DV:ML_DV + 1, :]), w["floor"][:, cols])
                for e in range(2):
                    hr = slice((2 * pr + e) * ML_DV, (2 * pr + e + 1) * ML_DV)
                    hmt_s[cc, hr, :] += h_t[:, e * L:(e + 1) * L]
                wc = w["wc"]
                wc_pair = jnp.where(lane_row < ML_DK, wc[:, 2 * pr * L:(2 * pr + 1) * L],
                                    wc[:, (2 * pr + 1) * L:(2 * pr + 2) * L])
                c_state[d][pr] = wc_pair * state + update
            yield
        for d in m_state:
            for h in range(ML_HEADS):
                m_s[d * ML_HEADS + h:d * ML_HEADS + h + 1, :] = m_state[d][:, h * L:(h + 1) * L]
            for pr in range(n_pairs):
                c_s[d * n_pairs + pr] = c_state[d][pr]

    def scan_streams(cs):
        return [(d, c if d == 0 else n_chunks - 1 - c) for c in cs for d in range(2)]

    if lat:
        def body2(n, carry):
            _interleave([(attend(n, pl.multiple_of(n * tq, tq), 2), 1),
                         (mlstm_streams(scan_streams([n])), 2)])
            return carry
        lax.fori_loop(0, n_chunks, body2, 0)
    else:
        def all_tiles():
            for n in range(T // tq):
                yield from attend(n, n * tq, 1)
        _interleave([(all_tiles(), 1), (mlstm_streams(scan_streams(range(n_chunks))), 3)])

    if not lat:
        for d in range(2):
            for pr in range(n_pairs):
                c_t = jnp.transpose(c_s[d * n_pairs + pr, 0:ML_DV, :])
                for e in range(2):
                    c_o[d, 2 * pr + e] = c_t[e * ML_DK:(e + 1) * ML_DK, :]
            for h in range(ML_HEADS):
                idx = d * n_pairs + h // 2
                hl = slice((h % 2) * ML_DK, (h % 2 + 1) * ML_DK)
                n_o[d, h:h + 1, :] = c_s[idx, ML_DV:ML_DV + 1, hl]
                m_o[d:d + 1, h:h + 1] = m_s[d * ML_HEADS + h:d * ML_HEADS + h + 1, 0:1]

    def phase3(i):
        r0 = i * ROW_BLOCK if isinstance(i, int) else pl.multiple_of(i * ROW_BLOCK, ROW_BLOCK)
        rows = pl.ds(r0, ROW_BLOCK)
        u = u_s[rows, :]

        def proj(c0, width):
            return _dot(u, w_ref[:, c0:c0 + width])

        hm = jnp.concatenate(
            [jnp.concatenate([jnp.transpose(hmt_s[i * chunks_per_block + j, h * ML_DV:(h + 1) * ML_DV, :])
                              for h in range(ML_HEADS)], axis=1)
             for j in range(chunks_per_block)], axis=0)
        ys = (ya_s[rows, :], _sigmoid(om_s[rows, :]) * hm, yw_s[rows, :])
        merged = None
        for b, (yb, zc) in enumerate(zip(ys, (C_ZA, C_ZM, C_ZW))):
            z = proj(zc, BRANCH_W)
            t = (yb * (z * _sigmoid(z))).astype(BF16)
            pb = _sigmoid(proj(C_GMERGE + b * D_MODEL, D_MODEL)) * _dot(t, wb_ref[b])
            merged = pb if merged is None else merged + pb
        o = _dot(merged.astype(BF16), wo_ref[...])
        hres = ALPHA * x_ref[rows, :] + gate * o
        y_ref[rows, :] = _layer_norm(hres) * lng_ref[...] + lnb_ref[...]

    if n_rows == 1:
        phase3(0)
    else:
        def body4(i, carry):
            phase3(i)
            return carry
        lax.fori_loop(0, n_rows, body4, 0)


def _const_spec(shape):
    nd = len(shape)
    return pl.BlockSpec(shape, lambda b: (0,) * nd, pipeline_mode=pl.Buffered(1))


def _layer_spec(shape, layer):
    nd = len(shape) - 1
    return pl.BlockSpec((None,) + tuple(shape[1:]), lambda b: (layer,) + (0,) * nd,
                        pipeline_mode=pl.Buffered(1))


def _layer_call(lat, layer, x, mod, weights, consts, extra):
    B, T, _ = x.shape
    S = T + PAST_LEN if lat else T
    n_chunks = T // CHUNK

    if lat:
        seq_spec = pl.BlockSpec((None, T, D_MODEL), lambda b: (b, 0, 0), pipeline_mode=pl.Buffered(1))
    else:
        seq_spec = pl.BlockSpec((None, T, D_MODEL), lambda b: (b, 0, 0))
    if lat:
        mod_spec = pl.BlockSpec((None, None, 3, D_MODEL), lambda b: (layer, b + 1, 0, 0))
    else:
        mod_spec = pl.BlockSpec((None, None, 3, D_MODEL), lambda b: (layer, 0, 0, 0))
    *vmem_consts, sink = consts
    in_specs = [seq_spec, mod_spec]
    in_specs += [_layer_spec(a.shape, layer) for a in weights]
    in_specs += [_const_spec(a.shape) for a in vmem_consts]
    in_specs.append(pl.BlockSpec(memory_space=pltpu.SMEM))
    args = [x, mod, *weights, *vmem_consts, sink]

    if lat:
        cos, sa, sb, cka, cva, ckw, cvw, sc, sn, sm = extra
        in_specs += [_const_spec(cos.shape)] * 3
        kv_spec = pl.BlockSpec((None, None, N_KV, PAST_LEN, HEAD_DIM), lambda b: (b, layer, 0, 0, 0))
        in_specs += [kv_spec] * 4
        in_specs += [
            pl.BlockSpec((None, None, 2, ML_HEADS, ML_DV, ML_DK), lambda b: (b, layer, 0, 0, 0, 0)),
            pl.BlockSpec((None, None, 2, ML_HEADS, ML_DK), lambda b: (b, layer, 0, 0, 0)),
            pl.BlockSpec((None, None, 2, ML_HEADS), lambda b: (b, layer, 0, 0)),
        ]
        args += [cos, sa, sb, cka, cva, ckw, cvw, sc, sn, sm]
        out_specs = seq_spec
        out_shape = jax.ShapeDtypeStruct((B, T, D_MODEL), F32)
        aliases = {}
    else:
        first_acc = len(args)
        in_specs += [pl.BlockSpec(memory_space=pl.ANY)] * len(extra)
        args += list(extra)
        aliases = {first_acc + i: 1 + i for i in range(len(extra))}
        kv_out = pl.BlockSpec((None, None, N_KV, HEAD_DIM, T), lambda b: (b, layer, 0, 0, 0))
        out_specs = [
            seq_spec, kv_out, kv_out, kv_out, kv_out,
            pl.BlockSpec((None, None, 2, ML_HEADS, ML_DK, ML_DV), lambda b: (b, layer, 0, 0, 0, 0)),
            pl.BlockSpec((None, 2, ML_HEADS, ML_DK), lambda b: (b, 0, 0, 0)),
            pl.BlockSpec((None, 2, ML_HEADS), lambda b: (b, 0, 0)),
        ]
        out_shape = [jax.ShapeDtypeStruct((B, T, D_MODEL), F32)]
        out_shape += [jax.ShapeDtypeStruct(a.shape, a.dtype) for a in extra]
        out_shape += [
            jax.ShapeDtypeStruct((B, 2, ML_HEADS, ML_DK), F32),
            jax.ShapeDtypeStruct((B, 2, ML_HEADS), F32),
        ]

    scratch = [
        ((T, D_MODEL), BF16),
        ((N_HEADS, T, HEAD_DIM), BF16),
        ((N_HEADS, T, HEAD_DIM), BF16),
        ((N_KV, S, HEAD_DIM), BF16),
        ((S // CHUNK, N_KV, VT_ROWS, CHUNK), BF16),
        ((N_KV, T, HEAD_DIM), BF16),
        ((n_chunks, N_KV, VT_ROWS, CHUNK), BF16),
        ((T, ML_HEADS * ML_DK), BF16),
        ((T, ML_HEADS * ML_DK), BF16),
        ((n_chunks, ML_HEADS, ST_ROWS, CHUNK), BF16),
        ((T, ML_HEADS * ML_DV), F32),
        ((T, LANES), F32),
        ((T, BRANCH_W), F32),
        ((T, BRANCH_W), F32),
        ((n_chunks, ML_HEADS * ML_DV, CHUNK), F32),
        ((ML_HEADS, ST_ROWS, 2 * ML_DK), F32),
        ((2 * ML_HEADS, LANES), F32),
    ]
    if lat:
        scratch += [((N_KV, PAST_LEN, HEAD_DIM), BF16),
                    ((N_KV, VT_ROWS, PAST_LEN), BF16)]
    scratch = [pltpu.VMEM(shape, dtype) for shape, dtype in scratch]

    return pl.pallas_call(
        functools.partial(_layer_kernel, lat, T, layer),
        grid=(B,),
        in_specs=in_specs,
        out_specs=out_specs,
        out_shape=out_shape,
        scratch_shapes=scratch,
        input_output_aliases=aliases,
        compiler_params=pltpu.CompilerParams(
            dimension_semantics=("arbitrary",), vmem_limit_bytes=VMEM_LIMIT_BYTES),
        name=("latent_layer" if lat else "context_layer") + str(layer),
    )(*args)


def _rope_tables(T):
    rows = T // GRID_W
    row = jnp.repeat(jnp.arange(rows, dtype=F32), GRID_W)
    col = jnp.tile(jnp.arange(GRID_W, dtype=F32), rows)
    inv = ROPE_THETA ** (-jnp.arange(0, AXIS_DIM, 2, dtype=F32) / AXIS_DIM)
    ar = row[:, None] * inv
    ac = col[:, None] * inv
    ang = jnp.concatenate([ar, ar, ac, ac], axis=-1)
    ang = jnp.concatenate([ang, ang], axis=-1)
    first_half = (jnp.arange(LANES) % AXIS_DIM) < (AXIS_DIM // 2)
    cos = jnp.cos(ang)
    sin = jnp.sin(ang)
    sa = jnp.where(first_half[None, :], -sin, 0.0)
    sb = jnp.where(first_half[None, :], 0.0, sin)
    return cos, sa, sb


def _constants(sink_logit):
    grp = np.arange(N_HEADS * HEAD_DIM) // HEAD_DIM
    bd = jnp.asarray(grp[:, None] == grp[None, :], dtype=BF16)
    ti = np.arange(CHUNK)
    tril = jnp.asarray(ti[None, :] <= ti[:, None], dtype=BF16)
    triu = jnp.asarray(ti[None, :] >= ti[:, None], dtype=BF16)
    lane = np.arange(LANES)[:, None]
    blk = (np.arange(ML_HEADS * CHUNK) // CHUNK)[None, :]
    sel = jnp.asarray(np.stack([lane == 2 * ML_HEADS * d + blk for d in range(2)]), dtype=BF16)
    eye = jnp.asarray(np.eye(HEAD_DIM), dtype=BF16)
    return bd, tril, triu, sel, eye, sink_logit.reshape(-1)


def kernel(x_prompt, x_sample, cache_attn_k, cache_attn_v, cache_win_k, cache_win_v, state_mlstm_C,
           state_mlstm_n, state_mlstm_m, c, c_ctx, w_mod, b_mod, w_in, qk_gain, sink_logit,
           mlstm_gate_bias, w_branch, w_out, ln_gain, ln_bias):
    dec_b = x_sample.shape[0]
    assert dec_b + 1 <= 8
    cond = jnp.concatenate([c_ctx[None, :], c, jnp.zeros((8 - 1 - dec_b, D_MODEL), F32)], axis=0)
    mod = _modulation(cond, w_mod, b_mod).reshape(DEPTH, 8, 3, D_MODEL)

    consts = _constants(sink_logit)
    cos, sa, sb = _rope_tables(x_sample.shape[1])
    lane_pad = jnp.zeros((DEPTH, 1, LANES - N_GATES), F32)
    weights = (
        _prep_in_weight(jnp.swapaxes(w_in, 1, 2)), w_branch.astype(BF16), w_out.astype(BF16),
        jnp.tile(qk_gain[:, 0:1, :], (1, 1, N_HEADS)),
        jnp.tile(qk_gain[:, 1:2, :], (1, 1, N_KV)),
        jnp.concatenate([mlstm_gate_bias.reshape(DEPTH, 1, N_GATES), lane_pad], axis=2),
        ln_gain[:, None, :], ln_bias[:, None, :],
    )

    B, T, _ = x_prompt.shape
    kv_t = jnp.zeros((B, DEPTH, N_KV, HEAD_DIM, T), F32)
    ctx = (kv_t, kv_t, kv_t, kv_t, jnp.zeros((B, DEPTH, 2, ML_HEADS, ML_DK, ML_DV), F32))
    xp, xs = x_prompt, x_sample
    small = []
    for l in range(DEPTH):
        outs = _layer_call(False, l, xp, mod, weights, consts, ctx)
        xp = outs[0]
        ctx = tuple(outs[1:6])
        small.append(outs[6:])
        extra = (cos, sa, sb, cache_attn_k, cache_attn_v, cache_win_k, cache_win_v,
                 state_mlstm_C, state_mlstm_n, state_mlstm_m)
        xs = _layer_call(True, l, xs, mod, weights, consts, extra)
    new_n, new_m = [jnp.stack([small[l][i] for l in range(DEPTH)], axis=1) for i in range(2)]
    return (xp, xs, *[jnp.swapaxes(a, -1, -2) for a in ctx], new_n, new_m)
```

```python
import functools

import jax
import jax.numpy as jnp
import numpy as np
from jax import lax
from jax.experimental import pallas as pl
from jax.experimental.pallas import tpu as pltpu

F32 = jnp.float32
BF16 = jnp.bfloat16

D_MODEL = 1024
DEPTH = 2
PAST_LEN = 256
GRID_W = 64
HEAD_DIM = 64
N_HEADS = 8
N_KV = 2
N_GROUP = N_HEADS // N_KV
WINDOW = 128
ML_HEADS = 4
ML_DK = 64
ML_DV = 128
CHUNK = 128
BRANCH_W = 512
ROPE_THETA = 10000.0
AXIS_DIM = HEAD_DIM // 2
LN_EPS = 1e-6
RMS_EPS = 1e-6
ALPHA = (2.0 * DEPTH) ** 0.25
LOG2E = float(np.log2(np.e))
LANES = 128
BF16_ROWS = 16
ROW_BLOCK = 256
CTX_SEQS_PER_STEP = 2
SEQ_STAGGER = 9
CTX_ALIASED = 5
VMEM_LIMIT_BYTES = 60 * 1024 * 1024
VT_ROWS = HEAD_DIM + BF16_ROWS
ST_ROWS = ML_DV + BF16_ROWS

_SIZES = (512, 128, 128, 512, 256, 256, 512, 512, 128, 512, 512, 128, 128, 512, 3072)
_OFF = [int(v) for v in np.concatenate([[0], np.cumsum(_SIZES)])]
(C_QA, C_KA, C_VA, C_ZA, C_QM, C_KM, C_VM, C_OM, C_GT, C_ZM, C_QW, C_KW, C_VW, C_ZW, C_GMERGE) = _OFF[:-1]
W_COLS = _OFF[-1]
GATE_COL = 2816
N_GATES = 4 * ML_HEADS


def _dot(a, b):
    return jnp.dot(a, b, preferred_element_type=F32)


def _dot_tb(a, b):
    return lax.dot_general(a, b, (((1,), (1,)), ((), ())), preferred_element_type=F32)


def _split(a):
    hi = a.astype(BF16)
    lo = (a - hi.astype(F32)).astype(BF16)
    return hi, lo


def _split3(a):
    hi = a.astype(BF16)
    r = a - hi.astype(F32)
    mid = r.astype(BF16)
    lo = (r - mid.astype(F32)).astype(BF16)
    return hi, mid, lo


def _layer_norm(x):
    mu = jnp.mean(x, axis=-1, keepdims=True)
    xc = x - mu
    var = jnp.mean(xc * xc, axis=-1, keepdims=True)
    return xc * lax.rsqrt(var + LN_EPS)


def _log_sigmoid(x):
    return jnp.minimum(x, 0.0) - jnp.log(1.0 + jnp.exp(-jnp.abs(x)))


def _sigmoid(x):
    return jax.nn.sigmoid(x)


def _mod_kernel(c_ref, w_ref, b_ref, o_ref):
    c = c_ref[...]
    s = c * _sigmoid(c)
    s_hi, s_lo = _split(s)
    w = w_ref[...]
    w_hi, w_lo = _split(w)
    o_ref[...] = _dot(s_hi, w_hi) + _dot(s_lo, w_hi) + _dot(s_hi, w_lo) + b_ref[...]


def _modulation(cond, w_mod, b_mod):
    rows = cond.shape[0]
    tn = 1024
    return pl.pallas_call(
        _mod_kernel,
        grid=(DEPTH, 3 * D_MODEL // tn),
        in_specs=[
            pl.BlockSpec((rows, D_MODEL), lambda l, j: (0, 0)),
            pl.BlockSpec((None, D_MODEL, tn), lambda l, j: (l, 0, j)),
            pl.BlockSpec((None, 1, tn), lambda l, j: (l, 0, j)),
        ],
        out_specs=pl.BlockSpec((None, rows, tn), lambda l, j: (l, 0, j)),
        out_shape=jax.ShapeDtypeStruct((DEPTH, rows, 3 * D_MODEL), F32),
        compiler_params=pltpu.CompilerParams(dimension_semantics=("arbitrary", "arbitrary")),
        name="adaln_modulation",
    )(cond, w_mod, b_mod.reshape(DEPTH, 1, 3 * D_MODEL))


PREP_K = 256


def _prep_kernel(wt_ref, o_ref):
    lane = lax.broadcasted_iota(jnp.int32, (PREP_K, LANES), 1)
    for c0 in range(0, W_COLS, LANES):
        r0 = c0 if c0 <= GATE_COL else c0 - (LANES - N_GATES)
        slab = jnp.transpose(wt_ref[r0:r0 + LANES, :])
        if c0 == GATE_COL:
            slab = jnp.where(lane < N_GATES, slab, 0.0)
        o_ref[:, c0:c0 + LANES] = slab.astype(BF16)


def _prep_in_weight(w_in_t):
    n_in = w_in_t.shape[1]
    return pl.pallas_call(
        _prep_kernel,
        grid=(DEPTH, D_MODEL // PREP_K),
        in_specs=[pl.BlockSpec((None, n_in, PREP_K), lambda l, i: (l, 0, i))],
        out_specs=pl.BlockSpec((None, PREP_K, W_COLS), lambda l, i: (l, i, 0)),
        out_shape=jax.ShapeDtypeStruct((DEPTH, D_MODEL, W_COLS), BF16),
        compiler_params=pltpu.CompilerParams(
            dimension_semantics=("arbitrary", "arbitrary"), vmem_limit_bytes=VMEM_LIMIT_BYTES),
        name="in_weight_prep",
    )(w_in_t)


def _rope(x, cos, sa, sb):
    return x * cos + pltpu.roll(x, LANES - AXIS_DIM // 2, 1) * sa + pltpu.roll(x, AXIS_DIM // 2, 1) * sb


def _group_rms(x, bd, gain):
    hi, lo = _split(x * x)
    ms = (_dot(hi, bd) + _dot(lo, bd)) * (1.0 / HEAD_DIM)
    return x * lax.rsqrt(ms + RMS_EPS) * gain


def _ones_row_tile():
    r = lax.broadcasted_iota(jnp.int32, (BF16_ROWS, LANES), 0)
    return jnp.where(r == 0, 1.0, 0.0).astype(BF16)


def _attend_chains(chains, results):
    state = []
    for qs, tiles, sink_row in chains:
        m_cols = qs.shape[0]
        if sink_row is None:
            m = jnp.full((1, m_cols), -jnp.inf, F32)
            acc = jnp.zeros((VT_ROWS, m_cols), F32)
        else:
            m = sink_row
            r = lax.broadcasted_iota(jnp.int32, (VT_ROWS, m_cols), 0)
            acc = jnp.where(r == HEAD_DIM, 1.0, 0.0)
        cur = tiles[0]()
        state.append([m, acc, cur, _dot_tb(cur[0], qs)])
    yield
    for t in range(max(len(tiles) for _, tiles, _ in chains)):
        for st, (qs, tiles, _) in zip(state, chains):
            if t >= len(tiles):
                continue
            m, acc, (_, v_t, valid), s = st
            if valid is not None:
                s = jnp.where(valid, s, -jnp.inf)
            if t + 1 < len(tiles):
                nxt = tiles[t + 1]()
                st[2], st[3] = nxt, _dot_tb(nxt[0], qs)
            m_new = jnp.maximum(m, jnp.max(s, axis=0, keepdims=True))
            p = jnp.exp2(s - m_new).astype(BF16)
            st[1] = acc * jnp.exp2(m - m_new) + _dot(v_t, p)
            st[0] = m_new
        yield
    results.extend(st[1][0:HEAD_DIM, :] / st[1][HEAD_DIM:HEAD_DIM + 1, :] for st in state)


def _interleave(weighted):
    live = [[g, n] for g, n in weighted]
    while live:
        for item in list(live):
            for _ in range(item[1]):
                try:
                    next(item[0])
                except StopIteration:
                    live.remove(item)
                    break
        yield


def _run(gen):
    for _ in gen:
        pass


def _delayed(gen, rounds):
    for _ in range(rounds):
        yield
    yield from gen


def _layer_kernel(lat, T, layer, nseq, *refs):
    if nseq == 1:
        _run(_sequence(lat, T, layer, *refs))
        return
    n_shared = 16 + CTX_ALIASED
    seqs = []
    for sq in range(nseq):
        per_seq = [refs[0].at[sq]] + list(refs[1:n_shared]) + [r.at[sq] for r in refs[n_shared:]]
        seqs.append(_sequence(lat, T, layer, *per_seq))
    _run(_interleave([(_delayed(g, sq * SEQ_STAGGER), 1) for sq, g in enumerate(seqs)]))


def _sequence(lat, T, layer, *refs):
    refs = list(refs)
    x_ref, mod_ref, w_ref, wb_ref, wo_ref = refs[:5]
    g0_ref, g1_ref, gb_ref, lng_ref, lnb_ref = refs[5:10]
    bd_ref, tril_ref, triu_ref, sel_ref, eye_ref, sink_ref = refs[10:16]
    pos = 16
    if lat:
        cos_ref, sa_ref, sb_ref = refs[pos:pos + 3]
        cka_ref, cva_ref, ckw_ref, cvw_ref, c0_ref, n0_ref, m0_ref = refs[pos + 3:pos + 10]
        pos += 10
        y_ref = refs[pos]
        pos += 1
    else:
        pos += CTX_ALIASED
        y_ref, ka_o, va_o, kw_o, vw_o, c_o, n_o, m_o = refs[pos:pos + 8]
        pos += 8
    (u_s, qa_s, qw_s, ka_s, vat_s, kw_s, vwt_s, qm_s, km_s, vmt_s, om_s, g_s,
     ya_s, yw_s, hmt_s, c_s, m_s) = refs[pos:pos + 17]
    pos += 17
    if lat:
        kwc_s, vwct_s = refs[pos:pos + 2]

    n_rows = T // ROW_BLOCK
    n_chunks = T // CHUNK
    chunks_per_block = ROW_BLOCK // CHUNK
    shift = mod_ref[0:1, :]
    scale = mod_ref[1:2, :]
    gate = mod_ref[2:3, :]
    ones_tile = _ones_row_tile()

    def phase1(i):
        R = ROW_BLOCK
        r0 = i * R if isinstance(i, int) else pl.multiple_of(i * R, R)
        rows = pl.ds(r0, R)
        x = x_ref[rows, :]
        u = (_layer_norm(x) * (1.0 + scale) + shift).astype(BF16)
        u_s[rows, :] = u
        yield

        def proj(c0, width):
            return _dot(u, w_ref[:, c0:c0 + width])

        if lat:
            cos = cos_ref[rows, :]
            sa = sa_ref[rows, :]
            sb = sb_ref[rows, :]

        def rope_wide(v):
            if not lat:
                return v
            slabs = [_rope(v[:, j:j + LANES], cos, sa, sb) for j in range(0, v.shape[1], LANES)]
            return slabs[0] if len(slabs) == 1 else jnp.concatenate(slabs, axis=1)

        qscale = LOG2E * HEAD_DIM ** -0.5
        qa = rope_wide(_group_rms(proj(C_QA, 512), bd_ref[...], g0_ref[...])) * qscale
        for h in range(N_HEADS):
            qa_s[h, rows, :] = qa[:, h * HEAD_DIM:(h + 1) * HEAD_DIM].astype(BF16)
        yield
        ka_n = _group_rms(proj(C_KA, LANES), bd_ref[0:LANES, 0:LANES], g1_ref[...])
        ka = rope_wide(ka_n)
        va = proj(C_VA, LANES)
        qw = rope_wide(proj(C_QW, 512)) * qscale
        for h in range(N_HEADS):
            qw_s[h, rows, :] = qw[:, h * HEAD_DIM:(h + 1) * HEAD_DIM].astype(BF16)
        yield
        kw_raw = proj(C_KW, LANES)
        kw = rope_wide(kw_raw)
        vw = proj(C_VW, LANES)
        for kv in range(N_KV):
            sl = slice(kv * HEAD_DIM, (kv + 1) * HEAD_DIM)
            ka_s[kv, rows, :] = ka[:, sl].astype(BF16)
            kw_s[kv, rows, :] = kw[:, sl].astype(BF16)
        yield
        qm_s[rows, :] = proj(C_QM, 256).astype(BF16)
        km_s[rows, :] = (proj(C_KM, 256) * (ML_DK ** -0.5)).astype(BF16)
        yield
        vm = proj(C_VM, 512)
        yield
        om_s[rows, :] = proj(C_OM, 512)
        g_s[rows, :] = proj(C_GT, LANES) + gb_ref[...]
        yield
        for j in range(chunks_per_block):
            cj = i * chunks_per_block + j
            cr = slice(j * CHUNK, (j + 1) * CHUNK)
            va_t = jnp.transpose(va[cr, :])
            vw_t = jnp.transpose(vw[cr, :])
            for kv in range(N_KV):
                hd = slice(kv * HEAD_DIM, (kv + 1) * HEAD_DIM)
                vat_s[cj, kv, 0:HEAD_DIM, :] = va_t[hd, :].astype(BF16)
                vat_s[cj, kv, HEAD_DIM:VT_ROWS, :] = ones_tile
                vwt_s[cj, kv, 0:HEAD_DIM, :] = vw_t[hd, :].astype(BF16)
                vwt_s[cj, kv, HEAD_DIM:VT_ROWS, :] = ones_tile
            if not lat:
                tc = slice(i * ROW_BLOCK + j * CHUNK, i * ROW_BLOCK + (j + 1) * CHUNK)
                ka_t = jnp.transpose(ka_n[cr, :])
                kw_t = jnp.transpose(kw_raw[cr, :])
                for kv in range(N_KV):
                    hd = slice(kv * HEAD_DIM, (kv + 1) * HEAD_DIM)
                    ka_o[kv, :, tc] = ka_t[hd, :]
                    va_o[kv, :, tc] = va_t[hd, :]
                    kw_o[kv, :, tc] = kw_t[hd, :]
                    vw_o[kv, :, tc] = vw_t[hd, :]
            for h in range(ML_HEADS):
                vmt_s[cj, h, 0:ML_DV, :] = jnp.transpose(vm[cr, h * ML_DV:(h + 1) * ML_DV]).astype(BF16)
                vmt_s[cj, h, ML_DV:ST_ROWS, :] = ones_tile
            yield

    if n_rows == 1:
        yield from phase1(0)
    else:
        def body1(i, carry):
            _run(phase1(i))
            return carry
        lax.fori_loop(0, n_rows, body1, 0)

    if lat:
        eye = eye_ref[...]
        for kv in range(N_KV):
            ka_s[kv, T:T + PAST_LEN, :] = cka_ref[kv].astype(BF16)
            kwc_s[kv] = ckw_ref[kv].astype(BF16)
            cva_t = _dot_tb(eye, cva_ref[kv].astype(BF16)).astype(BF16)
            cvw_t = _dot_tb(eye, cvw_ref[kv].astype(BF16)).astype(BF16)
            for j in range(PAST_LEN // CHUNK):
                vat_s[n_chunks + j, kv, 0:HEAD_DIM, :] = cva_t[:, j * CHUNK:(j + 1) * CHUNK]
                vat_s[n_chunks + j, kv, HEAD_DIM:VT_ROWS, :] = ones_tile
                vwct_s[kv, 0:HEAD_DIM, j * CHUNK:(j + 1) * CHUNK] = cvw_t[:, j * CHUNK:(j + 1) * CHUNK]
                vwct_s[kv, HEAD_DIM:VT_ROWS, j * CHUNK:(j + 1) * CHUNK] = ones_tile

    tq = CHUNK
    m_cols = N_GROUP * tq

    def sink_row(kv):
        col = lax.broadcasted_iota(jnp.int32, (1, m_cols), 1)
        row = jnp.full((1, m_cols), sink_ref[layer * N_HEADS + kv * N_GROUP], F32)
        for g in range(1, N_GROUP):
            row = jnp.where(col >= g * tq, sink_ref[layer * N_HEADS + kv * N_GROUP + g], row)
        return row * LOG2E

    def load_q(q_s, kv, q0):
        return q_s[kv * N_GROUP:(kv + 1) * N_GROUP, pl.ds(q0, tq), :].reshape(m_cols, HEAD_DIM)

    def store_heads(dst, q0, kv, o_t):
        for p in range(N_GROUP // 2):
            blk = jnp.concatenate([o_t[:, (2 * p) * tq:(2 * p + 1) * tq],
                                   o_t[:, (2 * p + 1) * tq:(2 * p + 2) * tq]], axis=0)
            c0 = (kv * N_GROUP + 2 * p) * HEAD_DIM
            dst[pl.ds(q0, tq), c0:c0 + LANES] = jnp.transpose(blk)

    def key_tiles(k_ref, vt_ref, kv, n_key_chunks, chunks_per_tile):
        def tile(c0):
            def load():
                k = k_ref[kv, c0 * CHUNK:(c0 + chunks_per_tile) * CHUNK, :]
                v_t = [vt_ref[c0 + j, kv] for j in range(chunks_per_tile)]
                return k, (v_t[0] if len(v_t) == 1 else jnp.concatenate(v_t, axis=1)), None
            return load
        return [tile(c0) for c0 in range(0, n_key_chunks, chunks_per_tile)]

    def band_tiles(kv, n, q0):
        j0 = jnp.clip(n - 1, 0, n_chunks - 3)
        w0 = pl.multiple_of(j0 * CHUNK, CHUNK)

        def band(c0, nc):
            def load():
                rows = nc * CHUNK
                kpos = w0 + c0 * CHUNK + lax.broadcasted_iota(jnp.int32, (rows, m_cols), 0)
                qpos = q0 + (lax.broadcasted_iota(jnp.int32, (rows, m_cols), 1) & (tq - 1))
                v_t = [vwt_s[j0 + c0 + j, kv] for j in range(nc)]
                return (kw_s[kv, pl.ds(w0 + c0 * CHUNK, rows), :],
                        v_t[0] if nc == 1 else jnp.concatenate(v_t, axis=1),
                        jnp.abs(kpos - qpos) <= WINDOW)
            return load
        return [band(0, 2), band(2, 1), lambda: (kwc_s[kv], vwct_s[kv], None)]

    def attend(n, q0, chunks_per_tile):
        chains, dsts = [], []
        n_key_chunks = ka_s.shape[1] // CHUNK
        for kv in range(N_KV):
            chains.append((load_q(qa_s, kv, q0), key_tiles(ka_s, vat_s, kv, n_key_chunks, chunks_per_tile), None))
            dsts.append((ya_s, kv))
        for kv in range(N_KV):
            tiles = band_tiles(kv, n, q0) if lat else key_tiles(kw_s, vwt_s, kv, n_chunks, chunks_per_tile)
            chains.append((load_q(qw_s, kv, q0), tiles, sink_row(kv)))
            dsts.append((yw_s, kv))
        results = []
        yield from _attend_chains(chains, results)
        for (dst, kv), o_t in zip(dsts, results):
            store_heads(dst, q0, kv, o_t)

    n_pairs = ML_HEADS // 2
    hmt_s[...] = jnp.zeros_like(hmt_s)
    for d in range(2):
        for h in range(ML_HEADS):
            idx = d * n_pairs + h // 2
            hl = slice((h % 2) * ML_DK, (h % 2 + 1) * ML_DK)
            c_s[idx, ML_DV:ST_ROWS, hl] = jnp.zeros((BF16_ROWS, ML_DK), F32)
            if lat:
                c_s[idx, 0:ML_DV, hl] = c0_ref[d, h]
                c_s[idx, ML_DV:ML_DV + 1, hl] = n0_ref[d, h:h + 1, :]
                m_s[d * ML_HEADS + h:d * ML_HEADS + h + 1, :] = jnp.broadcast_to(
                    m0_ref[d:d + 1, h:h + 1], (1, LANES))
            else:
                c_s[idx, 0:ML_DV, hl] = jnp.zeros((ML_DV, ML_DK), F32)
                m_s[d * ML_HEADS + h:d * ML_HEADS + h + 1, :] = jnp.zeros((1, LANES), F32)

    L = CHUNK
    s_idx = lax.broadcasted_iota(jnp.int32, (L, ML_HEADS * L), 0)
    t_idx = lax.broadcasted_iota(jnp.int32, (L, ML_HEADS * L), 1) & (L - 1)
    lane_row = lax.broadcasted_iota(jnp.int32, (1, LANES), 1)
    low_half = lax.broadcasted_iota(jnp.int32, (L, LANES), 1) < ML_DK
    ones_ll = jnp.ones((L, L), BF16)
    zeros_ll = jnp.zeros((L, L), BF16)

    def heads_row(src, r0, c0=0):
        return jnp.concatenate([src[r0 + h:r0 + h + 1, c0:c0 + L] for h in range(ML_HEADS)], axis=1)

    def block_diag_rows(x):
        zero = jnp.zeros_like(x)
        return jnp.concatenate([jnp.where(low_half, x, zero), jnp.where(low_half, zero, x)], axis=0)

    def mlstm_streams(streams):
        m_state = {d: heads_row(m_s, d * ML_HEADS) for d in sorted({d for d, _ in streams})}
        c_state = {d: [c_s[d * n_pairs + pr] for pr in range(n_pairs)] for d in m_state}
        work = []
        for d, cc in streams:
            r0 = cc * L if isinstance(cc, int) else pl.multiple_of(cc * L, L)
            rows = pl.ds(r0, L)
            g = g_s[rows, :]
            f_hi, f_lo = _split(_log_sigmoid(g))
            tri_c = tril_ref[...] if d == 0 else triu_ref[...]
            tri_r = triu_ref[...] if d == 0 else tril_ref[...]
            cum = _dot(tri_c, f_hi) + _dot(tri_c, f_lo)
            r = g - pltpu.roll(cum, LANES - ML_HEADS, 1)
            r1, r2, r3 = _split3(r)
            sel = sel_ref[d]
            rb = _dot(r1, sel) + _dot(r2, sel) + _dot(r3, sel)
            g_t = jnp.transpose(g)[0:N_GATES, :]
            ft_hi, ft_lo = _split(_log_sigmoid(g_t))
            tr = jnp.concatenate([tri_r, ones_ll], axis=1)
            ct = _dot(ft_hi, tr) + _dot(ft_lo, tr)
            work.append(dict(d=d, cc=cc, rows=rows, rb=rb, g_t=g_t, ct=ct))
            yield
        for w in work:
            d = w["d"]
            valid = (s_idx <= t_idx) if d == 0 else (s_idx >= t_idx)
            gi0 = 2 * ML_HEADS * d
            gf0 = gi0 + ML_HEADS
            b_row = heads_row(w["ct"], gf0)
            b_last = heads_row(w["ct"], gf0, L)
            i_row = heads_row(w["g_t"], gi0)
            m_prev = m_state[d]
            a_row = b_row + m_prev
            dm = jnp.where(valid, b_row + w["rb"], -jnp.inf)
            mt = jnp.maximum(a_row, jnp.max(dm, axis=0, keepdims=True))
            w["p"] = jnp.exp(dm - mt)
            w["w_inter"] = jnp.exp(a_row - mt)
            w["floor"] = jnp.exp(-mt)
            g_row = b_last - b_row + i_row
            g_max = jnp.concatenate(
                [jnp.broadcast_to(jnp.max(g_row[:, h * L:(h + 1) * L], axis=-1, keepdims=True), (1, L))
                 for h in range(ML_HEADS)], axis=1)
            m_new = jnp.maximum(b_last + m_prev, g_max)
            w["ws"] = jnp.exp(g_row - m_new)
            w["wc"] = jnp.exp(b_last + m_prev - m_new)
            m_state[d] = m_new
            yield
        for w in work:
            rows, cc = w["rows"], w["cc"]
            w["pairs"] = []
            for pr in range(n_pairs):
                lanes = slice(pr * LANES, (pr + 1) * LANES)
                cols = slice(pr * 2 * L, (pr + 1) * 2 * L)
                k_pair = km_s[rows, lanes]
                q_bd = block_diag_rows(qm_s[rows, lanes])
                k_bd = block_diag_rows(k_pair)
                s_t = (_dot_tb(k_pair, q_bd) * w["p"][:, cols]).astype(BF16)
                s_bd = jnp.concatenate(
                    [jnp.concatenate([s_t[:, 0:L], zeros_ll], axis=1),
                     jnp.concatenate([zeros_ll, s_t[:, L:2 * L]], axis=1)], axis=0)
                v_t = jnp.concatenate([vmt_s[cc, 2 * pr], vmt_s[cc, 2 * pr + 1]], axis=1)
                intra = _dot(v_t, s_bd)
                wv = (v_t.astype(F32) * w["ws"][:, cols]).astype(BF16)
                w["pairs"].append((q_bd, intra, _dot(wv, k_bd)))
            yield
        for w in work:
            d, cc = w["d"], w["cc"]
            for pr, (q_bd, intra, update) in enumerate(w["pairs"]):
                cols = slice(pr * 2 * L, (pr + 1) * 2 * L)
                state = c_state[d][pr]
                tot = intra + w["w_inter"][:, cols] * _dot_tb(state.astype(BF16), q_bd)
                h_t = tot[0:ML_DV, :] / jnp.maximum(jnp.abs(tot[ML_DV:ML_DV + 1, :]), w["floor"][:, cols])
                for e in range(2):
                    hr = slice((2 * pr + e) * ML_DV, (2 * pr + e + 1) * ML_DV)
                    hmt_s[cc, hr, :] += h_t[:, e * L:(e + 1) * L]
                wc = w["wc"]
                wc_pair = jnp.where(lane_row < ML_DK, wc[:, 2 * pr * L:(2 * pr + 1) * L],
                                    wc[:, (2 * pr + 1) * L:(2 * pr + 2) * L])
                c_state[d][pr] = wc_pair * state + update
            yield
        for d in m_state:
            for h in range(ML_HEADS):
                m_s[d * ML_HEADS + h:d * ML_HEADS + h + 1, :] = m_state[d][:, h * L:(h + 1) * L]
            for pr in range(n_pairs):
                c_s[d * n_pairs + pr] = c_state[d][pr]

    def scan_streams(cs):
        return [(d, c if d == 0 else n_chunks - 1 - c) for c in cs for d in range(2)]

    if lat:
        def body2(n, carry):
            _run(_interleave([(attend(n, pl.multiple_of(n * tq, tq), 2), 1),
                              (mlstm_streams(scan_streams([n])), 2)]))
            return carry
        lax.fori_loop(0, n_chunks, body2, 0)
    else:
        def all_tiles():
            for n in range(T // tq):
                yield from attend(n, n * tq, 1)
        yield from _interleave([(all_tiles(), 1), (mlstm_streams(scan_streams(range(n_chunks))), 3)])

    if not lat:
        for d in range(2):
            for pr in range(n_pairs):
                c_t = jnp.transpose(c_s[d * n_pairs + pr, 0:ML_DV, :])
                for e in range(2):
                    c_o[d, 2 * pr + e] = c_t[e * ML_DK:(e + 1) * ML_DK, :]
            for h in range(ML_HEADS):
                idx = d * n_pairs + h // 2
                hl = slice((h % 2) * ML_DK, (h % 2 + 1) * ML_DK)
                n_o[d, h:h + 1, :] = c_s[idx, ML_DV:ML_DV + 1, hl]
                m_o[d:d + 1, h:h + 1] = m_s[d * ML_HEADS + h:d * ML_HEADS + h + 1, 0:1]

    def phase3(i):
        r0 = i * ROW_BLOCK if isinstance(i, int) else pl.multiple_of(i * ROW_BLOCK, ROW_BLOCK)
        rows = pl.ds(r0, ROW_BLOCK)
        u = u_s[rows, :]

        def proj(c0, width):
            return _dot(u, w_ref[:, c0:c0 + width])

        hm = jnp.concatenate(
            [jnp.concatenate([jnp.transpose(hmt_s[i * chunks_per_block + j, h * ML_DV:(h + 1) * ML_DV, :])
                              for h in range(ML_HEADS)], axis=1)
             for j in range(chunks_per_block)], axis=0)
        ys = (ya_s[rows, :], _sigmoid(om_s[rows, :]) * hm, yw_s[rows, :])
        yield
        merged = None
        for b, (yb, zc) in enumerate(zip(ys, (C_ZA, C_ZM, C_ZW))):
            z = proj(zc, BRANCH_W)
            t = (yb * (z * _sigmoid(z))).astype(BF16)
            yield
            pb = _sigmoid(proj(C_GMERGE + b * D_MODEL, D_MODEL)) * _dot(t, wb_ref[b])
            merged = pb if merged is None else merged + pb
            yield
        o = _dot(merged.astype(BF16), wo_ref[...])
        yield
        hres = ALPHA * x_ref[rows, :] + gate * o
        y_ref[rows, :] = _layer_norm(hres) * lng_ref[...] + lnb_ref[...]
        yield

    if n_rows == 1:
        yield from phase3(0)
    else:
        def body4(i, carry):
            _run(phase3(i))
            return carry
        lax.fori_loop(0, n_rows, body4, 0)


def _const_spec(shape):
    nd = len(shape)
    return pl.BlockSpec(shape, lambda b: (0,) * nd, pipeline_mode=pl.Buffered(1))


def _layer_spec(shape, layer):
    nd = len(shape) - 1
    return pl.BlockSpec((None,) + tuple(shape[1:]), lambda b: (layer,) + (0,) * nd,
                        pipeline_mode=pl.Buffered(1))


def _layer_call(lat, layer, x, mod, weights, consts, extra):
    B, T, _ = x.shape
    S = T + PAST_LEN if lat else T
    n_chunks = T // CHUNK
    nseq = 1 if lat else CTX_SEQS_PER_STEP
    lead = None if nseq == 1 else nseq

    if lat:
        seq_spec = pl.BlockSpec((None, T, D_MODEL), lambda b: (b, 0, 0), pipeline_mode=pl.Buffered(1))
    else:
        seq_spec = pl.BlockSpec((lead, T, D_MODEL), lambda b: (b, 0, 0))
    if lat:
        mod_spec = pl.BlockSpec((None, None, 3, D_MODEL), lambda b: (layer, b + 1, 0, 0))
    else:
        mod_spec = pl.BlockSpec((None, None, 3, D_MODEL), lambda b: (layer, 0, 0, 0))
    *vmem_consts, sink = consts
    in_specs = [seq_spec, mod_spec]
    in_specs += [_layer_spec(a.shape, layer) for a in weights]
    in_specs += [_const_spec(a.shape) for a in vmem_consts]
    in_specs.append(pl.BlockSpec(memory_space=pltpu.SMEM))
    args = [x, mod, *weights, *vmem_consts, sink]

    if lat:
        cos, sa, sb, cka, cva, ckw, cvw, sc, sn, sm = extra
        in_specs += [_const_spec(cos.shape)] * 3
        kv_spec = pl.BlockSpec((None, None, N_KV, PAST_LEN, HEAD_DIM), lambda b: (b, layer, 0, 0, 0))
        in_specs += [kv_spec] * 4
        in_specs += [
            pl.BlockSpec((None, None, 2, ML_HEADS, ML_DV, ML_DK), lambda b: (b, layer, 0, 0, 0, 0)),
            pl.BlockSpec((None, None, 2, ML_HEADS, ML_DK), lambda b: (b, layer, 0, 0, 0)),
            pl.BlockSpec((None, None, 2, ML_HEADS), lambda b: (b, layer, 0, 0)),
        ]
        args += [cos, sa, sb, cka, cva, ckw, cvw, sc, sn, sm]
        out_specs = seq_spec
        out_shape = jax.ShapeDtypeStruct((B, T, D_MODEL), F32)
        aliases = {}
    else:
        first_acc = len(args)
        in_specs += [pl.BlockSpec(memory_space=pl.ANY)] * len(extra)
        args += list(extra)
        aliases = {first_acc + i: 1 + i for i in range(len(extra))}
        assert len(extra) == CTX_ALIASED
        kv_out = pl.BlockSpec((lead, None, N_KV, HEAD_DIM, T), lambda b: (b, layer, 0, 0, 0))
        out_specs = [
            seq_spec, kv_out, kv_out, kv_out, kv_out,
            pl.BlockSpec((lead, None, 2, ML_HEADS, ML_DK, ML_DV), lambda b: (b, layer, 0, 0, 0, 0)),
            pl.BlockSpec((lead, 2, ML_HEADS, ML_DK), lambda b: (b, 0, 0, 0)),
            pl.BlockSpec((lead, 2, ML_HEADS), lambda b: (b, 0, 0)),
        ]
        out_shape = [jax.ShapeDtypeStruct((B, T, D_MODEL), F32)]
        out_shape += [jax.ShapeDtypeStruct(a.shape, a.dtype) for a in extra]
        out_shape += [
            jax.ShapeDtypeStruct((B, 2, ML_HEADS, ML_DK), F32),
            jax.ShapeDtypeStruct((B, 2, ML_HEADS), F32),
        ]

    scratch = [
        ((T, D_MODEL), BF16),
        ((N_HEADS, T, HEAD_DIM), BF16),
        ((N_HEADS, T, HEAD_DIM), BF16),
        ((N_KV, S, HEAD_DIM), BF16),
        ((S // CHUNK, N_KV, VT_ROWS, CHUNK), BF16),
        ((N_KV, T, HEAD_DIM), BF16),
        ((n_chunks, N_KV, VT_ROWS, CHUNK), BF16),
        ((T, ML_HEADS * ML_DK), BF16),
        ((T, ML_HEADS * ML_DK), BF16),
        ((n_chunks, ML_HEADS, ST_ROWS, CHUNK), BF16),
        ((T, ML_HEADS * ML_DV), F32),
        ((T, LANES), F32),
        ((T, BRANCH_W), F32),
        ((T, BRANCH_W), F32),
        ((n_chunks, ML_HEADS * ML_DV, CHUNK), F32),
        ((ML_HEADS, ST_ROWS, 2 * ML_DK), F32),
        ((2 * ML_HEADS, LANES), F32),
    ]
    if lat:
        scratch += [((N_KV, PAST_LEN, HEAD_DIM), BF16),
                    ((N_KV, VT_ROWS, PAST_LEN), BF16)]
    if nseq > 1:
        scratch = [((nseq,) + shape, dtype) for shape, dtype in scratch]
    scratch = [pltpu.VMEM(shape, dtype) for shape, dtype in scratch]

    return pl.pallas_call(
        functools.partial(_layer_kernel, lat, T, layer, nseq),
        grid=(B // nseq,),
        in_specs=in_specs,
        out_specs=out_specs,
        out_shape=out_shape,
        scratch_shapes=scratch,
        input_output_aliases=aliases,
        compiler_params=pltpu.CompilerParams(
            dimension_semantics=("arbitrary",), vmem_limit_bytes=VMEM_LIMIT_BYTES),
        name=("latent_layer" if lat else "context_layer") + str(layer),
    )(*args)


def _rope_tables(T):
    rows = T // GRID_W
    row = jnp.repeat(jnp.arange(rows, dtype=F32), GRID_W)
    col = jnp.tile(jnp.arange(GRID_W, dtype=F32), rows)
    inv = ROPE_THETA ** (-jnp.arange(0, AXIS_DIM, 2, dtype=F32) / AXIS_DIM)
    ar = row[:, None] * inv
    ac = col[:, None] * inv
    ang = jnp.concatenate([ar, ar, ac, ac], axis=-1)
    ang = jnp.concatenate([ang, ang], axis=-1)
    first_half = (jnp.arange(LANES) % AXIS_DIM) < (AXIS_DIM // 2)
    cos = jnp.cos(ang)
    sin = jnp.sin(ang)
    sa = jnp.where(first_half[None, :], -sin, 0.0)
    sb = jnp.where(first_half[None, :], 0.0, sin)
    return cos, sa, sb


def _constants(sink_logit):
    grp = np.arange(N_HEADS * HEAD_DIM) // HEAD_DIM
    bd = jnp.asarray(grp[:, None] == grp[None, :], dtype=BF16)
    ti = np.arange(CHUNK)
    tril = jnp.asarray(ti[None, :] <= ti[:, None], dtype=BF16)
    triu = jnp.asarray(ti[None, :] >= ti[:, None], dtype=BF16)
    lane = np.arange(LANES)[:, None]
    blk = (np.arange(ML_HEADS * CHUNK) // CHUNK)[None, :]
    sel = jnp.asarray(np.stack([lane == 2 * ML_HEADS * d + blk for d in range(2)]), dtype=BF16)
    eye = jnp.asarray(np.eye(HEAD_DIM), dtype=BF16)
    return bd, tril, triu, sel, eye, sink_logit.reshape(-1)


def kernel(x_prompt, x_sample, cache_attn_k, cache_attn_v, cache_win_k, cache_win_v, state_mlstm_C,
           state_mlstm_n, state_mlstm_m, c, c_ctx, w_mod, b_mod, w_in, qk_gain, sink_logit,
           mlstm_gate_bias, w_branch, w_out, ln_gain, ln_bias):
    dec_b = x_sample.shape[0]
    assert dec_b + 1 <= 8
    cond = jnp.concatenate([c_ctx[None, :], c, jnp.zeros((8 - 1 - dec_b, D_MODEL), F32)], axis=0)
    mod = _modulation(cond, w_mod, b_mod).reshape(DEPTH, 8, 3, D_MODEL)

    consts = _constants(sink_logit)
    cos, sa, sb = _rope_tables(x_sample.shape[1])
    lane_pad = jnp.zeros((DEPTH, 1, LANES - N_GATES), F32)
    weights = (
        _prep_in_weight(jnp.swapaxes(w_in, 1, 2)), w_branch.astype(BF16), w_out.astype(BF16),
        jnp.tile(qk_gain[:, 0:1, :], (1, 1, N_HEADS)),
        jnp.tile(qk_gain[:, 1:2, :], (1, 1, N_KV)),
        jnp.concatenate([mlstm_gate_bias.reshape(DEPTH, 1, N_GATES), lane_pad], axis=2),
        ln_gain[:, None, :], ln_bias[:, None, :],
    )

    B, T, _ = x_prompt.shape
    kv_t = jnp.zeros((B, DEPTH, N_KV, HEAD_DIM, T), F32)
    ctx = (kv_t, kv_t, kv_t, kv_t, jnp.zeros((B, DEPTH, 2, ML_HEADS, ML_DK, ML_DV), F32))
    xp, xs = x_prompt, x_sample
    small = []
    for l in range(DEPTH):
        outs = _layer_call(False, l, xp, mod, weights, consts, ctx)
        xp = outs[0]
        ctx = tuple(outs[1:6])
        small.append(outs[6:])
        extra = (cos, sa, sb, cache_attn_k, cache_attn_v, cache_win_k, cache_win_v,
                 state_mlstm_C, state_mlstm_n, state_mlstm_m)
        xs = _layer_call(True, l, xs, mod, weights, consts, extra)
    new_n, new_m = [jnp.stack([small[l][i] for l in range(DEPTH)], axis=1) for i in range(2)]
    return (xp, xs, *[jnp.swapaxes(a, -1, -2) for a in ctx], new_n, new_m)
```

```python
import functools

import jax
import jax.numpy as jnp
import numpy as np
from jax import lax
from jax.experimental import pallas as pl
from jax.experimental.pallas import tpu as pltpu

F32 = jnp.float32
BF16 = jnp.bfloat16

D_MODEL = 1024
DEPTH = 2
PAST_LEN = 256
GRID_W = 64
HEAD_DIM = 64
N_HEADS = 8
N_KV = 2
N_GROUP = N_HEADS // N_KV
WINDOW = 128
ML_HEADS = 4
ML_DK = 64
ML_DV = 128
CHUNK = 128
BRANCH_W = 512
ROPE_THETA = 10000.0
AXIS_DIM = HEAD_DIM // 2
LN_EPS = 1e-6
RMS_EPS = 1e-6
ALPHA = (2.0 * DEPTH) ** 0.25
LOG2E = float(np.log2(np.e))
LANES = 128
BF16_ROWS = 16
ROW_BLOCK = 256
CTX_SEQS_PER_STEP = 1
SEQ_STAGGER = 9
CTX_ALIASED = 5
N_SHARED = 15
VMEM_LIMIT_BYTES = 60 * 1024 * 1024
VT_ROWS = HEAD_DIM + BF16_ROWS
ST_ROWS = ML_DV + BF16_ROWS

_SIZES = (512, 128, 128, 512, 256, 256, 512, 512, 128, 512, 512, 128, 128, 512, 3072)
_OFF = [int(v) for v in np.concatenate([[0], np.cumsum(_SIZES)])]
(C_QA, C_KA, C_VA, C_ZA, C_QM, C_KM, C_VM, C_OM, C_GT, C_ZM, C_QW, C_KW, C_VW, C_ZW, C_GMERGE) = _OFF[:-1]
W_COLS = _OFF[-1]
GATE_COL = 2816
N_GATES = 4 * ML_HEADS


def _dot(a, b):
    return jnp.dot(a, b, preferred_element_type=F32)


def _dot_tb(a, b):
    return lax.dot_general(a, b, (((1,), (1,)), ((), ())), preferred_element_type=F32)


def _split(a):
    hi = a.astype(BF16)
    lo = (a - hi.astype(F32)).astype(BF16)
    return hi, lo


def _layer_norm(x):
    mu = jnp.mean(x, axis=-1, keepdims=True)
    xc = x - mu
    var = jnp.mean(xc * xc, axis=-1, keepdims=True)
    return xc * lax.rsqrt(var + LN_EPS)


def _log_sigmoid(x):
    return jnp.minimum(x, 0.0) - jnp.log(1.0 + jnp.exp(-jnp.abs(x)))


def _sigmoid(x):
    return jax.nn.sigmoid(x)


def _mod_kernel(c_ref, w_ref, b_ref, o_ref):
    c = c_ref[...]
    s = c * _sigmoid(c)
    s_hi, s_lo = _split(s)
    w = w_ref[...]
    w_hi, w_lo = _split(w)
    o_ref[...] = _dot(s_hi, w_hi) + _dot(s_lo, w_hi) + _dot(s_hi, w_lo) + b_ref[...]


def _modulation(cond, w_mod, b_mod):
    rows = cond.shape[0]
    tn = 1024
    return pl.pallas_call(
        _mod_kernel,
        grid=(DEPTH, 3 * D_MODEL // tn),
        in_specs=[
            pl.BlockSpec((rows, D_MODEL), lambda l, j: (0, 0)),
            pl.BlockSpec((None, D_MODEL, tn), lambda l, j: (l, 0, j)),
            pl.BlockSpec((None, 1, tn), lambda l, j: (l, 0, j)),
        ],
        out_specs=pl.BlockSpec((None, rows, tn), lambda l, j: (l, 0, j)),
        out_shape=jax.ShapeDtypeStruct((DEPTH, rows, 3 * D_MODEL), F32),
        compiler_params=pltpu.CompilerParams(dimension_semantics=("arbitrary", "arbitrary")),
        name="adaln_modulation",
    )(cond, w_mod, b_mod.reshape(DEPTH, 1, 3 * D_MODEL))


PREP_K = 256


def _prep_kernel(wt_ref, o_ref):
    lane = lax.broadcasted_iota(jnp.int32, (PREP_K, LANES), 1)
    for c0 in range(0, W_COLS, LANES):
        r0 = c0 if c0 <= GATE_COL else c0 - (LANES - N_GATES)
        slab = jnp.transpose(wt_ref[r0:r0 + LANES, :])
        if c0 == GATE_COL:
            slab = jnp.where(lane < N_GATES, slab, 0.0)
        o_ref[:, c0:c0 + LANES] = slab.astype(BF16)


def _prep_in_weight(w_in_t):
    n_in = w_in_t.shape[1]
    return pl.pallas_call(
        _prep_kernel,
        grid=(DEPTH, D_MODEL // PREP_K),
        in_specs=[pl.BlockSpec((None, n_in, PREP_K), lambda l, i: (l, 0, i))],
        out_specs=pl.BlockSpec((None, PREP_K, W_COLS), lambda l, i: (l, i, 0)),
        out_shape=jax.ShapeDtypeStruct((DEPTH, D_MODEL, W_COLS), BF16),
        compiler_params=pltpu.CompilerParams(
            dimension_semantics=("arbitrary", "arbitrary"), vmem_limit_bytes=VMEM_LIMIT_BYTES),
        name="in_weight_prep",
    )(w_in_t)


def _rope(x, cos, sa, sb):
    return x * cos + pltpu.roll(x, LANES - AXIS_DIM // 2, 1) * sa + pltpu.roll(x, AXIS_DIM // 2, 1) * sb


def _group_rms(x, bd, gain):
    hi, lo = _split(x * x)
    ms = (_dot(hi, bd) + _dot(lo, bd)) * (1.0 / HEAD_DIM)
    return x * lax.rsqrt(ms + RMS_EPS) * gain


def _ones_row_tile():
    r = lax.broadcasted_iota(jnp.int32, (BF16_ROWS, LANES), 0)
    return jnp.where(r == 0, 1.0, 0.0).astype(BF16)


def _attend_chains(chains, results):
    state = []
    for qs, tiles, sink_row in chains:
        m_cols = qs.shape[0]
        if sink_row is None:
            m = jnp.full((1, m_cols), -jnp.inf, F32)
            acc = jnp.zeros((VT_ROWS, m_cols), F32)
        else:
            m = sink_row
            r = lax.broadcasted_iota(jnp.int32, (VT_ROWS, m_cols), 0)
            acc = jnp.where(r == HEAD_DIM, 1.0, 0.0)
        cur = tiles[0]()
        state.append([m, acc, cur, _dot_tb(cur[0], qs)])
    yield
    for t in range(max(len(tiles) for _, tiles, _ in chains)):
        for st, (qs, tiles, _) in zip(state, chains):
            if t >= len(tiles):
                continue
            m, acc, (_, v_t, valid), s = st
            if valid is not None:
                s = jnp.where(valid, s, -jnp.inf)
            if t + 1 < len(tiles):
                nxt = tiles[t + 1]()
                st[2], st[3] = nxt, _dot_tb(nxt[0], qs)
            m_new = jnp.maximum(m, jnp.max(s, axis=0, keepdims=True))
            p = jnp.exp2(s - m_new).astype(BF16)
            st[1] = acc * jnp.exp2(m - m_new) + _dot(v_t, p)
            st[0] = m_new
        yield
    results.extend(st[1][0:HEAD_DIM, :] / st[1][HEAD_DIM:HEAD_DIM + 1, :] for st in state)


def _interleave(weighted):
    live = [[g, n] for g, n in weighted]
    while live:
        for item in list(live):
            for _ in range(item[1]):
                try:
                    next(item[0])
                except StopIteration:
                    live.remove(item)
                    break
        yield


def _run(gen):
    for _ in gen:
        pass


def _delayed(gen, rounds):
    for _ in range(rounds):
        yield
    yield from gen


def _layer_kernel(lat, T, layer, nseq, *refs):
    if nseq == 1:
        _run(_sequence(lat, T, layer, *refs))
        return
    n_shared = N_SHARED + CTX_ALIASED
    seqs = []
    for sq in range(nseq):
        per_seq = [refs[0].at[sq]] + list(refs[1:n_shared]) + [r.at[sq] for r in refs[n_shared:]]
        seqs.append(_sequence(lat, T, layer, *per_seq))
    _run(_interleave([(_delayed(g, sq * SEQ_STAGGER), 1) for sq, g in enumerate(seqs)]))


def _sequence(lat, T, layer, *refs):
    refs = list(refs)
    x_ref, mod_ref, w_ref, wb_ref, wo_ref = refs[:5]
    g0_ref, g1_ref, gb_ref, lng_ref, lnb_ref = refs[5:10]
    bd_ref, tril_ref, triu_ref, eye_ref, sink_ref = refs[10:N_SHARED]
    pos = N_SHARED
    if lat:
        cos_ref, sa_ref, sb_ref = refs[pos:pos + 3]
        cka_ref, cva_ref, ckw_ref, cvw_ref, c0_ref, n0_ref, m0_ref = refs[pos + 3:pos + 10]
        pos += 10
        y_ref = refs[pos]
        pos += 1
    else:
        pos += CTX_ALIASED
        y_ref, ka_o, va_o, kw_o, vw_o, c_o, n_o, m_o = refs[pos:pos + 8]
        pos += 8
    (u_s, qa_s, qw_s, ka_s, vat_s, kw_s, vwt_s, qm_s, km_s, vmt_s, om_s, g_s,
     ya_s, yw_s, hmt_s, c_s, m_s) = refs[pos:pos + 17]
    pos += 17
    if lat:
        kwc_s, vwct_s = refs[pos:pos + 2]

    row_block = min(T, ROW_BLOCK)
    n_rows = T // row_block
    n_chunks = T // CHUNK
    chunks_per_block = row_block // CHUNK
    shift = mod_ref[0:1, :]
    scale = mod_ref[1:2, :]
    gate = mod_ref[2:3, :]
    ones_tile = _ones_row_tile()

    norm_rows = 64

    def row_slice(i, j, n):
        r0 = i * row_block + j
        return pl.ds(r0 if isinstance(r0, int) else pl.multiple_of(r0, n), n)

    def phase1_norm(i):
        for j in range(0, row_block, norm_rows):
            piece = row_slice(i, j, norm_rows)
            u_s[piece, :] = (_layer_norm(x_ref[piece, :]) * (1.0 + scale) + shift).astype(BF16)
            yield

    def phase1_proj(i):
        R = row_block
        rows = row_slice(i, 0, R)
        u = u_s[rows, :]

        def proj(c0, width):
            return _dot(u, w_ref[:, c0:c0 + width])

        if lat:
            cos = cos_ref[rows, :]
            sa = sa_ref[rows, :]
            sb = sb_ref[rows, :]

        def rope_wide(v):
            if not lat:
                return v
            slabs = [_rope(v[:, j:j + LANES], cos, sa, sb) for j in range(0, v.shape[1], LANES)]
            return slabs[0] if len(slabs) == 1 else jnp.concatenate(slabs, axis=1)

        qscale = LOG2E * HEAD_DIM ** -0.5
        qa = rope_wide(_group_rms(proj(C_QA, 512), bd_ref[...], g0_ref[...])) * qscale
        for h in range(N_HEADS):
            qa_s[h, rows, :] = qa[:, h * HEAD_DIM:(h + 1) * HEAD_DIM].astype(BF16)
        yield
        ka_n = _group_rms(proj(C_KA, LANES), bd_ref[0:LANES, 0:LANES], g1_ref[...])
        ka = rope_wide(ka_n)
        va = proj(C_VA, LANES)
        qw = rope_wide(proj(C_QW, 512)) * qscale
        for h in range(N_HEADS):
            qw_s[h, rows, :] = qw[:, h * HEAD_DIM:(h + 1) * HEAD_DIM].astype(BF16)
        yield
        kw_raw = proj(C_KW, LANES)
        kw = rope_wide(kw_raw)
        vw = proj(C_VW, LANES)
        for kv in range(N_KV):
            sl = slice(kv * HEAD_DIM, (kv + 1) * HEAD_DIM)
            ka_s[kv, rows, :] = ka[:, sl].astype(BF16)
            kw_s[kv, rows, :] = kw[:, sl].astype(BF16)
        yield
        qm_s[rows, :] = proj(C_QM, 256).astype(BF16)
        km_s[rows, :] = (proj(C_KM, 256) * (ML_DK ** -0.5)).astype(BF16)
        yield
        vm = proj(C_VM, 512)
        yield
        om_s[rows, :] = proj(C_OM, 512)
        g_s[rows, :] = proj(C_GT, LANES) + gb_ref[...]
        yield
        for j in range(chunks_per_block):
            cj = i * chunks_per_block + j
            cr = slice(j * CHUNK, (j + 1) * CHUNK)
            va_t = jnp.transpose(va[cr, :])
            vw_t = jnp.transpose(vw[cr, :])
            for kv in range(N_KV):
                hd = slice(kv * HEAD_DIM, (kv + 1) * HEAD_DIM)
                vat_s[cj, kv, 0:HEAD_DIM, :] = va_t[hd, :].astype(BF16)
                vat_s[cj, kv, HEAD_DIM:VT_ROWS, :] = ones_tile
                vwt_s[cj, kv, 0:HEAD_DIM, :] = vw_t[hd, :].astype(BF16)
                vwt_s[cj, kv, HEAD_DIM:VT_ROWS, :] = ones_tile
            if not lat:
                tc = slice(i * row_block + j * CHUNK, i * row_block + (j + 1) * CHUNK)
                ka_t = jnp.transpose(ka_n[cr, :])
                kw_t = jnp.transpose(kw_raw[cr, :])
                for kv in range(N_KV):
                    hd = slice(kv * HEAD_DIM, (kv + 1) * HEAD_DIM)
                    ka_o[kv, :, tc] = ka_t[hd, :]
                    va_o[kv, :, tc] = va_t[hd, :]
                    kw_o[kv, :, tc] = kw_t[hd, :]
                    vw_o[kv, :, tc] = vw_t[hd, :]
            for h in range(ML_HEADS):
                vmt_s[cj, h, 0:ML_DV, :] = jnp.transpose(vm[cr, h * ML_DV:(h + 1) * ML_DV]).astype(BF16)
                vmt_s[cj, h, ML_DV:ST_ROWS, :] = ones_tile
            yield

    if n_rows == 1:
        yield from phase1_norm(0)
        yield from phase1_proj(0)
    else:
        def body1(i, carry):
            _run(phase1_norm(i))
            _run(phase1_proj(i))
            return carry
        lax.fori_loop(0, n_rows, body1, 0)

    if lat:
        eye = eye_ref[...]
        for kv in range(N_KV):
            ka_s[kv, T:T + PAST_LEN, :] = cka_ref[kv].astype(BF16)
            kwc_s[kv] = ckw_ref[kv].astype(BF16)
            cva_t = _dot_tb(eye, cva_ref[kv].astype(BF16)).astype(BF16)
            cvw_t = _dot_tb(eye, cvw_ref[kv].astype(BF16)).astype(BF16)
            for j in range(PAST_LEN // CHUNK):
                vat_s[n_chunks + j, kv, 0:HEAD_DIM, :] = cva_t[:, j * CHUNK:(j + 1) * CHUNK]
                vat_s[n_chunks + j, kv, HEAD_DIM:VT_ROWS, :] = ones_tile
                vwct_s[kv, 0:HEAD_DIM, j * CHUNK:(j + 1) * CHUNK] = cvw_t[:, j * CHUNK:(j + 1) * CHUNK]
                vwct_s[kv, HEAD_DIM:VT_ROWS, j * CHUNK:(j + 1) * CHUNK] = ones_tile

    tq = CHUNK
    m_cols = N_GROUP * tq

    def sink_row(kv):
        col = lax.broadcasted_iota(jnp.int32, (1, m_cols), 1)
        row = jnp.full((1, m_cols), sink_ref[layer * N_HEADS + kv * N_GROUP], F32)
        for g in range(1, N_GROUP):
            row = jnp.where(col >= g * tq, sink_ref[layer * N_HEADS + kv * N_GROUP + g], row)
        return row * LOG2E

    def load_q(q_s, kv, q0):
        return q_s[kv * N_GROUP:(kv + 1) * N_GROUP, pl.ds(q0, tq), :].reshape(m_cols, HEAD_DIM)

    def store_heads(dst, q0, kv, o_t):
        for p in range(N_GROUP // 2):
            blk = jnp.concatenate([o_t[:, (2 * p) * tq:(2 * p + 1) * tq],
                                   o_t[:, (2 * p + 1) * tq:(2 * p + 2) * tq]], axis=0)
            c0 = (kv * N_GROUP + 2 * p) * HEAD_DIM
            dst[pl.ds(q0, tq), c0:c0 + LANES] = jnp.transpose(blk).astype(dst.dtype)

    def key_tiles(k_ref, vt_ref, kv, n_key_chunks, chunks_per_tile):
        def tile(c0):
            def load():
                k = k_ref[kv, c0 * CHUNK:(c0 + chunks_per_tile) * CHUNK, :]
                v_t = [vt_ref[c0 + j, kv] for j in range(chunks_per_tile)]
                return k, (v_t[0] if len(v_t) == 1 else jnp.concatenate(v_t, axis=1)), None
            return load
        return [tile(c0) for c0 in range(0, n_key_chunks, chunks_per_tile)]

    def band_tiles(kv, n, q0):
        j0 = jnp.clip(n - 1, 0, n_chunks - 3)
        w0 = pl.multiple_of(j0 * CHUNK, CHUNK)

        def band(c0, nc):
            def load():
                rows = nc * CHUNK
                kpos = w0 + c0 * CHUNK + lax.broadcasted_iota(jnp.int32, (rows, m_cols), 0)
                qpos = q0 + (lax.broadcasted_iota(jnp.int32, (rows, m_cols), 1) & (tq - 1))
                v_t = [vwt_s[j0 + c0 + j, kv] for j in range(nc)]
                return (kw_s[kv, pl.ds(w0 + c0 * CHUNK, rows), :],
                        v_t[0] if nc == 1 else jnp.concatenate(v_t, axis=1),
                        jnp.abs(kpos - qpos) <= WINDOW)
            return load
        return [band(0, 2), band(2, 1), lambda: (kwc_s[kv], vwct_s[kv], None)]

    def attend(n, q0, chunks_per_tile):
        chains, dsts = [], []
        n_key_chunks = ka_s.shape[1] // CHUNK
        for kv in range(N_KV):
            chains.append((load_q(qa_s, kv, q0), key_tiles(ka_s, vat_s, kv, n_key_chunks, chunks_per_tile), None))
            dsts.append((ya_s, kv))
        for kv in range(N_KV):
            tiles = band_tiles(kv, n, q0) if lat else key_tiles(kw_s, vwt_s, kv, n_chunks, chunks_per_tile)
            chains.append((load_q(qw_s, kv, q0), tiles, sink_row(kv)))
            dsts.append((yw_s, kv))
        results = []
        yield from _attend_chains(chains, results)
        for (dst, kv), o_t in zip(dsts, results):
            store_heads(dst, q0, kv, o_t)

    n_pairs = ML_HEADS // 2
    hmt_s[...] = jnp.zeros_like(hmt_s)
    for d in range(2):
        for h in range(ML_HEADS):
            idx = d * n_pairs + h // 2
            hl = slice((h % 2) * ML_DK, (h % 2 + 1) * ML_DK)
            c_s[idx, ML_DV:ST_ROWS, hl] = jnp.zeros((BF16_ROWS, ML_DK), F32)
            if lat:
                c_s[idx, 0:ML_DV, hl] = c0_ref[d, h]
                c_s[idx, ML_DV:ML_DV + 1, hl] = n0_ref[d, h:h + 1, :]
                m_s[d * ML_HEADS + h:d * ML_HEADS + h + 1, :] = jnp.broadcast_to(
                    m0_ref[d:d + 1, h:h + 1], (1, LANES))
            else:
                c_s[idx, 0:ML_DV, hl] = jnp.zeros((ML_DV, ML_DK), F32)
                m_s[d * ML_HEADS + h:d * ML_HEADS + h + 1, :] = jnp.zeros((1, LANES), F32)

    L = CHUNK
    s_idx = lax.broadcasted_iota(jnp.int32, (L, ML_HEADS * L), 0)
    t_idx = lax.broadcasted_iota(jnp.int32, (L, ML_HEADS * L), 1) & (L - 1)
    lane_row = lax.broadcasted_iota(jnp.int32, (1, LANES), 1)
    low_half = lax.broadcasted_iota(jnp.int32, (L, LANES), 1) < ML_DK
    ones_ll = jnp.ones((L, L), BF16)
    zeros_ll = jnp.zeros((L, L), BF16)

    def heads_row(src, r0, c0=0):
        return jnp.concatenate([src[r0 + h:r0 + h + 1, c0:c0 + L] for h in range(ML_HEADS)], axis=1)

    def block_diag_rows(x):
        zero = jnp.zeros_like(x)
        return jnp.concatenate([jnp.where(low_half, x, zero), jnp.where(low_half, zero, x)], axis=0)

    def mlstm_streams(streams):
        m_state = {d: heads_row(m_s, d * ML_HEADS) for d in sorted({d for d, _ in streams})}
        c_state = {d: [c_s[d * n_pairs + pr] for pr in range(n_pairs)] for d in m_state}
        work = []
        for d, cc in streams:
            r0 = cc * L if isinstance(cc, int) else pl.multiple_of(cc * L, L)
            rows = pl.ds(r0, L)
            g = g_s[rows, :]
            f_hi, f_lo = _split(_log_sigmoid(g))
            tri_c = tril_ref[...] if d == 0 else triu_ref[...]
            tri_r = triu_ref[...] if d == 0 else tril_ref[...]
            cum = _dot(tri_c, f_hi) + _dot(tri_c, f_lo)
            r = g - pltpu.roll(cum, LANES - ML_HEADS, 1)
            gi0 = 2 * ML_HEADS * d
            rb = jnp.concatenate([jnp.broadcast_to(r[:, gi0 + h:gi0 + h + 1], (L, L))
                                  for h in range(ML_HEADS)], axis=1)
            g_t = jnp.transpose(g)[0:N_GATES, :]
            ft_hi, ft_lo = _split(_log_sigmoid(g_t))
            tr = jnp.concatenate([tri_r, ones_ll], axis=1)
            ct = _dot(ft_hi, tr) + _dot(ft_lo, tr)
            work.append(dict(d=d, cc=cc, rows=rows, rb=rb, g_t=g_t, ct=ct))
            yield
        for w in work:
            d = w["d"]
            valid = (s_idx <= t_idx) if d == 0 else (s_idx >= t_idx)
            gi0 = 2 * ML_HEADS * d
            gf0 = gi0 + ML_HEADS
            b_row = heads_row(w["ct"], gf0)
            b_last = heads_row(w["ct"], gf0, L)
            i_row = heads_row(w["g_t"], gi0)
            m_prev = m_state[d]
            a_row = b_row + m_prev
            dm = jnp.where(valid, b_row + w["rb"], -jnp.inf)
            mt = jnp.maximum(a_row, jnp.max(dm, axis=0, keepdims=True))
            w["p"] = jnp.exp(dm - mt)
            w["w_inter"] = jnp.exp(a_row - mt)
            w["floor"] = jnp.exp(-mt)
            g_row = b_last - b_row + i_row
            g_max = jnp.concatenate(
                [jnp.broadcast_to(jnp.max(g_row[:, h * L:(h + 1) * L], axis=-1, keepdims=True), (1, L))
                 for h in range(ML_HEADS)], axis=1)
            m_new = jnp.maximum(b_last + m_prev, g_max)
            w["ws"] = jnp.exp(g_row - m_new)
            w["wc"] = jnp.exp(b_last + m_prev - m_new)
            m_state[d] = m_new
            yield
        for w in work:
            rows, cc = w["rows"], w["cc"]
            w["pairs"] = []
            for pr in range(n_pairs):
                lanes = slice(pr * LANES, (pr + 1) * LANES)
                cols = slice(pr * 2 * L, (pr + 1) * 2 * L)
                k_pair = km_s[rows, lanes]
                q_bd = block_diag_rows(qm_s[rows, lanes])
                k_bd = block_diag_rows(k_pair)
                s_t = (_dot_tb(k_pair, q_bd) * w["p"][:, cols]).astype(BF16)
                s_bd = jnp.concatenate(
                    [jnp.concatenate([s_t[:, 0:L], zeros_ll], axis=1),
                     jnp.concatenate([zeros_ll, s_t[:, L:2 * L]], axis=1)], axis=0)
                v_t = jnp.concatenate([vmt_s[cc, 2 * pr], vmt_s[cc, 2 * pr + 1]], axis=1)
                intra = _dot(v_t, s_bd)
                wv = (v_t.astype(F32) * w["ws"][:, cols]).astype(BF16)
                w["pairs"].append((q_bd, intra, _dot(wv, k_bd)))
            yield
        for w in work:
            d, cc = w["d"], w["cc"]
            for pr, (q_bd, intra, update) in enumerate(w["pairs"]):
                cols = slice(pr * 2 * L, (pr + 1) * 2 * L)
                state = c_state[d][pr]
                tot = intra + w["w_inter"][:, cols] * _dot_tb(state.astype(BF16), q_bd)
                h_t = tot[0:ML_DV, :] / jnp.maximum(jnp.abs(tot[ML_DV:ML_DV + 1, :]), w["floor"][:, cols])
                for e in range(2):
                    hr = slice((2 * pr + e) * ML_DV, (2 * pr + e + 1) * ML_DV)
                    hmt_s[cc, hr, :] += h_t[:, e * L:(e + 1) * L]
                wc = w["wc"]
                wc_pair = jnp.where(lane_row < ML_DK, wc[:, 2 * pr * L:(2 * pr + 1) * L],
                                    wc[:, (2 * pr + 1) * L:(2 * pr + 2) * L])
                c_state[d][pr] = wc_pair * state + update
            yield
        for d in m_state:
            for h in range(ML_HEADS):
                m_s[d * ML_HEADS + h:d * ML_HEADS + h + 1, :] = m_state[d][:, h * L:(h + 1) * L]
            for pr in range(n_pairs):
                c_s[d * n_pairs + pr] = c_state[d][pr]

    def scan_streams(cs):
        return [(d, c if d == 0 else n_chunks - 1 - c) for c in cs for d in range(2)]

    if lat:
        def body2(n, carry):
            _run(_interleave([(attend(n, pl.multiple_of(n * tq, tq), 2), 1),
                              (mlstm_streams(scan_streams([n])), 2)]))
            return carry
        lax.fori_loop(0, n_chunks, body2, 0)
    else:
        def all_tiles():
            for n in range(T // tq):
                yield from attend(n, n * tq, 1)
        yield from _interleave([(all_tiles(), 1), (mlstm_streams(scan_streams(range(n_chunks))), 3)])

    if not lat:
        for d in range(2):
            for pr in range(n_pairs):
                c_t = jnp.transpose(c_s[d * n_pairs + pr, 0:ML_DV, :])
                for e in range(2):
                    c_o[d, 2 * pr + e] = c_t[e * ML_DK:(e + 1) * ML_DK, :]
            for h in range(ML_HEADS):
                idx = d * n_pairs + h // 2
                hl = slice((h % 2) * ML_DK, (h % 2 + 1) * ML_DK)
                n_o[d, h:h + 1, :] = c_s[idx, ML_DV:ML_DV + 1, hl]
                m_o[d:d + 1, h:h + 1] = m_s[d * ML_HEADS + h:d * ML_HEADS + h + 1, 0:1]

    def phase3_merge(i, out):
        rows = row_slice(i, 0, row_block)
        u = u_s[rows, :]

        def proj(c0, width):
            return _dot(u, w_ref[:, c0:c0 + width])

        hm = jnp.concatenate(
            [jnp.concatenate([jnp.transpose(hmt_s[i * chunks_per_block + j, h * ML_DV:(h + 1) * ML_DV, :])
                              for h in range(ML_HEADS)], axis=1)
             for j in range(chunks_per_block)], axis=0)
        ys = (ya_s[rows, :].astype(F32), _sigmoid(om_s[rows, :]) * hm, yw_s[rows, :].astype(F32))
        yield
        merged = None
        for b, (yb, zc) in enumerate(zip(ys, (C_ZA, C_ZM, C_ZW))):
            z = proj(zc, BRANCH_W)
            t = (yb * (z * _sigmoid(z))).astype(BF16)
            yield
            pb = _sigmoid(proj(C_GMERGE + b * D_MODEL, D_MODEL)) * _dot(t, wb_ref[b])
            merged = pb if merged is None else merged + pb
            yield
        out.append(_dot(merged.astype(BF16), wo_ref[...]))
        yield

    def phase3_norm(i, o):
        for j in range(0, row_block, norm_rows):
            piece = row_slice(i, j, norm_rows)
            hres = ALPHA * x_ref[piece, :] + gate * o[j:j + norm_rows, :]
            y_ref[piece, :] = _layer_norm(hres) * lng_ref[...] + lnb_ref[...]
            yield

    def phase3(i):
        box = []
        yield from phase3_merge(i, box)
        yield from phase3_norm(i, box[0])

    if n_rows == 1:
        yield from phase3(0)
    else:
        def body4(i, carry):
            _run(phase3(i))
            return carry
        lax.fori_loop(0, n_rows, body4, 0)


def _const_spec(shape):
    nd = len(shape)
    return pl.BlockSpec(shape, lambda b: (0,) * nd, pipeline_mode=pl.Buffered(1))


def _layer_spec(shape, layer):
    nd = len(shape) - 1
    return pl.BlockSpec((None,) + tuple(shape[1:]), lambda b: (layer,) + (0,) * nd,
                        pipeline_mode=pl.Buffered(1))


def _layer_call(lat, layer, x, mod, weights, consts, extra):
    B, T, _ = x.shape
    S = T + PAST_LEN if lat else T
    n_chunks = T // CHUNK
    nseq = 1 if lat else CTX_SEQS_PER_STEP
    lead = None if nseq == 1 else nseq

    if lat:
        seq_spec = pl.BlockSpec((None, T, D_MODEL), lambda b: (b, 0, 0), pipeline_mode=pl.Buffered(1))
    else:
        seq_spec = pl.BlockSpec((lead, T, D_MODEL), lambda b: (b, 0, 0))
    if lat:
        mod_spec = pl.BlockSpec((None, None, 3, D_MODEL), lambda b: (layer, b + 1, 0, 0))
    else:
        mod_spec = pl.BlockSpec((None, None, 3, D_MODEL), lambda b: (layer, 0, 0, 0))
    *vmem_consts, sink = consts
    in_specs = [seq_spec, mod_spec]
    in_specs += [_layer_spec(a.shape, layer) for a in weights]
    in_specs += [_const_spec(a.shape) for a in vmem_consts]
    in_specs.append(pl.BlockSpec(memory_space=pltpu.SMEM))
    args = [x, mod, *weights, *vmem_consts, sink]

    if lat:
        cos, sa, sb, cka, cva, ckw, cvw, sc, sn, sm = extra
        in_specs += [_const_spec(cos.shape)] * 3
        kv_spec = pl.BlockSpec((None, None, N_KV, PAST_LEN, HEAD_DIM), lambda b: (b, layer, 0, 0, 0))
        in_specs += [kv_spec] * 4
        in_specs += [
            pl.BlockSpec((None, None, 2, ML_HEADS, ML_DV, ML_DK), lambda b: (b, layer, 0, 0, 0, 0)),
            pl.BlockSpec((None, None, 2, ML_HEADS, ML_DK), lambda b: (b, layer, 0, 0, 0)),
            pl.BlockSpec((None, None, 2, ML_HEADS), lambda b: (b, layer, 0, 0)),
        ]
        args += [cos, sa, sb, cka, cva, ckw, cvw, sc, sn, sm]
        out_specs = seq_spec
        out_shape = jax.ShapeDtypeStruct((B, T, D_MODEL), F32)
        aliases = {}
    else:
        first_acc = len(args)
        in_specs += [pl.BlockSpec(memory_space=pl.ANY)] * len(extra)
        args += list(extra)
        aliases = {first_acc + i: 1 + i for i in range(len(extra))}
        assert len(extra) == CTX_ALIASED
        kv_out = pl.BlockSpec((lead, None, N_KV, HEAD_DIM, T), lambda b: (b, layer, 0, 0, 0))
        out_specs = [
            seq_spec, kv_out, kv_out, kv_out, kv_out,
            pl.BlockSpec((lead, None, 2, ML_HEADS, ML_DK, ML_DV), lambda b: (b, layer, 0, 0, 0, 0)),
            pl.BlockSpec((lead, 2, ML_HEADS, ML_DK), lambda b: (b, 0, 0, 0)),
            pl.BlockSpec((lead, 2, ML_HEADS), lambda b: (b, 0, 0)),
        ]
        out_shape = [jax.ShapeDtypeStruct((B, T, D_MODEL), F32)]
        out_shape += [jax.ShapeDtypeStruct(a.shape, a.dtype) for a in extra]
        out_shape += [
            jax.ShapeDtypeStruct((B, 2, ML_HEADS, ML_DK), F32),
            jax.ShapeDtypeStruct((B, 2, ML_HEADS), F32),
        ]

    scratch = [
        ((T, D_MODEL), BF16),
        ((N_HEADS, T, HEAD_DIM), BF16),
        ((N_HEADS, T, HEAD_DIM), BF16),
        ((N_KV, S, HEAD_DIM), BF16),
        ((S // CHUNK, N_KV, VT_ROWS, CHUNK), BF16),
        ((N_KV, T, HEAD_DIM), BF16),
        ((n_chunks, N_KV, VT_ROWS, CHUNK), BF16),
        ((T, ML_HEADS * ML_DK), BF16),
        ((T, ML_HEADS * ML_DK), BF16),
        ((n_chunks, ML_HEADS, ST_ROWS, CHUNK), BF16),
        ((T, ML_HEADS * ML_DV), F32),
        ((T, LANES), F32),
        ((T, BRANCH_W), BF16),
        ((T, BRANCH_W), BF16),
        ((n_chunks, ML_HEADS * ML_DV, CHUNK), F32),
        ((ML_HEADS, ST_ROWS, 2 * ML_DK), F32),
        ((2 * ML_HEADS, LANES), F32),
    ]
    if lat:
        scratch += [((N_KV, PAST_LEN, HEAD_DIM), BF16),
                    ((N_KV, VT_ROWS, PAST_LEN), BF16)]
    if nseq > 1:
        scratch = [((nseq,) + shape, dtype) for shape, dtype in scratch]
    scratch = [pltpu.VMEM(shape, dtype) for shape, dtype in scratch]

    return pl.pallas_call(
        functools.partial(_layer_kernel, lat, T, layer, nseq),
        grid=(B // nseq,),
        in_specs=in_specs,
        out_specs=out_specs,
        out_shape=out_shape,
        scratch_shapes=scratch,
        input_output_aliases=aliases,
        compiler_params=pltpu.CompilerParams(
            dimension_semantics=("arbitrary",), vmem_limit_bytes=VMEM_LIMIT_BYTES),
        name=("latent_layer" if lat else "context_layer") + str(layer),
    )(*args)


def _rope_tables(T):
    rows = T // GRID_W
    row = jnp.repeat(jnp.arange(rows, dtype=F32), GRID_W)
    col = jnp.tile(jnp.arange(GRID_W, dtype=F32), rows)
    inv = ROPE_THETA ** (-jnp.arange(0, AXIS_DIM, 2, dtype=F32) / AXIS_DIM)
    ar = row[:, None] * inv
    ac = col[:, None] * inv
    ang = jnp.concatenate([ar, ar, ac, ac], axis=-1)
    ang = jnp.concatenate([ang, ang], axis=-1)
    first_half = (jnp.arange(LANES) % AXIS_DIM) < (AXIS_DIM // 2)
    cos = jnp.cos(ang)
    sin = jnp.sin(ang)
    sa = jnp.where(first_half[None, :], -sin, 0.0)
    sb = jnp.where(first_half[None, :], 0.0, sin)
    return cos, sa, sb


def _constants(sink_logit):
    grp = np.arange(N_HEADS * HEAD_DIM) // HEAD_DIM
    bd = jnp.asarray(grp[:, None] == grp[None, :], dtype=BF16)
    ti = np.arange(CHUNK)
    tril = jnp.asarray(ti[None, :] <= ti[:, None], dtype=BF16)
    triu = jnp.asarray(ti[None, :] >= ti[:, None], dtype=BF16)
    eye = jnp.asarray(np.eye(HEAD_DIM), dtype=BF16)
    return bd, tril, triu, eye, sink_logit.reshape(-1)


def kernel(x_prompt, x_sample, cache_attn_k, cache_attn_v, cache_win_k, cache_win_v, state_mlstm_C,
           state_mlstm_n, state_mlstm_m, c, c_ctx, w_mod, b_mod, w_in, qk_gain, sink_logit,
           mlstm_gate_bias, w_branch, w_out, ln_gain, ln_bias):
    dec_b = x_sample.shape[0]
    assert dec_b + 1 <= 8
    cond = jnp.concatenate([c_ctx[None, :], c, jnp.zeros((8 - 1 - dec_b, D_MODEL), F32)], axis=0)
    mod = _modulation(cond, w_mod, b_mod).reshape(DEPTH, 8, 3, D_MODEL)

    consts = _constants(sink_logit)
    cos, sa, sb = _rope_tables(x_sample.shape[1])
    lane_pad = jnp.zeros((DEPTH, 1, LANES - N_GATES), F32)
    weights = (
        _prep_in_weight(jnp.swapaxes(w_in, 1, 2)), w_branch.astype(BF16), w_out.astype(BF16),
        jnp.tile(qk_gain[:, 0:1, :], (1, 1, N_HEADS)),
        jnp.tile(qk_gain[:, 1:2, :], (1, 1, N_KV)),
        jnp.concatenate([mlstm_gate_bias.reshape(DEPTH, 1, N_GATES), lane_pad], axis=2),
        ln_gain[:, None, :], ln_bias[:, None, :],
    )

    B, T, _ = x_prompt.shape
    kv_t = jnp.zeros((B, DEPTH, N_KV, HEAD_DIM, T), F32)
    ctx = (kv_t, kv_t, kv_t, kv_t, jnp.zeros((B, DEPTH, 2, ML_HEADS, ML_DK, ML_DV), F32))
    xp, xs = x_prompt, x_sample
    small = []
    for l in range(DEPTH):
        outs = _layer_call(False, l, xp, mod, weights, consts, ctx)
        xp = outs[0]
        ctx = tuple(outs[1:6])
        small.append(outs[6:])
        extra = (cos, sa, sb, cache_attn_k, cache_attn_v, cache_win_k, cache_win_v,
                 state_mlstm_C, state_mlstm_n, state_mlstm_m)
        xs = _layer_call(True, l, xs, mod, weights, consts, extra)
    new_n, new_m = [jnp.stack([small[l][i] for l in range(DEPTH)], axis=1) for i in range(2)]
    return (xp, xs, *[jnp.swapaxes(a, -1, -2) for a in ctx], new_n, new_m)
```

```python
import functools

import jax
import jax.numpy as jnp
import numpy as np
from jax import lax
from jax.experimental import pallas as pl
from jax.experimental.pallas import tpu as pltpu

F32 = jnp.float32
BF16 = jnp.bfloat16

D_MODEL = 1024
DEPTH = 2
PAST_LEN = 256
GRID_W = 64
HEAD_DIM = 64
N_HEADS = 8
N_KV = 2
N_GROUP = N_HEADS // N_KV
WINDOW = 128
ML_HEADS = 4
ML_DK = 64
ML_DV = 128
CHUNK = 128
BRANCH_W = 512
ROPE_THETA = 10000.0
AXIS_DIM = HEAD_DIM // 2
LN_EPS = 1e-6
RMS_EPS = 1e-6
ALPHA = (2.0 * DEPTH) ** 0.25
LOG2E = float(np.log2(np.e))
LANES = 128
BF16_ROWS = 16
ROW_BLOCK = 256
CTX_ALIASED = 5
N_SHARED = 15
VMEM_LIMIT_BYTES = 60 * 1024 * 1024
VT_ROWS = HEAD_DIM + BF16_ROWS
ST_ROWS = ML_DV + BF16_ROWS

_SIZES = (512, 128, 128, 512, 256, 256, 512, 512, 128, 512, 512, 128, 128, 512, 3072)
_OFF = [int(v) for v in np.concatenate([[0], np.cumsum(_SIZES)])]
(C_QA, C_KA, C_VA, C_ZA, C_QM, C_KM, C_VM, C_OM, C_GT, C_ZM, C_QW, C_KW, C_VW, C_ZW, C_GMERGE) = _OFF[:-1]
W_COLS = _OFF[-1]
GATE_COL = 2816
N_GATES = 4 * ML_HEADS


def _dot(a, b):
    return jnp.dot(a, b, preferred_element_type=F32)


def _dot_tb(a, b):
    return lax.dot_general(a, b, (((1,), (1,)), ((), ())), preferred_element_type=F32)


def _split(a):
    hi = a.astype(BF16)
    lo = (a - hi.astype(F32)).astype(BF16)
    return hi, lo


def _layer_norm(x):
    mu = jnp.mean(x, axis=-1, keepdims=True)
    xc = x - mu
    var = jnp.mean(xc * xc, axis=-1, keepdims=True)
    return xc * lax.rsqrt(var + LN_EPS)


def _log_sigmoid(x):
    return jnp.minimum(x, 0.0) - jnp.log(1.0 + jnp.exp(-jnp.abs(x)))


def _sigmoid(x):
    return jax.nn.sigmoid(x)


def _mod_kernel(c_ref, w_ref, b_ref, o_ref):
    c = c_ref[...]
    s = c * _sigmoid(c)
    s_hi, s_lo = _split(s)
    w = w_ref[...]
    w_hi, w_lo = _split(w)
    o_ref[...] = _dot(s_hi, w_hi) + _dot(s_lo, w_hi) + _dot(s_hi, w_lo) + b_ref[...]


def _modulation(cond, w_mod, b_mod):
    rows = cond.shape[0]
    tn = 1024
    return pl.pallas_call(
        _mod_kernel,
        grid=(DEPTH, 3 * D_MODEL // tn),
        in_specs=[
            pl.BlockSpec((rows, D_MODEL), lambda l, j: (0, 0)),
            pl.BlockSpec((None, D_MODEL, tn), lambda l, j: (l, 0, j)),
            pl.BlockSpec((None, 1, tn), lambda l, j: (l, 0, j)),
        ],
        out_specs=pl.BlockSpec((None, rows, tn), lambda l, j: (l, 0, j)),
        out_shape=jax.ShapeDtypeStruct((DEPTH, rows, 3 * D_MODEL), F32),
        compiler_params=pltpu.CompilerParams(dimension_semantics=("arbitrary", "arbitrary")),
        name="adaln_modulation",
    )(cond, w_mod, b_mod.reshape(DEPTH, 1, 3 * D_MODEL))


PREP_K = 256


def _prep_kernel(wt_ref, o_ref):
    lane = lax.broadcasted_iota(jnp.int32, (PREP_K, LANES), 1)
    for c0 in range(0, W_COLS, LANES):
        r0 = c0 if c0 <= GATE_COL else c0 - (LANES - N_GATES)
        slab = jnp.transpose(wt_ref[r0:r0 + LANES, :])
        if c0 == GATE_COL:
            slab = jnp.where(lane < N_GATES, slab, 0.0)
        o_ref[:, c0:c0 + LANES] = slab.astype(BF16)


def _prep_in_weight(w_in_t):
    n_in = w_in_t.shape[1]
    return pl.pallas_call(
        _prep_kernel,
        grid=(DEPTH, D_MODEL // PREP_K),
        in_specs=[pl.BlockSpec((None, n_in, PREP_K), lambda l, i: (l, 0, i))],
        out_specs=pl.BlockSpec((None, PREP_K, W_COLS), lambda l, i: (l, i, 0)),
        out_shape=jax.ShapeDtypeStruct((DEPTH, D_MODEL, W_COLS), BF16),
        compiler_params=pltpu.CompilerParams(
            dimension_semantics=("arbitrary", "arbitrary"), vmem_limit_bytes=VMEM_LIMIT_BYTES),
        name="in_weight_prep",
    )(w_in_t)


def _rope(x, cos, sa, sb):
    return x * cos + pltpu.roll(x, LANES - AXIS_DIM // 2, 1) * sa + pltpu.roll(x, AXIS_DIM // 2, 1) * sb


def _group_rms(x, bd, gain):
    hi, lo = _split(x * x)
    ms = (_dot(hi, bd) + _dot(lo, bd)) * (1.0 / HEAD_DIM)
    return x * lax.rsqrt(ms + RMS_EPS) * gain


def _ones_row_tile():
    r = lax.broadcasted_iota(jnp.int32, (BF16_ROWS, LANES), 0)
    return jnp.where(r == 0, 1.0, 0.0).astype(BF16)


def _attend_chains(chains, results):
    state = []
    for qs, tiles, sink_row in chains:
        m_cols = qs.shape[0]
        if sink_row is None:
            m = jnp.full((1, m_cols), -jnp.inf, F32)
            acc = jnp.zeros((VT_ROWS, m_cols), F32)
        else:
            m = sink_row
            r = lax.broadcasted_iota(jnp.int32, (VT_ROWS, m_cols), 0)
            acc = jnp.where(r == HEAD_DIM, 1.0, 0.0)
        cur = tiles[0]()
        state.append([m, acc, cur, _dot_tb(cur[0], qs)])
    yield
    for t in range(max(len(tiles) for _, tiles, _ in chains)):
        for st, (qs, tiles, _) in zip(state, chains):
            if t >= len(tiles):
                continue
            m, acc, (_, v_t, valid), s = st
            if valid is not None:
                s = jnp.where(valid, s, -jnp.inf)
            if t + 1 < len(tiles):
                nxt = tiles[t + 1]()
                st[2], st[3] = nxt, _dot_tb(nxt[0], qs)
            m_new = jnp.maximum(m, jnp.max(s, axis=0, keepdims=True))
            p = jnp.exp2(s - m_new).astype(BF16)
            st[1] = acc * jnp.exp2(m - m_new) + _dot(v_t, p)
            st[0] = m_new
        yield
    results.extend(st[1][0:HEAD_DIM, :] / st[1][HEAD_DIM:HEAD_DIM + 1, :] for st in state)


def _interleave(weighted):
    live = [[g, n] for g, n in weighted]
    while live:
        for item in list(live):
            for _ in range(item[1]):
                try:
                    next(item[0])
                except StopIteration:
                    live.remove(item)
                    break
        yield


def _run(gen):
    for _ in gen:
        pass


def _layer_kernel(lat, T, layer, *refs):
    _run(_sequence(lat, T, layer, *refs))


def _sequence(lat, T, layer, *refs):
    refs = list(refs)
    x_ref, mod_ref, w_ref, wb_ref, wo_ref = refs[:5]
    g0_ref, g1_ref, gb_ref, lng_ref, lnb_ref = refs[5:10]
    bd_ref, tril_ref, triu_ref, eye_ref, sink_ref = refs[10:N_SHARED]
    pos = N_SHARED
    if lat:
        cos_ref, sa_ref, sb_ref = refs[pos:pos + 3]
        cka_ref, cva_ref, ckw_ref, cvw_ref, c0_ref, n0_ref, m0_ref = refs[pos + 3:pos + 10]
        pos += 10
        y_ref = refs[pos]
        pos += 1
    else:
        pos += CTX_ALIASED
        x_next_ref = refs[pos]
        pos += 1
        y_ref, ka_o, va_o, kw_o, vw_o, c_o, n_o, m_o = refs[pos:pos + 8]
        pos += 8
    (u_s, qa_s, qw_s, ka_s, vat_s, kw_s, vwt_s, qm_s, km_s, vmt_s, om_s, g_s,
     ya_s, yw_s, hmt_s, c_s, m_s) = refs[pos:pos + 17]
    pos += 17
    if not lat:
        step = pl.program_id(0)
        u_next_s = u_s.at[(step + 1) % 2]
        u_s = u_s.at[step % 2]
    if lat:
        kwc_s, vwct_s = refs[pos:pos + 2]

    row_block = min(T, ROW_BLOCK)
    n_rows = T // row_block
    n_chunks = T // CHUNK
    chunks_per_block = row_block // CHUNK
    shift = mod_ref[0:1, :]
    scale = mod_ref[1:2, :]
    gate = mod_ref[2:3, :]
    ones_tile = _ones_row_tile()

    norm_rows = 64

    def row_slice(i, j, n):
        r0 = i * row_block + j
        return pl.ds(r0 if isinstance(r0, int) else pl.multiple_of(r0, n), n)

    def phase1_norm(i, src=x_ref, dst=u_s):
        for j in range(0, row_block, norm_rows):
            piece = row_slice(i, j, norm_rows)
            dst[piece, :] = (_layer_norm(src[piece, :]) * (1.0 + scale) + shift).astype(BF16)
            yield

    def phase1_proj(i):
        R = row_block
        rows = row_slice(i, 0, R)
        u = u_s[rows, :]

        def proj(c0, width):
            return _dot(u, w_ref[:, c0:c0 + width])

        if lat:
            cos = cos_ref[rows, :]
            sa = sa_ref[rows, :]
            sb = sb_ref[rows, :]

        def rope_wide(v):
            if not lat:
                return v
            slabs = [_rope(v[:, j:j + LANES], cos, sa, sb) for j in range(0, v.shape[1], LANES)]
            return slabs[0] if len(slabs) == 1 else jnp.concatenate(slabs, axis=1)

        qscale = LOG2E * HEAD_DIM ** -0.5
        qa = rope_wide(_group_rms(proj(C_QA, 512), bd_ref[...], g0_ref[...])) * qscale
        for h in range(N_HEADS):
            qa_s[h, rows, :] = qa[:, h * HEAD_DIM:(h + 1) * HEAD_DIM].astype(BF16)
        yield
        ka_n = _group_rms(proj(C_KA, LANES), bd_ref[0:LANES, 0:LANES], g1_ref[...])
        ka = rope_wide(ka_n)
        va = proj(C_VA, LANES)
        qw = rope_wide(proj(C_QW, 512)) * qscale
        for h in range(N_HEADS):
            qw_s[h, rows, :] = qw[:, h * HEAD_DIM:(h + 1) * HEAD_DIM].astype(BF16)
        yield
        kw_raw = proj(C_KW, LANES)
        kw = rope_wide(kw_raw)
        vw = proj(C_VW, LANES)
        for kv in range(N_KV):
            sl = slice(kv * HEAD_DIM, (kv + 1) * HEAD_DIM)
            ka_s[kv, rows, :] = ka[:, sl].astype(BF16)
            kw_s[kv, rows, :] = kw[:, sl].astype(BF16)
        yield
        qm_s[rows, :] = proj(C_QM, 256).astype(BF16)
        km_s[rows, :] = (proj(C_KM, 256) * (ML_DK ** -0.5)).astype(BF16)
        yield
        vm = proj(C_VM, 512)
        yield
        om_s[rows, :] = proj(C_OM, 512)
        g_s[rows, :] = proj(C_GT, LANES) + gb_ref[...]
        yield
        for j in range(chunks_per_block):
            cj = i * chunks_per_block + j
            cr = slice(j * CHUNK, (j + 1) * CHUNK)
            va_t = jnp.transpose(va[cr, :])
            vw_t = jnp.transpose(vw[cr, :])
            for kv in range(N_KV):
                hd = slice(kv * HEAD_DIM, (kv + 1) * HEAD_DIM)
                vat_s[cj, kv, 0:HEAD_DIM, :] = va_t[hd, :].astype(BF16)
                vat_s[cj, kv, HEAD_DIM:VT_ROWS, :] = ones_tile
                vwt_s[cj, kv, 0:HEAD_DIM, :] = vw_t[hd, :].astype(BF16)
                vwt_s[cj, kv, HEAD_DIM:VT_ROWS, :] = ones_tile
            if not lat:
                tc = slice(i * row_block + j * CHUNK, i * row_block + (j + 1) * CHUNK)
                ka_t = jnp.transpose(ka_n[cr, :])
                kw_t = jnp.transpose(kw_raw[cr, :])
                for kv in range(N_KV):
                    hd = slice(kv * HEAD_DIM, (kv + 1) * HEAD_DIM)
                    ka_o[kv, :, tc] = ka_t[hd, :]
                    va_o[kv, :, tc] = va_t[hd, :]
                    kw_o[kv, :, tc] = kw_t[hd, :]
                    vw_o[kv, :, tc] = vw_t[hd, :]
            for h in range(ML_HEADS):
                vmt_s[cj, h, 0:ML_DV, :] = jnp.transpose(vm[cr, h * ML_DV:(h + 1) * ML_DV]).astype(BF16)
                vmt_s[cj, h, ML_DV:ST_ROWS, :] = ones_tile
            yield

    if not lat:
        @pl.when(step == 0)
        def _():
            _run(phase1_norm(0))
        yield from phase1_proj(0)
    else:
        def body1(i, carry):
            _run(phase1_norm(i))
            _run(phase1_proj(i))
            return carry
        lax.fori_loop(0, n_rows, body1, 0)

    if lat:
        eye = eye_ref[...]
        for kv in range(N_KV):
            ka_s[kv, T:T + PAST_LEN, :] = cka_ref[kv].astype(BF16)
            kwc_s[kv] = ckw_ref[kv].astype(BF16)
            cva_t = _dot_tb(eye, cva_ref[kv].astype(BF16)).astype(BF16)
            cvw_t = _dot_tb(eye, cvw_ref[kv].astype(BF16)).astype(BF16)
            for j in range(PAST_LEN // CHUNK):
                vat_s[n_chunks + j, kv, 0:HEAD_DIM, :] = cva_t[:, j * CHUNK:(j + 1) * CHUNK]
                vat_s[n_chunks + j, kv, HEAD_DIM:VT_ROWS, :] = ones_tile
                vwct_s[kv, 0:HEAD_DIM, j * CHUNK:(j + 1) * CHUNK] = cvw_t[:, j * CHUNK:(j + 1) * CHUNK]
                vwct_s[kv, HEAD_DIM:VT_ROWS, j * CHUNK:(j + 1) * CHUNK] = ones_tile

    tq = CHUNK
    m_cols = N_GROUP * tq

    def sink_row(kv):
        col = lax.broadcasted_iota(jnp.int32, (1, m_cols), 1)
        row = jnp.full((1, m_cols), sink_ref[layer * N_HEADS + kv * N_GROUP], F32)
        for g in range(1, N_GROUP):
            row = jnp.where(col >= g * tq, sink_ref[layer * N_HEADS + kv * N_GROUP + g], row)
        return row * LOG2E

    def load_q(q_s, kv, q0):
        return q_s[kv * N_GROUP:(kv + 1) * N_GROUP, pl.ds(q0, tq), :].reshape(m_cols, HEAD_DIM)

    def store_heads(dst, q0, kv, o_t):
        for p in range(N_GROUP // 2):
            blk = jnp.concatenate([o_t[:, (2 * p) * tq:(2 * p + 1) * tq],
                                   o_t[:, (2 * p + 1) * tq:(2 * p + 2) * tq]], axis=0)
            c0 = (kv * N_GROUP + 2 * p) * HEAD_DIM
            dst[pl.ds(q0, tq), c0:c0 + LANES] = jnp.transpose(blk).astype(dst.dtype)

    def key_tiles(k_ref, vt_ref, kv, n_key_chunks, chunks_per_tile):
        def tile(c0):
            def load():
                k = k_ref[kv, c0 * CHUNK:(c0 + chunks_per_tile) * CHUNK, :]
                v_t = [vt_ref[c0 + j, kv] for j in range(chunks_per_tile)]
                return k, (v_t[0] if len(v_t) == 1 else jnp.concatenate(v_t, axis=1)), None
            return load
        return [tile(c0) for c0 in range(0, n_key_chunks, chunks_per_tile)]

    def band_tiles(kv, n, q0):
        j0 = jnp.clip(n - 1, 0, n_chunks - 3)
        w0 = pl.multiple_of(j0 * CHUNK, CHUNK)

        def band(c0, nc):
            def load():
                rows = nc * CHUNK
                kpos = w0 + c0 * CHUNK + lax.broadcasted_iota(jnp.int32, (rows, m_cols), 0)
                qpos = q0 + (lax.broadcasted_iota(jnp.int32, (rows, m_cols), 1) & (tq - 1))
                v_t = [vwt_s[j0 + c0 + j, kv] for j in range(nc)]
                return (kw_s[kv, pl.ds(w0 + c0 * CHUNK, rows), :],
                        v_t[0] if nc == 1 else jnp.concatenate(v_t, axis=1),
                        jnp.abs(kpos - qpos) <= WINDOW)
            return load
        return [band(0, 2), band(2, 1), lambda: (kwc_s[kv], vwct_s[kv], None)]

    def attend(n, q0, chunks_per_tile):
        chains, dsts = [], []
        n_key_chunks = ka_s.shape[1] // CHUNK
        for kv in range(N_KV):
            chains.append((load_q(qa_s, kv, q0), key_tiles(ka_s, vat_s, kv, n_key_chunks, chunks_per_tile), None))
            dsts.append((ya_s, kv))
        for kv in range(N_KV):
            tiles = band_tiles(kv, n, q0) if lat else key_tiles(kw_s, vwt_s, kv, n_chunks, chunks_per_tile)
            chains.append((load_q(qw_s, kv, q0), tiles, sink_row(kv)))
            dsts.append((yw_s, kv))
        results = []
        yield from _attend_chains(chains, results)
        for (dst, kv), o_t in zip(dsts, results):
            store_heads(dst, q0, kv, o_t)

    n_pairs = ML_HEADS // 2
    hmt_s[...] = jnp.zeros_like(hmt_s)
    for d in range(2):
        for h in range(ML_HEADS):
            idx = d * n_pairs + h // 2
            hl = slice((h % 2) * ML_DK, (h % 2 + 1) * ML_DK)
            c_s[idx, ML_DV:ST_ROWS, hl] = jnp.zeros((BF16_ROWS, ML_DK), F32)
            if lat:
                c_s[idx, 0:ML_DV, hl] = c0_ref[d, h]
                c_s[idx, ML_DV:ML_DV + 1, hl] = n0_ref[d, h:h + 1, :]
                m_s[d * ML_HEADS + h:d * ML_HEADS + h + 1, :] = jnp.broadcast_to(
                    m0_ref[d:d + 1, h:h + 1], (1, LANES))
            else:
                c_s[idx, 0:ML_DV, hl] = jnp.zeros((ML_DV, ML_DK), F32)
                m_s[d * ML_HEADS + h:d * ML_HEADS + h + 1, :] = jnp.zeros((1, LANES), F32)

    L = CHUNK
    s_idx = lax.broadcasted_iota(jnp.int32, (L, ML_HEADS * L), 0)
    t_idx = lax.broadcasted_iota(jnp.int32, (L, ML_HEADS * L), 1) & (L - 1)
    lane_row = lax.broadcasted_iota(jnp.int32, (1, LANES), 1)
    low_half = lax.broadcasted_iota(jnp.int32, (L, LANES), 1) < ML_DK
    ones_ll = jnp.ones((L, L), BF16)
    zeros_ll = jnp.zeros((L, L), BF16)

    def heads_row(src, r0, c0=0):
        return jnp.concatenate([src[r0 + h:r0 + h + 1, c0:c0 + L] for h in range(ML_HEADS)], axis=1)

    def block_diag_rows(x):
        zero = jnp.zeros_like(x)
        return jnp.concatenate([jnp.where(low_half, x, zero), jnp.where(low_half, zero, x)], axis=0)

    def mlstm_streams(streams):
        m_state = {d: heads_row(m_s, d * ML_HEADS) for d in sorted({d for d, _ in streams})}
        c_state = {d: [c_s[d * n_pairs + pr] for pr in range(n_pairs)] for d in m_state}
        work = []
        for d, cc in streams:
            r0 = cc * L if isinstance(cc, int) else pl.multiple_of(cc * L, L)
            rows = pl.ds(r0, L)
            g = g_s[rows, :]
            f_hi, f_lo = _split(_log_sigmoid(g))
            tri_c = tril_ref[...] if d == 0 else triu_ref[...]
            tri_r = triu_ref[...] if d == 0 else tril_ref[...]
            cum = _dot(tri_c, f_hi) + _dot(tri_c, f_lo)
            r = g - pltpu.roll(cum, LANES - ML_HEADS, 1)
            gi0 = 2 * ML_HEADS * d
            rb = jnp.concatenate([jnp.broadcast_to(r[:, gi0 + h:gi0 + h + 1], (L, L))
                                  for h in range(ML_HEADS)], axis=1)
            g_t = jnp.transpose(g)[0:N_GATES, :]
            ft_hi, ft_lo = _split(_log_sigmoid(g_t))
            tr = jnp.concatenate([tri_r, ones_ll], axis=1)
            ct = _dot(ft_hi, tr) + _dot(ft_lo, tr)
            work.append(dict(d=d, cc=cc, rows=rows, rb=rb, g_t=g_t, ct=ct))
            yield
        for w in work:
            d = w["d"]
            valid = (s_idx <= t_idx) if d == 0 else (s_idx >= t_idx)
            gi0 = 2 * ML_HEADS * d
            gf0 = gi0 + ML_HEADS
            b_row = heads_row(w["ct"], gf0)
            b_last = heads_row(w["ct"], gf0, L)
            i_row = heads_row(w["g_t"], gi0)
            m_prev = m_state[d]
            a_row = b_row + m_prev
            dm = jnp.where(valid, b_row + w["rb"], -jnp.inf)
            mt = jnp.maximum(a_row, jnp.max(dm, axis=0, keepdims=True))
            w["p"] = jnp.exp(dm - mt)
            w["w_inter"] = jnp.exp(a_row - mt)
            w["floor"] = jnp.exp(-mt)
            g_row = b_last - b_row + i_row
            g_max = jnp.concatenate(
                [jnp.broadcast_to(jnp.max(g_row[:, h * L:(h + 1) * L], axis=-1, keepdims=True), (1, L))
                 for h in range(ML_HEADS)], axis=1)
            m_new = jnp.maximum(b_last + m_prev, g_max)
            w["ws"] = jnp.exp(g_row - m_new)
            w["wc"] = jnp.exp(b_last + m_prev - m_new)
            m_state[d] = m_new
            yield
        for w in work:
            rows, cc = w["rows"], w["cc"]
            w["pairs"] = []
            for pr in range(n_pairs):
                lanes = slice(pr * LANES, (pr + 1) * LANES)
                cols = slice(pr * 2 * L, (pr + 1) * 2 * L)
                k_pair = km_s[rows, lanes]
                q_bd = block_diag_rows(qm_s[rows, lanes])
                k_bd = block_diag_rows(k_pair)
                s_t = (_dot_tb(k_pair, q_bd) * w["p"][:, cols]).astype(BF16)
                s_bd = jnp.concatenate(
                    [jnp.concatenate([s_t[:, 0:L], zeros_ll], axis=1),
                     jnp.concatenate([zeros_ll, s_t[:, L:2 * L]], axis=1)], axis=0)
                v_t = jnp.concatenate([vmt_s[cc, 2 * pr], vmt_s[cc, 2 * pr + 1]], axis=1)
                intra = _dot(v_t, s_bd)
                wv = (v_t.astype(F32) * w["ws"][:, cols]).astype(BF16)
                w["pairs"].append((q_bd, intra, _dot(wv, k_bd)))
            yield
        for w in work:
            d, cc = w["d"], w["cc"]
            for pr, (q_bd, intra, update) in enumerate(w["pairs"]):
                cols = slice(pr * 2 * L, (pr + 1) * 2 * L)
                state = c_state[d][pr]
                tot = intra + w["w_inter"][:, cols] * _dot_tb(state.astype(BF16), q_bd)
                h_t = tot[0:ML_DV, :] / jnp.maximum(jnp.abs(tot[ML_DV:ML_DV + 1, :]), w["floor"][:, cols])
                for e in range(2):
                    hr = slice((2 * pr + e) * ML_DV, (2 * pr + e + 1) * ML_DV)
                    hmt_s[cc, hr, :] += h_t[:, e * L:(e + 1) * L]
                wc = w["wc"]
                wc_pair = jnp.where(lane_row < ML_DK, wc[:, 2 * pr * L:(2 * pr + 1) * L],
                                    wc[:, (2 * pr + 1) * L:(2 * pr + 2) * L])
                c_state[d][pr] = wc_pair * state + update
            yield
        for d in m_state:
            for h in range(ML_HEADS):
                m_s[d * ML_HEADS + h:d * ML_HEADS + h + 1, :] = m_state[d][:, h * L:(h + 1) * L]
            for pr in range(n_pairs):
                c_s[d * n_pairs + pr] = c_state[d][pr]

    def scan_streams(cs):
        return [(d, c if d == 0 else n_chunks - 1 - c) for c in cs for d in range(2)]

    if lat:
        def body2(n, carry):
            _run(_interleave([(attend(n, pl.multiple_of(n * tq, tq), 2), 1),
                              (mlstm_streams(scan_streams([n])), 2)]))
            return carry
        lax.fori_loop(0, n_chunks, body2, 0)
    else:
        def all_tiles():
            for n in range(T // tq):
                yield from attend(n, n * tq, 1)
        yield from _interleave([(all_tiles(), 1), (mlstm_streams(scan_streams(range(n_chunks))), 3)])

    if not lat:
        for d in range(2):
            for pr in range(n_pairs):
                c_t = jnp.transpose(c_s[d * n_pairs + pr, 0:ML_DV, :])
                for e in range(2):
                    c_o[d, 2 * pr + e] = c_t[e * ML_DK:(e + 1) * ML_DK, :]
            for h in range(ML_HEADS):
                idx = d * n_pairs + h // 2
                hl = slice((h % 2) * ML_DK, (h % 2 + 1) * ML_DK)
                n_o[d, h:h + 1, :] = c_s[idx, ML_DV:ML_DV + 1, hl]
                m_o[d:d + 1, h:h + 1] = m_s[d * ML_HEADS + h:d * ML_HEADS + h + 1, 0:1]

    def phase3_merge(i, out):
        rows = row_slice(i, 0, row_block)
        u = u_s[rows, :]

        def proj(c0, width):
            return _dot(u, w_ref[:, c0:c0 + width])

        hm = jnp.concatenate(
            [jnp.concatenate([jnp.transpose(hmt_s[i * chunks_per_block + j, h * ML_DV:(h + 1) * ML_DV, :])
                              for h in range(ML_HEADS)], axis=1)
             for j in range(chunks_per_block)], axis=0)
        ys = (ya_s[rows, :].astype(F32), _sigmoid(om_s[rows, :]) * hm, yw_s[rows, :].astype(F32))
        yield
        merged = None
        for b, (yb, zc) in enumerate(zip(ys, (C_ZA, C_ZM, C_ZW))):
            z = proj(zc, BRANCH_W)
            t = (yb * (z * _sigmoid(z))).astype(BF16)
            yield
            pb = _sigmoid(proj(C_GMERGE + b * D_MODEL, D_MODEL)) * _dot(t, wb_ref[b])
            merged = pb if merged is None else merged + pb
            yield
        out.append(_dot(merged.astype(BF16), wo_ref[...]))
        yield

    def phase3_norm(i, o):
        for j in range(0, row_block, norm_rows):
            piece = row_slice(i, j, norm_rows)
            hres = ALPHA * x_ref[piece, :] + gate * o[j:j + norm_rows, :]
            y_ref[piece, :] = _layer_norm(hres) * lng_ref[...] + lnb_ref[...]
            yield

    def phase3(i):
        box = []
        yield from phase3_merge(i, box)
        yield from phase3_norm(i, box[0])

    if not lat:
        box = []
        yield from _interleave([(phase3_merge(0, box), 1), (phase1_norm(0, x_next_ref, u_next_s), 1)])
        yield from phase3_norm(0, box[0])
    else:
        def body4(i, carry):
            _run(phase3(i))
            return carry
        lax.fori_loop(0, n_rows, body4, 0)


def _const_spec(shape):
    nd = len(shape)
    return pl.BlockSpec(shape, lambda b: (0,) * nd, pipeline_mode=pl.Buffered(1))


def _layer_spec(shape, layer):
    nd = len(shape) - 1
    return pl.BlockSpec((None,) + tuple(shape[1:]), lambda b: (layer,) + (0,) * nd,
                        pipeline_mode=pl.Buffered(1))


def _layer_call(lat, layer, x, mod, weights, consts, extra):
    B, T, _ = x.shape
    S = T + PAST_LEN if lat else T
    n_chunks = T // CHUNK

    if lat:
        seq_spec = pl.BlockSpec((None, T, D_MODEL), lambda b: (b, 0, 0), pipeline_mode=pl.Buffered(1))
    else:
        seq_spec = pl.BlockSpec((None, T, D_MODEL), lambda b: (b, 0, 0))
    if lat:
        mod_spec = pl.BlockSpec((None, None, 3, D_MODEL), lambda b: (layer, b + 1, 0, 0))
    else:
        mod_spec = pl.BlockSpec((None, None, 3, D_MODEL), lambda b: (layer, 0, 0, 0))
    *vmem_consts, sink = consts
    in_specs = [seq_spec, mod_spec]
    in_specs += [_layer_spec(a.shape, layer) for a in weights]
    in_specs += [_const_spec(a.shape) for a in vmem_consts]
    in_specs.append(pl.BlockSpec(memory_space=pltpu.SMEM))
    args = [x, mod, *weights, *vmem_consts, sink]

    if lat:
        cos, sa, sb, cka, cva, ckw, cvw, sc, sn, sm = extra
        in_specs += [_const_spec(cos.shape)] * 3
        kv_spec = pl.BlockSpec((None, None, N_KV, PAST_LEN, HEAD_DIM), lambda b: (b, layer, 0, 0, 0))
        in_specs += [kv_spec] * 4
        in_specs += [
            pl.BlockSpec((None, None, 2, ML_HEADS, ML_DV, ML_DK), lambda b: (b, layer, 0, 0, 0, 0)),
            pl.BlockSpec((None, None, 2, ML_HEADS, ML_DK), lambda b: (b, layer, 0, 0, 0)),
            pl.BlockSpec((None, None, 2, ML_HEADS), lambda b: (b, layer, 0, 0)),
        ]
        args += [cos, sa, sb, cka, cva, ckw, cvw, sc, sn, sm]
        out_specs = seq_spec
        out_shape = jax.ShapeDtypeStruct((B, T, D_MODEL), F32)
        aliases = {}
    else:
        first_acc = len(args)
        in_specs += [pl.BlockSpec(memory_space=pl.ANY)] * len(extra)
        args += list(extra)
        aliases = {first_acc + i: 1 + i for i in range(len(extra))}
        in_specs.append(pl.BlockSpec((None, T, D_MODEL), lambda b: (jnp.minimum(b + 1, B - 1), 0, 0)))
        args.append(x)
        assert len(extra) == CTX_ALIASED
        kv_out = pl.BlockSpec((None, None, N_KV, HEAD_DIM, T), lambda b: (b, layer, 0, 0, 0))
        out_specs = [
            seq_spec, kv_out, kv_out, kv_out, kv_out,
            pl.BlockSpec((None, None, 2, ML_HEADS, ML_DK, ML_DV), lambda b: (b, layer, 0, 0, 0, 0)),
            pl.BlockSpec((None, 2, ML_HEADS, ML_DK), lambda b: (b, 0, 0, 0)),
            pl.BlockSpec((None, 2, ML_HEADS), lambda b: (b, 0, 0)),
        ]
        out_shape = [jax.ShapeDtypeStruct((B, T, D_MODEL), F32)]
        out_shape += [jax.ShapeDtypeStruct(a.shape, a.dtype) for a in extra]
        out_shape += [
            jax.ShapeDtypeStruct((B, 2, ML_HEADS, ML_DK), F32),
            jax.ShapeDtypeStruct((B, 2, ML_HEADS), F32),
        ]

    scratch = [
        ((T, D_MODEL) if lat else (2, T, D_MODEL), BF16),
        ((N_HEADS, T, HEAD_DIM), BF16),
        ((N_HEADS, T, HEAD_DIM), BF16),
        ((N_KV, S, HEAD_DIM), BF16),
        ((S // CHUNK, N_KV, VT_ROWS, CHUNK), BF16),
        ((N_KV, T, HEAD_DIM), BF16),
        ((n_chunks, N_KV, VT_ROWS, CHUNK), BF16),
        ((T, ML_HEADS * ML_DK), BF16),
        ((T, ML_HEADS * ML_DK), BF16),
        ((n_chunks, ML_HEADS, ST_ROWS, CHUNK), BF16),
        ((T, ML_HEADS * ML_DV), F32),
        ((T, LANES), F32),
        ((T, BRANCH_W), BF16),
        ((T, BRANCH_W), BF16),
        ((n_chunks, ML_HEADS * ML_DV, CHUNK), F32),
        ((ML_HEADS, ST_ROWS, 2 * ML_DK), F32),
        ((2 * ML_HEADS, LANES), F32),
    ]
    if lat:
        scratch += [((N_KV, PAST_LEN, HEAD_DIM), BF16),
                    ((N_KV, VT_ROWS, PAST_LEN), BF16)]
    scratch = [pltpu.VMEM(shape, dtype) for shape, dtype in scratch]

    return pl.pallas_call(
        functools.partial(_layer_kernel, lat, T, layer),
        grid=(B,),
        in_specs=in_specs,
        out_specs=out_specs,
        out_shape=out_shape,
        scratch_shapes=scratch,
        input_output_aliases=aliases,
        compiler_params=pltpu.CompilerParams(
            dimension_semantics=("arbitrary",), vmem_limit_bytes=VMEM_LIMIT_BYTES),
        name=("latent_layer" if lat else "context_layer") + str(layer),
    )(*args)


def _rope_tables(T):
    rows = T // GRID_W
    row = jnp.repeat(jnp.arange(rows, dtype=F32), GRID_W)
    col = jnp.tile(jnp.arange(GRID_W, dtype=F32), rows)
    inv = ROPE_THETA ** (-jnp.arange(0, AXIS_DIM, 2, dtype=F32) / AXIS_DIM)
    ar = row[:, None] * inv
    ac = col[:, None] * inv
    ang = jnp.concatenate([ar, ar, ac, ac], axis=-1)
    ang = jnp.concatenate([ang, ang], axis=-1)
    first_half = (jnp.arange(LANES) % AXIS_DIM) < (AXIS_DIM // 2)
    cos = jnp.cos(ang)
    sin = jnp.sin(ang)
    sa = jnp.where(first_half[None, :], -sin, 0.0)
    sb = jnp.where(first_half[None, :], 0.0, sin)
    return cos, sa, sb


def _constants(sink_logit):
    grp = np.arange(N_HEADS * HEAD_DIM) // HEAD_DIM
    bd = jnp.asarray(grp[:, None] == grp[None, :], dtype=BF16)
    ti = np.arange(CHUNK)
    tril = jnp.asarray(ti[None, :] <= ti[:, None], dtype=BF16)
    triu = jnp.asarray(ti[None, :] >= ti[:, None], dtype=BF16)
    eye = jnp.asarray(np.eye(HEAD_DIM), dtype=BF16)
    return bd, tril, triu, eye, sink_logit.reshape(-1)


def kernel(x_prompt, x_sample, cache_attn_k, cache_attn_v, cache_win_k, cache_win_v, state_mlstm_C,
           state_mlstm_n, state_mlstm_m, c, c_ctx, w_mod, b_mod, w_in, qk_gain, sink_logit,
           mlstm_gate_bias, w_branch, w_out, ln_gain, ln_bias):
    dec_b = x_sample.shape[0]
    assert dec_b + 1 <= 8
    cond = jnp.concatenate([c_ctx[None, :], c, jnp.zeros((8 - 1 - dec_b, D_MODEL), F32)], axis=0)
    mod = _modulation(cond, w_mod, b_mod).reshape(DEPTH, 8, 3, D_MODEL)

    consts = _constants(sink_logit)
    cos, sa, sb = _rope_tables(x_sample.shape[1])
    lane_pad = jnp.zeros((DEPTH, 1, LANES - N_GATES), F32)
    weights = (
        _prep_in_weight(jnp.swapaxes(w_in, 1, 2)), w_branch.astype(BF16), w_out.astype(BF16),
        jnp.tile(qk_gain[:, 0:1, :], (1, 1, N_HEADS)),
        jnp.tile(qk_gain[:, 1:2, :], (1, 1, N_KV)),
        jnp.concatenate([mlstm_gate_bias.reshape(DEPTH, 1, N_GATES), lane_pad], axis=2),
        ln_gain[:, None, :], ln_bias[:, None, :],
    )

    B, T, _ = x_prompt.shape
    kv_t = jnp.zeros((B, DEPTH, N_KV, HEAD_DIM, T), F32)
    ctx = (kv_t, kv_t, kv_t, kv_t, jnp.zeros((B, DEPTH, 2, ML_HEADS, ML_DK, ML_DV), F32))
    xp, xs = x_prompt, x_sample
    small = []
    for l in range(DEPTH):
        outs = _layer_call(False, l, xp, mod, weights, consts, ctx)
        xp = outs[0]
        ctx = tuple(outs[1:6])
        small.append(outs[6:])
        extra = (cos, sa, sb, cache_attn_k, cache_attn_v, cache_win_k, cache_win_v,
                 state_mlstm_C, state_mlstm_n, state_mlstm_m)
        xs = _layer_call(True, l, xs, mod, weights, consts, extra)
    new_n, new_m = [jnp.stack([small[l][i] for l in range(DEPTH)], axis=1) for i in range(2)]
    return (xp, xs, *[jnp.swapaxes(a, -1, -2) for a in ctx], new_n, new_m)
```

```python
import functools

import jax
import jax.numpy as jnp
import numpy as np
from jax import lax
from jax.experimental import pallas as pl
from jax.experimental.pallas import tpu as pltpu

F32 = jnp.float32
BF16 = jnp.bfloat16

D_MODEL = 1024
DEPTH = 2
PAST_LEN = 256
GRID_W = 64
HEAD_DIM = 64
N_HEADS = 8
N_KV = 2
N_GROUP = N_HEADS // N_KV
WINDOW = 128
ML_HEADS = 4
ML_DK = 64
ML_DV = 128
CHUNK = 128
BRANCH_W = 512
ROPE_THETA = 10000.0
AXIS_DIM = HEAD_DIM // 2
LN_EPS = 1e-6
RMS_EPS = 1e-6
ALPHA = (2.0 * DEPTH) ** 0.25
LOG2E = float(np.log2(np.e))
LANES = 128
BF16_ROWS = 16
ROW_BLOCK = 256
CHAINS_IN_FLIGHT = 4
CTX_ALIASED = 5
N_SHARED = 15
VMEM_LIMIT_BYTES = 60 * 1024 * 1024
VT_ROWS = HEAD_DIM + BF16_ROWS
ST_ROWS = ML_DV + BF16_ROWS

_SIZES = (512, 128, 128, 512, 256, 256, 512, 512, 128, 512, 512, 128, 128, 512, 3072)
_OFF = [int(v) for v in np.concatenate([[0], np.cumsum(_SIZES)])]
(C_QA, C_KA, C_VA, C_ZA, C_QM, C_KM, C_VM, C_OM, C_GT, C_ZM, C_QW, C_KW, C_VW, C_ZW, C_GMERGE) = _OFF[:-1]
W_COLS = _OFF[-1]
GATE_COL = 2816
N_GATES = 4 * ML_HEADS


def _dot(a, b):
    return jnp.dot(a, b, preferred_element_type=F32)


def _dot_tb(a, b):
    return lax.dot_general(a, b, (((1,), (1,)), ((), ())), preferred_element_type=F32)


def _split(a):
    hi = a.astype(BF16)
    lo = (a - hi.astype(F32)).astype(BF16)
    return hi, lo


def _layer_norm(x):
    mu = jnp.mean(x, axis=-1, keepdims=True)
    xc = x - mu
    var = jnp.mean(xc * xc, axis=-1, keepdims=True)
    return xc * lax.rsqrt(var + LN_EPS)


def _log_sigmoid(x):
    return jnp.minimum(x, 0.0) - jnp.log(1.0 + jnp.exp(-jnp.abs(x)))


def _sigmoid(x):
    return jax.nn.sigmoid(x)


def _mod_kernel(c_ref, w_ref, b_ref, o_ref):
    c = c_ref[...]
    s = c * _sigmoid(c)
    s_hi, s_lo = _split(s)
    w = w_ref[...]
    w_hi, w_lo = _split(w)
    o_ref[...] = _dot(s_hi, w_hi) + _dot(s_lo, w_hi) + _dot(s_hi, w_lo) + b_ref[...]


def _modulation(cond, w_mod, b_mod):
    rows = cond.shape[0]
    tn = 1024
    return pl.pallas_call(
        _mod_kernel,
        grid=(DEPTH, 3 * D_MODEL // tn),
        in_specs=[
            pl.BlockSpec((rows, D_MODEL), lambda l, j: (0, 0)),
            pl.BlockSpec((None, D_MODEL, tn), lambda l, j: (l, 0, j)),
            pl.BlockSpec((None, 1, tn), lambda l, j: (l, 0, j)),
        ],
        out_specs=pl.BlockSpec((None, rows, tn), lambda l, j: (l, 0, j)),
        out_shape=jax.ShapeDtypeStruct((DEPTH, rows, 3 * D_MODEL), F32),
        compiler_params=pltpu.CompilerParams(dimension_semantics=("arbitrary", "arbitrary")),
        name="adaln_modulation",
    )(cond, w_mod, b_mod.reshape(DEPTH, 1, 3 * D_MODEL))


PREP_K = 256


def _prep_kernel(wt_ref, o_ref):
    lane = lax.broadcasted_iota(jnp.int32, (PREP_K, LANES), 1)
    for c0 in range(0, W_COLS, LANES):
        r0 = c0 if c0 <= GATE_COL else c0 - (LANES - N_GATES)
        slab = jnp.transpose(wt_ref[r0:r0 + LANES, :])
        if c0 == GATE_COL:
            slab = jnp.where(lane < N_GATES, slab, 0.0)
        o_ref[:, c0:c0 + LANES] = slab.astype(BF16)


def _prep_in_weight(w_in_t):
    n_in = w_in_t.shape[1]
    return pl.pallas_call(
        _prep_kernel,
        grid=(DEPTH, D_MODEL // PREP_K),
        in_specs=[pl.BlockSpec((None, n_in, PREP_K), lambda l, i: (l, 0, i))],
        out_specs=pl.BlockSpec((None, PREP_K, W_COLS), lambda l, i: (l, i, 0)),
        out_shape=jax.ShapeDtypeStruct((DEPTH, D_MODEL, W_COLS), BF16),
        compiler_params=pltpu.CompilerParams(
            dimension_semantics=("arbitrary", "arbitrary"), vmem_limit_bytes=VMEM_LIMIT_BYTES),
        name="in_weight_prep",
    )(w_in_t)


def _rope(x, cos, sa, sb):
    return x * cos + pltpu.roll(x, LANES - AXIS_DIM // 2, 1) * sa + pltpu.roll(x, AXIS_DIM // 2, 1) * sb


def _group_rms(x, bd, gain):
    ms = _dot((x * x).astype(BF16), bd) * (1.0 / HEAD_DIM)
    return x * lax.rsqrt(ms + RMS_EPS) * gain


def _ones_row_tile():
    r = lax.broadcasted_iota(jnp.int32, (BF16_ROWS, LANES), 0)
    return jnp.where(r == 0, 1.0, 0.0).astype(BF16)


def _attend_chains(chains, results):
    state = []
    for qs, tiles, sink_row in chains:
        m_cols = qs.shape[0]
        if sink_row is None:
            m = jnp.full((1, m_cols), -jnp.inf, F32)
            acc = jnp.zeros((VT_ROWS, m_cols), F32)
        else:
            m = sink_row
            r = lax.broadcasted_iota(jnp.int32, (VT_ROWS, m_cols), 0)
            acc = jnp.where(r == HEAD_DIM, 1.0, 0.0)
        cur = tiles[0]()
        state.append([m, acc, cur, _dot_tb(cur[0], qs)])
    yield
    for t in range(max(len(tiles) for _, tiles, _ in chains)):
        for st, (qs, tiles, _) in zip(state, chains):
            if t >= len(tiles):
                continue
            m, acc, (_, v_t, valid), s = st
            if valid is not None:
                s = jnp.where(valid, s, -jnp.inf)
            if t + 1 < len(tiles):
                nxt = tiles[t + 1]()
                st[2], st[3] = nxt, _dot_tb(nxt[0], qs)
            m_new = jnp.maximum(m, jnp.max(s, axis=0, keepdims=True))
            p = jnp.exp2(s - m_new).astype(BF16)
            st[1] = acc * jnp.exp2(m - m_new) + _dot(v_t, p)
            st[0] = m_new
        yield
    results.extend(st[1][0:HEAD_DIM, :] / st[1][HEAD_DIM:HEAD_DIM + 1, :] for st in state)


def _interleave(weighted):
    live = [[g, n] for g, n in weighted]
    while live:
        for item in list(live):
            for _ in range(item[1]):
                try:
                    next(item[0])
                except StopIteration:
                    live.remove(item)
                    break
        yield


def _run(gen):
    for _ in gen:
        pass


def _layer_kernel(lat, T, layer, *refs):
    _run(_sequence(lat, T, layer, *refs))


def _sequence(lat, T, layer, *refs):
    refs = list(refs)
    x_ref, mod_ref, w_ref, wb_ref, wo_ref = refs[:5]
    g0_ref, g1_ref, gb_ref, lng_ref, lnb_ref = refs[5:10]
    bd_ref, tril_ref, triu_ref, eye_ref, sink_ref = refs[10:N_SHARED]
    pos = N_SHARED
    if lat:
        cos_ref, sa_ref, sb_ref = refs[pos:pos + 3]
        cka_ref, cva_ref, ckw_ref, cvw_ref, c0_ref, n0_ref, m0_ref = refs[pos + 3:pos + 10]
        pos += 10
        y_ref = refs[pos]
        pos += 1
    else:
        pos += CTX_ALIASED
        x_next_ref = refs[pos]
        pos += 1
        y_ref, ka_o, va_o, kw_o, vw_o, c_o, n_o, m_o = refs[pos:pos + 8]
        pos += 8
    (u_s, qa_s, qw_s, ka_s, vat_s, kw_s, vwt_s, qm_s, km_s, vmt_s, om_s, g_s,
     ya_s, yw_s, hmt_s, c_s, m_s) = refs[pos:pos + 17]
    pos += 17
    if not lat:
        step = pl.program_id(0)
        u_next_s = u_s.at[(step + 1) % 2]
        u_s = u_s.at[step % 2]
    if lat:
        kwc_s, vwct_s = refs[pos:pos + 2]

    row_block = min(T, ROW_BLOCK)
    n_rows = T // row_block
    n_chunks = T // CHUNK
    chunks_per_block = row_block // CHUNK
    shift = mod_ref[0:1, :]
    scale = mod_ref[1:2, :]
    gate = mod_ref[2:3, :]
    ones_tile = _ones_row_tile()

    norm_rows = 64

    def row_slice(i, j, n):
        r0 = i * row_block + j
        return pl.ds(r0 if isinstance(r0, int) else pl.multiple_of(r0, n), n)

    def phase1_norm(i, src=x_ref, dst=u_s):
        for j in range(0, row_block, norm_rows):
            piece = row_slice(i, j, norm_rows)
            dst[piece, :] = (_layer_norm(src[piece, :]) * (1.0 + scale) + shift).astype(BF16)
            yield

    def phase1_proj(i):
        R = row_block
        rows = row_slice(i, 0, R)
        u = u_s[rows, :]

        def proj(c0, width):
            return _dot(u, w_ref[:, c0:c0 + width])

        if lat:
            cos = cos_ref[rows, :]
            sa = sa_ref[rows, :]
            sb = sb_ref[rows, :]

        def rope_wide(v):
            if not lat:
                return v
            slabs = [_rope(v[:, j:j + LANES], cos, sa, sb) for j in range(0, v.shape[1], LANES)]
            return slabs[0] if len(slabs) == 1 else jnp.concatenate(slabs, axis=1)

        qscale = LOG2E * HEAD_DIM ** -0.5
        qa = rope_wide(_group_rms(proj(C_QA, 512), bd_ref[...], g0_ref[...])) * qscale
        for h in range(N_HEADS):
            qa_s[h, rows, :] = qa[:, h * HEAD_DIM:(h + 1) * HEAD_DIM].astype(BF16)
        yield
        ka_n = _group_rms(proj(C_KA, LANES), bd_ref[0:LANES, 0:LANES], g1_ref[...])
        ka = rope_wide(ka_n)
        va = proj(C_VA, LANES)
        qw = rope_wide(proj(C_QW, 512)) * qscale
        for h in range(N_HEADS):
            qw_s[h, rows, :] = qw[:, h * HEAD_DIM:(h + 1) * HEAD_DIM].astype(BF16)
        yield
        kw_raw = proj(C_KW, LANES)
        kw = rope_wide(kw_raw)
        vw = proj(C_VW, LANES)
        for kv in range(N_KV):
            sl = slice(kv * HEAD_DIM, (kv + 1) * HEAD_DIM)
            ka_s[kv, rows, :] = ka[:, sl].astype(BF16)
            kw_s[kv, rows, :] = kw[:, sl].astype(BF16)
        yield
        qm_s[rows, :] = proj(C_QM, 256).astype(BF16)
        km_s[rows, :] = (proj(C_KM, 256) * (ML_DK ** -0.5)).astype(BF16)
        yield
        vm = proj(C_VM, 512)
        yield
        om_s[rows, :] = proj(C_OM, 512)
        g_s[rows, :] = proj(C_GT, LANES) + gb_ref[...]
        yield
        for j in range(chunks_per_block):
            cj = i * chunks_per_block + j
            cr = slice(j * CHUNK, (j + 1) * CHUNK)
            va_t = jnp.transpose(va[cr, :])
            vw_t = jnp.transpose(vw[cr, :])
            for kv in range(N_KV):
                hd = slice(kv * HEAD_DIM, (kv + 1) * HEAD_DIM)
                vat_s[cj, kv, 0:HEAD_DIM, :] = va_t[hd, :].astype(BF16)
                vat_s[cj, kv, HEAD_DIM:VT_ROWS, :] = ones_tile
                vwt_s[cj, kv, 0:HEAD_DIM, :] = vw_t[hd, :].astype(BF16)
                vwt_s[cj, kv, HEAD_DIM:VT_ROWS, :] = ones_tile
            if not lat:
                tc = slice(i * row_block + j * CHUNK, i * row_block + (j + 1) * CHUNK)
                ka_t = jnp.transpose(ka_n[cr, :])
                kw_t = jnp.transpose(kw_raw[cr, :])
                for kv in range(N_KV):
                    hd = slice(kv * HEAD_DIM, (kv + 1) * HEAD_DIM)
                    ka_o[kv, :, tc] = ka_t[hd, :]
                    va_o[kv, :, tc] = va_t[hd, :]
                    kw_o[kv, :, tc] = kw_t[hd, :]
                    vw_o[kv, :, tc] = vw_t[hd, :]
            for h in range(ML_HEADS):
                vmt_s[cj, h, 0:ML_DV, :] = jnp.transpose(vm[cr, h * ML_DV:(h + 1) * ML_DV]).astype(BF16)
                vmt_s[cj, h, ML_DV:ST_ROWS, :] = ones_tile
            yield

    if not lat:
        @pl.when(step == 0)
        def _():
            _run(phase1_norm(0))
        yield from phase1_proj(0)
    else:
        def body1(i, carry):
            _run(phase1_norm(i))
            _run(phase1_proj(i))
            return carry
        lax.fori_loop(0, n_rows, body1, 0)

    if lat:
        eye = eye_ref[...]
        for kv in range(N_KV):
            ka_s[kv, T:T + PAST_LEN, :] = cka_ref[kv].astype(BF16)
            kwc_s[kv] = ckw_ref[kv].astype(BF16)
            cva_t = _dot_tb(eye, cva_ref[kv].astype(BF16)).astype(BF16)
            cvw_t = _dot_tb(eye, cvw_ref[kv].astype(BF16)).astype(BF16)
            for j in range(PAST_LEN // CHUNK):
                vat_s[n_chunks + j, kv, 0:HEAD_DIM, :] = cva_t[:, j * CHUNK:(j + 1) * CHUNK]
                vat_s[n_chunks + j, kv, HEAD_DIM:VT_ROWS, :] = ones_tile
                vwct_s[kv, 0:HEAD_DIM, j * CHUNK:(j + 1) * CHUNK] = cvw_t[:, j * CHUNK:(j + 1) * CHUNK]
                vwct_s[kv, HEAD_DIM:VT_ROWS, j * CHUNK:(j + 1) * CHUNK] = ones_tile

    tq = CHUNK
    m_cols = N_GROUP * tq

    def sink_row(kv):
        col = lax.broadcasted_iota(jnp.int32, (1, m_cols), 1)
        row = jnp.full((1, m_cols), sink_ref[layer * N_HEADS + kv * N_GROUP], F32)
        for g in range(1, N_GROUP):
            row = jnp.where(col >= g * tq, sink_ref[layer * N_HEADS + kv * N_GROUP + g], row)
        return row * LOG2E

    def load_q(q_s, kv, q0):
        return q_s[kv * N_GROUP:(kv + 1) * N_GROUP, pl.ds(q0, tq), :].reshape(m_cols, HEAD_DIM)

    def store_heads(dst, q0, kv, o_t):
        for p in range(N_GROUP // 2):
            blk = jnp.concatenate([o_t[:, (2 * p) * tq:(2 * p + 1) * tq],
                                   o_t[:, (2 * p + 1) * tq:(2 * p + 2) * tq]], axis=0)
            c0 = (kv * N_GROUP + 2 * p) * HEAD_DIM
            dst[pl.ds(q0, tq), c0:c0 + LANES] = jnp.transpose(blk).astype(dst.dtype)

    def key_tiles(k_ref, vt_ref, kv, n_key_chunks, chunks_per_tile):
        def tile(c0):
            def load():
                k = k_ref[kv, c0 * CHUNK:(c0 + chunks_per_tile) * CHUNK, :]
                v_t = [vt_ref[c0 + j, kv] for j in range(chunks_per_tile)]
                return k, (v_t[0] if len(v_t) == 1 else jnp.concatenate(v_t, axis=1)), None
            return load
        return [tile(c0) for c0 in range(0, n_key_chunks, chunks_per_tile)]

    def band_tiles(kv, n, q0):
        j0 = jnp.clip(n - 1, 0, n_chunks - 3)
        w0 = pl.multiple_of(j0 * CHUNK, CHUNK)

        def band(c0, nc):
            def load():
                rows = nc * CHUNK
                kpos = w0 + c0 * CHUNK + lax.broadcasted_iota(jnp.int32, (rows, m_cols), 0)
                qpos = q0 + (lax.broadcasted_iota(jnp.int32, (rows, m_cols), 1) & (tq - 1))
                v_t = [vwt_s[j0 + c0 + j, kv] for j in range(nc)]
                return (kw_s[kv, pl.ds(w0 + c0 * CHUNK, rows), :],
                        v_t[0] if nc == 1 else jnp.concatenate(v_t, axis=1),
                        jnp.abs(kpos - qpos) <= WINDOW)
            return load
        return [band(0, 2), band(2, 1), lambda: (kwc_s[kv], vwct_s[kv], None)]

    def attend(n, q0, chunks_per_tile):
        chains, dsts = [], []
        n_key_chunks = ka_s.shape[1] // CHUNK
        for kv in range(N_KV):
            chains.append((load_q(qa_s, kv, q0), key_tiles(ka_s, vat_s, kv, n_key_chunks, chunks_per_tile), None))
            dsts.append((ya_s, kv))
        for kv in range(N_KV):
            tiles = band_tiles(kv, n, q0) if lat else key_tiles(kw_s, vwt_s, kv, n_chunks, chunks_per_tile)
            chains.append((load_q(qw_s, kv, q0), tiles, sink_row(kv)))
            dsts.append((yw_s, kv))
        for g0 in range(0, len(chains), CHAINS_IN_FLIGHT):
            results = []
            yield from _attend_chains(chains[g0:g0 + CHAINS_IN_FLIGHT], results)
            for (dst, kv), o_t in zip(dsts[g0:g0 + CHAINS_IN_FLIGHT], results):
                store_heads(dst, q0, kv, o_t)

    n_pairs = ML_HEADS // 2
    hmt_s[...] = jnp.zeros_like(hmt_s)
    for d in range(2):
        for h in range(ML_HEADS):
            idx = d * n_pairs + h // 2
            hl = slice((h % 2) * ML_DK, (h % 2 + 1) * ML_DK)
            c_s[idx, ML_DV:ST_ROWS, hl] = jnp.zeros((BF16_ROWS, ML_DK), F32)
            if lat:
                c_s[idx, 0:ML_DV, hl] = c0_ref[d, h]
                c_s[idx, ML_DV:ML_DV + 1, hl] = n0_ref[d, h:h + 1, :]
                m_s[d * ML_HEADS + h:d * ML_HEADS + h + 1, :] = jnp.broadcast_to(
                    m0_ref[d:d + 1, h:h + 1], (1, LANES))
            else:
                c_s[idx, 0:ML_DV, hl] = jnp.zeros((ML_DV, ML_DK), F32)
                m_s[d * ML_HEADS + h:d * ML_HEADS + h + 1, :] = jnp.zeros((1, LANES), F32)

    L = CHUNK
    s_idx = lax.broadcasted_iota(jnp.int32, (L, ML_HEADS * L), 0)
    t_idx = lax.broadcasted_iota(jnp.int32, (L, ML_HEADS * L), 1) & (L - 1)
    lane_row = lax.broadcasted_iota(jnp.int32, (1, LANES), 1)
    low_half = lax.broadcasted_iota(jnp.int32, (L, LANES), 1) < ML_DK
    ones_ll = jnp.ones((L, L), BF16)
    zeros_ll = jnp.zeros((L, L), BF16)

    def heads_row(src, r0, c0=0):
        return jnp.concatenate([src[r0 + h:r0 + h + 1, c0:c0 + L] for h in range(ML_HEADS)], axis=1)

    def block_diag_rows(x):
        zero = jnp.zeros_like(x)
        return jnp.concatenate([jnp.where(low_half, x, zero), jnp.where(low_half, zero, x)], axis=0)

    def mlstm_streams(streams):
        m_state = {d: heads_row(m_s, d * ML_HEADS) for d in sorted({d for d, _ in streams})}
        c_state = {d: [c_s[d * n_pairs + pr] for pr in range(n_pairs)] for d in m_state}
        work = []
        for d, cc in streams:
            r0 = cc * L if isinstance(cc, int) else pl.multiple_of(cc * L, L)
            rows = pl.ds(r0, L)
            g = g_s[rows, :]
            f_hi, f_lo = _split(_log_sigmoid(g))
            tri_c = tril_ref[...] if d == 0 else triu_ref[...]
            tri_r = triu_ref[...] if d == 0 else tril_ref[...]
            cum = _dot(tri_c, f_hi) + _dot(tri_c, f_lo)
            r = g - pltpu.roll(cum, LANES - ML_HEADS, 1)
            gi0 = 2 * ML_HEADS * d
            rb = jnp.concatenate([jnp.broadcast_to(r[:, gi0 + h:gi0 + h + 1], (L, L))
                                  for h in range(ML_HEADS)], axis=1)
            g_t = jnp.transpose(g)[0:N_GATES, :]
            ft_hi, ft_lo = _split(_log_sigmoid(g_t))
            tr = jnp.concatenate([tri_r, ones_ll], axis=1)
            ct = _dot(ft_hi, tr) + _dot(ft_lo, tr)
            work.append(dict(d=d, cc=cc, rows=rows, rb=rb, g_t=g_t, ct=ct))
            yield
        for w in work:
            d = w["d"]
            valid = (s_idx <= t_idx) if d == 0 else (s_idx >= t_idx)
            gi0 = 2 * ML_HEADS * d
            gf0 = gi0 + ML_HEADS
            b_row = heads_row(w["ct"], gf0)
            b_last = heads_row(w["ct"], gf0, L)
            i_row = heads_row(w["g_t"], gi0)
            m_prev = m_state[d]
            a_row = b_row + m_prev
            dm = jnp.where(valid, b_row + w["rb"], -jnp.inf)
            mt = jnp.maximum(a_row, jnp.max(dm, axis=0, keepdims=True))
            w["p"] = jnp.exp(dm - mt)
            w["w_inter"] = jnp.exp(a_row - mt)
            w["floor"] = jnp.exp(-mt)
            g_row = b_last - b_row + i_row
            g_max = jnp.concatenate(
                [jnp.broadcast_to(jnp.max(g_row[:, h * L:(h + 1) * L], axis=-1, keepdims=True), (1, L))
                 for h in range(ML_HEADS)], axis=1)
            m_new = jnp.maximum(b_last + m_prev, g_max)
            w["ws"] = jnp.exp(g_row - m_new)
            w["wc"] = jnp.exp(b_last + m_prev - m_new)
            m_state[d] = m_new
            yield
        for w in work:
            rows, cc = w["rows"], w["cc"]
            w["pairs"] = []
            for pr in range(n_pairs):
                lanes = slice(pr * LANES, (pr + 1) * LANES)
                cols = slice(pr * 2 * L, (pr + 1) * 2 * L)
                k_pair = km_s[rows, lanes]
                q_bd = block_diag_rows(qm_s[rows, lanes])
                k_bd = block_diag_rows(k_pair)
                s_t = (_dot_tb(k_pair, q_bd) * w["p"][:, cols]).astype(BF16)
                s_bd = jnp.concatenate(
                    [jnp.concatenate([s_t[:, 0:L], zeros_ll], axis=1),
                     jnp.concatenate([zeros_ll, s_t[:, L:2 * L]], axis=1)], axis=0)
                v_t = jnp.concatenate([vmt_s[cc, 2 * pr], vmt_s[cc, 2 * pr + 1]], axis=1)
                intra = _dot(v_t, s_bd)
                wv = (v_t.astype(F32) * w["ws"][:, cols]).astype(BF16)
                w["pairs"].append((q_bd, intra, _dot(wv, k_bd)))
            yield
        for w in work:
            d, cc = w["d"], w["cc"]
            for pr, (q_bd, intra, update) in enumerate(w["pairs"]):
                cols = slice(pr * 2 * L, (pr + 1) * 2 * L)
                state = c_state[d][pr]
                tot = intra + w["w_inter"][:, cols] * _dot_tb(state.astype(BF16), q_bd)
                h_t = tot[0:ML_DV, :] / jnp.maximum(jnp.abs(tot[ML_DV:ML_DV + 1, :]), w["floor"][:, cols])
                for e in range(2):
                    hr = slice((2 * pr + e) * ML_DV, (2 * pr + e + 1) * ML_DV)
                    hmt_s[cc, hr, :] += h_t[:, e * L:(e + 1) * L]
                wc = w["wc"]
                wc_pair = jnp.where(lane_row < ML_DK, wc[:, 2 * pr * L:(2 * pr + 1) * L],
                                    wc[:, (2 * pr + 1) * L:(2 * pr + 2) * L])
                c_state[d][pr] = wc_pair * state + update
            yield
        for d in m_state:
            for h in range(ML_HEADS):
                m_s[d * ML_HEADS + h:d * ML_HEADS + h + 1, :] = m_state[d][:, h * L:(h + 1) * L]
            for pr in range(n_pairs):
                c_s[d * n_pairs + pr] = c_state[d][pr]

    def scan_streams(cs):
        return [(d, c if d == 0 else n_chunks - 1 - c) for c in cs for d in range(2)]

    if lat:
        def body2(n, carry):
            _run(_interleave([(attend(n, pl.multiple_of(n * tq, tq), 2), 1),
                              (mlstm_streams(scan_streams([n])), 1)]))
            return carry
        lax.fori_loop(0, n_chunks, body2, 0)
    else:
        def all_tiles():
            for n in range(T // tq):
                yield from attend(n, n * tq, 2)
        yield from _interleave([(all_tiles(), 1), (mlstm_streams(scan_streams(range(n_chunks))), 3)])

    if not lat:
        for d in range(2):
            for pr in range(n_pairs):
                c_t = jnp.transpose(c_s[d * n_pairs + pr, 0:ML_DV, :])
                for e in range(2):
                    c_o[d, 2 * pr + e] = c_t[e * ML_DK:(e + 1) * ML_DK, :]
            for h in range(ML_HEADS):
                idx = d * n_pairs + h // 2
                hl = slice((h % 2) * ML_DK, (h % 2 + 1) * ML_DK)
                n_o[d, h:h + 1, :] = c_s[idx, ML_DV:ML_DV + 1, hl]
                m_o[d:d + 1, h:h + 1] = m_s[d * ML_HEADS + h:d * ML_HEADS + h + 1, 0:1]

    def phase3_merge(i, out):
        rows = row_slice(i, 0, row_block)
        u = u_s[rows, :]

        def proj(c0, width):
            return _dot(u, w_ref[:, c0:c0 + width])

        hm = jnp.concatenate(
            [jnp.concatenate([jnp.transpose(hmt_s[i * chunks_per_block + j, h * ML_DV:(h + 1) * ML_DV, :])
                              for h in range(ML_HEADS)], axis=1)
             for j in range(chunks_per_block)], axis=0)
        ys = (ya_s[rows, :].astype(F32), _sigmoid(om_s[rows, :]) * hm, yw_s[rows, :].astype(F32))
        yield
        merged = None
        for b, (yb, zc) in enumerate(zip(ys, (C_ZA, C_ZM, C_ZW))):
            z = proj(zc, BRANCH_W)
            t = (yb * (z * _sigmoid(z))).astype(BF16)
            yield
            pb = _sigmoid(proj(C_GMERGE + b * D_MODEL, D_MODEL)) * _dot(t, wb_ref[b])
            merged = pb if merged is None else merged + pb
            yield
        out.append(_dot(merged.astype(BF16), wo_ref[...]))
        yield

    def phase3_norm(i, o):
        for j in range(0, row_block, norm_rows):
            piece = row_slice(i, j, norm_rows)
            hres = ALPHA * x_ref[piece, :] + gate * o[j:j + norm_rows, :]
            y_ref[piece, :] = _layer_norm(hres) * lng_ref[...] + lnb_ref[...]
            yield

    def phase3(i):
        box = []
        yield from phase3_merge(i, box)
        yield from phase3_norm(i, box[0])

    if not lat:
        box = []
        yield from _interleave([(phase3_merge(0, box), 1), (phase1_norm(0, x_next_ref, u_next_s), 1)])
        yield from phase3_norm(0, box[0])
    else:
        def body4(i, carry):
            _run(phase3(i))
            return carry
        lax.fori_loop(0, n_rows, body4, 0)


def _const_spec(shape):
    nd = len(shape)
    return pl.BlockSpec(shape, lambda b: (0,) * nd, pipeline_mode=pl.Buffered(1))


def _layer_spec(shape, layer):
    nd = len(shape) - 1
    return pl.BlockSpec((None,) + tuple(shape[1:]), lambda b: (layer,) + (0,) * nd,
                        pipeline_mode=pl.Buffered(1))


def _layer_call(lat, layer, x, mod, weights, consts, extra):
    B, T, _ = x.shape
    S = T + PAST_LEN if lat else T
    n_chunks = T // CHUNK

    if lat:
        seq_spec = pl.BlockSpec((None, T, D_MODEL), lambda b: (b, 0, 0), pipeline_mode=pl.Buffered(1))
    else:
        seq_spec = pl.BlockSpec((None, T, D_MODEL), lambda b: (b, 0, 0))
    if lat:
        mod_spec = pl.BlockSpec((None, None, 3, D_MODEL), lambda b: (layer, b + 1, 0, 0))
    else:
        mod_spec = pl.BlockSpec((None, None, 3, D_MODEL), lambda b: (layer, 0, 0, 0))
    *vmem_consts, sink = consts
    in_specs = [seq_spec, mod_spec]
    in_specs += [_layer_spec(a.shape, layer) for a in weights]
    in_specs += [_const_spec(a.shape) for a in vmem_consts]
    in_specs.append(pl.BlockSpec(memory_space=pltpu.SMEM))
    args = [x, mod, *weights, *vmem_consts, sink]

    if lat:
        cos, sa, sb, cka, cva, ckw, cvw, sc, sn, sm = extra
        in_specs += [_const_spec(cos.shape)] * 3
        kv_spec = pl.BlockSpec((None, None, N_KV, PAST_LEN, HEAD_DIM), lambda b: (b, layer, 0, 0, 0))
        in_specs += [kv_spec] * 4
        in_specs += [
            pl.BlockSpec((None, None, 2, ML_HEADS, ML_DV, ML_DK), lambda b: (b, layer, 0, 0, 0, 0)),
            pl.BlockSpec((None, None, 2, ML_HEADS, ML_DK), lambda b: (b, layer, 0, 0, 0)),
            pl.BlockSpec((None, None, 2, ML_HEADS), lambda b: (b, layer, 0, 0)),
        ]
        args += [cos, sa, sb, cka, cva, ckw, cvw, sc, sn, sm]
        out_specs = seq_spec
        out_shape = jax.ShapeDtypeStruct((B, T, D_MODEL), F32)
        aliases = {}
    else:
        first_acc = len(args)
        in_specs += [pl.BlockSpec(memory_space=pl.ANY)] * len(extra)
        args += list(extra)
        aliases = {first_acc + i: 1 + i for i in range(len(extra))}
        in_specs.append(pl.BlockSpec((None, T, D_MODEL), lambda b: (jnp.minimum(b + 1, B - 1), 0, 0)))
        args.append(x)
        assert len(extra) == CTX_ALIASED
        kv_out = pl.BlockSpec((None, None, N_KV, HEAD_DIM, T), lambda b: (b, layer, 0, 0, 0))
        out_specs = [
            seq_spec, kv_out, kv_out, kv_out, kv_out,
            pl.BlockSpec((None, None, 2, ML_HEADS, ML_DK, ML_DV), lambda b: (b, layer, 0, 0, 0, 0)),
            pl.BlockSpec((None, 2, ML_HEADS, ML_DK), lambda b: (b, 0, 0, 0)),
            pl.BlockSpec((None, 2, ML_HEADS), lambda b: (b, 0, 0)),
        ]
        out_shape = [jax.ShapeDtypeStruct((B, T, D_MODEL), F32)]
        out_shape += [jax.ShapeDtypeStruct(a.shape, a.dtype) for a in extra]
        out_shape += [
            jax.ShapeDtypeStruct((B, 2, ML_HEADS, ML_DK), F32),
            jax.ShapeDtypeStruct((B, 2, ML_HEADS), F32),
        ]

    scratch = [
        ((T, D_MODEL) if lat else (2, T, D_MODEL), BF16),
        ((N_HEADS, T, HEAD_DIM), BF16),
        ((N_HEADS, T, HEAD_DIM), BF16),
        ((N_KV, S, HEAD_DIM), BF16),
        ((S // CHUNK, N_KV, VT_ROWS, CHUNK), BF16),
        ((N_KV, T, HEAD_DIM), BF16),
        ((n_chunks, N_KV, VT_ROWS, CHUNK), BF16),
        ((T, ML_HEADS * ML_DK), BF16),
        ((T, ML_HEADS * ML_DK), BF16),
        ((n_chunks, ML_HEADS, ST_ROWS, CHUNK), BF16),
        ((T, ML_HEADS * ML_DV), F32),
        ((T, LANES), F32),
        ((T, BRANCH_W), BF16),
        ((T, BRANCH_W), BF16),
        ((n_chunks, ML_HEADS * ML_DV, CHUNK), F32),
        ((ML_HEADS, ST_ROWS, 2 * ML_DK), F32),
        ((2 * ML_HEADS, LANES), F32),
    ]
    if lat:
        scratch += [((N_KV, PAST_LEN, HEAD_DIM), BF16),
                    ((N_KV, VT_ROWS, PAST_LEN), BF16)]
    scratch = [pltpu.VMEM(shape, dtype) for shape, dtype in scratch]

    return pl.pallas_call(
        functools.partial(_layer_kernel, lat, T, layer),
        grid=(B,),
        in_specs=in_specs,
        out_specs=out_specs,
        out_shape=out_shape,
        scratch_shapes=scratch,
        input_output_aliases=aliases,
        compiler_params=pltpu.CompilerParams(
            dimension_semantics=("arbitrary",), vmem_limit_bytes=VMEM_LIMIT_BYTES),
        name=("latent_layer" if lat else "context_layer") + str(layer),
    )(*args)


def _rope_tables(T):
    rows = T // GRID_W
    row = jnp.repeat(jnp.arange(rows, dtype=F32), GRID_W)
    col = jnp.tile(jnp.arange(GRID_W, dtype=F32), rows)
    inv = ROPE_THETA ** (-jnp.arange(0, AXIS_DIM, 2, dtype=F32) / AXIS_DIM)
    ar = row[:, None] * inv
    ac = col[:, None] * inv
    ang = jnp.concatenate([ar, ar, ac, ac], axis=-1)
    ang = jnp.concatenate([ang, ang], axis=-1)
    first_half = (jnp.arange(LANES) % AXIS_DIM) < (AXIS_DIM // 2)
    cos = jnp.cos(ang)
    sin = jnp.sin(ang)
    sa = jnp.where(first_half[None, :], -sin, 0.0)
    sb = jnp.where(first_half[None, :], 0.0, sin)
    return cos, sa, sb


def _constants(sink_logit):
    grp = np.arange(N_HEADS * HEAD_DIM) // HEAD_DIM
    bd = jnp.asarray(grp[:, None] == grp[None, :], dtype=BF16)
    ti = np.arange(CHUNK)
    tril = jnp.asarray(ti[None, :] <= ti[:, None], dtype=BF16)
    triu = jnp.asarray(ti[None, :] >= ti[:, None], dtype=BF16)
    eye = jnp.asarray(np.eye(HEAD_DIM), dtype=BF16)
    return bd, tril, triu, eye, sink_logit.reshape(-1)


def kernel(x_prompt, x_sample, cache_attn_k, cache_attn_v, cache_win_k, cache_win_v, state_mlstm_C,
           state_mlstm_n, state_mlstm_m, c, c_ctx, w_mod, b_mod, w_in, qk_gain, sink_logit,
           mlstm_gate_bias, w_branch, w_out, ln_gain, ln_bias):
    dec_b = x_sample.shape[0]
    assert dec_b + 1 <= 8
    cond = jnp.concatenate([c_ctx[None, :], c, jnp.zeros((8 - 1 - dec_b, D_MODEL), F32)], axis=0)
    mod = _modulation(cond, w_mod, b_mod).reshape(DEPTH, 8, 3, D_MODEL)

    consts = _constants(sink_logit)
    cos, sa, sb = _rope_tables(x_sample.shape[1])
    lane_pad = jnp.zeros((DEPTH, 1, LANES - N_GATES), F32)
    weights = (
        _prep_in_weight(jnp.swapaxes(w_in, 1, 2)), w_branch.astype(BF16), w_out.astype(BF16),
        jnp.tile(qk_gain[:, 0:1, :], (1, 1, N_HEADS)),
        jnp.tile(qk_gain[:, 1:2, :], (1, 1, N_KV)),
        jnp.concatenate([mlstm_gate_bias.reshape(DEPTH, 1, N_GATES), lane_pad], axis=2),
        ln_gain[:, None, :], ln_bias[:, None, :],
    )

    B, T, _ = x_prompt.shape
    kv_t = jnp.zeros((B, DEPTH, N_KV, HEAD_DIM, T), F32)
    ctx = (kv_t, kv_t, kv_t, kv_t, jnp.zeros((B, DEPTH, 2, ML_HEADS, ML_DK, ML_DV), F32))
    xp, xs = x_prompt, x_sample
    small = []
    for l in range(DEPTH):
        outs = _layer_call(False, l, xp, mod, weights, consts, ctx)
        xp = outs[0]
        ctx = tuple(outs[1:6])
        small.append(outs[6:])
        extra = (cos, sa, sb, cache_attn_k, cache_attn_v, cache_win_k, cache_win_v,
                 state_mlstm_C, state_mlstm_n, state_mlstm_m)
        xs = _layer_call(True, l, xs, mod, weights, consts, extra)
    new_n, new_m = [jnp.stack([small[l][i] for l in range(DEPTH)], axis=1) for i in range(2)]
    return (xp, xs, *[jnp.swapaxes(a, -1, -2) for a in ctx], new_n, new_m)
```

```python
import functools

import jax
import jax.numpy as jnp
import numpy as np
from jax import lax
from jax.experimental import pallas as pl
from jax.experimental.pallas import tpu as pltpu

F32 = jnp.float32
BF16 = jnp.bfloat16

D_MODEL = 1024
DEPTH = 2
PAST_LEN = 256
GRID_W = 64
HEAD_DIM = 64
N_HEADS = 8
N_KV = 2
N_GROUP = N_HEADS // N_KV
WINDOW = 128
ML_HEADS = 4
ML_DK = 64
ML_DV = 128
CHUNK = 128
BRANCH_W = 512
ROPE_THETA = 10000.0
AXIS_DIM = HEAD_DIM // 2
LN_EPS = 1e-6
RMS_EPS = 1e-6
ALPHA = (2.0 * DEPTH) ** 0.25
LOG2E = float(np.log2(np.e))
LANES = 128
BF16_ROWS = 16
ROW_BLOCK = 256
CHAINS_IN_FLIGHT = 4
CTX_ALIASED = 5
N_SHARED = 15
VMEM_LIMIT_BYTES = 60 * 1024 * 1024
VT_ROWS = HEAD_DIM + BF16_ROWS
ST_ROWS = ML_DV + BF16_ROWS

_SIZES = (512, 128, 128, 512, 256, 256, 512, 512, 128, 512, 512, 128, 128, 512, 3072)
_OFF = [int(v) for v in np.concatenate([[0], np.cumsum(_SIZES)])]
(C_QA, C_KA, C_VA, C_ZA, C_QM, C_KM, C_VM, C_OM, C_GT, C_ZM, C_QW, C_KW, C_VW, C_ZW, C_GMERGE) = _OFF[:-1]
W_COLS = _OFF[-1]
GATE_COL = 2816
N_GATES = 4 * ML_HEADS


def _dot(a, b):
    return jnp.dot(a, b, preferred_element_type=F32)


def _dot_tb(a, b):
    return lax.dot_general(a, b, (((1,), (1,)), ((), ())), preferred_element_type=F32)


def _split(a):
    hi = a.astype(BF16)
    lo = (a - hi.astype(F32)).astype(BF16)
    return hi, lo


def _layer_norm(x):
    mu = jnp.mean(x, axis=-1, keepdims=True)
    xc = x - mu
    var = jnp.mean(xc * xc, axis=-1, keepdims=True)
    return xc * lax.rsqrt(var + LN_EPS)


def _log_sigmoid(x):
    return jnp.minimum(x, 0.0) - jnp.log(1.0 + jnp.exp(-jnp.abs(x)))


def _sigmoid(x):
    return jax.nn.sigmoid(x)


def _mod_kernel(c_ref, w_ref, b_ref, o_ref):
    c = c_ref[...]
    s = c * _sigmoid(c)
    s_hi, s_lo = _split(s)
    w = w_ref[...]
    w_hi, w_lo = _split(w)
    o_ref[...] = _dot(s_hi, w_hi) + _dot(s_lo, w_hi) + _dot(s_hi, w_lo) + b_ref[...]


def _modulation(cond, w_mod, b_mod):
    rows = cond.shape[0]
    tn = 1024
    return pl.pallas_call(
        _mod_kernel,
        grid=(DEPTH, 3 * D_MODEL // tn),
        in_specs=[
            pl.BlockSpec((rows, D_MODEL), lambda l, j: (0, 0)),
            pl.BlockSpec((None, D_MODEL, tn), lambda l, j: (l, 0, j)),
            pl.BlockSpec((None, 1, tn), lambda l, j: (l, 0, j)),
        ],
        out_specs=pl.BlockSpec((None, rows, tn), lambda l, j: (l, 0, j)),
        out_shape=jax.ShapeDtypeStruct((DEPTH, rows, 3 * D_MODEL), F32),
        compiler_params=pltpu.CompilerParams(dimension_semantics=("arbitrary", "arbitrary")),
        name="adaln_modulation",
    )(cond, w_mod, b_mod.reshape(DEPTH, 1, 3 * D_MODEL))


PREP_K = 256


def _prep_kernel(wt_ref, o_ref):
    lane = lax.broadcasted_iota(jnp.int32, (PREP_K, LANES), 1)
    for c0 in range(0, W_COLS, LANES):
        r0 = c0 if c0 <= GATE_COL else c0 - (LANES - N_GATES)
        slab = jnp.transpose(wt_ref[r0:r0 + LANES, :])
        if c0 == GATE_COL:
            slab = jnp.where(lane < N_GATES, slab, 0.0)
        o_ref[:, c0:c0 + LANES] = slab.astype(BF16)


def _prep_in_weight(w_in_t):
    n_in = w_in_t.shape[1]
    return pl.pallas_call(
        _prep_kernel,
        grid=(DEPTH, D_MODEL // PREP_K),
        in_specs=[pl.BlockSpec((None, n_in, PREP_K), lambda l, i: (l, 0, i))],
        out_specs=pl.BlockSpec((None, PREP_K, W_COLS), lambda l, i: (l, i, 0)),
        out_shape=jax.ShapeDtypeStruct((DEPTH, D_MODEL, W_COLS), BF16),
        compiler_params=pltpu.CompilerParams(
            dimension_semantics=("arbitrary", "arbitrary"), vmem_limit_bytes=VMEM_LIMIT_BYTES),
        name="in_weight_prep",
    )(w_in_t)


def _rope(x, cos, sa, sb):
    return x * cos + pltpu.roll(x, LANES - AXIS_DIM // 2, 1) * sa + pltpu.roll(x, AXIS_DIM // 2, 1) * sb


def _group_rms(x, bd, gain):
    ms = _dot((x * x).astype(BF16), bd) * (1.0 / HEAD_DIM)
    return x * lax.rsqrt(ms + RMS_EPS) * gain


def _ones_row_tile():
    r = lax.broadcasted_iota(jnp.int32, (BF16_ROWS, LANES), 0)
    return jnp.where(r == 0, 1.0, 0.0).astype(BF16)


def _attend_chains(chains, results):
    state = []
    for qs, tiles, sink_row in chains:
        m_cols = qs.shape[0]
        if sink_row is None:
            m = jnp.full((1, m_cols), -jnp.inf, F32)
            acc = jnp.zeros((VT_ROWS, m_cols), F32)
        else:
            m = sink_row
            r = lax.broadcasted_iota(jnp.int32, (VT_ROWS, m_cols), 0)
            acc = jnp.where(r == HEAD_DIM, 1.0, 0.0)
        cur = tiles[0]()
        state.append([m, acc, cur, _dot_tb(cur[0], qs)])
    yield
    for t in range(max(len(tiles) for _, tiles, _ in chains)):
        for st, (qs, tiles, _) in zip(state, chains):
            if t >= len(tiles):
                continue
            m, acc, (_, v_t, valid), s = st
            if valid is not None:
                s = jnp.where(valid, s, -jnp.inf)
            if t + 1 < len(tiles):
                nxt = tiles[t + 1]()
                st[2], st[3] = nxt, _dot_tb(nxt[0], qs)
            m_new = jnp.maximum(m, jnp.max(s, axis=0, keepdims=True))
            p = jnp.exp2(s - m_new).astype(BF16)
            st[1] = acc * jnp.exp2(m - m_new) + _dot(v_t, p)
            st[0] = m_new
        yield
    results.extend(st[1][0:HEAD_DIM, :] / st[1][HEAD_DIM:HEAD_DIM + 1, :] for st in state)


def _interleave(weighted):
    live = [[g, n] for g, n in weighted]
    while live:
        for item in list(live):
            for _ in range(item[1]):
                try:
                    next(item[0])
                except StopIteration:
                    live.remove(item)
                    break
        yield


def _run(gen):
    for _ in gen:
        pass


def _layer_kernel(lat, T, layer, *refs):
    _run(_sequence(lat, T, layer, *refs))


def _sequence(lat, T, layer, *refs):
    refs = list(refs)
    x_ref, mod_ref, w_ref, wb_ref, wo_ref = refs[:5]
    g0_ref, g1_ref, gb_ref, lng_ref, lnb_ref = refs[5:10]
    bd_ref, tril_ref, triu_ref, eye_ref, sink_ref = refs[10:N_SHARED]
    pos = N_SHARED
    if lat:
        cos_ref, sa_ref, sb_ref = refs[pos:pos + 3]
        cka_ref, cva_ref, ckw_ref, cvw_ref, c0_ref, n0_ref, m0_ref = refs[pos + 3:pos + 10]
        pos += 10
        y_ref = refs[pos]
        pos += 1
    else:
        if layer > 0:
            pos += CTX_ALIASED
        x_next_ref = refs[pos]
        pos += 1
        y_ref, ka_o, va_o, kw_o, vw_o, c_o, n_o, m_o = refs[pos:pos + 8]
        pos += 8
        if layer == 0:
            for full in (ka_o, va_o, kw_o, vw_o, c_o):
                full[1:] = jnp.zeros((DEPTH - 1,) + full.shape[1:], F32)
            ka_o, va_o, kw_o, vw_o, c_o = (full.at[0] for full in (ka_o, va_o, kw_o, vw_o, c_o))
    (u_s, qa_s, qw_s, ka_s, vat_s, kw_s, vwt_s, qm_s, km_s, vmt_s, om_s, g_s,
     ya_s, yw_s, hmt_s, c_s, m_s) = refs[pos:pos + 17]
    pos += 17
    if not lat:
        step = pl.program_id(0)
        u_next_s = u_s.at[(step + 1) % 2]
        u_s = u_s.at[step % 2]
    if lat:
        kwc_s, vwct_s = refs[pos:pos + 2]

    row_block = min(T, ROW_BLOCK)
    n_rows = T // row_block
    n_chunks = T // CHUNK
    chunks_per_block = row_block // CHUNK
    shift = mod_ref[0:1, :]
    scale = mod_ref[1:2, :]
    gate = mod_ref[2:3, :]
    ones_tile = _ones_row_tile()

    norm_rows = 64

    def row_slice(i, j, n):
        r0 = i * row_block + j
        return pl.ds(r0 if isinstance(r0, int) else pl.multiple_of(r0, n), n)

    def phase1_norm(i, src=x_ref, dst=u_s):
        for j in range(0, row_block, norm_rows):
            piece = row_slice(i, j, norm_rows)
            dst[piece, :] = (_layer_norm(src[piece, :]) * (1.0 + scale) + shift).astype(BF16)
            yield

    def phase1_proj(i):
        R = row_block
        rows = row_slice(i, 0, R)
        u = u_s[rows, :]

        def proj(c0, width):
            return _dot(u, w_ref[:, c0:c0 + width])

        if lat:
            cos = cos_ref[rows, :]
            sa = sa_ref[rows, :]
            sb = sb_ref[rows, :]

        def rope_wide(v):
            if not lat:
                return v
            slabs = [_rope(v[:, j:j + LANES], cos, sa, sb) for j in range(0, v.shape[1], LANES)]
            return slabs[0] if len(slabs) == 1 else jnp.concatenate(slabs, axis=1)

        qscale = LOG2E * HEAD_DIM ** -0.5
        qa = rope_wide(_group_rms(proj(C_QA, 512), bd_ref[...], g0_ref[...])) * qscale
        for h in range(N_HEADS):
            qa_s[h, rows, :] = qa[:, h * HEAD_DIM:(h + 1) * HEAD_DIM].astype(BF16)
        yield
        ka_n = _group_rms(proj(C_KA, LANES), bd_ref[0:LANES, 0:LANES], g1_ref[...])
        ka = rope_wide(ka_n)
        va = proj(C_VA, LANES)
        qw = rope_wide(proj(C_QW, 512)) * qscale
        for h in range(N_HEADS):
            qw_s[h, rows, :] = qw[:, h * HEAD_DIM:(h + 1) * HEAD_DIM].astype(BF16)
        yield
        kw_raw = proj(C_KW, LANES)
        kw = rope_wide(kw_raw)
        vw = proj(C_VW, LANES)
        for kv in range(N_KV):
            sl = slice(kv * HEAD_DIM, (kv + 1) * HEAD_DIM)
            ka_s[kv, rows, :] = ka[:, sl].astype(BF16)
            kw_s[kv, rows, :] = kw[:, sl].astype(BF16)
        yield
        qm_s[rows, :] = proj(C_QM, 256).astype(BF16)
        km_s[rows, :] = (proj(C_KM, 256) * (ML_DK ** -0.5)).astype(BF16)
        yield
        vm = proj(C_VM, 512)
        yield
        om_s[rows, :] = proj(C_OM, 512)
        g_s[rows, :] = proj(C_GT, LANES) + gb_ref[...]
        yield
        for j in range(chunks_per_block):
            cj = i * chunks_per_block + j
            cr = slice(j * CHUNK, (j + 1) * CHUNK)
            va_t = jnp.transpose(va[cr, :])
            vw_t = jnp.transpose(vw[cr, :])
            for kv in range(N_KV):
                hd = slice(kv * HEAD_DIM, (kv + 1) * HEAD_DIM)
                vat_s[cj, kv, 0:HEAD_DIM, :] = va_t[hd, :].astype(BF16)
                vat_s[cj, kv, HEAD_DIM:VT_ROWS, :] = ones_tile
                vwt_s[cj, kv, 0:HEAD_DIM, :] = vw_t[hd, :].astype(BF16)
                vwt_s[cj, kv, HEAD_DIM:VT_ROWS, :] = ones_tile
            if not lat:
                tc = slice(i * row_block + j * CHUNK, i * row_block + (j + 1) * CHUNK)
                ka_t = jnp.transpose(ka_n[cr, :])
                kw_t = jnp.transpose(kw_raw[cr, :])
                for kv in range(N_KV):
                    hd = slice(kv * HEAD_DIM, (kv + 1) * HEAD_DIM)
                    ka_o[kv, :, tc] = ka_t[hd, :]
                    va_o[kv, :, tc] = va_t[hd, :]
                    kw_o[kv, :, tc] = kw_t[hd, :]
                    vw_o[kv, :, tc] = vw_t[hd, :]
            for h in range(ML_HEADS):
                vmt_s[cj, h, 0:ML_DV, :] = jnp.transpose(vm[cr, h * ML_DV:(h + 1) * ML_DV]).astype(BF16)
                vmt_s[cj, h, ML_DV:ST_ROWS, :] = ones_tile
            yield

    if not lat:
        @pl.when(step == 0)
        def _():
            _run(phase1_norm(0))
        yield from phase1_proj(0)
    else:
        def body1(i, carry):
            _run(phase1_norm(i))
            _run(phase1_proj(i))
            return carry
        lax.fori_loop(0, n_rows, body1, 0)

    if lat:
        eye = eye_ref[...]
        for kv in range(N_KV):
            ka_s[kv, T:T + PAST_LEN, :] = cka_ref[kv].astype(BF16)
            kwc_s[kv] = ckw_ref[kv].astype(BF16)
            cva_t = _dot_tb(eye, cva_ref[kv].astype(BF16)).astype(BF16)
            cvw_t = _dot_tb(eye, cvw_ref[kv].astype(BF16)).astype(BF16)
            for j in range(PAST_LEN // CHUNK):
                vat_s[n_chunks + j, kv, 0:HEAD_DIM, :] = cva_t[:, j * CHUNK:(j + 1) * CHUNK]
                vat_s[n_chunks + j, kv, HEAD_DIM:VT_ROWS, :] = ones_tile
                vwct_s[kv, 0:HEAD_DIM, j * CHUNK:(j + 1) * CHUNK] = cvw_t[:, j * CHUNK:(j + 1) * CHUNK]
                vwct_s[kv, HEAD_DIM:VT_ROWS, j * CHUNK:(j + 1) * CHUNK] = ones_tile

    tq = CHUNK
    m_cols = N_GROUP * tq

    def sink_row(kv):
        col = lax.broadcasted_iota(jnp.int32, (1, m_cols), 1)
        row = jnp.full((1, m_cols), sink_ref[layer * N_HEADS + kv * N_GROUP], F32)
        for g in range(1, N_GROUP):
            row = jnp.where(col >= g * tq, sink_ref[layer * N_HEADS + kv * N_GROUP + g], row)
        return row * LOG2E

    def load_q(q_s, kv, q0):
        return q_s[kv * N_GROUP:(kv + 1) * N_GROUP, pl.ds(q0, tq), :].reshape(m_cols, HEAD_DIM)

    def store_heads(dst, q0, kv, o_t):
        for p in range(N_GROUP // 2):
            blk = jnp.concatenate([o_t[:, (2 * p) * tq:(2 * p + 1) * tq],
                                   o_t[:, (2 * p + 1) * tq:(2 * p + 2) * tq]], axis=0)
            c0 = (kv * N_GROUP + 2 * p) * HEAD_DIM
            dst[pl.ds(q0, tq), c0:c0 + LANES] = jnp.transpose(blk).astype(dst.dtype)

    def key_tiles(k_ref, vt_ref, kv, n_key_chunks, chunks_per_tile):
        def tile(c0):
            def load():
                k = k_ref[kv, c0 * CHUNK:(c0 + chunks_per_tile) * CHUNK, :]
                v_t = [vt_ref[c0 + j, kv] for j in range(chunks_per_tile)]
                return k, (v_t[0] if len(v_t) == 1 else jnp.concatenate(v_t, axis=1)), None
            return load
        return [tile(c0) for c0 in range(0, n_key_chunks, chunks_per_tile)]

    def band_tiles(kv, n, q0):
        j0 = jnp.clip(n - 1, 0, n_chunks - 3)
        w0 = pl.multiple_of(j0 * CHUNK, CHUNK)

        def band(c0, nc):
            def load():
                rows = nc * CHUNK
                kpos = w0 + c0 * CHUNK + lax.broadcasted_iota(jnp.int32, (rows, m_cols), 0)
                qpos = q0 + (lax.broadcasted_iota(jnp.int32, (rows, m_cols), 1) & (tq - 1))
                v_t = [vwt_s[j0 + c0 + j, kv] for j in range(nc)]
                return (kw_s[kv, pl.ds(w0 + c0 * CHUNK, rows), :],
                        v_t[0] if nc == 1 else jnp.concatenate(v_t, axis=1),
                        jnp.abs(kpos - qpos) <= WINDOW)
            return load
        return [band(0, 2), band(2, 1), lambda: (kwc_s[kv], vwct_s[kv], None)]

    def attend(n, q0, chunks_per_tile):
        chains, dsts = [], []
        n_key_chunks = ka_s.shape[1] // CHUNK
        for kv in range(N_KV):
            chains.append((load_q(qa_s, kv, q0), key_tiles(ka_s, vat_s, kv, n_key_chunks, chunks_per_tile), None))
            dsts.append((ya_s, kv))
        for kv in range(N_KV):
            tiles = band_tiles(kv, n, q0) if lat else key_tiles(kw_s, vwt_s, kv, n_chunks, chunks_per_tile)
            chains.append((load_q(qw_s, kv, q0), tiles, sink_row(kv)))
            dsts.append((yw_s, kv))
        for g0 in range(0, len(chains), CHAINS_IN_FLIGHT):
            results = []
            yield from _attend_chains(chains[g0:g0 + CHAINS_IN_FLIGHT], results)
            for (dst, kv), o_t in zip(dsts[g0:g0 + CHAINS_IN_FLIGHT], results):
                store_heads(dst, q0, kv, o_t)

    n_pairs = ML_HEADS // 2
    hmt_s[...] = jnp.zeros_like(hmt_s)
    for d in range(2):
        for h in range(ML_HEADS):
            idx = d * n_pairs + h // 2
            hl = slice((h % 2) * ML_DK, (h % 2 + 1) * ML_DK)
            c_s[idx, ML_DV:ST_ROWS, hl] = jnp.zeros((BF16_ROWS, ML_DK), F32)
            if lat:
                c_s[idx, 0:ML_DV, hl] = c0_ref[d, h]
                c_s[idx, ML_DV:ML_DV + 1, hl] = n0_ref[d, h:h + 1, :]
                m_s[d * ML_HEADS + h:d * ML_HEADS + h + 1, :] = jnp.broadcast_to(
                    m0_ref[d:d + 1, h:h + 1], (1, LANES))
            else:
                c_s[idx, 0:ML_DV, hl] = jnp.zeros((ML_DV, ML_DK), F32)
                m_s[d * ML_HEADS + h:d * ML_HEADS + h + 1, :] = jnp.zeros((1, LANES), F32)

    L = CHUNK
    s_idx = lax.broadcasted_iota(jnp.int32, (L, ML_HEADS * L), 0)
    t_idx = lax.broadcasted_iota(jnp.int32, (L, ML_HEADS * L), 1) & (L - 1)
    lane_row = lax.broadcasted_iota(jnp.int32, (1, LANES), 1)
    low_half = lax.broadcasted_iota(jnp.int32, (L, LANES), 1) < ML_DK
    ones_ll = jnp.ones((L, L), BF16)
    zeros_ll = jnp.zeros((L, L), BF16)

    def heads_row(src, r0, c0=0):
        return jnp.concatenate([src[r0 + h:r0 + h + 1, c0:c0 + L] for h in range(ML_HEADS)], axis=1)

    def block_diag_rows(x):
        zero = jnp.zeros_like(x)
        return jnp.concatenate([jnp.where(low_half, x, zero), jnp.where(low_half, zero, x)], axis=0)

    def mlstm_streams(streams):
        m_state = {d: heads_row(m_s, d * ML_HEADS) for d in sorted({d for d, _ in streams})}
        c_state = {d: [c_s[d * n_pairs + pr] for pr in range(n_pairs)] for d in m_state}
        work = []
        for d, cc in streams:
            r0 = cc * L if isinstance(cc, int) else pl.multiple_of(cc * L, L)
            rows = pl.ds(r0, L)
            g = g_s[rows, :]
            f_hi, f_lo = _split(_log_sigmoid(g))
            tri_c = tril_ref[...] if d == 0 else triu_ref[...]
            tri_r = triu_ref[...] if d == 0 else tril_ref[...]
            cum = _dot(tri_c, f_hi) + _dot(tri_c, f_lo)
            r = g - pltpu.roll(cum, LANES - ML_HEADS, 1)
            gi0 = 2 * ML_HEADS * d
            rb = jnp.concatenate([jnp.broadcast_to(r[:, gi0 + h:gi0 + h + 1], (L, L))
                                  for h in range(ML_HEADS)], axis=1)
            g_t = jnp.transpose(g)[0:N_GATES, :]
            ft_hi, ft_lo = _split(_log_sigmoid(g_t))
            tr = jnp.concatenate([tri_r, ones_ll], axis=1)
            ct = _dot(ft_hi, tr) + _dot(ft_lo, tr)
            work.append(dict(d=d, cc=cc, rows=rows, rb=rb, g_t=g_t, ct=ct))
            yield
        for w in work:
            d = w["d"]
            valid = (s_idx <= t_idx) if d == 0 else (s_idx >= t_idx)
            gi0 = 2 * ML_HEADS * d
            gf0 = gi0 + ML_HEADS
            b_row = heads_row(w["ct"], gf0)
            b_last = heads_row(w["ct"], gf0, L)
            i_row = heads_row(w["g_t"], gi0)
            m_prev = m_state[d]
            a_row = b_row + m_prev
            dm = jnp.where(valid, b_row + w["rb"], -jnp.inf)
            mt = jnp.maximum(a_row, jnp.max(dm, axis=0, keepdims=True))
            w["p"] = jnp.exp(dm - mt)
            w["w_inter"] = jnp.exp(a_row - mt)
            w["floor"] = jnp.exp(-mt)
            g_row = b_last - b_row + i_row
            g_max = jnp.concatenate(
                [jnp.broadcast_to(jnp.max(g_row[:, h * L:(h + 1) * L], axis=-1, keepdims=True), (1, L))
                 for h in range(ML_HEADS)], axis=1)
            m_new = jnp.maximum(b_last + m_prev, g_max)
            w["ws"] = jnp.exp(g_row - m_new)
            w["wc"] = jnp.exp(b_last + m_prev - m_new)
            m_state[d] = m_new
            yield
        for w in work:
            rows, cc = w["rows"], w["cc"]
            w["pairs"] = []
            for pr in range(n_pairs):
                lanes = slice(pr * LANES, (pr + 1) * LANES)
                cols = slice(pr * 2 * L, (pr + 1) * 2 * L)
                k_pair = km_s[rows, lanes]
                q_bd = block_diag_rows(qm_s[rows, lanes])
                k_bd = block_diag_rows(k_pair)
                s_t = (_dot_tb(k_pair, q_bd) * w["p"][:, cols]).astype(BF16)
                s_bd = jnp.concatenate(
                    [jnp.concatenate([s_t[:, 0:L], zeros_ll], axis=1),
                     jnp.concatenate([zeros_ll, s_t[:, L:2 * L]], axis=1)], axis=0)
                v_t = jnp.concatenate([vmt_s[cc, 2 * pr], vmt_s[cc, 2 * pr + 1]], axis=1)
                intra = _dot(v_t, s_bd)
                wv = (v_t.astype(F32) * w["ws"][:, cols]).astype(BF16)
                w["pairs"].append((q_bd, intra, _dot(wv, k_bd)))
            yield
        for w in work:
            d, cc = w["d"], w["cc"]
            for pr, (q_bd, intra, update) in enumerate(w["pairs"]):
                cols = slice(pr * 2 * L, (pr + 1) * 2 * L)
                state = c_state[d][pr]
                tot = intra + w["w_inter"][:, cols] * _dot_tb(state.astype(BF16), q_bd)
                h_t = tot[0:ML_DV, :] / jnp.maximum(jnp.abs(tot[ML_DV:ML_DV + 1, :]), w["floor"][:, cols])
                for e in range(2):
                    hr = slice((2 * pr + e) * ML_DV, (2 * pr + e + 1) * ML_DV)
                    hmt_s[cc, hr, :] += h_t[:, e * L:(e + 1) * L]
                wc = w["wc"]
                wc_pair = jnp.where(lane_row < ML_DK, wc[:, 2 * pr * L:(2 * pr + 1) * L],
                                    wc[:, (2 * pr + 1) * L:(2 * pr + 2) * L])
                c_state[d][pr] = wc_pair * state + update
            yield
        for d in m_state:
            for h in range(ML_HEADS):
                m_s[d * ML_HEADS + h:d * ML_HEADS + h + 1, :] = m_state[d][:, h * L:(h + 1) * L]
            for pr in range(n_pairs):
                c_s[d * n_pairs + pr] = c_state[d][pr]

    def scan_streams(cs):
        return [(d, c if d == 0 else n_chunks - 1 - c) for c in cs for d in range(2)]

    if lat:
        def body2(n, carry):
            _run(_interleave([(attend(n, pl.multiple_of(n * tq, tq), 2), 1),
                              (mlstm_streams(scan_streams([n])), 1)]))
            return carry
        lax.fori_loop(0, n_chunks, body2, 0)
    else:
        def all_tiles():
            for n in range(T // tq):
                yield from attend(n, n * tq, 2)
        yield from _interleave([(all_tiles(), 1), (mlstm_streams(scan_streams(range(n_chunks))), 3)])

    if not lat:
        for d in range(2):
            for pr in range(n_pairs):
                c_t = jnp.transpose(c_s[d * n_pairs + pr, 0:ML_DV, :])
                for e in range(2):
                    c_o[d, 2 * pr + e] = c_t[e * ML_DK:(e + 1) * ML_DK, :]
            for h in range(ML_HEADS):
                idx = d * n_pairs + h // 2
                hl = slice((h % 2) * ML_DK, (h % 2 + 1) * ML_DK)
                n_o[d, h:h + 1, :] = c_s[idx, ML_DV:ML_DV + 1, hl]
                m_o[d:d + 1, h:h + 1] = m_s[d * ML_HEADS + h:d * ML_HEADS + h + 1, 0:1]

    def phase3_merge(i, out):
        rows = row_slice(i, 0, row_block)
        u = u_s[rows, :]

        def proj(c0, width):
            return _dot(u, w_ref[:, c0:c0 + width])

        hm = jnp.concatenate(
            [jnp.concatenate([jnp.transpose(hmt_s[i * chunks_per_block + j, h * ML_DV:(h + 1) * ML_DV, :])
                              for h in range(ML_HEADS)], axis=1)
             for j in range(chunks_per_block)], axis=0)
        ys = (ya_s[rows, :].astype(F32), _sigmoid(om_s[rows, :]) * hm, yw_s[rows, :].astype(F32))
        yield
        merged = None
        for b, (yb, zc) in enumerate(zip(ys, (C_ZA, C_ZM, C_ZW))):
            z = proj(zc, BRANCH_W)
            t = (yb * (z * _sigmoid(z))).astype(BF16)
            yield
            pb = _sigmoid(proj(C_GMERGE + b * D_MODEL, D_MODEL)) * _dot(t, wb_ref[b])
            merged = pb if merged is None else merged + pb
            yield
        out.append(_dot(merged.astype(BF16), wo_ref[...]))
        yield

    def phase3_norm(i, o):
        for j in range(0, row_block, norm_rows):
            piece = row_slice(i, j, norm_rows)
            hres = ALPHA * x_ref[piece, :] + gate * o[j:j + norm_rows, :]
            y_ref[piece, :] = _layer_norm(hres) * lng_ref[...] + lnb_ref[...]
            yield

    def phase3(i):
        box = []
        yield from phase3_merge(i, box)
        yield from phase3_norm(i, box[0])

    if not lat:
        box = []
        yield from _interleave([(phase3_merge(0, box), 1), (phase1_norm(0, x_next_ref, u_next_s), 1)])
        yield from phase3_norm(0, box[0])
    else:
        def body4(i, carry):
            _run(phase3(i))
            return carry
        lax.fori_loop(0, n_rows, body4, 0)


def _const_spec(shape):
    nd = len(shape)
    return pl.BlockSpec(shape, lambda b: (0,) * nd, pipeline_mode=pl.Buffered(1))


def _layer_spec(shape, layer):
    nd = len(shape) - 1
    return pl.BlockSpec((None,) + tuple(shape[1:]), lambda b: (layer,) + (0,) * nd,
                        pipeline_mode=pl.Buffered(1))


def _layer_call(lat, layer, x, mod, weights, consts, extra):
    B, T, _ = x.shape
    S = T + PAST_LEN if lat else T
    n_chunks = T // CHUNK

    if lat:
        seq_spec = pl.BlockSpec((None, T, D_MODEL), lambda b: (b, 0, 0), pipeline_mode=pl.Buffered(1))
    else:
        seq_spec = pl.BlockSpec((None, T, D_MODEL), lambda b: (b, 0, 0))
    if lat:
        mod_spec = pl.BlockSpec((None, None, 3, D_MODEL), lambda b: (layer, b + 1, 0, 0))
    else:
        mod_spec = pl.BlockSpec((None, None, 3, D_MODEL), lambda b: (layer, 0, 0, 0))
    *vmem_consts, sink = consts
    in_specs = [seq_spec, mod_spec]
    in_specs += [_layer_spec(a.shape, layer) for a in weights]
    in_specs += [_const_spec(a.shape) for a in vmem_consts]
    in_specs.append(pl.BlockSpec(memory_space=pltpu.SMEM))
    args = [x, mod, *weights, *vmem_consts, sink]

    if lat:
        cos, sa, sb, cka, cva, ckw, cvw, sc, sn, sm = extra
        in_specs += [_const_spec(cos.shape)] * 3
        kv_spec = pl.BlockSpec((None, None, N_KV, PAST_LEN, HEAD_DIM), lambda b: (b, layer, 0, 0, 0))
        in_specs += [kv_spec] * 4
        in_specs += [
            pl.BlockSpec((None, None, 2, ML_HEADS, ML_DV, ML_DK), lambda b: (b, layer, 0, 0, 0, 0)),
            pl.BlockSpec((None, None, 2, ML_HEADS, ML_DK), lambda b: (b, layer, 0, 0, 0)),
            pl.BlockSpec((None, None, 2, ML_HEADS), lambda b: (b, layer, 0, 0)),
        ]
        args += [cos, sa, sb, cka, cva, ckw, cvw, sc, sn, sm]
        out_specs = seq_spec
        out_shape = jax.ShapeDtypeStruct((B, T, D_MODEL), F32)
        aliases = {}
    else:
        kv_shape = (B, DEPTH, N_KV, HEAD_DIM, T)
        c_shape = (B, DEPTH, 2, ML_HEADS, ML_DK, ML_DV)
        if layer == 0:
            aliases = {}
            ld = DEPTH
            layer_idx = 0
        else:
            first_acc = len(args)
            in_specs += [pl.BlockSpec(memory_space=pl.ANY)] * len(extra)
            args += list(extra)
            aliases = {first_acc + i: 1 + i for i in range(len(extra))}
            assert len(extra) == CTX_ALIASED
            ld = None
            layer_idx = layer
        in_specs.append(pl.BlockSpec((None, T, D_MODEL), lambda b: (jnp.minimum(b + 1, B - 1), 0, 0)))
        args.append(x)
        kv_out = pl.BlockSpec((None, ld, N_KV, HEAD_DIM, T), lambda b: (b, layer_idx, 0, 0, 0))
        out_specs = [
            seq_spec, kv_out, kv_out, kv_out, kv_out,
            pl.BlockSpec((None, ld, 2, ML_HEADS, ML_DK, ML_DV), lambda b: (b, layer_idx, 0, 0, 0, 0)),
            pl.BlockSpec((None, 2, ML_HEADS, ML_DK), lambda b: (b, 0, 0, 0)),
            pl.BlockSpec((None, 2, ML_HEADS), lambda b: (b, 0, 0)),
        ]
        out_shape = [jax.ShapeDtypeStruct((B, T, D_MODEL), F32)]
        out_shape += [jax.ShapeDtypeStruct(kv_shape, F32)] * 4 + [jax.ShapeDtypeStruct(c_shape, F32)]
        out_shape += [
            jax.ShapeDtypeStruct((B, 2, ML_HEADS, ML_DK), F32),
            jax.ShapeDtypeStruct((B, 2, ML_HEADS), F32),
        ]

    scratch = [
        ((T, D_MODEL) if lat else (2, T, D_MODEL), BF16),
        ((N_HEADS, T, HEAD_DIM), BF16),
        ((N_HEADS, T, HEAD_DIM), BF16),
        ((N_KV, S, HEAD_DIM), BF16),
        ((S // CHUNK, N_KV, VT_ROWS, CHUNK), BF16),
        ((N_KV, T, HEAD_DIM), BF16),
        ((n_chunks, N_KV, VT_ROWS, CHUNK), BF16),
        ((T, ML_HEADS * ML_DK), BF16),
        ((T, ML_HEADS * ML_DK), BF16),
        ((n_chunks, ML_HEADS, ST_ROWS, CHUNK), BF16),
        ((T, ML_HEADS * ML_DV), F32),
        ((T, LANES), F32),
        ((T, BRANCH_W), BF16),
        ((T, BRANCH_W), BF16),
        ((n_chunks, ML_HEADS * ML_DV, CHUNK), F32),
        ((ML_HEADS, ST_ROWS, 2 * ML_DK), F32),
        ((2 * ML_HEADS, LANES), F32),
    ]
    if lat:
        scratch += [((N_KV, PAST_LEN, HEAD_DIM), BF16),
                    ((N_KV, VT_ROWS, PAST_LEN), BF16)]
    scratch = [pltpu.VMEM(shape, dtype) for shape, dtype in scratch]

    return pl.pallas_call(
        functools.partial(_layer_kernel, lat, T, layer),
        grid=(B,),
        in_specs=in_specs,
        out_specs=out_specs,
        out_shape=out_shape,
        scratch_shapes=scratch,
        input_output_aliases=aliases,
        compiler_params=pltpu.CompilerParams(
            dimension_semantics=("arbitrary",), vmem_limit_bytes=VMEM_LIMIT_BYTES),
        name=("latent_layer" if lat else "context_layer") + str(layer),
    )(*args)


def _rope_tables(T):
    rows = T // GRID_W
    row = jnp.repeat(jnp.arange(rows, dtype=F32), GRID_W)
    col = jnp.tile(jnp.arange(GRID_W, dtype=F32), rows)
    inv = ROPE_THETA ** (-jnp.arange(0, AXIS_DIM, 2, dtype=F32) / AXIS_DIM)
    ar = row[:, None] * inv
    ac = col[:, None] * inv
    ang = jnp.concatenate([ar, ar, ac, ac], axis=-1)
    ang = jnp.concatenate([ang, ang], axis=-1)
    first_half = (jnp.arange(LANES) % AXIS_DIM) < (AXIS_DIM // 2)
    cos = jnp.cos(ang)
    sin = jnp.sin(ang)
    sa = jnp.where(first_half[None, :], -sin, 0.0)
    sb = jnp.where(first_half[None, :], 0.0, sin)
    return cos, sa, sb


def _constants(sink_logit):
    grp = np.arange(N_HEADS * HEAD_DIM) // HEAD_DIM
    bd = jnp.asarray(grp[:, None] == grp[None, :], dtype=BF16)
    ti = np.arange(CHUNK)
    tril = jnp.asarray(ti[None, :] <= ti[:, None], dtype=BF16)
    triu = jnp.asarray(ti[None, :] >= ti[:, None], dtype=BF16)
    eye = jnp.asarray(np.eye(HEAD_DIM), dtype=BF16)
    return bd, tril, triu, eye, sink_logit.reshape(-1)


def kernel(x_prompt, x_sample, cache_attn_k, cache_attn_v, cache_win_k, cache_win_v, state_mlstm_C,
           state_mlstm_n, state_mlstm_m, c, c_ctx, w_mod, b_mod, w_in, qk_gain, sink_logit,
           mlstm_gate_bias, w_branch, w_out, ln_gain, ln_bias):
    dec_b = x_sample.shape[0]
    assert dec_b + 1 <= 8
    cond = jnp.concatenate([c_ctx[None, :], c, jnp.zeros((8 - 1 - dec_b, D_MODEL), F32)], axis=0)
    mod = _modulation(cond, w_mod, b_mod).reshape(DEPTH, 8, 3, D_MODEL)

    consts = _constants(sink_logit)
    cos, sa, sb = _rope_tables(x_sample.shape[1])
    lane_pad = jnp.zeros((DEPTH, 1, LANES - N_GATES), F32)
    weights = (
        _prep_in_weight(jnp.swapaxes(w_in, 1, 2)), w_branch.astype(BF16), w_out.astype(BF16),
        jnp.tile(qk_gain[:, 0:1, :], (1, 1, N_HEADS)),
        jnp.tile(qk_gain[:, 1:2, :], (1, 1, N_KV)),
        jnp.concatenate([mlstm_gate_bias.reshape(DEPTH, 1, N_GATES), lane_pad], axis=2),
        ln_gain[:, None, :], ln_bias[:, None, :],
    )

    ctx = None
    xp, xs = x_prompt, x_sample
    small = []
    for l in range(DEPTH):
        outs = _layer_call(False, l, xp, mod, weights, consts, ctx)
        xp = outs[0]
        ctx = tuple(outs[1:6])
        small.append(outs[6:])
        extra = (cos, sa, sb, cache_attn_k, cache_attn_v, cache_win_k, cache_win_v,
                 state_mlstm_C, state_mlstm_n, state_mlstm_m)
        xs = _layer_call(True, l, xs, mod, weights, consts, extra)
    new_n, new_m = [jnp.stack([small[l][i] for l in range(DEPTH)], axis=1) for i in range(2)]
    return (xp, xs, *[jnp.swapaxes(a, -1, -2) for a in ctx], new_n, new_m)
```

```python
import functools

import jax
import jax.numpy as jnp
import numpy as np
from jax import lax
from jax.experimental import pallas as pl
from jax.experimental.pallas import tpu as pltpu

F32 = jnp.float32
BF16 = jnp.bfloat16

D_MODEL = 1024
DEPTH = 2
PAST_LEN = 256
GRID_W = 64
HEAD_DIM = 64
N_HEADS = 8
N_KV = 2
N_GROUP = N_HEADS // N_KV
WINDOW = 128
ML_HEADS = 4
ML_DK = 64
ML_DV = 128
CHUNK = 128
BRANCH_W = 512
ROPE_THETA = 10000.0
AXIS_DIM = HEAD_DIM // 2
LN_EPS = 1e-6
RMS_EPS = 1e-6
ALPHA = (2.0 * DEPTH) ** 0.25
LOG2E = float(np.log2(np.e))
LANES = 128
BF16_ROWS = 16
ROW_BLOCK = 256
CHAINS_IN_FLIGHT = 4
CTX_ALIASED = 5
N_SHARED = 15
VMEM_LIMIT_BYTES = 62 * 1024 * 1024
VT_ROWS = HEAD_DIM + BF16_ROWS
ST_ROWS = ML_DV + BF16_ROWS

_SIZES = (512, 128, 128, 512, 256, 256, 512, 512, 128, 512, 512, 128, 128, 512, 3072)
_OFF = [int(v) for v in np.concatenate([[0], np.cumsum(_SIZES)])]
(C_QA, C_KA, C_VA, C_ZA, C_QM, C_KM, C_VM, C_OM, C_GT, C_ZM, C_QW, C_KW, C_VW, C_ZW, C_GMERGE) = _OFF[:-1]
W_COLS = _OFF[-1]
GATE_COL = 2816
N_GATES = 4 * ML_HEADS


def _dot(a, b):
    return jnp.dot(a, b, preferred_element_type=F32)


def _dot_tb(a, b):
    return lax.dot_general(a, b, (((1,), (1,)), ((), ())), preferred_element_type=F32)


def _split(a):
    hi = a.astype(BF16)
    lo = (a - hi.astype(F32)).astype(BF16)
    return hi, lo


def _layer_norm(x):
    mu = jnp.mean(x, axis=-1, keepdims=True)
    xc = x - mu
    var = jnp.mean(xc * xc, axis=-1, keepdims=True)
    return xc * lax.rsqrt(var + LN_EPS)


def _log_sigmoid(x):
    return jnp.minimum(x, 0.0) - jnp.log(1.0 + jnp.exp(-jnp.abs(x)))


def _sigmoid(x):
    return jax.nn.sigmoid(x)


def _mod_kernel(c_ref, w_ref, b_ref, o_ref):
    c = c_ref[...]
    s = c * _sigmoid(c)
    s_hi, s_lo = _split(s)
    w = w_ref[...]
    w_hi, w_lo = _split(w)
    o_ref[...] = _dot(s_hi, w_hi) + _dot(s_lo, w_hi) + _dot(s_hi, w_lo) + b_ref[...]


def _modulation(cond, w_mod, b_mod):
    rows = cond.shape[0]
    tn = 1024
    return pl.pallas_call(
        _mod_kernel,
        grid=(DEPTH, 3 * D_MODEL // tn),
        in_specs=[
            pl.BlockSpec((rows, D_MODEL), lambda l, j: (0, 0)),
            pl.BlockSpec((None, D_MODEL, tn), lambda l, j: (l, 0, j)),
            pl.BlockSpec((None, 1, tn), lambda l, j: (l, 0, j)),
        ],
        out_specs=pl.BlockSpec((None, rows, tn), lambda l, j: (l, 0, j)),
        out_shape=jax.ShapeDtypeStruct((DEPTH, rows, 3 * D_MODEL), F32),
        compiler_params=pltpu.CompilerParams(dimension_semantics=("arbitrary", "arbitrary")),
        name="adaln_modulation",
    )(cond, w_mod, b_mod.reshape(DEPTH, 1, 3 * D_MODEL))


PREP_K = 256


def _prep_kernel(wt_ref, o_ref):
    lane = lax.broadcasted_iota(jnp.int32, (PREP_K, LANES), 1)
    for c0 in range(0, W_COLS, LANES):
        r0 = c0 if c0 <= GATE_COL else c0 - (LANES - N_GATES)
        slab = jnp.transpose(wt_ref[r0:r0 + LANES, :])
        if c0 == GATE_COL:
            slab = jnp.where(lane < N_GATES, slab, 0.0)
        o_ref[:, c0:c0 + LANES] = slab.astype(BF16)


def _prep_in_weight(w_in_t):
    n_in = w_in_t.shape[1]
    return pl.pallas_call(
        _prep_kernel,
        grid=(DEPTH, D_MODEL // PREP_K),
        in_specs=[pl.BlockSpec((None, n_in, PREP_K), lambda l, i: (l, 0, i))],
        out_specs=pl.BlockSpec((None, PREP_K, W_COLS), lambda l, i: (l, i, 0)),
        out_shape=jax.ShapeDtypeStruct((DEPTH, D_MODEL, W_COLS), BF16),
        compiler_params=pltpu.CompilerParams(
            dimension_semantics=("arbitrary", "arbitrary"), vmem_limit_bytes=VMEM_LIMIT_BYTES),
        name="in_weight_prep",
    )(w_in_t)


def _rope(x, cos, sa, sb):
    return x * cos + pltpu.roll(x, LANES - AXIS_DIM // 2, 1) * sa + pltpu.roll(x, AXIS_DIM // 2, 1) * sb


def _group_rms(x, bd, gain):
    ms = _dot((x * x).astype(BF16), bd) * (1.0 / HEAD_DIM)
    return x * lax.rsqrt(ms + RMS_EPS) * gain


def _ones_row_tile():
    r = lax.broadcasted_iota(jnp.int32, (BF16_ROWS, LANES), 0)
    return jnp.where(r == 0, 1.0, 0.0).astype(BF16)


def _attend_chains(chains, results):
    state = []
    for qs, tiles, sink_row in chains:
        m_cols = qs.shape[0]
        if sink_row is None:
            m = jnp.full((1, m_cols), -jnp.inf, F32)
            acc = jnp.zeros((VT_ROWS, m_cols), F32)
        else:
            m = sink_row
            r = lax.broadcasted_iota(jnp.int32, (VT_ROWS, m_cols), 0)
            acc = jnp.where(r == HEAD_DIM, 1.0, 0.0)
        cur = tiles[0]()
        state.append([m, acc, cur, _dot_tb(cur[0], qs)])
    yield
    for t in range(max(len(tiles) for _, tiles, _ in chains)):
        for st, (qs, tiles, _) in zip(state, chains):
            if t >= len(tiles):
                continue
            m, acc, (_, v_t, valid), s = st
            if valid is not None:
                s = jnp.where(valid, s, -jnp.inf)
            if t + 1 < len(tiles):
                nxt = tiles[t + 1]()
                st[2], st[3] = nxt, _dot_tb(nxt[0], qs)
            m_new = jnp.maximum(m, jnp.max(s, axis=0, keepdims=True))
            p = jnp.exp2(s - m_new).astype(BF16)
            st[1] = acc * jnp.exp2(m - m_new) + _dot(v_t, p)
            st[0] = m_new
        yield
    results.extend(st[1][0:HEAD_DIM, :] / st[1][HEAD_DIM:HEAD_DIM + 1, :] for st in state)


def _interleave(weighted):
    live = [[g, n] for g, n in weighted]
    while live:
        for item in list(live):
            for _ in range(item[1]):
                try:
                    next(item[0])
                except StopIteration:
                    live.remove(item)
                    break
        yield


def _run(gen):
    for _ in gen:
        pass


def _layer_kernel(lat, T, layer, *refs):
    _run(_sequence(lat, T, layer, *refs))


def _sequence(lat, T, layer, *refs):
    refs = list(refs)
    x_ref, mod_ref, w_ref, wb_ref, wo_ref = refs[:5]
    g0_ref, g1_ref, gb_ref, lng_ref, lnb_ref = refs[5:10]
    bd_ref, tril_ref, triu_ref, eye_ref, sink_ref = refs[10:N_SHARED]
    pos = N_SHARED
    if lat:
        cos_ref, sa_ref, sb_ref = refs[pos:pos + 3]
        cka_ref, cva_ref, ckw_ref, cvw_ref, c0_ref, n0_ref, m0_ref = refs[pos + 3:pos + 10]
        pos += 10
        y_ref = refs[pos]
        pos += 1
    else:
        if layer > 0:
            pos += CTX_ALIASED
        x_next_ref = refs[pos]
        pos += 1
        y_ref, ka_o, va_o, kw_o, vw_o, c_o, n_o, m_o = refs[pos:pos + 8]
        pos += 8
        if layer == 0:
            for full in (ka_o, va_o, kw_o, vw_o, c_o):
                full[1:] = jnp.zeros((DEPTH - 1,) + full.shape[1:], F32)
            ka_o, va_o, kw_o, vw_o, c_o = (full.at[0] for full in (ka_o, va_o, kw_o, vw_o, c_o))
    (u_s, qa_s, qw_s, ka_s, vat_s, kw_s, vwt_s, qm_s, km_s, vmt_s, om_s, g_s,
     ya_s, yw_s, hmt_s, c_s, m_s) = refs[pos:pos + 17]
    pos += 17
    if not lat:
        step = pl.program_id(0)
        u_next_s = u_s.at[(step + 1) % 2]
        u_s = u_s.at[step % 2]
    if lat:
        kwc_s, vwct_s = refs[pos:pos + 2]

    row_block = min(T, ROW_BLOCK)
    n_rows = T // row_block
    n_chunks = T // CHUNK
    chunks_per_block = row_block // CHUNK
    shift = mod_ref[0:1, :]
    scale = mod_ref[1:2, :]
    gate = mod_ref[2:3, :]
    ones_tile = _ones_row_tile()

    norm_rows = 64

    def row_slice(i, j, n):
        r0 = i * row_block + j
        return pl.ds(r0 if isinstance(r0, int) else pl.multiple_of(r0, n), n)

    def phase1_norm(i, src=x_ref, dst=u_s):
        for j in range(0, row_block, norm_rows):
            piece = row_slice(i, j, norm_rows)
            dst[piece, :] = (_layer_norm(src[piece, :]) * (1.0 + scale) + shift).astype(BF16)
            yield

    def phase1_proj(i):
        R = row_block
        rows = row_slice(i, 0, R)
        u = u_s[rows, :]

        def proj(c0, width):
            return _dot(u, w_ref[:, c0:c0 + width])

        if lat:
            cos = cos_ref[rows, :]
            sa = sa_ref[rows, :]
            sb = sb_ref[rows, :]

        def rope_wide(v):
            if not lat:
                return v
            slabs = [_rope(v[:, j:j + LANES], cos, sa, sb) for j in range(0, v.shape[1], LANES)]
            return slabs[0] if len(slabs) == 1 else jnp.concatenate(slabs, axis=1)

        qscale = LOG2E * HEAD_DIM ** -0.5
        qa = rope_wide(_group_rms(proj(C_QA, 512), bd_ref[...], g0_ref[...])) * qscale
        for h in range(N_HEADS):
            qa_s[h, rows, :] = qa[:, h * HEAD_DIM:(h + 1) * HEAD_DIM].astype(BF16)
        yield
        ka_n = _group_rms(proj(C_KA, LANES), bd_ref[0:LANES, 0:LANES], g1_ref[...])
        ka = rope_wide(ka_n)
        va = proj(C_VA, LANES)
        qw = rope_wide(proj(C_QW, 512)) * qscale
        for h in range(N_HEADS):
            qw_s[h, rows, :] = qw[:, h * HEAD_DIM:(h + 1) * HEAD_DIM].astype(BF16)
        yield
        kw_raw = proj(C_KW, LANES)
        kw = rope_wide(kw_raw)
        vw = proj(C_VW, LANES)
        for kv in range(N_KV):
            sl = slice(kv * HEAD_DIM, (kv + 1) * HEAD_DIM)
            ka_s[kv, rows, :] = ka[:, sl].astype(BF16)
            kw_s[kv, rows, :] = kw[:, sl].astype(BF16)
        yield
        qm_s[rows, :] = proj(C_QM, 256).astype(BF16)
        km_s[rows, :] = (proj(C_KM, 256) * (ML_DK ** -0.5)).astype(BF16)
        yield
        vm = proj(C_VM, 512)
        yield
        om_s[rows, :] = _sigmoid(proj(C_OM, 512)).astype(BF16)
        g_s[rows, :] = proj(C_GT, LANES) + gb_ref[...]
        yield
        for j in range(chunks_per_block):
            cj = i * chunks_per_block + j
            cr = slice(j * CHUNK, (j + 1) * CHUNK)
            va_t = jnp.transpose(va[cr, :])
            vw_t = jnp.transpose(vw[cr, :])
            for kv in range(N_KV):
                hd = slice(kv * HEAD_DIM, (kv + 1) * HEAD_DIM)
                vat_s[cj, kv, 0:HEAD_DIM, :] = va_t[hd, :].astype(BF16)
                vat_s[cj, kv, HEAD_DIM:VT_ROWS, :] = ones_tile
                vwt_s[cj, kv, 0:HEAD_DIM, :] = vw_t[hd, :].astype(BF16)
                vwt_s[cj, kv, HEAD_DIM:VT_ROWS, :] = ones_tile
            if not lat:
                tc = slice(i * row_block + j * CHUNK, i * row_block + (j + 1) * CHUNK)
                ka_t = jnp.transpose(ka_n[cr, :])
                kw_t = jnp.transpose(kw_raw[cr, :])
                for kv in range(N_KV):
                    hd = slice(kv * HEAD_DIM, (kv + 1) * HEAD_DIM)
                    ka_o[kv, :, tc] = ka_t[hd, :]
                    va_o[kv, :, tc] = va_t[hd, :]
                    kw_o[kv, :, tc] = kw_t[hd, :]
                    vw_o[kv, :, tc] = vw_t[hd, :]
            for h in range(ML_HEADS):
                vmt_s[cj, h, 0:ML_DV, :] = jnp.transpose(vm[cr, h * ML_DV:(h + 1) * ML_DV]).astype(BF16)
                vmt_s[cj, h, ML_DV:ST_ROWS, :] = ones_tile
            yield

    if not lat:
        @pl.when(step == 0)
        def _():
            _run(phase1_norm(0))
        yield from phase1_proj(0)
    else:
        def body1(i, carry):
            _run(phase1_norm(i))
            _run(phase1_proj(i))
            return carry
        lax.fori_loop(0, n_rows, body1, 0)

    if lat:
        eye = eye_ref[...]
        for kv in range(N_KV):
            ka_s[kv, T:T + PAST_LEN, :] = cka_ref[kv].astype(BF16)
            kwc_s[kv] = ckw_ref[kv].astype(BF16)
            cva_t = _dot_tb(eye, cva_ref[kv].astype(BF16)).astype(BF16)
            cvw_t = _dot_tb(eye, cvw_ref[kv].astype(BF16)).astype(BF16)
            for j in range(PAST_LEN // CHUNK):
                vat_s[n_chunks + j, kv, 0:HEAD_DIM, :] = cva_t[:, j * CHUNK:(j + 1) * CHUNK]
                vat_s[n_chunks + j, kv, HEAD_DIM:VT_ROWS, :] = ones_tile
                vwct_s[kv, 0:HEAD_DIM, j * CHUNK:(j + 1) * CHUNK] = cvw_t[:, j * CHUNK:(j + 1) * CHUNK]
                vwct_s[kv, HEAD_DIM:VT_ROWS, j * CHUNK:(j + 1) * CHUNK] = ones_tile

    tq = CHUNK
    m_cols = N_GROUP * tq

    def sink_row(kv):
        col = lax.broadcasted_iota(jnp.int32, (1, m_cols), 1)
        row = jnp.full((1, m_cols), sink_ref[layer * N_HEADS + kv * N_GROUP], F32)
        for g in range(1, N_GROUP):
            row = jnp.where(col >= g * tq, sink_ref[layer * N_HEADS + kv * N_GROUP + g], row)
        return row * LOG2E

    def load_q(q_s, kv, q0):
        return q_s[kv * N_GROUP:(kv + 1) * N_GROUP, pl.ds(q0, tq), :].reshape(m_cols, HEAD_DIM)

    def store_heads(dst, q0, kv, o_t):
        for p in range(N_GROUP // 2):
            blk = jnp.concatenate([o_t[:, (2 * p) * tq:(2 * p + 1) * tq],
                                   o_t[:, (2 * p + 1) * tq:(2 * p + 2) * tq]], axis=0)
            c0 = (kv * N_GROUP + 2 * p) * HEAD_DIM
            dst[pl.ds(q0, tq), c0:c0 + LANES] = jnp.transpose(blk).astype(dst.dtype)

    def key_tiles(k_ref, vt_ref, kv, n_key_chunks, chunks_per_tile):
        def tile(c0):
            def load():
                k = k_ref[kv, c0 * CHUNK:(c0 + chunks_per_tile) * CHUNK, :]
                v_t = [vt_ref[c0 + j, kv] for j in range(chunks_per_tile)]
                return k, (v_t[0] if len(v_t) == 1 else jnp.concatenate(v_t, axis=1)), None
            return load
        return [tile(c0) for c0 in range(0, n_key_chunks, chunks_per_tile)]

    def band_tiles(kv, n, q0):
        j0 = jnp.clip(n - 1, 0, n_chunks - 3)
        w0 = pl.multiple_of(j0 * CHUNK, CHUNK)

        def band(c0, nc):
            def load():
                rows = nc * CHUNK
                kpos = w0 + c0 * CHUNK + lax.broadcasted_iota(jnp.int32, (rows, m_cols), 0)
                qpos = q0 + (lax.broadcasted_iota(jnp.int32, (rows, m_cols), 1) & (tq - 1))
                v_t = [vwt_s[j0 + c0 + j, kv] for j in range(nc)]
                return (kw_s[kv, pl.ds(w0 + c0 * CHUNK, rows), :],
                        v_t[0] if nc == 1 else jnp.concatenate(v_t, axis=1),
                        jnp.abs(kpos - qpos) <= WINDOW)
            return load
        return [band(0, 2), band(2, 1), lambda: (kwc_s[kv], vwct_s[kv], None)]

    def attend(n, q0, chunks_per_tile):
        chains, dsts = [], []
        n_key_chunks = ka_s.shape[1] // CHUNK
        for kv in range(N_KV):
            chains.append((load_q(qa_s, kv, q0), key_tiles(ka_s, vat_s, kv, n_key_chunks, chunks_per_tile), None))
            dsts.append((ya_s, kv))
        for kv in range(N_KV):
            tiles = band_tiles(kv, n, q0) if lat else key_tiles(kw_s, vwt_s, kv, n_chunks, chunks_per_tile)
            chains.append((load_q(qw_s, kv, q0), tiles, sink_row(kv)))
            dsts.append((yw_s, kv))
        for g0 in range(0, len(chains), CHAINS_IN_FLIGHT):
            results = []
            yield from _attend_chains(chains[g0:g0 + CHAINS_IN_FLIGHT], results)
            for (dst, kv), o_t in zip(dsts[g0:g0 + CHAINS_IN_FLIGHT], results):
                store_heads(dst, q0, kv, o_t)

    n_pairs = ML_HEADS // 2
    hmt_s[...] = jnp.zeros_like(hmt_s)
    for d in range(2):
        for h in range(ML_HEADS):
            idx = d * n_pairs + h // 2
            hl = slice((h % 2) * ML_DK, (h % 2 + 1) * ML_DK)
            c_s[idx, ML_DV:ST_ROWS, hl] = jnp.zeros((BF16_ROWS, ML_DK), F32)
            if lat:
                c_s[idx, 0:ML_DV, hl] = c0_ref[d, h]
                c_s[idx, ML_DV:ML_DV + 1, hl] = n0_ref[d, h:h + 1, :]
                m_s[d * ML_HEADS + h:d * ML_HEADS + h + 1, :] = jnp.broadcast_to(
                    m0_ref[d:d + 1, h:h + 1], (1, LANES))
            else:
                c_s[idx, 0:ML_DV, hl] = jnp.zeros((ML_DV, ML_DK), F32)
                m_s[d * ML_HEADS + h:d * ML_HEADS + h + 1, :] = jnp.zeros((1, LANES), F32)

    L = CHUNK
    s_idx = lax.broadcasted_iota(jnp.int32, (L, ML_HEADS * L), 0)
    t_idx = lax.broadcasted_iota(jnp.int32, (L, ML_HEADS * L), 1) & (L - 1)
    lane_row = lax.broadcasted_iota(jnp.int32, (1, LANES), 1)
    low_half = lax.broadcasted_iota(jnp.int32, (L, LANES), 1) < ML_DK
    ones_ll = jnp.ones((L, L), BF16)
    zeros_ll = jnp.zeros((L, L), BF16)

    def heads_row(src, r0, c0=0):
        return jnp.concatenate([src[r0 + h:r0 + h + 1, c0:c0 + L] for h in range(ML_HEADS)], axis=1)

    def block_diag_rows(x):
        zero = jnp.zeros_like(x)
        return jnp.concatenate([jnp.where(low_half, x, zero), jnp.where(low_half, zero, x)], axis=0)

    def mlstm_streams(streams):
        m_state = {d: heads_row(m_s, d * ML_HEADS) for d in sorted({d for d, _ in streams})}
        c_state = {d: [c_s[d * n_pairs + pr] for pr in range(n_pairs)] for d in m_state}
        work = []
        for d, cc in streams:
            r0 = cc * L if isinstance(cc, int) else pl.multiple_of(cc * L, L)
            rows = pl.ds(r0, L)
            g = g_s[rows, :]
            f_hi, f_lo = _split(_log_sigmoid(g))
            tri_c = tril_ref[...] if d == 0 else triu_ref[...]
            tri_r = triu_ref[...] if d == 0 else tril_ref[...]
            cum = _dot(tri_c, f_hi) + _dot(tri_c, f_lo)
            r = g - pltpu.roll(cum, LANES - ML_HEADS, 1)
            gi0 = 2 * ML_HEADS * d
            rb = jnp.concatenate([jnp.broadcast_to(r[:, gi0 + h:gi0 + h + 1], (L, L))
                                  for h in range(ML_HEADS)], axis=1)
            g_t = jnp.transpose(g)[0:N_GATES, :]
            ft_hi, ft_lo = _split(_log_sigmoid(g_t))
            tr = jnp.concatenate([tri_r, ones_ll], axis=1)
            ct = _dot(ft_hi, tr) + _dot(ft_lo, tr)
            work.append(dict(d=d, cc=cc, rows=rows, rb=rb, g_t=g_t, ct=ct))
            yield
        for w in work:
            d = w["d"]
            valid = (s_idx <= t_idx) if d == 0 else (s_idx >= t_idx)
            gi0 = 2 * ML_HEADS * d
            gf0 = gi0 + ML_HEADS
            b_row = heads_row(w["ct"], gf0)
            b_last = heads_row(w["ct"], gf0, L)
            i_row = heads_row(w["g_t"], gi0)
            m_prev = m_state[d]
            a_row = b_row + m_prev
            dm = jnp.where(valid, b_row + w["rb"], -jnp.inf)
            mt = jnp.maximum(a_row, jnp.max(dm, axis=0, keepdims=True))
            w["p"] = jnp.exp(dm - mt)
            w["w_inter"] = jnp.exp(a_row - mt)
            w["floor"] = jnp.exp(-mt)
            g_row = b_last - b_row + i_row
            g_max = jnp.concatenate(
                [jnp.broadcast_to(jnp.max(g_row[:, h * L:(h + 1) * L], axis=-1, keepdims=True), (1, L))
                 for h in range(ML_HEADS)], axis=1)
            m_new = jnp.maximum(b_last + m_prev, g_max)
            w["ws"] = jnp.exp(g_row - m_new)
            w["wc"] = jnp.exp(b_last + m_prev - m_new)
            m_state[d] = m_new
            yield
        for w in work:
            rows, cc = w["rows"], w["cc"]
            w["pairs"] = []
            for pr in range(n_pairs):
                lanes = slice(pr * LANES, (pr + 1) * LANES)
                cols = slice(pr * 2 * L, (pr + 1) * 2 * L)
                k_pair = km_s[rows, lanes]
                q_bd = block_diag_rows(qm_s[rows, lanes])
                k_bd = block_diag_rows(k_pair)
                s_t = (_dot_tb(k_pair, q_bd) * w["p"][:, cols]).astype(BF16)
                s_bd = jnp.concatenate(
                    [jnp.concatenate([s_t[:, 0:L], zeros_ll], axis=1),
                     jnp.concatenate([zeros_ll, s_t[:, L:2 * L]], axis=1)], axis=0)
                v_t = jnp.concatenate([vmt_s[cc, 2 * pr], vmt_s[cc, 2 * pr + 1]], axis=1)
                intra = _dot(v_t, s_bd)
                wv = (v_t.astype(F32) * w["ws"][:, cols]).astype(BF16)
                w["pairs"].append((q_bd, intra, _dot(wv, k_bd)))
            yield
        for w in work:
            d, cc = w["d"], w["cc"]
            for pr, (q_bd, intra, update) in enumerate(w["pairs"]):
                cols = slice(pr * 2 * L, (pr + 1) * 2 * L)
                state = c_state[d][pr]
                tot = intra + w["w_inter"][:, cols] * _dot_tb(state.astype(BF16), q_bd)
                h_t = tot[0:ML_DV, :] / jnp.maximum(jnp.abs(tot[ML_DV:ML_DV + 1, :]), w["floor"][:, cols])
                for e in range(2):
                    hr = slice((2 * pr + e) * ML_DV, (2 * pr + e + 1) * ML_DV)
                    hmt_s[cc, hr, :] += h_t[:, e * L:(e + 1) * L]
                wc = w["wc"]
                wc_pair = jnp.where(lane_row < ML_DK, wc[:, 2 * pr * L:(2 * pr + 1) * L],
                                    wc[:, (2 * pr + 1) * L:(2 * pr + 2) * L])
                c_state[d][pr] = wc_pair * state + update
            yield
        for d in m_state:
            for h in range(ML_HEADS):
                m_s[d * ML_HEADS + h:d * ML_HEADS + h + 1, :] = m_state[d][:, h * L:(h + 1) * L]
            for pr in range(n_pairs):
                c_s[d * n_pairs + pr] = c_state[d][pr]

    def scan_streams(cs):
        return [(d, c if d == 0 else n_chunks - 1 - c) for c in cs for d in range(2)]

    if lat:
        def body2(n, carry):
            _run(_interleave([(attend(n, pl.multiple_of(n * tq, tq), 2), 1),
                              (mlstm_streams(scan_streams([n])), 1)]))
            return carry
        lax.fori_loop(0, n_chunks, body2, 0)
    else:
        def all_tiles():
            for n in range(T // tq):
                yield from attend(n, n * tq, 2)
        yield from _interleave([(all_tiles(), 1), (mlstm_streams(scan_streams(range(n_chunks))), 3)])

    if not lat:
        for d in range(2):
            for pr in range(n_pairs):
                c_t = jnp.transpose(c_s[d * n_pairs + pr, 0:ML_DV, :])
                for e in range(2):
                    c_o[d, 2 * pr + e] = c_t[e * ML_DK:(e + 1) * ML_DK, :]
            for h in range(ML_HEADS):
                idx = d * n_pairs + h // 2
                hl = slice((h % 2) * ML_DK, (h % 2 + 1) * ML_DK)
                n_o[d, h:h + 1, :] = c_s[idx, ML_DV:ML_DV + 1, hl]
                m_o[d:d + 1, h:h + 1] = m_s[d * ML_HEADS + h:d * ML_HEADS + h + 1, 0:1]

    def phase3_merge(i, out):
        rows = row_slice(i, 0, row_block)
        u = u_s[rows, :]

        def proj(c0, width):
            return _dot(u, w_ref[:, c0:c0 + width])

        hm = jnp.concatenate(
            [jnp.concatenate([jnp.transpose(hmt_s[i * chunks_per_block + j, h * ML_DV:(h + 1) * ML_DV, :])
                              for h in range(ML_HEADS)], axis=1)
             for j in range(chunks_per_block)], axis=0)
        ys = (ya_s[rows, :].astype(F32), om_s[rows, :].astype(F32) * hm, yw_s[rows, :].astype(F32))
        yield
        merged = None
        for b, (yb, zc) in enumerate(zip(ys, (C_ZA, C_ZM, C_ZW))):
            z = proj(zc, BRANCH_W)
            t = (yb * (z * _sigmoid(z))).astype(BF16)
            yield
            pb = _sigmoid(proj(C_GMERGE + b * D_MODEL, D_MODEL)) * _dot(t, wb_ref[b])
            merged = pb if merged is None else merged + pb
            yield
        out.append(_dot(merged.astype(BF16), wo_ref[...]))
        yield

    def phase3_norm(i, o):
        for j in range(0, row_block, norm_rows):
            piece = row_slice(i, j, norm_rows)
            hres = ALPHA * x_ref[piece, :] + gate * o[j:j + norm_rows, :]
            y_ref[piece, :] = _layer_norm(hres) * lng_ref[...] + lnb_ref[...]
            yield

    def phase3(i):
        box = []
        yield from phase3_merge(i, box)
        yield from phase3_norm(i, box[0])

    if not lat:
        box = []
        yield from _interleave([(phase3_merge(0, box), 1), (phase1_norm(0, x_next_ref, u_next_s), 1)])
        yield from phase3_norm(0, box[0])
    else:
        def body4(i, carry):
            _run(phase3(i))
            return carry
        lax.fori_loop(0, n_rows, body4, 0)


def _const_spec(shape):
    nd = len(shape)
    return pl.BlockSpec(shape, lambda b: (0,) * nd, pipeline_mode=pl.Buffered(1))


def _layer_spec(shape, layer):
    nd = len(shape) - 1
    return pl.BlockSpec((None,) + tuple(shape[1:]), lambda b: (layer,) + (0,) * nd,
                        pipeline_mode=pl.Buffered(1))


def _layer_call(lat, layer, x, mod, weights, consts, extra):
    B, T, _ = x.shape
    S = T + PAST_LEN if lat else T
    n_chunks = T // CHUNK

    seq_spec = pl.BlockSpec((None, T, D_MODEL), lambda b: (b, 0, 0))
    out_seq_spec = seq_spec
    if lat:
        mod_spec = pl.BlockSpec((None, None, 3, D_MODEL), lambda b: (layer, b + 1, 0, 0))
    else:
        mod_spec = pl.BlockSpec((None, None, 3, D_MODEL), lambda b: (layer, 0, 0, 0))
    *vmem_consts, sink = consts
    in_specs = [seq_spec, mod_spec]
    in_specs += [_layer_spec(a.shape, layer) for a in weights]
    in_specs += [_const_spec(a.shape) for a in vmem_consts]
    in_specs.append(pl.BlockSpec(memory_space=pltpu.SMEM))
    args = [x, mod, *weights, *vmem_consts, sink]

    if lat:
        cos, sa, sb, cka, cva, ckw, cvw, sc, sn, sm = extra
        in_specs += [_const_spec(cos.shape)] * 3
        kv_spec = pl.BlockSpec((None, None, N_KV, PAST_LEN, HEAD_DIM), lambda b: (b, layer, 0, 0, 0))
        in_specs += [kv_spec] * 4
        in_specs += [
            pl.BlockSpec((None, None, 2, ML_HEADS, ML_DV, ML_DK), lambda b: (b, layer, 0, 0, 0, 0)),
            pl.BlockSpec((None, None, 2, ML_HEADS, ML_DK), lambda b: (b, layer, 0, 0, 0)),
            pl.BlockSpec((None, None, 2, ML_HEADS), lambda b: (b, layer, 0, 0)),
        ]
        args += [cos, sa, sb, cka, cva, ckw, cvw, sc, sn, sm]
        out_specs = out_seq_spec
        out_shape = jax.ShapeDtypeStruct((B, T, D_MODEL), F32)
        aliases = {}
    else:
        kv_shape = (B, DEPTH, N_KV, HEAD_DIM, T)
        c_shape = (B, DEPTH, 2, ML_HEADS, ML_DK, ML_DV)
        if layer == 0:
            aliases = {}
            ld = DEPTH
            layer_idx = 0
        else:
            first_acc = len(args)
            in_specs += [pl.BlockSpec(memory_space=pl.ANY)] * len(extra)
            args += list(extra)
            aliases = {first_acc + i: 1 + i for i in range(len(extra))}
            assert len(extra) == CTX_ALIASED
            ld = None
            layer_idx = layer
        in_specs.append(pl.BlockSpec((None, T, D_MODEL), lambda b: (jnp.minimum(b + 1, B - 1), 0, 0)))
        args.append(x)
        kv_out = pl.BlockSpec((None, ld, N_KV, HEAD_DIM, T), lambda b: (b, layer_idx, 0, 0, 0))
        out_specs = [
            seq_spec, kv_out, kv_out, kv_out, kv_out,
            pl.BlockSpec((None, ld, 2, ML_HEADS, ML_DK, ML_DV), lambda b: (b, layer_idx, 0, 0, 0, 0)),
            pl.BlockSpec((None, 2, ML_HEADS, ML_DK), lambda b: (b, 0, 0, 0)),
            pl.BlockSpec((None, 2, ML_HEADS), lambda b: (b, 0, 0)),
        ]
        out_shape = [jax.ShapeDtypeStruct((B, T, D_MODEL), F32)]
        out_shape += [jax.ShapeDtypeStruct(kv_shape, F32)] * 4 + [jax.ShapeDtypeStruct(c_shape, F32)]
        out_shape += [
            jax.ShapeDtypeStruct((B, 2, ML_HEADS, ML_DK), F32),
            jax.ShapeDtypeStruct((B, 2, ML_HEADS), F32),
        ]

    scratch = [
        ((T, D_MODEL) if lat else (2, T, D_MODEL), BF16),
        ((N_HEADS, T, HEAD_DIM), BF16),
        ((N_HEADS, T, HEAD_DIM), BF16),
        ((N_KV, S, HEAD_DIM), BF16),
        ((S // CHUNK, N_KV, VT_ROWS, CHUNK), BF16),
        ((N_KV, T, HEAD_DIM), BF16),
        ((n_chunks, N_KV, VT_ROWS, CHUNK), BF16),
        ((T, ML_HEADS * ML_DK), BF16),
        ((T, ML_HEADS * ML_DK), BF16),
        ((n_chunks, ML_HEADS, ST_ROWS, CHUNK), BF16),
        ((T, ML_HEADS * ML_DV), BF16),
        ((T, LANES), F32),
        ((T, BRANCH_W), BF16),
        ((T, BRANCH_W), BF16),
        ((n_chunks, ML_HEADS * ML_DV, CHUNK), F32),
        ((ML_HEADS, ST_ROWS, 2 * ML_DK), F32),
        ((2 * ML_HEADS, LANES), F32),
    ]
    if lat:
        scratch += [((N_KV, PAST_LEN, HEAD_DIM), BF16),
                    ((N_KV, VT_ROWS, PAST_LEN), BF16)]
    scratch = [pltpu.VMEM(shape, dtype) for shape, dtype in scratch]

    return pl.pallas_call(
        functools.partial(_layer_kernel, lat, T, layer),
        grid=(B,),
        in_specs=in_specs,
        out_specs=out_specs,
        out_shape=out_shape,
        scratch_shapes=scratch,
        input_output_aliases=aliases,
        compiler_params=pltpu.CompilerParams(
            dimension_semantics=("arbitrary",), vmem_limit_bytes=VMEM_LIMIT_BYTES),
        name=("latent_layer" if lat else "context_layer") + str(layer),
    )(*args)


def _rope_tables(T):
    rows = T // GRID_W
    row = jnp.repeat(jnp.arange(rows, dtype=F32), GRID_W)
    col = jnp.tile(jnp.arange(GRID_W, dtype=F32), rows)
    inv = ROPE_THETA ** (-jnp.arange(0, AXIS_DIM, 2, dtype=F32) / AXIS_DIM)
    ar = row[:, None] * inv
    ac = col[:, None] * inv
    ang = jnp.concatenate([ar, ar, ac, ac], axis=-1)
    ang = jnp.concatenate([ang, ang], axis=-1)
    first_half = (jnp.arange(LANES) % AXIS_DIM) < (AXIS_DIM // 2)
    cos = jnp.cos(ang)
    sin = jnp.sin(ang)
    sa = jnp.where(first_half[None, :], -sin, 0.0)
    sb = jnp.where(first_half[None, :], 0.0, sin)
    return cos, sa, sb


def _constants(sink_logit):
    grp = np.arange(N_HEADS * HEAD_DIM) // HEAD_DIM
    bd = jnp.asarray(grp[:, None] == grp[None, :], dtype=BF16)
    ti = np.arange(CHUNK)
    tril = jnp.asarray(ti[None, :] <= ti[:, None], dtype=BF16)
    triu = jnp.asarray(ti[None, :] >= ti[:, None], dtype=BF16)
    eye = jnp.asarray(np.eye(HEAD_DIM), dtype=BF16)
    return bd, tril, triu, eye, sink_logit.reshape(-1)


def kernel(x_prompt, x_sample, cache_attn_k, cache_attn_v, cache_win_k, cache_win_v, state_mlstm_C,
           state_mlstm_n, state_mlstm_m, c, c_ctx, w_mod, b_mod, w_in, qk_gain, sink_logit,
           mlstm_gate_bias, w_branch, w_out, ln_gain, ln_bias):
    dec_b = x_sample.shape[0]
    assert dec_b + 1 <= 8
    cond = jnp.concatenate([c_ctx[None, :], c, jnp.zeros((8 - 1 - dec_b, D_MODEL), F32)], axis=0)
    mod = _modulation(cond, w_mod, b_mod).reshape(DEPTH, 8, 3, D_MODEL)

    consts = _constants(sink_logit)
    cos, sa, sb = _rope_tables(x_sample.shape[1])
    lane_pad = jnp.zeros((DEPTH, 1, LANES - N_GATES), F32)
    weights = (
        _prep_in_weight(jnp.swapaxes(w_in, 1, 2)), w_branch.astype(BF16), w_out.astype(BF16),
        jnp.tile(qk_gain[:, 0:1, :], (1, 1, N_HEADS)),
        jnp.tile(qk_gain[:, 1:2, :], (1, 1, N_KV)),
        jnp.concatenate([mlstm_gate_bias.reshape(DEPTH, 1, N_GATES), lane_pad], axis=2),
        ln_gain[:, None, :], ln_bias[:, None, :],
    )

    ctx = None
    xp, xs = x_prompt, x_sample
    small = []
    for l in range(DEPTH):
        outs = _layer_call(False, l, xp, mod, weights, consts, ctx)
        xp = outs[0]
        ctx = tuple(outs[1:6])
        small.append(outs[6:])
        extra = (cos, sa, sb, cache_attn_k, cache_attn_v, cache_win_k, cache_win_v,
                 state_mlstm_C, state_mlstm_n, state_mlstm_m)
        xs = _layer_call(True, l, xs, mod, weights, consts, extra)
    new_n, new_m = [jnp.stack([small[l][i] for l in range(DEPTH)], axis=1) for i in range(2)]
    return (xp, xs, *[jnp.swapaxes(a, -1, -2) for a in ctx], new_n, new_m)
```

```python
import functools

import jax
import jax.numpy as jnp
import numpy as np
from jax import lax
from jax.experimental import pallas as pl
from jax.experimental.pallas import tpu as pltpu

F32 = jnp.float32
BF16 = jnp.bfloat16

D_MODEL = 1024
DEPTH = 2
PAST_LEN = 256
GRID_W = 64
HEAD_DIM = 64
N_HEADS = 8
N_KV = 2
N_GROUP = N_HEADS // N_KV
WINDOW = 128
ML_HEADS = 4
ML_DK = 64
ML_DV = 128
CHUNK = 128
BRANCH_W = 512
ROPE_THETA = 10000.0
AXIS_DIM = HEAD_DIM // 2
LN_EPS = 1e-6
RMS_EPS = 1e-6
ALPHA = (2.0 * DEPTH) ** 0.25
LOG2E = float(np.log2(np.e))
LANES = 128
BF16_ROWS = 16
ROW_BLOCK = 256
CHAINS_IN_FLIGHT = 4
CTX_ALIASED = 5
N_SHARED = 14
VMEM_LIMIT_BYTES = 62 * 1024 * 1024
VT_ROWS = HEAD_DIM + BF16_ROWS
ST_ROWS = ML_DV + BF16_ROWS

_SIZES = (512, 128, 128, 512, 256, 256, 512, 512, 128, 512, 512, 128, 128, 512, 3072)
_OFF = [int(v) for v in np.concatenate([[0], np.cumsum(_SIZES)])]
(C_QA, C_KA, C_VA, C_ZA, C_QM, C_KM, C_VM, C_OM, C_GT, C_ZM, C_QW, C_KW, C_VW, C_ZW, C_GMERGE) = _OFF[:-1]
W_COLS = _OFF[-1]
GATE_COL = 2816
N_GATES = 4 * ML_HEADS


def _dot(a, b):
    return jnp.dot(a, b, preferred_element_type=F32)


def _dot_tb(a, b):
    return lax.dot_general(a, b, (((1,), (1,)), ((), ())), preferred_element_type=F32)


def _split(a):
    hi = a.astype(BF16)
    lo = (a - hi.astype(F32)).astype(BF16)
    return hi, lo


def _layer_norm(x):
    mu = jnp.mean(x, axis=-1, keepdims=True)
    xc = x - mu
    var = jnp.mean(xc * xc, axis=-1, keepdims=True)
    return xc * lax.rsqrt(var + LN_EPS)


def _log_sigmoid(x):
    return jnp.minimum(x, 0.0) - jnp.log(1.0 + jnp.exp(-jnp.abs(x)))


def _sigmoid(x):
    return jax.nn.sigmoid(x)


def _mod_kernel(c_ref, w_ref, b_ref, o_ref):
    c = c_ref[...]
    s = c * _sigmoid(c)
    s_hi, s_lo = _split(s)
    w = w_ref[...]
    w_hi, w_lo = _split(w)
    o_ref[...] = _dot(s_hi, w_hi) + _dot(s_lo, w_hi) + _dot(s_hi, w_lo) + b_ref[...]


def _modulation(cond, w_mod, b_mod):
    rows = cond.shape[0]
    tn = 1024
    return pl.pallas_call(
        _mod_kernel,
        grid=(DEPTH, 3 * D_MODEL // tn),
        in_specs=[
            pl.BlockSpec((rows, D_MODEL), lambda l, j: (0, 0)),
            pl.BlockSpec((None, D_MODEL, tn), lambda l, j: (l, 0, j)),
            pl.BlockSpec((None, 1, tn), lambda l, j: (l, 0, j)),
        ],
        out_specs=pl.BlockSpec((None, rows, tn), lambda l, j: (l, 0, j)),
        out_shape=jax.ShapeDtypeStruct((DEPTH, rows, 3 * D_MODEL), F32),
        compiler_params=pltpu.CompilerParams(dimension_semantics=("arbitrary", "arbitrary")),
        name="adaln_modulation",
    )(cond, w_mod, b_mod.reshape(DEPTH, 1, 3 * D_MODEL))


PREP_K = 256


def _prep_kernel(wt_ref, o_ref):
    lane = lax.broadcasted_iota(jnp.int32, (PREP_K, LANES), 1)
    for c0 in range(0, W_COLS, LANES):
        r0 = c0 if c0 <= GATE_COL else c0 - (LANES - N_GATES)
        slab = jnp.transpose(wt_ref[r0:r0 + LANES, :])
        if c0 == GATE_COL:
            slab = jnp.where(lane < N_GATES, slab, 0.0)
        o_ref[:, c0:c0 + LANES] = slab.astype(BF16)


def _prep_in_weight(w_in_t):
    n_in = w_in_t.shape[1]
    return pl.pallas_call(
        _prep_kernel,
        grid=(DEPTH, D_MODEL // PREP_K),
        in_specs=[pl.BlockSpec((None, n_in, PREP_K), lambda l, i: (l, 0, i))],
        out_specs=pl.BlockSpec((None, PREP_K, W_COLS), lambda l, i: (l, i, 0)),
        out_shape=jax.ShapeDtypeStruct((DEPTH, D_MODEL, W_COLS), BF16),
        compiler_params=pltpu.CompilerParams(
            dimension_semantics=("arbitrary", "arbitrary"), vmem_limit_bytes=VMEM_LIMIT_BYTES),
        name="in_weight_prep",
    )(w_in_t)


def _rope(x, cos, sa, sb):
    return x * cos + pltpu.roll(x, LANES - AXIS_DIM // 2, 1) * sa + pltpu.roll(x, AXIS_DIM // 2, 1) * sb


def _group_rms(x, bd, gain):
    ms = _dot((x * x).astype(BF16), bd) * (1.0 / HEAD_DIM)
    return x * lax.rsqrt(ms + RMS_EPS) * gain


def _ones_row_tile():
    r = lax.broadcasted_iota(jnp.int32, (BF16_ROWS, LANES), 0)
    return jnp.where(r == 0, 1.0, 0.0).astype(BF16)


def _attend_chains(chains, results):
    state = []
    for qs, tiles, sink_row in chains:
        m_cols = qs.shape[0]
        if sink_row is None:
            m = jnp.full((1, m_cols), -jnp.inf, F32)
            acc = jnp.zeros((VT_ROWS, m_cols), F32)
        else:
            m = sink_row
            r = lax.broadcasted_iota(jnp.int32, (VT_ROWS, m_cols), 0)
            acc = jnp.where(r == HEAD_DIM, 1.0, 0.0)
        cur = tiles[0]()
        state.append([m, acc, cur, _dot_tb(cur[0], qs)])
    yield
    for t in range(max(len(tiles) for _, tiles, _ in chains)):
        for st, (qs, tiles, _) in zip(state, chains):
            if t >= len(tiles):
                continue
            m, acc, (_, v_t, valid), s = st
            if valid is not None:
                s = jnp.where(valid, s, -jnp.inf)
            if t + 1 < len(tiles):
                nxt = tiles[t + 1]()
                st[2], st[3] = nxt, _dot_tb(nxt[0], qs)
            m_new = jnp.maximum(m, jnp.max(s, axis=0, keepdims=True))
            p = jnp.exp2(s - m_new).astype(BF16)
            st[1] = acc * jnp.exp2(m - m_new) + _dot(v_t, p)
            st[0] = m_new
        yield
    results.extend(st[1][0:HEAD_DIM, :] / st[1][HEAD_DIM:HEAD_DIM + 1, :] for st in state)


def _interleave(weighted):
    live = [[g, n] for g, n in weighted]
    while live:
        for item in list(live):
            for _ in range(item[1]):
                try:
                    next(item[0])
                except StopIteration:
                    live.remove(item)
                    break
        yield


def _run(gen):
    for _ in gen:
        pass


def _layer_kernel(lat, T, layer, *refs):
    _run(_sequence(lat, T, layer, *refs))


def _sequence(lat, T, layer, *refs):
    refs = list(refs)
    x_ref, mod_ref, w_ref, wb_ref, wo_ref = refs[:5]
    g0_ref, g1_ref, gb_ref, lng_ref, lnb_ref = refs[5:10]
    bd_ref, tril_ref, triu_ref, sink_ref = refs[10:N_SHARED]
    pos = N_SHARED
    if lat:
        cos_ref, sa_ref, sb_ref = refs[pos:pos + 3]
        cka_ref, cva_ref, ckw_ref, cvw_ref, c0_ref, n0_ref, m0_ref = refs[pos + 3:pos + 10]
        pos += 10
        y_ref = refs[pos]
        pos += 1
    else:
        if layer > 0:
            pos += CTX_ALIASED
        x_next_ref = refs[pos]
        pos += 1
        y_ref, ka_o, va_o, kw_o, vw_o, c_o, n_o, m_o = refs[pos:pos + 8]
        pos += 8
        if layer == 0:
            for full in (ka_o, va_o, kw_o, vw_o, c_o):
                full[1:] = jnp.zeros((DEPTH - 1,) + full.shape[1:], F32)
            ka_o, va_o, kw_o, vw_o, c_o = (full.at[0] for full in (ka_o, va_o, kw_o, vw_o, c_o))
    (u_s, qa_s, qw_s, ka_s, vat_s, kw_s, vwt_s, qm_s, km_s, vmt_s, om_s, g_s,
     ya_s, yw_s, hmt_s, c_s, m_s) = refs[pos:pos + 17]
    pos += 17
    if not lat:
        step = pl.program_id(0)
        u_next_s = u_s.at[(step + 1) % 2]
        u_s = u_s.at[step % 2]
    if lat:
        kwc_s, vwct_s = refs[pos:pos + 2]

    row_block = min(T, ROW_BLOCK)
    n_rows = T // row_block
    n_chunks = T // CHUNK
    chunks_per_block = row_block // CHUNK
    shift = mod_ref[0:1, :]
    scale = mod_ref[1:2, :]
    gate = mod_ref[2:3, :]
    ones_tile = _ones_row_tile()

    norm_rows = 64

    def row_slice(i, j, n):
        r0 = i * row_block + j
        return pl.ds(r0 if isinstance(r0, int) else pl.multiple_of(r0, n), n)

    def phase1_norm(i, src=x_ref, dst=u_s):
        for j in range(0, row_block, norm_rows):
            piece = row_slice(i, j, norm_rows)
            dst[piece, :] = (_layer_norm(src[piece, :]) * (1.0 + scale) + shift).astype(BF16)
            yield

    def phase1_proj(i):
        R = row_block
        rows = row_slice(i, 0, R)
        u = u_s[rows, :]

        def proj(c0, width):
            return _dot(u, w_ref[:, c0:c0 + width])

        if lat:
            cos = cos_ref[rows, :]
            sa = sa_ref[rows, :]
            sb = sb_ref[rows, :]

        def rope_wide(v):
            if not lat:
                return v
            slabs = [_rope(v[:, j:j + LANES], cos, sa, sb) for j in range(0, v.shape[1], LANES)]
            return slabs[0] if len(slabs) == 1 else jnp.concatenate(slabs, axis=1)

        qscale = LOG2E * HEAD_DIM ** -0.5
        qa = rope_wide(_group_rms(proj(C_QA, 512), bd_ref[...], g0_ref[...])) * qscale
        for h in range(N_HEADS):
            qa_s[h, rows, :] = qa[:, h * HEAD_DIM:(h + 1) * HEAD_DIM].astype(BF16)
        yield
        ka_n = _group_rms(proj(C_KA, LANES), bd_ref[0:LANES, 0:LANES], g1_ref[...])
        ka = rope_wide(ka_n)
        va = proj(C_VA, LANES)
        qw = rope_wide(proj(C_QW, 512)) * qscale
        for h in range(N_HEADS):
            qw_s[h, rows, :] = qw[:, h * HEAD_DIM:(h + 1) * HEAD_DIM].astype(BF16)
        yield
        kw_raw = proj(C_KW, LANES)
        kw = rope_wide(kw_raw)
        vw = proj(C_VW, LANES)
        for kv in range(N_KV):
            sl = slice(kv * HEAD_DIM, (kv + 1) * HEAD_DIM)
            ka_s[kv, rows, :] = ka[:, sl].astype(BF16)
            kw_s[kv, rows, :] = kw[:, sl].astype(BF16)
        yield
        qm_s[rows, :] = proj(C_QM, 256).astype(BF16)
        km_s[rows, :] = (proj(C_KM, 256) * (ML_DK ** -0.5)).astype(BF16)
        yield
        vm = proj(C_VM, 512)
        yield
        om_s[rows, :] = _sigmoid(proj(C_OM, 512)).astype(BF16)
        g_s[rows, :] = proj(C_GT, LANES) + gb_ref[...]
        yield
        for j in range(chunks_per_block):
            cj = i * chunks_per_block + j
            cr = slice(j * CHUNK, (j + 1) * CHUNK)
            va_t = jnp.transpose(va[cr, :])
            vw_t = jnp.transpose(vw[cr, :])
            for kv in range(N_KV):
                hd = slice(kv * HEAD_DIM, (kv + 1) * HEAD_DIM)
                vat_s[cj, kv, 0:HEAD_DIM, :] = va_t[hd, :].astype(BF16)
                vat_s[cj, kv, HEAD_DIM:VT_ROWS, :] = ones_tile
                vwt_s[cj, kv, 0:HEAD_DIM, :] = vw_t[hd, :].astype(BF16)
                vwt_s[cj, kv, HEAD_DIM:VT_ROWS, :] = ones_tile
            if not lat:
                tc = slice(i * row_block + j * CHUNK, i * row_block + (j + 1) * CHUNK)
                ka_t = jnp.transpose(ka_n[cr, :])
                kw_t = jnp.transpose(kw_raw[cr, :])
                for kv in range(N_KV):
                    hd = slice(kv * HEAD_DIM, (kv + 1) * HEAD_DIM)
                    ka_o[kv, :, tc] = ka_t[hd, :]
                    va_o[kv, :, tc] = va_t[hd, :]
                    kw_o[kv, :, tc] = kw_t[hd, :]
                    vw_o[kv, :, tc] = vw_t[hd, :]
            for h in range(ML_HEADS):
                vmt_s[cj, h, 0:ML_DV, :] = jnp.transpose(vm[cr, h * ML_DV:(h + 1) * ML_DV]).astype(BF16)
                vmt_s[cj, h, ML_DV:ST_ROWS, :] = ones_tile
            yield

    if not lat:
        @pl.when(step == 0)
        def _():
            _run(phase1_norm(0))
        yield from phase1_proj(0)
    else:
        def body1(i, carry):
            _run(phase1_norm(i))
            _run(phase1_proj(i))
            return carry
        lax.fori_loop(0, n_rows, body1, 0)

    if lat:
        for j in range(PAST_LEN // CHUNK):
            cols = slice(j * CHUNK, (j + 1) * CHUNK)
            ka_c = jnp.transpose(jnp.concatenate([cka_ref[kv, :, cols] for kv in range(N_KV)], axis=0))
            kw_c = jnp.transpose(jnp.concatenate([ckw_ref[kv, :, cols] for kv in range(N_KV)], axis=0))
            for kv in range(N_KV):
                hd = slice(kv * HEAD_DIM, (kv + 1) * HEAD_DIM)
                ka_s[kv, T + j * CHUNK:T + (j + 1) * CHUNK, :] = ka_c[:, hd].astype(BF16)
                kwc_s[kv, cols, :] = kw_c[:, hd].astype(BF16)
                vat_s[n_chunks + j, kv, 0:HEAD_DIM, :] = cva_ref[kv, :, cols].astype(BF16)
                vat_s[n_chunks + j, kv, HEAD_DIM:VT_ROWS, :] = ones_tile
                vwct_s[kv, 0:HEAD_DIM, cols] = cvw_ref[kv, :, cols].astype(BF16)
                vwct_s[kv, HEAD_DIM:VT_ROWS, cols] = ones_tile

    tq = CHUNK
    m_cols = N_GROUP * tq

    def sink_row(kv):
        col = lax.broadcasted_iota(jnp.int32, (1, m_cols), 1)
        row = jnp.full((1, m_cols), sink_ref[layer * N_HEADS + kv * N_GROUP], F32)
        for g in range(1, N_GROUP):
            row = jnp.where(col >= g * tq, sink_ref[layer * N_HEADS + kv * N_GROUP + g], row)
        return row * LOG2E

    def load_q(q_s, kv, q0):
        return q_s[kv * N_GROUP:(kv + 1) * N_GROUP, pl.ds(q0, tq), :].reshape(m_cols, HEAD_DIM)

    def store_heads(dst, q0, kv, o_t):
        for p in range(N_GROUP // 2):
            blk = jnp.concatenate([o_t[:, (2 * p) * tq:(2 * p + 1) * tq],
                                   o_t[:, (2 * p + 1) * tq:(2 * p + 2) * tq]], axis=0)
            c0 = (kv * N_GROUP + 2 * p) * HEAD_DIM
            dst[pl.ds(q0, tq), c0:c0 + LANES] = jnp.transpose(blk).astype(dst.dtype)

    def key_tiles(k_ref, vt_ref, kv, n_key_chunks, chunks_per_tile):
        def tile(c0):
            def load():
                k = k_ref[kv, c0 * CHUNK:(c0 + chunks_per_tile) * CHUNK, :]
                v_t = [vt_ref[c0 + j, kv] for j in range(chunks_per_tile)]
                return k, (v_t[0] if len(v_t) == 1 else jnp.concatenate(v_t, axis=1)), None
            return load
        return [tile(c0) for c0 in range(0, n_key_chunks, chunks_per_tile)]

    def band_tiles(kv, n, q0):
        j0 = jnp.clip(n - 1, 0, n_chunks - 3)
        w0 = pl.multiple_of(j0 * CHUNK, CHUNK)

        def band(c0, nc):
            def load():
                rows = nc * CHUNK
                kpos = w0 + c0 * CHUNK + lax.broadcasted_iota(jnp.int32, (rows, m_cols), 0)
                qpos = q0 + (lax.broadcasted_iota(jnp.int32, (rows, m_cols), 1) & (tq - 1))
                v_t = [vwt_s[j0 + c0 + j, kv] for j in range(nc)]
                return (kw_s[kv, pl.ds(w0 + c0 * CHUNK, rows), :],
                        v_t[0] if nc == 1 else jnp.concatenate(v_t, axis=1),
                        jnp.abs(kpos - qpos) <= WINDOW)
            return load
        return [band(0, 2), band(2, 1), lambda: (kwc_s[kv], vwct_s[kv], None)]

    def attend(n, q0, chunks_per_tile):
        chains, dsts = [], []
        n_key_chunks = ka_s.shape[1] // CHUNK
        for kv in range(N_KV):
            chains.append((load_q(qa_s, kv, q0), key_tiles(ka_s, vat_s, kv, n_key_chunks, chunks_per_tile), None))
            dsts.append((ya_s, kv))
        for kv in range(N_KV):
            tiles = band_tiles(kv, n, q0) if lat else key_tiles(kw_s, vwt_s, kv, n_chunks, chunks_per_tile)
            chains.append((load_q(qw_s, kv, q0), tiles, sink_row(kv)))
            dsts.append((yw_s, kv))
        for g0 in range(0, len(chains), CHAINS_IN_FLIGHT):
            results = []
            yield from _attend_chains(chains[g0:g0 + CHAINS_IN_FLIGHT], results)
            for (dst, kv), o_t in zip(dsts[g0:g0 + CHAINS_IN_FLIGHT], results):
                store_heads(dst, q0, kv, o_t)

    n_pairs = ML_HEADS // 2
    hmt_s[...] = jnp.zeros_like(hmt_s)
    for d in range(2):
        if lat:
            for pr in range(n_pairs):
                c_s[d * n_pairs + pr, 0:ML_DV, :] = jnp.transpose(
                    jnp.concatenate([c0_ref[d, 2 * pr], c0_ref[d, 2 * pr + 1]], axis=0))
        for h in range(ML_HEADS):
            idx = d * n_pairs + h // 2
            hl = slice((h % 2) * ML_DK, (h % 2 + 1) * ML_DK)
            c_s[idx, ML_DV:ST_ROWS, hl] = jnp.zeros((BF16_ROWS, ML_DK), F32)
            if lat:
                c_s[idx, ML_DV:ML_DV + 1, hl] = n0_ref[d, h:h + 1, :]
                m_s[d * ML_HEADS + h:d * ML_HEADS + h + 1, :] = jnp.broadcast_to(
                    m0_ref[d:d + 1, h:h + 1], (1, LANES))
            else:
                c_s[idx, 0:ML_DV, hl] = jnp.zeros((ML_DV, ML_DK), F32)
                m_s[d * ML_HEADS + h:d * ML_HEADS + h + 1, :] = jnp.zeros((1, LANES), F32)

    L = CHUNK
    s_idx = lax.broadcasted_iota(jnp.int32, (L, ML_HEADS * L), 0)
    t_idx = lax.broadcasted_iota(jnp.int32, (L, ML_HEADS * L), 1) & (L - 1)
    lane_row = lax.broadcasted_iota(jnp.int32, (1, LANES), 1)
    low_half = lax.broadcasted_iota(jnp.int32, (L, LANES), 1) < ML_DK
    ones_ll = jnp.ones((L, L), BF16)
    zeros_ll = jnp.zeros((L, L), BF16)

    def heads_row(src, r0, c0=0):
        return jnp.concatenate([src[r0 + h:r0 + h + 1, c0:c0 + L] for h in range(ML_HEADS)], axis=1)

    def block_diag_rows(x):
        zero = jnp.zeros_like(x)
        return jnp.concatenate([jnp.where(low_half, x, zero), jnp.where(low_half, zero, x)], axis=0)

    def mlstm_streams(streams):
        m_state = {d: heads_row(m_s, d * ML_HEADS) for d in sorted({d for d, _ in streams})}
        c_state = {d: [c_s[d * n_pairs + pr] for pr in range(n_pairs)] for d in m_state}
        work = []
        for d, cc in streams:
            r0 = cc * L if isinstance(cc, int) else pl.multiple_of(cc * L, L)
            rows = pl.ds(r0, L)
            g = g_s[rows, :]
            f_hi, f_lo = _split(_log_sigmoid(g))
            tri_c = tril_ref[...] if d == 0 else triu_ref[...]
            tri_r = triu_ref[...] if d == 0 else tril_ref[...]
            cum = _dot(tri_c, f_hi) + _dot(tri_c, f_lo)
            r = g - pltpu.roll(cum, LANES - ML_HEADS, 1)
            gi0 = 2 * ML_HEADS * d
            rb = jnp.concatenate([jnp.broadcast_to(r[:, gi0 + h:gi0 + h + 1], (L, L))
                                  for h in range(ML_HEADS)], axis=1)
            g_t = jnp.transpose(g)[0:N_GATES, :]
            ft_hi, ft_lo = _split(_log_sigmoid(g_t))
            tr = jnp.concatenate([tri_r, ones_ll], axis=1)
            ct = _dot(ft_hi, tr) + _dot(ft_lo, tr)
            work.append(dict(d=d, cc=cc, rows=rows, rb=rb, g_t=g_t, ct=ct))
            yield
        for w in work:
            d = w["d"]
            valid = (s_idx <= t_idx) if d == 0 else (s_idx >= t_idx)
            gi0 = 2 * ML_HEADS * d
            gf0 = gi0 + ML_HEADS
            b_row = heads_row(w["ct"], gf0)
            b_last = heads_row(w["ct"], gf0, L)
            i_row = heads_row(w["g_t"], gi0)
            m_prev = m_state[d]
            a_row = b_row + m_prev
            dm = jnp.where(valid, b_row + w["rb"], -jnp.inf)
            mt = jnp.maximum(a_row, jnp.max(dm, axis=0, keepdims=True))
            w["p"] = jnp.exp(dm - mt)
            w["w_inter"] = jnp.exp(a_row - mt)
            w["floor"] = jnp.exp(-mt)
            g_row = b_last - b_row + i_row
            g_max = jnp.concatenate(
                [jnp.broadcast_to(jnp.max(g_row[:, h * L:(h + 1) * L], axis=-1, keepdims=True), (1, L))
                 for h in range(ML_HEADS)], axis=1)
            m_new = jnp.maximum(b_last + m_prev, g_max)
            w["ws"] = jnp.exp(g_row - m_new)
            w["wc"] = jnp.exp(b_last + m_prev - m_new)
            m_state[d] = m_new
            yield
        for w in work:
            rows, cc = w["rows"], w["cc"]
            w["pairs"] = []
            for pr in range(n_pairs):
                lanes = slice(pr * LANES, (pr + 1) * LANES)
                cols = slice(pr * 2 * L, (pr + 1) * 2 * L)
                k_pair = km_s[rows, lanes]
                q_bd = block_diag_rows(qm_s[rows, lanes])
                k_bd = block_diag_rows(k_pair)
                s_t = (_dot_tb(k_pair, q_bd) * w["p"][:, cols]).astype(BF16)
                s_bd = jnp.concatenate(
                    [jnp.concatenate([s_t[:, 0:L], zeros_ll], axis=1),
                     jnp.concatenate([zeros_ll, s_t[:, L:2 * L]], axis=1)], axis=0)
                v_t = jnp.concatenate([vmt_s[cc, 2 * pr], vmt_s[cc, 2 * pr + 1]], axis=1)
                intra = _dot(v_t, s_bd)
                wv = (v_t.astype(F32) * w["ws"][:, cols]).astype(BF16)
                w["pairs"].append((q_bd, intra, _dot(wv, k_bd)))
            yield
        for w in work:
            d, cc = w["d"], w["cc"]
            for pr, (q_bd, intra, update) in enumerate(w["pairs"]):
                cols = slice(pr * 2 * L, (pr + 1) * 2 * L)
                state = c_state[d][pr]
                tot = intra + w["w_inter"][:, cols] * _dot_tb(state.astype(BF16), q_bd)
                h_t = tot[0:ML_DV, :] / jnp.maximum(jnp.abs(tot[ML_DV:ML_DV + 1, :]), w["floor"][:, cols])
                for e in range(2):
                    hr = slice((2 * pr + e) * ML_DV, (2 * pr + e + 1) * ML_DV)
                    hmt_s[cc, hr, :] += h_t[:, e * L:(e + 1) * L]
                wc = w["wc"]
                wc_pair = jnp.where(lane_row < ML_DK, wc[:, 2 * pr * L:(2 * pr + 1) * L],
                                    wc[:, (2 * pr + 1) * L:(2 * pr + 2) * L])
                c_state[d][pr] = wc_pair * state + update
            yield
        for d in m_state:
            for h in range(ML_HEADS):
                m_s[d * ML_HEADS + h:d * ML_HEADS + h + 1, :] = m_state[d][:, h * L:(h + 1) * L]
            for pr in range(n_pairs):
                c_s[d * n_pairs + pr] = c_state[d][pr]

    def scan_streams(cs):
        return [(d, c if d == 0 else n_chunks - 1 - c) for c in cs for d in range(2)]

    if lat:
        def body2(n, carry):
            _run(_interleave([(attend(n, pl.multiple_of(n * tq, tq), 2), 1),
                              (mlstm_streams(scan_streams([n])), 1)]))
            return carry
        lax.fori_loop(0, n_chunks, body2, 0)
    else:
        def all_tiles():
            for n in range(T // tq):
                yield from attend(n, n * tq, 2)
        yield from _interleave([(all_tiles(), 1), (mlstm_streams(scan_streams(range(n_chunks))), 3)])

    if not lat:
        for d in range(2):
            for pr in range(n_pairs):
                c_t = jnp.transpose(c_s[d * n_pairs + pr, 0:ML_DV, :])
                for e in range(2):
                    c_o[d, 2 * pr + e] = c_t[e * ML_DK:(e + 1) * ML_DK, :]
            for h in range(ML_HEADS):
                idx = d * n_pairs + h // 2
                hl = slice((h % 2) * ML_DK, (h % 2 + 1) * ML_DK)
                n_o[d, h:h + 1, :] = c_s[idx, ML_DV:ML_DV + 1, hl]
                m_o[d:d + 1, h:h + 1] = m_s[d * ML_HEADS + h:d * ML_HEADS + h + 1, 0:1]

    def phase3_merge(i, out):
        rows = row_slice(i, 0, row_block)
        u = u_s[rows, :]

        def proj(c0, width):
            return _dot(u, w_ref[:, c0:c0 + width])

        hm = jnp.concatenate(
            [jnp.concatenate([jnp.transpose(hmt_s[i * chunks_per_block + j, h * ML_DV:(h + 1) * ML_DV, :])
                              for h in range(ML_HEADS)], axis=1)
             for j in range(chunks_per_block)], axis=0)
        ys = (ya_s[rows, :].astype(F32), om_s[rows, :].astype(F32) * hm, yw_s[rows, :].astype(F32))
        yield
        merged = None
        for b, (yb, zc) in enumerate(zip(ys, (C_ZA, C_ZM, C_ZW))):
            z = proj(zc, BRANCH_W)
            t = (yb * (z * _sigmoid(z))).astype(BF16)
            yield
            pb = _sigmoid(proj(C_GMERGE + b * D_MODEL, D_MODEL)) * _dot(t, wb_ref[b])
            merged = pb if merged is None else merged + pb
            yield
        out.append(_dot(merged.astype(BF16), wo_ref[...]))
        yield

    def phase3_norm(i, o):
        for j in range(0, row_block, norm_rows):
            piece = row_slice(i, j, norm_rows)
            hres = ALPHA * x_ref[piece, :] + gate * o[j:j + norm_rows, :]
            y_ref[piece, :] = _layer_norm(hres) * lng_ref[...] + lnb_ref[...]
            yield

    def phase3(i):
        box = []
        yield from phase3_merge(i, box)
        yield from phase3_norm(i, box[0])

    if not lat:
        box = []
        yield from _interleave([(phase3_merge(0, box), 1), (phase1_norm(0, x_next_ref, u_next_s), 1)])
        yield from phase3_norm(0, box[0])
    else:
        def body4(i, carry):
            _run(phase3(i))
            return carry
        lax.fori_loop(0, n_rows, body4, 0)


def _const_spec(shape):
    nd = len(shape)
    return pl.BlockSpec(shape, lambda b: (0,) * nd, pipeline_mode=pl.Buffered(1))


def _layer_spec(shape, layer):
    nd = len(shape) - 1
    return pl.BlockSpec((None,) + tuple(shape[1:]), lambda b: (layer,) + (0,) * nd,
                        pipeline_mode=pl.Buffered(1))


def _layer_call(lat, layer, x, mod, weights, consts, extra):
    B, T, _ = x.shape
    S = T + PAST_LEN if lat else T
    n_chunks = T // CHUNK

    seq_spec = pl.BlockSpec((None, T, D_MODEL), lambda b: (b, 0, 0))
    out_seq_spec = seq_spec
    if lat:
        mod_spec = pl.BlockSpec((None, None, 3, D_MODEL), lambda b: (layer, b + 1, 0, 0))
    else:
        mod_spec = pl.BlockSpec((None, None, 3, D_MODEL), lambda b: (layer, 0, 0, 0))
    *vmem_consts, sink = consts
    in_specs = [seq_spec, mod_spec]
    in_specs += [_layer_spec(a.shape, layer) for a in weights]
    in_specs += [_const_spec(a.shape) for a in vmem_consts]
    in_specs.append(pl.BlockSpec(memory_space=pltpu.SMEM))
    args = [x, mod, *weights, *vmem_consts, sink]

    if lat:
        cos, sa, sb, cka, cva, ckw, cvw, sc, sn, sm = extra
        in_specs += [_const_spec(cos.shape)] * 3
        kv_spec = pl.BlockSpec((None, None, N_KV, HEAD_DIM, PAST_LEN), lambda b: (b, layer, 0, 0, 0))
        in_specs += [kv_spec] * 4
        in_specs += [
            pl.BlockSpec((None, None, 2, ML_HEADS, ML_DK, ML_DV), lambda b: (b, layer, 0, 0, 0, 0)),
            pl.BlockSpec((None, None, 2, ML_HEADS, ML_DK), lambda b: (b, layer, 0, 0, 0)),
            pl.BlockSpec((None, None, 2, ML_HEADS), lambda b: (b, layer, 0, 0)),
        ]
        args += [cos, sa, sb, cka, cva, ckw, cvw, sc, sn, sm]
        out_specs = out_seq_spec
        out_shape = jax.ShapeDtypeStruct((B, T, D_MODEL), F32)
        aliases = {}
    else:
        kv_shape = (B, DEPTH, N_KV, HEAD_DIM, T)
        c_shape = (B, DEPTH, 2, ML_HEADS, ML_DK, ML_DV)
        if layer == 0:
            aliases = {}
            ld = DEPTH
            layer_idx = 0
        else:
            first_acc = len(args)
            in_specs += [pl.BlockSpec(memory_space=pl.ANY)] * len(extra)
            args += list(extra)
            aliases = {first_acc + i: 1 + i for i in range(len(extra))}
            assert len(extra) == CTX_ALIASED
            ld = None
            layer_idx = layer
        in_specs.append(pl.BlockSpec((None, T, D_MODEL), lambda b: (jnp.minimum(b + 1, B - 1), 0, 0)))
        args.append(x)
        kv_out = pl.BlockSpec((None, ld, N_KV, HEAD_DIM, T), lambda b: (b, layer_idx, 0, 0, 0))
        out_specs = [
            seq_spec, kv_out, kv_out, kv_out, kv_out,
            pl.BlockSpec((None, ld, 2, ML_HEADS, ML_DK, ML_DV), lambda b: (b, layer_idx, 0, 0, 0, 0)),
            pl.BlockSpec((None, 2, ML_HEADS, ML_DK), lambda b: (b, 0, 0, 0)),
            pl.BlockSpec((None, 2, ML_HEADS), lambda b: (b, 0, 0)),
        ]
        out_shape = [jax.ShapeDtypeStruct((B, T, D_MODEL), F32)]
        out_shape += [jax.ShapeDtypeStruct(kv_shape, F32)] * 4 + [jax.ShapeDtypeStruct(c_shape, F32)]
        out_shape += [
            jax.ShapeDtypeStruct((B, 2, ML_HEADS, ML_DK), F32),
            jax.ShapeDtypeStruct((B, 2, ML_HEADS), F32),
        ]

    scratch = [
        ((T, D_MODEL) if lat else (2, T, D_MODEL), BF16),
        ((N_HEADS, T, HEAD_DIM), BF16),
        ((N_HEADS, T, HEAD_DIM), BF16),
        ((N_KV, S, HEAD_DIM), BF16),
        ((S // CHUNK, N_KV, VT_ROWS, CHUNK), BF16),
        ((N_KV, T, HEAD_DIM), BF16),
        ((n_chunks, N_KV, VT_ROWS, CHUNK), BF16),
        ((T, ML_HEADS * ML_DK), BF16),
        ((T, ML_HEADS * ML_DK), BF16),
        ((n_chunks, ML_HEADS, ST_ROWS, CHUNK), BF16),
        ((T, ML_HEADS * ML_DV), BF16),
        ((T, LANES), F32),
        ((T, BRANCH_W), BF16),
        ((T, BRANCH_W), BF16),
        ((n_chunks, ML_HEADS * ML_DV, CHUNK), F32),
        ((ML_HEADS, ST_ROWS, 2 * ML_DK), F32),
        ((2 * ML_HEADS, LANES), F32),
    ]
    if lat:
        scratch += [((N_KV, PAST_LEN, HEAD_DIM), BF16),
                    ((N_KV, VT_ROWS, PAST_LEN), BF16)]
    scratch = [pltpu.VMEM(shape, dtype) for shape, dtype in scratch]

    return pl.pallas_call(
        functools.partial(_layer_kernel, lat, T, layer),
        grid=(B,),
        in_specs=in_specs,
        out_specs=out_specs,
        out_shape=out_shape,
        scratch_shapes=scratch,
        input_output_aliases=aliases,
        compiler_params=pltpu.CompilerParams(
            dimension_semantics=("arbitrary",), vmem_limit_bytes=VMEM_LIMIT_BYTES),
        name=("latent_layer" if lat else "context_layer") + str(layer),
    )(*args)


def _rope_tables(T):
    rows = T // GRID_W
    row = jnp.repeat(jnp.arange(rows, dtype=F32), GRID_W)
    col = jnp.tile(jnp.arange(GRID_W, dtype=F32), rows)
    inv = ROPE_THETA ** (-jnp.arange(0, AXIS_DIM, 2, dtype=F32) / AXIS_DIM)
    ar = row[:, None] * inv
    ac = col[:, None] * inv
    ang = jnp.concatenate([ar, ar, ac, ac], axis=-1)
    ang = jnp.concatenate([ang, ang], axis=-1)
    first_half = (jnp.arange(LANES) % AXIS_DIM) < (AXIS_DIM // 2)
    cos = jnp.cos(ang)
    sin = jnp.sin(ang)
    sa = jnp.where(first_half[None, :], -sin, 0.0)
    sb = jnp.where(first_half[None, :], 0.0, sin)
    return cos, sa, sb


def _constants(sink_logit):
    grp = np.arange(N_HEADS * HEAD_DIM) // HEAD_DIM
    bd = jnp.asarray(grp[:, None] == grp[None, :], dtype=BF16)
    ti = np.arange(CHUNK)
    tril = jnp.asarray(ti[None, :] <= ti[:, None], dtype=BF16)
    triu = jnp.asarray(ti[None, :] >= ti[:, None], dtype=BF16)
    return bd, tril, triu, sink_logit.reshape(-1)


def kernel(x_prompt, x_sample, cache_attn_k, cache_attn_v, cache_win_k, cache_win_v, state_mlstm_C,
           state_mlstm_n, state_mlstm_m, c, c_ctx, w_mod, b_mod, w_in, qk_gain, sink_logit,
           mlstm_gate_bias, w_branch, w_out, ln_gain, ln_bias):
    dec_b = x_sample.shape[0]
    assert dec_b + 1 <= 8
    cond = jnp.concatenate([c_ctx[None, :], c, jnp.zeros((8 - 1 - dec_b, D_MODEL), F32)], axis=0)
    mod = _modulation(cond, w_mod, b_mod).reshape(DEPTH, 8, 3, D_MODEL)

    consts = _constants(sink_logit)
    cos, sa, sb = _rope_tables(x_sample.shape[1])
    lane_pad = jnp.zeros((DEPTH, 1, LANES - N_GATES), F32)
    weights = (
        _prep_in_weight(jnp.swapaxes(w_in, 1, 2)), w_branch.astype(BF16), w_out.astype(BF16),
        jnp.tile(qk_gain[:, 0:1, :], (1, 1, N_HEADS)),
        jnp.tile(qk_gain[:, 1:2, :], (1, 1, N_KV)),
        jnp.concatenate([mlstm_gate_bias.reshape(DEPTH, 1, N_GATES), lane_pad], axis=2),
        ln_gain[:, None, :], ln_bias[:, None, :],
    )

    ctx = None
    cached_t = [jnp.swapaxes(a, -1, -2)
                for a in (cache_attn_k, cache_attn_v, cache_win_k, cache_win_v, state_mlstm_C)]
    xp, xs = x_prompt, x_sample
    small = []
    for l in range(DEPTH):
        outs = _layer_call(False, l, xp, mod, weights, consts, ctx)
        xp = outs[0]
        ctx = tuple(outs[1:6])
        small.append(outs[6:])
        extra = (cos, sa, sb, *cached_t, state_mlstm_n, state_mlstm_m)
        xs = _layer_call(True, l, xs, mod, weights, consts, extra)
    new_n, new_m = [jnp.stack([small[l][i] for l in range(DEPTH)], axis=1) for i in range(2)]
    return (xp, xs, *[jnp.swapaxes(a, -1, -2) for a in ctx], new_n, new_m)
```

```python
import functools

import jax
import jax.numpy as jnp
import numpy as np
from jax import lax
from jax.experimental import pallas as pl
from jax.experimental.pallas import tpu as pltpu

F32 = jnp.float32
BF16 = jnp.bfloat16

D_MODEL = 1024
DEPTH = 2
PAST_LEN = 256
GRID_W = 64
HEAD_DIM = 64
N_HEADS = 8
N_KV = 2
N_GROUP = N_HEADS // N_KV
WINDOW = 128
ML_HEADS = 4
ML_DK = 64
ML_DV = 128
CHUNK = 128
BRANCH_W = 512
ROPE_THETA = 10000.0
AXIS_DIM = HEAD_DIM // 2
LN_EPS = 1e-6
RMS_EPS = 1e-6
ALPHA = (2.0 * DEPTH) ** 0.25
LOG2E = float(np.log2(np.e))
LANES = 128
BF16_ROWS = 16
ROW_BLOCK = 256
CHAINS_IN_FLIGHT = 4
CTX_ALIASED = 5
N_SHARED = 14
VMEM_LIMIT_BYTES = 60 * 1024 * 1024
LATENT_VMEM_LIMIT_BYTES = 62 * 1024 * 1024
VT_ROWS = HEAD_DIM + BF16_ROWS
ST_ROWS = ML_DV + BF16_ROWS

_SIZES = (512, 128, 128, 512, 256, 256, 512, 512, 128, 512, 512, 128, 128, 512, 3072)
_OFF = [int(v) for v in np.concatenate([[0], np.cumsum(_SIZES)])]
(C_QA, C_KA, C_VA, C_ZA, C_QM, C_KM, C_VM, C_OM, C_GT, C_ZM, C_QW, C_KW, C_VW, C_ZW, C_GMERGE) = _OFF[:-1]
W_COLS = _OFF[-1]
GATE_COL = 2816
N_GATES = 4 * ML_HEADS


def _dot(a, b):
    return jnp.dot(a, b, preferred_element_type=F32)


def _dot_tb(a, b):
    return lax.dot_general(a, b, (((1,), (1,)), ((), ())), preferred_element_type=F32)


def _split(a):
    hi = a.astype(BF16)
    lo = (a - hi.astype(F32)).astype(BF16)
    return hi, lo


def _layer_norm(x):
    mu = jnp.mean(x, axis=-1, keepdims=True)
    xc = x - mu
    var = jnp.mean(xc * xc, axis=-1, keepdims=True)
    return xc * lax.rsqrt(var + LN_EPS)


def _log_sigmoid(x):
    return jnp.minimum(x, 0.0) - jnp.log(1.0 + jnp.exp(-jnp.abs(x)))


def _sigmoid(x):
    return jax.nn.sigmoid(x)


def _mod_kernel(c_ref, w_ref, b_ref, o_ref):
    c = c_ref[...]
    s = c * _sigmoid(c)
    s_hi, s_lo = _split(s)
    w = w_ref[...]
    w_hi, w_lo = _split(w)
    o_ref[...] = _dot(s_hi, w_hi) + _dot(s_lo, w_hi) + _dot(s_hi, w_lo) + b_ref[...]


def _modulation(cond, w_mod, b_mod):
    rows = cond.shape[0]
    tn = 1024
    return pl.pallas_call(
        _mod_kernel,
        grid=(DEPTH, 3 * D_MODEL // tn),
        in_specs=[
            pl.BlockSpec((rows, D_MODEL), lambda l, j: (0, 0)),
            pl.BlockSpec((None, D_MODEL, tn), lambda l, j: (l, 0, j)),
            pl.BlockSpec((None, 1, tn), lambda l, j: (l, 0, j)),
        ],
        out_specs=pl.BlockSpec((None, rows, tn), lambda l, j: (l, 0, j)),
        out_shape=jax.ShapeDtypeStruct((DEPTH, rows, 3 * D_MODEL), F32),
        compiler_params=pltpu.CompilerParams(dimension_semantics=("arbitrary", "arbitrary")),
        name="adaln_modulation",
    )(cond, w_mod, b_mod.reshape(DEPTH, 1, 3 * D_MODEL))


PREP_K = 256


def _prep_kernel(wt_ref, o_ref):
    lane = lax.broadcasted_iota(jnp.int32, (PREP_K, LANES), 1)
    for c0 in range(0, W_COLS, LANES):
        r0 = c0 if c0 <= GATE_COL else c0 - (LANES - N_GATES)
        slab = jnp.transpose(wt_ref[r0:r0 + LANES, :])
        if c0 == GATE_COL:
            slab = jnp.where(lane < N_GATES, slab, 0.0)
        o_ref[:, c0:c0 + LANES] = slab.astype(BF16)


def _prep_in_weight(w_in_t):
    n_in = w_in_t.shape[1]
    return pl.pallas_call(
        _prep_kernel,
        grid=(DEPTH, D_MODEL // PREP_K),
        in_specs=[pl.BlockSpec((None, n_in, PREP_K), lambda l, i: (l, 0, i))],
        out_specs=pl.BlockSpec((None, PREP_K, W_COLS), lambda l, i: (l, i, 0)),
        out_shape=jax.ShapeDtypeStruct((DEPTH, D_MODEL, W_COLS), BF16),
        compiler_params=pltpu.CompilerParams(
            dimension_semantics=("arbitrary", "arbitrary"), vmem_limit_bytes=VMEM_LIMIT_BYTES),
        name="in_weight_prep",
    )(w_in_t)


def _rope(x, cos, sa, sb):
    return x * cos + pltpu.roll(x, LANES - AXIS_DIM // 2, 1) * sa + pltpu.roll(x, AXIS_DIM // 2, 1) * sb


def _group_rms(x, bd, gain):
    ms = _dot((x * x).astype(BF16), bd) * (1.0 / HEAD_DIM)
    return x * lax.rsqrt(ms + RMS_EPS) * gain


def _ones_row_tile():
    r = lax.broadcasted_iota(jnp.int32, (BF16_ROWS, LANES), 0)
    return jnp.where(r == 0, 1.0, 0.0).astype(BF16)


def _attend_chains(chains, results):
    state = []
    for qs, tiles, sink_row in chains:
        m_cols = qs.shape[0]
        if sink_row is None:
            m = jnp.full((1, m_cols), -jnp.inf, F32)
            acc = jnp.zeros((VT_ROWS, m_cols), F32)
        else:
            m = sink_row
            r = lax.broadcasted_iota(jnp.int32, (VT_ROWS, m_cols), 0)
            acc = jnp.where(r == HEAD_DIM, 1.0, 0.0)
        cur = tiles[0]()
        state.append([m, acc, cur, _dot_tb(cur[0], qs)])
    yield
    for t in range(max(len(tiles) for _, tiles, _ in chains)):
        for st, (qs, tiles, _) in zip(state, chains):
            if t >= len(tiles):
                continue
            m, acc, (_, v_t, valid), s = st
            if valid is not None:
                s = jnp.where(valid, s, -jnp.inf)
            if t + 1 < len(tiles):
                nxt = tiles[t + 1]()
                st[2], st[3] = nxt, _dot_tb(nxt[0], qs)
            m_new = jnp.maximum(m, jnp.max(s, axis=0, keepdims=True))
            p = jnp.exp2(s - m_new).astype(BF16)
            st[1] = acc * jnp.exp2(m - m_new) + _dot(v_t, p)
            st[0] = m_new
        yield
    results.extend(st[1][0:HEAD_DIM, :] / st[1][HEAD_DIM:HEAD_DIM + 1, :] for st in state)


def _interleave(weighted):
    live = [[g, n] for g, n in weighted]
    while live:
        for item in list(live):
            for _ in range(item[1]):
                try:
                    next(item[0])
                except StopIteration:
                    live.remove(item)
                    break
        yield


def _run(gen):
    for _ in gen:
        pass


def _layer_kernel(lat, T, layer, *refs):
    _run(_sequence(lat, T, layer, *refs))


def _sequence(lat, T, layer, *refs):
    refs = list(refs)
    x_ref, mod_ref, w_ref, wb_ref, wo_ref = refs[:5]
    g0_ref, g1_ref, gb_ref, lng_ref, lnb_ref = refs[5:10]
    bd_ref, tril_ref, triu_ref, sink_ref = refs[10:N_SHARED]
    pos = N_SHARED
    if lat:
        cos_ref, sa_ref, sb_ref = refs[pos:pos + 3]
        cka_ref, cva_ref, ckw_ref, cvw_ref, c0_ref, n0_ref, m0_ref = refs[pos + 3:pos + 10]
        pos += 10
        y_ref = refs[pos]
        pos += 1
    else:
        if layer > 0:
            pos += CTX_ALIASED
        x_next_ref = refs[pos]
        pos += 1
        y_ref, ka_o, va_o, kw_o, vw_o, c_o, n_o, m_o = refs[pos:pos + 8]
        pos += 8
        if layer == 0:
            for full in (ka_o, va_o, kw_o, vw_o, c_o):
                full[1:] = jnp.zeros((DEPTH - 1,) + full.shape[1:], F32)
            ka_o, va_o, kw_o, vw_o, c_o = (full.at[0] for full in (ka_o, va_o, kw_o, vw_o, c_o))
    (u_s, qa_s, qw_s, ka_s, vat_s, kw_s, vwt_s, qm_s, km_s, vmt_s, om_s, g_s,
     ya_s, yw_s, hmt_s, c_s, m_s) = refs[pos:pos + 17]
    pos += 17
    if not lat:
        step = pl.program_id(0)
        u_next_s = u_s.at[(step + 1) % 2]
        u_s = u_s.at[step % 2]
    if lat:
        kwc_s, vwct_s = refs[pos:pos + 2]

    row_block = min(T, ROW_BLOCK)
    n_rows = T // row_block
    n_chunks = T // CHUNK
    chunks_per_block = row_block // CHUNK
    shift = mod_ref[0:1, :]
    scale = mod_ref[1:2, :]
    gate = mod_ref[2:3, :]
    ones_tile = _ones_row_tile()

    norm_rows = 64

    def row_slice(i, j, n):
        r0 = i * row_block + j
        return pl.ds(r0 if isinstance(r0, int) else pl.multiple_of(r0, n), n)

    def phase1_norm(i, src=x_ref, dst=u_s):
        for j in range(0, row_block, norm_rows):
            piece = row_slice(i, j, norm_rows)
            dst[piece, :] = (_layer_norm(src[piece, :]) * (1.0 + scale) + shift).astype(BF16)
            yield

    def phase1_proj(i):
        R = row_block
        rows = row_slice(i, 0, R)
        u = u_s[rows, :]

        def proj(c0, width):
            return _dot(u, w_ref[:, c0:c0 + width])

        if lat:
            cos = cos_ref[rows, :]
            sa = sa_ref[rows, :]
            sb = sb_ref[rows, :]

        def rope_wide(v):
            if not lat:
                return v
            slabs = [_rope(v[:, j:j + LANES], cos, sa, sb) for j in range(0, v.shape[1], LANES)]
            return slabs[0] if len(slabs) == 1 else jnp.concatenate(slabs, axis=1)

        qscale = LOG2E * HEAD_DIM ** -0.5
        qa = rope_wide(_group_rms(proj(C_QA, 512), bd_ref[...], g0_ref[...])) * qscale
        for h in range(N_HEADS):
            qa_s[h, rows, :] = qa[:, h * HEAD_DIM:(h + 1) * HEAD_DIM].astype(BF16)
        yield
        ka_n = _group_rms(proj(C_KA, LANES), bd_ref[0:LANES, 0:LANES], g1_ref[...])
        ka = rope_wide(ka_n)
        va = proj(C_VA, LANES)
        qw = rope_wide(proj(C_QW, 512)) * qscale
        for h in range(N_HEADS):
            qw_s[h, rows, :] = qw[:, h * HEAD_DIM:(h + 1) * HEAD_DIM].astype(BF16)
        yield
        kw_raw = proj(C_KW, LANES)
        kw = rope_wide(kw_raw)
        vw = proj(C_VW, LANES)
        for kv in range(N_KV):
            sl = slice(kv * HEAD_DIM, (kv + 1) * HEAD_DIM)
            ka_s[kv, rows, :] = ka[:, sl].astype(BF16)
            kw_s[kv, rows, :] = kw[:, sl].astype(BF16)
        yield
        qm_s[rows, :] = proj(C_QM, 256).astype(BF16)
        km_s[rows, :] = (proj(C_KM, 256) * (ML_DK ** -0.5)).astype(BF16)
        yield
        vm = proj(C_VM, 512)
        yield
        om_s[rows, :] = _sigmoid(proj(C_OM, 512)).astype(BF16)
        g_s[rows, :] = proj(C_GT, LANES) + gb_ref[...]
        yield
        for j in range(chunks_per_block):
            cj = i * chunks_per_block + j
            cr = slice(j * CHUNK, (j + 1) * CHUNK)
            va_t = jnp.transpose(va[cr, :])
            vw_t = jnp.transpose(vw[cr, :])
            for kv in range(N_KV):
                hd = slice(kv * HEAD_DIM, (kv + 1) * HEAD_DIM)
                vat_s[cj, kv, 0:HEAD_DIM, :] = va_t[hd, :].astype(BF16)
                vat_s[cj, kv, HEAD_DIM:VT_ROWS, :] = ones_tile
                vwt_s[cj, kv, 0:HEAD_DIM, :] = vw_t[hd, :].astype(BF16)
                vwt_s[cj, kv, HEAD_DIM:VT_ROWS, :] = ones_tile
            if not lat:
                tc = slice(i * row_block + j * CHUNK, i * row_block + (j + 1) * CHUNK)
                ka_t = jnp.transpose(ka_n[cr, :])
                kw_t = jnp.transpose(kw_raw[cr, :])
                for kv in range(N_KV):
                    hd = slice(kv * HEAD_DIM, (kv + 1) * HEAD_DIM)
                    ka_o[kv, :, tc] = ka_t[hd, :]
                    va_o[kv, :, tc] = va_t[hd, :]
                    kw_o[kv, :, tc] = kw_t[hd, :]
                    vw_o[kv, :, tc] = vw_t[hd, :]
            for h in range(ML_HEADS):
                vmt_s[cj, h, 0:ML_DV, :] = jnp.transpose(vm[cr, h * ML_DV:(h + 1) * ML_DV]).astype(BF16)
                vmt_s[cj, h, ML_DV:ST_ROWS, :] = ones_tile
            yield

    if not lat:
        @pl.when(step == 0)
        def _():
            _run(phase1_norm(0))
        yield from phase1_proj(0)
    else:
        def body1(i, carry):
            _run(phase1_norm(i))
            _run(phase1_proj(i))
            return carry
        lax.fori_loop(0, n_rows, body1, 0)

    if lat:
        for j in range(PAST_LEN // CHUNK):
            cols = slice(j * CHUNK, (j + 1) * CHUNK)
            ka_c = jnp.transpose(jnp.concatenate([cka_ref[kv, :, cols] for kv in range(N_KV)], axis=0))
            kw_c = jnp.transpose(jnp.concatenate([ckw_ref[kv, :, cols] for kv in range(N_KV)], axis=0))
            for kv in range(N_KV):
                hd = slice(kv * HEAD_DIM, (kv + 1) * HEAD_DIM)
                ka_s[kv, T + j * CHUNK:T + (j + 1) * CHUNK, :] = ka_c[:, hd].astype(BF16)
                kwc_s[kv, cols, :] = kw_c[:, hd].astype(BF16)
                vat_s[n_chunks + j, kv, 0:HEAD_DIM, :] = cva_ref[kv, :, cols].astype(BF16)
                vat_s[n_chunks + j, kv, HEAD_DIM:VT_ROWS, :] = ones_tile
                vwct_s[kv, 0:HEAD_DIM, cols] = cvw_ref[kv, :, cols].astype(BF16)
                vwct_s[kv, HEAD_DIM:VT_ROWS, cols] = ones_tile

    tq = CHUNK
    m_cols = N_GROUP * tq

    def sink_row(kv):
        col = lax.broadcasted_iota(jnp.int32, (1, m_cols), 1)
        row = jnp.full((1, m_cols), sink_ref[layer * N_HEADS + kv * N_GROUP], F32)
        for g in range(1, N_GROUP):
            row = jnp.where(col >= g * tq, sink_ref[layer * N_HEADS + kv * N_GROUP + g], row)
        return row * LOG2E

    def load_q(q_s, kv, q0):
        return q_s[kv * N_GROUP:(kv + 1) * N_GROUP, pl.ds(q0, tq), :].reshape(m_cols, HEAD_DIM)

    def store_heads(dst, q0, kv, o_t):
        for p in range(N_GROUP // 2):
            blk = jnp.concatenate([o_t[:, (2 * p) * tq:(2 * p + 1) * tq],
                                   o_t[:, (2 * p + 1) * tq:(2 * p + 2) * tq]], axis=0)
            c0 = (kv * N_GROUP + 2 * p) * HEAD_DIM
            dst[pl.ds(q0, tq), c0:c0 + LANES] = jnp.transpose(blk).astype(dst.dtype)

    def key_tiles(k_ref, vt_ref, kv, n_key_chunks, chunks_per_tile):
        def tile(c0):
            def load():
                k = k_ref[kv, c0 * CHUNK:(c0 + chunks_per_tile) * CHUNK, :]
                v_t = [vt_ref[c0 + j, kv] for j in range(chunks_per_tile)]
                return k, (v_t[0] if len(v_t) == 1 else jnp.concatenate(v_t, axis=1)), None
            return load
        return [tile(c0) for c0 in range(0, n_key_chunks, chunks_per_tile)]

    def band_tiles(kv, n, q0):
        j0 = jnp.clip(n - 1, 0, n_chunks - 3)
        w0 = pl.multiple_of(j0 * CHUNK, CHUNK)

        def band(c0, nc):
            def load():
                rows = nc * CHUNK
                kpos = w0 + c0 * CHUNK + lax.broadcasted_iota(jnp.int32, (rows, m_cols), 0)
                qpos = q0 + (lax.broadcasted_iota(jnp.int32, (rows, m_cols), 1) & (tq - 1))
                v_t = [vwt_s[j0 + c0 + j, kv] for j in range(nc)]
                return (kw_s[kv, pl.ds(w0 + c0 * CHUNK, rows), :],
                        v_t[0] if nc == 1 else jnp.concatenate(v_t, axis=1),
                        jnp.abs(kpos - qpos) <= WINDOW)
            return load
        return [band(0, 2), band(2, 1), lambda: (kwc_s[kv], vwct_s[kv], None)]

    def attend(n, q0, chunks_per_tile):
        chains, dsts = [], []
        n_key_chunks = ka_s.shape[1] // CHUNK
        for kv in range(N_KV):
            chains.append((load_q(qa_s, kv, q0), key_tiles(ka_s, vat_s, kv, n_key_chunks, chunks_per_tile), None))
            dsts.append((ya_s, kv))
        for kv in range(N_KV):
            tiles = band_tiles(kv, n, q0) if lat else key_tiles(kw_s, vwt_s, kv, n_chunks, chunks_per_tile)
            chains.append((load_q(qw_s, kv, q0), tiles, sink_row(kv)))
            dsts.append((yw_s, kv))
        for g0 in range(0, len(chains), CHAINS_IN_FLIGHT):
            results = []
            yield from _attend_chains(chains[g0:g0 + CHAINS_IN_FLIGHT], results)
            for (dst, kv), o_t in zip(dsts[g0:g0 + CHAINS_IN_FLIGHT], results):
                store_heads(dst, q0, kv, o_t)

    n_pairs = ML_HEADS // 2
    hmt_s[...] = jnp.zeros_like(hmt_s)
    for d in range(2):
        if lat:
            for pr in range(n_pairs):
                c_s[d * n_pairs + pr, 0:ML_DV, :] = jnp.transpose(
                    jnp.concatenate([c0_ref[d, 2 * pr], c0_ref[d, 2 * pr + 1]], axis=0))
        for h in range(ML_HEADS):
            idx = d * n_pairs + h // 2
            hl = slice((h % 2) * ML_DK, (h % 2 + 1) * ML_DK)
            c_s[idx, ML_DV:ST_ROWS, hl] = jnp.zeros((BF16_ROWS, ML_DK), F32)
            if lat:
                c_s[idx, ML_DV:ML_DV + 1, hl] = n0_ref[d, h:h + 1, :]
                m_s[d * ML_HEADS + h:d * ML_HEADS + h + 1, :] = jnp.broadcast_to(
                    m0_ref[d:d + 1, h:h + 1], (1, LANES))
            else:
                c_s[idx, 0:ML_DV, hl] = jnp.zeros((ML_DV, ML_DK), F32)
                m_s[d * ML_HEADS + h:d * ML_HEADS + h + 1, :] = jnp.zeros((1, LANES), F32)

    L = CHUNK
    s_idx = lax.broadcasted_iota(jnp.int32, (L, ML_HEADS * L), 0)
    t_idx = lax.broadcasted_iota(jnp.int32, (L, ML_HEADS * L), 1) & (L - 1)
    lane_row = lax.broadcasted_iota(jnp.int32, (1, LANES), 1)
    low_half = lax.broadcasted_iota(jnp.int32, (L, LANES), 1) < ML_DK
    ones_ll = jnp.ones((L, L), BF16)
    zeros_ll = jnp.zeros((L, L), BF16)

    def heads_row(src, r0, c0=0):
        return jnp.concatenate([src[r0 + h:r0 + h + 1, c0:c0 + L] for h in range(ML_HEADS)], axis=1)

    def block_diag_rows(x):
        zero = jnp.zeros_like(x)
        return jnp.concatenate([jnp.where(low_half, x, zero), jnp.where(low_half, zero, x)], axis=0)

    def mlstm_streams(streams):
        m_state = {d: heads_row(m_s, d * ML_HEADS) for d in sorted({d for d, _ in streams})}
        c_state = {d: [c_s[d * n_pairs + pr] for pr in range(n_pairs)] for d in m_state}
        work = []
        for d, cc in streams:
            r0 = cc * L if isinstance(cc, int) else pl.multiple_of(cc * L, L)
            rows = pl.ds(r0, L)
            g = g_s[rows, :]
            f_hi, f_lo = _split(_log_sigmoid(g))
            tri_c = tril_ref[...] if d == 0 else triu_ref[...]
            tri_r = triu_ref[...] if d == 0 else tril_ref[...]
            cum = _dot(tri_c, f_hi) + _dot(tri_c, f_lo)
            r = g - pltpu.roll(cum, LANES - ML_HEADS, 1)
            gi0 = 2 * ML_HEADS * d
            rb = jnp.concatenate([jnp.broadcast_to(r[:, gi0 + h:gi0 + h + 1], (L, L))
                                  for h in range(ML_HEADS)], axis=1)
            g_t = jnp.transpose(g)[0:N_GATES, :]
            ft_hi, ft_lo = _split(_log_sigmoid(g_t))
            tr = jnp.concatenate([tri_r, ones_ll], axis=1)
            ct = _dot(ft_hi, tr) + _dot(ft_lo, tr)
            work.append(dict(d=d, cc=cc, rows=rows, rb=rb, g_t=g_t, ct=ct))
            yield
        for w in work:
            d = w["d"]
            valid = (s_idx <= t_idx) if d == 0 else (s_idx >= t_idx)
            gi0 = 2 * ML_HEADS * d
            gf0 = gi0 + ML_HEADS
            b_row = heads_row(w["ct"], gf0)
            b_last = heads_row(w["ct"], gf0, L)
            i_row = heads_row(w["g_t"], gi0)
            m_prev = m_state[d]
            a_row = b_row + m_prev
            dm = jnp.where(valid, b_row + w["rb"], -jnp.inf)
            mt = jnp.maximum(a_row, jnp.max(dm, axis=0, keepdims=True))
            w["p"] = jnp.exp(dm - mt)
            w["w_inter"] = jnp.exp(a_row - mt)
            w["floor"] = jnp.exp(-mt)
            g_row = b_last - b_row + i_row
            g_max = jnp.concatenate(
                [jnp.broadcast_to(jnp.max(g_row[:, h * L:(h + 1) * L], axis=-1, keepdims=True), (1, L))
                 for h in range(ML_HEADS)], axis=1)
            m_new = jnp.maximum(b_last + m_prev, g_max)
            w["ws"] = jnp.exp(g_row - m_new)
            w["wc"] = jnp.exp(b_last + m_prev - m_new)
            m_state[d] = m_new
            yield
        for w in work:
            rows, cc = w["rows"], w["cc"]
            w["pairs"] = []
            for pr in range(n_pairs):
                lanes = slice(pr * LANES, (pr + 1) * LANES)
                cols = slice(pr * 2 * L, (pr + 1) * 2 * L)
                k_pair = km_s[rows, lanes]
                q_bd = block_diag_rows(qm_s[rows, lanes])
                k_bd = block_diag_rows(k_pair)
                s_t = (_dot_tb(k_pair, q_bd) * w["p"][:, cols]).astype(BF16)
                s_bd = jnp.concatenate(
                    [jnp.concatenate([s_t[:, 0:L], zeros_ll], axis=1),
                     jnp.concatenate([zeros_ll, s_t[:, L:2 * L]], axis=1)], axis=0)
                v_t = jnp.concatenate([vmt_s[cc, 2 * pr], vmt_s[cc, 2 * pr + 1]], axis=1)
                intra = _dot(v_t, s_bd)
                wv = (v_t.astype(F32) * w["ws"][:, cols]).astype(BF16)
                w["pairs"].append((q_bd, intra, _dot(wv, k_bd)))
            yield
        for w in work:
            d, cc = w["d"], w["cc"]
            for pr, (q_bd, intra, update) in enumerate(w["pairs"]):
                cols = slice(pr * 2 * L, (pr + 1) * 2 * L)
                state = c_state[d][pr]
                tot = intra + w["w_inter"][:, cols] * _dot_tb(state.astype(BF16), q_bd)
                h_t = tot[0:ML_DV, :] / jnp.maximum(jnp.abs(tot[ML_DV:ML_DV + 1, :]), w["floor"][:, cols])
                for e in range(2):
                    hr = slice((2 * pr + e) * ML_DV, (2 * pr + e + 1) * ML_DV)
                    hmt_s[cc, hr, :] += h_t[:, e * L:(e + 1) * L]
                wc = w["wc"]
                wc_pair = jnp.where(lane_row < ML_DK, wc[:, 2 * pr * L:(2 * pr + 1) * L],
                                    wc[:, (2 * pr + 1) * L:(2 * pr + 2) * L])
                c_state[d][pr] = wc_pair * state + update
            yield
        for d in m_state:
            for h in range(ML_HEADS):
                m_s[d * ML_HEADS + h:d * ML_HEADS + h + 1, :] = m_state[d][:, h * L:(h + 1) * L]
            for pr in range(n_pairs):
                c_s[d * n_pairs + pr] = c_state[d][pr]

    def scan_streams(cs):
        return [(d, c if d == 0 else n_chunks - 1 - c) for c in cs for d in range(2)]

    if lat:
        def body2(n, carry):
            _run(_interleave([(attend(n, pl.multiple_of(n * tq, tq), 2), 1),
                              (mlstm_streams(scan_streams([n])), 1)]))
            return carry
        lax.fori_loop(0, n_chunks, body2, 0)
    else:
        def all_tiles():
            for n in range(T // tq):
                yield from attend(n, n * tq, 2)
        yield from _interleave([(all_tiles(), 1), (mlstm_streams(scan_streams(range(n_chunks))), 3)])

    if not lat:
        for d in range(2):
            for pr in range(n_pairs):
                c_t = jnp.transpose(c_s[d * n_pairs + pr, 0:ML_DV, :])
                for e in range(2):
                    c_o[d, 2 * pr + e] = c_t[e * ML_DK:(e + 1) * ML_DK, :]
            for h in range(ML_HEADS):
                idx = d * n_pairs + h // 2
                hl = slice((h % 2) * ML_DK, (h % 2 + 1) * ML_DK)
                n_o[d, h:h + 1, :] = c_s[idx, ML_DV:ML_DV + 1, hl]
                m_o[d:d + 1, h:h + 1] = m_s[d * ML_HEADS + h:d * ML_HEADS + h + 1, 0:1]

    def phase3_merge(i, out):
        rows = row_slice(i, 0, row_block)
        u = u_s[rows, :]

        def proj(c0, width):
            return _dot(u, w_ref[:, c0:c0 + width])

        hm = jnp.concatenate(
            [jnp.concatenate([jnp.transpose(hmt_s[i * chunks_per_block + j, h * ML_DV:(h + 1) * ML_DV, :])
                              for h in range(ML_HEADS)], axis=1)
             for j in range(chunks_per_block)], axis=0)
        ys = (ya_s[rows, :].astype(F32), om_s[rows, :].astype(F32) * hm, yw_s[rows, :].astype(F32))
        yield
        merged = None
        for b, (yb, zc) in enumerate(zip(ys, (C_ZA, C_ZM, C_ZW))):
            z = proj(zc, BRANCH_W)
            t = (yb * (z * _sigmoid(z))).astype(BF16)
            yield
            pb = _sigmoid(proj(C_GMERGE + b * D_MODEL, D_MODEL)) * _dot(t, wb_ref[b])
            merged = pb if merged is None else merged + pb
            yield
        out.append(_dot(merged.astype(BF16), wo_ref[...]))
        yield

    def phase3_norm(i, o):
        for j in range(0, row_block, norm_rows):
            piece = row_slice(i, j, norm_rows)
            hres = ALPHA * x_ref[piece, :] + gate * o[j:j + norm_rows, :]
            y_ref[piece, :] = _layer_norm(hres) * lng_ref[...] + lnb_ref[...]
            yield

    def phase3(i):
        box = []
        yield from phase3_merge(i, box)
        yield from phase3_norm(i, box[0])

    if not lat:
        box = []
        yield from _interleave([(phase3_merge(0, box), 1), (phase1_norm(0, x_next_ref, u_next_s), 1)])
        yield from phase3_norm(0, box[0])
    else:
        def body4(i, carry):
            _run(phase3(i))
            return carry
        lax.fori_loop(0, n_rows, body4, 0)


def _const_spec(shape):
    nd = len(shape)
    return pl.BlockSpec(shape, lambda b: (0,) * nd, pipeline_mode=pl.Buffered(1))


def _layer_spec(shape, layer):
    nd = len(shape) - 1
    return pl.BlockSpec((None,) + tuple(shape[1:]), lambda b: (layer,) + (0,) * nd,
                        pipeline_mode=pl.Buffered(1))


def _layer_call(lat, layer, x, mod, weights, consts, extra):
    B, T, _ = x.shape
    S = T + PAST_LEN if lat else T
    n_chunks = T // CHUNK

    seq_spec = pl.BlockSpec((None, T, D_MODEL), lambda b: (b, 0, 0))
    out_seq_spec = seq_spec
    if lat:
        mod_spec = pl.BlockSpec((None, None, 3, D_MODEL), lambda b: (layer, b + 1, 0, 0))
    else:
        mod_spec = pl.BlockSpec((None, None, 3, D_MODEL), lambda b: (layer, 0, 0, 0))
    *vmem_consts, sink = consts
    in_specs = [seq_spec, mod_spec]
    in_specs += [_layer_spec(a.shape, layer) for a in weights]
    in_specs += [_const_spec(a.shape) for a in vmem_consts]
    in_specs.append(pl.BlockSpec(memory_space=pltpu.SMEM))
    args = [x, mod, *weights, *vmem_consts, sink]

    if lat:
        cos, sa, sb, cka, cva, ckw, cvw, sc, sn, sm = extra
        in_specs += [_const_spec(cos.shape)] * 3
        kv_spec = pl.BlockSpec((None, None, N_KV, HEAD_DIM, PAST_LEN), lambda b: (b, layer, 0, 0, 0))
        in_specs += [kv_spec] * 4
        in_specs += [
            pl.BlockSpec((None, None, 2, ML_HEADS, ML_DK, ML_DV), lambda b: (b, layer, 0, 0, 0, 0)),
            pl.BlockSpec((None, None, 2, ML_HEADS, ML_DK), lambda b: (b, layer, 0, 0, 0)),
            pl.BlockSpec((None, None, 2, ML_HEADS), lambda b: (b, layer, 0, 0)),
        ]
        args += [cos, sa, sb, cka, cva, ckw, cvw, sc, sn, sm]
        out_specs = out_seq_spec
        out_shape = jax.ShapeDtypeStruct((B, T, D_MODEL), F32)
        aliases = {}
    else:
        kv_shape = (B, DEPTH, N_KV, HEAD_DIM, T)
        c_shape = (B, DEPTH, 2, ML_HEADS, ML_DK, ML_DV)
        if layer == 0:
            aliases = {}
            ld = DEPTH
            layer_idx = 0
        else:
            first_acc = len(args)
            in_specs += [pl.BlockSpec(memory_space=pl.ANY)] * len(extra)
            args += list(extra)
            aliases = {first_acc + i: 1 + i for i in range(len(extra))}
            assert len(extra) == CTX_ALIASED
            ld = None
            layer_idx = layer
        in_specs.append(pl.BlockSpec((None, T, D_MODEL), lambda b: (jnp.minimum(b + 1, B - 1), 0, 0)))
        args.append(x)
        kv_out = pl.BlockSpec((None, ld, N_KV, HEAD_DIM, T), lambda b: (b, layer_idx, 0, 0, 0))
        out_specs = [
            seq_spec, kv_out, kv_out, kv_out, kv_out,
            pl.BlockSpec((None, ld, 2, ML_HEADS, ML_DK, ML_DV), lambda b: (b, layer_idx, 0, 0, 0, 0)),
            pl.BlockSpec((None, 2, ML_HEADS, ML_DK), lambda b: (b, 0, 0, 0)),
            pl.BlockSpec((None, 2, ML_HEADS), lambda b: (b, 0, 0)),
        ]
        out_shape = [jax.ShapeDtypeStruct((B, T, D_MODEL), F32)]
        out_shape += [jax.ShapeDtypeStruct(kv_shape, F32)] * 4 + [jax.ShapeDtypeStruct(c_shape, F32)]
        out_shape += [
            jax.ShapeDtypeStruct((B, 2, ML_HEADS, ML_DK), F32),
            jax.ShapeDtypeStruct((B, 2, ML_HEADS), F32),
        ]

    scratch = [
        ((T, D_MODEL) if lat else (2, T, D_MODEL), BF16),
        ((N_HEADS, T, HEAD_DIM), BF16),
        ((N_HEADS, T, HEAD_DIM), BF16),
        ((N_KV, S, HEAD_DIM), BF16),
        ((S // CHUNK, N_KV, VT_ROWS, CHUNK), BF16),
        ((N_KV, T, HEAD_DIM), BF16),
        ((n_chunks, N_KV, VT_ROWS, CHUNK), BF16),
        ((T, ML_HEADS * ML_DK), BF16),
        ((T, ML_HEADS * ML_DK), BF16),
        ((n_chunks, ML_HEADS, ST_ROWS, CHUNK), BF16),
        ((T, ML_HEADS * ML_DV), BF16),
        ((T, LANES), F32),
        ((T, BRANCH_W), BF16),
        ((T, BRANCH_W), BF16),
        ((n_chunks, ML_HEADS * ML_DV, CHUNK), F32),
        ((ML_HEADS, ST_ROWS, 2 * ML_DK), F32),
        ((2 * ML_HEADS, LANES), F32),
    ]
    if lat:
        scratch += [((N_KV, PAST_LEN, HEAD_DIM), BF16),
                    ((N_KV, VT_ROWS, PAST_LEN), BF16)]
    scratch = [pltpu.VMEM(shape, dtype) for shape, dtype in scratch]

    return pl.pallas_call(
        functools.partial(_layer_kernel, lat, T, layer),
        grid=(B,),
        in_specs=in_specs,
        out_specs=out_specs,
        out_shape=out_shape,
        scratch_shapes=scratch,
        input_output_aliases=aliases,
        compiler_params=pltpu.CompilerParams(
            dimension_semantics=("arbitrary",),
            vmem_limit_bytes=LATENT_VMEM_LIMIT_BYTES if lat else VMEM_LIMIT_BYTES),
        name=("latent_layer" if lat else "context_layer") + str(layer),
    )(*args)


def _rope_tables(T):
    rows = T // GRID_W
    row = jnp.repeat(jnp.arange(rows, dtype=F32), GRID_W)
    col = jnp.tile(jnp.arange(GRID_W, dtype=F32), rows)
    inv = ROPE_THETA ** (-jnp.arange(0, AXIS_DIM, 2, dtype=F32) / AXIS_DIM)
    ar = row[:, None] * inv
    ac = col[:, None] * inv
    ang = jnp.concatenate([ar, ar, ac, ac], axis=-1)
    ang = jnp.concatenate([ang, ang], axis=-1)
    first_half = (jnp.arange(LANES) % AXIS_DIM) < (AXIS_DIM // 2)
    cos = jnp.cos(ang)
    sin = jnp.sin(ang)
    sa = jnp.where(first_half[None, :], -sin, 0.0)
    sb = jnp.where(first_half[None, :], 0.0, sin)
    return cos, sa, sb


def _constants(sink_logit):
    grp = np.arange(N_HEADS * HEAD_DIM) // HEAD_DIM
    bd = jnp.asarray(grp[:, None] == grp[None, :], dtype=BF16)
    ti = np.arange(CHUNK)
    tril = jnp.asarray(ti[None, :] <= ti[:, None], dtype=BF16)
    triu = jnp.asarray(ti[None, :] >= ti[:, None], dtype=BF16)
    return bd, tril, triu, sink_logit.reshape(-1)


def kernel(x_prompt, x_sample, cache_attn_k, cache_attn_v, cache_win_k, cache_win_v, state_mlstm_C,
           state_mlstm_n, state_mlstm_m, c, c_ctx, w_mod, b_mod, w_in, qk_gain, sink_logit,
           mlstm_gate_bias, w_branch, w_out, ln_gain, ln_bias):
    dec_b = x_sample.shape[0]
    assert dec_b + 1 <= 8
    cond = jnp.concatenate([c_ctx[None, :], c, jnp.zeros((8 - 1 - dec_b, D_MODEL), F32)], axis=0)
    mod = _modulation(cond, w_mod, b_mod).reshape(DEPTH, 8, 3, D_MODEL)

    consts = _constants(sink_logit)
    cos, sa, sb = _rope_tables(x_sample.shape[1])
    lane_pad = jnp.zeros((DEPTH, 1, LANES - N_GATES), F32)
    weights = (
        _prep_in_weight(jnp.swapaxes(w_in, 1, 2)), w_branch.astype(BF16), w_out.astype(BF16),
        jnp.tile(qk_gain[:, 0:1, :], (1, 1, N_HEADS)),
        jnp.tile(qk_gain[:, 1:2, :], (1, 1, N_KV)),
        jnp.concatenate([mlstm_gate_bias.reshape(DEPTH, 1, N_GATES), lane_pad], axis=2),
        ln_gain[:, None, :], ln_bias[:, None, :],
    )

    ctx = None
    cached_t = [jnp.swapaxes(a, -1, -2)
                for a in (cache_attn_k, cache_attn_v, cache_win_k, cache_win_v, state_mlstm_C)]
    xp, xs = x_prompt, x_sample
    small = []
    for l in range(DEPTH):
        outs = _layer_call(False, l, xp, mod, weights, consts, ctx)
        xp = outs[0]
        ctx = tuple(outs[1:6])
        small.append(outs[6:])
        extra = (cos, sa, sb, *cached_t, state_mlstm_n, state_mlstm_m)
        xs = _layer_call(True, l, xs, mod, weights, consts, extra)
    new_n, new_m = [jnp.stack([small[l][i] for l in range(DEPTH)], axis=1) for i in range(2)]
    return (xp, xs, *[jnp.swapaxes(a, -1, -2) for a in ctx], new_n, new_m)
```

```python
import functools

import jax
import jax.numpy as jnp
import numpy as np
from jax import lax
from jax.experimental import pallas as pl
from jax.experimental.pallas import tpu as pltpu

F32 = jnp.float32
BF16 = jnp.bfloat16

D_MODEL = 1024
DEPTH = 2
PAST_LEN = 256
GRID_W = 64
HEAD_DIM = 64
N_HEADS = 8
N_KV = 2
N_GROUP = N_HEADS // N_KV
WINDOW = 128
ML_HEADS = 4
ML_DK = 64
ML_DV = 128
CHUNK = 128
BRANCH_W = 512
ROPE_THETA = 10000.0
AXIS_DIM = HEAD_DIM // 2
LN_EPS = 1e-6
RMS_EPS = 1e-6
ALPHA = (2.0 * DEPTH) ** 0.25
LOG2E = float(np.log2(np.e))
LANES = 128
BF16_ROWS = 16
ROW_BLOCK = 256
CHAINS_IN_FLIGHT = 4
CTX_ALIASED = 5
N_SHARED = 14
VMEM_LIMIT_BYTES = 60 * 1024 * 1024
LATENT_VMEM_LIMIT_BYTES = 62 * 1024 * 1024
VT_ROWS = HEAD_DIM + BF16_ROWS
ST_ROWS = ML_DV + BF16_ROWS

_SIZES = (512, 128, 128, 512, 256, 256, 512, 512, 128, 512, 512, 128, 128, 512, 3072)
_OFF = [int(v) for v in np.concatenate([[0], np.cumsum(_SIZES)])]
(C_QA, C_KA, C_VA, C_ZA, C_QM, C_KM, C_VM, C_OM, C_GT, C_ZM, C_QW, C_KW, C_VW, C_ZW, C_GMERGE) = _OFF[:-1]
W_COLS = _OFF[-1]
GATE_COL = 2816
N_GATES = 4 * ML_HEADS


def _dot(a, b):
    return jnp.dot(a, b, preferred_element_type=F32)


def _dot_tb(a, b):
    return lax.dot_general(a, b, (((1,), (1,)), ((), ())), preferred_element_type=F32)


def _split(a):
    hi = a.astype(BF16)
    lo = (a - hi.astype(F32)).astype(BF16)
    return hi, lo


def _layer_norm(x):
    mu = jnp.mean(x, axis=-1, keepdims=True)
    xc = x - mu
    var = jnp.mean(xc * xc, axis=-1, keepdims=True)
    return xc * lax.rsqrt(var + LN_EPS)


def _log_sigmoid(x):
    return jnp.minimum(x, 0.0) - jnp.log(1.0 + jnp.exp(-jnp.abs(x)))


def _sigmoid(x):
    return jax.nn.sigmoid(x)


def _mod_kernel(c_ref, w_ref, b_ref, o_ref):
    c = c_ref[...]
    s = c * _sigmoid(c)
    s_hi, s_lo = _split(s)
    w = w_ref[...]
    w_hi, w_lo = _split(w)
    o_ref[...] = _dot(s_hi, w_hi) + _dot(s_lo, w_hi) + _dot(s_hi, w_lo) + b_ref[...]


def _modulation(cond, w_mod, b_mod):
    rows = cond.shape[0]
    tn = 1024
    return pl.pallas_call(
        _mod_kernel,
        grid=(DEPTH, 3 * D_MODEL // tn),
        in_specs=[
            pl.BlockSpec((rows, D_MODEL), lambda l, j: (0, 0)),
            pl.BlockSpec((None, D_MODEL, tn), lambda l, j: (l, 0, j)),
            pl.BlockSpec((None, 1, tn), lambda l, j: (l, 0, j)),
        ],
        out_specs=pl.BlockSpec((None, rows, tn), lambda l, j: (l, 0, j)),
        out_shape=jax.ShapeDtypeStruct((DEPTH, rows, 3 * D_MODEL), F32),
        compiler_params=pltpu.CompilerParams(dimension_semantics=("arbitrary", "arbitrary")),
        name="adaln_modulation",
    )(cond, w_mod, b_mod.reshape(DEPTH, 1, 3 * D_MODEL))


PREP_K = 256


def _prep_kernel(wt_ref, o_ref):
    lane = lax.broadcasted_iota(jnp.int32, (PREP_K, LANES), 1)
    for c0 in range(0, W_COLS, LANES):
        r0 = c0 if c0 <= GATE_COL else c0 - (LANES - N_GATES)
        slab = jnp.transpose(wt_ref[r0:r0 + LANES, :])
        if c0 == GATE_COL:
            slab = jnp.where(lane < N_GATES, slab, 0.0)
        o_ref[:, c0:c0 + LANES] = slab.astype(BF16)


def _prep_in_weight(w_in_t):
    n_in = w_in_t.shape[1]
    return pl.pallas_call(
        _prep_kernel,
        grid=(DEPTH, D_MODEL // PREP_K),
        in_specs=[pl.BlockSpec((None, n_in, PREP_K), lambda l, i: (l, 0, i))],
        out_specs=pl.BlockSpec((None, PREP_K, W_COLS), lambda l, i: (l, i, 0)),
        out_shape=jax.ShapeDtypeStruct((DEPTH, D_MODEL, W_COLS), BF16),
        compiler_params=pltpu.CompilerParams(
            dimension_semantics=("arbitrary", "arbitrary"), vmem_limit_bytes=VMEM_LIMIT_BYTES),
        name="in_weight_prep",
    )(w_in_t)


def _rope(x, cos, sa, sb):
    return x * cos + pltpu.roll(x, LANES - AXIS_DIM // 2, 1) * sa + pltpu.roll(x, AXIS_DIM // 2, 1) * sb


def _group_rms(x, bd, gain):
    ms = _dot((x * x).astype(BF16), bd) * (1.0 / HEAD_DIM)
    return x * lax.rsqrt(ms + RMS_EPS) * gain


def _ones_row_tile():
    r = lax.broadcasted_iota(jnp.int32, (BF16_ROWS, LANES), 0)
    return jnp.where(r == 0, 1.0, 0.0).astype(BF16)


def _attend_chains(chains, results, depth):
    state = []
    for qs, tiles, sink_row in chains:
        m_cols = qs.shape[0]
        if sink_row is None:
            m = jnp.full((1, m_cols), -jnp.inf, F32)
            acc = jnp.zeros((VT_ROWS, m_cols), F32)
        else:
            m = sink_row
            r = lax.broadcasted_iota(jnp.int32, (VT_ROWS, m_cols), 0)
            acc = jnp.where(r == HEAD_DIM, 1.0, 0.0)
        state.append([m, acc])
    order = [(t, c) for t in range(max(len(tiles) for _, tiles, _ in chains))
             for c, (_, tiles, _) in enumerate(chains) if t < len(tiles)]

    def issue(t, c):
        qs, tiles, _ = chains[c]
        tile = tiles[t]()
        return tile, _dot_tb(tile[0], qs)

    pending = [issue(*item) for item in order[:depth]]
    yield
    for k, (t, c) in enumerate(order):
        (_, v_t, valid), s = pending.pop(0)
        if k + depth < len(order):
            pending.append(issue(*order[k + depth]))
        if valid is not None:
            s = jnp.where(valid, s, -jnp.inf)
        m, acc = state[c]
        m_new = jnp.maximum(m, jnp.max(s, axis=0, keepdims=True))
        p = jnp.exp2(s - m_new).astype(BF16)
        state[c] = [m_new, acc * jnp.exp2(m - m_new) + _dot(v_t, p)]
        if k + 1 == len(order) or order[k + 1][0] != t:
            yield
    results.extend(acc[0:HEAD_DIM, :] / acc[HEAD_DIM:HEAD_DIM + 1, :] for _, acc in state)


def _interleave(weighted):
    live = [[g, n] for g, n in weighted]
    while live:
        for item in list(live):
            for _ in range(item[1]):
                try:
                    next(item[0])
                except StopIteration:
                    live.remove(item)
                    break
        yield


def _run(gen):
    for _ in gen:
        pass


def _layer_kernel(lat, T, layer, *refs):
    _run(_sequence(lat, T, layer, *refs))


def _sequence(lat, T, layer, *refs):
    refs = list(refs)
    x_ref, mod_ref, w_ref, wb_ref, wo_ref = refs[:5]
    g0_ref, g1_ref, gb_ref, lng_ref, lnb_ref = refs[5:10]
    bd_ref, tril_ref, triu_ref, sink_ref = refs[10:N_SHARED]
    pos = N_SHARED
    if lat:
        cos_ref, sa_ref, sb_ref = refs[pos:pos + 3]
        cka_ref, cva_ref, ckw_ref, cvw_ref, c0_ref, n0_ref, m0_ref = refs[pos + 3:pos + 10]
        pos += 10
        y_ref = refs[pos]
        pos += 1
    else:
        if layer > 0:
            pos += CTX_ALIASED
        x_next_ref = refs[pos]
        pos += 1
        y_ref, ka_o, va_o, kw_o, vw_o, c_o, n_o, m_o = refs[pos:pos + 8]
        pos += 8
        if layer == 0:
            for full in (ka_o, va_o, kw_o, vw_o, c_o):
                full[1:] = jnp.zeros((DEPTH - 1,) + full.shape[1:], F32)
            ka_o, va_o, kw_o, vw_o, c_o = (full.at[0] for full in (ka_o, va_o, kw_o, vw_o, c_o))
    (u_s, qa_s, qw_s, ka_s, vat_s, kw_s, vwt_s, qm_s, km_s, vmt_s, om_s, g_s,
     ya_s, yw_s, hmt_s, c_s, m_s) = refs[pos:pos + 17]
    pos += 17
    if not lat:
        step = pl.program_id(0)
        u_next_s = u_s.at[(step + 1) % 2]
        u_s = u_s.at[step % 2]
    if lat:
        kwc_s, vwct_s = refs[pos:pos + 2]

    row_block = min(T, ROW_BLOCK)
    n_rows = T // row_block
    n_chunks = T // CHUNK
    chunks_per_block = row_block // CHUNK
    shift = mod_ref[0:1, :]
    scale = mod_ref[1:2, :]
    gate = mod_ref[2:3, :]
    ones_tile = _ones_row_tile()

    norm_rows = 64

    def row_slice(i, j, n):
        r0 = i * row_block + j
        return pl.ds(r0 if isinstance(r0, int) else pl.multiple_of(r0, n), n)

    def phase1_norm(i, src=x_ref, dst=u_s):
        for j in range(0, row_block, norm_rows):
            piece = row_slice(i, j, norm_rows)
            dst[piece, :] = (_layer_norm(src[piece, :]) * (1.0 + scale) + shift).astype(BF16)
            yield

    def phase1_proj(i):
        R = row_block
        rows = row_slice(i, 0, R)
        u = u_s[rows, :]

        def proj(c0, width):
            return _dot(u, w_ref[:, c0:c0 + width])

        if lat:
            cos = cos_ref[rows, :]
            sa = sa_ref[rows, :]
            sb = sb_ref[rows, :]

        def rope_wide(v):
            if not lat:
                return v
            slabs = [_rope(v[:, j:j + LANES], cos, sa, sb) for j in range(0, v.shape[1], LANES)]
            return slabs[0] if len(slabs) == 1 else jnp.concatenate(slabs, axis=1)

        qscale = LOG2E * HEAD_DIM ** -0.5
        qa = rope_wide(_group_rms(proj(C_QA, 512), bd_ref[...], g0_ref[...])) * qscale
        for h in range(N_HEADS):
            qa_s[h, rows, :] = qa[:, h * HEAD_DIM:(h + 1) * HEAD_DIM].astype(BF16)
        yield
        ka_n = _group_rms(proj(C_KA, LANES), bd_ref[0:LANES, 0:LANES], g1_ref[...])
        ka = rope_wide(ka_n)
        va = proj(C_VA, LANES)
        qw = rope_wide(proj(C_QW, 512)) * qscale
        for h in range(N_HEADS):
            qw_s[h, rows, :] = qw[:, h * HEAD_DIM:(h + 1) * HEAD_DIM].astype(BF16)
        yield
        kw_raw = proj(C_KW, LANES)
        kw = rope_wide(kw_raw)
        vw = proj(C_VW, LANES)
        for kv in range(N_KV):
            sl = slice(kv * HEAD_DIM, (kv + 1) * HEAD_DIM)
            ka_s[kv, rows, :] = ka[:, sl].astype(BF16)
            kw_s[kv, rows, :] = kw[:, sl].astype(BF16)
        yield
        qm_s[rows, :] = proj(C_QM, 256).astype(BF16)
        km_s[rows, :] = (proj(C_KM, 256) * (ML_DK ** -0.5)).astype(BF16)
        yield
        vm = proj(C_VM, 512)
        yield
        om_s[rows, :] = _sigmoid(proj(C_OM, 512)).astype(BF16)
        g_s[rows, :] = proj(C_GT, LANES) + gb_ref[...]
        yield
        for j in range(chunks_per_block):
            cj = i * chunks_per_block + j
            cr = slice(j * CHUNK, (j + 1) * CHUNK)
            va_t = jnp.transpose(va[cr, :])
            vw_t = jnp.transpose(vw[cr, :])
            for kv in range(N_KV):
                hd = slice(kv * HEAD_DIM, (kv + 1) * HEAD_DIM)
                vat_s[cj, kv, 0:HEAD_DIM, :] = va_t[hd, :].astype(BF16)
                vat_s[cj, kv, HEAD_DIM:VT_ROWS, :] = ones_tile
                vwt_s[cj, kv, 0:HEAD_DIM, :] = vw_t[hd, :].astype(BF16)
                vwt_s[cj, kv, HEAD_DIM:VT_ROWS, :] = ones_tile
            if not lat:
                tc = slice(i * row_block + j * CHUNK, i * row_block + (j + 1) * CHUNK)
                ka_t = jnp.transpose(ka_n[cr, :])
                kw_t = jnp.transpose(kw_raw[cr, :])
                for kv in range(N_KV):
                    hd = slice(kv * HEAD_DIM, (kv + 1) * HEAD_DIM)
                    ka_o[kv, :, tc] = ka_t[hd, :]
                    va_o[kv, :, tc] = va_t[hd, :]
                    kw_o[kv, :, tc] = kw_t[hd, :]
                    vw_o[kv, :, tc] = vw_t[hd, :]
            for h in range(ML_HEADS):
                vmt_s[cj, h, 0:ML_DV, :] = jnp.transpose(vm[cr, h * ML_DV:(h + 1) * ML_DV]).astype(BF16)
                vmt_s[cj, h, ML_DV:ST_ROWS, :] = ones_tile
            yield

    if not lat:
        @pl.when(step == 0)
        def _():
            _run(phase1_norm(0))
        yield from phase1_proj(0)
    else:
        def body1(i, carry):
            _run(phase1_norm(i))
            _run(phase1_proj(i))
            return carry
        lax.fori_loop(0, n_rows, body1, 0)

    if lat:
        for j in range(PAST_LEN // CHUNK):
            cols = slice(j * CHUNK, (j + 1) * CHUNK)
            ka_c = jnp.transpose(jnp.concatenate([cka_ref[kv, :, cols] for kv in range(N_KV)], axis=0))
            kw_c = jnp.transpose(jnp.concatenate([ckw_ref[kv, :, cols] for kv in range(N_KV)], axis=0))
            for kv in range(N_KV):
                hd = slice(kv * HEAD_DIM, (kv + 1) * HEAD_DIM)
                ka_s[kv, T + j * CHUNK:T + (j + 1) * CHUNK, :] = ka_c[:, hd].astype(BF16)
                kwc_s[kv, cols, :] = kw_c[:, hd].astype(BF16)
                vat_s[n_chunks + j, kv, 0:HEAD_DIM, :] = cva_ref[kv, :, cols].astype(BF16)
                vat_s[n_chunks + j, kv, HEAD_DIM:VT_ROWS, :] = ones_tile
                vwct_s[kv, 0:HEAD_DIM, cols] = cvw_ref[kv, :, cols].astype(BF16)
                vwct_s[kv, HEAD_DIM:VT_ROWS, cols] = ones_tile

    tq = CHUNK
    scores_in_flight = 2 if lat else CHAINS_IN_FLIGHT
    m_cols = N_GROUP * tq

    def sink_row(kv):
        col = lax.broadcasted_iota(jnp.int32, (1, m_cols), 1)
        row = jnp.full((1, m_cols), sink_ref[layer * N_HEADS + kv * N_GROUP], F32)
        for g in range(1, N_GROUP):
            row = jnp.where(col >= g * tq, sink_ref[layer * N_HEADS + kv * N_GROUP + g], row)
        return row * LOG2E

    def load_q(q_s, kv, q0):
        return q_s[kv * N_GROUP:(kv + 1) * N_GROUP, pl.ds(q0, tq), :].reshape(m_cols, HEAD_DIM)

    def store_heads(dst, q0, kv, o_t):
        for p in range(N_GROUP // 2):
            blk = jnp.concatenate([o_t[:, (2 * p) * tq:(2 * p + 1) * tq],
                                   o_t[:, (2 * p + 1) * tq:(2 * p + 2) * tq]], axis=0)
            c0 = (kv * N_GROUP + 2 * p) * HEAD_DIM
            dst[pl.ds(q0, tq), c0:c0 + LANES] = jnp.transpose(blk).astype(dst.dtype)

    def key_tiles(k_ref, vt_ref, kv, n_key_chunks, chunks_per_tile):
        def tile(c0):
            def load():
                k = k_ref[kv, c0 * CHUNK:(c0 + chunks_per_tile) * CHUNK, :]
                v_t = [vt_ref[c0 + j, kv] for j in range(chunks_per_tile)]
                return k, (v_t[0] if len(v_t) == 1 else jnp.concatenate(v_t, axis=1)), None
            return load
        return [tile(c0) for c0 in range(0, n_key_chunks, chunks_per_tile)]

    def band_tiles(kv, n, q0):
        j0 = jnp.clip(n - 1, 0, n_chunks - 3)
        w0 = pl.multiple_of(j0 * CHUNK, CHUNK)

        def band(c0, nc):
            def load():
                rows = nc * CHUNK
                kpos = w0 + c0 * CHUNK + lax.broadcasted_iota(jnp.int32, (rows, m_cols), 0)
                qpos = q0 + (lax.broadcasted_iota(jnp.int32, (rows, m_cols), 1) & (tq - 1))
                v_t = [vwt_s[j0 + c0 + j, kv] for j in range(nc)]
                return (kw_s[kv, pl.ds(w0 + c0 * CHUNK, rows), :],
                        v_t[0] if nc == 1 else jnp.concatenate(v_t, axis=1),
                        jnp.abs(kpos - qpos) <= WINDOW)
            return load
        return [band(0, 2), band(2, 1), lambda: (kwc_s[kv], vwct_s[kv], None)]

    def attend(n, q0, chunks_per_tile):
        chains, dsts = [], []
        n_key_chunks = ka_s.shape[1] // CHUNK
        for kv in range(N_KV):
            chains.append((load_q(qa_s, kv, q0), key_tiles(ka_s, vat_s, kv, n_key_chunks, chunks_per_tile), None))
            dsts.append((ya_s, kv))
        for kv in range(N_KV):
            tiles = band_tiles(kv, n, q0) if lat else key_tiles(kw_s, vwt_s, kv, n_chunks, chunks_per_tile)
            chains.append((load_q(qw_s, kv, q0), tiles, sink_row(kv)))
            dsts.append((yw_s, kv))
        for g0 in range(0, len(chains), CHAINS_IN_FLIGHT):
            results = []
            yield from _attend_chains(chains[g0:g0 + CHAINS_IN_FLIGHT], results, scores_in_flight)
            for (dst, kv), o_t in zip(dsts[g0:g0 + CHAINS_IN_FLIGHT], results):
                store_heads(dst, q0, kv, o_t)

    n_pairs = ML_HEADS // 2
    hmt_s[...] = jnp.zeros_like(hmt_s)
    for d in range(2):
        if lat:
            for pr in range(n_pairs):
                c_s[d * n_pairs + pr, 0:ML_DV, :] = jnp.transpose(
                    jnp.concatenate([c0_ref[d, 2 * pr], c0_ref[d, 2 * pr + 1]], axis=0))
        for h in range(ML_HEADS):
            idx = d * n_pairs + h // 2
            hl = slice((h % 2) * ML_DK, (h % 2 + 1) * ML_DK)
            c_s[idx, ML_DV:ST_ROWS, hl] = jnp.zeros((BF16_ROWS, ML_DK), F32)
            if lat:
                c_s[idx, ML_DV:ML_DV + 1, hl] = n0_ref[d, h:h + 1, :]
                m_s[d * ML_HEADS + h:d * ML_HEADS + h + 1, :] = jnp.broadcast_to(
                    m0_ref[d:d + 1, h:h + 1], (1, LANES))
            else:
                c_s[idx, 0:ML_DV, hl] = jnp.zeros((ML_DV, ML_DK), F32)
                m_s[d * ML_HEADS + h:d * ML_HEADS + h + 1, :] = jnp.zeros((1, LANES), F32)

    L = CHUNK
    s_idx = lax.broadcasted_iota(jnp.int32, (L, ML_HEADS * L), 0)
    t_idx = lax.broadcasted_iota(jnp.int32, (L, ML_HEADS * L), 1) & (L - 1)
    lane_row = lax.broadcasted_iota(jnp.int32, (1, LANES), 1)
    low_half = lax.broadcasted_iota(jnp.int32, (L, LANES), 1) < ML_DK
    ones_ll = jnp.ones((L, L), BF16)
    zeros_ll = jnp.zeros((L, L), BF16)

    def heads_row(src, r0, c0=0):
        return jnp.concatenate([src[r0 + h:r0 + h + 1, c0:c0 + L] for h in range(ML_HEADS)], axis=1)

    def block_diag_rows(x):
        zero = jnp.zeros_like(x)
        return jnp.concatenate([jnp.where(low_half, x, zero), jnp.where(low_half, zero, x)], axis=0)

    def mlstm_streams(streams):
        m_state = {d: heads_row(m_s, d * ML_HEADS) for d in sorted({d for d, _ in streams})}
        c_state = {d: [c_s[d * n_pairs + pr] for pr in range(n_pairs)] for d in m_state}
        work = []
        for d, cc in streams:
            r0 = cc * L if isinstance(cc, int) else pl.multiple_of(cc * L, L)
            rows = pl.ds(r0, L)
            g = g_s[rows, :]
            f_hi, f_lo = _split(_log_sigmoid(g))
            tri_c = tril_ref[...] if d == 0 else triu_ref[...]
            tri_r = triu_ref[...] if d == 0 else tril_ref[...]
            cum = _dot(tri_c, f_hi) + _dot(tri_c, f_lo)
            r = g - pltpu.roll(cum, LANES - ML_HEADS, 1)
            gi0 = 2 * ML_HEADS * d
            rb = jnp.concatenate([jnp.broadcast_to(r[:, gi0 + h:gi0 + h + 1], (L, L))
                                  for h in range(ML_HEADS)], axis=1)
            g_t = jnp.transpose(g)[0:N_GATES, :]
            ft_hi, ft_lo = _split(_log_sigmoid(g_t))
            tr = jnp.concatenate([tri_r, ones_ll], axis=1)
            ct = _dot(ft_hi, tr) + _dot(ft_lo, tr)
            work.append(dict(d=d, cc=cc, rows=rows, rb=rb, g_t=g_t, ct=ct))
            yield
        for w in work:
            d = w["d"]
            valid = (s_idx <= t_idx) if d == 0 else (s_idx >= t_idx)
            gi0 = 2 * ML_HEADS * d
            gf0 = gi0 + ML_HEADS
            b_row = heads_row(w["ct"], gf0)
            b_last = heads_row(w["ct"], gf0, L)
            i_row = heads_row(w["g_t"], gi0)
            m_prev = m_state[d]
            a_row = b_row + m_prev
            dm = jnp.where(valid, b_row + w["rb"], -jnp.inf)
            mt = jnp.maximum(a_row, jnp.max(dm, axis=0, keepdims=True))
            w["p"] = jnp.exp(dm - mt)
            w["w_inter"] = jnp.exp(a_row - mt)
            w["floor"] = jnp.exp(-mt)
            g_row = b_last - b_row + i_row
            g_max = jnp.concatenate(
                [jnp.broadcast_to(jnp.max(g_row[:, h * L:(h + 1) * L], axis=-1, keepdims=True), (1, L))
                 for h in range(ML_HEADS)], axis=1)
            m_new = jnp.maximum(b_last + m_prev, g_max)
            w["ws"] = jnp.exp(g_row - m_new)
            w["wc"] = jnp.exp(b_last + m_prev - m_new)
            m_state[d] = m_new
            yield
        for w in work:
            rows, cc = w["rows"], w["cc"]
            w["pairs"] = []
            for pr in range(n_pairs):
                lanes = slice(pr * LANES, (pr + 1) * LANES)
                cols = slice(pr * 2 * L, (pr + 1) * 2 * L)
                k_pair = km_s[rows, lanes]
                q_bd = block_diag_rows(qm_s[rows, lanes])
                k_bd = block_diag_rows(k_pair)
                s_t = (_dot_tb(k_pair, q_bd) * w["p"][:, cols]).astype(BF16)
                s_bd = jnp.concatenate(
                    [jnp.concatenate([s_t[:, 0:L], zeros_ll], axis=1),
                     jnp.concatenate([zeros_ll, s_t[:, L:2 * L]], axis=1)], axis=0)
                v_t = jnp.concatenate([vmt_s[cc, 2 * pr], vmt_s[cc, 2 * pr + 1]], axis=1)
                intra = _dot(v_t, s_bd)
                wv = (v_t.astype(F32) * w["ws"][:, cols]).astype(BF16)
                w["pairs"].append((q_bd, intra, _dot(wv, k_bd)))
            yield
        for w in work:
            d, cc = w["d"], w["cc"]
            for pr, (q_bd, intra, update) in enumerate(w["pairs"]):
                cols = slice(pr * 2 * L, (pr + 1) * 2 * L)
                state = c_state[d][pr]
                tot = intra + w["w_inter"][:, cols] * _dot_tb(state.astype(BF16), q_bd)
                h_t = tot[0:ML_DV, :] / jnp.maximum(jnp.abs(tot[ML_DV:ML_DV + 1, :]), w["floor"][:, cols])
                for e in range(2):
                    hr = slice((2 * pr + e) * ML_DV, (2 * pr + e + 1) * ML_DV)
                    hmt_s[cc, hr, :] += h_t[:, e * L:(e + 1) * L]
                wc = w["wc"]
                wc_pair = jnp.where(lane_row < ML_DK, wc[:, 2 * pr * L:(2 * pr + 1) * L],
                                    wc[:, (2 * pr + 1) * L:(2 * pr + 2) * L])
                c_state[d][pr] = wc_pair * state + update
            yield
        for d in m_state:
            for h in range(ML_HEADS):
                m_s[d * ML_HEADS + h:d * ML_HEADS + h + 1, :] = m_state[d][:, h * L:(h + 1) * L]
            for pr in range(n_pairs):
                c_s[d * n_pairs + pr] = c_state[d][pr]

    def scan_streams(cs):
        return [(d, c if d == 0 else n_chunks - 1 - c) for c in cs for d in range(2)]

    if lat:
        def body2(n, carry):
            _run(_interleave([(attend(n, pl.multiple_of(n * tq, tq), 2), 1),
                              (mlstm_streams(scan_streams([n])), 1)]))
            return carry
        lax.fori_loop(0, n_chunks, body2, 0)
    else:
        def all_tiles():
            for n in range(T // tq):
                yield from attend(n, n * tq, 2)
        yield from _interleave([(all_tiles(), 1), (mlstm_streams(scan_streams(range(n_chunks))), 3)])

    if not lat:
        for d in range(2):
            for pr in range(n_pairs):
                c_t = jnp.transpose(c_s[d * n_pairs + pr, 0:ML_DV, :])
                for e in range(2):
                    c_o[d, 2 * pr + e] = c_t[e * ML_DK:(e + 1) * ML_DK, :]
            for h in range(ML_HEADS):
                idx = d * n_pairs + h // 2
                hl = slice((h % 2) * ML_DK, (h % 2 + 1) * ML_DK)
                n_o[d, h:h + 1, :] = c_s[idx, ML_DV:ML_DV + 1, hl]
                m_o[d:d + 1, h:h + 1] = m_s[d * ML_HEADS + h:d * ML_HEADS + h + 1, 0:1]

    def phase3_merge(i, out):
        rows = row_slice(i, 0, row_block)
        u = u_s[rows, :]

        def proj(c0, width):
            return _dot(u, w_ref[:, c0:c0 + width])

        hm = jnp.concatenate(
            [jnp.concatenate([jnp.transpose(hmt_s[i * chunks_per_block + j, h * ML_DV:(h + 1) * ML_DV, :])
                              for h in range(ML_HEADS)], axis=1)
             for j in range(chunks_per_block)], axis=0)
        ys = (ya_s[rows, :].astype(F32), om_s[rows, :].astype(F32) * hm, yw_s[rows, :].astype(F32))
        yield
        merged = None
        for b, (yb, zc) in enumerate(zip(ys, (C_ZA, C_ZM, C_ZW))):
            z = proj(zc, BRANCH_W)
            t = (yb * (z * _sigmoid(z))).astype(BF16)
            yield
            pb = _sigmoid(proj(C_GMERGE + b * D_MODEL, D_MODEL)) * _dot(t, wb_ref[b])
            merged = pb if merged is None else merged + pb
            yield
        out.append(_dot(merged.astype(BF16), wo_ref[...]))
        yield

    def phase3_norm(i, o):
        for j in range(0, row_block, norm_rows):
            piece = row_slice(i, j, norm_rows)
            hres = ALPHA * x_ref[piece, :] + gate * o[j:j + norm_rows, :]
            y_ref[piece, :] = _layer_norm(hres) * lng_ref[...] + lnb_ref[...]
            yield

    def phase3(i):
        box = []
        yield from phase3_merge(i, box)
        yield from phase3_norm(i, box[0])

    if not lat:
        box = []
        yield from _interleave([(phase3_merge(0, box), 1), (phase1_norm(0, x_next_ref, u_next_s), 1)])
        yield from phase3_norm(0, box[0])
    else:
        def body4(i, carry):
            _run(phase3(i))
            return carry
        lax.fori_loop(0, n_rows, body4, 0)


def _const_spec(shape):
    nd = len(shape)
    return pl.BlockSpec(shape, lambda b: (0,) * nd, pipeline_mode=pl.Buffered(1))


def _layer_spec(shape, layer):
    nd = len(shape) - 1
    return pl.BlockSpec((None,) + tuple(shape[1:]), lambda b: (layer,) + (0,) * nd,
                        pipeline_mode=pl.Buffered(1))


def _layer_call(lat, layer, x, mod, weights, consts, extra):
    B, T, _ = x.shape
    S = T + PAST_LEN if lat else T
    n_chunks = T // CHUNK

    seq_spec = pl.BlockSpec((None, T, D_MODEL), lambda b: (b, 0, 0))
    out_seq_spec = seq_spec
    if lat:
        mod_spec = pl.BlockSpec((None, None, 3, D_MODEL), lambda b: (layer, b + 1, 0, 0))
    else:
        mod_spec = pl.BlockSpec((None, None, 3, D_MODEL), lambda b: (layer, 0, 0, 0))
    *vmem_consts, sink = consts
    in_specs = [seq_spec, mod_spec]
    in_specs += [_layer_spec(a.shape, layer) for a in weights]
    in_specs += [_const_spec(a.shape) for a in vmem_consts]
    in_specs.append(pl.BlockSpec(memory_space=pltpu.SMEM))
    args = [x, mod, *weights, *vmem_consts, sink]

    if lat:
        cos, sa, sb, cka, cva, ckw, cvw, sc, sn, sm = extra
        in_specs += [_const_spec(cos.shape)] * 3
        kv_spec = pl.BlockSpec((None, None, N_KV, HEAD_DIM, PAST_LEN), lambda b: (b, layer, 0, 0, 0))
        in_specs += [kv_spec] * 4
        in_specs += [
            pl.BlockSpec((None, None, 2, ML_HEADS, ML_DK, ML_DV), lambda b: (b, layer, 0, 0, 0, 0)),
            pl.BlockSpec((None, None, 2, ML_HEADS, ML_DK), lambda b: (b, layer, 0, 0, 0)),
            pl.BlockSpec((None, None, 2, ML_HEADS), lambda b: (b, layer, 0, 0)),
        ]
        args += [cos, sa, sb, cka, cva, ckw, cvw, sc, sn, sm]
        out_specs = out_seq_spec
        out_shape = jax.ShapeDtypeStruct((B, T, D_MODEL), F32)
        aliases = {}
    else:
        kv_shape = (B, DEPTH, N_KV, HEAD_DIM, T)
        c_shape = (B, DEPTH, 2, ML_HEADS, ML_DK, ML_DV)
        if layer == 0:
            aliases = {}
            ld = DEPTH
            layer_idx = 0
        else:
            first_acc = len(args)
            in_specs += [pl.BlockSpec(memory_space=pl.ANY)] * len(extra)
            args += list(extra)
            aliases = {first_acc + i: 1 + i for i in range(len(extra))}
            assert len(extra) == CTX_ALIASED
            ld = None
            layer_idx = layer
        in_specs.append(pl.BlockSpec((None, T, D_MODEL), lambda b: (jnp.minimum(b + 1, B - 1), 0, 0)))
        args.append(x)
        kv_out = pl.BlockSpec((None, ld, N_KV, HEAD_DIM, T), lambda b: (b, layer_idx, 0, 0, 0))
        out_specs = [
            seq_spec, kv_out, kv_out, kv_out, kv_out,
            pl.BlockSpec((None, ld, 2, ML_HEADS, ML_DK, ML_DV), lambda b: (b, layer_idx, 0, 0, 0, 0)),
            pl.BlockSpec((None, 2, ML_HEADS, ML_DK), lambda b: (b, 0, 0, 0)),
            pl.BlockSpec((None, 2, ML_HEADS), lambda b: (b, 0, 0)),
        ]
        out_shape = [jax.ShapeDtypeStruct((B, T, D_MODEL), F32)]
        out_shape += [jax.ShapeDtypeStruct(kv_shape, F32)] * 4 + [jax.ShapeDtypeStruct(c_shape, F32)]
        out_shape += [
            jax.ShapeDtypeStruct((B, 2, ML_HEADS, ML_DK), F32),
            jax.ShapeDtypeStruct((B, 2, ML_HEADS), F32),
        ]

    scratch = [
        ((T, D_MODEL) if lat else (2, T, D_MODEL), BF16),
        ((N_HEADS, T, HEAD_DIM), BF16),
        ((N_HEADS, T, HEAD_DIM), BF16),
        ((N_KV, S, HEAD_DIM), BF16),
        ((S // CHUNK, N_KV, VT_ROWS, CHUNK), BF16),
        ((N_KV, T, HEAD_DIM), BF16),
        ((n_chunks, N_KV, VT_ROWS, CHUNK), BF16),
        ((T, ML_HEADS * ML_DK), BF16),
        ((T, ML_HEADS * ML_DK), BF16),
        ((n_chunks, ML_HEADS, ST_ROWS, CHUNK), BF16),
        ((T, ML_HEADS * ML_DV), BF16),
        ((T, LANES), F32),
        ((T, BRANCH_W), BF16),
        ((T, BRANCH_W), BF16),
        ((n_chunks, ML_HEADS * ML_DV, CHUNK), F32),
        ((ML_HEADS, ST_ROWS, 2 * ML_DK), F32),
        ((2 * ML_HEADS, LANES), F32),
    ]
    if lat:
        scratch += [((N_KV, PAST_LEN, HEAD_DIM), BF16),
                    ((N_KV, VT_ROWS, PAST_LEN), BF16)]
    scratch = [pltpu.VMEM(shape, dtype) for shape, dtype in scratch]

    return pl.pallas_call(
        functools.partial(_layer_kernel, lat, T, layer),
        grid=(B,),
        in_specs=in_specs,
        out_specs=out_specs,
        out_shape=out_shape,
        scratch_shapes=scratch,
        input_output_aliases=aliases,
        compiler_params=pltpu.CompilerParams(
            dimension_semantics=("arbitrary",),
            vmem_limit_bytes=LATENT_VMEM_LIMIT_BYTES if lat else VMEM_LIMIT_BYTES),
        name=("latent_layer" if lat else "context_layer") + str(layer),
    )(*args)


def _rope_tables(T):
    rows = T // GRID_W
    row = jnp.repeat(jnp.arange(rows, dtype=F32), GRID_W)
    col = jnp.tile(jnp.arange(GRID_W, dtype=F32), rows)
    inv = ROPE_THETA ** (-jnp.arange(0, AXIS_DIM, 2, dtype=F32) / AXIS_DIM)
    ar = row[:, None] * inv
    ac = col[:, None] * inv
    ang = jnp.concatenate([ar, ar, ac, ac], axis=-1)
    ang = jnp.concatenate([ang, ang], axis=-1)
    first_half = (jnp.arange(LANES) % AXIS_DIM) < (AXIS_DIM // 2)
    cos = jnp.cos(ang)
    sin = jnp.sin(ang)
    sa = jnp.where(first_half[None, :], -sin, 0.0)
    sb = jnp.where(first_half[None, :], 0.0, sin)
    return cos, sa, sb


def _constants(sink_logit):
    grp = np.arange(N_HEADS * HEAD_DIM) // HEAD_DIM
    bd = jnp.asarray(grp[:, None] == grp[None, :], dtype=BF16)
    ti = np.arange(CHUNK)
    tril = jnp.asarray(ti[None, :] <= ti[:, None], dtype=BF16)
    triu = jnp.asarray(ti[None, :] >= ti[:, None], dtype=BF16)
    return bd, tril, triu, sink_logit.reshape(-1)


def kernel(x_prompt, x_sample, cache_attn_k, cache_attn_v, cache_win_k, cache_win_v, state_mlstm_C,
           state_mlstm_n, state_mlstm_m, c, c_ctx, w_mod, b_mod, w_in, qk_gain, sink_logit,
           mlstm_gate_bias, w_branch, w_out, ln_gain, ln_bias):
    dec_b = x_sample.shape[0]
    assert dec_b + 1 <= 8
    cond = jnp.concatenate([c_ctx[None, :], c, jnp.zeros((8 - 1 - dec_b, D_MODEL), F32)], axis=0)
    mod = _modulation(cond, w_mod, b_mod).reshape(DEPTH, 8, 3, D_MODEL)

    consts = _constants(sink_logit)
    cos, sa, sb = _rope_tables(x_sample.shape[1])
    lane_pad = jnp.zeros((DEPTH, 1, LANES - N_GATES), F32)
    weights = (
        _prep_in_weight(jnp.swapaxes(w_in, 1, 2)), w_branch.astype(BF16), w_out.astype(BF16),
        jnp.tile(qk_gain[:, 0:1, :], (1, 1, N_HEADS)),
        jnp.tile(qk_gain[:, 1:2, :], (1, 1, N_KV)),
        jnp.concatenate([mlstm_gate_bias.reshape(DEPTH, 1, N_GATES), lane_pad], axis=2),
        ln_gain[:, None, :], ln_bias[:, None, :],
    )

    ctx = None
    cached_t = [jnp.swapaxes(a, -1, -2)
                for a in (cache_attn_k, cache_attn_v, cache_win_k, cache_win_v, state_mlstm_C)]
    xp, xs = x_prompt, x_sample
    small = []
    for l in range(DEPTH):
        outs = _layer_call(False, l, xp, mod, weights, consts, ctx)
        xp = outs[0]
        ctx = tuple(outs[1:6])
        small.append(outs[6:])
        extra = (cos, sa, sb, *cached_t, state_mlstm_n, state_mlstm_m)
        xs = _layer_call(True, l, xs, mod, weights, consts, extra)
    new_n, new_m = [jnp.stack([small[l][i] for l in range(DEPTH)], axis=1) for i in range(2)]
    return (xp, xs, *[jnp.swapaxes(a, -1, -2) for a in ctx], new_n, new_m)
```

```python
import functools

import jax
import jax.numpy as jnp
import numpy as np
from jax import lax
from jax.experimental import pallas as pl
from jax.experimental.pallas import tpu as pltpu

F32 = jnp.float32
BF16 = jnp.bfloat16

D_MODEL = 1024
DEPTH = 2
PAST_LEN = 256
GRID_W = 64
HEAD_DIM = 64
N_HEADS = 8
N_KV = 2
N_GROUP = N_HEADS // N_KV
WINDOW = 128
ML_HEADS = 4
ML_DK = 64
ML_DV = 128
CHUNK = 128
BRANCH_W = 512
ROPE_THETA = 10000.0
AXIS_DIM = HEAD_DIM // 2
LN_EPS = 1e-6
RMS_EPS = 1e-6
ALPHA = (2.0 * DEPTH) ** 0.25
LOG2E = float(np.log2(np.e))
LANES = 128
BF16_ROWS = 16
ROW_BLOCK = 256
CHAINS_IN_FLIGHT = 4
CTX_ALIASED = 5
N_SHARED = 14
VMEM_LIMIT_BYTES = 60 * 1024 * 1024
LATENT_VMEM_LIMIT_BYTES = 62 * 1024 * 1024
VT_ROWS = HEAD_DIM + BF16_ROWS
ST_ROWS = ML_DV + BF16_ROWS

_SIZES = (512, 128, 128, 512, 256, 256, 512, 512, 128, 512, 512, 128, 128, 512, 3072)
_OFF = [int(v) for v in np.concatenate([[0], np.cumsum(_SIZES)])]
(C_QA, C_KA, C_VA, C_ZA, C_QM, C_KM, C_VM, C_OM, C_GT, C_ZM, C_QW, C_KW, C_VW, C_ZW, C_GMERGE) = _OFF[:-1]
W_COLS = _OFF[-1]
GATE_COL = 2816
N_GATES = 4 * ML_HEADS


def _dot(a, b):
    return jnp.dot(a, b, preferred_element_type=F32)


def _dot_tb(a, b):
    return lax.dot_general(a, b, (((1,), (1,)), ((), ())), preferred_element_type=F32)


def _split(a):
    hi = a.astype(BF16)
    lo = (a - hi.astype(F32)).astype(BF16)
    return hi, lo


def _layer_norm(x):
    mu = jnp.mean(x, axis=-1, keepdims=True)
    xc = x - mu
    var = jnp.mean(xc * xc, axis=-1, keepdims=True)
    return xc * lax.rsqrt(var + LN_EPS)


def _log_sigmoid(x):
    return jnp.minimum(x, 0.0) - jnp.log(1.0 + jnp.exp(-jnp.abs(x)))


def _sigmoid(x):
    return jax.nn.sigmoid(x)


def _mod_kernel(c_ref, w_ref, b_ref, o_ref):
    c = c_ref[...]
    s = c * _sigmoid(c)
    s_hi, s_lo = _split(s)
    w = w_ref[...]
    w_hi, w_lo = _split(w)
    o_ref[...] = _dot(s_hi, w_hi) + _dot(s_lo, w_hi) + _dot(s_hi, w_lo) + b_ref[...]


def _modulation(cond, w_mod, b_mod):
    rows = cond.shape[0]
    tn = 1536
    return pl.pallas_call(
        _mod_kernel,
        grid=(DEPTH, 3 * D_MODEL // tn),
        in_specs=[
            pl.BlockSpec((rows, D_MODEL), lambda l, j: (0, 0)),
            pl.BlockSpec((None, D_MODEL, tn), lambda l, j: (l, 0, j)),
            pl.BlockSpec((None, 1, tn), lambda l, j: (l, 0, j)),
        ],
        out_specs=pl.BlockSpec((None, rows, tn), lambda l, j: (l, 0, j)),
        out_shape=jax.ShapeDtypeStruct((DEPTH, rows, 3 * D_MODEL), F32),
        compiler_params=pltpu.CompilerParams(dimension_semantics=("arbitrary", "arbitrary")),
        name="adaln_modulation",
    )(cond, w_mod, b_mod.reshape(DEPTH, 1, 3 * D_MODEL))


PREP_K = 256


def _prep_kernel(wt_ref, o_ref):
    lane = lax.broadcasted_iota(jnp.int32, (PREP_K, LANES), 1)
    for c0 in range(0, W_COLS, LANES):
        r0 = c0 if c0 <= GATE_COL else c0 - (LANES - N_GATES)
        slab = jnp.transpose(wt_ref[r0:r0 + LANES, :])
        if c0 == GATE_COL:
            slab = jnp.where(lane < N_GATES, slab, 0.0)
        o_ref[:, c0:c0 + LANES] = slab.astype(BF16)


def _prep_in_weight(w_in_t):
    n_in = w_in_t.shape[1]
    return pl.pallas_call(
        _prep_kernel,
        grid=(DEPTH, D_MODEL // PREP_K),
        in_specs=[pl.BlockSpec((None, n_in, PREP_K), lambda l, i: (l, 0, i))],
        out_specs=pl.BlockSpec((None, PREP_K, W_COLS), lambda l, i: (l, i, 0)),
        out_shape=jax.ShapeDtypeStruct((DEPTH, D_MODEL, W_COLS), BF16),
        compiler_params=pltpu.CompilerParams(
            dimension_semantics=("arbitrary", "arbitrary"), vmem_limit_bytes=VMEM_LIMIT_BYTES),
        name="in_weight_prep",
    )(w_in_t)


def _rope(x, cos, sa, sb):
    return x * cos + pltpu.roll(x, LANES - AXIS_DIM // 2, 1) * sa + pltpu.roll(x, AXIS_DIM // 2, 1) * sb


def _group_rms(x, bd, gain):
    ms = _dot((x * x).astype(BF16), bd) * (1.0 / HEAD_DIM)
    return x * lax.rsqrt(ms + RMS_EPS) * gain


def _ones_row_tile():
    r = lax.broadcasted_iota(jnp.int32, (BF16_ROWS, LANES), 0)
    return jnp.where(r == 0, 1.0, 0.0).astype(BF16)


def _attend_chains(chains, results, depth):
    state = []
    for qs, tiles, sink_row in chains:
        m_cols = qs.shape[0]
        if sink_row is None:
            m = jnp.full((1, m_cols), -jnp.inf, F32)
            acc = jnp.zeros((VT_ROWS, m_cols), F32)
        else:
            m = sink_row
            r = lax.broadcasted_iota(jnp.int32, (VT_ROWS, m_cols), 0)
            acc = jnp.where(r == HEAD_DIM, 1.0, 0.0)
        state.append([m, acc])
    order = [(t, c) for t in range(max(len(tiles) for _, tiles, _ in chains))
             for c, (_, tiles, _) in enumerate(chains) if t < len(tiles)]

    def issue(t, c):
        qs, tiles, _ = chains[c]
        tile = tiles[t]()
        return tile, _dot_tb(tile[0], qs)

    pending = [issue(*item) for item in order[:depth]]
    yield
    for k, (t, c) in enumerate(order):
        (_, v_t, valid), s = pending.pop(0)
        if k + depth < len(order):
            pending.append(issue(*order[k + depth]))
        if valid is not None:
            s = jnp.where(valid, s, -jnp.inf)
        m, acc = state[c]
        m_new = jnp.maximum(m, jnp.max(s, axis=0, keepdims=True))
        p = jnp.exp2(s - m_new).astype(BF16)
        state[c] = [m_new, acc * jnp.exp2(m - m_new) + _dot(v_t, p)]
        if k + 1 == len(order) or order[k + 1][0] != t:
            yield
    results.extend(acc[0:HEAD_DIM, :] / acc[HEAD_DIM:HEAD_DIM + 1, :] for _, acc in state)


def _interleave(weighted):
    live = [[g, n] for g, n in weighted]
    while live:
        for item in list(live):
            for _ in range(item[1]):
                try:
                    next(item[0])
                except StopIteration:
                    live.remove(item)
                    break
        yield


def _run(gen):
    for _ in gen:
        pass


def _layer_kernel(lat, T, layer, *refs):
    _run(_sequence(lat, T, layer, *refs))


def _sequence(lat, T, layer, *refs):
    refs = list(refs)
    x_ref, mod_ref, w_ref, wb_ref, wo_ref = refs[:5]
    g0_ref, g1_ref, gb_ref, lng_ref, lnb_ref = refs[5:10]
    bd_ref, tril_ref, triu_ref, sink_ref = refs[10:N_SHARED]
    pos = N_SHARED
    if lat:
        cos_ref, sa_ref, sb_ref = refs[pos:pos + 3]
        cka_ref, cva_ref, ckw_ref, cvw_ref, c0_ref, n0_ref, m0_ref = refs[pos + 3:pos + 10]
        pos += 10
        y_ref = refs[pos]
        pos += 1
    else:
        if layer > 0:
            pos += CTX_ALIASED
        x_next_ref = refs[pos]
        pos += 1
        y_ref, ka_o, va_o, kw_o, vw_o, c_o, n_o, m_o = refs[pos:pos + 8]
        pos += 8
        if layer == 0:
            for full in (ka_o, va_o, kw_o, vw_o, c_o):
                full[1:] = jnp.zeros((DEPTH - 1,) + full.shape[1:], F32)
            ka_o, va_o, kw_o, vw_o, c_o = (full.at[0] for full in (ka_o, va_o, kw_o, vw_o, c_o))
    (u_s, qa_s, qw_s, ka_s, vat_s, kw_s, vwt_s, qm_s, km_s, vmt_s, om_s, g_s,
     ya_s, yw_s, hmt_s, c_s, m_s) = refs[pos:pos + 17]
    pos += 17
    if not lat:
        step = pl.program_id(0)
        u_next_s = u_s.at[(step + 1) % 2]
        u_s = u_s.at[step % 2]
    if lat:
        kwc_s, vwct_s = refs[pos:pos + 2]

    row_block = min(T, ROW_BLOCK)
    n_rows = T // row_block
    n_chunks = T // CHUNK
    chunks_per_block = row_block // CHUNK
    shift = mod_ref[0:1, :]
    scale = mod_ref[1:2, :]
    gate = mod_ref[2:3, :]
    ones_tile = _ones_row_tile()

    norm_rows = 64

    def row_slice(i, j, n):
        r0 = i * row_block + j
        return pl.ds(r0 if isinstance(r0, int) else pl.multiple_of(r0, n), n)

    def phase1_norm(i, src=x_ref, dst=u_s):
        for j in range(0, row_block, norm_rows):
            piece = row_slice(i, j, norm_rows)
            dst[piece, :] = (_layer_norm(src[piece, :]) * (1.0 + scale) + shift).astype(BF16)
            yield

    def phase1_proj(i):
        R = row_block
        rows = row_slice(i, 0, R)
        u = u_s[rows, :]

        def proj(c0, width):
            return _dot(u, w_ref[:, c0:c0 + width])

        if lat:
            cos = cos_ref[rows, :]
            sa = sa_ref[rows, :]
            sb = sb_ref[rows, :]

        def rope_wide(v):
            if not lat:
                return v
            slabs = [_rope(v[:, j:j + LANES], cos, sa, sb) for j in range(0, v.shape[1], LANES)]
            return slabs[0] if len(slabs) == 1 else jnp.concatenate(slabs, axis=1)

        qscale = LOG2E * HEAD_DIM ** -0.5
        qa = rope_wide(_group_rms(proj(C_QA, 512), bd_ref[...], g0_ref[...])) * qscale
        for h in range(N_HEADS):
            qa_s[h, rows, :] = qa[:, h * HEAD_DIM:(h + 1) * HEAD_DIM].astype(BF16)
        yield
        ka_n = _group_rms(proj(C_KA, LANES), bd_ref[0:LANES, 0:LANES], g1_ref[...])
        ka = rope_wide(ka_n)
        va = proj(C_VA, LANES)
        qw = rope_wide(proj(C_QW, 512)) * qscale
        for h in range(N_HEADS):
            qw_s[h, rows, :] = qw[:, h * HEAD_DIM:(h + 1) * HEAD_DIM].astype(BF16)
        yield
        kw_raw = proj(C_KW, LANES)
        kw = rope_wide(kw_raw)
        vw = proj(C_VW, LANES)
        for kv in range(N_KV):
            sl = slice(kv * HEAD_DIM, (kv + 1) * HEAD_DIM)
            ka_s[kv, rows, :] = ka[:, sl].astype(BF16)
            kw_s[kv, rows, :] = kw[:, sl].astype(BF16)
        yield
        qm_s[rows, :] = proj(C_QM, 256).astype(BF16)
        km_s[rows, :] = (proj(C_KM, 256) * (ML_DK ** -0.5)).astype(BF16)
        yield
        vm = proj(C_VM, 512)
        yield
        om_s[rows, :] = _sigmoid(proj(C_OM, 512)).astype(BF16)
        g_s[rows, :] = proj(C_GT, LANES) + gb_ref[...]
        yield
        for j in range(chunks_per_block):
            cj = i * chunks_per_block + j
            cr = slice(j * CHUNK, (j + 1) * CHUNK)
            va_t = jnp.transpose(va[cr, :])
            vw_t = jnp.transpose(vw[cr, :])
            for kv in range(N_KV):
                hd = slice(kv * HEAD_DIM, (kv + 1) * HEAD_DIM)
                vat_s[cj, kv, 0:HEAD_DIM, :] = va_t[hd, :].astype(BF16)
                vat_s[cj, kv, HEAD_DIM:VT_ROWS, :] = ones_tile
                vwt_s[cj, kv, 0:HEAD_DIM, :] = vw_t[hd, :].astype(BF16)
                vwt_s[cj, kv, HEAD_DIM:VT_ROWS, :] = ones_tile
            if not lat:
                tc = slice(i * row_block + j * CHUNK, i * row_block + (j + 1) * CHUNK)
                ka_t = jnp.transpose(ka_n[cr, :])
                kw_t = jnp.transpose(kw_raw[cr, :])
                for kv in range(N_KV):
                    hd = slice(kv * HEAD_DIM, (kv + 1) * HEAD_DIM)
                    ka_o[kv, :, tc] = ka_t[hd, :]
                    va_o[kv, :, tc] = va_t[hd, :]
                    kw_o[kv, :, tc] = kw_t[hd, :]
                    vw_o[kv, :, tc] = vw_t[hd, :]
            for h in range(ML_HEADS):
                vmt_s[cj, h, 0:ML_DV, :] = jnp.transpose(vm[cr, h * ML_DV:(h + 1) * ML_DV]).astype(BF16)
                vmt_s[cj, h, ML_DV:ST_ROWS, :] = ones_tile
            yield

    if not lat:
        @pl.when(step == 0)
        def _():
            _run(phase1_norm(0))
        yield from phase1_proj(0)
    else:
        def body1(i, carry):
            _run(phase1_norm(i))
            _run(phase1_proj(i))
            return carry
        lax.fori_loop(0, n_rows, body1, 0)

    if lat:
        for j in range(PAST_LEN // CHUNK):
            cols = slice(j * CHUNK, (j + 1) * CHUNK)
            ka_c = jnp.transpose(jnp.concatenate([cka_ref[kv, :, cols] for kv in range(N_KV)], axis=0))
            kw_c = jnp.transpose(jnp.concatenate([ckw_ref[kv, :, cols] for kv in range(N_KV)], axis=0))
            for kv in range(N_KV):
                hd = slice(kv * HEAD_DIM, (kv + 1) * HEAD_DIM)
                ka_s[kv, T + j * CHUNK:T + (j + 1) * CHUNK, :] = ka_c[:, hd].astype(BF16)
                kwc_s[kv, cols, :] = kw_c[:, hd].astype(BF16)
                vat_s[n_chunks + j, kv, 0:HEAD_DIM, :] = cva_ref[kv, :, cols].astype(BF16)
                vat_s[n_chunks + j, kv, HEAD_DIM:VT_ROWS, :] = ones_tile
                vwct_s[kv, 0:HEAD_DIM, cols] = cvw_ref[kv, :, cols].astype(BF16)
                vwct_s[kv, HEAD_DIM:VT_ROWS, cols] = ones_tile

    tq = CHUNK
    scores_in_flight = 2 if lat else CHAINS_IN_FLIGHT
    m_cols = N_GROUP * tq

    def sink_row(kv):
        col = lax.broadcasted_iota(jnp.int32, (1, m_cols), 1)
        row = jnp.full((1, m_cols), sink_ref[layer * N_HEADS + kv * N_GROUP], F32)
        for g in range(1, N_GROUP):
            row = jnp.where(col >= g * tq, sink_ref[layer * N_HEADS + kv * N_GROUP + g], row)
        return row * LOG2E

    def load_q(q_s, kv, q0):
        return q_s[kv * N_GROUP:(kv + 1) * N_GROUP, pl.ds(q0, tq), :].reshape(m_cols, HEAD_DIM)

    def store_heads(dst, q0, kv, o_t):
        for p in range(N_GROUP // 2):
            blk = jnp.concatenate([o_t[:, (2 * p) * tq:(2 * p + 1) * tq],
                                   o_t[:, (2 * p + 1) * tq:(2 * p + 2) * tq]], axis=0)
            c0 = (kv * N_GROUP + 2 * p) * HEAD_DIM
            dst[pl.ds(q0, tq), c0:c0 + LANES] = jnp.transpose(blk).astype(dst.dtype)

    def key_tiles(k_ref, vt_ref, kv, n_key_chunks, chunks_per_tile):
        def tile(c0):
            def load():
                k = k_ref[kv, c0 * CHUNK:(c0 + chunks_per_tile) * CHUNK, :]
                v_t = [vt_ref[c0 + j, kv] for j in range(chunks_per_tile)]
                return k, (v_t[0] if len(v_t) == 1 else jnp.concatenate(v_t, axis=1)), None
            return load
        return [tile(c0) for c0 in range(0, n_key_chunks, chunks_per_tile)]

    def band_tiles(kv, n, q0):
        j0 = jnp.clip(n - 1, 0, n_chunks - 3)
        w0 = pl.multiple_of(j0 * CHUNK, CHUNK)

        def band(c0, nc):
            def load():
                rows = nc * CHUNK
                kpos = w0 + c0 * CHUNK + lax.broadcasted_iota(jnp.int32, (rows, m_cols), 0)
                qpos = q0 + (lax.broadcasted_iota(jnp.int32, (rows, m_cols), 1) & (tq - 1))
                v_t = [vwt_s[j0 + c0 + j, kv] for j in range(nc)]
                return (kw_s[kv, pl.ds(w0 + c0 * CHUNK, rows), :],
                        v_t[0] if nc == 1 else jnp.concatenate(v_t, axis=1),
                        jnp.abs(kpos - qpos) <= WINDOW)
            return load
        return [band(0, 2), band(2, 1), lambda: (kwc_s[kv], vwct_s[kv], None)]

    def attend(n, q0, chunks_per_tile):
        chains, dsts = [], []
        n_key_chunks = ka_s.shape[1] // CHUNK
        for kv in range(N_KV):
            chains.append((load_q(qa_s, kv, q0), key_tiles(ka_s, vat_s, kv, n_key_chunks, chunks_per_tile), None))
            dsts.append((ya_s, kv))
        for kv in range(N_KV):
            tiles = band_tiles(kv, n, q0) if lat else key_tiles(kw_s, vwt_s, kv, n_chunks, chunks_per_tile)
            chains.append((load_q(qw_s, kv, q0), tiles, sink_row(kv)))
            dsts.append((yw_s, kv))
        for g0 in range(0, len(chains), CHAINS_IN_FLIGHT):
            results = []
            yield from _attend_chains(chains[g0:g0 + CHAINS_IN_FLIGHT], results, scores_in_flight)
            for (dst, kv), o_t in zip(dsts[g0:g0 + CHAINS_IN_FLIGHT], results):
                store_heads(dst, q0, kv, o_t)

    n_pairs = ML_HEADS // 2
    hmt_s[...] = jnp.zeros_like(hmt_s)
    for d in range(2):
        if lat:
            for pr in range(n_pairs):
                c_s[d * n_pairs + pr, 0:ML_DV, :] = jnp.transpose(
                    jnp.concatenate([c0_ref[d, 2 * pr], c0_ref[d, 2 * pr + 1]], axis=0))
        for h in range(ML_HEADS):
            idx = d * n_pairs + h // 2
            hl = slice((h % 2) * ML_DK, (h % 2 + 1) * ML_DK)
            c_s[idx, ML_DV:ST_ROWS, hl] = jnp.zeros((BF16_ROWS, ML_DK), F32)
            if lat:
                c_s[idx, ML_DV:ML_DV + 1, hl] = n0_ref[d, h:h + 1, :]
                m_s[d * ML_HEADS + h:d * ML_HEADS + h + 1, :] = jnp.broadcast_to(
                    m0_ref[d:d + 1, h:h + 1], (1, LANES))
            else:
                c_s[idx, 0:ML_DV, hl] = jnp.zeros((ML_DV, ML_DK), F32)
                m_s[d * ML_HEADS + h:d * ML_HEADS + h + 1, :] = jnp.zeros((1, LANES), F32)

    L = CHUNK
    s_idx = lax.broadcasted_iota(jnp.int32, (L, ML_HEADS * L), 0)
    t_idx = lax.broadcasted_iota(jnp.int32, (L, ML_HEADS * L), 1) & (L - 1)
    lane_row = lax.broadcasted_iota(jnp.int32, (1, LANES), 1)
    low_half = lax.broadcasted_iota(jnp.int32, (L, LANES), 1) < ML_DK
    ones_ll = jnp.ones((L, L), BF16)
    zeros_ll = jnp.zeros((L, L), BF16)

    def heads_row(src, r0, c0=0):
        return jnp.concatenate([src[r0 + h:r0 + h + 1, c0:c0 + L] for h in range(ML_HEADS)], axis=1)

    def block_diag_rows(x):
        zero = jnp.zeros_like(x)
        return jnp.concatenate([jnp.where(low_half, x, zero), jnp.where(low_half, zero, x)], axis=0)

    def mlstm_streams(streams):
        m_state = {d: heads_row(m_s, d * ML_HEADS) for d in sorted({d for d, _ in streams})}
        c_state = {d: [c_s[d * n_pairs + pr] for pr in range(n_pairs)] for d in m_state}
        work = []
        for d, cc in streams:
            r0 = cc * L if isinstance(cc, int) else pl.multiple_of(cc * L, L)
            rows = pl.ds(r0, L)
            g = g_s[rows, :]
            f_hi, f_lo = _split(_log_sigmoid(g))
            tri_c = tril_ref[...] if d == 0 else triu_ref[...]
            tri_r = triu_ref[...] if d == 0 else tril_ref[...]
            cum = _dot(tri_c, f_hi) + _dot(tri_c, f_lo)
            r = g - pltpu.roll(cum, LANES - ML_HEADS, 1)
            gi0 = 2 * ML_HEADS * d
            rb = jnp.concatenate([jnp.broadcast_to(r[:, gi0 + h:gi0 + h + 1], (L, L))
                                  for h in range(ML_HEADS)], axis=1)
            g_t = jnp.transpose(g)[0:N_GATES, :]
            ft_hi, ft_lo = _split(_log_sigmoid(g_t))
            tr = jnp.concatenate([tri_r, ones_ll], axis=1)
            ct = _dot(ft_hi, tr) + _dot(ft_lo, tr)
            work.append(dict(d=d, cc=cc, rows=rows, rb=rb, g_t=g_t, ct=ct))
            yield
        for w in work:
            d = w["d"]
            valid = (s_idx <= t_idx) if d == 0 else (s_idx >= t_idx)
            gi0 = 2 * ML_HEADS * d
            gf0 = gi0 + ML_HEADS
            b_row = heads_row(w["ct"], gf0)
            b_last = heads_row(w["ct"], gf0, L)
            i_row = heads_row(w["g_t"], gi0)
            m_prev = m_state[d]
            a_row = b_row + m_prev
            dm = jnp.where(valid, b_row + w["rb"], -jnp.inf)
            mt = jnp.maximum(a_row, jnp.max(dm, axis=0, keepdims=True))
            w["p"] = jnp.exp(dm - mt)
            w["w_inter"] = jnp.exp(a_row - mt)
            w["floor"] = jnp.exp(-mt)
            g_row = b_last - b_row + i_row
            g_max = jnp.concatenate(
                [jnp.broadcast_to(jnp.max(g_row[:, h * L:(h + 1) * L], axis=-1, keepdims=True), (1, L))
                 for h in range(ML_HEADS)], axis=1)
            m_new = jnp.maximum(b_last + m_prev, g_max)
            w["ws"] = jnp.exp(g_row - m_new)
            w["wc"] = jnp.exp(b_last + m_prev - m_new)
            m_state[d] = m_new
            yield
        for w in work:
            rows, cc = w["rows"], w["cc"]
            w["pairs"] = []
            for pr in range(n_pairs):
                lanes = slice(pr * LANES, (pr + 1) * LANES)
                cols = slice(pr * 2 * L, (pr + 1) * 2 * L)
                k_pair = km_s[rows, lanes]
                q_bd = block_diag_rows(qm_s[rows, lanes])
                k_bd = block_diag_rows(k_pair)
                s_t = (_dot_tb(k_pair, q_bd) * w["p"][:, cols]).astype(BF16)
                s_bd = jnp.concatenate(
                    [jnp.concatenate([s_t[:, 0:L], zeros_ll], axis=1),
                     jnp.concatenate([zeros_ll, s_t[:, L:2 * L]], axis=1)], axis=0)
                v_t = jnp.concatenate([vmt_s[cc, 2 * pr], vmt_s[cc, 2 * pr + 1]], axis=1)
                intra = _dot(v_t, s_bd)
                wv = (v_t.astype(F32) * w["ws"][:, cols]).astype(BF16)
                w["pairs"].append((q_bd, intra, _dot(wv, k_bd)))
            yield
        for w in work:
            d, cc = w["d"], w["cc"]
            for pr, (q_bd, intra, update) in enumerate(w["pairs"]):
                cols = slice(pr * 2 * L, (pr + 1) * 2 * L)
                state = c_state[d][pr]
                tot = intra + w["w_inter"][:, cols] * _dot_tb(state.astype(BF16), q_bd)
                h_t = tot[0:ML_DV, :] / jnp.maximum(jnp.abs(tot[ML_DV:ML_DV + 1, :]), w["floor"][:, cols])
                for e in range(2):
                    hr = slice((2 * pr + e) * ML_DV, (2 * pr + e + 1) * ML_DV)
                    hmt_s[cc, hr, :] += h_t[:, e * L:(e + 1) * L]
                wc = w["wc"]
                wc_pair = jnp.where(lane_row < ML_DK, wc[:, 2 * pr * L:(2 * pr + 1) * L],
                                    wc[:, (2 * pr + 1) * L:(2 * pr + 2) * L])
                c_state[d][pr] = wc_pair * state + update
            yield
        for d in m_state:
            for h in range(ML_HEADS):
                m_s[d * ML_HEADS + h:d * ML_HEADS + h + 1, :] = m_state[d][:, h * L:(h + 1) * L]
            for pr in range(n_pairs):
                c_s[d * n_pairs + pr] = c_state[d][pr]

    def scan_streams(cs):
        return [(d, c if d == 0 else n_chunks - 1 - c) for c in cs for d in range(2)]

    if lat:
        def body2(n, carry):
            _run(_interleave([(attend(n, pl.multiple_of(n * tq, tq), 2), 1),
                              (mlstm_streams(scan_streams([n])), 1)]))
            return carry
        lax.fori_loop(0, n_chunks, body2, 0)
    else:
        def all_tiles():
            for n in range(T // tq):
                yield from attend(n, n * tq, 2)
        yield from _interleave([(all_tiles(), 1), (mlstm_streams(scan_streams(range(n_chunks))), 3)])

    if not lat:
        for d in range(2):
            for pr in range(n_pairs):
                c_t = jnp.transpose(c_s[d * n_pairs + pr, 0:ML_DV, :])
                for e in range(2):
                    c_o[d, 2 * pr + e] = c_t[e * ML_DK:(e + 1) * ML_DK, :]
            for h in range(ML_HEADS):
                idx = d * n_pairs + h // 2
                hl = slice((h % 2) * ML_DK, (h % 2 + 1) * ML_DK)
                n_o[d, h:h + 1, :] = c_s[idx, ML_DV:ML_DV + 1, hl]
                m_o[d:d + 1, h:h + 1] = m_s[d * ML_HEADS + h:d * ML_HEADS + h + 1, 0:1]

    def phase3_merge(i, out):
        rows = row_slice(i, 0, row_block)
        u = u_s[rows, :]

        def proj(c0, width):
            return _dot(u, w_ref[:, c0:c0 + width])

        hm = jnp.concatenate(
            [jnp.concatenate([jnp.transpose(hmt_s[i * chunks_per_block + j, h * ML_DV:(h + 1) * ML_DV, :])
                              for h in range(ML_HEADS)], axis=1)
             for j in range(chunks_per_block)], axis=0)
        ys = (ya_s[rows, :].astype(F32), om_s[rows, :].astype(F32) * hm, yw_s[rows, :].astype(F32))
        yield
        merged = None
        for b, (yb, zc) in enumerate(zip(ys, (C_ZA, C_ZM, C_ZW))):
            z = proj(zc, BRANCH_W)
            t = (yb * (z * _sigmoid(z))).astype(BF16)
            yield
            pb = _sigmoid(proj(C_GMERGE + b * D_MODEL, D_MODEL)) * _dot(t, wb_ref[b])
            merged = pb if merged is None else merged + pb
            yield
        out.append(_dot(merged.astype(BF16), wo_ref[...]))
        yield

    def phase3_norm(i, o):
        for j in range(0, row_block, norm_rows):
            piece = row_slice(i, j, norm_rows)
            hres = ALPHA * x_ref[piece, :] + gate * o[j:j + norm_rows, :]
            y_ref[piece, :] = _layer_norm(hres) * lng_ref[...] + lnb_ref[...]
            yield

    def phase3(i):
        box = []
        yield from phase3_merge(i, box)
        yield from phase3_norm(i, box[0])

    if not lat:
        box = []
        yield from _interleave([(phase3_merge(0, box), 1), (phase1_norm(0, x_next_ref, u_next_s), 1)])
        yield from phase3_norm(0, box[0])
    else:
        def body4(i, carry):
            _run(phase3(i))
            return carry
        lax.fori_loop(0, n_rows, body4, 0)


def _const_spec(shape):
    nd = len(shape)
    return pl.BlockSpec(shape, lambda b: (0,) * nd, pipeline_mode=pl.Buffered(1))


def _layer_spec(shape, layer):
    nd = len(shape) - 1
    return pl.BlockSpec((None,) + tuple(shape[1:]), lambda b: (layer,) + (0,) * nd,
                        pipeline_mode=pl.Buffered(1))


def _layer_call(lat, layer, x, mod, weights, consts, extra):
    B, T, _ = x.shape
    S = T + PAST_LEN if lat else T
    n_chunks = T // CHUNK

    seq_spec = pl.BlockSpec((None, T, D_MODEL), lambda b: (b, 0, 0))
    out_seq_spec = seq_spec
    if lat:
        mod_spec = pl.BlockSpec((None, None, 3, D_MODEL), lambda b: (layer, b + 1, 0, 0))
    else:
        mod_spec = pl.BlockSpec((None, None, 3, D_MODEL), lambda b: (layer, 0, 0, 0))
    *vmem_consts, sink = consts
    in_specs = [seq_spec, mod_spec]
    in_specs += [_layer_spec(a.shape, layer) for a in weights]
    in_specs += [_const_spec(a.shape) for a in vmem_consts]
    in_specs.append(pl.BlockSpec(memory_space=pltpu.SMEM))
    args = [x, mod, *weights, *vmem_consts, sink]

    if lat:
        cos, sa, sb, cka, cva, ckw, cvw, sc, sn, sm = extra
        in_specs += [_const_spec(cos.shape)] * 3
        kv_spec = pl.BlockSpec((None, None, N_KV, HEAD_DIM, PAST_LEN), lambda b: (b, layer, 0, 0, 0))
        in_specs += [kv_spec] * 4
        in_specs += [
            pl.BlockSpec((None, None, 2, ML_HEADS, ML_DK, ML_DV), lambda b: (b, layer, 0, 0, 0, 0)),
            pl.BlockSpec((None, None, 2, ML_HEADS, ML_DK), lambda b: (b, layer, 0, 0, 0)),
            pl.BlockSpec((None, None, 2, ML_HEADS), lambda b: (b, layer, 0, 0)),
        ]
        args += [cos, sa, sb, cka, cva, ckw, cvw, sc, sn, sm]
        out_specs = out_seq_spec
        out_shape = jax.ShapeDtypeStruct((B, T, D_MODEL), F32)
        aliases = {}
    else:
        kv_shape = (B, DEPTH, N_KV, HEAD_DIM, T)
        c_shape = (B, DEPTH, 2, ML_HEADS, ML_DK, ML_DV)
        if layer == 0:
            aliases = {}
            ld = DEPTH
            layer_idx = 0
        else:
            first_acc = len(args)
            in_specs += [pl.BlockSpec(memory_space=pl.ANY)] * len(extra)
            args += list(extra)
            aliases = {first_acc + i: 1 + i for i in range(len(extra))}
            assert len(extra) == CTX_ALIASED
            ld = None
            layer_idx = layer
        in_specs.append(pl.BlockSpec((None, T, D_MODEL), lambda b: (jnp.minimum(b + 1, B - 1), 0, 0)))
        args.append(x)
        kv_out = pl.BlockSpec((None, ld, N_KV, HEAD_DIM, T), lambda b: (b, layer_idx, 0, 0, 0))
        out_specs = [
            seq_spec, kv_out, kv_out, kv_out, kv_out,
            pl.BlockSpec((None, ld, 2, ML_HEADS, ML_DK, ML_DV), lambda b: (b, layer_idx, 0, 0, 0, 0)),
            pl.BlockSpec((None, 2, ML_HEADS, ML_DK), lambda b: (b, 0, 0, 0)),
            pl.BlockSpec((None, 2, ML_HEADS), lambda b: (b, 0, 0)),
        ]
        out_shape = [jax.ShapeDtypeStruct((B, T, D_MODEL), F32)]
        out_shape += [jax.ShapeDtypeStruct(kv_shape, F32)] * 4 + [jax.ShapeDtypeStruct(c_shape, F32)]
        out_shape += [
            jax.ShapeDtypeStruct((B, 2, ML_HEADS, ML_DK), F32),
            jax.ShapeDtypeStruct((B, 2, ML_HEADS), F32),
        ]

    scratch = [
        ((T, D_MODEL) if lat else (2, T, D_MODEL), BF16),
        ((N_HEADS, T, HEAD_DIM), BF16),
        ((N_HEADS, T, HEAD_DIM), BF16),
        ((N_KV, S, HEAD_DIM), BF16),
        ((S // CHUNK, N_KV, VT_ROWS, CHUNK), BF16),
        ((N_KV, T, HEAD_DIM), BF16),
        ((n_chunks, N_KV, VT_ROWS, CHUNK), BF16),
        ((T, ML_HEADS * ML_DK), BF16),
        ((T, ML_HEADS * ML_DK), BF16),
        ((n_chunks, ML_HEADS, ST_ROWS, CHUNK), BF16),
        ((T, ML_HEADS * ML_DV), BF16),
        ((T, LANES), F32),
        ((T, BRANCH_W), BF16),
        ((T, BRANCH_W), BF16),
        ((n_chunks, ML_HEADS * ML_DV, CHUNK), F32),
        ((ML_HEADS, ST_ROWS, 2 * ML_DK), F32),
        ((2 * ML_HEADS, LANES), F32),
    ]
    if lat:
        scratch += [((N_KV, PAST_LEN, HEAD_DIM), BF16),
                    ((N_KV, VT_ROWS, PAST_LEN), BF16)]
    scratch = [pltpu.VMEM(shape, dtype) for shape, dtype in scratch]

    return pl.pallas_call(
        functools.partial(_layer_kernel, lat, T, layer),
        grid=(B,),
        in_specs=in_specs,
        out_specs=out_specs,
        out_shape=out_shape,
        scratch_shapes=scratch,
        input_output_aliases=aliases,
        compiler_params=pltpu.CompilerParams(
            dimension_semantics=("arbitrary",),
            vmem_limit_bytes=LATENT_VMEM_LIMIT_BYTES if lat else VMEM_LIMIT_BYTES),
        name=("latent_layer" if lat else "context_layer") + str(layer),
    )(*args)


def _rope_tables(T):
    f32 = np.float32
    rows = T // GRID_W
    row = np.repeat(np.arange(rows, dtype=f32), GRID_W)
    col = np.tile(np.arange(GRID_W, dtype=f32), rows)
    inv = (f32(ROPE_THETA) ** (-np.arange(0, AXIS_DIM, 2, dtype=f32) / f32(AXIS_DIM))).astype(f32)
    ar = row[:, None] * inv
    ac = col[:, None] * inv
    ang = np.concatenate([ar, ar, ac, ac], axis=-1)
    ang = np.concatenate([ang, ang], axis=-1)
    first_half = (np.arange(LANES) % AXIS_DIM) < (AXIS_DIM // 2)
    cos = np.cos(ang).astype(f32)
    sin = np.sin(ang).astype(f32)
    sa = np.where(first_half[None, :], -sin, f32(0.0))
    sb = np.where(first_half[None, :], f32(0.0), sin)
    return jnp.asarray(cos), jnp.asarray(sa), jnp.asarray(sb)


def _constants(sink_logit):
    grp = np.arange(N_HEADS * HEAD_DIM) // HEAD_DIM
    bd = jnp.asarray(grp[:, None] == grp[None, :], dtype=BF16)
    ti = np.arange(CHUNK)
    tril = jnp.asarray(ti[None, :] <= ti[:, None], dtype=BF16)
    triu = jnp.asarray(ti[None, :] >= ti[:, None], dtype=BF16)
    return bd, tril, triu, sink_logit.reshape(-1)


def kernel(x_prompt, x_sample, cache_attn_k, cache_attn_v, cache_win_k, cache_win_v, state_mlstm_C,
           state_mlstm_n, state_mlstm_m, c, c_ctx, w_mod, b_mod, w_in, qk_gain, sink_logit,
           mlstm_gate_bias, w_branch, w_out, ln_gain, ln_bias):
    dec_b = x_sample.shape[0]
    assert dec_b + 1 <= 8
    cond = jnp.concatenate([c_ctx[None, :], c, jnp.zeros((8 - 1 - dec_b, D_MODEL), F32)], axis=0)
    mod = _modulation(cond, w_mod, b_mod).reshape(DEPTH, 8, 3, D_MODEL)

    consts = _constants(sink_logit)
    cos, sa, sb = _rope_tables(x_sample.shape[1])
    lane_pad = jnp.zeros((DEPTH, 1, LANES - N_GATES), F32)
    weights = (
        _prep_in_weight(jnp.swapaxes(w_in, 1, 2)), w_branch.astype(BF16), w_out.astype(BF16),
        jnp.tile(qk_gain[:, 0:1, :], (1, 1, N_HEADS)),
        jnp.tile(qk_gain[:, 1:2, :], (1, 1, N_KV)),
        jnp.concatenate([mlstm_gate_bias.reshape(DEPTH, 1, N_GATES), lane_pad], axis=2),
        ln_gain[:, None, :], ln_bias[:, None, :],
    )

    ctx = None
    cached_t = [jnp.swapaxes(a, -1, -2)
                for a in (cache_attn_k, cache_attn_v, cache_win_k, cache_win_v, state_mlstm_C)]
    xp, xs = x_prompt, x_sample
    small = []
    for l in range(DEPTH):
        outs = _layer_call(False, l, xp, mod, weights, consts, ctx)
        xp = outs[0]
        ctx = tuple(outs[1:6])
        small.append(outs[6:])
        extra = (cos, sa, sb, *cached_t, state_mlstm_n, state_mlstm_m)
        xs = _layer_call(True, l, xs, mod, weights, consts, extra)
    new_n, new_m = [jnp.stack([small[l][i] for l in range(DEPTH)], axis=1) for i in range(2)]
    return (xp, xs, *[jnp.swapaxes(a, -1, -2) for a in ctx], new_n, new_m)
```

```python
import functools

import jax
import jax.numpy as jnp
import numpy as np
from jax import lax
from jax.experimental import pallas as pl
from jax.experimental.pallas import tpu as pltpu

F32 = jnp.float32
BF16 = jnp.bfloat16

D_MODEL = 1024
DEPTH = 2
PAST_LEN = 256
GRID_W = 64
HEAD_DIM = 64
N_HEADS = 8
N_KV = 2
N_GROUP = N_HEADS // N_KV
WINDOW = 128
ML_HEADS = 4
ML_DK = 64
ML_DV = 128
CHUNK = 128
BRANCH_W = 512
ROPE_THETA = 10000.0
AXIS_DIM = HEAD_DIM // 2
LN_EPS = 1e-6
RMS_EPS = 1e-6
ALPHA = (2.0 * DEPTH) ** 0.25
LOG2E = float(np.log2(np.e))
LANES = 128
BF16_ROWS = 16
ROW_BLOCK = 256
CHAINS_IN_FLIGHT = 4
CTX_ALIASED = 5
N_SHARED = 14
VMEM_LIMIT_BYTES = 60 * 1024 * 1024
LATENT_VMEM_LIMIT_BYTES = 62 * 1024 * 1024
VT_ROWS = HEAD_DIM + BF16_ROWS
ST_ROWS = ML_DV + BF16_ROWS

_SIZES = (512, 128, 128, 512, 256, 256, 512, 512, 128, 512, 512, 128, 128, 512, 3072)
_OFF = [int(v) for v in np.concatenate([[0], np.cumsum(_SIZES)])]
(C_QA, C_KA, C_VA, C_ZA, C_QM, C_KM, C_VM, C_OM, C_GT, C_ZM, C_QW, C_KW, C_VW, C_ZW, C_GMERGE) = _OFF[:-1]
W_COLS = _OFF[-1]
GATE_COL = 2816
N_GATES = 4 * ML_HEADS


def _dot(a, b):
    return jnp.dot(a, b, preferred_element_type=F32)


def _dot_tb(a, b):
    return lax.dot_general(a, b, (((1,), (1,)), ((), ())), preferred_element_type=F32)


def _split(a):
    hi = a.astype(BF16)
    lo = (a - hi.astype(F32)).astype(BF16)
    return hi, lo


def _layer_norm(x):
    mu = jnp.mean(x, axis=-1, keepdims=True)
    xc = x - mu
    var = jnp.mean(xc * xc, axis=-1, keepdims=True)
    return xc * lax.rsqrt(var + LN_EPS)


def _log_sigmoid(x):
    return jnp.minimum(x, 0.0) - jnp.log(1.0 + jnp.exp(-jnp.abs(x)))


def _sigmoid(x):
    return jax.nn.sigmoid(x)


def _mod_kernel(c_ref, w_ref, b_ref, o_ref):
    c = c_ref[...]
    s = c * _sigmoid(c)
    s_hi, s_lo = _split(s)
    w = w_ref[...]
    w_hi, w_lo = _split(w)
    o_ref[...] = _dot(s_hi, w_hi) + _dot(s_lo, w_hi) + _dot(s_hi, w_lo) + b_ref[...]


def _modulation(cond, w_mod, b_mod):
    rows = cond.shape[0]
    tn = 1536
    return pl.pallas_call(
        _mod_kernel,
        grid=(DEPTH, 3 * D_MODEL // tn),
        in_specs=[
            pl.BlockSpec((rows, D_MODEL), lambda l, j: (0, 0)),
            pl.BlockSpec((None, D_MODEL, tn), lambda l, j: (l, 0, j)),
            pl.BlockSpec((None, 1, tn), lambda l, j: (l, 0, j)),
        ],
        out_specs=pl.BlockSpec((None, rows, tn), lambda l, j: (l, 0, j)),
        out_shape=jax.ShapeDtypeStruct((DEPTH, rows, 3 * D_MODEL), F32),
        compiler_params=pltpu.CompilerParams(dimension_semantics=("arbitrary", "arbitrary")),
        name="adaln_modulation",
    )(cond, w_mod, b_mod.reshape(DEPTH, 1, 3 * D_MODEL))


PREP_K = 256


def _prep_kernel(wt_ref, o_ref):
    lane = lax.broadcasted_iota(jnp.int32, (PREP_K, LANES), 1)
    for c0 in range(0, W_COLS, LANES):
        r0 = c0 if c0 <= GATE_COL else c0 - (LANES - N_GATES)
        slab = jnp.transpose(wt_ref[r0:r0 + LANES, :])
        if c0 == GATE_COL:
            slab = jnp.where(lane < N_GATES, slab, 0.0)
        o_ref[:, c0:c0 + LANES] = slab.astype(BF16)


def _prep_in_weight(w_in_t):
    n_in = w_in_t.shape[1]
    return pl.pallas_call(
        _prep_kernel,
        grid=(DEPTH, D_MODEL // PREP_K),
        in_specs=[pl.BlockSpec((None, n_in, PREP_K), lambda l, i: (l, 0, i))],
        out_specs=pl.BlockSpec((None, PREP_K, W_COLS), lambda l, i: (l, i, 0)),
        out_shape=jax.ShapeDtypeStruct((DEPTH, D_MODEL, W_COLS), BF16),
        compiler_params=pltpu.CompilerParams(
            dimension_semantics=("arbitrary", "arbitrary"), vmem_limit_bytes=VMEM_LIMIT_BYTES),
        name="in_weight_prep",
    )(w_in_t)


def _rope(x, cos, sa, sb):
    return x * cos + pltpu.roll(x, LANES - AXIS_DIM // 2, 1) * sa + pltpu.roll(x, AXIS_DIM // 2, 1) * sb


def _group_rms(x, bd, gain):
    ms = _dot((x * x).astype(BF16), bd) * (1.0 / HEAD_DIM)
    return x * lax.rsqrt(ms + RMS_EPS) * gain


def _ones_row_tile():
    r = lax.broadcasted_iota(jnp.int32, (BF16_ROWS, LANES), 0)
    return jnp.where(r == 0, 1.0, 0.0).astype(BF16)


def _attend_chains(chains, results, depth):
    state = []
    for qs, tiles, sink_row in chains:
        m_cols = qs.shape[0]
        if sink_row is None:
            m = jnp.full((1, m_cols), -jnp.inf, F32)
            acc = jnp.zeros((VT_ROWS, m_cols), F32)
        else:
            m = sink_row
            r = lax.broadcasted_iota(jnp.int32, (VT_ROWS, m_cols), 0)
            acc = jnp.where(r == HEAD_DIM, 1.0, 0.0)
        state.append([m, acc])
    order = [(t, c) for t in range(max(len(tiles) for _, tiles, _ in chains))
             for c, (_, tiles, _) in enumerate(chains) if t < len(tiles)]

    def issue(t, c):
        qs, tiles, _ = chains[c]
        tile = tiles[t]()
        return tile, _dot_tb(tile[0], qs)

    pending = [issue(*item) for item in order[:depth]]
    yield
    for k, (t, c) in enumerate(order):
        (_, v_t, valid), s = pending.pop(0)
        if k + depth < len(order):
            pending.append(issue(*order[k + depth]))
        if valid is not None:
            s = jnp.where(valid, s, -jnp.inf)
        m, acc = state[c]
        m_new = jnp.maximum(m, jnp.max(s, axis=0, keepdims=True))
        p = jnp.exp2(s - m_new).astype(BF16)
        state[c] = [m_new, acc * jnp.exp2(m - m_new) + _dot(v_t, p)]
        if k + 1 == len(order) or order[k + 1][0] != t:
            yield
    results.extend(acc[0:HEAD_DIM, :] / acc[HEAD_DIM:HEAD_DIM + 1, :] for _, acc in state)


def _interleave(weighted):
    live = [[g, n] for g, n in weighted]
    while live:
        for item in list(live):
            for _ in range(item[1]):
                try:
                    next(item[0])
                except StopIteration:
                    live.remove(item)
                    break
        yield


def _run(gen):
    for _ in gen:
        pass


def _layer_kernel(lat, T, layer, *refs):
    _run(_sequence(lat, T, layer, *refs))


def _sequence(lat, T, layer, *refs):
    refs = list(refs)
    x_ref, mod_ref, w_ref, wb_hbm, wo_hbm = refs[:5]
    g0_ref, g1_ref, gb_ref, lng_ref, lnb_ref = refs[5:10]
    bd_ref, tril_ref, triu_ref, sink_ref = refs[10:N_SHARED]
    pos = N_SHARED
    if lat:
        cos_ref, sa_ref, sb_ref = refs[pos:pos + 3]
        cka_ref, cva_ref, ckw_ref, cvw_ref, c0_ref, n0_ref, m0_ref = refs[pos + 3:pos + 10]
        pos += 10
        w_hbm = refs[pos]
        y_ref = refs[pos + 1]
        pos += 2
    else:
        if layer > 0:
            pos += CTX_ALIASED
        x_next_ref = refs[pos]
        w_hbm = refs[pos + 1]
        pos += 2
        y_ref, ka_o, va_o, kw_o, vw_o, c_o, n_o, m_o = refs[pos:pos + 8]
        pos += 8
        if layer == 0:
            for full in (ka_o, va_o, kw_o, vw_o, c_o):
                full[1:] = jnp.zeros((DEPTH - 1,) + full.shape[1:], F32)
            ka_o, va_o, kw_o, vw_o, c_o = (full.at[0] for full in (ka_o, va_o, kw_o, vw_o, c_o))
    (u_s, qa_s, qw_s, ka_s, vat_s, kw_s, vwt_s, qm_s, km_s, vmt_s, om_s, g_s,
     ya_s, yw_s, hmt_s, c_s, m_s) = refs[pos:pos + 17]
    pos += 17
    if not lat:
        step = pl.program_id(0)
        u_next_s = u_s.at[(step + 1) % 2]
        u_s = u_s.at[step % 2]
    if lat:
        kwc_s, vwct_s = refs[pos:pos + 2]
        pos += 2
    wt_s, wb_ref, wo_ref, w_sem = refs[pos:pos + 4]

    first_step = pl.program_id(0) == 0
    late_copies = (
        pltpu.make_async_copy(w_hbm.at[layer, :, C_ZW:W_COLS], wt_s, w_sem.at[0]),
        pltpu.make_async_copy(wb_hbm.at[layer], wb_ref, w_sem.at[1]),
        pltpu.make_async_copy(wo_hbm.at[layer], wo_ref, w_sem.at[2]),
    )

    @pl.when(first_step)
    def _():
        for cp in late_copies:
            cp.start()

    row_block = min(T, ROW_BLOCK)
    n_rows = T // row_block
    n_chunks = T // CHUNK
    chunks_per_block = row_block // CHUNK
    shift = mod_ref[0:1, :]
    scale = mod_ref[1:2, :]
    gate = mod_ref[2:3, :]
    ones_tile = _ones_row_tile()

    norm_rows = 64

    def row_slice(i, j, n):
        r0 = i * row_block + j
        return pl.ds(r0 if isinstance(r0, int) else pl.multiple_of(r0, n), n)

    def phase1_norm(i, src=x_ref, dst=u_s):
        for j in range(0, row_block, norm_rows):
            piece = row_slice(i, j, norm_rows)
            dst[piece, :] = (_layer_norm(src[piece, :]) * (1.0 + scale) + shift).astype(BF16)
            yield

    def phase1_proj(i):
        R = row_block
        rows = row_slice(i, 0, R)
        u = u_s[rows, :]

        def proj(c0, width):
            return _dot(u, w_ref[:, c0:c0 + width])

        if lat:
            cos = cos_ref[rows, :]
            sa = sa_ref[rows, :]
            sb = sb_ref[rows, :]

        def rope_wide(v):
            if not lat:
                return v
            slabs = [_rope(v[:, j:j + LANES], cos, sa, sb) for j in range(0, v.shape[1], LANES)]
            return slabs[0] if len(slabs) == 1 else jnp.concatenate(slabs, axis=1)

        qscale = LOG2E * HEAD_DIM ** -0.5
        qa = rope_wide(_group_rms(proj(C_QA, 512), bd_ref[...], g0_ref[...])) * qscale
        for h in range(N_HEADS):
            qa_s[h, rows, :] = qa[:, h * HEAD_DIM:(h + 1) * HEAD_DIM].astype(BF16)
        yield
        ka_n = _group_rms(proj(C_KA, LANES), bd_ref[0:LANES, 0:LANES], g1_ref[...])
        ka = rope_wide(ka_n)
        va = proj(C_VA, LANES)
        qw = rope_wide(proj(C_QW, 512)) * qscale
        for h in range(N_HEADS):
            qw_s[h, rows, :] = qw[:, h * HEAD_DIM:(h + 1) * HEAD_DIM].astype(BF16)
        yield
        kw_raw = proj(C_KW, LANES)
        kw = rope_wide(kw_raw)
        vw = proj(C_VW, LANES)
        for kv in range(N_KV):
            sl = slice(kv * HEAD_DIM, (kv + 1) * HEAD_DIM)
            ka_s[kv, rows, :] = ka[:, sl].astype(BF16)
            kw_s[kv, rows, :] = kw[:, sl].astype(BF16)
        yield
        qm_s[rows, :] = proj(C_QM, 256).astype(BF16)
        km_s[rows, :] = (proj(C_KM, 256) * (ML_DK ** -0.5)).astype(BF16)
        yield
        vm = proj(C_VM, 512)
        yield
        om_s[rows, :] = _sigmoid(proj(C_OM, 512)).astype(BF16)
        g_s[rows, :] = proj(C_GT, LANES) + gb_ref[...]
        yield
        for j in range(chunks_per_block):
            cj = i * chunks_per_block + j
            cr = slice(j * CHUNK, (j + 1) * CHUNK)
            va_t = jnp.transpose(va[cr, :])
            vw_t = jnp.transpose(vw[cr, :])
            for kv in range(N_KV):
                hd = slice(kv * HEAD_DIM, (kv + 1) * HEAD_DIM)
                vat_s[cj, kv, 0:HEAD_DIM, :] = va_t[hd, :].astype(BF16)
                vat_s[cj, kv, HEAD_DIM:VT_ROWS, :] = ones_tile
                vwt_s[cj, kv, 0:HEAD_DIM, :] = vw_t[hd, :].astype(BF16)
                vwt_s[cj, kv, HEAD_DIM:VT_ROWS, :] = ones_tile
            if not lat:
                tc = slice(i * row_block + j * CHUNK, i * row_block + (j + 1) * CHUNK)
                ka_t = jnp.transpose(ka_n[cr, :])
                kw_t = jnp.transpose(kw_raw[cr, :])
                for kv in range(N_KV):
                    hd = slice(kv * HEAD_DIM, (kv + 1) * HEAD_DIM)
                    ka_o[kv, :, tc] = ka_t[hd, :]
                    va_o[kv, :, tc] = va_t[hd, :]
                    kw_o[kv, :, tc] = kw_t[hd, :]
                    vw_o[kv, :, tc] = vw_t[hd, :]
            for h in range(ML_HEADS):
                vmt_s[cj, h, 0:ML_DV, :] = jnp.transpose(vm[cr, h * ML_DV:(h + 1) * ML_DV]).astype(BF16)
                vmt_s[cj, h, ML_DV:ST_ROWS, :] = ones_tile
            yield

    if not lat:
        @pl.when(step == 0)
        def _():
            _run(phase1_norm(0))
        yield from phase1_proj(0)
    else:
        def body1(i, carry):
            _run(phase1_norm(i))
            _run(phase1_proj(i))
            return carry
        lax.fori_loop(0, n_rows, body1, 0)

    if lat:
        for j in range(PAST_LEN // CHUNK):
            cols = slice(j * CHUNK, (j + 1) * CHUNK)
            ka_c = jnp.transpose(jnp.concatenate([cka_ref[kv, :, cols] for kv in range(N_KV)], axis=0))
            kw_c = jnp.transpose(jnp.concatenate([ckw_ref[kv, :, cols] for kv in range(N_KV)], axis=0))
            for kv in range(N_KV):
                hd = slice(kv * HEAD_DIM, (kv + 1) * HEAD_DIM)
                ka_s[kv, T + j * CHUNK:T + (j + 1) * CHUNK, :] = ka_c[:, hd].astype(BF16)
                kwc_s[kv, cols, :] = kw_c[:, hd].astype(BF16)
                vat_s[n_chunks + j, kv, 0:HEAD_DIM, :] = cva_ref[kv, :, cols].astype(BF16)
                vat_s[n_chunks + j, kv, HEAD_DIM:VT_ROWS, :] = ones_tile
                vwct_s[kv, 0:HEAD_DIM, cols] = cvw_ref[kv, :, cols].astype(BF16)
                vwct_s[kv, HEAD_DIM:VT_ROWS, cols] = ones_tile

    tq = CHUNK
    scores_in_flight = 2 if lat else CHAINS_IN_FLIGHT
    m_cols = N_GROUP * tq

    def sink_row(kv):
        col = lax.broadcasted_iota(jnp.int32, (1, m_cols), 1)
        row = jnp.full((1, m_cols), sink_ref[layer * N_HEADS + kv * N_GROUP], F32)
        for g in range(1, N_GROUP):
            row = jnp.where(col >= g * tq, sink_ref[layer * N_HEADS + kv * N_GROUP + g], row)
        return row * LOG2E

    def load_q(q_s, kv, q0):
        return q_s[kv * N_GROUP:(kv + 1) * N_GROUP, pl.ds(q0, tq), :].reshape(m_cols, HEAD_DIM)

    def store_heads(dst, q0, kv, o_t):
        for p in range(N_GROUP // 2):
            blk = jnp.concatenate([o_t[:, (2 * p) * tq:(2 * p + 1) * tq],
                                   o_t[:, (2 * p + 1) * tq:(2 * p + 2) * tq]], axis=0)
            c0 = (kv * N_GROUP + 2 * p) * HEAD_DIM
            dst[pl.ds(q0, tq), c0:c0 + LANES] = jnp.transpose(blk).astype(dst.dtype)

    def key_tiles(k_ref, vt_ref, kv, n_key_chunks, chunks_per_tile):
        def tile(c0):
            def load():
                k = k_ref[kv, c0 * CHUNK:(c0 + chunks_per_tile) * CHUNK, :]
                v_t = [vt_ref[c0 + j, kv] for j in range(chunks_per_tile)]
                return k, (v_t[0] if len(v_t) == 1 else jnp.concatenate(v_t, axis=1)), None
            return load
        return [tile(c0) for c0 in range(0, n_key_chunks, chunks_per_tile)]

    def band_tiles(kv, n, q0):
        j0 = jnp.clip(n - 1, 0, n_chunks - 3)
        w0 = pl.multiple_of(j0 * CHUNK, CHUNK)

        def band(c0, nc):
            def load():
                rows = nc * CHUNK
                kpos = w0 + c0 * CHUNK + lax.broadcasted_iota(jnp.int32, (rows, m_cols), 0)
                qpos = q0 + (lax.broadcasted_iota(jnp.int32, (rows, m_cols), 1) & (tq - 1))
                v_t = [vwt_s[j0 + c0 + j, kv] for j in range(nc)]
                return (kw_s[kv, pl.ds(w0 + c0 * CHUNK, rows), :],
                        v_t[0] if nc == 1 else jnp.concatenate(v_t, axis=1),
                        jnp.abs(kpos - qpos) <= WINDOW)
            return load
        return [band(0, 2), band(2, 1), lambda: (kwc_s[kv], vwct_s[kv], None)]

    def attend(n, q0, chunks_per_tile):
        chains, dsts = [], []
        n_key_chunks = ka_s.shape[1] // CHUNK
        for kv in range(N_KV):
            chains.append((load_q(qa_s, kv, q0), key_tiles(ka_s, vat_s, kv, n_key_chunks, chunks_per_tile), None))
            dsts.append((ya_s, kv))
        for kv in range(N_KV):
            tiles = band_tiles(kv, n, q0) if lat else key_tiles(kw_s, vwt_s, kv, n_chunks, chunks_per_tile)
            chains.append((load_q(qw_s, kv, q0), tiles, sink_row(kv)))
            dsts.append((yw_s, kv))
        for g0 in range(0, len(chains), CHAINS_IN_FLIGHT):
            results = []
            yield from _attend_chains(chains[g0:g0 + CHAINS_IN_FLIGHT], results, scores_in_flight)
            for (dst, kv), o_t in zip(dsts[g0:g0 + CHAINS_IN_FLIGHT], results):
                store_heads(dst, q0, kv, o_t)

    n_pairs = ML_HEADS // 2
    hmt_s[...] = jnp.zeros_like(hmt_s)
    for d in range(2):
        if lat:
            for pr in range(n_pairs):
                c_s[d * n_pairs + pr, 0:ML_DV, :] = jnp.transpose(
                    jnp.concatenate([c0_ref[d, 2 * pr], c0_ref[d, 2 * pr + 1]], axis=0))
        for h in range(ML_HEADS):
            idx = d * n_pairs + h // 2
            hl = slice((h % 2) * ML_DK, (h % 2 + 1) * ML_DK)
            c_s[idx, ML_DV:ST_ROWS, hl] = jnp.zeros((BF16_ROWS, ML_DK), F32)
            if lat:
                c_s[idx, ML_DV:ML_DV + 1, hl] = n0_ref[d, h:h + 1, :]
                m_s[d * ML_HEADS + h:d * ML_HEADS + h + 1, :] = jnp.broadcast_to(
                    m0_ref[d:d + 1, h:h + 1], (1, LANES))
            else:
                c_s[idx, 0:ML_DV, hl] = jnp.zeros((ML_DV, ML_DK), F32)
                m_s[d * ML_HEADS + h:d * ML_HEADS + h + 1, :] = jnp.zeros((1, LANES), F32)

    L = CHUNK
    s_idx = lax.broadcasted_iota(jnp.int32, (L, ML_HEADS * L), 0)
    t_idx = lax.broadcasted_iota(jnp.int32, (L, ML_HEADS * L), 1) & (L - 1)
    lane_row = lax.broadcasted_iota(jnp.int32, (1, LANES), 1)
    low_half = lax.broadcasted_iota(jnp.int32, (L, LANES), 1) < ML_DK
    ones_ll = jnp.ones((L, L), BF16)
    zeros_ll = jnp.zeros((L, L), BF16)

    def heads_row(src, r0, c0=0):
        return jnp.concatenate([src[r0 + h:r0 + h + 1, c0:c0 + L] for h in range(ML_HEADS)], axis=1)

    def block_diag_rows(x):
        zero = jnp.zeros_like(x)
        return jnp.concatenate([jnp.where(low_half, x, zero), jnp.where(low_half, zero, x)], axis=0)

    def mlstm_streams(streams):
        m_state = {d: heads_row(m_s, d * ML_HEADS) for d in sorted({d for d, _ in streams})}
        c_state = {d: [c_s[d * n_pairs + pr] for pr in range(n_pairs)] for d in m_state}
        work = []
        for d, cc in streams:
            r0 = cc * L if isinstance(cc, int) else pl.multiple_of(cc * L, L)
            rows = pl.ds(r0, L)
            g = g_s[rows, :]
            f_hi, f_lo = _split(_log_sigmoid(g))
            tri_c = tril_ref[...] if d == 0 else triu_ref[...]
            tri_r = triu_ref[...] if d == 0 else tril_ref[...]
            cum = _dot(tri_c, f_hi) + _dot(tri_c, f_lo)
            r = g - pltpu.roll(cum, LANES - ML_HEADS, 1)
            gi0 = 2 * ML_HEADS * d
            rb = jnp.concatenate([jnp.broadcast_to(r[:, gi0 + h:gi0 + h + 1], (L, L))
                                  for h in range(ML_HEADS)], axis=1)
            g_t = jnp.transpose(g)[0:N_GATES, :]
            ft_hi, ft_lo = _split(_log_sigmoid(g_t))
            tr = jnp.concatenate([tri_r, ones_ll], axis=1)
            ct = _dot(ft_hi, tr) + _dot(ft_lo, tr)
            work.append(dict(d=d, cc=cc, rows=rows, rb=rb, g_t=g_t, ct=ct))
            yield
        for w in work:
            d = w["d"]
            valid = (s_idx <= t_idx) if d == 0 else (s_idx >= t_idx)
            gi0 = 2 * ML_HEADS * d
            gf0 = gi0 + ML_HEADS
            b_row = heads_row(w["ct"], gf0)
            b_last = heads_row(w["ct"], gf0, L)
            i_row = heads_row(w["g_t"], gi0)
            m_prev = m_state[d]
            a_row = b_row + m_prev
            dm = jnp.where(valid, b_row + w["rb"], -jnp.inf)
            mt = jnp.maximum(a_row, jnp.max(dm, axis=0, keepdims=True))
            w["p"] = jnp.exp(dm - mt)
            w["w_inter"] = jnp.exp(a_row - mt)
            w["floor"] = jnp.exp(-mt)
            g_row = b_last - b_row + i_row
            g_max = jnp.concatenate(
                [jnp.broadcast_to(jnp.max(g_row[:, h * L:(h + 1) * L], axis=-1, keepdims=True), (1, L))
                 for h in range(ML_HEADS)], axis=1)
            m_new = jnp.maximum(b_last + m_prev, g_max)
            w["ws"] = jnp.exp(g_row - m_new)
            w["wc"] = jnp.exp(b_last + m_prev - m_new)
            m_state[d] = m_new
            yield
        for w in work:
            rows, cc = w["rows"], w["cc"]
            w["pairs"] = []
            for pr in range(n_pairs):
                lanes = slice(pr * LANES, (pr + 1) * LANES)
                cols = slice(pr * 2 * L, (pr + 1) * 2 * L)
                k_pair = km_s[rows, lanes]
                q_bd = block_diag_rows(qm_s[rows, lanes])
                k_bd = block_diag_rows(k_pair)
                s_t = (_dot_tb(k_pair, q_bd) * w["p"][:, cols]).astype(BF16)
                s_bd = jnp.concatenate(
                    [jnp.concatenate([s_t[:, 0:L], zeros_ll], axis=1),
                     jnp.concatenate([zeros_ll, s_t[:, L:2 * L]], axis=1)], axis=0)
                v_t = jnp.concatenate([vmt_s[cc, 2 * pr], vmt_s[cc, 2 * pr + 1]], axis=1)
                intra = _dot(v_t, s_bd)
                wv = (v_t.astype(F32) * w["ws"][:, cols]).astype(BF16)
                w["pairs"].append((q_bd, intra, _dot(wv, k_bd)))
            yield
        for w in work:
            d, cc = w["d"], w["cc"]
            for pr, (q_bd, intra, update) in enumerate(w["pairs"]):
                cols = slice(pr * 2 * L, (pr + 1) * 2 * L)
                state = c_state[d][pr]
                tot = intra + w["w_inter"][:, cols] * _dot_tb(state.astype(BF16), q_bd)
                h_t = tot[0:ML_DV, :] / jnp.maximum(jnp.abs(tot[ML_DV:ML_DV + 1, :]), w["floor"][:, cols])
                for e in range(2):
                    hr = slice((2 * pr + e) * ML_DV, (2 * pr + e + 1) * ML_DV)
                    hmt_s[cc, hr, :] += h_t[:, e * L:(e + 1) * L]
                wc = w["wc"]
                wc_pair = jnp.where(lane_row < ML_DK, wc[:, 2 * pr * L:(2 * pr + 1) * L],
                                    wc[:, (2 * pr + 1) * L:(2 * pr + 2) * L])
                c_state[d][pr] = wc_pair * state + update
            yield
        for d in m_state:
            for h in range(ML_HEADS):
                m_s[d * ML_HEADS + h:d * ML_HEADS + h + 1, :] = m_state[d][:, h * L:(h + 1) * L]
            for pr in range(n_pairs):
                c_s[d * n_pairs + pr] = c_state[d][pr]

    def scan_streams(cs):
        return [(d, c if d == 0 else n_chunks - 1 - c) for c in cs for d in range(2)]

    if lat:
        def body2(n, carry):
            _run(_interleave([(attend(n, pl.multiple_of(n * tq, tq), 2), 1),
                              (mlstm_streams(scan_streams([n])), 1)]))
            return carry
        lax.fori_loop(0, n_chunks, body2, 0)
    else:
        def all_tiles():
            for n in range(T // tq):
                yield from attend(n, n * tq, 2)
        yield from _interleave([(all_tiles(), 1), (mlstm_streams(scan_streams(range(n_chunks))), 3)])

    if not lat:
        for d in range(2):
            for pr in range(n_pairs):
                c_t = jnp.transpose(c_s[d * n_pairs + pr, 0:ML_DV, :])
                for e in range(2):
                    c_o[d, 2 * pr + e] = c_t[e * ML_DK:(e + 1) * ML_DK, :]
            for h in range(ML_HEADS):
                idx = d * n_pairs + h // 2
                hl = slice((h % 2) * ML_DK, (h % 2 + 1) * ML_DK)
                n_o[d, h:h + 1, :] = c_s[idx, ML_DV:ML_DV + 1, hl]
                m_o[d:d + 1, h:h + 1] = m_s[d * ML_HEADS + h:d * ML_HEADS + h + 1, 0:1]

    def phase3_merge(i, out):
        rows = row_slice(i, 0, row_block)
        u = u_s[rows, :]

        def proj(c0, width):
            if c0 >= C_ZW:
                return _dot(u, wt_s[:, c0 - C_ZW:c0 - C_ZW + width])
            return _dot(u, w_ref[:, c0:c0 + width])

        hm = jnp.concatenate(
            [jnp.concatenate([jnp.transpose(hmt_s[i * chunks_per_block + j, h * ML_DV:(h + 1) * ML_DV, :])
                              for h in range(ML_HEADS)], axis=1)
             for j in range(chunks_per_block)], axis=0)
        ys = (ya_s[rows, :].astype(F32), om_s[rows, :].astype(F32) * hm, yw_s[rows, :].astype(F32))
        yield
        merged = None
        for b, (yb, zc) in enumerate(zip(ys, (C_ZA, C_ZM, C_ZW))):
            z = proj(zc, BRANCH_W)
            t = (yb * (z * _sigmoid(z))).astype(BF16)
            yield
            pb = _sigmoid(proj(C_GMERGE + b * D_MODEL, D_MODEL)) * _dot(t, wb_ref[b])
            merged = pb if merged is None else merged + pb
            yield
        out.append(_dot(merged.astype(BF16), wo_ref[...]))
        yield

    def phase3_norm(i, o):
        for j in range(0, row_block, norm_rows):
            piece = row_slice(i, j, norm_rows)
            hres = ALPHA * x_ref[piece, :] + gate * o[j:j + norm_rows, :]
            y_ref[piece, :] = _layer_norm(hres) * lng_ref[...] + lnb_ref[...]
            yield

    @pl.when(first_step)
    def _():
        for cp in late_copies:
            cp.wait()

    def phase3(i):
        box = []
        yield from phase3_merge(i, box)
        yield from phase3_norm(i, box[0])

    if not lat:
        box = []
        yield from _interleave([(phase3_merge(0, box), 1), (phase1_norm(0, x_next_ref, u_next_s), 1)])
        yield from phase3_norm(0, box[0])
    else:
        def body4(i, carry):
            _run(phase3(i))
            return carry
        lax.fori_loop(0, n_rows, body4, 0)


def _const_spec(shape):
    nd = len(shape)
    return pl.BlockSpec(shape, lambda b: (0,) * nd, pipeline_mode=pl.Buffered(1))


def _layer_spec(shape, layer):
    nd = len(shape) - 1
    return pl.BlockSpec((None,) + tuple(shape[1:]), lambda b: (layer,) + (0,) * nd,
                        pipeline_mode=pl.Buffered(1))


def _layer_call(lat, layer, x, mod, weights, consts, extra):
    B, T, _ = x.shape
    S = T + PAST_LEN if lat else T
    n_chunks = T // CHUNK

    seq_spec = pl.BlockSpec((None, T, D_MODEL), lambda b: (b, 0, 0))
    out_seq_spec = seq_spec
    if lat:
        mod_spec = pl.BlockSpec((None, None, 3, D_MODEL), lambda b: (layer, b + 1, 0, 0))
    else:
        mod_spec = pl.BlockSpec((None, None, 3, D_MODEL), lambda b: (layer, 0, 0, 0))
    *vmem_consts, sink = consts
    in_specs = [seq_spec, mod_spec]
    w_all, wb_all, wo_all = weights[:3]
    in_specs.append(pl.BlockSpec((None, D_MODEL, C_ZW), lambda b: (layer, 0, 0), pipeline_mode=pl.Buffered(1)))
    in_specs += [pl.BlockSpec(memory_space=pl.ANY)] * 2
    in_specs += [_layer_spec(a.shape, layer) for a in weights[3:]]
    in_specs += [_const_spec(a.shape) for a in vmem_consts]
    in_specs.append(pl.BlockSpec(memory_space=pltpu.SMEM))
    args = [x, mod, *weights, *vmem_consts, sink]

    if lat:
        cos, sa, sb, cka, cva, ckw, cvw, sc, sn, sm = extra
        in_specs += [_const_spec(cos.shape)] * 3
        kv_spec = pl.BlockSpec((None, None, N_KV, HEAD_DIM, PAST_LEN), lambda b: (b, layer, 0, 0, 0))
        in_specs += [kv_spec] * 4
        in_specs += [
            pl.BlockSpec((None, None, 2, ML_HEADS, ML_DK, ML_DV), lambda b: (b, layer, 0, 0, 0, 0)),
            pl.BlockSpec((None, None, 2, ML_HEADS, ML_DK), lambda b: (b, layer, 0, 0, 0)),
            pl.BlockSpec((None, None, 2, ML_HEADS), lambda b: (b, layer, 0, 0)),
        ]
        args += [cos, sa, sb, cka, cva, ckw, cvw, sc, sn, sm]
        out_specs = out_seq_spec
        out_shape = jax.ShapeDtypeStruct((B, T, D_MODEL), F32)
        aliases = {}
    else:
        kv_shape = (B, DEPTH, N_KV, HEAD_DIM, T)
        c_shape = (B, DEPTH, 2, ML_HEADS, ML_DK, ML_DV)
        if layer == 0:
            aliases = {}
            ld = DEPTH
            layer_idx = 0
        else:
            first_acc = len(args)
            in_specs += [pl.BlockSpec(memory_space=pl.ANY)] * len(extra)
            args += list(extra)
            aliases = {first_acc + i: 1 + i for i in range(len(extra))}
            assert len(extra) == CTX_ALIASED
            ld = None
            layer_idx = layer
        in_specs.append(pl.BlockSpec((None, T, D_MODEL), lambda b: (jnp.minimum(b + 1, B - 1), 0, 0)))
        args.append(x)
        kv_out = pl.BlockSpec((None, ld, N_KV, HEAD_DIM, T), lambda b: (b, layer_idx, 0, 0, 0))
        out_specs = [
            seq_spec, kv_out, kv_out, kv_out, kv_out,
            pl.BlockSpec((None, ld, 2, ML_HEADS, ML_DK, ML_DV), lambda b: (b, layer_idx, 0, 0, 0, 0)),
            pl.BlockSpec((None, 2, ML_HEADS, ML_DK), lambda b: (b, 0, 0, 0)),
            pl.BlockSpec((None, 2, ML_HEADS), lambda b: (b, 0, 0)),
        ]
        out_shape = [jax.ShapeDtypeStruct((B, T, D_MODEL), F32)]
        out_shape += [jax.ShapeDtypeStruct(kv_shape, F32)] * 4 + [jax.ShapeDtypeStruct(c_shape, F32)]
        out_shape += [
            jax.ShapeDtypeStruct((B, 2, ML_HEADS, ML_DK), F32),
            jax.ShapeDtypeStruct((B, 2, ML_HEADS), F32),
        ]

    scratch = [
        ((T, D_MODEL) if lat else (2, T, D_MODEL), BF16),
        ((N_HEADS, T, HEAD_DIM), BF16),
        ((N_HEADS, T, HEAD_DIM), BF16),
        ((N_KV, S, HEAD_DIM), BF16),
        ((S // CHUNK, N_KV, VT_ROWS, CHUNK), BF16),
        ((N_KV, T, HEAD_DIM), BF16),
        ((n_chunks, N_KV, VT_ROWS, CHUNK), BF16),
        ((T, ML_HEADS * ML_DK), BF16),
        ((T, ML_HEADS * ML_DK), BF16),
        ((n_chunks, ML_HEADS, ST_ROWS, CHUNK), BF16),
        ((T, ML_HEADS * ML_DV), BF16),
        ((T, LANES), F32),
        ((T, BRANCH_W), BF16),
        ((T, BRANCH_W), BF16),
        ((n_chunks, ML_HEADS * ML_DV, CHUNK), F32),
        ((ML_HEADS, ST_ROWS, 2 * ML_DK), F32),
        ((2 * ML_HEADS, LANES), F32),
    ]
    if lat:
        scratch += [((N_KV, PAST_LEN, HEAD_DIM), BF16),
                    ((N_KV, VT_ROWS, PAST_LEN), BF16)]
    scratch = [pltpu.VMEM(shape, dtype) for shape, dtype in scratch]
    in_specs.append(pl.BlockSpec(memory_space=pl.ANY))
    args.append(w_all)
    scratch += [
        pltpu.VMEM((D_MODEL, W_COLS - C_ZW), BF16),
        pltpu.VMEM(tuple(wb_all.shape[1:]), BF16),
        pltpu.VMEM(tuple(wo_all.shape[1:]), BF16),
        pltpu.SemaphoreType.DMA((3,)),
    ]

    return pl.pallas_call(
        functools.partial(_layer_kernel, lat, T, layer),
        grid=(B,),
        in_specs=in_specs,
        out_specs=out_specs,
        out_shape=out_shape,
        scratch_shapes=scratch,
        input_output_aliases=aliases,
        compiler_params=pltpu.CompilerParams(
            dimension_semantics=("arbitrary",),
            vmem_limit_bytes=LATENT_VMEM_LIMIT_BYTES if lat else VMEM_LIMIT_BYTES),
        name=("latent_layer" if lat else "context_layer") + str(layer),
    )(*args)


def _rope_tables(T):
    f32 = np.float32
    rows = T // GRID_W
    row = np.repeat(np.arange(rows, dtype=f32), GRID_W)
    col = np.tile(np.arange(GRID_W, dtype=f32), rows)
    inv = (f32(ROPE_THETA) ** (-np.arange(0, AXIS_DIM, 2, dtype=f32) / f32(AXIS_DIM))).astype(f32)
    ar = row[:, None] * inv
    ac = col[:, None] * inv
    ang = np.concatenate([ar, ar, ac, ac], axis=-1)
    ang = np.concatenate([ang, ang], axis=-1)
    first_half = (np.arange(LANES) % AXIS_DIM) < (AXIS_DIM // 2)
    cos = np.cos(ang).astype(f32)
    sin = np.sin(ang).astype(f32)
    sa = np.where(first_half[None, :], -sin, f32(0.0))
    sb = np.where(first_half[None, :], f32(0.0), sin)
    return jnp.asarray(cos), jnp.asarray(sa), jnp.asarray(sb)


def _constants(sink_logit):
    grp = np.arange(N_HEADS * HEAD_DIM) // HEAD_DIM
    bd = jnp.asarray(grp[:, None] == grp[None, :], dtype=BF16)
    ti = np.arange(CHUNK)
    tril = jnp.asarray(ti[None, :] <= ti[:, None], dtype=BF16)
    triu = jnp.asarray(ti[None, :] >= ti[:, None], dtype=BF16)
    return bd, tril, triu, sink_logit.reshape(-1)


def kernel(x_prompt, x_sample, cache_attn_k, cache_attn_v, cache_win_k, cache_win_v, state_mlstm_C,
           state_mlstm_n, state_mlstm_m, c, c_ctx, w_mod, b_mod, w_in, qk_gain, sink_logit,
           mlstm_gate_bias, w_branch, w_out, ln_gain, ln_bias):
    dec_b = x_sample.shape[0]
    assert dec_b + 1 <= 8
    cond = jnp.concatenate([c_ctx[None, :], c, jnp.zeros((8 - 1 - dec_b, D_MODEL), F32)], axis=0)
    mod = _modulation(cond, w_mod, b_mod).reshape(DEPTH, 8, 3, D_MODEL)

    consts = _constants(sink_logit)
    cos, sa, sb = _rope_tables(x_sample.shape[1])
    lane_pad = jnp.zeros((DEPTH, 1, LANES - N_GATES), F32)
    weights = (
        _prep_in_weight(jnp.swapaxes(w_in, 1, 2)), w_branch.astype(BF16), w_out.astype(BF16),
        jnp.tile(qk_gain[:, 0:1, :], (1, 1, N_HEADS)),
        jnp.tile(qk_gain[:, 1:2, :], (1, 1, N_KV)),
        jnp.concatenate([mlstm_gate_bias.reshape(DEPTH, 1, N_GATES), lane_pad], axis=2),
        ln_gain[:, None, :], ln_bias[:, None, :],
    )

    ctx = None
    cached_t = [jnp.swapaxes(a, -1, -2)
                for a in (cache_attn_k, cache_attn_v, cache_win_k, cache_win_v, state_mlstm_C)]
    xp, xs = x_prompt, x_sample
    small = []
    for l in range(DEPTH):
        outs = _layer_call(False, l, xp, mod, weights, consts, ctx)
        xp = outs[0]
        ctx = tuple(outs[1:6])
        small.append(outs[6:])
        extra = (cos, sa, sb, *cached_t, state_mlstm_n, state_mlstm_m)
        xs = _layer_call(True, l, xs, mod, weights, consts, extra)
    new_n, new_m = [jnp.stack([small[l][i] for l in range(DEPTH)], axis=1) for i in range(2)]
    return (xp, xs, *[jnp.swapaxes(a, -1, -2) for a in ctx], new_n, new_m)
```

```python
import functools

import jax
import jax.numpy as jnp
import numpy as np
from jax import lax
from jax.experimental import pallas as pl
from jax.experimental.pallas import tpu as pltpu

F32 = jnp.float32
BF16 = jnp.bfloat16

D_MODEL = 1024
DEPTH = 2
PAST_LEN = 256
GRID_W = 64
HEAD_DIM = 64
N_HEADS = 8
N_KV = 2
N_GROUP = N_HEADS // N_KV
WINDOW = 128
ML_HEADS = 4
ML_DK = 64
ML_DV = 128
CHUNK = 128
BRANCH_W = 512
ROPE_THETA = 10000.0
AXIS_DIM = HEAD_DIM // 2
LN_EPS = 1e-6
RMS_EPS = 1e-6
ALPHA = (2.0 * DEPTH) ** 0.25
LOG2E = float(np.log2(np.e))
LANES = 128
BF16_ROWS = 16
ROW_BLOCK = 256
MOD_COLS = 1536
CTX_ALIASED = 5
N_SHARED = 14
VMEM_LIMIT_BYTES = 60 * 1024 * 1024
LATENT_VMEM_LIMIT_BYTES = 62 * 1024 * 1024
VT_ROWS = HEAD_DIM + BF16_ROWS
ST_ROWS = ML_DV + BF16_ROWS

_SIZES = (512, 128, 128, 512, 256, 256, 512, 512, 128, 512, 512, 128, 128, 512, 3072)
_OFF = [int(v) for v in np.concatenate([[0], np.cumsum(_SIZES)])]
(C_QA, C_KA, C_VA, C_ZA, C_QM, C_KM, C_VM, C_OM, C_GT, C_ZM, C_QW, C_KW, C_VW, C_ZW, C_GMERGE) = _OFF[:-1]
W_COLS = _OFF[-1]
GATE_COL = 2816
N_GATES = 4 * ML_HEADS


def _dot(a, b):
    return jnp.dot(a, b, preferred_element_type=F32)


def _dot_tb(a, b):
    return lax.dot_general(a, b, (((1,), (1,)), ((), ())), preferred_element_type=F32)


def _split(a):
    hi = a.astype(BF16)
    lo = (a - hi.astype(F32)).astype(BF16)
    return hi, lo


def _layer_norm(x):
    mu = jnp.mean(x, axis=-1, keepdims=True)
    xc = x - mu
    var = jnp.mean(xc * xc, axis=-1, keepdims=True)
    return xc * lax.rsqrt(var + LN_EPS)


def _log_sigmoid(x):
    return jnp.minimum(x, 0.0) - jnp.log(1.0 + jnp.exp(-jnp.abs(x)))


def _sigmoid(x):
    return jax.nn.sigmoid(x)


def _mod_kernel(c_ref, w_ref, b_ref, o_ref):
    c = c_ref[...]
    s = c * _sigmoid(c)
    s_hi, s_lo = _split(s)
    w = w_ref[...]
    w_hi, w_lo = _split(w)
    o_ref[...] = _dot(s_hi, w_hi) + _dot(s_lo, w_hi) + _dot(s_hi, w_lo) + b_ref[...]


def _modulation(cond, w_mod, b_mod):
    rows = cond.shape[0]
    tn = MOD_COLS
    return pl.pallas_call(
        _mod_kernel,
        grid=(DEPTH, 3 * D_MODEL // tn),
        in_specs=[
            pl.BlockSpec((rows, D_MODEL), lambda l, j: (0, 0)),
            pl.BlockSpec((None, D_MODEL, tn), lambda l, j: (l, 0, j)),
            pl.BlockSpec((None, 1, tn), lambda l, j: (l, 0, j)),
        ],
        out_specs=pl.BlockSpec((None, rows, tn), lambda l, j: (l, 0, j)),
        out_shape=jax.ShapeDtypeStruct((DEPTH, rows, 3 * D_MODEL), F32),
        compiler_params=pltpu.CompilerParams(dimension_semantics=("arbitrary", "arbitrary")),
        name="adaln_modulation",
    )(cond, w_mod, b_mod.reshape(DEPTH, 1, 3 * D_MODEL))


PREP_K = 256


def _prep_kernel(wt_ref, o_ref):
    lane = lax.broadcasted_iota(jnp.int32, (PREP_K, LANES), 1)
    for c0 in range(0, W_COLS, LANES):
        r0 = c0 if c0 <= GATE_COL else c0 - (LANES - N_GATES)
        slab = jnp.transpose(wt_ref[r0:r0 + LANES, :])
        if c0 == GATE_COL:
            slab = jnp.where(lane < N_GATES, slab, 0.0)
        o_ref[:, c0:c0 + LANES] = slab.astype(BF16)


def _prep_in_weight(w_in_t):
    n_in = w_in_t.shape[1]
    return pl.pallas_call(
        _prep_kernel,
        grid=(DEPTH, D_MODEL // PREP_K),
        in_specs=[pl.BlockSpec((None, n_in, PREP_K), lambda l, i: (l, 0, i))],
        out_specs=pl.BlockSpec((None, PREP_K, W_COLS), lambda l, i: (l, i, 0)),
        out_shape=jax.ShapeDtypeStruct((DEPTH, D_MODEL, W_COLS), BF16),
        compiler_params=pltpu.CompilerParams(
            dimension_semantics=("arbitrary", "arbitrary"), vmem_limit_bytes=VMEM_LIMIT_BYTES),
        name="in_weight_prep",
    )(w_in_t)


def _rope(x, cos, sa, sb):
    return x * cos + pltpu.roll(x, LANES - AXIS_DIM // 2, 1) * sa + pltpu.roll(x, AXIS_DIM // 2, 1) * sb


def _group_rms(x, bd, gain):
    ms = _dot((x * x).astype(BF16), bd) * (1.0 / HEAD_DIM)
    return x * lax.rsqrt(ms + RMS_EPS) * gain


def _ones_row_tile():
    r = lax.broadcasted_iota(jnp.int32, (BF16_ROWS, LANES), 0)
    return jnp.where(r == 0, 1.0, 0.0).astype(BF16)


def _attend_chains(chains, results, depth):
    state = []
    for qs, tiles, sink_row in chains:
        m_cols = qs.shape[0]
        if sink_row is None:
            m = jnp.full((1, m_cols), -jnp.inf, F32)
            acc = jnp.zeros((VT_ROWS, m_cols), F32)
        else:
            m = sink_row
            r = lax.broadcasted_iota(jnp.int32, (VT_ROWS, m_cols), 0)
            acc = jnp.where(r == HEAD_DIM, 1.0, 0.0)
        state.append([m, acc])
    order = [(t, c) for t in range(max(len(tiles) for _, tiles, _ in chains))
             for c, (_, tiles, _) in enumerate(chains) if t < len(tiles)]

    def issue(t, c):
        qs, tiles, _ = chains[c]
        tile = tiles[t]()
        return tile, _dot_tb(tile[0], qs)

    pending = [issue(*item) for item in order[:depth]]
    yield
    for k, (t, c) in enumerate(order):
        (_, v_t, valid), s = pending.pop(0)
        if k + depth < len(order):
            pending.append(issue(*order[k + depth]))
        if valid is not None:
            s = jnp.where(valid, s, -jnp.inf)
        m, acc = state[c]
        m_new = jnp.maximum(m, jnp.max(s, axis=0, keepdims=True))
        p = jnp.exp2(s - m_new).astype(BF16)
        state[c] = [m_new, acc * jnp.exp2(m - m_new) + _dot(v_t, p)]
        if k + 1 == len(order) or order[k + 1][0] != t:
            yield
    results.extend(acc[0:HEAD_DIM, :] / acc[HEAD_DIM:HEAD_DIM + 1, :] for _, acc in state)


def _interleave(weighted):
    live = [[g, n] for g, n in weighted]
    while live:
        for item in list(live):
            for _ in range(item[1]):
                try:
                    next(item[0])
                except StopIteration:
                    live.remove(item)
                    break
        yield


def _run(gen):
    for _ in gen:
        pass


def _layer_kernel(lat, T, layer, *refs):
    _run(_sequence(lat, T, layer, *refs))


def _sequence(lat, T, layer, *refs):
    refs = list(refs)
    x_ref, mod_ref, w_ref, wb_ref, wo_ref = refs[:5]
    g0_ref, g1_ref, gb_ref, lng_ref, lnb_ref = refs[5:10]
    bd_ref, tril_ref, triu_ref, sink_ref = refs[10:N_SHARED]
    pos = N_SHARED
    if lat:
        cos_ref, sa_ref, sb_ref = refs[pos:pos + 3]
        cka_ref, cva_ref, ckw_ref, cvw_ref, c0_ref, n0_ref, m0_ref = refs[pos + 3:pos + 10]
        pos += 10
        w_hbm = refs[pos]
        y_ref = refs[pos + 1]
        pos += 2
    else:
        if layer > 0:
            pos += CTX_ALIASED
        x_next_ref = refs[pos]
        pos += 1
        y_ref, ka_o, va_o, kw_o, vw_o, c_o, n_o, m_o = refs[pos:pos + 8]
        pos += 8
        if layer == 0:
            for full in (ka_o, va_o, kw_o, vw_o, c_o):
                full[1:] = jnp.zeros((DEPTH - 1,) + full.shape[1:], F32)
            ka_o, va_o, kw_o, vw_o, c_o = (full.at[0] for full in (ka_o, va_o, kw_o, vw_o, c_o))
    (u_s, qa_s, qw_s, ka_s, vat_s, kw_s, vwt_s, qm_s, km_s, vmt_s, om_s, g_s,
     ya_s, yw_s, hmt_s, c_s, m_s) = refs[pos:pos + 17]
    pos += 17
    if not lat:
        step = pl.program_id(0)
        u_next_s = u_s.at[(step + 1) % 2]
        u_s = u_s.at[step % 2]
    wt_s = None
    if lat:
        kwc_s, vwct_s = refs[pos:pos + 2]
        wb_hbm, wo_hbm = wb_ref, wo_ref
        wt_s, wb_ref, wo_ref, w_sem = refs[pos + 2:pos + 6]

        first_step = pl.program_id(0) == 0
        late_copies = (
            pltpu.make_async_copy(w_hbm.at[layer, :, C_ZW:W_COLS], wt_s, w_sem.at[0]),
            pltpu.make_async_copy(wb_hbm.at[layer], wb_ref, w_sem.at[1]),
            pltpu.make_async_copy(wo_hbm.at[layer], wo_ref, w_sem.at[2]),
        )

        @pl.when(first_step)
        def _():
            for cp in late_copies:
                cp.start()

    row_block = min(T, ROW_BLOCK)
    n_rows = T // row_block
    n_chunks = T // CHUNK
    chunks_per_block = row_block // CHUNK
    shift = mod_ref[0:1, :]
    scale = mod_ref[1:2, :]
    gate = mod_ref[2:3, :]
    ones_tile = _ones_row_tile()

    norm_rows = 64

    def row_slice(i, j, n):
        r0 = i * row_block + j
        return pl.ds(r0 if isinstance(r0, int) else pl.multiple_of(r0, n), n)

    def phase1_norm(i, src=x_ref, dst=u_s):
        for j in range(0, row_block, norm_rows):
            piece = row_slice(i, j, norm_rows)
            dst[piece, :] = (_layer_norm(src[piece, :]) * (1.0 + scale) + shift).astype(BF16)
            yield

    def phase1_proj(i):
        R = row_block
        rows = row_slice(i, 0, R)
        u = u_s[rows, :]

        def proj(c0, width):
            return _dot(u, w_ref[:, c0:c0 + width])

        if lat:
            cos = cos_ref[rows, :]
            sa = sa_ref[rows, :]
            sb = sb_ref[rows, :]

        def rope_wide(v):
            if not lat:
                return v
            slabs = [_rope(v[:, j:j + LANES], cos, sa, sb) for j in range(0, v.shape[1], LANES)]
            return slabs[0] if len(slabs) == 1 else jnp.concatenate(slabs, axis=1)

        qscale = LOG2E * HEAD_DIM ** -0.5
        qa = rope_wide(_group_rms(proj(C_QA, 512), bd_ref[...], g0_ref[...])) * qscale
        for h in range(N_HEADS):
            qa_s[h, rows, :] = qa[:, h * HEAD_DIM:(h + 1) * HEAD_DIM].astype(BF16)
        yield
        ka_n = _group_rms(proj(C_KA, LANES), bd_ref[0:LANES, 0:LANES], g1_ref[...])
        ka = rope_wide(ka_n)
        va = proj(C_VA, LANES)
        qw = rope_wide(proj(C_QW, 512)) * qscale
        for h in range(N_HEADS):
            qw_s[h, rows, :] = qw[:, h * HEAD_DIM:(h + 1) * HEAD_DIM].astype(BF16)
        yield
        kw_raw = proj(C_KW, LANES)
        kw = rope_wide(kw_raw)
        vw = proj(C_VW, LANES)
        for kv in range(N_KV):
            sl = slice(kv * HEAD_DIM, (kv + 1) * HEAD_DIM)
            ka_s[kv, rows, :] = ka[:, sl].astype(BF16)
            kw_s[kv, rows, :] = kw[:, sl].astype(BF16)
        yield
        qm_s[rows, :] = proj(C_QM, 256).astype(BF16)
        km_s[rows, :] = (proj(C_KM, 256) * (ML_DK ** -0.5)).astype(BF16)
        yield
        vm = proj(C_VM, 512)
        yield
        om_s[rows, :] = _sigmoid(proj(C_OM, 512)).astype(BF16)
        g_s[rows, :] = proj(C_GT, LANES) + gb_ref[...]
        yield
        for j in range(chunks_per_block):
            cj = i * chunks_per_block + j
            cr = slice(j * CHUNK, (j + 1) * CHUNK)
            va_t = jnp.transpose(va[cr, :])
            vw_t = jnp.transpose(vw[cr, :])
            for kv in range(N_KV):
                hd = slice(kv * HEAD_DIM, (kv + 1) * HEAD_DIM)
                vat_s[cj, kv, 0:HEAD_DIM, :] = va_t[hd, :].astype(BF16)
                vat_s[cj, kv, HEAD_DIM:VT_ROWS, :] = ones_tile
                vwt_s[cj, kv, 0:HEAD_DIM, :] = vw_t[hd, :].astype(BF16)
                vwt_s[cj, kv, HEAD_DIM:VT_ROWS, :] = ones_tile
            if not lat:
                tc = slice(i * row_block + j * CHUNK, i * row_block + (j + 1) * CHUNK)
                ka_t = jnp.transpose(ka_n[cr, :])
                kw_t = jnp.transpose(kw_raw[cr, :])
                for kv in range(N_KV):
                    hd = slice(kv * HEAD_DIM, (kv + 1) * HEAD_DIM)
                    ka_o[kv, :, tc] = ka_t[hd, :]
                    va_o[kv, :, tc] = va_t[hd, :]
                    kw_o[kv, :, tc] = kw_t[hd, :]
                    vw_o[kv, :, tc] = vw_t[hd, :]
            for h in range(ML_HEADS):
                vmt_s[cj, h, 0:ML_DV, :] = jnp.transpose(vm[cr, h * ML_DV:(h + 1) * ML_DV]).astype(BF16)
                vmt_s[cj, h, ML_DV:ST_ROWS, :] = ones_tile
            yield

    if not lat:
        @pl.when(step == 0)
        def _():
            _run(phase1_norm(0))
        yield from phase1_proj(0)
    else:
        def body1(i, carry):
            _run(phase1_norm(i))
            _run(phase1_proj(i))
            return carry
        lax.fori_loop(0, n_rows, body1, 0)

    if lat:
        for j in range(PAST_LEN // CHUNK):
            cols = slice(j * CHUNK, (j + 1) * CHUNK)
            ka_c = jnp.transpose(jnp.concatenate([cka_ref[kv, :, cols] for kv in range(N_KV)], axis=0))
            kw_c = jnp.transpose(jnp.concatenate([ckw_ref[kv, :, cols] for kv in range(N_KV)], axis=0))
            for kv in range(N_KV):
                hd = slice(kv * HEAD_DIM, (kv + 1) * HEAD_DIM)
                ka_s[kv, T + j * CHUNK:T + (j + 1) * CHUNK, :] = ka_c[:, hd].astype(BF16)
                kwc_s[kv, cols, :] = kw_c[:, hd].astype(BF16)
                vat_s[n_chunks + j, kv, 0:HEAD_DIM, :] = cva_ref[kv, :, cols].astype(BF16)
                vat_s[n_chunks + j, kv, HEAD_DIM:VT_ROWS, :] = ones_tile
                vwct_s[kv, 0:HEAD_DIM, cols] = cvw_ref[kv, :, cols].astype(BF16)
                vwct_s[kv, HEAD_DIM:VT_ROWS, cols] = ones_tile

    tq = CHUNK
    scores_in_flight = 2 if lat else 2 * N_KV
    m_cols = N_GROUP * tq

    def sink_row(kv):
        col = lax.broadcasted_iota(jnp.int32, (1, m_cols), 1)
        row = jnp.full((1, m_cols), sink_ref[layer * N_HEADS + kv * N_GROUP], F32)
        for g in range(1, N_GROUP):
            row = jnp.where(col >= g * tq, sink_ref[layer * N_HEADS + kv * N_GROUP + g], row)
        return row * LOG2E

    def load_q(q_s, kv, q0):
        return q_s[kv * N_GROUP:(kv + 1) * N_GROUP, pl.ds(q0, tq), :].reshape(m_cols, HEAD_DIM)

    def store_heads(dst, q0, kv, o_t):
        for p in range(N_GROUP // 2):
            blk = jnp.concatenate([o_t[:, (2 * p) * tq:(2 * p + 1) * tq],
                                   o_t[:, (2 * p + 1) * tq:(2 * p + 2) * tq]], axis=0)
            c0 = (kv * N_GROUP + 2 * p) * HEAD_DIM
            dst[pl.ds(q0, tq), c0:c0 + LANES] = jnp.transpose(blk).astype(dst.dtype)

    def key_tiles(k_ref, vt_ref, kv, n_key_chunks, chunks_per_tile):
        def tile(c0):
            def load():
                k = k_ref[kv, c0 * CHUNK:(c0 + chunks_per_tile) * CHUNK, :]
                v_t = [vt_ref[c0 + j, kv] for j in range(chunks_per_tile)]
                return k, (v_t[0] if len(v_t) == 1 else jnp.concatenate(v_t, axis=1)), None
            return load
        return [tile(c0) for c0 in range(0, n_key_chunks, chunks_per_tile)]

    def band_tiles(kv, n, q0):
        j0 = jnp.clip(n - 1, 0, n_chunks - 3)
        w0 = pl.multiple_of(j0 * CHUNK, CHUNK)

        def band(c0, nc):
            def load():
                rows = nc * CHUNK
                kpos = w0 + c0 * CHUNK + lax.broadcasted_iota(jnp.int32, (rows, m_cols), 0)
                qpos = q0 + (lax.broadcasted_iota(jnp.int32, (rows, m_cols), 1) & (tq - 1))
                v_t = [vwt_s[j0 + c0 + j, kv] for j in range(nc)]
                return (kw_s[kv, pl.ds(w0 + c0 * CHUNK, rows), :],
                        v_t[0] if nc == 1 else jnp.concatenate(v_t, axis=1),
                        jnp.abs(kpos - qpos) <= WINDOW)
            return load
        return [band(0, 2), band(2, 1), lambda: (kwc_s[kv], vwct_s[kv], None)]

    def attend(n, q0, chunks_per_tile):
        chains, dsts = [], []
        n_key_chunks = ka_s.shape[1] // CHUNK
        for kv in range(N_KV):
            chains.append((load_q(qa_s, kv, q0), key_tiles(ka_s, vat_s, kv, n_key_chunks, chunks_per_tile), None))
            dsts.append((ya_s, kv))
        for kv in range(N_KV):
            tiles = band_tiles(kv, n, q0) if lat else key_tiles(kw_s, vwt_s, kv, n_chunks, chunks_per_tile)
            chains.append((load_q(qw_s, kv, q0), tiles, sink_row(kv)))
            dsts.append((yw_s, kv))
        results = []
        yield from _attend_chains(chains, results, scores_in_flight)
        for (dst, kv), o_t in zip(dsts, results):
            store_heads(dst, q0, kv, o_t)

    n_pairs = ML_HEADS // 2
    hmt_s[...] = jnp.zeros_like(hmt_s)
    for d in range(2):
        if lat:
            for pr in range(n_pairs):
                c_s[d * n_pairs + pr, 0:ML_DV, :] = jnp.transpose(
                    jnp.concatenate([c0_ref[d, 2 * pr], c0_ref[d, 2 * pr + 1]], axis=0))
        for h in range(ML_HEADS):
            idx = d * n_pairs + h // 2
            hl = slice((h % 2) * ML_DK, (h % 2 + 1) * ML_DK)
            c_s[idx, ML_DV:ST_ROWS, hl] = jnp.zeros((BF16_ROWS, ML_DK), F32)
            if lat:
                c_s[idx, ML_DV:ML_DV + 1, hl] = n0_ref[d, h:h + 1, :]
                m_s[d * ML_HEADS + h:d * ML_HEADS + h + 1, :] = jnp.broadcast_to(
                    m0_ref[d:d + 1, h:h + 1], (1, LANES))
            else:
                c_s[idx, 0:ML_DV, hl] = jnp.zeros((ML_DV, ML_DK), F32)
                m_s[d * ML_HEADS + h:d * ML_HEADS + h + 1, :] = jnp.zeros((1, LANES), F32)

    L = CHUNK
    s_idx = lax.broadcasted_iota(jnp.int32, (L, ML_HEADS * L), 0)
    t_idx = lax.broadcasted_iota(jnp.int32, (L, ML_HEADS * L), 1) & (L - 1)
    lane_row = lax.broadcasted_iota(jnp.int32, (1, LANES), 1)
    low_half = lax.broadcasted_iota(jnp.int32, (L, LANES), 1) < ML_DK
    ones_ll = jnp.ones((L, L), BF16)
    zeros_ll = jnp.zeros((L, L), BF16)

    def heads_row(src, r0, c0=0):
        return jnp.concatenate([src[r0 + h:r0 + h + 1, c0:c0 + L] for h in range(ML_HEADS)], axis=1)

    def block_diag_rows(x):
        zero = jnp.zeros_like(x)
        return jnp.concatenate([jnp.where(low_half, x, zero), jnp.where(low_half, zero, x)], axis=0)

    def mlstm_streams(streams):
        m_state = {d: heads_row(m_s, d * ML_HEADS) for d in sorted({d for d, _ in streams})}
        c_state = {d: [c_s[d * n_pairs + pr] for pr in range(n_pairs)] for d in m_state}
        work = []
        for d, cc in streams:
            r0 = cc * L if isinstance(cc, int) else pl.multiple_of(cc * L, L)
            rows = pl.ds(r0, L)
            g = g_s[rows, :]
            f_hi, f_lo = _split(_log_sigmoid(g))
            tri_c = tril_ref[...] if d == 0 else triu_ref[...]
            tri_r = triu_ref[...] if d == 0 else tril_ref[...]
            cum = _dot(tri_c, f_hi) + _dot(tri_c, f_lo)
            r = g - pltpu.roll(cum, LANES - ML_HEADS, 1)
            gi0 = 2 * ML_HEADS * d
            rb = jnp.concatenate([jnp.broadcast_to(r[:, gi0 + h:gi0 + h + 1], (L, L))
                                  for h in range(ML_HEADS)], axis=1)
            g_t = jnp.transpose(g)[0:N_GATES, :]
            ft_hi, ft_lo = _split(_log_sigmoid(g_t))
            tr = jnp.concatenate([tri_r, ones_ll], axis=1)
            ct = _dot(ft_hi, tr) + _dot(ft_lo, tr)
            work.append(dict(d=d, cc=cc, rows=rows, rb=rb, g_t=g_t, ct=ct))
            yield
        for w in work:
            d = w["d"]
            valid = (s_idx <= t_idx) if d == 0 else (s_idx >= t_idx)
            gi0 = 2 * ML_HEADS * d
            gf0 = gi0 + ML_HEADS
            b_row = heads_row(w["ct"], gf0)
            b_last = heads_row(w["ct"], gf0, L)
            i_row = heads_row(w["g_t"], gi0)
            m_prev = m_state[d]
            a_row = b_row + m_prev
            dm = jnp.where(valid, b_row + w["rb"], -jnp.inf)
            mt = jnp.maximum(a_row, jnp.max(dm, axis=0, keepdims=True))
            w["p"] = jnp.exp(dm - mt)
            w["w_inter"] = jnp.exp(a_row - mt)
            w["floor"] = jnp.exp(-mt)
            g_row = b_last - b_row + i_row
            g_max = jnp.concatenate(
                [jnp.broadcast_to(jnp.max(g_row[:, h * L:(h + 1) * L], axis=-1, keepdims=True), (1, L))
                 for h in range(ML_HEADS)], axis=1)
            m_new = jnp.maximum(b_last + m_prev, g_max)
            w["ws"] = jnp.exp(g_row - m_new)
            w["wc"] = jnp.exp(b_last + m_prev - m_new)
            m_state[d] = m_new
            yield
        for w in work:
            rows, cc = w["rows"], w["cc"]
            w["pairs"] = []
            for pr in range(n_pairs):
                lanes = slice(pr * LANES, (pr + 1) * LANES)
                cols = slice(pr * 2 * L, (pr + 1) * 2 * L)
                k_pair = km_s[rows, lanes]
                q_bd = block_diag_rows(qm_s[rows, lanes])
                k_bd = block_diag_rows(k_pair)
                s_t = (_dot_tb(k_pair, q_bd) * w["p"][:, cols]).astype(BF16)
                s_bd = jnp.concatenate(
                    [jnp.concatenate([s_t[:, 0:L], zeros_ll], axis=1),
                     jnp.concatenate([zeros_ll, s_t[:, L:2 * L]], axis=1)], axis=0)
                v_t = jnp.concatenate([vmt_s[cc, 2 * pr], vmt_s[cc, 2 * pr + 1]], axis=1)
                intra = _dot(v_t, s_bd)
                wv = (v_t.astype(F32) * w["ws"][:, cols]).astype(BF16)
                w["pairs"].append((q_bd, intra, _dot(wv, k_bd)))
            yield
        for w in work:
            d, cc = w["d"], w["cc"]
            for pr, (q_bd, intra, update) in enumerate(w["pairs"]):
                cols = slice(pr * 2 * L, (pr + 1) * 2 * L)
                state = c_state[d][pr]
                tot = intra + w["w_inter"][:, cols] * _dot_tb(state.astype(BF16), q_bd)
                h_t = tot[0:ML_DV, :] / jnp.maximum(jnp.abs(tot[ML_DV:ML_DV + 1, :]), w["floor"][:, cols])
                for e in range(2):
                    hr = slice((2 * pr + e) * ML_DV, (2 * pr + e + 1) * ML_DV)
                    hmt_s[cc, hr, :] += h_t[:, e * L:(e + 1) * L]
                wc = w["wc"]
                wc_pair = jnp.where(lane_row < ML_DK, wc[:, 2 * pr * L:(2 * pr + 1) * L],
                                    wc[:, (2 * pr + 1) * L:(2 * pr + 2) * L])
                c_state[d][pr] = wc_pair * state + update
            yield
        for d in m_state:
            for h in range(ML_HEADS):
                m_s[d * ML_HEADS + h:d * ML_HEADS + h + 1, :] = m_state[d][:, h * L:(h + 1) * L]
            for pr in range(n_pairs):
                c_s[d * n_pairs + pr] = c_state[d][pr]

    def scan_streams(cs):
        return [(d, c if d == 0 else n_chunks - 1 - c) for c in cs for d in range(2)]

    if lat:
        def body2(n, carry):
            _run(_interleave([(attend(n, pl.multiple_of(n * tq, tq), 2), 1),
                              (mlstm_streams(scan_streams([n])), 1)]))
            return carry
        lax.fori_loop(0, n_chunks, body2, 0)
    else:
        def all_tiles():
            for n in range(T // tq):
                yield from attend(n, n * tq, 2)
        yield from _interleave([(all_tiles(), 1), (mlstm_streams(scan_streams(range(n_chunks))), 3)])

    if not lat:
        for d in range(2):
            for pr in range(n_pairs):
                c_t = jnp.transpose(c_s[d * n_pairs + pr, 0:ML_DV, :])
                for e in range(2):
                    c_o[d, 2 * pr + e] = c_t[e * ML_DK:(e + 1) * ML_DK, :]
            for h in range(ML_HEADS):
                idx = d * n_pairs + h // 2
                hl = slice((h % 2) * ML_DK, (h % 2 + 1) * ML_DK)
                n_o[d, h:h + 1, :] = c_s[idx, ML_DV:ML_DV + 1, hl]
                m_o[d:d + 1, h:h + 1] = m_s[d * ML_HEADS + h:d * ML_HEADS + h + 1, 0:1]

    def phase3_merge(i, out):
        rows = row_slice(i, 0, row_block)
        u = u_s[rows, :]

        def proj(c0, width):
            if wt_s is not None and c0 >= C_ZW:
                return _dot(u, wt_s[:, c0 - C_ZW:c0 - C_ZW + width])
            return _dot(u, w_ref[:, c0:c0 + width])

        hm = jnp.concatenate(
            [jnp.concatenate([jnp.transpose(hmt_s[i * chunks_per_block + j, h * ML_DV:(h + 1) * ML_DV, :])
                              for h in range(ML_HEADS)], axis=1)
             for j in range(chunks_per_block)], axis=0)
        ys = (ya_s[rows, :].astype(F32), om_s[rows, :].astype(F32) * hm, yw_s[rows, :].astype(F32))
        yield
        merged = None
        for b, (yb, zc) in enumerate(zip(ys, (C_ZA, C_ZM, C_ZW))):
            z = proj(zc, BRANCH_W)
            t = (yb * (z * _sigmoid(z))).astype(BF16)
            yield
            pb = _sigmoid(proj(C_GMERGE + b * D_MODEL, D_MODEL)) * _dot(t, wb_ref[b])
            merged = pb if merged is None else merged + pb
            yield
        out.append(_dot(merged.astype(BF16), wo_ref[...]))
        yield

    def phase3_norm(i, o):
        for j in range(0, row_block, norm_rows):
            piece = row_slice(i, j, norm_rows)
            hres = ALPHA * x_ref[piece, :] + gate * o[j:j + norm_rows, :]
            y_ref[piece, :] = _layer_norm(hres) * lng_ref[...] + lnb_ref[...]
            yield

    if lat:
        @pl.when(first_step)
        def _():
            for cp in late_copies:
                cp.wait()

    def phase3(i):
        box = []
        yield from phase3_merge(i, box)
        yield from phase3_norm(i, box[0])

    if not lat:
        box = []
        yield from _interleave([(phase3_merge(0, box), 1), (phase1_norm(0, x_next_ref, u_next_s), 1)])
        yield from phase3_norm(0, box[0])
    else:
        def body4(i, carry):
            _run(phase3(i))
            return carry
        lax.fori_loop(0, n_rows, body4, 0)


def _const_spec(shape):
    nd = len(shape)
    return pl.BlockSpec(shape, lambda b: (0,) * nd, pipeline_mode=pl.Buffered(1))


def _layer_spec(shape, layer):
    nd = len(shape) - 1
    return pl.BlockSpec((None,) + tuple(shape[1:]), lambda b: (layer,) + (0,) * nd,
                        pipeline_mode=pl.Buffered(1))


def _layer_call(lat, layer, x, mod, weights, consts, extra):
    B, T, _ = x.shape
    S = T + PAST_LEN if lat else T
    n_chunks = T // CHUNK

    seq_spec = pl.BlockSpec((None, T, D_MODEL), lambda b: (b, 0, 0))
    out_seq_spec = seq_spec
    if lat:
        mod_spec = pl.BlockSpec((None, None, 3, D_MODEL), lambda b: (layer, b + 1, 0, 0))
    else:
        mod_spec = pl.BlockSpec((None, None, 3, D_MODEL), lambda b: (layer, 0, 0, 0))
    *vmem_consts, sink = consts
    in_specs = [seq_spec, mod_spec]
    w_all, wb_all, wo_all = weights[:3]
    if lat:
        in_specs.append(
            pl.BlockSpec((None, D_MODEL, C_ZW), lambda b: (layer, 0, 0), pipeline_mode=pl.Buffered(1)))
        in_specs += [pl.BlockSpec(memory_space=pl.ANY)] * 2
        in_specs += [_layer_spec(a.shape, layer) for a in weights[3:]]
    else:
        in_specs += [_layer_spec(a.shape, layer) for a in weights]
    in_specs += [_const_spec(a.shape) for a in vmem_consts]
    in_specs.append(pl.BlockSpec(memory_space=pltpu.SMEM))
    args = [x, mod, *weights, *vmem_consts, sink]

    if lat:
        cos, sa, sb, cka, cva, ckw, cvw, sc, sn, sm = extra
        in_specs += [_const_spec(cos.shape)] * 3
        kv_spec = pl.BlockSpec((None, None, N_KV, HEAD_DIM, PAST_LEN), lambda b: (b, layer, 0, 0, 0))
        in_specs += [kv_spec] * 4
        in_specs += [
            pl.BlockSpec((None, None, 2, ML_HEADS, ML_DK, ML_DV), lambda b: (b, layer, 0, 0, 0, 0)),
            pl.BlockSpec((None, None, 2, ML_HEADS, ML_DK), lambda b: (b, layer, 0, 0, 0)),
            pl.BlockSpec((None, None, 2, ML_HEADS), lambda b: (b, layer, 0, 0)),
        ]
        args += [cos, sa, sb, cka, cva, ckw, cvw, sc, sn, sm]
        out_specs = out_seq_spec
        out_shape = jax.ShapeDtypeStruct((B, T, D_MODEL), F32)
        aliases = {}
    else:
        kv_shape = (B, DEPTH, N_KV, HEAD_DIM, T)
        c_shape = (B, DEPTH, 2, ML_HEADS, ML_DK, ML_DV)
        if layer == 0:
            aliases = {}
            ld = DEPTH
            layer_idx = 0
        else:
            first_acc = len(args)
            in_specs += [pl.BlockSpec(memory_space=pl.ANY)] * len(extra)
            args += list(extra)
            aliases = {first_acc + i: 1 + i for i in range(len(extra))}
            assert len(extra) == CTX_ALIASED
            ld = None
            layer_idx = layer
        in_specs.append(pl.BlockSpec((None, T, D_MODEL), lambda b: (jnp.minimum(b + 1, B - 1), 0, 0)))
        args.append(x)
        kv_out = pl.BlockSpec((None, ld, N_KV, HEAD_DIM, T), lambda b: (b, layer_idx, 0, 0, 0))
        out_specs = [
            seq_spec, kv_out, kv_out, kv_out, kv_out,
            pl.BlockSpec((None, ld, 2, ML_HEADS, ML_DK, ML_DV), lambda b: (b, layer_idx, 0, 0, 0, 0)),
            pl.BlockSpec((None, 2, ML_HEADS, ML_DK), lambda b: (b, 0, 0, 0)),
            pl.BlockSpec((None, 2, ML_HEADS), lambda b: (b, 0, 0)),
        ]
        out_shape = [jax.ShapeDtypeStruct((B, T, D_MODEL), F32)]
        out_shape += [jax.ShapeDtypeStruct(kv_shape, F32)] * 4 + [jax.ShapeDtypeStruct(c_shape, F32)]
        out_shape += [
            jax.ShapeDtypeStruct((B, 2, ML_HEADS, ML_DK), F32),
            jax.ShapeDtypeStruct((B, 2, ML_HEADS), F32),
        ]

    scratch = [
        ((T, D_MODEL) if lat else (2, T, D_MODEL), BF16),
        ((N_HEADS, T, HEAD_DIM), BF16),
        ((N_HEADS, T, HEAD_DIM), BF16),
        ((N_KV, S, HEAD_DIM), BF16),
        ((S // CHUNK, N_KV, VT_ROWS, CHUNK), BF16),
        ((N_KV, T, HEAD_DIM), BF16),
        ((n_chunks, N_KV, VT_ROWS, CHUNK), BF16),
        ((T, ML_HEADS * ML_DK), BF16),
        ((T, ML_HEADS * ML_DK), BF16),
        ((n_chunks, ML_HEADS, ST_ROWS, CHUNK), BF16),
        ((T, ML_HEADS * ML_DV), BF16),
        ((T, LANES), F32),
        ((T, BRANCH_W), BF16),
        ((T, BRANCH_W), BF16),
        ((n_chunks, ML_HEADS * ML_DV, CHUNK), F32),
        ((ML_HEADS, ST_ROWS, 2 * ML_DK), F32),
        ((2 * ML_HEADS, LANES), F32),
    ]
    if lat:
        scratch += [((N_KV, PAST_LEN, HEAD_DIM), BF16),
                    ((N_KV, VT_ROWS, PAST_LEN), BF16)]
    scratch = [pltpu.VMEM(shape, dtype) for shape, dtype in scratch]
    if lat:
        in_specs.append(pl.BlockSpec(memory_space=pl.ANY))
        args.append(w_all)
        scratch += [
            pltpu.VMEM((D_MODEL, W_COLS - C_ZW), BF16),
            pltpu.VMEM(tuple(wb_all.shape[1:]), BF16),
            pltpu.VMEM(tuple(wo_all.shape[1:]), BF16),
            pltpu.SemaphoreType.DMA((3,)),
        ]

    return pl.pallas_call(
        functools.partial(_layer_kernel, lat, T, layer),
        grid=(B,),
        in_specs=in_specs,
        out_specs=out_specs,
        out_shape=out_shape,
        scratch_shapes=scratch,
        input_output_aliases=aliases,
        compiler_params=pltpu.CompilerParams(
            dimension_semantics=("arbitrary",),
            vmem_limit_bytes=LATENT_VMEM_LIMIT_BYTES if lat else VMEM_LIMIT_BYTES),
        name=("latent_layer" if lat else "context_layer") + str(layer),
    )(*args)


def _rope_tables(T):
    f32 = np.float32
    rows = T // GRID_W
    row = np.repeat(np.arange(rows, dtype=f32), GRID_W)
    col = np.tile(np.arange(GRID_W, dtype=f32), rows)
    inv = (f32(ROPE_THETA) ** (-np.arange(0, AXIS_DIM, 2, dtype=f32) / f32(AXIS_DIM))).astype(f32)
    ar = row[:, None] * inv
    ac = col[:, None] * inv
    ang = np.concatenate([ar, ar, ac, ac], axis=-1)
    ang = np.concatenate([ang, ang], axis=-1)
    first_half = (np.arange(LANES) % AXIS_DIM) < (AXIS_DIM // 2)
    cos = np.cos(ang).astype(f32)
    sin = np.sin(ang).astype(f32)
    sa = np.where(first_half[None, :], -sin, f32(0.0))
    sb = np.where(first_half[None, :], f32(0.0), sin)
    return jnp.asarray(cos), jnp.asarray(sa), jnp.asarray(sb)


def _constants(sink_logit):
    grp = np.arange(N_HEADS * HEAD_DIM) // HEAD_DIM
    bd = jnp.asarray(grp[:, None] == grp[None, :], dtype=BF16)
    ti = np.arange(CHUNK)
    tril = jnp.asarray(ti[None, :] <= ti[:, None], dtype=BF16)
    triu = jnp.asarray(ti[None, :] >= ti[:, None], dtype=BF16)
    return bd, tril, triu, sink_logit.reshape(-1)


def kernel(x_prompt, x_sample, cache_attn_k, cache_attn_v, cache_win_k, cache_win_v, state_mlstm_C,
           state_mlstm_n, state_mlstm_m, c, c_ctx, w_mod, b_mod, w_in, qk_gain, sink_logit,
           mlstm_gate_bias, w_branch, w_out, ln_gain, ln_bias):
    dec_b = x_sample.shape[0]
    assert dec_b + 1 <= 8
    cond = jnp.concatenate([c_ctx[None, :], c, jnp.zeros((8 - 1 - dec_b, D_MODEL), F32)], axis=0)
    mod = _modulation(cond, w_mod, b_mod).reshape(DEPTH, 8, 3, D_MODEL)

    consts = _constants(sink_logit)
    cos, sa, sb = _rope_tables(x_sample.shape[1])
    lane_pad = jnp.zeros((DEPTH, 1, LANES - N_GATES), F32)
    weights = (
        _prep_in_weight(jnp.swapaxes(w_in, 1, 2)), w_branch.astype(BF16), w_out.astype(BF16),
        jnp.tile(qk_gain[:, 0:1, :], (1, 1, N_HEADS)),
        jnp.tile(qk_gain[:, 1:2, :], (1, 1, N_KV)),
        jnp.concatenate([mlstm_gate_bias.reshape(DEPTH, 1, N_GATES), lane_pad], axis=2),
        ln_gain[:, None, :], ln_bias[:, None, :],
    )

    ctx = None
    cached_t = [jnp.swapaxes(a, -1, -2)
                for a in (cache_attn_k, cache_attn_v, cache_win_k, cache_win_v, state_mlstm_C)]
    xp, xs = x_prompt, x_sample
    small = []
    for l in range(DEPTH):
        outs = _layer_call(False, l, xp, mod, weights, consts, ctx)
        xp = outs[0]
        ctx = tuple(outs[1:6])
        small.append(outs[6:])
        extra = (cos, sa, sb, *cached_t, state_mlstm_n, state_mlstm_m)
        xs = _layer_call(True, l, xs, mod, weights, consts, extra)
    new_n, new_m = [jnp.stack([small[l][i] for l in range(DEPTH)], axis=1) for i in range(2)]
    return (xp, xs, *[jnp.swapaxes(a, -1, -2) for a in ctx], new_n, new_m)
```

```python
import functools

import jax
import jax.numpy as jnp
import numpy as np
from jax import lax
from jax.experimental import pallas as pl
from jax.experimental.pallas import tpu as pltpu

F32 = jnp.float32
BF16 = jnp.bfloat16

D_MODEL = 1024
DEPTH = 2
PAST_LEN = 256
GRID_W = 64
HEAD_DIM = 64
N_HEADS = 8
N_KV = 2
N_GROUP = N_HEADS // N_KV
WINDOW = 128
ML_HEADS = 4
ML_DK = 64
ML_DV = 128
CHUNK = 128
BRANCH_W = 512
ROPE_THETA = 10000.0
AXIS_DIM = HEAD_DIM // 2
LN_EPS = 1e-6
RMS_EPS = 1e-6
ALPHA = (2.0 * DEPTH) ** 0.25
LOG2E = float(np.log2(np.e))
LANES = 128
BF16_ROWS = 16
ROW_BLOCK = 256
MOD_COLS = 1536
CTX_ALIASED = 5
N_SHARED = 14
VMEM_LIMIT_BYTES = 60 * 1024 * 1024
LATENT_VMEM_LIMIT_BYTES = 62 * 1024 * 1024
VT_ROWS = HEAD_DIM + BF16_ROWS
ST_ROWS = ML_DV + BF16_ROWS

_SIZES = (512, 128, 128, 512, 256, 256, 512, 512, 128, 512, 512, 128, 128, 512, 3072)
_OFF = [int(v) for v in np.concatenate([[0], np.cumsum(_SIZES)])]
(C_QA, C_KA, C_VA, C_ZA, C_QM, C_KM, C_VM, C_OM, C_GT, C_ZM, C_QW, C_KW, C_VW, C_ZW, C_GMERGE) = _OFF[:-1]
W_COLS = _OFF[-1]
GATE_COL = 2816
N_GATES = 4 * ML_HEADS


def _dot(a, b):
    return jnp.dot(a, b, preferred_element_type=F32)


def _dot_tb(a, b):
    return lax.dot_general(a, b, (((1,), (1,)), ((), ())), preferred_element_type=F32)


def _split(a):
    hi = a.astype(BF16)
    lo = (a - hi.astype(F32)).astype(BF16)
    return hi, lo


def _layer_norm(x):
    mu = jnp.mean(x, axis=-1, keepdims=True)
    xc = x - mu
    var = jnp.mean(xc * xc, axis=-1, keepdims=True)
    return xc * lax.rsqrt(var + LN_EPS)


def _log_sigmoid(x):
    return jnp.minimum(x, 0.0) - jnp.log(1.0 + jnp.exp(-jnp.abs(x)))


def _sigmoid(x):
    return jax.nn.sigmoid(x)


def _mod_kernel(c_ref, w_ref, b_ref, o_ref):
    c = c_ref[...]
    s = c * _sigmoid(c)
    s_hi, s_lo = _split(s)
    w = w_ref[...]
    w_hi, w_lo = _split(w)
    o_ref[...] = _dot(s_hi, w_hi) + _dot(s_lo, w_hi) + _dot(s_hi, w_lo) + b_ref[...]


def _modulation(cond, w_mod, b_mod):
    rows = cond.shape[0]
    tn = MOD_COLS
    return pl.pallas_call(
        _mod_kernel,
        grid=(DEPTH, 3 * D_MODEL // tn),
        in_specs=[
            pl.BlockSpec((rows, D_MODEL), lambda l, j: (0, 0)),
            pl.BlockSpec((None, D_MODEL, tn), lambda l, j: (l, 0, j)),
            pl.BlockSpec((None, 1, tn), lambda l, j: (l, 0, j)),
        ],
        out_specs=pl.BlockSpec((None, rows, tn), lambda l, j: (l, 0, j)),
        out_shape=jax.ShapeDtypeStruct((DEPTH, rows, 3 * D_MODEL), F32),
        compiler_params=pltpu.CompilerParams(dimension_semantics=("arbitrary", "arbitrary")),
        name="adaln_modulation",
    )(cond, w_mod, b_mod.reshape(DEPTH, 1, 3 * D_MODEL))


PREP_K = 256


def _prep_kernel(wt_ref, o_ref):
    lane = lax.broadcasted_iota(jnp.int32, (PREP_K, LANES), 1)
    for c0 in range(0, W_COLS, LANES):
        r0 = c0 if c0 <= GATE_COL else c0 - (LANES - N_GATES)
        slab = jnp.transpose(wt_ref[r0:r0 + LANES, :])
        if c0 == GATE_COL:
            slab = jnp.where(lane < N_GATES, slab, 0.0)
        o_ref[:, c0:c0 + LANES] = slab.astype(BF16)


def _prep_in_weight(w_in_t):
    n_in = w_in_t.shape[1]
    return pl.pallas_call(
        _prep_kernel,
        grid=(DEPTH, D_MODEL // PREP_K),
        in_specs=[pl.BlockSpec((None, n_in, PREP_K), lambda l, i: (l, 0, i))],
        out_specs=pl.BlockSpec((None, PREP_K, W_COLS), lambda l, i: (l, i, 0)),
        out_shape=jax.ShapeDtypeStruct((DEPTH, D_MODEL, W_COLS), BF16),
        compiler_params=pltpu.CompilerParams(
            dimension_semantics=("arbitrary", "arbitrary"), vmem_limit_bytes=VMEM_LIMIT_BYTES),
        name="in_weight_prep",
    )(w_in_t)


def _rope(x, cos, sa, sb):
    return x * cos + pltpu.roll(x, LANES - AXIS_DIM // 2, 1) * sa + pltpu.roll(x, AXIS_DIM // 2, 1) * sb


def _group_rms(x, bd, gain):
    ms = _dot((x * x).astype(BF16), bd) * (1.0 / HEAD_DIM)
    return x * lax.rsqrt(ms + RMS_EPS) * gain


def _ones_row_tile():
    r = lax.broadcasted_iota(jnp.int32, (BF16_ROWS, LANES), 0)
    return jnp.where(r == 0, 1.0, 0.0).astype(BF16)


def _attend_chains(chains, results, depth):
    state = []
    for qs, tiles, sink_row in chains:
        m_cols = qs.shape[0]
        if sink_row is None:
            m = jnp.full((1, m_cols), -jnp.inf, F32)
            acc = jnp.zeros((VT_ROWS, m_cols), F32)
        else:
            m = sink_row
            r = lax.broadcasted_iota(jnp.int32, (VT_ROWS, m_cols), 0)
            acc = jnp.where(r == HEAD_DIM, 1.0, 0.0)
        state.append([m, acc])
    order = [(t, c) for t in range(max(len(tiles) for _, tiles, _ in chains))
             for c, (_, tiles, _) in enumerate(chains) if t < len(tiles)]

    def issue(t, c):
        qs, tiles, _ = chains[c]
        tile = tiles[t]()
        return tile, _dot_tb(tile[0], qs)

    pending = [issue(*item) for item in order[:depth]]
    yield
    for k, (t, c) in enumerate(order):
        (_, v_t, valid), s = pending.pop(0)
        if k + depth < len(order):
            pending.append(issue(*order[k + depth]))
        if valid is not None:
            s = jnp.where(valid, s, -jnp.inf)
        m, acc = state[c]
        m_new = jnp.maximum(m, jnp.max(s, axis=0, keepdims=True))
        p = jnp.exp2(s - m_new).astype(BF16)
        state[c] = [m_new, acc * jnp.exp2(m - m_new) + _dot(v_t, p)]
        if k + 1 == len(order) or order[k + 1][0] != t:
            yield
    results.extend(acc[0:HEAD_DIM, :] / acc[HEAD_DIM:HEAD_DIM + 1, :] for _, acc in state)


def _interleave(weighted):
    live = [[g, n] for g, n in weighted]
    while live:
        for item in list(live):
            for _ in range(item[1]):
                try:
                    next(item[0])
                except StopIteration:
                    live.remove(item)
                    break
        yield


def _run(gen):
    for _ in gen:
        pass


def _layer_kernel(lat, T, layer, n_seq, *refs):
    if lat:
        _run(_sequence(lat, T, layer, n_seq, False, *refs))
        return
    step = pl.program_id(0)

    @pl.when(step < n_seq)
    def _():
        _run(_sequence(lat, T, layer, n_seq, False, *refs))

    @pl.when(step == n_seq)
    def _():
        _run(_sequence(lat, T, layer, n_seq, True, *refs))


def _sequence(lat, T, layer, n_seq, flush_only, *refs):
    refs = list(refs)
    x_ref, mod_ref, w_ref, wb_ref, wo_ref = refs[:5]
    g0_ref, g1_ref, gb_ref, lng_ref, lnb_ref = refs[5:10]
    bd_ref, tril_ref, triu_ref, sink_ref = refs[10:N_SHARED]
    pos = N_SHARED
    if lat:
        cos_ref, sa_ref, sb_ref = refs[pos:pos + 3]
        cka_ref, cva_ref, ckw_ref, cvw_ref, c0_ref, n0_ref, m0_ref = refs[pos + 3:pos + 10]
        pos += 10
        w_hbm = refs[pos]
        y_ref = refs[pos + 1]
        pos += 2
    else:
        if layer > 0:
            pos += CTX_ALIASED
        x_next_ref = refs[pos]
        pos += 1
        y_ref, ka_o, va_o, kw_o, vw_o, c_o, n_o, m_o = refs[pos:pos + 8]
        pos += 8
        if layer == 0:
            for full in (ka_o, va_o, kw_o, vw_o, c_o):
                full[1:] = jnp.zeros((DEPTH - 1,) + full.shape[1:], F32)
            ka_o, va_o, kw_o, vw_o, c_o = (full.at[0] for full in (ka_o, va_o, kw_o, vw_o, c_o))
    (u_s, qa_s, qw_s, ka_s, vat_s, kw_s, vwt_s, qm_s, km_s, vmt_s, om_s, g_s,
     ya_s, yw_s, hmt_s, c_s, m_s) = refs[pos:pos + 17]
    pos += 17
    if not lat:
        step = pl.program_id(0)
        u_next_s = u_s.at[(step + 1) % 2]
        u_s = u_s.at[step % 2]
        hres_s = refs[pos]
    wt_s = None
    if lat:
        kwc_s, vwct_s = refs[pos:pos + 2]
        wb_hbm, wo_hbm = wb_ref, wo_ref
        wt_s, wb_ref, wo_ref, w_sem = refs[pos + 2:pos + 6]

        first_step = pl.program_id(0) == 0
        late_copies = (
            pltpu.make_async_copy(w_hbm.at[layer, :, C_ZW:W_COLS], wt_s, w_sem.at[0]),
            pltpu.make_async_copy(wb_hbm.at[layer], wb_ref, w_sem.at[1]),
            pltpu.make_async_copy(wo_hbm.at[layer], wo_ref, w_sem.at[2]),
        )

        @pl.when(first_step)
        def _():
            for cp in late_copies:
                cp.start()

    row_block = min(T, ROW_BLOCK)
    n_rows = T // row_block
    n_chunks = T // CHUNK
    chunks_per_block = row_block // CHUNK
    shift = mod_ref[0:1, :]
    scale = mod_ref[1:2, :]
    gate = mod_ref[2:3, :]
    ones_tile = _ones_row_tile()

    def deferred_norm():
        for j in range(0, T, 64):
            y_ref[j:j + 64, :] = _layer_norm(hres_s[j:j + 64, :]) * lng_ref[...] + lnb_ref[...]
            yield

    if flush_only:
        yield from deferred_norm()
        return

    norm_rows = 64

    def row_slice(i, j, n):
        r0 = i * row_block + j
        return pl.ds(r0 if isinstance(r0, int) else pl.multiple_of(r0, n), n)

    def phase1_norm(i, src=x_ref, dst=u_s):
        for j in range(0, row_block, norm_rows):
            piece = row_slice(i, j, norm_rows)
            dst[piece, :] = (_layer_norm(src[piece, :]) * (1.0 + scale) + shift).astype(BF16)
            yield

    def phase1_proj(i):
        R = row_block
        rows = row_slice(i, 0, R)
        u = u_s[rows, :]

        def proj(c0, width):
            return _dot(u, w_ref[:, c0:c0 + width])

        if lat:
            cos = cos_ref[rows, :]
            sa = sa_ref[rows, :]
            sb = sb_ref[rows, :]

        def rope_wide(v):
            if not lat:
                return v
            slabs = [_rope(v[:, j:j + LANES], cos, sa, sb) for j in range(0, v.shape[1], LANES)]
            return slabs[0] if len(slabs) == 1 else jnp.concatenate(slabs, axis=1)

        qscale = LOG2E * HEAD_DIM ** -0.5
        qa = rope_wide(_group_rms(proj(C_QA, 512), bd_ref[...], g0_ref[...])) * qscale
        for h in range(N_HEADS):
            qa_s[h, rows, :] = qa[:, h * HEAD_DIM:(h + 1) * HEAD_DIM].astype(BF16)
        yield
        ka_n = _group_rms(proj(C_KA, LANES), bd_ref[0:LANES, 0:LANES], g1_ref[...])
        ka = rope_wide(ka_n)
        va = proj(C_VA, LANES)
        qw = rope_wide(proj(C_QW, 512)) * qscale
        for h in range(N_HEADS):
            qw_s[h, rows, :] = qw[:, h * HEAD_DIM:(h + 1) * HEAD_DIM].astype(BF16)
        yield
        kw_raw = proj(C_KW, LANES)
        kw = rope_wide(kw_raw)
        vw = proj(C_VW, LANES)
        for kv in range(N_KV):
            sl = slice(kv * HEAD_DIM, (kv + 1) * HEAD_DIM)
            ka_s[kv, rows, :] = ka[:, sl].astype(BF16)
            kw_s[kv, rows, :] = kw[:, sl].astype(BF16)
        yield
        qm_s[rows, :] = proj(C_QM, 256).astype(BF16)
        km_s[rows, :] = (proj(C_KM, 256) * (ML_DK ** -0.5)).astype(BF16)
        yield
        vm = proj(C_VM, 512)
        yield
        om_s[rows, :] = _sigmoid(proj(C_OM, 512)).astype(BF16)
        g_s[rows, :] = proj(C_GT, LANES) + gb_ref[...]
        yield
        for j in range(chunks_per_block):
            cj = i * chunks_per_block + j
            cr = slice(j * CHUNK, (j + 1) * CHUNK)
            va_t = jnp.transpose(va[cr, :])
            vw_t = jnp.transpose(vw[cr, :])
            for kv in range(N_KV):
                hd = slice(kv * HEAD_DIM, (kv + 1) * HEAD_DIM)
                vat_s[cj, kv, 0:HEAD_DIM, :] = va_t[hd, :].astype(BF16)
                vat_s[cj, kv, HEAD_DIM:VT_ROWS, :] = ones_tile
                vwt_s[cj, kv, 0:HEAD_DIM, :] = vw_t[hd, :].astype(BF16)
                vwt_s[cj, kv, HEAD_DIM:VT_ROWS, :] = ones_tile
            if not lat:
                tc = slice(i * row_block + j * CHUNK, i * row_block + (j + 1) * CHUNK)
                ka_t = jnp.transpose(ka_n[cr, :])
                kw_t = jnp.transpose(kw_raw[cr, :])
                for kv in range(N_KV):
                    hd = slice(kv * HEAD_DIM, (kv + 1) * HEAD_DIM)
                    ka_o[kv, :, tc] = ka_t[hd, :]
                    va_o[kv, :, tc] = va_t[hd, :]
                    kw_o[kv, :, tc] = kw_t[hd, :]
                    vw_o[kv, :, tc] = vw_t[hd, :]
            for h in range(ML_HEADS):
                vmt_s[cj, h, 0:ML_DV, :] = jnp.transpose(vm[cr, h * ML_DV:(h + 1) * ML_DV]).astype(BF16)
                vmt_s[cj, h, ML_DV:ST_ROWS, :] = ones_tile
            yield

    if not lat:
        @pl.when(step == 0)
        def _():
            _run(phase1_norm(0))
            hres_s[...] = jnp.zeros_like(hres_s)
        yield from _interleave([(phase1_proj(0), 1), (deferred_norm(), 1)])
    else:
        def body1(i, carry):
            _run(phase1_norm(i))
            _run(phase1_proj(i))
            return carry
        lax.fori_loop(0, n_rows, body1, 0)

    if lat:
        for j in range(PAST_LEN // CHUNK):
            cols = slice(j * CHUNK, (j + 1) * CHUNK)
            ka_c = jnp.transpose(jnp.concatenate([cka_ref[kv, :, cols] for kv in range(N_KV)], axis=0))
            kw_c = jnp.transpose(jnp.concatenate([ckw_ref[kv, :, cols] for kv in range(N_KV)], axis=0))
            for kv in range(N_KV):
                hd = slice(kv * HEAD_DIM, (kv + 1) * HEAD_DIM)
                ka_s[kv, T + j * CHUNK:T + (j + 1) * CHUNK, :] = ka_c[:, hd].astype(BF16)
                kwc_s[kv, cols, :] = kw_c[:, hd].astype(BF16)
                vat_s[n_chunks + j, kv, 0:HEAD_DIM, :] = cva_ref[kv, :, cols].astype(BF16)
                vat_s[n_chunks + j, kv, HEAD_DIM:VT_ROWS, :] = ones_tile
                vwct_s[kv, 0:HEAD_DIM, cols] = cvw_ref[kv, :, cols].astype(BF16)
                vwct_s[kv, HEAD_DIM:VT_ROWS, cols] = ones_tile

    tq = CHUNK
    scores_in_flight = 2 if lat else 2 * N_KV
    m_cols = N_GROUP * tq

    def sink_row(kv):
        col = lax.broadcasted_iota(jnp.int32, (1, m_cols), 1)
        row = jnp.full((1, m_cols), sink_ref[layer * N_HEADS + kv * N_GROUP], F32)
        for g in range(1, N_GROUP):
            row = jnp.where(col >= g * tq, sink_ref[layer * N_HEADS + kv * N_GROUP + g], row)
        return row * LOG2E

    def load_q(q_s, kv, q0):
        return q_s[kv * N_GROUP:(kv + 1) * N_GROUP, pl.ds(q0, tq), :].reshape(m_cols, HEAD_DIM)

    def store_heads(dst, q0, kv, o_t):
        for p in range(N_GROUP // 2):
            blk = jnp.concatenate([o_t[:, (2 * p) * tq:(2 * p + 1) * tq],
                                   o_t[:, (2 * p + 1) * tq:(2 * p + 2) * tq]], axis=0)
            c0 = (kv * N_GROUP + 2 * p) * HEAD_DIM
            dst[pl.ds(q0, tq), c0:c0 + LANES] = jnp.transpose(blk).astype(dst.dtype)

    def key_tiles(k_ref, vt_ref, kv, n_key_chunks, chunks_per_tile):
        def tile(c0):
            def load():
                k = k_ref[kv, c0 * CHUNK:(c0 + chunks_per_tile) * CHUNK, :]
                v_t = [vt_ref[c0 + j, kv] for j in range(chunks_per_tile)]
                return k, (v_t[0] if len(v_t) == 1 else jnp.concatenate(v_t, axis=1)), None
            return load
        return [tile(c0) for c0 in range(0, n_key_chunks, chunks_per_tile)]

    def band_tiles(kv, n, q0):
        j0 = jnp.clip(n - 1, 0, n_chunks - 3)
        w0 = pl.multiple_of(j0 * CHUNK, CHUNK)

        def band(c0, nc):
            def load():
                rows = nc * CHUNK
                kpos = w0 + c0 * CHUNK + lax.broadcasted_iota(jnp.int32, (rows, m_cols), 0)
                qpos = q0 + (lax.broadcasted_iota(jnp.int32, (rows, m_cols), 1) & (tq - 1))
                v_t = [vwt_s[j0 + c0 + j, kv] for j in range(nc)]
                return (kw_s[kv, pl.ds(w0 + c0 * CHUNK, rows), :],
                        v_t[0] if nc == 1 else jnp.concatenate(v_t, axis=1),
                        jnp.abs(kpos - qpos) <= WINDOW)
            return load
        return [band(0, 2), band(2, 1), lambda: (kwc_s[kv], vwct_s[kv], None)]

    def attend(n, q0, chunks_per_tile):
        chains, dsts = [], []
        n_key_chunks = ka_s.shape[1] // CHUNK
        for kv in range(N_KV):
            chains.append((load_q(qa_s, kv, q0), key_tiles(ka_s, vat_s, kv, n_key_chunks, chunks_per_tile), None))
            dsts.append((ya_s, kv))
        for kv in range(N_KV):
            tiles = band_tiles(kv, n, q0) if lat else key_tiles(kw_s, vwt_s, kv, n_chunks, chunks_per_tile)
            chains.append((load_q(qw_s, kv, q0), tiles, sink_row(kv)))
            dsts.append((yw_s, kv))
        results = []
        yield from _attend_chains(chains, results, scores_in_flight)
        for (dst, kv), o_t in zip(dsts, results):
            store_heads(dst, q0, kv, o_t)

    n_pairs = ML_HEADS // 2
    hmt_s[...] = jnp.zeros_like(hmt_s)
    for d in range(2):
        if lat:
            for pr in range(n_pairs):
                c_s[d * n_pairs + pr, 0:ML_DV, :] = jnp.transpose(
                    jnp.concatenate([c0_ref[d, 2 * pr], c0_ref[d, 2 * pr + 1]], axis=0))
        for h in range(ML_HEADS):
            idx = d * n_pairs + h // 2
            hl = slice((h % 2) * ML_DK, (h % 2 + 1) * ML_DK)
            c_s[idx, ML_DV:ST_ROWS, hl] = jnp.zeros((BF16_ROWS, ML_DK), F32)
            if lat:
                c_s[idx, ML_DV:ML_DV + 1, hl] = n0_ref[d, h:h + 1, :]
                m_s[d * ML_HEADS + h:d * ML_HEADS + h + 1, :] = jnp.broadcast_to(
                    m0_ref[d:d + 1, h:h + 1], (1, LANES))
            else:
                c_s[idx, 0:ML_DV, hl] = jnp.zeros((ML_DV, ML_DK), F32)
                m_s[d * ML_HEADS + h:d * ML_HEADS + h + 1, :] = jnp.zeros((1, LANES), F32)

    L = CHUNK
    s_idx = lax.broadcasted_iota(jnp.int32, (L, ML_HEADS * L), 0)
    t_idx = lax.broadcasted_iota(jnp.int32, (L, ML_HEADS * L), 1) & (L - 1)
    lane_row = lax.broadcasted_iota(jnp.int32, (1, LANES), 1)
    low_half = lax.broadcasted_iota(jnp.int32, (L, LANES), 1) < ML_DK
    ones_ll = jnp.ones((L, L), BF16)
    zeros_ll = jnp.zeros((L, L), BF16)

    def heads_row(src, r0, c0=0):
        return jnp.concatenate([src[r0 + h:r0 + h + 1, c0:c0 + L] for h in range(ML_HEADS)], axis=1)

    def block_diag_rows(x):
        zero = jnp.zeros_like(x)
        return jnp.concatenate([jnp.where(low_half, x, zero), jnp.where(low_half, zero, x)], axis=0)

    def mlstm_streams(streams):
        m_state = {d: heads_row(m_s, d * ML_HEADS) for d in sorted({d for d, _ in streams})}
        c_state = {d: [c_s[d * n_pairs + pr] for pr in range(n_pairs)] for d in m_state}
        work = []
        for d, cc in streams:
            r0 = cc * L if isinstance(cc, int) else pl.multiple_of(cc * L, L)
            rows = pl.ds(r0, L)
            g = g_s[rows, :]
            f_hi, f_lo = _split(_log_sigmoid(g))
            tri_c = tril_ref[...] if d == 0 else triu_ref[...]
            tri_r = triu_ref[...] if d == 0 else tril_ref[...]
            cum = _dot(tri_c, f_hi) + _dot(tri_c, f_lo)
            r = g - pltpu.roll(cum, LANES - ML_HEADS, 1)
            gi0 = 2 * ML_HEADS * d
            rb = jnp.concatenate([jnp.broadcast_to(r[:, gi0 + h:gi0 + h + 1], (L, L))
                                  for h in range(ML_HEADS)], axis=1)
            g_t = jnp.transpose(g)[0:N_GATES, :]
            ft_hi, ft_lo = _split(_log_sigmoid(g_t))
            tr = jnp.concatenate([tri_r, ones_ll], axis=1)
            ct = _dot(ft_hi, tr) + _dot(ft_lo, tr)
            work.append(dict(d=d, cc=cc, rows=rows, rb=rb, g_t=g_t, ct=ct))
            yield
        for w in work:
            d = w["d"]
            valid = (s_idx <= t_idx) if d == 0 else (s_idx >= t_idx)
            gi0 = 2 * ML_HEADS * d
            gf0 = gi0 + ML_HEADS
            b_row = heads_row(w["ct"], gf0)
            b_last = heads_row(w["ct"], gf0, L)
            i_row = heads_row(w["g_t"], gi0)
            m_prev = m_state[d]
            a_row = b_row + m_prev
            dm = jnp.where(valid, b_row + w["rb"], -jnp.inf)
            mt = jnp.maximum(a_row, jnp.max(dm, axis=0, keepdims=True))
            w["p"] = jnp.exp(dm - mt)
            w["w_inter"] = jnp.exp(a_row - mt)
            w["floor"] = jnp.exp(-mt)
            g_row = b_last - b_row + i_row
            g_max = jnp.concatenate(
                [jnp.broadcast_to(jnp.max(g_row[:, h * L:(h + 1) * L], axis=-1, keepdims=True), (1, L))
                 for h in range(ML_HEADS)], axis=1)
            m_new = jnp.maximum(b_last + m_prev, g_max)
            w["ws"] = jnp.exp(g_row - m_new)
            w["wc"] = jnp.exp(b_last + m_prev - m_new)
            m_state[d] = m_new
            yield
        for w in work:
            rows, cc = w["rows"], w["cc"]
            w["pairs"] = []
            for pr in range(n_pairs):
                lanes = slice(pr * LANES, (pr + 1) * LANES)
                cols = slice(pr * 2 * L, (pr + 1) * 2 * L)
                k_pair = km_s[rows, lanes]
                q_bd = block_diag_rows(qm_s[rows, lanes])
                k_bd = block_diag_rows(k_pair)
                s_t = (_dot_tb(k_pair, q_bd) * w["p"][:, cols]).astype(BF16)
                s_bd = jnp.concatenate(
                    [jnp.concatenate([s_t[:, 0:L], zeros_ll], axis=1),
                     jnp.concatenate([zeros_ll, s_t[:, L:2 * L]], axis=1)], axis=0)
                v_t = jnp.concatenate([vmt_s[cc, 2 * pr], vmt_s[cc, 2 * pr + 1]], axis=1)
                intra = _dot(v_t, s_bd)
                wv = (v_t.astype(F32) * w["ws"][:, cols]).astype(BF16)
                w["pairs"].append((q_bd, intra, _dot(wv, k_bd)))
            yield
        for w in work:
            d, cc = w["d"], w["cc"]
            for pr, (q_bd, intra, update) in enumerate(w["pairs"]):
                cols = slice(pr * 2 * L, (pr + 1) * 2 * L)
                state = c_state[d][pr]
                tot = intra + w["w_inter"][:, cols] * _dot_tb(state.astype(BF16), q_bd)
                h_t = tot[0:ML_DV, :] / jnp.maximum(jnp.abs(tot[ML_DV:ML_DV + 1, :]), w["floor"][:, cols])
                for e in range(2):
                    hr = slice((2 * pr + e) * ML_DV, (2 * pr + e + 1) * ML_DV)
                    hmt_s[cc, hr, :] += h_t[:, e * L:(e + 1) * L]
                wc = w["wc"]
                wc_pair = jnp.where(lane_row < ML_DK, wc[:, 2 * pr * L:(2 * pr + 1) * L],
                                    wc[:, (2 * pr + 1) * L:(2 * pr + 2) * L])
                c_state[d][pr] = wc_pair * state + update
            yield
        for d in m_state:
            for h in range(ML_HEADS):
                m_s[d * ML_HEADS + h:d * ML_HEADS + h + 1, :] = m_state[d][:, h * L:(h + 1) * L]
            for pr in range(n_pairs):
                c_s[d * n_pairs + pr] = c_state[d][pr]

    def scan_streams(cs):
        return [(d, c if d == 0 else n_chunks - 1 - c) for c in cs for d in range(2)]

    if lat:
        def body2(n, carry):
            _run(_interleave([(attend(n, pl.multiple_of(n * tq, tq), 2), 1),
                              (mlstm_streams(scan_streams([n])), 1)]))
            return carry
        lax.fori_loop(0, n_chunks, body2, 0)
    else:
        def all_tiles():
            for n in range(T // tq):
                yield from attend(n, n * tq, 2)
        yield from _interleave([(all_tiles(), 1), (mlstm_streams(scan_streams(range(n_chunks))), 3)])

    if not lat:
        for d in range(2):
            for pr in range(n_pairs):
                c_t = jnp.transpose(c_s[d * n_pairs + pr, 0:ML_DV, :])
                for e in range(2):
                    c_o[d, 2 * pr + e] = c_t[e * ML_DK:(e + 1) * ML_DK, :]
            for h in range(ML_HEADS):
                idx = d * n_pairs + h // 2
                hl = slice((h % 2) * ML_DK, (h % 2 + 1) * ML_DK)
                n_o[d, h:h + 1, :] = c_s[idx, ML_DV:ML_DV + 1, hl]
                m_o[d:d + 1, h:h + 1] = m_s[d * ML_HEADS + h:d * ML_HEADS + h + 1, 0:1]

    def phase3_merge(i, out):
        rows = row_slice(i, 0, row_block)
        u = u_s[rows, :]

        def proj(c0, width):
            if wt_s is not None and c0 >= C_ZW:
                return _dot(u, wt_s[:, c0 - C_ZW:c0 - C_ZW + width])
            return _dot(u, w_ref[:, c0:c0 + width])

        hm = jnp.concatenate(
            [jnp.concatenate([jnp.transpose(hmt_s[i * chunks_per_block + j, h * ML_DV:(h + 1) * ML_DV, :])
                              for h in range(ML_HEADS)], axis=1)
             for j in range(chunks_per_block)], axis=0)
        ys = (ya_s[rows, :].astype(F32), om_s[rows, :].astype(F32) * hm, yw_s[rows, :].astype(F32))
        yield
        merged = None
        for b, (yb, zc) in enumerate(zip(ys, (C_ZA, C_ZM, C_ZW))):
            z = proj(zc, BRANCH_W)
            t = (yb * (z * _sigmoid(z))).astype(BF16)
            yield
            pb = _sigmoid(proj(C_GMERGE + b * D_MODEL, D_MODEL)) * _dot(t, wb_ref[b])
            merged = pb if merged is None else merged + pb
            yield
        out.append(_dot(merged.astype(BF16), wo_ref[...]))
        yield

    def phase3_norm(i, o):
        for j in range(0, row_block, norm_rows):
            piece = row_slice(i, j, norm_rows)
            hres = ALPHA * x_ref[piece, :] + gate * o[j:j + norm_rows, :]
            y_ref[piece, :] = _layer_norm(hres) * lng_ref[...] + lnb_ref[...]
            yield

    if lat:
        @pl.when(first_step)
        def _():
            for cp in late_copies:
                cp.wait()

    def phase3(i):
        box = []
        yield from phase3_merge(i, box)
        yield from phase3_norm(i, box[0])

    if not lat:
        box = []
        yield from _interleave([(phase3_merge(0, box), 1), (phase1_norm(0, x_next_ref, u_next_s), 1)])
        hres_s[...] = ALPHA * x_ref[...] + gate * box[0]
    else:
        def body4(i, carry):
            _run(phase3(i))
            return carry
        lax.fori_loop(0, n_rows, body4, 0)


def _const_spec(shape):
    nd = len(shape)
    return pl.BlockSpec(shape, lambda b: (0,) * nd, pipeline_mode=pl.Buffered(1))


def _layer_spec(shape, layer):
    nd = len(shape) - 1
    return pl.BlockSpec((None,) + tuple(shape[1:]), lambda b: (layer,) + (0,) * nd,
                        pipeline_mode=pl.Buffered(1))


def _layer_call(lat, layer, x, mod, weights, consts, extra):
    B, T, _ = x.shape
    S = T + PAST_LEN if lat else T
    n_chunks = T // CHUNK

    if lat:
        seq_spec = out_seq_spec = pl.BlockSpec((None, T, D_MODEL), lambda b: (b, 0, 0))
        cur = lambda b: b
    else:
        cur = lambda b: jnp.minimum(b, B - 1)
        seq_spec = pl.BlockSpec((None, T, D_MODEL), lambda b: (cur(b), 0, 0))
        out_seq_spec = pl.BlockSpec((None, T, D_MODEL), lambda b: (jnp.maximum(b - 1, 0), 0, 0))
    if lat:
        mod_spec = pl.BlockSpec((None, None, 3, D_MODEL), lambda b: (layer, b + 1, 0, 0))
    else:
        mod_spec = pl.BlockSpec((None, None, 3, D_MODEL), lambda b: (layer, 0, 0, 0))
    *vmem_consts, sink = consts
    in_specs = [seq_spec, mod_spec]
    w_all, wb_all, wo_all = weights[:3]
    if lat:
        in_specs.append(
            pl.BlockSpec((None, D_MODEL, C_ZW), lambda b: (layer, 0, 0), pipeline_mode=pl.Buffered(1)))
        in_specs += [pl.BlockSpec(memory_space=pl.ANY)] * 2
        in_specs += [_layer_spec(a.shape, layer) for a in weights[3:]]
    else:
        in_specs += [_layer_spec(a.shape, layer) for a in weights]
    in_specs += [_const_spec(a.shape) for a in vmem_consts]
    in_specs.append(pl.BlockSpec(memory_space=pltpu.SMEM))
    args = [x, mod, *weights, *vmem_consts, sink]

    if lat:
        cos, sa, sb, cka, cva, ckw, cvw, sc, sn, sm = extra
        in_specs += [_const_spec(cos.shape)] * 3
        kv_spec = pl.BlockSpec((None, None, N_KV, HEAD_DIM, PAST_LEN), lambda b: (b, layer, 0, 0, 0))
        in_specs += [kv_spec] * 4
        in_specs += [
            pl.BlockSpec((None, None, 2, ML_HEADS, ML_DK, ML_DV), lambda b: (b, layer, 0, 0, 0, 0)),
            pl.BlockSpec((None, None, 2, ML_HEADS, ML_DK), lambda b: (b, layer, 0, 0, 0)),
            pl.BlockSpec((None, None, 2, ML_HEADS), lambda b: (b, layer, 0, 0)),
        ]
        args += [cos, sa, sb, cka, cva, ckw, cvw, sc, sn, sm]
        out_specs = out_seq_spec
        out_shape = jax.ShapeDtypeStruct((B, T, D_MODEL), F32)
        aliases = {}
    else:
        kv_shape = (B, DEPTH, N_KV, HEAD_DIM, T)
        c_shape = (B, DEPTH, 2, ML_HEADS, ML_DK, ML_DV)
        if layer == 0:
            aliases = {}
            ld = DEPTH
            layer_idx = 0
        else:
            first_acc = len(args)
            in_specs += [pl.BlockSpec(memory_space=pl.ANY)] * len(extra)
            args += list(extra)
            aliases = {first_acc + i: 1 + i for i in range(len(extra))}
            assert len(extra) == CTX_ALIASED
            ld = None
            layer_idx = layer
        in_specs.append(pl.BlockSpec((None, T, D_MODEL), lambda b: (jnp.minimum(b + 1, B - 1), 0, 0)))
        args.append(x)
        kv_out = pl.BlockSpec((None, ld, N_KV, HEAD_DIM, T), lambda b: (cur(b), layer_idx, 0, 0, 0))
        out_specs = [
            out_seq_spec, kv_out, kv_out, kv_out, kv_out,
            pl.BlockSpec((None, ld, 2, ML_HEADS, ML_DK, ML_DV), lambda b: (cur(b), layer_idx, 0, 0, 0, 0)),
            pl.BlockSpec((None, 2, ML_HEADS, ML_DK), lambda b: (cur(b), 0, 0, 0)),
            pl.BlockSpec((None, 2, ML_HEADS), lambda b: (cur(b), 0, 0)),
        ]
        out_shape = [jax.ShapeDtypeStruct((B, T, D_MODEL), F32)]
        out_shape += [jax.ShapeDtypeStruct(kv_shape, F32)] * 4 + [jax.ShapeDtypeStruct(c_shape, F32)]
        out_shape += [
            jax.ShapeDtypeStruct((B, 2, ML_HEADS, ML_DK), F32),
            jax.ShapeDtypeStruct((B, 2, ML_HEADS), F32),
        ]

    scratch = [
        ((T, D_MODEL) if lat else (2, T, D_MODEL), BF16),
        ((N_HEADS, T, HEAD_DIM), BF16),
        ((N_HEADS, T, HEAD_DIM), BF16),
        ((N_KV, S, HEAD_DIM), BF16),
        ((S // CHUNK, N_KV, VT_ROWS, CHUNK), BF16),
        ((N_KV, T, HEAD_DIM), BF16),
        ((n_chunks, N_KV, VT_ROWS, CHUNK), BF16),
        ((T, ML_HEADS * ML_DK), BF16),
        ((T, ML_HEADS * ML_DK), BF16),
        ((n_chunks, ML_HEADS, ST_ROWS, CHUNK), BF16),
        ((T, ML_HEADS * ML_DV), BF16),
        ((T, LANES), F32),
        ((T, BRANCH_W), BF16),
        ((T, BRANCH_W), BF16),
        ((n_chunks, ML_HEADS * ML_DV, CHUNK), F32),
        ((ML_HEADS, ST_ROWS, 2 * ML_DK), F32),
        ((2 * ML_HEADS, LANES), F32),
    ]
    if lat:
        scratch += [((N_KV, PAST_LEN, HEAD_DIM), BF16),
                    ((N_KV, VT_ROWS, PAST_LEN), BF16)]
    if not lat:
        scratch.append(((T, D_MODEL), F32))
    scratch = [pltpu.VMEM(shape, dtype) for shape, dtype in scratch]
    if lat:
        in_specs.append(pl.BlockSpec(memory_space=pl.ANY))
        args.append(w_all)
        scratch += [
            pltpu.VMEM((D_MODEL, W_COLS - C_ZW), BF16),
            pltpu.VMEM(tuple(wb_all.shape[1:]), BF16),
            pltpu.VMEM(tuple(wo_all.shape[1:]), BF16),
            pltpu.SemaphoreType.DMA((3,)),
        ]

    return pl.pallas_call(
        functools.partial(_layer_kernel, lat, T, layer, B),
        grid=(B,) if lat else (B + 1,),
        in_specs=in_specs,
        out_specs=out_specs,
        out_shape=out_shape,
        scratch_shapes=scratch,
        input_output_aliases=aliases,
        compiler_params=pltpu.CompilerParams(
            dimension_semantics=("arbitrary",),
            vmem_limit_bytes=LATENT_VMEM_LIMIT_BYTES if lat else VMEM_LIMIT_BYTES),
        name=("latent_layer" if lat else "context_layer") + str(layer),
    )(*args)


def _rope_tables(T):
    f32 = np.float32
    rows = T // GRID_W
    row = np.repeat(np.arange(rows, dtype=f32), GRID_W)
    col = np.tile(np.arange(GRID_W, dtype=f32), rows)
    inv = (f32(ROPE_THETA) ** (-np.arange(0, AXIS_DIM, 2, dtype=f32) / f32(AXIS_DIM))).astype(f32)
    ar = row[:, None] * inv
    ac = col[:, None] * inv
    ang = np.concatenate([ar, ar, ac, ac], axis=-1)
    ang = np.concatenate([ang, ang], axis=-1)
    first_half = (np.arange(LANES) % AXIS_DIM) < (AXIS_DIM // 2)
    cos = np.cos(ang).astype(f32)
    sin = np.sin(ang).astype(f32)
    sa = np.where(first_half[None, :], -sin, f32(0.0))
    sb = np.where(first_half[None, :], f32(0.0), sin)
    return jnp.asarray(cos), jnp.asarray(sa), jnp.asarray(sb)


def _constants(sink_logit):
    grp = np.arange(N_HEADS * HEAD_DIM) // HEAD_DIM
    bd = jnp.asarray(grp[:, None] == grp[None, :], dtype=BF16)
    ti = np.arange(CHUNK)
    tril = jnp.asarray(ti[None, :] <= ti[:, None], dtype=BF16)
    triu = jnp.asarray(ti[None, :] >= ti[:, None], dtype=BF16)
    return bd, tril, triu, sink_logit.reshape(-1)


def kernel(x_prompt, x_sample, cache_attn_k, cache_attn_v, cache_win_k, cache_win_v, state_mlstm_C,
           state_mlstm_n, state_mlstm_m, c, c_ctx, w_mod, b_mod, w_in, qk_gain, sink_logit,
           mlstm_gate_bias, w_branch, w_out, ln_gain, ln_bias):
    dec_b = x_sample.shape[0]
    assert dec_b + 1 <= 8
    cond = jnp.concatenate([c_ctx[None, :], c, jnp.zeros((8 - 1 - dec_b, D_MODEL), F32)], axis=0)
    mod = _modulation(cond, w_mod, b_mod).reshape(DEPTH, 8, 3, D_MODEL)

    consts = _constants(sink_logit)
    cos, sa, sb = _rope_tables(x_sample.shape[1])
    lane_pad = jnp.zeros((DEPTH, 1, LANES - N_GATES), F32)
    weights = (
        _prep_in_weight(jnp.swapaxes(w_in, 1, 2)), w_branch.astype(BF16), w_out.astype(BF16),
        jnp.tile(qk_gain[:, 0:1, :], (1, 1, N_HEADS)),
        jnp.tile(qk_gain[:, 1:2, :], (1, 1, N_KV)),
        jnp.concatenate([mlstm_gate_bias.reshape(DEPTH, 1, N_GATES), lane_pad], axis=2),
        ln_gain[:, None, :], ln_bias[:, None, :],
    )

    ctx = None
    cached_t = [jnp.swapaxes(a, -1, -2)
                for a in (cache_attn_k, cache_attn_v, cache_win_k, cache_win_v, state_mlstm_C)]
    xp, xs = x_prompt, x_sample
    small = []
    for l in range(DEPTH):
        outs = _layer_call(False, l, xp, mod, weights, consts, ctx)
        xp = outs[0]
        ctx = tuple(outs[1:6])
        small.append(outs[6:])
        extra = (cos, sa, sb, *cached_t, state_mlstm_n, state_mlstm_m)
        xs = _layer_call(True, l, xs, mod, weights, consts, extra)
    new_n, new_m = [jnp.stack([small[l][i] for l in range(DEPTH)], axis=1) for i in range(2)]
    return (xp, xs, *[jnp.swapaxes(a, -1, -2) for a in ctx], new_n, new_m)
```

```python
import functools

import jax
import jax.numpy as jnp
import numpy as np
from jax import lax
from jax.experimental import pallas as pl
from jax.experimental.pallas import tpu as pltpu

F32 = jnp.float32
BF16 = jnp.bfloat16

D_MODEL = 1024
DEPTH = 2
PAST_LEN = 256
GRID_W = 64
HEAD_DIM = 64
N_HEADS = 8
N_KV = 2
N_GROUP = N_HEADS // N_KV
WINDOW = 128
ML_HEADS = 4
ML_DK = 64
ML_DV = 128
CHUNK = 128
BRANCH_W = 512
ROPE_THETA = 10000.0
AXIS_DIM = HEAD_DIM // 2
LN_EPS = 1e-6
RMS_EPS = 1e-6
ALPHA = (2.0 * DEPTH) ** 0.25
LOG2E = float(np.log2(np.e))
LANES = 128
BF16_ROWS = 16
ROW_BLOCK = 256
MOD_COLS = 1536
CTX_ALIASED = 5
N_SHARED = 14
VMEM_LIMIT_BYTES = 60 * 1024 * 1024
LATENT_VMEM_LIMIT_BYTES = 62 * 1024 * 1024
VT_ROWS = HEAD_DIM + BF16_ROWS
ST_ROWS = ML_DV + BF16_ROWS

_SIZES = (512, 128, 128, 512, 256, 256, 512, 512, 128, 512, 512, 128, 128, 512, 3072)
_OFF = [int(v) for v in np.concatenate([[0], np.cumsum(_SIZES)])]
(C_QA, C_KA, C_VA, C_ZA, C_QM, C_KM, C_VM, C_OM, C_GT, C_ZM, C_QW, C_KW, C_VW, C_ZW, C_GMERGE) = _OFF[:-1]
W_COLS = _OFF[-1]
GATE_COL = 2816
N_GATES = 4 * ML_HEADS


def _dot(a, b):
    return jnp.dot(a, b, preferred_element_type=F32)


def _dot_tb(a, b):
    return lax.dot_general(a, b, (((1,), (1,)), ((), ())), preferred_element_type=F32)


def _split(a):
    hi = a.astype(BF16)
    lo = (a - hi.astype(F32)).astype(BF16)
    return hi, lo


def _layer_norm(x):
    mu = jnp.mean(x, axis=-1, keepdims=True)
    xc = x - mu
    var = jnp.mean(xc * xc, axis=-1, keepdims=True)
    return xc * lax.rsqrt(var + LN_EPS)


def _log_sigmoid(x):
    return jnp.minimum(x, 0.0) - jnp.log(1.0 + jnp.exp(-jnp.abs(x)))


def _sigmoid(x):
    return jax.nn.sigmoid(x)


def _mod_kernel(c_ref, w_ref, b_ref, o_ref):
    c = c_ref[...]
    s = c * _sigmoid(c)
    s_hi, s_lo = _split(s)
    w = w_ref[...]
    w_hi, w_lo = _split(w)
    o_ref[...] = _dot(s_hi, w_hi) + _dot(s_lo, w_hi) + _dot(s_hi, w_lo) + b_ref[...]


def _modulation(cond, w_mod, b_mod):
    rows = cond.shape[0]
    tn = MOD_COLS
    return pl.pallas_call(
        _mod_kernel,
        grid=(DEPTH, 3 * D_MODEL // tn),
        in_specs=[
            pl.BlockSpec((rows, D_MODEL), lambda l, j: (0, 0)),
            pl.BlockSpec((None, D_MODEL, tn), lambda l, j: (l, 0, j)),
            pl.BlockSpec((None, 1, tn), lambda l, j: (l, 0, j)),
        ],
        out_specs=pl.BlockSpec((None, rows, tn), lambda l, j: (l, 0, j)),
        out_shape=jax.ShapeDtypeStruct((DEPTH, rows, 3 * D_MODEL), F32),
        compiler_params=pltpu.CompilerParams(dimension_semantics=("arbitrary", "arbitrary")),
        name="adaln_modulation",
    )(cond, w_mod, b_mod.reshape(DEPTH, 1, 3 * D_MODEL))


PREP_K = 256


def _prep_kernel(wt_ref, o_ref):
    lane = lax.broadcasted_iota(jnp.int32, (PREP_K, LANES), 1)
    for c0 in range(0, W_COLS, LANES):
        r0 = c0 if c0 <= GATE_COL else c0 - (LANES - N_GATES)
        slab = jnp.transpose(wt_ref[r0:r0 + LANES, :])
        if c0 == GATE_COL:
            slab = jnp.where(lane < N_GATES, slab, 0.0)
        o_ref[:, c0:c0 + LANES] = slab.astype(BF16)


def _prep_in_weight(w_in_t):
    n_in = w_in_t.shape[1]
    return pl.pallas_call(
        _prep_kernel,
        grid=(DEPTH, D_MODEL // PREP_K),
        in_specs=[pl.BlockSpec((None, n_in, PREP_K), lambda l, i: (l, 0, i))],
        out_specs=pl.BlockSpec((None, PREP_K, W_COLS), lambda l, i: (l, i, 0)),
        out_shape=jax.ShapeDtypeStruct((DEPTH, D_MODEL, W_COLS), BF16),
        compiler_params=pltpu.CompilerParams(
            dimension_semantics=("arbitrary", "arbitrary"), vmem_limit_bytes=VMEM_LIMIT_BYTES),
        name="in_weight_prep",
    )(w_in_t)


def _rope(x, cos, sa, sb):
    return x * cos + pltpu.roll(x, LANES - AXIS_DIM // 2, 1) * sa + pltpu.roll(x, AXIS_DIM // 2, 1) * sb


def _group_rms(x, bd, gain):
    ms = _dot((x * x).astype(BF16), bd) * (1.0 / HEAD_DIM)
    return x * lax.rsqrt(ms + RMS_EPS) * gain


def _ones_row_tile():
    r = lax.broadcasted_iota(jnp.int32, (BF16_ROWS, LANES), 0)
    return jnp.where(r == 0, 1.0, 0.0).astype(BF16)


def _attend_chains(chains, results, depth):
    state = []
    for qs, tiles, sink_row in chains:
        m_cols = qs.shape[0]
        if sink_row is None:
            m = jnp.full((1, m_cols), -jnp.inf, F32)
            acc = jnp.zeros((VT_ROWS, m_cols), F32)
        else:
            m = sink_row
            r = lax.broadcasted_iota(jnp.int32, (VT_ROWS, m_cols), 0)
            acc = jnp.where(r == HEAD_DIM, 1.0, 0.0)
        state.append([m, acc])
    order = [(t, c) for t in range(max(len(tiles) for _, tiles, _ in chains))
             for c, (_, tiles, _) in enumerate(chains) if t < len(tiles)]

    def issue(t, c):
        qs, tiles, _ = chains[c]
        tile = tiles[t]()
        return tile, _dot_tb(tile[0], qs)

    pending = [issue(*item) for item in order[:depth]]
    yield
    for k, (t, c) in enumerate(order):
        (_, v_t, valid), s = pending.pop(0)
        if k + depth < len(order):
            pending.append(issue(*order[k + depth]))
        if valid is not None:
            s = jnp.where(valid, s, -jnp.inf)
        m, acc = state[c]
        m_new = jnp.maximum(m, jnp.max(s, axis=0, keepdims=True))
        p = jnp.exp2(s - m_new).astype(BF16)
        state[c] = [m_new, acc * jnp.exp2(m - m_new) + _dot(v_t, p)]
        if k + 1 == len(order) or order[k + 1][0] != t:
            yield
    results.extend(acc[0:HEAD_DIM, :] / acc[HEAD_DIM:HEAD_DIM + 1, :] for _, acc in state)


def _interleave(weighted):
    live = [[g, n] for g, n in weighted]
    while live:
        for item in list(live):
            for _ in range(item[1]):
                try:
                    next(item[0])
                except StopIteration:
                    live.remove(item)
                    break
        yield


def _run(gen):
    for _ in gen:
        pass


def _layer_kernel(lat, T, layer, *refs):
    _run(_sequence(lat, T, layer, *refs))


def _sequence(lat, T, layer, *refs):
    refs = list(refs)
    x_ref, mod_ref, w_ref, wb_ref, wo_ref = refs[:5]
    g0_ref, g1_ref, gb_ref, lng_ref, lnb_ref = refs[5:10]
    bd_ref, tril_ref, triu_ref, sink_ref = refs[10:N_SHARED]
    pos = N_SHARED
    if lat:
        cos_ref, sa_ref, sb_ref = refs[pos:pos + 3]
        cka_ref, cva_ref, ckw_ref, cvw_ref, c0_ref, n0_ref, m0_ref = refs[pos + 3:pos + 10]
        pos += 10
        w_hbm = refs[pos]
        y_ref = refs[pos + 1]
        pos += 2
    else:
        if layer > 0:
            pos += CTX_ALIASED
        x_next_ref = refs[pos]
        pos += 1
        y_ref, ka_o, va_o, kw_o, vw_o, c_o, n_o, m_o = refs[pos:pos + 8]
        pos += 8
        if layer == 0:
            for full in (ka_o, va_o, kw_o, vw_o, c_o):
                full[1:] = jnp.zeros((DEPTH - 1,) + full.shape[1:], F32)
            ka_o, va_o, kw_o, vw_o, c_o = (full.at[0] for full in (ka_o, va_o, kw_o, vw_o, c_o))
    (u_s, qa_s, qw_s, ka_s, vat_s, kw_s, vwt_s, qm_s, km_s, vmt_s, om_s, g_s,
     ya_s, yw_s, hmt_s, c_s, m_s) = refs[pos:pos + 17]
    pos += 17
    if not lat:
        step = pl.program_id(0)
        u_next_s = u_s.at[(step + 1) % 2]
        u_s = u_s.at[step % 2]
    wt_s = None
    if lat:
        kwc_s, vwct_s = refs[pos:pos + 2]
        wb_hbm, wo_hbm = wb_ref, wo_ref
        wt_s, wb_ref, wo_ref, w_sem = refs[pos + 2:pos + 6]

        first_step = pl.program_id(0) == 0
        late_copies = (
            pltpu.make_async_copy(w_hbm.at[layer, :, C_ZW:W_COLS], wt_s, w_sem.at[0]),
            pltpu.make_async_copy(wb_hbm.at[layer], wb_ref, w_sem.at[1]),
            pltpu.make_async_copy(wo_hbm.at[layer], wo_ref, w_sem.at[2]),
        )

        @pl.when(first_step)
        def _():
            for cp in late_copies:
                cp.start()

    row_block = min(T, ROW_BLOCK)
    n_rows = T // row_block
    n_chunks = T // CHUNK
    chunks_per_block = row_block // CHUNK
    shift = mod_ref[0:1, :]
    scale = mod_ref[1:2, :]
    gate = mod_ref[2:3, :]
    ones_tile = _ones_row_tile()

    norm_rows = 64

    def row_slice(i, j, n):
        r0 = i * row_block + j
        return pl.ds(r0 if isinstance(r0, int) else pl.multiple_of(r0, n), n)

    def phase1_norm(i, src=x_ref, dst=u_s):
        for j in range(0, row_block, norm_rows):
            piece = row_slice(i, j, norm_rows)
            dst[piece, :] = (_layer_norm(src[piece, :]) * (1.0 + scale) + shift).astype(BF16)
            yield

    def phase1_proj(i):
        R = row_block
        rows = row_slice(i, 0, R)
        u = u_s[rows, :]

        def proj(c0, width):
            return _dot(u, w_ref[:, c0:c0 + width])

        if lat:
            cos = cos_ref[rows, :]
            sa = sa_ref[rows, :]
            sb = sb_ref[rows, :]

        def rope_wide(v):
            if not lat:
                return v
            slabs = [_rope(v[:, j:j + LANES], cos, sa, sb) for j in range(0, v.shape[1], LANES)]
            return slabs[0] if len(slabs) == 1 else jnp.concatenate(slabs, axis=1)

        qscale = LOG2E * HEAD_DIM ** -0.5
        qa = rope_wide(_group_rms(proj(C_QA, 512), bd_ref[...], g0_ref[...])) * qscale
        for h in range(N_HEADS):
            qa_s[h, rows, :] = qa[:, h * HEAD_DIM:(h + 1) * HEAD_DIM].astype(BF16)
        yield
        ka_n = _group_rms(proj(C_KA, LANES), bd_ref[0:LANES, 0:LANES], g1_ref[...])
        ka = rope_wide(ka_n)
        va = proj(C_VA, LANES)
        qw = rope_wide(proj(C_QW, 512)) * qscale
        for h in range(N_HEADS):
            qw_s[h, rows, :] = qw[:, h * HEAD_DIM:(h + 1) * HEAD_DIM].astype(BF16)
        yield
        kw_raw = proj(C_KW, LANES)
        kw = rope_wide(kw_raw)
        vw = proj(C_VW, LANES)
        for kv in range(N_KV):
            sl = slice(kv * HEAD_DIM, (kv + 1) * HEAD_DIM)
            ka_s[kv, rows, :] = ka[:, sl].astype(BF16)
            kw_s[kv, rows, :] = kw[:, sl].astype(BF16)
        yield
        qm_s[rows, :] = proj(C_QM, 256).astype(BF16)
        km_s[rows, :] = (proj(C_KM, 256) * (ML_DK ** -0.5)).astype(BF16)
        yield
        vm = proj(C_VM, 512)
        yield
        om_s[rows, :] = _sigmoid(proj(C_OM, 512)).astype(BF16)
        g_s[rows, :] = proj(C_GT, LANES) + gb_ref[...]
        yield
        for j in range(chunks_per_block):
            cj = i * chunks_per_block + j
            cr = slice(j * CHUNK, (j + 1) * CHUNK)
            va_t = jnp.transpose(va[cr, :])
            vw_t = jnp.transpose(vw[cr, :])
            for kv in range(N_KV):
                hd = slice(kv * HEAD_DIM, (kv + 1) * HEAD_DIM)
                vat_s[cj, kv, 0:HEAD_DIM, :] = va_t[hd, :].astype(BF16)
                vat_s[cj, kv, HEAD_DIM:VT_ROWS, :] = ones_tile
                vwt_s[cj, kv, 0:HEAD_DIM, :] = vw_t[hd, :].astype(BF16)
                vwt_s[cj, kv, HEAD_DIM:VT_ROWS, :] = ones_tile
            if not lat:
                tc = slice(i * row_block + j * CHUNK, i * row_block + (j + 1) * CHUNK)
                ka_t = jnp.transpose(ka_n[cr, :])
                kw_t = jnp.transpose(kw_raw[cr, :])
                for kv in range(N_KV):
                    hd = slice(kv * HEAD_DIM, (kv + 1) * HEAD_DIM)
                    ka_o[kv, :, tc] = ka_t[hd, :]
                    va_o[kv, :, tc] = va_t[hd, :]
                    kw_o[kv, :, tc] = kw_t[hd, :]
                    vw_o[kv, :, tc] = vw_t[hd, :]
            for h in range(ML_HEADS):
                vmt_s[cj, h, 0:ML_DV, :] = jnp.transpose(vm[cr, h * ML_DV:(h + 1) * ML_DV]).astype(BF16)
                vmt_s[cj, h, ML_DV:ST_ROWS, :] = ones_tile
            yield

    if not lat:
        @pl.when(step == 0)
        def _():
            _run(phase1_norm(0))
        yield from phase1_proj(0)
    else:
        def body1(i, carry):
            _run(phase1_norm(i))
            _run(phase1_proj(i))
            return carry
        lax.fori_loop(0, n_rows, body1, 0)

    if lat:
        for j in range(PAST_LEN // CHUNK):
            cols = slice(j * CHUNK, (j + 1) * CHUNK)
            ka_c = jnp.transpose(jnp.concatenate([cka_ref[kv, :, cols] for kv in range(N_KV)], axis=0))
            kw_c = jnp.transpose(jnp.concatenate([ckw_ref[kv, :, cols] for kv in range(N_KV)], axis=0))
            for kv in range(N_KV):
                hd = slice(kv * HEAD_DIM, (kv + 1) * HEAD_DIM)
                ka_s[kv, T + j * CHUNK:T + (j + 1) * CHUNK, :] = ka_c[:, hd].astype(BF16)
                kwc_s[kv, cols, :] = kw_c[:, hd].astype(BF16)
                vat_s[n_chunks + j, kv, 0:HEAD_DIM, :] = cva_ref[kv, :, cols].astype(BF16)
                vat_s[n_chunks + j, kv, HEAD_DIM:VT_ROWS, :] = ones_tile
                vwct_s[kv, 0:HEAD_DIM, cols] = cvw_ref[kv, :, cols].astype(BF16)
                vwct_s[kv, HEAD_DIM:VT_ROWS, cols] = ones_tile

    tq = CHUNK
    scores_in_flight = 2 if lat else 2 * N_KV
    m_cols = N_GROUP * tq

    def sink_row(kv):
        col = lax.broadcasted_iota(jnp.int32, (1, m_cols), 1)
        row = jnp.full((1, m_cols), sink_ref[layer * N_HEADS + kv * N_GROUP], F32)
        for g in range(1, N_GROUP):
            row = jnp.where(col >= g * tq, sink_ref[layer * N_HEADS + kv * N_GROUP + g], row)
        return row * LOG2E

    def load_q(q_s, kv, q0):
        return q_s[kv * N_GROUP:(kv + 1) * N_GROUP, pl.ds(q0, tq), :].reshape(m_cols, HEAD_DIM)

    def store_heads(dst, q0, kv, o_t):
        for p in range(N_GROUP // 2):
            blk = jnp.concatenate([o_t[:, (2 * p) * tq:(2 * p + 1) * tq],
                                   o_t[:, (2 * p + 1) * tq:(2 * p + 2) * tq]], axis=0)
            c0 = (kv * N_GROUP + 2 * p) * HEAD_DIM
            dst[pl.ds(q0, tq), c0:c0 + LANES] = jnp.transpose(blk).astype(dst.dtype)

    def key_tiles(k_ref, vt_ref, kv, n_key_chunks, chunks_per_tile):
        def tile(c0):
            def load():
                k = k_ref[kv, c0 * CHUNK:(c0 + chunks_per_tile) * CHUNK, :]
                v_t = [vt_ref[c0 + j, kv] for j in range(chunks_per_tile)]
                return k, (v_t[0] if len(v_t) == 1 else jnp.concatenate(v_t, axis=1)), None
            return load
        return [tile(c0) for c0 in range(0, n_key_chunks, chunks_per_tile)]

    def band_tiles(kv, n, q0):
        j0 = jnp.clip(n - 1, 0, n_chunks - 3)
        w0 = pl.multiple_of(j0 * CHUNK, CHUNK)

        def band(c0, nc):
            def load():
                rows = nc * CHUNK
                kpos = w0 + c0 * CHUNK + lax.broadcasted_iota(jnp.int32, (rows, m_cols), 0)
                qpos = q0 + (lax.broadcasted_iota(jnp.int32, (rows, m_cols), 1) & (tq - 1))
                v_t = [vwt_s[j0 + c0 + j, kv] for j in range(nc)]
                return (kw_s[kv, pl.ds(w0 + c0 * CHUNK, rows), :],
                        v_t[0] if nc == 1 else jnp.concatenate(v_t, axis=1),
                        jnp.abs(kpos - qpos) <= WINDOW)
            return load
        return [band(0, 2), band(2, 1), lambda: (kwc_s[kv], vwct_s[kv], None)]

    def attend(n, q0, chunks_per_tile):
        chains, dsts = [], []
        n_key_chunks = ka_s.shape[1] // CHUNK
        for kv in range(N_KV):
            chains.append((load_q(qa_s, kv, q0), key_tiles(ka_s, vat_s, kv, n_key_chunks, chunks_per_tile), None))
            dsts.append((ya_s, kv))
        for kv in range(N_KV):
            tiles = band_tiles(kv, n, q0) if lat else key_tiles(kw_s, vwt_s, kv, n_chunks, chunks_per_tile)
            chains.append((load_q(qw_s, kv, q0), tiles, sink_row(kv)))
            dsts.append((yw_s, kv))
        results = []
        yield from _attend_chains(chains, results, scores_in_flight)
        for (dst, kv), o_t in zip(dsts, results):
            store_heads(dst, q0, kv, o_t)

    n_pairs = ML_HEADS // 2
    hmt_s[...] = jnp.zeros_like(hmt_s)
    for d in range(2):
        if lat:
            for pr in range(n_pairs):
                c_s[d * n_pairs + pr, 0:ML_DV, :] = jnp.transpose(
                    jnp.concatenate([c0_ref[d, 2 * pr], c0_ref[d, 2 * pr + 1]], axis=0))
        for h in range(ML_HEADS):
            idx = d * n_pairs + h // 2
            hl = slice((h % 2) * ML_DK, (h % 2 + 1) * ML_DK)
            c_s[idx, ML_DV:ST_ROWS, hl] = jnp.zeros((BF16_ROWS, ML_DK), F32)
            if lat:
                c_s[idx, ML_DV:ML_DV + 1, hl] = n0_ref[d, h:h + 1, :]
                m_s[d * ML_HEADS + h:d * ML_HEADS + h + 1, :] = jnp.broadcast_to(
                    m0_ref[d:d + 1, h:h + 1], (1, LANES))
            else:
                c_s[idx, 0:ML_DV, hl] = jnp.zeros((ML_DV, ML_DK), F32)
                m_s[d * ML_HEADS + h:d * ML_HEADS + h + 1, :] = jnp.zeros((1, LANES), F32)

    L = CHUNK
    s_idx = lax.broadcasted_iota(jnp.int32, (L, ML_HEADS * L), 0)
    t_idx = lax.broadcasted_iota(jnp.int32, (L, ML_HEADS * L), 1) & (L - 1)
    lane_row = lax.broadcasted_iota(jnp.int32, (1, LANES), 1)
    low_half = lax.broadcasted_iota(jnp.int32, (L, LANES), 1) < ML_DK
    ones_ll = jnp.ones((L, L), BF16)
    zeros_ll = jnp.zeros((L, L), BF16)

    def heads_row(src, r0, c0=0):
        return jnp.concatenate([src[r0 + h:r0 + h + 1, c0:c0 + L] for h in range(ML_HEADS)], axis=1)

    def block_diag_rows(x):
        zero = jnp.zeros_like(x)
        return jnp.concatenate([jnp.where(low_half, x, zero), jnp.where(low_half, zero, x)], axis=0)

    def mlstm_streams(streams):
        m_state = {d: heads_row(m_s, d * ML_HEADS) for d in sorted({d for d, _ in streams})}
        c_state = {d: [c_s[d * n_pairs + pr] for pr in range(n_pairs)] for d in m_state}
        work = []
        for d, cc in streams:
            r0 = cc * L if isinstance(cc, int) else pl.multiple_of(cc * L, L)
            rows = pl.ds(r0, L)
            g = g_s[rows, :]
            tri_r = triu_ref[...] if d == 0 else tril_ref[...]
            gi0 = 2 * ML_HEADS * d

            def row_sums():
                g_t = jnp.transpose(g)[0:N_GATES, :]
                ft_hi, ft_lo = _split(_log_sigmoid(g_t))
                tr = jnp.concatenate([tri_r, ones_ll], axis=1)
                return g_t, _dot(ft_hi, tr) + _dot(ft_lo, tr)

            if lat:
                g_t, ct = row_sums()
                r_rows = g_t[gi0:gi0 + 8, :] - pltpu.roll(ct[gi0:gi0 + 8, 0:L], ML_HEADS, 0)
                r = jnp.transpose(jnp.concatenate([r_rows, jnp.zeros((L - 8, L), F32)], axis=0))
                r_lane0 = 0
            else:
                f_hi, f_lo = _split(_log_sigmoid(g))
                tri_c = tril_ref[...] if d == 0 else triu_ref[...]
                cum = _dot(tri_c, f_hi) + _dot(tri_c, f_lo)
                r = g - pltpu.roll(cum, LANES - ML_HEADS, 1)
                r_lane0 = gi0
            rb = jnp.concatenate([jnp.broadcast_to(r[:, r_lane0 + h:r_lane0 + h + 1], (L, L))
                                  for h in range(ML_HEADS)], axis=1)
            if not lat:
                g_t, ct = row_sums()
            work.append(dict(d=d, cc=cc, rows=rows, rb=rb, g_t=g_t, ct=ct))
            yield
        for w in work:
            d = w["d"]
            valid = (s_idx <= t_idx) if d == 0 else (s_idx >= t_idx)
            gi0 = 2 * ML_HEADS * d
            gf0 = gi0 + ML_HEADS
            b_row = heads_row(w["ct"], gf0)
            b_last = heads_row(w["ct"], gf0, L)
            i_row = heads_row(w["g_t"], gi0)
            m_prev = m_state[d]
            a_row = b_row + m_prev
            dm = jnp.where(valid, b_row + w["rb"], -jnp.inf)
            mt = jnp.maximum(a_row, jnp.max(dm, axis=0, keepdims=True))
            w["p"] = jnp.exp(dm - mt)
            w["w_inter"] = jnp.exp(a_row - mt)
            w["floor"] = jnp.exp(-mt)
            g_row = b_last - b_row + i_row
            g_max = jnp.concatenate(
                [jnp.broadcast_to(jnp.max(g_row[:, h * L:(h + 1) * L], axis=-1, keepdims=True), (1, L))
                 for h in range(ML_HEADS)], axis=1)
            m_new = jnp.maximum(b_last + m_prev, g_max)
            w["ws"] = jnp.exp(g_row - m_new)
            w["wc"] = jnp.exp(b_last + m_prev - m_new)
            m_state[d] = m_new
            yield
        for w in work:
            rows, cc = w["rows"], w["cc"]
            w["pairs"] = []
            for pr in range(n_pairs):
                lanes = slice(pr * LANES, (pr + 1) * LANES)
                cols = slice(pr * 2 * L, (pr + 1) * 2 * L)
                k_pair = km_s[rows, lanes]
                q_bd = block_diag_rows(qm_s[rows, lanes])
                k_bd = block_diag_rows(k_pair)
                s_t = (_dot_tb(k_pair, q_bd) * w["p"][:, cols]).astype(BF16)
                s_bd = jnp.concatenate(
                    [jnp.concatenate([s_t[:, 0:L], zeros_ll], axis=1),
                     jnp.concatenate([zeros_ll, s_t[:, L:2 * L]], axis=1)], axis=0)
                v_t = jnp.concatenate([vmt_s[cc, 2 * pr], vmt_s[cc, 2 * pr + 1]], axis=1)
                intra = _dot(v_t, s_bd)
                wv = (v_t.astype(F32) * w["ws"][:, cols]).astype(BF16)
                w["pairs"].append((q_bd, intra, _dot(wv, k_bd)))
            yield
        for w in work:
            d, cc = w["d"], w["cc"]
            for pr, (q_bd, intra, update) in enumerate(w["pairs"]):
                cols = slice(pr * 2 * L, (pr + 1) * 2 * L)
                state = c_state[d][pr]
                tot = intra + w["w_inter"][:, cols] * _dot_tb(state.astype(BF16), q_bd)
                h_t = tot[0:ML_DV, :] / jnp.maximum(jnp.abs(tot[ML_DV:ML_DV + 1, :]), w["floor"][:, cols])
                for e in range(2):
                    hr = slice((2 * pr + e) * ML_DV, (2 * pr + e + 1) * ML_DV)
                    hmt_s[cc, hr, :] += h_t[:, e * L:(e + 1) * L]
                wc = w["wc"]
                wc_pair = jnp.where(lane_row < ML_DK, wc[:, 2 * pr * L:(2 * pr + 1) * L],
                                    wc[:, (2 * pr + 1) * L:(2 * pr + 2) * L])
                c_state[d][pr] = wc_pair * state + update
            yield
        for d in m_state:
            for h in range(ML_HEADS):
                m_s[d * ML_HEADS + h:d * ML_HEADS + h + 1, :] = m_state[d][:, h * L:(h + 1) * L]
            for pr in range(n_pairs):
                c_s[d * n_pairs + pr] = c_state[d][pr]

    def scan_streams(cs):
        return [(d, c if d == 0 else n_chunks - 1 - c) for c in cs for d in range(2)]

    if lat:
        def body2(n, carry):
            _run(_interleave([(attend(n, pl.multiple_of(n * tq, tq), 2), 1),
                              (mlstm_streams(scan_streams([n])), 1)]))
            return carry
        lax.fori_loop(0, n_chunks, body2, 0)
    else:
        def all_tiles():
            for n in range(T // tq):
                yield from attend(n, n * tq, 2)
        yield from _interleave([(all_tiles(), 1), (mlstm_streams(scan_streams(range(n_chunks))), 3)])

    if not lat:
        for d in range(2):
            for pr in range(n_pairs):
                c_t = jnp.transpose(c_s[d * n_pairs + pr, 0:ML_DV, :])
                for e in range(2):
                    c_o[d, 2 * pr + e] = c_t[e * ML_DK:(e + 1) * ML_DK, :]
            for h in range(ML_HEADS):
                idx = d * n_pairs + h // 2
                hl = slice((h % 2) * ML_DK, (h % 2 + 1) * ML_DK)
                n_o[d, h:h + 1, :] = c_s[idx, ML_DV:ML_DV + 1, hl]
                m_o[d:d + 1, h:h + 1] = m_s[d * ML_HEADS + h:d * ML_HEADS + h + 1, 0:1]

    def phase3_merge(i, out):
        rows = row_slice(i, 0, row_block)
        u = u_s[rows, :]

        def proj(c0, width):
            if wt_s is not None and c0 >= C_ZW:
                return _dot(u, wt_s[:, c0 - C_ZW:c0 - C_ZW + width])
            return _dot(u, w_ref[:, c0:c0 + width])

        hm = jnp.concatenate(
            [jnp.concatenate([jnp.transpose(hmt_s[i * chunks_per_block + j, h * ML_DV:(h + 1) * ML_DV, :])
                              for h in range(ML_HEADS)], axis=1)
             for j in range(chunks_per_block)], axis=0)
        ys = (ya_s[rows, :].astype(F32), om_s[rows, :].astype(F32) * hm, yw_s[rows, :].astype(F32))
        yield
        merged = None
        for b, (yb, zc) in enumerate(zip(ys, (C_ZA, C_ZM, C_ZW))):
            z = proj(zc, BRANCH_W)
            t = (yb * (z * _sigmoid(z))).astype(BF16)
            yield
            pb = _sigmoid(proj(C_GMERGE + b * D_MODEL, D_MODEL)) * _dot(t, wb_ref[b])
            merged = pb if merged is None else merged + pb
            yield
        out.append(_dot(merged.astype(BF16), wo_ref[...]))
        yield

    def phase3_norm(i, o):
        for j in range(0, row_block, norm_rows):
            piece = row_slice(i, j, norm_rows)
            hres = ALPHA * x_ref[piece, :] + gate * o[j:j + norm_rows, :]
            y_ref[piece, :] = _layer_norm(hres) * lng_ref[...] + lnb_ref[...]
            yield

    if lat:
        @pl.when(first_step)
        def _():
            for cp in late_copies:
                cp.wait()

    def phase3(i):
        box = []
        yield from phase3_merge(i, box)
        yield from phase3_norm(i, box[0])

    if not lat:
        box = []
        yield from _interleave([(phase3_merge(0, box), 1), (phase1_norm(0, x_next_ref, u_next_s), 1)])
        yield from phase3_norm(0, box[0])
    else:
        def body4(i, carry):
            _run(phase3(i))
            return carry
        lax.fori_loop(0, n_rows, body4, 0)


def _const_spec(shape):
    nd = len(shape)
    return pl.BlockSpec(shape, lambda b: (0,) * nd, pipeline_mode=pl.Buffered(1))


def _layer_spec(shape, layer):
    nd = len(shape) - 1
    return pl.BlockSpec((None,) + tuple(shape[1:]), lambda b: (layer,) + (0,) * nd,
                        pipeline_mode=pl.Buffered(1))


def _layer_call(lat, layer, x, mod, weights, consts, extra):
    B, T, _ = x.shape
    S = T + PAST_LEN if lat else T
    n_chunks = T // CHUNK

    seq_spec = pl.BlockSpec((None, T, D_MODEL), lambda b: (b, 0, 0))
    out_seq_spec = seq_spec
    if lat:
        mod_spec = pl.BlockSpec((None, None, 3, D_MODEL), lambda b: (layer, b + 1, 0, 0))
    else:
        mod_spec = pl.BlockSpec((None, None, 3, D_MODEL), lambda b: (layer, 0, 0, 0))
    *vmem_consts, sink = consts
    in_specs = [seq_spec, mod_spec]
    w_all, wb_all, wo_all = weights[:3]
    if lat:
        in_specs.append(
            pl.BlockSpec((None, D_MODEL, C_ZW), lambda b: (layer, 0, 0), pipeline_mode=pl.Buffered(1)))
        in_specs += [pl.BlockSpec(memory_space=pl.ANY)] * 2
        in_specs += [_layer_spec(a.shape, layer) for a in weights[3:]]
    else:
        in_specs += [_layer_spec(a.shape, layer) for a in weights]
    in_specs += [_const_spec(a.shape) for a in vmem_consts]
    in_specs.append(pl.BlockSpec(memory_space=pltpu.SMEM))
    args = [x, mod, *weights, *vmem_consts, sink]

    if lat:
        cos, sa, sb, cka, cva, ckw, cvw, sc, sn, sm = extra
        in_specs += [_const_spec(cos.shape)] * 3
        kv_spec = pl.BlockSpec((None, None, N_KV, HEAD_DIM, PAST_LEN), lambda b: (b, layer, 0, 0, 0))
        in_specs += [kv_spec] * 4
        in_specs += [
            pl.BlockSpec((None, None, 2, ML_HEADS, ML_DK, ML_DV), lambda b: (b, layer, 0, 0, 0, 0)),
            pl.BlockSpec((None, None, 2, ML_HEADS, ML_DK), lambda b: (b, layer, 0, 0, 0)),
            pl.BlockSpec((None, None, 2, ML_HEADS), lambda b: (b, layer, 0, 0)),
        ]
        args += [cos, sa, sb, cka, cva, ckw, cvw, sc, sn, sm]
        out_specs = out_seq_spec
        out_shape = jax.ShapeDtypeStruct((B, T, D_MODEL), F32)
        aliases = {}
    else:
        kv_shape = (B, DEPTH, N_KV, HEAD_DIM, T)
        c_shape = (B, DEPTH, 2, ML_HEADS, ML_DK, ML_DV)
        if layer == 0:
            aliases = {}
            ld = DEPTH
            layer_idx = 0
        else:
            first_acc = len(args)
            in_specs += [pl.BlockSpec(memory_space=pl.ANY)] * len(extra)
            args += list(extra)
            aliases = {first_acc + i: 1 + i for i in range(len(extra))}
            assert len(extra) == CTX_ALIASED
            ld = None
            layer_idx = layer
        in_specs.append(pl.BlockSpec((None, T, D_MODEL), lambda b: (jnp.minimum(b + 1, B - 1), 0, 0)))
        args.append(x)
        kv_out = pl.BlockSpec((None, ld, N_KV, HEAD_DIM, T), lambda b: (b, layer_idx, 0, 0, 0))
        out_specs = [
            seq_spec, kv_out, kv_out, kv_out, kv_out,
            pl.BlockSpec((None, ld, 2, ML_HEADS, ML_DK, ML_DV), lambda b: (b, layer_idx, 0, 0, 0, 0)),
            pl.BlockSpec((None, 2, ML_HEADS, ML_DK), lambda b: (b, 0, 0, 0)),
            pl.BlockSpec((None, 2, ML_HEADS), lambda b: (b, 0, 0)),
        ]
        out_shape = [jax.ShapeDtypeStruct((B, T, D_MODEL), F32)]
        out_shape += [jax.ShapeDtypeStruct(kv_shape, F32)] * 4 + [jax.ShapeDtypeStruct(c_shape, F32)]
        out_shape += [
            jax.ShapeDtypeStruct((B, 2, ML_HEADS, ML_DK), F32),
            jax.ShapeDtypeStruct((B, 2, ML_HEADS), F32),
        ]

    scratch = [
        ((T, D_MODEL) if lat else (2, T, D_MODEL), BF16),
        ((N_HEADS, T, HEAD_DIM), BF16),
        ((N_HEADS, T, HEAD_DIM), BF16),
        ((N_KV, S, HEAD_DIM), BF16),
        ((S // CHUNK, N_KV, VT_ROWS, CHUNK), BF16),
        ((N_KV, T, HEAD_DIM), BF16),
        ((n_chunks, N_KV, VT_ROWS, CHUNK), BF16),
        ((T, ML_HEADS * ML_DK), BF16),
        ((T, ML_HEADS * ML_DK), BF16),
        ((n_chunks, ML_HEADS, ST_ROWS, CHUNK), BF16),
        ((T, ML_HEADS * ML_DV), BF16),
        ((T, LANES), F32),
        ((T, BRANCH_W), BF16),
        ((T, BRANCH_W), BF16),
        ((n_chunks, ML_HEADS * ML_DV, CHUNK), F32),
        ((ML_HEADS, ST_ROWS, 2 * ML_DK), F32),
        ((2 * ML_HEADS, LANES), F32),
    ]
    if lat:
        scratch += [((N_KV, PAST_LEN, HEAD_DIM), BF16),
                    ((N_KV, VT_ROWS, PAST_LEN), BF16)]
    scratch = [pltpu.VMEM(shape, dtype) for shape, dtype in scratch]
    if lat:
        in_specs.append(pl.BlockSpec(memory_space=pl.ANY))
        args.append(w_all)
        scratch += [
            pltpu.VMEM((D_MODEL, W_COLS - C_ZW), BF16),
            pltpu.VMEM(tuple(wb_all.shape[1:]), BF16),
            pltpu.VMEM(tuple(wo_all.shape[1:]), BF16),
            pltpu.SemaphoreType.DMA((3,)),
        ]

    return pl.pallas_call(
        functools.partial(_layer_kernel, lat, T, layer),
        grid=(B,),
        in_specs=in_specs,
        out_specs=out_specs,
        out_shape=out_shape,
        scratch_shapes=scratch,
        input_output_aliases=aliases,
        compiler_params=pltpu.CompilerParams(
            dimension_semantics=("arbitrary",),
            vmem_limit_bytes=LATENT_VMEM_LIMIT_BYTES if lat else VMEM_LIMIT_BYTES),
        name=("latent_layer" if lat else "context_layer") + str(layer),
    )(*args)


def _rope_tables(T):
    f32 = np.float32
    rows = T // GRID_W
    row = np.repeat(np.arange(rows, dtype=f32), GRID_W)
    col = np.tile(np.arange(GRID_W, dtype=f32), rows)
    inv = (f32(ROPE_THETA) ** (-np.arange(0, AXIS_DIM, 2, dtype=f32) / f32(AXIS_DIM))).astype(f32)
    ar = row[:, None] * inv
    ac = col[:, None] * inv
    ang = np.concatenate([ar, ar, ac, ac], axis=-1)
    ang = np.concatenate([ang, ang], axis=-1)
    first_half = (np.arange(LANES) % AXIS_DIM) < (AXIS_DIM // 2)
    cos = np.cos(ang).astype(f32)
    sin = np.sin(ang).astype(f32)
    sa = np.where(first_half[None, :], -sin, f32(0.0))
    sb = np.where(first_half[None, :], f32(0.0), sin)
    return jnp.asarray(cos), jnp.asarray(sa), jnp.asarray(sb)


def _constants(sink_logit):
    grp = np.arange(N_HEADS * HEAD_DIM) // HEAD_DIM
    bd = jnp.asarray(grp[:, None] == grp[None, :], dtype=BF16)
    ti = np.arange(CHUNK)
    tril = jnp.asarray(ti[None, :] <= ti[:, None], dtype=BF16)
    triu = jnp.asarray(ti[None, :] >= ti[:, None], dtype=BF16)
    return bd, tril, triu, sink_logit.reshape(-1)


def kernel(x_prompt, x_sample, cache_attn_k, cache_attn_v, cache_win_k, cache_win_v, state_mlstm_C,
           state_mlstm_n, state_mlstm_m, c, c_ctx, w_mod, b_mod, w_in, qk_gain, sink_logit,
           mlstm_gate_bias, w_branch, w_out, ln_gain, ln_bias):
    dec_b = x_sample.shape[0]
    assert dec_b + 1 <= 8
    cond = jnp.concatenate([c_ctx[None, :], c, jnp.zeros((8 - 1 - dec_b, D_MODEL), F32)], axis=0)
    mod = _modulation(cond, w_mod, b_mod).reshape(DEPTH, 8, 3, D_MODEL)

    consts = _constants(sink_logit)
    cos, sa, sb = _rope_tables(x_sample.shape[1])
    lane_pad = jnp.zeros((DEPTH, 1, LANES - N_GATES), F32)
    weights = (
        _prep_in_weight(jnp.swapaxes(w_in, 1, 2)), w_branch.astype(BF16), w_out.astype(BF16),
        jnp.tile(qk_gain[:, 0:1, :], (1, 1, N_HEADS)),
        jnp.tile(qk_gain[:, 1:2, :], (1, 1, N_KV)),
        jnp.concatenate([mlstm_gate_bias.reshape(DEPTH, 1, N_GATES), lane_pad], axis=2),
        ln_gain[:, None, :], ln_bias[:, None, :],
    )

    ctx = None
    cached_t = [jnp.swapaxes(a, -1, -2)
                for a in (cache_attn_k, cache_attn_v, cache_win_k, cache_win_v, state_mlstm_C)]
    xp, xs = x_prompt, x_sample
    small = []
    for l in range(DEPTH):
        outs = _layer_call(False, l, xp, mod, weights, consts, ctx)
        xp = outs[0]
        ctx = tuple(outs[1:6])
        small.append(outs[6:])
        extra = (cos, sa, sb, *cached_t, state_mlstm_n, state_mlstm_m)
        xs = _layer_call(True, l, xs, mod, weights, consts, extra)
    new_n, new_m = [jnp.stack([small[l][i] for l in range(DEPTH)], axis=1) for i in range(2)]
    return (xp, xs, *[jnp.swapaxes(a, -1, -2) for a in ctx], new_n, new_m)
```

```python
import functools

import jax
import jax.numpy as jnp
import numpy as np
from jax import lax
from jax.experimental import pallas as pl
from jax.experimental.pallas import tpu as pltpu

F32 = jnp.float32
BF16 = jnp.bfloat16

D_MODEL = 1024
DEPTH = 2
PAST_LEN = 256
GRID_W = 64
HEAD_DIM = 64
N_HEADS = 8
N_KV = 2
N_GROUP = N_HEADS // N_KV
WINDOW = 128
ML_HEADS = 4
ML_DK = 64
ML_DV = 128
CHUNK = 128
BRANCH_W = 512
ROPE_THETA = 10000.0
AXIS_DIM = HEAD_DIM // 2
LN_EPS = 1e-6
RMS_EPS = 1e-6
ALPHA = (2.0 * DEPTH) ** 0.25
LOG2E = float(np.log2(np.e))
LANES = 128
BF16_ROWS = 16
ROW_BLOCK = 256
MOD_COLS = 1536
CTX_ALIASED = 5
N_SHARED = 14
VMEM_LIMIT_BYTES = 60 * 1024 * 1024
LATENT_VMEM_LIMIT_BYTES = 62 * 1024 * 1024
VT_ROWS = HEAD_DIM + BF16_ROWS
ST_ROWS = ML_DV + BF16_ROWS

_SIZES = (512, 128, 128, 512, 256, 256, 512, 512, 128, 512, 512, 128, 128, 512, 3072)
_OFF = [int(v) for v in np.concatenate([[0], np.cumsum(_SIZES)])]
(C_QA, C_KA, C_VA, C_ZA, C_QM, C_KM, C_VM, C_OM, C_GT, C_ZM, C_QW, C_KW, C_VW, C_ZW, C_GMERGE) = _OFF[:-1]
W_COLS = _OFF[-1]
GATE_COL = 2816
N_GATES = 4 * ML_HEADS


def _dot(a, b):
    return jnp.dot(a, b, preferred_element_type=F32)


def _dot_tb(a, b):
    return lax.dot_general(a, b, (((1,), (1,)), ((), ())), preferred_element_type=F32)


def _split(a):
    hi = a.astype(BF16)
    lo = (a - hi.astype(F32)).astype(BF16)
    return hi, lo


def _layer_norm(x):
    mu = jnp.mean(x, axis=-1, keepdims=True)
    xc = x - mu
    var = jnp.mean(xc * xc, axis=-1, keepdims=True)
    return xc * lax.rsqrt(var + LN_EPS)


def _log_sigmoid(x):
    return jnp.minimum(x, 0.0) - jnp.log(1.0 + jnp.exp(-jnp.abs(x)))


def _sigmoid(x):
    return jax.nn.sigmoid(x)


def _mod_kernel(c_ref, w_ref, b_ref, o_ref):
    c = c_ref[...]
    s = c * _sigmoid(c)
    s_hi, s_lo = _split(s)
    w = w_ref[...]
    w_hi, w_lo = _split(w)
    o_ref[...] = _dot(s_hi, w_hi) + _dot(s_lo, w_hi) + _dot(s_hi, w_lo) + b_ref[...]


def _modulation(cond, w_mod, b_mod):
    rows = cond.shape[0]
    tn = MOD_COLS
    return pl.pallas_call(
        _mod_kernel,
        grid=(DEPTH, 3 * D_MODEL // tn),
        in_specs=[
            pl.BlockSpec((rows, D_MODEL), lambda l, j: (0, 0)),
            pl.BlockSpec((None, D_MODEL, tn), lambda l, j: (l, 0, j)),
            pl.BlockSpec((None, 1, tn), lambda l, j: (l, 0, j)),
        ],
        out_specs=pl.BlockSpec((None, rows, tn), lambda l, j: (l, 0, j)),
        out_shape=jax.ShapeDtypeStruct((DEPTH, rows, 3 * D_MODEL), F32),
        compiler_params=pltpu.CompilerParams(dimension_semantics=("arbitrary", "arbitrary")),
        name="adaln_modulation",
    )(cond, w_mod, b_mod.reshape(DEPTH, 1, 3 * D_MODEL))


PREP_K = 256


def _prep_kernel(wt_ref, o_ref):
    lane = lax.broadcasted_iota(jnp.int32, (PREP_K, LANES), 1)
    for c0 in range(0, W_COLS, LANES):
        r0 = c0 if c0 <= GATE_COL else c0 - (LANES - N_GATES)
        slab = jnp.transpose(wt_ref[r0:r0 + LANES, :])
        if c0 == GATE_COL:
            slab = jnp.where(lane < N_GATES, slab, 0.0)
        o_ref[:, c0:c0 + LANES] = slab.astype(BF16)


def _prep_in_weight(w_in_t):
    n_in = w_in_t.shape[1]
    return pl.pallas_call(
        _prep_kernel,
        grid=(DEPTH, D_MODEL // PREP_K),
        in_specs=[pl.BlockSpec((None, n_in, PREP_K), lambda l, i: (l, 0, i))],
        out_specs=pl.BlockSpec((None, PREP_K, W_COLS), lambda l, i: (l, i, 0)),
        out_shape=jax.ShapeDtypeStruct((DEPTH, D_MODEL, W_COLS), BF16),
        compiler_params=pltpu.CompilerParams(
            dimension_semantics=("arbitrary", "arbitrary"), vmem_limit_bytes=VMEM_LIMIT_BYTES),
        name="in_weight_prep",
    )(w_in_t)


def _rope(x, cos, sa, sb):
    return x * cos + pltpu.roll(x, LANES - AXIS_DIM // 2, 1) * sa + pltpu.roll(x, AXIS_DIM // 2, 1) * sb


def _group_rms(x, bd, gain):
    ms = _dot((x * x).astype(BF16), bd) * (1.0 / HEAD_DIM)
    return x * lax.rsqrt(ms + RMS_EPS) * gain


def _ones_row_tile():
    r = lax.broadcasted_iota(jnp.int32, (BF16_ROWS, LANES), 0)
    return jnp.where(r == 0, 1.0, 0.0).astype(BF16)


def _attend_chains(chains, results, depth):
    state = []
    for qs, tiles, sink_row in chains:
        m_cols = qs.shape[0]
        if sink_row is None:
            m = jnp.full((1, m_cols), -jnp.inf, F32)
            acc = jnp.zeros((VT_ROWS, m_cols), F32)
        else:
            m = sink_row
            r = lax.broadcasted_iota(jnp.int32, (VT_ROWS, m_cols), 0)
            acc = jnp.where(r == HEAD_DIM, 1.0, 0.0)
        state.append([m, acc])
    order = [(t, c) for t in range(max(len(tiles) for _, tiles, _ in chains))
             for c, (_, tiles, _) in enumerate(chains) if t < len(tiles)]

    def issue(t, c):
        qs, tiles, _ = chains[c]
        tile = tiles[t]()
        return tile, _dot_tb(tile[0], qs)

    pending = [issue(*item) for item in order[:depth]]
    yield
    for k, (t, c) in enumerate(order):
        (_, v_t, valid), s = pending.pop(0)
        if k + depth < len(order):
            pending.append(issue(*order[k + depth]))
        if valid is not None:
            s = jnp.where(valid, s, -jnp.inf)
        m, acc = state[c]
        m_new = jnp.maximum(m, jnp.max(s, axis=0, keepdims=True))
        p = jnp.exp2(s - m_new).astype(BF16)
        state[c] = [m_new, acc * jnp.exp2(m - m_new) + _dot(v_t, p)]
        if k + 1 == len(order) or order[k + 1][0] != t:
            yield
    results.extend(acc[0:HEAD_DIM, :] / acc[HEAD_DIM:HEAD_DIM + 1, :] for _, acc in state)


def _interleave(weighted):
    live = [[g, n] for g, n in weighted]
    while live:
        for item in list(live):
            for _ in range(item[1]):
                try:
                    next(item[0])
                except StopIteration:
                    live.remove(item)
                    break
        yield


def _run(gen):
    for _ in gen:
        pass


def _layer_kernel(lat, T, layer, *refs):
    _run(_sequence(lat, T, layer, *refs))


def _sequence(lat, T, layer, *refs):
    refs = list(refs)
    x_ref, mod_ref, w_ref, wb_ref, wo_ref = refs[:5]
    g0_ref, g1_ref, gb_ref, lng_ref, lnb_ref = refs[5:10]
    bd_ref, tril_ref, triu_ref, sink_ref = refs[10:N_SHARED]
    pos = N_SHARED
    if lat:
        cos_ref, sa_ref, sb_ref = refs[pos:pos + 3]
        cka_ref, cva_ref, ckw_ref, cvw_ref, c0_ref, n0_ref, m0_ref = refs[pos + 3:pos + 10]
        pos += 10
        w_hbm = refs[pos]
        y_ref = refs[pos + 1]
        pos += 2
    else:
        if layer > 0:
            pos += CTX_ALIASED
        x_next_ref = refs[pos]
        pos += 1
        y_ref, ka_o, va_o, kw_o, vw_o, c_o, n_o, m_o = refs[pos:pos + 8]
        pos += 8
        if layer == 0:
            for full in (ka_o, va_o, kw_o, vw_o, c_o):
                full[1:] = jnp.zeros((DEPTH - 1,) + full.shape[1:], F32)
            ka_o, va_o, kw_o, vw_o, c_o = (full.at[0] for full in (ka_o, va_o, kw_o, vw_o, c_o))
    (u_s, qa_s, qw_s, ka_s, vat_s, kw_s, vwt_s, qm_s, km_s, vmt_s, om_s, g_s,
     ya_s, yw_s, hmt_s, c_s, m_s) = refs[pos:pos + 17]
    pos += 17
    if not lat:
        step = pl.program_id(0)
        u_next_s = u_s.at[(step + 1) % 2]
        u_s = u_s.at[step % 2]
    wt_s = None
    if lat:
        kwc_s, vwct_s = refs[pos:pos + 2]
        wb_hbm, wo_hbm = wb_ref, wo_ref
        wt_s, wb_ref, wo_ref, w_sem = refs[pos + 2:pos + 6]

        first_step = pl.program_id(0) == 0
        late_copies = (
            pltpu.make_async_copy(w_hbm.at[layer, :, C_ZW:W_COLS], wt_s, w_sem.at[0]),
            pltpu.make_async_copy(wb_hbm.at[layer], wb_ref, w_sem.at[1]),
            pltpu.make_async_copy(wo_hbm.at[layer], wo_ref, w_sem.at[2]),
        )

        @pl.when(first_step)
        def _():
            for cp in late_copies:
                cp.start()

    row_block = min(T, ROW_BLOCK)
    n_rows = T // row_block
    n_chunks = T // CHUNK
    chunks_per_block = row_block // CHUNK
    shift = mod_ref[0:1, :]
    scale = mod_ref[1:2, :]
    gate = mod_ref[2:3, :]
    ones_tile = _ones_row_tile()

    norm_rows = 64

    def row_slice(i, j, n):
        r0 = i * row_block + j
        return pl.ds(r0 if isinstance(r0, int) else pl.multiple_of(r0, n), n)

    def phase1_norm(i, src=x_ref, dst=u_s):
        for j in range(0, row_block, norm_rows):
            piece = row_slice(i, j, norm_rows)
            dst[piece, :] = (_layer_norm(src[piece, :]) * (1.0 + scale) + shift).astype(BF16)
            yield

    def phase1_proj(i):
        R = row_block
        rows = row_slice(i, 0, R)
        u = u_s[rows, :]

        def proj(c0, width):
            return _dot(u, w_ref[:, c0:c0 + width])

        if lat:
            cos = cos_ref[rows, :]
            sa = sa_ref[rows, :]
            sb = sb_ref[rows, :]

        def rope_wide(v):
            if not lat:
                return v
            slabs = [_rope(v[:, j:j + LANES], cos, sa, sb) for j in range(0, v.shape[1], LANES)]
            return slabs[0] if len(slabs) == 1 else jnp.concatenate(slabs, axis=1)

        qscale = LOG2E * HEAD_DIM ** -0.5
        qa = rope_wide(_group_rms(proj(C_QA, 512), bd_ref[...], g0_ref[...])) * qscale
        for h in range(N_HEADS):
            qa_s[h, rows, :] = qa[:, h * HEAD_DIM:(h + 1) * HEAD_DIM].astype(BF16)
        yield
        ka_n = _group_rms(proj(C_KA, LANES), bd_ref[0:LANES, 0:LANES], g1_ref[...])
        ka = rope_wide(ka_n)
        va = proj(C_VA, LANES)
        qw = rope_wide(proj(C_QW, 512)) * qscale
        for h in range(N_HEADS):
            qw_s[h, rows, :] = qw[:, h * HEAD_DIM:(h + 1) * HEAD_DIM].astype(BF16)
        yield
        kw_raw = proj(C_KW, LANES)
        kw = rope_wide(kw_raw)
        vw = proj(C_VW, LANES)
        for kv in range(N_KV):
            sl = slice(kv * HEAD_DIM, (kv + 1) * HEAD_DIM)
            ka_s[kv, rows, :] = ka[:, sl].astype(BF16)
            kw_s[kv, rows, :] = kw[:, sl].astype(BF16)
        yield
        qm_s[rows, :] = proj(C_QM, 256).astype(BF16)
        km_s[rows, :] = (proj(C_KM, 256) * (ML_DK ** -0.5)).astype(BF16)
        yield
        vm = proj(C_VM, 512)
        yield
        om_s[rows, :] = _sigmoid(proj(C_OM, 512)).astype(BF16)
        g_s[rows, :] = proj(C_GT, LANES) + gb_ref[...]
        yield
        for j in range(chunks_per_block):
            cj = i * chunks_per_block + j
            cr = slice(j * CHUNK, (j + 1) * CHUNK)
            va_t = jnp.transpose(va[cr, :])
            vw_t = jnp.transpose(vw[cr, :])
            for kv in range(N_KV):
                hd = slice(kv * HEAD_DIM, (kv + 1) * HEAD_DIM)
                vat_s[cj, kv, 0:HEAD_DIM, :] = va_t[hd, :].astype(BF16)
                vat_s[cj, kv, HEAD_DIM:VT_ROWS, :] = ones_tile
                vwt_s[cj, kv, 0:HEAD_DIM, :] = vw_t[hd, :].astype(BF16)
                vwt_s[cj, kv, HEAD_DIM:VT_ROWS, :] = ones_tile
            if not lat:
                tc = slice(i * row_block + j * CHUNK, i * row_block + (j + 1) * CHUNK)
                ka_t = jnp.transpose(ka_n[cr, :])
                kw_t = jnp.transpose(kw_raw[cr, :])
                for kv in range(N_KV):
                    hd = slice(kv * HEAD_DIM, (kv + 1) * HEAD_DIM)
                    ka_o[kv, :, tc] = ka_t[hd, :]
                    va_o[kv, :, tc] = va_t[hd, :]
                    kw_o[kv, :, tc] = kw_t[hd, :]
                    vw_o[kv, :, tc] = vw_t[hd, :]
            for h in range(ML_HEADS):
                vmt_s[cj, h, 0:ML_DV, :] = jnp.transpose(vm[cr, h * ML_DV:(h + 1) * ML_DV]).astype(BF16)
                vmt_s[cj, h, ML_DV:ST_ROWS, :] = ones_tile
            yield

    if not lat:
        @pl.when(step == 0)
        def _():
            _run(phase1_norm(0))
        yield from phase1_proj(0)
    else:
        def body1(i, carry):
            _run(phase1_norm(i))
            _run(phase1_proj(i))
            return carry
        lax.fori_loop(0, n_rows, body1, 0)

    if lat:
        for j in range(PAST_LEN // CHUNK):
            cols = slice(j * CHUNK, (j + 1) * CHUNK)
            ka_c = jnp.transpose(jnp.concatenate([cka_ref[kv, :, cols] for kv in range(N_KV)], axis=0))
            kw_c = jnp.transpose(jnp.concatenate([ckw_ref[kv, :, cols] for kv in range(N_KV)], axis=0))
            for kv in range(N_KV):
                hd = slice(kv * HEAD_DIM, (kv + 1) * HEAD_DIM)
                ka_s[kv, T + j * CHUNK:T + (j + 1) * CHUNK, :] = ka_c[:, hd].astype(BF16)
                kwc_s[kv, cols, :] = kw_c[:, hd].astype(BF16)
                vat_s[n_chunks + j, kv, 0:HEAD_DIM, :] = cva_ref[kv, :, cols].astype(BF16)
                vat_s[n_chunks + j, kv, HEAD_DIM:VT_ROWS, :] = ones_tile
                vwct_s[kv, 0:HEAD_DIM, cols] = cvw_ref[kv, :, cols].astype(BF16)
                vwct_s[kv, HEAD_DIM:VT_ROWS, cols] = ones_tile

    tq = CHUNK
    scores_in_flight = 2 if lat else 2 * N_KV
    m_cols = N_GROUP * tq

    def sink_row(kv):
        col = lax.broadcasted_iota(jnp.int32, (1, m_cols), 1)
        row = jnp.full((1, m_cols), sink_ref[layer * N_HEADS + kv * N_GROUP], F32)
        for g in range(1, N_GROUP):
            row = jnp.where(col >= g * tq, sink_ref[layer * N_HEADS + kv * N_GROUP + g], row)
        return row * LOG2E

    def load_q(q_s, kv, q0):
        return q_s[kv * N_GROUP:(kv + 1) * N_GROUP, pl.ds(q0, tq), :].reshape(m_cols, HEAD_DIM)

    def store_heads(dst, q0, kv, o_t):
        for p in range(N_GROUP // 2):
            blk = jnp.concatenate([o_t[:, (2 * p) * tq:(2 * p + 1) * tq],
                                   o_t[:, (2 * p + 1) * tq:(2 * p + 2) * tq]], axis=0)
            c0 = (kv * N_GROUP + 2 * p) * HEAD_DIM
            dst[pl.ds(q0, tq), c0:c0 + LANES] = jnp.transpose(blk).astype(dst.dtype)

    def key_tiles(k_ref, vt_ref, kv, n_key_chunks, chunks_per_tile):
        def tile(c0):
            def load():
                k = k_ref[kv, c0 * CHUNK:(c0 + chunks_per_tile) * CHUNK, :]
                v_t = [vt_ref[c0 + j, kv] for j in range(chunks_per_tile)]
                return k, (v_t[0] if len(v_t) == 1 else jnp.concatenate(v_t, axis=1)), None
            return load
        return [tile(c0) for c0 in range(0, n_key_chunks, chunks_per_tile)]

    def band_tiles(kv, n, q0):
        j0 = jnp.clip(n - 1, 0, n_chunks - 3)
        w0 = pl.multiple_of(j0 * CHUNK, CHUNK)

        def band(c0, nc):
            def load():
                rows = nc * CHUNK
                kpos = w0 + c0 * CHUNK + lax.broadcasted_iota(jnp.int32, (rows, m_cols), 0)
                qpos = q0 + (lax.broadcasted_iota(jnp.int32, (rows, m_cols), 1) & (tq - 1))
                v_t = [vwt_s[j0 + c0 + j, kv] for j in range(nc)]
                return (kw_s[kv, pl.ds(w0 + c0 * CHUNK, rows), :],
                        v_t[0] if nc == 1 else jnp.concatenate(v_t, axis=1),
                        jnp.abs(kpos - qpos) <= WINDOW)
            return load
        return [band(0, 2), band(2, 1), lambda: (kwc_s[kv], vwct_s[kv], None)]

    def attend(n, q0, chunks_per_tile):
        chains, dsts = [], []
        n_key_chunks = ka_s.shape[1] // CHUNK
        for kv in range(N_KV):
            chains.append((load_q(qa_s, kv, q0), key_tiles(ka_s, vat_s, kv, n_key_chunks, chunks_per_tile), None))
            dsts.append((ya_s, kv))
        for kv in range(N_KV):
            tiles = band_tiles(kv, n, q0) if lat else key_tiles(kw_s, vwt_s, kv, n_chunks, chunks_per_tile)
            chains.append((load_q(qw_s, kv, q0), tiles, sink_row(kv)))
            dsts.append((yw_s, kv))
        results = []
        yield from _attend_chains(chains, results, scores_in_flight)
        for (dst, kv), o_t in zip(dsts, results):
            store_heads(dst, q0, kv, o_t)

    n_pairs = ML_HEADS // 2
    hmt_s[...] = jnp.zeros_like(hmt_s)
    for d in range(2):
        if lat:
            for pr in range(n_pairs):
                c_s[d * n_pairs + pr, 0:ML_DV, :] = jnp.transpose(
                    jnp.concatenate([c0_ref[d, 2 * pr], c0_ref[d, 2 * pr + 1]], axis=0))
        for h in range(ML_HEADS):
            idx = d * n_pairs + h // 2
            hl = slice((h % 2) * ML_DK, (h % 2 + 1) * ML_DK)
            c_s[idx, ML_DV:ST_ROWS, hl] = jnp.zeros((BF16_ROWS, ML_DK), F32)
            if lat:
                c_s[idx, ML_DV:ML_DV + 1, hl] = n0_ref[d, h:h + 1, :]
                m_s[d * ML_HEADS + h:d * ML_HEADS + h + 1, :] = jnp.broadcast_to(
                    m0_ref[d:d + 1, h:h + 1], (1, LANES))
            else:
                c_s[idx, 0:ML_DV, hl] = jnp.zeros((ML_DV, ML_DK), F32)
                m_s[d * ML_HEADS + h:d * ML_HEADS + h + 1, :] = jnp.zeros((1, LANES), F32)

    L = CHUNK
    s_idx = lax.broadcasted_iota(jnp.int32, (L, ML_HEADS * L), 0)
    t_idx = lax.broadcasted_iota(jnp.int32, (L, ML_HEADS * L), 1) & (L - 1)
    lane_row = lax.broadcasted_iota(jnp.int32, (1, LANES), 1)
    low_half = lax.broadcasted_iota(jnp.int32, (L, LANES), 1) < ML_DK
    ones_ll = jnp.ones((L, L), BF16)
    zeros_ll = jnp.zeros((L, L), BF16)

    def heads_row(src, r0, c0=0):
        return jnp.concatenate([src[r0 + h:r0 + h + 1, c0:c0 + L] for h in range(ML_HEADS)], axis=1)

    def block_diag_rows(x):
        zero = jnp.zeros_like(x)
        return jnp.concatenate([jnp.where(low_half, x, zero), jnp.where(low_half, zero, x)], axis=0)

    def mlstm_streams(streams):
        m_state = {d: heads_row(m_s, d * ML_HEADS) for d in sorted({d for d, _ in streams})}
        c_state = {d: [c_s[d * n_pairs + pr] for pr in range(n_pairs)] for d in m_state}
        work = []
        for d, cc in streams:
            r0 = cc * L if isinstance(cc, int) else pl.multiple_of(cc * L, L)
            rows = pl.ds(r0, L)
            g = g_s[rows, :]
            tri_r = triu_ref[...] if d == 0 else tril_ref[...]
            gi0 = 2 * ML_HEADS * d
            g_t = jnp.transpose(g)[0:N_GATES, :]
            ft_hi, ft_lo = _split(_log_sigmoid(g_t))
            tr = jnp.concatenate([tri_r, ones_ll], axis=1)
            ct = _dot(ft_hi, tr) + _dot(ft_lo, tr)
            r_rows = g_t[gi0:gi0 + 8, :] - pltpu.roll(ct[gi0:gi0 + 8, 0:L], ML_HEADS, 0)
            r = jnp.transpose(jnp.concatenate([r_rows, jnp.zeros((L - 8, L), F32)], axis=0))
            rb = jnp.concatenate([jnp.broadcast_to(r[:, h:h + 1], (L, L))
                                  for h in range(ML_HEADS)], axis=1)
            work.append(dict(d=d, cc=cc, rows=rows, rb=rb, g_t=g_t, ct=ct))
            yield
        for w in work:
            d = w["d"]
            valid = (s_idx <= t_idx) if d == 0 else (s_idx >= t_idx)
            gi0 = 2 * ML_HEADS * d
            gf0 = gi0 + ML_HEADS
            b_row = heads_row(w["ct"], gf0)
            b_last = heads_row(w["ct"], gf0, L)
            i_row = heads_row(w["g_t"], gi0)
            m_prev = m_state[d]
            a_row = b_row + m_prev
            dm = jnp.where(valid, b_row + w["rb"], -jnp.inf)
            mt = jnp.maximum(a_row, jnp.max(dm, axis=0, keepdims=True))
            w["p"] = jnp.exp(dm - mt)
            w["w_inter"] = jnp.exp(a_row - mt)
            w["floor"] = jnp.exp(-mt)
            g_row = b_last - b_row + i_row
            g_max = jnp.concatenate(
                [jnp.broadcast_to(jnp.max(g_row[:, h * L:(h + 1) * L], axis=-1, keepdims=True), (1, L))
                 for h in range(ML_HEADS)], axis=1)
            m_new = jnp.maximum(b_last + m_prev, g_max)
            w["ws"] = jnp.exp(g_row - m_new)
            w["wc"] = jnp.exp(b_last + m_prev - m_new)
            m_state[d] = m_new
            yield
        for w in work:
            rows, cc = w["rows"], w["cc"]
            w["pairs"] = []
            for pr in range(n_pairs):
                lanes = slice(pr * LANES, (pr + 1) * LANES)
                cols = slice(pr * 2 * L, (pr + 1) * 2 * L)
                k_pair = km_s[rows, lanes]
                q_bd = block_diag_rows(qm_s[rows, lanes])
                k_bd = block_diag_rows(k_pair)
                s_t = (_dot_tb(k_pair, q_bd) * w["p"][:, cols]).astype(BF16)
                s_bd = jnp.concatenate(
                    [jnp.concatenate([s_t[:, 0:L], zeros_ll], axis=1),
                     jnp.concatenate([zeros_ll, s_t[:, L:2 * L]], axis=1)], axis=0)
                v_t = jnp.concatenate([vmt_s[cc, 2 * pr], vmt_s[cc, 2 * pr + 1]], axis=1)
                intra = _dot(v_t, s_bd)
                wv = (v_t.astype(F32) * w["ws"][:, cols]).astype(BF16)
                w["pairs"].append((q_bd, intra, _dot(wv, k_bd)))
            yield
        for w in work:
            d, cc = w["d"], w["cc"]
            for pr, (q_bd, intra, update) in enumerate(w["pairs"]):
                cols = slice(pr * 2 * L, (pr + 1) * 2 * L)
                state = c_state[d][pr]
                tot = intra + w["w_inter"][:, cols] * _dot_tb(state.astype(BF16), q_bd)
                h_t = tot[0:ML_DV, :] / jnp.maximum(jnp.abs(tot[ML_DV:ML_DV + 1, :]), w["floor"][:, cols])
                for e in range(2):
                    hr = slice((2 * pr + e) * ML_DV, (2 * pr + e + 1) * ML_DV)
                    hmt_s[cc, hr, :] += h_t[:, e * L:(e + 1) * L]
                wc = w["wc"]
                wc_pair = jnp.where(lane_row < ML_DK, wc[:, 2 * pr * L:(2 * pr + 1) * L],
                                    wc[:, (2 * pr + 1) * L:(2 * pr + 2) * L])
                c_state[d][pr] = wc_pair * state + update
            yield
        for d in m_state:
            for h in range(ML_HEADS):
                m_s[d * ML_HEADS + h:d * ML_HEADS + h + 1, :] = m_state[d][:, h * L:(h + 1) * L]
            for pr in range(n_pairs):
                c_s[d * n_pairs + pr] = c_state[d][pr]

    def scan_streams(cs):
        return [(d, c if d == 0 else n_chunks - 1 - c) for c in cs for d in range(2)]

    if lat:
        def body2(n, carry):
            _run(_interleave([(attend(n, pl.multiple_of(n * tq, tq), 2), 1),
                              (mlstm_streams(scan_streams([n])), 1)]))
            return carry
        lax.fori_loop(0, n_chunks, body2, 0)
    else:
        def all_tiles():
            for n in range(T // tq):
                yield from attend(n, n * tq, 2)
        yield from _interleave([(all_tiles(), 1), (mlstm_streams(scan_streams(range(n_chunks))), 3)])

    if not lat:
        for d in range(2):
            for pr in range(n_pairs):
                c_t = jnp.transpose(c_s[d * n_pairs + pr, 0:ML_DV, :])
                for e in range(2):
                    c_o[d, 2 * pr + e] = c_t[e * ML_DK:(e + 1) * ML_DK, :]
            for h in range(ML_HEADS):
                idx = d * n_pairs + h // 2
                hl = slice((h % 2) * ML_DK, (h % 2 + 1) * ML_DK)
                n_o[d, h:h + 1, :] = c_s[idx, ML_DV:ML_DV + 1, hl]
                m_o[d:d + 1, h:h + 1] = m_s[d * ML_HEADS + h:d * ML_HEADS + h + 1, 0:1]

    def phase3_merge(i, out):
        rows = row_slice(i, 0, row_block)
        u = u_s[rows, :]

        def proj(c0, width):
            if wt_s is not None and c0 >= C_ZW:
                return _dot(u, wt_s[:, c0 - C_ZW:c0 - C_ZW + width])
            return _dot(u, w_ref[:, c0:c0 + width])

        hm = jnp.concatenate(
            [jnp.concatenate([jnp.transpose(hmt_s[i * chunks_per_block + j, h * ML_DV:(h + 1) * ML_DV, :])
                              for h in range(ML_HEADS)], axis=1)
             for j in range(chunks_per_block)], axis=0)
        ys = (ya_s[rows, :].astype(F32), om_s[rows, :].astype(F32) * hm, yw_s[rows, :].astype(F32))
        yield
        merged = None
        for b, (yb, zc) in enumerate(zip(ys, (C_ZA, C_ZM, C_ZW))):
            z = proj(zc, BRANCH_W)
            t = (yb * (z * _sigmoid(z))).astype(BF16)
            yield
            pb = _sigmoid(proj(C_GMERGE + b * D_MODEL, D_MODEL)) * _dot(t, wb_ref[b])
            merged = pb if merged is None else merged + pb
            yield
        out.append(_dot(merged.astype(BF16), wo_ref[...]))
        yield

    def phase3_norm(i, o):
        for j in range(0, row_block, norm_rows):
            piece = row_slice(i, j, norm_rows)
            hres = ALPHA * x_ref[piece, :] + gate * o[j:j + norm_rows, :]
            y_ref[piece, :] = _layer_norm(hres) * lng_ref[...] + lnb_ref[...]
            yield

    if lat:
        @pl.when(first_step)
        def _():
            for cp in late_copies:
                cp.wait()

    def phase3(i):
        box = []
        yield from phase3_merge(i, box)
        yield from phase3_norm(i, box[0])

    if not lat:
        box = []
        yield from _interleave([(phase3_merge(0, box), 1), (phase1_norm(0, x_next_ref, u_next_s), 1)])
        yield from phase3_norm(0, box[0])
    else:
        def body4(i, carry):
            _run(phase3(i))
            return carry
        lax.fori_loop(0, n_rows, body4, 0)


def _const_spec(shape):
    nd = len(shape)
    return pl.BlockSpec(shape, lambda b: (0,) * nd, pipeline_mode=pl.Buffered(1))


def _layer_spec(shape, layer):
    nd = len(shape) - 1
    return pl.BlockSpec((None,) + tuple(shape[1:]), lambda b: (layer,) + (0,) * nd,
                        pipeline_mode=pl.Buffered(1))


def _layer_call(lat, layer, x, mod, weights, consts, extra):
    B, T, _ = x.shape
    S = T + PAST_LEN if lat else T
    n_chunks = T // CHUNK

    seq_spec = pl.BlockSpec((None, T, D_MODEL), lambda b: (b, 0, 0))
    out_seq_spec = seq_spec
    if lat:
        mod_spec = pl.BlockSpec((None, None, 3, D_MODEL), lambda b: (layer, b + 1, 0, 0))
    else:
        mod_spec = pl.BlockSpec((None, None, 3, D_MODEL), lambda b: (layer, 0, 0, 0))
    *vmem_consts, sink = consts
    in_specs = [seq_spec, mod_spec]
    w_all, wb_all, wo_all = weights[:3]
    if lat:
        in_specs.append(
            pl.BlockSpec((None, D_MODEL, C_ZW), lambda b: (layer, 0, 0), pipeline_mode=pl.Buffered(1)))
        in_specs += [pl.BlockSpec(memory_space=pl.ANY)] * 2
        in_specs += [_layer_spec(a.shape, layer) for a in weights[3:]]
    else:
        in_specs += [_layer_spec(a.shape, layer) for a in weights]
    in_specs += [_const_spec(a.shape) for a in vmem_consts]
    in_specs.append(pl.BlockSpec(memory_space=pltpu.SMEM))
    args = [x, mod, *weights, *vmem_consts, sink]

    if lat:
        cos, sa, sb, cka, cva, ckw, cvw, sc, sn, sm = extra
        in_specs += [_const_spec(cos.shape)] * 3
        kv_spec = pl.BlockSpec((None, None, N_KV, HEAD_DIM, PAST_LEN), lambda b: (b, layer, 0, 0, 0))
        in_specs += [kv_spec] * 4
        in_specs += [
            pl.BlockSpec((None, None, 2, ML_HEADS, ML_DK, ML_DV), lambda b: (b, layer, 0, 0, 0, 0)),
            pl.BlockSpec((None, None, 2, ML_HEADS, ML_DK), lambda b: (b, layer, 0, 0, 0)),
            pl.BlockSpec((None, None, 2, ML_HEADS), lambda b: (b, layer, 0, 0)),
        ]
        args += [cos, sa, sb, cka, cva, ckw, cvw, sc, sn, sm]
        out_specs = out_seq_spec
        out_shape = jax.ShapeDtypeStruct((B, T, D_MODEL), F32)
        aliases = {}
    else:
        kv_shape = (B, DEPTH, N_KV, HEAD_DIM, T)
        c_shape = (B, DEPTH, 2, ML_HEADS, ML_DK, ML_DV)
        if layer == 0:
            aliases = {}
            ld = DEPTH
            layer_idx = 0
        else:
            first_acc = len(args)
            in_specs += [pl.BlockSpec(memory_space=pl.ANY)] * len(extra)
            args += list(extra)
            aliases = {first_acc + i: 1 + i for i in range(len(extra))}
            assert len(extra) == CTX_ALIASED
            ld = None
            layer_idx = layer
        in_specs.append(pl.BlockSpec((None, T, D_MODEL), lambda b: (jnp.minimum(b + 1, B - 1), 0, 0)))
        args.append(x)
        kv_out = pl.BlockSpec((None, ld, N_KV, HEAD_DIM, T), lambda b: (b, layer_idx, 0, 0, 0))
        out_specs = [
            seq_spec, kv_out, kv_out, kv_out, kv_out,
            pl.BlockSpec((None, ld, 2, ML_HEADS, ML_DK, ML_DV), lambda b: (b, layer_idx, 0, 0, 0, 0)),
            pl.BlockSpec((None, 2, ML_HEADS, ML_DK), lambda b: (b, 0, 0, 0)),
            pl.BlockSpec((None, 2, ML_HEADS), lambda b: (b, 0, 0)),
        ]
        out_shape = [jax.ShapeDtypeStruct((B, T, D_MODEL), F32)]
        out_shape += [jax.ShapeDtypeStruct(kv_shape, F32)] * 4 + [jax.ShapeDtypeStruct(c_shape, F32)]
        out_shape += [
            jax.ShapeDtypeStruct((B, 2, ML_HEADS, ML_DK), F32),
            jax.ShapeDtypeStruct((B, 2, ML_HEADS), F32),
        ]

    scratch = [
        ((T, D_MODEL) if lat else (2, T, D_MODEL), BF16),
        ((N_HEADS, T, HEAD_DIM), BF16),
        ((N_HEADS, T, HEAD_DIM), BF16),
        ((N_KV, S, HEAD_DIM), BF16),
        ((S // CHUNK, N_KV, VT_ROWS, CHUNK), BF16),
        ((N_KV, T, HEAD_DIM), BF16),
        ((n_chunks, N_KV, VT_ROWS, CHUNK), BF16),
        ((T, ML_HEADS * ML_DK), BF16),
        ((T, ML_HEADS * ML_DK), BF16),
        ((n_chunks, ML_HEADS, ST_ROWS, CHUNK), BF16),
        ((T, ML_HEADS * ML_DV), BF16),
        ((T, LANES), F32),
        ((T, BRANCH_W), BF16),
        ((T, BRANCH_W), BF16),
        ((n_chunks, ML_HEADS * ML_DV, CHUNK), F32),
        ((ML_HEADS, ST_ROWS, 2 * ML_DK), F32),
        ((2 * ML_HEADS, LANES), F32),
    ]
    if lat:
        scratch += [((N_KV, PAST_LEN, HEAD_DIM), BF16),
                    ((N_KV, VT_ROWS, PAST_LEN), BF16)]
    scratch = [pltpu.VMEM(shape, dtype) for shape, dtype in scratch]
    if lat:
        in_specs.append(pl.BlockSpec(memory_space=pl.ANY))
        args.append(w_all)
        scratch += [
            pltpu.VMEM((D_MODEL, W_COLS - C_ZW), BF16),
            pltpu.VMEM(tuple(wb_all.shape[1:]), BF16),
            pltpu.VMEM(tuple(wo_all.shape[1:]), BF16),
            pltpu.SemaphoreType.DMA((3,)),
        ]

    return pl.pallas_call(
        functools.partial(_layer_kernel, lat, T, layer),
        grid=(B,),
        in_specs=in_specs,
        out_specs=out_specs,
        out_shape=out_shape,
        scratch_shapes=scratch,
        input_output_aliases=aliases,
        compiler_params=pltpu.CompilerParams(
            dimension_semantics=("arbitrary",),
            vmem_limit_bytes=LATENT_VMEM_LIMIT_BYTES if lat else VMEM_LIMIT_BYTES),
        name=("latent_layer" if lat else "context_layer") + str(layer),
    )(*args)


def _rope_tables(T):
    f32 = np.float32
    rows = T // GRID_W
    row = np.repeat(np.arange(rows, dtype=f32), GRID_W)
    col = np.tile(np.arange(GRID_W, dtype=f32), rows)
    inv = (f32(ROPE_THETA) ** (-np.arange(0, AXIS_DIM, 2, dtype=f32) / f32(AXIS_DIM))).astype(f32)
    ar = row[:, None] * inv
    ac = col[:, None] * inv
    ang = np.concatenate([ar, ar, ac, ac], axis=-1)
    ang = np.concatenate([ang, ang], axis=-1)
    first_half = (np.arange(LANES) % AXIS_DIM) < (AXIS_DIM // 2)
    cos = np.cos(ang).astype(f32)
    sin = np.sin(ang).astype(f32)
    sa = np.where(first_half[None, :], -sin, f32(0.0))
    sb = np.where(first_half[None, :], f32(0.0), sin)
    return jnp.asarray(cos), jnp.asarray(sa), jnp.asarray(sb)


def _constants(sink_logit):
    grp = np.arange(N_HEADS * HEAD_DIM) // HEAD_DIM
    bd = jnp.asarray(grp[:, None] == grp[None, :], dtype=BF16)
    ti = np.arange(CHUNK)
    tril = jnp.asarray(ti[None, :] <= ti[:, None], dtype=BF16)
    triu = jnp.asarray(ti[None, :] >= ti[:, None], dtype=BF16)
    return bd, tril, triu, sink_logit.reshape(-1)


def kernel(x_prompt, x_sample, cache_attn_k, cache_attn_v, cache_win_k, cache_win_v, state_mlstm_C,
           state_mlstm_n, state_mlstm_m, c, c_ctx, w_mod, b_mod, w_in, qk_gain, sink_logit,
           mlstm_gate_bias, w_branch, w_out, ln_gain, ln_bias):
    dec_b = x_sample.shape[0]
    assert dec_b + 1 <= 8
    cond = jnp.concatenate([c_ctx[None, :], c, jnp.zeros((8 - 1 - dec_b, D_MODEL), F32)], axis=0)
    mod = _modulation(cond, w_mod, b_mod).reshape(DEPTH, 8, 3, D_MODEL)

    consts = _constants(sink_logit)
    cos, sa, sb = _rope_tables(x_sample.shape[1])
    lane_pad = jnp.zeros((DEPTH, 1, LANES - N_GATES), F32)
    weights = (
        _prep_in_weight(jnp.swapaxes(w_in, 1, 2)), w_branch.astype(BF16), w_out.astype(BF16),
        jnp.tile(qk_gain[:, 0:1, :], (1, 1, N_HEADS)),
        jnp.tile(qk_gain[:, 1:2, :], (1, 1, N_KV)),
        jnp.concatenate([mlstm_gate_bias.reshape(DEPTH, 1, N_GATES), lane_pad], axis=2),
        ln_gain[:, None, :], ln_bias[:, None, :],
    )

    ctx = None
    cached_t = [jnp.swapaxes(a, -1, -2)
                for a in (cache_attn_k, cache_attn_v, cache_win_k, cache_win_v, state_mlstm_C)]
    xp, xs = x_prompt, x_sample
    small = []
    for l in range(DEPTH):
        outs = _layer_call(False, l, xp, mod, weights, consts, ctx)
        xp = outs[0]
        ctx = tuple(outs[1:6])
        small.append(outs[6:])
        extra = (cos, sa, sb, *cached_t, state_mlstm_n, state_mlstm_m)
        xs = _layer_call(True, l, xs, mod, weights, consts, extra)
    new_n, new_m = [jnp.stack([small[l][i] for l in range(DEPTH)], axis=1) for i in range(2)]
    return (xp, xs, *[jnp.swapaxes(a, -1, -2) for a in ctx], new_n, new_m)
```

```python
import functools

import jax
import jax.numpy as jnp
import numpy as np
from jax import lax
from jax.experimental import pallas as pl
from jax.experimental.pallas import tpu as pltpu

F32 = jnp.float32
BF16 = jnp.bfloat16

D_MODEL = 1024
DEPTH = 2
PAST_LEN = 256
GRID_W = 64
HEAD_DIM = 64
N_HEADS = 8
N_KV = 2
N_GROUP = N_HEADS // N_KV
WINDOW = 128
ML_HEADS = 4
ML_DK = 64
ML_DV = 128
CHUNK = 128
BRANCH_W = 512
ROPE_THETA = 10000.0
AXIS_DIM = HEAD_DIM // 2
LN_EPS = 1e-6
RMS_EPS = 1e-6
ALPHA = (2.0 * DEPTH) ** 0.25
LOG2E = float(np.log2(np.e))
LANES = 128
BF16_ROWS = 16
ROW_BLOCK = 256
MOD_COLS = 1536
CTX_ALIASED = 5
N_SHARED = 14
VMEM_LIMIT_BYTES = 60 * 1024 * 1024
LATENT_VMEM_LIMIT_BYTES = 62 * 1024 * 1024
VT_ROWS = HEAD_DIM + BF16_ROWS
ST_ROWS = ML_DV + BF16_ROWS

_SIZES = (512, 128, 128, 512, 256, 256, 512, 512, 128, 512, 512, 128, 128, 512, 3072)
_OFF = [int(v) for v in np.concatenate([[0], np.cumsum(_SIZES)])]
(C_QA, C_KA, C_VA, C_ZA, C_QM, C_KM, C_VM, C_OM, C_GT, C_ZM, C_QW, C_KW, C_VW, C_ZW, C_GMERGE) = _OFF[:-1]
W_COLS = _OFF[-1]
GATE_COL = 2816
N_GATES = 4 * ML_HEADS


def _dot(a, b):
    return jnp.dot(a, b, preferred_element_type=F32)


def _dot_tb(a, b):
    return lax.dot_general(a, b, (((1,), (1,)), ((), ())), preferred_element_type=F32)


def _split(a):
    hi = a.astype(BF16)
    lo = (a - hi.astype(F32)).astype(BF16)
    return hi, lo


def _layer_norm(x):
    mu = jnp.mean(x, axis=-1, keepdims=True)
    xc = x - mu
    var = jnp.mean(xc * xc, axis=-1, keepdims=True)
    return xc * lax.rsqrt(var + LN_EPS)


def _log_sigmoid(x):
    return jnp.minimum(x, 0.0) - jnp.log(1.0 + jnp.exp(-jnp.abs(x)))


def _sigmoid(x):
    return jax.nn.sigmoid(x)


def _mod_kernel(c_ref, w_ref, b_ref, o_ref):
    c = c_ref[...]
    s = c * _sigmoid(c)
    s_hi, s_lo = _split(s)
    w = w_ref[...]
    w_hi, w_lo = _split(w)
    o_ref[...] = _dot(s_hi, w_hi) + _dot(s_lo, w_hi) + _dot(s_hi, w_lo) + b_ref[...]


def _modulation(cond, w_mod, b_mod):
    rows = cond.shape[0]
    tn = MOD_COLS
    return pl.pallas_call(
        _mod_kernel,
        grid=(DEPTH, 3 * D_MODEL // tn),
        in_specs=[
            pl.BlockSpec((rows, D_MODEL), lambda l, j: (0, 0)),
            pl.BlockSpec((None, D_MODEL, tn), lambda l, j: (l, 0, j)),
            pl.BlockSpec((None, 1, tn), lambda l, j: (l, 0, j)),
        ],
        out_specs=pl.BlockSpec((None, rows, tn), lambda l, j: (l, 0, j)),
        out_shape=jax.ShapeDtypeStruct((DEPTH, rows, 3 * D_MODEL), F32),
        compiler_params=pltpu.CompilerParams(dimension_semantics=("arbitrary", "arbitrary")),
        name="adaln_modulation",
    )(cond, w_mod, b_mod.reshape(DEPTH, 1, 3 * D_MODEL))


PREP_K = 256


def _prep_kernel(wt_ref, wb_ref, wo_ref, o_ref, ob_ref, oo_ref):
    ob_ref[...] = wb_ref[...].astype(BF16)
    oo_ref[...] = wo_ref[...].astype(BF16)
    lane = lax.broadcasted_iota(jnp.int32, (PREP_K, LANES), 1)
    for c0 in range(0, W_COLS, LANES):
        r0 = c0 if c0 <= GATE_COL else c0 - (LANES - N_GATES)
        slab = jnp.transpose(wt_ref[r0:r0 + LANES, :])
        if c0 == GATE_COL:
            slab = jnp.where(lane < N_GATES, slab, 0.0)
        o_ref[:, c0:c0 + LANES] = slab.astype(BF16)


def _prep_weights(w_in_t, w_branch, w_out):
    n_in = w_in_t.shape[1]
    steps = D_MODEL // PREP_K
    _, n_br, br_w, _ = w_branch.shape
    br_rows, out_rows = br_w // steps, w_out.shape[1] // steps
    br_spec = pl.BlockSpec((None, n_br, br_rows, D_MODEL), lambda l, i: (l, 0, i, 0))
    out_spec = pl.BlockSpec((None, out_rows, D_MODEL), lambda l, i: (l, i, 0))
    return pl.pallas_call(
        _prep_kernel,
        grid=(DEPTH, steps),
        in_specs=[pl.BlockSpec((None, n_in, PREP_K), lambda l, i: (l, 0, i)), br_spec, out_spec],
        out_specs=[pl.BlockSpec((None, PREP_K, W_COLS), lambda l, i: (l, i, 0)), br_spec, out_spec],
        out_shape=[jax.ShapeDtypeStruct((DEPTH, D_MODEL, W_COLS), BF16),
                   jax.ShapeDtypeStruct(w_branch.shape, BF16), jax.ShapeDtypeStruct(w_out.shape, BF16)],
        compiler_params=pltpu.CompilerParams(
            dimension_semantics=("arbitrary", "arbitrary"), vmem_limit_bytes=VMEM_LIMIT_BYTES),
        name="in_weight_prep",
    )(w_in_t, w_branch, w_out)


def _rope(x, cos, sa, sb):
    return x * cos + pltpu.roll(x, LANES - AXIS_DIM // 2, 1) * sa + pltpu.roll(x, AXIS_DIM // 2, 1) * sb


def _group_rms(x, bd, gain):
    ms = _dot((x * x).astype(BF16), bd) * (1.0 / HEAD_DIM)
    return x * lax.rsqrt(ms + RMS_EPS) * gain


def _ones_row_tile():
    r = lax.broadcasted_iota(jnp.int32, (BF16_ROWS, LANES), 0)
    return jnp.where(r == 0, 1.0, 0.0).astype(BF16)


def _attend_chains(chains, results, depth):
    state = []
    for qs, tiles, sink_row in chains:
        m_cols = qs.shape[0]
        if sink_row is None:
            m = jnp.full((1, m_cols), -jnp.inf, F32)
            acc = jnp.zeros((VT_ROWS, m_cols), F32)
        else:
            m = sink_row
            r = lax.broadcasted_iota(jnp.int32, (VT_ROWS, m_cols), 0)
            acc = jnp.where(r == HEAD_DIM, 1.0, 0.0)
        state.append([m, acc])
    order = [(t, c) for t in range(max(len(tiles) for _, tiles, _ in chains))
             for c, (_, tiles, _) in enumerate(chains) if t < len(tiles)]

    def issue(t, c):
        qs, tiles, _ = chains[c]
        tile = tiles[t]()
        return tile, _dot_tb(tile[0], qs)

    pending = [issue(*item) for item in order[:depth]]
    yield
    for k, (t, c) in enumerate(order):
        (_, v_t, valid), s = pending.pop(0)
        if k + depth < len(order):
            pending.append(issue(*order[k + depth]))
        if valid is not None:
            s = jnp.where(valid, s, -jnp.inf)
        m, acc = state[c]
        m_new = jnp.maximum(m, jnp.max(s, axis=0, keepdims=True))
        p = jnp.exp2(s - m_new).astype(BF16)
        state[c] = [m_new, acc * jnp.exp2(m - m_new) + _dot(v_t, p)]
        if k + 1 == len(order) or order[k + 1][0] != t:
            yield
    results.extend(acc[0:HEAD_DIM, :] / acc[HEAD_DIM:HEAD_DIM + 1, :] for _, acc in state)


def _interleave(weighted):
    live = [[g, n] for g, n in weighted]
    while live:
        for item in list(live):
            for _ in range(item[1]):
                try:
                    next(item[0])
                except StopIteration:
                    live.remove(item)
                    break
        yield


def _run(gen):
    for _ in gen:
        pass


def _layer_kernel(lat, T, layer, *refs):
    _run(_sequence(lat, T, layer, *refs))


def _sequence(lat, T, layer, *refs):
    refs = list(refs)
    x_ref, mod_ref, w_ref, wb_ref, wo_ref = refs[:5]
    g0_ref, g1_ref, gb_ref, lng_ref, lnb_ref = refs[5:10]
    bd_ref, tril_ref, triu_ref, sink_ref = refs[10:N_SHARED]
    pos = N_SHARED
    if lat:
        cos_ref, sa_ref, sb_ref = refs[pos:pos + 3]
        cka_ref, cva_ref, ckw_ref, cvw_ref, c0_ref, n0_ref, m0_ref = refs[pos + 3:pos + 10]
        pos += 10
        w_hbm = refs[pos]
        y_ref = refs[pos + 1]
        pos += 2
    else:
        if layer > 0:
            pos += CTX_ALIASED
        x_next_ref = refs[pos]
        pos += 1
        y_ref, ka_o, va_o, kw_o, vw_o, c_o, n_o, m_o = refs[pos:pos + 8]
        pos += 8
        if layer == 0:
            for full in (ka_o, va_o, kw_o, vw_o, c_o):
                full[1:] = jnp.zeros((DEPTH - 1,) + full.shape[1:], F32)
            ka_o, va_o, kw_o, vw_o, c_o = (full.at[0] for full in (ka_o, va_o, kw_o, vw_o, c_o))
    (u_s, qa_s, qw_s, ka_s, vat_s, kw_s, vwt_s, qm_s, km_s, vmt_s, om_s, g_s,
     ya_s, yw_s, hmt_s, c_s, m_s) = refs[pos:pos + 17]
    pos += 17
    if not lat:
        step = pl.program_id(0)
        u_next_s = u_s.at[(step + 1) % 2]
        u_s = u_s.at[step % 2]
    wt_s = None
    if lat:
        kwc_s, vwct_s = refs[pos:pos + 2]
        wb_hbm, wo_hbm = wb_ref, wo_ref
        wt_s, wb_ref, wo_ref, w_sem = refs[pos + 2:pos + 6]

        first_step = pl.program_id(0) == 0
        late_copies = (
            pltpu.make_async_copy(w_hbm.at[layer, :, C_ZW:W_COLS], wt_s, w_sem.at[0]),
            pltpu.make_async_copy(wb_hbm.at[layer], wb_ref, w_sem.at[1]),
            pltpu.make_async_copy(wo_hbm.at[layer], wo_ref, w_sem.at[2]),
        )

        @pl.when(first_step)
        def _():
            for cp in late_copies:
                cp.start()

    row_block = min(T, ROW_BLOCK)
    n_rows = T // row_block
    n_chunks = T // CHUNK
    chunks_per_block = row_block // CHUNK
    shift = mod_ref[0:1, :]
    scale = mod_ref[1:2, :]
    gate = mod_ref[2:3, :]
    ones_tile = _ones_row_tile()

    norm_rows = 64

    def row_slice(i, j, n):
        r0 = i * row_block + j
        return pl.ds(r0 if isinstance(r0, int) else pl.multiple_of(r0, n), n)

    def phase1_norm(i, src=x_ref, dst=u_s):
        for j in range(0, row_block, norm_rows):
            piece = row_slice(i, j, norm_rows)
            dst[piece, :] = (_layer_norm(src[piece, :]) * (1.0 + scale) + shift).astype(BF16)
            yield

    def phase1_proj(i):
        R = row_block
        rows = row_slice(i, 0, R)
        u = u_s[rows, :]

        def proj(c0, width):
            return _dot(u, w_ref[:, c0:c0 + width])

        if lat:
            cos = cos_ref[rows, :]
            sa = sa_ref[rows, :]
            sb = sb_ref[rows, :]

        def rope_wide(v):
            if not lat:
                return v
            slabs = [_rope(v[:, j:j + LANES], cos, sa, sb) for j in range(0, v.shape[1], LANES)]
            return slabs[0] if len(slabs) == 1 else jnp.concatenate(slabs, axis=1)

        qscale = LOG2E * HEAD_DIM ** -0.5
        qa = rope_wide(_group_rms(proj(C_QA, 512), bd_ref[...], g0_ref[...])) * qscale
        for h in range(N_HEADS):
            qa_s[h, rows, :] = qa[:, h * HEAD_DIM:(h + 1) * HEAD_DIM].astype(BF16)
        yield
        ka_n = _group_rms(proj(C_KA, LANES), bd_ref[0:LANES, 0:LANES], g1_ref[...])
        ka = rope_wide(ka_n)
        va = proj(C_VA, LANES)
        qw = rope_wide(proj(C_QW, 512)) * qscale
        for h in range(N_HEADS):
            qw_s[h, rows, :] = qw[:, h * HEAD_DIM:(h + 1) * HEAD_DIM].astype(BF16)
        yield
        kw_raw = proj(C_KW, LANES)
        kw = rope_wide(kw_raw)
        vw = proj(C_VW, LANES)
        for kv in range(N_KV):
            sl = slice(kv * HEAD_DIM, (kv + 1) * HEAD_DIM)
            ka_s[kv, rows, :] = ka[:, sl].astype(BF16)
            kw_s[kv, rows, :] = kw[:, sl].astype(BF16)
        yield
        qm_s[rows, :] = proj(C_QM, 256).astype(BF16)
        km_s[rows, :] = (proj(C_KM, 256) * (ML_DK ** -0.5)).astype(BF16)
        yield
        vm = proj(C_VM, 512)
        yield
        om_s[rows, :] = _sigmoid(proj(C_OM, 512)).astype(BF16)
        g_s[rows, :] = proj(C_GT, LANES) + gb_ref[...]
        yield
        for j in range(chunks_per_block):
            cj = i * chunks_per_block + j
            cr = slice(j * CHUNK, (j + 1) * CHUNK)
            va_t = jnp.transpose(va[cr, :])
            vw_t = jnp.transpose(vw[cr, :])
            for kv in range(N_KV):
                hd = slice(kv * HEAD_DIM, (kv + 1) * HEAD_DIM)
                vat_s[cj, kv, 0:HEAD_DIM, :] = va_t[hd, :].astype(BF16)
                vat_s[cj, kv, HEAD_DIM:VT_ROWS, :] = ones_tile
                vwt_s[cj, kv, 0:HEAD_DIM, :] = vw_t[hd, :].astype(BF16)
                vwt_s[cj, kv, HEAD_DIM:VT_ROWS, :] = ones_tile
            if not lat:
                tc = slice(i * row_block + j * CHUNK, i * row_block + (j + 1) * CHUNK)
                ka_t = jnp.transpose(ka_n[cr, :])
                kw_t = jnp.transpose(kw_raw[cr, :])
                for kv in range(N_KV):
                    hd = slice(kv * HEAD_DIM, (kv + 1) * HEAD_DIM)
                    ka_o[kv, :, tc] = ka_t[hd, :]
                    va_o[kv, :, tc] = va_t[hd, :]
                    kw_o[kv, :, tc] = kw_t[hd, :]
                    vw_o[kv, :, tc] = vw_t[hd, :]
            for h in range(ML_HEADS):
                vmt_s[cj, h, 0:ML_DV, :] = jnp.transpose(vm[cr, h * ML_DV:(h + 1) * ML_DV]).astype(BF16)
                vmt_s[cj, h, ML_DV:ST_ROWS, :] = ones_tile
            yield

    if not lat:
        @pl.when(step == 0)
        def _():
            _run(phase1_norm(0))
        yield from phase1_proj(0)
    else:
        def body1(i, carry):
            _run(phase1_norm(i))
            _run(phase1_proj(i))
            return carry
        lax.fori_loop(0, n_rows, body1, 0)

    if lat:
        for j in range(PAST_LEN // CHUNK):
            cols = slice(j * CHUNK, (j + 1) * CHUNK)
            ka_c = jnp.transpose(jnp.concatenate([cka_ref[kv, :, cols] for kv in range(N_KV)], axis=0))
            kw_c = jnp.transpose(jnp.concatenate([ckw_ref[kv, :, cols] for kv in range(N_KV)], axis=0))
            for kv in range(N_KV):
                hd = slice(kv * HEAD_DIM, (kv + 1) * HEAD_DIM)
                ka_s[kv, T + j * CHUNK:T + (j + 1) * CHUNK, :] = ka_c[:, hd].astype(BF16)
                kwc_s[kv, cols, :] = kw_c[:, hd].astype(BF16)
                vat_s[n_chunks + j, kv, 0:HEAD_DIM, :] = cva_ref[kv, :, cols].astype(BF16)
                vat_s[n_chunks + j, kv, HEAD_DIM:VT_ROWS, :] = ones_tile
                vwct_s[kv, 0:HEAD_DIM, cols] = cvw_ref[kv, :, cols].astype(BF16)
                vwct_s[kv, HEAD_DIM:VT_ROWS, cols] = ones_tile

    tq = CHUNK
    scores_in_flight = 2 if lat else 2 * N_KV
    m_cols = N_GROUP * tq

    def sink_row(kv):
        col = lax.broadcasted_iota(jnp.int32, (1, m_cols), 1)
        row = jnp.full((1, m_cols), sink_ref[layer * N_HEADS + kv * N_GROUP], F32)
        for g in range(1, N_GROUP):
            row = jnp.where(col >= g * tq, sink_ref[layer * N_HEADS + kv * N_GROUP + g], row)
        return row * LOG2E

    def load_q(q_s, kv, q0):
        return q_s[kv * N_GROUP:(kv + 1) * N_GROUP, pl.ds(q0, tq), :].reshape(m_cols, HEAD_DIM)

    def store_heads(dst, q0, kv, o_t):
        for p in range(N_GROUP // 2):
            blk = jnp.concatenate([o_t[:, (2 * p) * tq:(2 * p + 1) * tq],
                                   o_t[:, (2 * p + 1) * tq:(2 * p + 2) * tq]], axis=0)
            c0 = (kv * N_GROUP + 2 * p) * HEAD_DIM
            dst[pl.ds(q0, tq), c0:c0 + LANES] = jnp.transpose(blk).astype(dst.dtype)

    def key_tiles(k_ref, vt_ref, kv, n_key_chunks, chunks_per_tile):
        def tile(c0):
            def load():
                k = k_ref[kv, c0 * CHUNK:(c0 + chunks_per_tile) * CHUNK, :]
                v_t = [vt_ref[c0 + j, kv] for j in range(chunks_per_tile)]
                return k, (v_t[0] if len(v_t) == 1 else jnp.concatenate(v_t, axis=1)), None
            return load
        return [tile(c0) for c0 in range(0, n_key_chunks, chunks_per_tile)]

    def band_tiles(kv, n, q0):
        j0 = jnp.clip(n - 1, 0, n_chunks - 3)
        w0 = pl.multiple_of(j0 * CHUNK, CHUNK)

        def band(c0, nc):
            def load():
                rows = nc * CHUNK
                kpos = w0 + c0 * CHUNK + lax.broadcasted_iota(jnp.int32, (rows, m_cols), 0)
                qpos = q0 + (lax.broadcasted_iota(jnp.int32, (rows, m_cols), 1) & (tq - 1))
                v_t = [vwt_s[j0 + c0 + j, kv] for j in range(nc)]
                return (kw_s[kv, pl.ds(w0 + c0 * CHUNK, rows), :],
                        v_t[0] if nc == 1 else jnp.concatenate(v_t, axis=1),
                        jnp.abs(kpos - qpos) <= WINDOW)
            return load
        return [band(0, 2), band(2, 1), lambda: (kwc_s[kv], vwct_s[kv], None)]

    def attend(n, q0, chunks_per_tile):
        chains, dsts = [], []
        n_key_chunks = ka_s.shape[1] // CHUNK
        for kv in range(N_KV):
            chains.append((load_q(qa_s, kv, q0), key_tiles(ka_s, vat_s, kv, n_key_chunks, chunks_per_tile), None))
            dsts.append((ya_s, kv))
        for kv in range(N_KV):
            tiles = band_tiles(kv, n, q0) if lat else key_tiles(kw_s, vwt_s, kv, n_chunks, chunks_per_tile)
            chains.append((load_q(qw_s, kv, q0), tiles, sink_row(kv)))
            dsts.append((yw_s, kv))
        results = []
        yield from _attend_chains(chains, results, scores_in_flight)
        for (dst, kv), o_t in zip(dsts, results):
            store_heads(dst, q0, kv, o_t)

    n_pairs = ML_HEADS // 2
    hmt_s[...] = jnp.zeros_like(hmt_s)
    for d in range(2):
        if lat:
            for pr in range(n_pairs):
                c_s[d * n_pairs + pr, 0:ML_DV, :] = jnp.transpose(
                    jnp.concatenate([c0_ref[d, 2 * pr], c0_ref[d, 2 * pr + 1]], axis=0))
        for h in range(ML_HEADS):
            idx = d * n_pairs + h // 2
            hl = slice((h % 2) * ML_DK, (h % 2 + 1) * ML_DK)
            c_s[idx, ML_DV:ST_ROWS, hl] = jnp.zeros((BF16_ROWS, ML_DK), F32)
            if lat:
                c_s[idx, ML_DV:ML_DV + 1, hl] = n0_ref[d, h:h + 1, :]
                m_s[d * ML_HEADS + h:d * ML_HEADS + h + 1, :] = jnp.broadcast_to(
                    m0_ref[d:d + 1, h:h + 1], (1, LANES))
            else:
                c_s[idx, 0:ML_DV, hl] = jnp.zeros((ML_DV, ML_DK), F32)
                m_s[d * ML_HEADS + h:d * ML_HEADS + h + 1, :] = jnp.zeros((1, LANES), F32)

    L = CHUNK
    s_idx = lax.broadcasted_iota(jnp.int32, (L, ML_HEADS * L), 0)
    t_idx = lax.broadcasted_iota(jnp.int32, (L, ML_HEADS * L), 1) & (L - 1)
    lane_row = lax.broadcasted_iota(jnp.int32, (1, LANES), 1)
    low_half = lax.broadcasted_iota(jnp.int32, (L, LANES), 1) < ML_DK
    ones_ll = jnp.ones((L, L), BF16)
    zeros_ll = jnp.zeros((L, L), BF16)

    def heads_row(src, r0, c0=0):
        return jnp.concatenate([src[r0 + h:r0 + h + 1, c0:c0 + L] for h in range(ML_HEADS)], axis=1)

    def block_diag_rows(x):
        zero = jnp.zeros_like(x)
        return jnp.concatenate([jnp.where(low_half, x, zero), jnp.where(low_half, zero, x)], axis=0)

    def mlstm_streams(streams):
        m_state = {d: heads_row(m_s, d * ML_HEADS) for d in sorted({d for d, _ in streams})}
        c_state = {d: [c_s[d * n_pairs + pr] for pr in range(n_pairs)] for d in m_state}
        work = []
        for d, cc in streams:
            r0 = cc * L if isinstance(cc, int) else pl.multiple_of(cc * L, L)
            rows = pl.ds(r0, L)
            g = g_s[rows, :]
            tri_r = triu_ref[...] if d == 0 else tril_ref[...]
            gi0 = 2 * ML_HEADS * d

            def row_sums():
                g_t = jnp.transpose(g)[0:N_GATES, :]
                ft_hi, ft_lo = _split(_log_sigmoid(g_t))
                tr = jnp.concatenate([tri_r, ones_ll], axis=1)
                return g_t, _dot(ft_hi, tr) + _dot(ft_lo, tr)

            if lat:
                g_t, ct = row_sums()
                r_rows = g_t[gi0:gi0 + 8, :] - pltpu.roll(ct[gi0:gi0 + 8, 0:L], ML_HEADS, 0)
                r = jnp.transpose(jnp.concatenate([r_rows, jnp.zeros((L - 8, L), F32)], axis=0))
                r_lane0 = 0
            else:
                f_hi, f_lo = _split(_log_sigmoid(g))
                tri_c = tril_ref[...] if d == 0 else triu_ref[...]
                cum = _dot(tri_c, f_hi) + _dot(tri_c, f_lo)
                r = g - pltpu.roll(cum, LANES - ML_HEADS, 1)
                r_lane0 = gi0
            rb = jnp.concatenate([jnp.broadcast_to(r[:, r_lane0 + h:r_lane0 + h + 1], (L, L))
                                  for h in range(ML_HEADS)], axis=1)
            if not lat:
                g_t, ct = row_sums()
            work.append(dict(d=d, cc=cc, rows=rows, rb=rb, g_t=g_t, ct=ct))
            yield
        for w in work:
            d = w["d"]
            valid = (s_idx <= t_idx) if d == 0 else (s_idx >= t_idx)
            gi0 = 2 * ML_HEADS * d
            gf0 = gi0 + ML_HEADS
            b_row = heads_row(w["ct"], gf0)
            b_last = heads_row(w["ct"], gf0, L)
            i_row = heads_row(w["g_t"], gi0)
            m_prev = m_state[d]
            a_row = b_row + m_prev
            dm = jnp.where(valid, b_row + w["rb"], -jnp.inf)
            mt = jnp.maximum(a_row, jnp.max(dm, axis=0, keepdims=True))
            w["p"] = jnp.exp(dm - mt)
            w["w_inter"] = jnp.exp(a_row - mt)
            w["floor"] = jnp.exp(-mt)
            g_row = b_last - b_row + i_row
            g_max = jnp.concatenate(
                [jnp.broadcast_to(jnp.max(g_row[:, h * L:(h + 1) * L], axis=-1, keepdims=True), (1, L))
                 for h in range(ML_HEADS)], axis=1)
            m_new = jnp.maximum(b_last + m_prev, g_max)
            w["ws"] = jnp.exp(g_row - m_new)
            w["wc"] = jnp.exp(b_last + m_prev - m_new)
            m_state[d] = m_new
            yield
        for w in work:
            rows, cc = w["rows"], w["cc"]
            w["pairs"] = []
            for pr in range(n_pairs):
                lanes = slice(pr * LANES, (pr + 1) * LANES)
                cols = slice(pr * 2 * L, (pr + 1) * 2 * L)
                k_pair = km_s[rows, lanes]
                q_bd = block_diag_rows(qm_s[rows, lanes])
                k_bd = block_diag_rows(k_pair)
                s_t = (_dot_tb(k_pair, q_bd) * w["p"][:, cols]).astype(BF16)
                s_bd = jnp.concatenate(
                    [jnp.concatenate([s_t[:, 0:L], zeros_ll], axis=1),
                     jnp.concatenate([zeros_ll, s_t[:, L:2 * L]], axis=1)], axis=0)
                v_t = jnp.concatenate([vmt_s[cc, 2 * pr], vmt_s[cc, 2 * pr + 1]], axis=1)
                intra = _dot(v_t, s_bd)
                wv = (v_t.astype(F32) * w["ws"][:, cols]).astype(BF16)
                w["pairs"].append((q_bd, intra, _dot(wv, k_bd)))
            yield
        for w in work:
            d, cc = w["d"], w["cc"]
            for pr, (q_bd, intra, update) in enumerate(w["pairs"]):
                cols = slice(pr * 2 * L, (pr + 1) * 2 * L)
                state = c_state[d][pr]
                tot = intra + w["w_inter"][:, cols] * _dot_tb(state.astype(BF16), q_bd)
                h_t = tot[0:ML_DV, :] / jnp.maximum(jnp.abs(tot[ML_DV:ML_DV + 1, :]), w["floor"][:, cols])
                for e in range(2):
                    hr = slice((2 * pr + e) * ML_DV, (2 * pr + e + 1) * ML_DV)
                    hmt_s[cc, hr, :] += h_t[:, e * L:(e + 1) * L]
                wc = w["wc"]
                wc_pair = jnp.where(lane_row < ML_DK, wc[:, 2 * pr * L:(2 * pr + 1) * L],
                                    wc[:, (2 * pr + 1) * L:(2 * pr + 2) * L])
                c_state[d][pr] = wc_pair * state + update
            yield
        for d in m_state:
            for h in range(ML_HEADS):
                m_s[d * ML_HEADS + h:d * ML_HEADS + h + 1, :] = m_state[d][:, h * L:(h + 1) * L]
            for pr in range(n_pairs):
                c_s[d * n_pairs + pr] = c_state[d][pr]

    def scan_streams(cs):
        return [(d, c if d == 0 else n_chunks - 1 - c) for c in cs for d in range(2)]

    if lat:
        def body2(n, carry):
            _run(_interleave([(attend(n, pl.multiple_of(n * tq, tq), 2), 1),
                              (mlstm_streams(scan_streams([n])), 1)]))
            return carry
        lax.fori_loop(0, n_chunks, body2, 0)
    else:
        def all_tiles():
            for n in range(T // tq):
                yield from attend(n, n * tq, 2)
        yield from _interleave([(all_tiles(), 1), (mlstm_streams(scan_streams(range(n_chunks))), 3)])

    if not lat:
        for d in range(2):
            for pr in range(n_pairs):
                c_t = jnp.transpose(c_s[d * n_pairs + pr, 0:ML_DV, :])
                for e in range(2):
                    c_o[d, 2 * pr + e] = c_t[e * ML_DK:(e + 1) * ML_DK, :]
            for h in range(ML_HEADS):
                idx = d * n_pairs + h // 2
                hl = slice((h % 2) * ML_DK, (h % 2 + 1) * ML_DK)
                n_o[d, h:h + 1, :] = c_s[idx, ML_DV:ML_DV + 1, hl]
                m_o[d:d + 1, h:h + 1] = m_s[d * ML_HEADS + h:d * ML_HEADS + h + 1, 0:1]

    def phase3_merge(i, out):
        rows = row_slice(i, 0, row_block)
        u = u_s[rows, :]

        def proj(c0, width):
            if wt_s is not None and c0 >= C_ZW:
                return _dot(u, wt_s[:, c0 - C_ZW:c0 - C_ZW + width])
            return _dot(u, w_ref[:, c0:c0 + width])

        hm = jnp.concatenate(
            [jnp.concatenate([jnp.transpose(hmt_s[i * chunks_per_block + j, h * ML_DV:(h + 1) * ML_DV, :])
                              for h in range(ML_HEADS)], axis=1)
             for j in range(chunks_per_block)], axis=0)
        ys = (ya_s[rows, :].astype(F32), om_s[rows, :].astype(F32) * hm, yw_s[rows, :].astype(F32))
        yield
        merged = None
        for b, (yb, zc) in enumerate(zip(ys, (C_ZA, C_ZM, C_ZW))):
            z = proj(zc, BRANCH_W)
            t = (yb * (z * _sigmoid(z))).astype(BF16)
            yield
            pb = _sigmoid(proj(C_GMERGE + b * D_MODEL, D_MODEL)) * _dot(t, wb_ref[b])
            merged = pb if merged is None else merged + pb
            yield
        out.append(_dot(merged.astype(BF16), wo_ref[...]))
        yield

    def phase3_norm(i, o):
        for j in range(0, row_block, norm_rows):
            piece = row_slice(i, j, norm_rows)
            hres = ALPHA * x_ref[piece, :] + gate * o[j:j + norm_rows, :]
            y_ref[piece, :] = _layer_norm(hres) * lng_ref[...] + lnb_ref[...]
            yield

    if lat:
        @pl.when(first_step)
        def _():
            for cp in late_copies:
                cp.wait()

    def phase3(i):
        box = []
        yield from phase3_merge(i, box)
        yield from phase3_norm(i, box[0])

    if not lat:
        box = []
        yield from _interleave([(phase3_merge(0, box), 1), (phase1_norm(0, x_next_ref, u_next_s), 1)])
        yield from phase3_norm(0, box[0])
    else:
        def body4(i, carry):
            _run(phase3(i))
            return carry
        lax.fori_loop(0, n_rows, body4, 0)


def _const_spec(shape):
    nd = len(shape)
    return pl.BlockSpec(shape, lambda b: (0,) * nd, pipeline_mode=pl.Buffered(1))


def _layer_spec(shape, layer):
    nd = len(shape) - 1
    return pl.BlockSpec((None,) + tuple(shape[1:]), lambda b: (layer,) + (0,) * nd,
                        pipeline_mode=pl.Buffered(1))


def _layer_call(lat, layer, x, mod, weights, consts, extra):
    B, T, _ = x.shape
    S = T + PAST_LEN if lat else T
    n_chunks = T // CHUNK

    seq_spec = pl.BlockSpec((None, T, D_MODEL), lambda b: (b, 0, 0))
    out_seq_spec = seq_spec
    if lat:
        mod_spec = pl.BlockSpec((None, None, 3, D_MODEL), lambda b: (layer, b + 1, 0, 0))
    else:
        mod_spec = pl.BlockSpec((None, None, 3, D_MODEL), lambda b: (layer, 0, 0, 0))
    *vmem_consts, sink = consts
    in_specs = [seq_spec, mod_spec]
    w_all, wb_all, wo_all = weights[:3]
    if lat:
        in_specs.append(
            pl.BlockSpec((None, D_MODEL, C_ZW), lambda b: (layer, 0, 0), pipeline_mode=pl.Buffered(1)))
        in_specs += [pl.BlockSpec(memory_space=pl.ANY)] * 2
        in_specs += [_layer_spec(a.shape, layer) for a in weights[3:]]
    else:
        in_specs += [_layer_spec(a.shape, layer) for a in weights]
    in_specs += [_const_spec(a.shape) for a in vmem_consts]
    in_specs.append(pl.BlockSpec(memory_space=pltpu.SMEM))
    args = [x, mod, *weights, *vmem_consts, sink]

    if lat:
        cos, sa, sb, cka, cva, ckw, cvw, sc, sn, sm = extra
        in_specs += [_const_spec(cos.shape)] * 3
        kv_spec = pl.BlockSpec((None, None, N_KV, HEAD_DIM, PAST_LEN), lambda b: (b, layer, 0, 0, 0))
        in_specs += [kv_spec] * 4
        in_specs += [
            pl.BlockSpec((None, None, 2, ML_HEADS, ML_DK, ML_DV), lambda b: (b, layer, 0, 0, 0, 0)),
            pl.BlockSpec((None, None, 2, ML_HEADS, ML_DK), lambda b: (b, layer, 0, 0, 0)),
            pl.BlockSpec((None, None, 2, ML_HEADS), lambda b: (b, layer, 0, 0)),
        ]
        args += [cos, sa, sb, cka, cva, ckw, cvw, sc, sn, sm]
        out_specs = out_seq_spec
        out_shape = jax.ShapeDtypeStruct((B, T, D_MODEL), F32)
        aliases = {}
    else:
        kv_shape = (B, DEPTH, N_KV, HEAD_DIM, T)
        c_shape = (B, DEPTH, 2, ML_HEADS, ML_DK, ML_DV)
        if layer == 0:
            aliases = {}
            ld = DEPTH
            layer_idx = 0
        else:
            first_acc = len(args)
            in_specs += [pl.BlockSpec(memory_space=pl.ANY)] * len(extra)
            args += list(extra)
            aliases = {first_acc + i: 1 + i for i in range(len(extra))}
            assert len(extra) == CTX_ALIASED
            ld = None
            layer_idx = layer
        in_specs.append(pl.BlockSpec((None, T, D_MODEL), lambda b: (jnp.minimum(b + 1, B - 1), 0, 0)))
        args.append(x)
        kv_out = pl.BlockSpec((None, ld, N_KV, HEAD_DIM, T), lambda b: (b, layer_idx, 0, 0, 0))
        out_specs = [
            seq_spec, kv_out, kv_out, kv_out, kv_out,
            pl.BlockSpec((None, ld, 2, ML_HEADS, ML_DK, ML_DV), lambda b: (b, layer_idx, 0, 0, 0, 0)),
            pl.BlockSpec((None, 2, ML_HEADS, ML_DK), lambda b: (b, 0, 0, 0)),
            pl.BlockSpec((None, 2, ML_HEADS), lambda b: (b, 0, 0)),
        ]
        out_shape = [jax.ShapeDtypeStruct((B, T, D_MODEL), F32)]
        out_shape += [jax.ShapeDtypeStruct(kv_shape, F32)] * 4 + [jax.ShapeDtypeStruct(c_shape, F32)]
        out_shape += [
            jax.ShapeDtypeStruct((B, 2, ML_HEADS, ML_DK), F32),
            jax.ShapeDtypeStruct((B, 2, ML_HEADS), F32),
        ]

    scratch = [
        ((T, D_MODEL) if lat else (2, T, D_MODEL), BF16),
        ((N_HEADS, T, HEAD_DIM), BF16),
        ((N_HEADS, T, HEAD_DIM), BF16),
        ((N_KV, S, HEAD_DIM), BF16),
        ((S // CHUNK, N_KV, VT_ROWS, CHUNK), BF16),
        ((N_KV, T, HEAD_DIM), BF16),
        ((n_chunks, N_KV, VT_ROWS, CHUNK), BF16),
        ((T, ML_HEADS * ML_DK), BF16),
        ((T, ML_HEADS * ML_DK), BF16),
        ((n_chunks, ML_HEADS, ST_ROWS, CHUNK), BF16),
        ((T, ML_HEADS * ML_DV), BF16),
        ((T, LANES), F32),
        ((T, BRANCH_W), BF16),
        ((T, BRANCH_W), BF16),
        ((n_chunks, ML_HEADS * ML_DV, CHUNK), F32),
        ((ML_HEADS, ST_ROWS, 2 * ML_DK), F32),
        ((2 * ML_HEADS, LANES), F32),
    ]
    if lat:
        scratch += [((N_KV, PAST_LEN, HEAD_DIM), BF16),
                    ((N_KV, VT_ROWS, PAST_LEN), BF16)]
    scratch = [pltpu.VMEM(shape, dtype) for shape, dtype in scratch]
    if lat:
        in_specs.append(pl.BlockSpec(memory_space=pl.ANY))
        args.append(w_all)
        scratch += [
            pltpu.VMEM((D_MODEL, W_COLS - C_ZW), BF16),
            pltpu.VMEM(tuple(wb_all.shape[1:]), BF16),
            pltpu.VMEM(tuple(wo_all.shape[1:]), BF16),
            pltpu.SemaphoreType.DMA((3,)),
        ]

    return pl.pallas_call(
        functools.partial(_layer_kernel, lat, T, layer),
        grid=(B,),
        in_specs=in_specs,
        out_specs=out_specs,
        out_shape=out_shape,
        scratch_shapes=scratch,
        input_output_aliases=aliases,
        compiler_params=pltpu.CompilerParams(
            dimension_semantics=("arbitrary",),
            vmem_limit_bytes=LATENT_VMEM_LIMIT_BYTES if lat else VMEM_LIMIT_BYTES),
        name=("latent_layer" if lat else "context_layer") + str(layer),
    )(*args)


def _rope_tables(T):
    f32 = np.float32
    rows = T // GRID_W
    row = np.repeat(np.arange(rows, dtype=f32), GRID_W)
    col = np.tile(np.arange(GRID_W, dtype=f32), rows)
    inv = (f32(ROPE_THETA) ** (-np.arange(0, AXIS_DIM, 2, dtype=f32) / f32(AXIS_DIM))).astype(f32)
    ar = row[:, None] * inv
    ac = col[:, None] * inv
    ang = np.concatenate([ar, ar, ac, ac], axis=-1)
    ang = np.concatenate([ang, ang], axis=-1)
    first_half = (np.arange(LANES) % AXIS_DIM) < (AXIS_DIM // 2)
    cos = np.cos(ang).astype(f32)
    sin = np.sin(ang).astype(f32)
    sa = np.where(first_half[None, :], -sin, f32(0.0))
    sb = np.where(first_half[None, :], f32(0.0), sin)
    return jnp.asarray(cos), jnp.asarray(sa), jnp.asarray(sb)


def _constants(sink_logit):
    grp = np.arange(N_HEADS * HEAD_DIM) // HEAD_DIM
    bd = jnp.asarray(grp[:, None] == grp[None, :], dtype=BF16)
    ti = np.arange(CHUNK)
    tril = jnp.asarray(ti[None, :] <= ti[:, None], dtype=BF16)
    triu = jnp.asarray(ti[None, :] >= ti[:, None], dtype=BF16)
    return bd, tril, triu, sink_logit.reshape(-1)


def kernel(x_prompt, x_sample, cache_attn_k, cache_attn_v, cache_win_k, cache_win_v, state_mlstm_C,
           state_mlstm_n, state_mlstm_m, c, c_ctx, w_mod, b_mod, w_in, qk_gain, sink_logit,
           mlstm_gate_bias, w_branch, w_out, ln_gain, ln_bias):
    dec_b = x_sample.shape[0]
    assert dec_b + 1 <= 8
    cond = jnp.concatenate([c_ctx[None, :], c, jnp.zeros((8 - 1 - dec_b, D_MODEL), F32)], axis=0)
    mod = _modulation(cond, w_mod, b_mod).reshape(DEPTH, 8, 3, D_MODEL)

    consts = _constants(sink_logit)
    cos, sa, sb = _rope_tables(x_sample.shape[1])
    lane_pad = jnp.zeros((DEPTH, 1, LANES - N_GATES), F32)
    weights = (
        *_prep_weights(jnp.swapaxes(w_in, 1, 2), w_branch, w_out),
        jnp.tile(qk_gain[:, 0:1, :], (1, 1, N_HEADS)),
        jnp.tile(qk_gain[:, 1:2, :], (1, 1, N_KV)),
        jnp.concatenate([mlstm_gate_bias.reshape(DEPTH, 1, N_GATES), lane_pad], axis=2),
        ln_gain[:, None, :], ln_bias[:, None, :],
    )

    ctx = None
    cached_t = [jnp.swapaxes(a, -1, -2)
                for a in (cache_attn_k, cache_attn_v, cache_win_k, cache_win_v, state_mlstm_C)]
    xp, xs = x_prompt, x_sample
    small = []
    for l in range(DEPTH):
        outs = _layer_call(False, l, xp, mod, weights, consts, ctx)
        xp = outs[0]
        ctx = tuple(outs[1:6])
        small.append(outs[6:])
        extra = (cos, sa, sb, *cached_t, state_mlstm_n, state_mlstm_m)
        xs = _layer_call(True, l, xs, mod, weights, consts, extra)
    new_n, new_m = [jnp.stack([small[l][i] for l in range(DEPTH)], axis=1) for i in range(2)]
    return (xp, xs, *[jnp.swapaxes(a, -1, -2) for a in ctx], new_n, new_m)
```

```python
import functools

import jax
import jax.numpy as jnp
import numpy as np
from jax import lax
from jax.experimental import pallas as pl
from jax.experimental.pallas import tpu as pltpu

F32 = jnp.float32
BF16 = jnp.bfloat16

D_MODEL = 1024
DEPTH = 2
PAST_LEN = 256
GRID_W = 64
HEAD_DIM = 64
N_HEADS = 8
N_KV = 2
N_GROUP = N_HEADS // N_KV
WINDOW = 128
ML_HEADS = 4
ML_DK = 64
ML_DV = 128
CHUNK = 128
BRANCH_W = 512
ROPE_THETA = 10000.0
AXIS_DIM = HEAD_DIM // 2
LN_EPS = 1e-6
RMS_EPS = 1e-6
ALPHA = (2.0 * DEPTH) ** 0.25
LOG2E = float(np.log2(np.e))
LANES = 128
BF16_ROWS = 16
ROW_BLOCK = 256
MOD_COLS = 512
CTX_ALIASED = 5
N_SHARED = 14
VMEM_LIMIT_BYTES = 60 * 1024 * 1024
LATENT_VMEM_LIMIT_BYTES = 62 * 1024 * 1024
VT_ROWS = HEAD_DIM + BF16_ROWS
ST_ROWS = ML_DV + BF16_ROWS

_SIZES = (512, 128, 128, 512, 256, 256, 512, 512, 128, 512, 512, 128, 128, 512, 3072)
_OFF = [int(v) for v in np.concatenate([[0], np.cumsum(_SIZES)])]
(C_QA, C_KA, C_VA, C_ZA, C_QM, C_KM, C_VM, C_OM, C_GT, C_ZM, C_QW, C_KW, C_VW, C_ZW, C_GMERGE) = _OFF[:-1]
W_COLS = _OFF[-1]
GATE_COL = 2816
N_GATES = 4 * ML_HEADS


def _dot(a, b):
    return jnp.dot(a, b, preferred_element_type=F32)


def _dot_tb(a, b):
    return lax.dot_general(a, b, (((1,), (1,)), ((), ())), preferred_element_type=F32)


def _split(a):
    hi = a.astype(BF16)
    lo = (a - hi.astype(F32)).astype(BF16)
    return hi, lo


def _layer_norm(x):
    mu = jnp.mean(x, axis=-1, keepdims=True)
    xc = x - mu
    var = jnp.mean(xc * xc, axis=-1, keepdims=True)
    return xc * lax.rsqrt(var + LN_EPS)


def _log_sigmoid(x):
    return jnp.minimum(x, 0.0) - jnp.log(1.0 + jnp.exp(-jnp.abs(x)))


def _sigmoid(x):
    return jax.nn.sigmoid(x)


def _mod_kernel(c_ref, w_ref, b_ref, o_ref):
    c = c_ref[...]
    s = c * _sigmoid(c)
    s_hi, s_lo = _split(s)
    w = w_ref[...]
    w_hi, w_lo = _split(w)
    o_ref[...] = _dot(s_hi, w_hi) + _dot(s_lo, w_hi) + _dot(s_hi, w_lo) + b_ref[...]


def _modulation(cond, w_mod, b_mod):
    rows = cond.shape[0]
    tn = MOD_COLS
    return pl.pallas_call(
        _mod_kernel,
        grid=(DEPTH, 3 * D_MODEL // tn),
        in_specs=[
            pl.BlockSpec((rows, D_MODEL), lambda l, j: (0, 0)),
            pl.BlockSpec((None, D_MODEL, tn), lambda l, j: (l, 0, j)),
            pl.BlockSpec((None, 1, tn), lambda l, j: (l, 0, j)),
        ],
        out_specs=pl.BlockSpec((None, rows, tn), lambda l, j: (l, 0, j)),
        out_shape=jax.ShapeDtypeStruct((DEPTH, rows, 3 * D_MODEL), F32),
        compiler_params=pltpu.CompilerParams(dimension_semantics=("arbitrary", "arbitrary")),
        name="adaln_modulation",
    )(cond, w_mod, b_mod.reshape(DEPTH, 1, 3 * D_MODEL))


PREP_K = 256


def _prep_kernel(wt_ref, wb_ref, wo_ref, o_ref, ob_ref, oo_ref):
    ob_ref[...] = wb_ref[...].astype(BF16)
    oo_ref[...] = wo_ref[...].astype(BF16)
    lane = lax.broadcasted_iota(jnp.int32, (PREP_K, LANES), 1)
    for c0 in range(0, W_COLS, LANES):
        r0 = c0 if c0 <= GATE_COL else c0 - (LANES - N_GATES)
        slab = jnp.transpose(wt_ref[r0:r0 + LANES, :])
        if c0 == GATE_COL:
            slab = jnp.where(lane < N_GATES, slab, 0.0)
        o_ref[:, c0:c0 + LANES] = slab.astype(BF16)


def _prep_weights(w_in_t, w_branch, w_out):
    n_in = w_in_t.shape[1]
    steps = D_MODEL // PREP_K
    _, n_br, br_w, _ = w_branch.shape
    br_rows, out_rows = br_w // steps, w_out.shape[1] // steps
    br_spec = pl.BlockSpec((None, n_br, br_rows, D_MODEL), lambda l, i: (l, 0, i, 0))
    out_spec = pl.BlockSpec((None, out_rows, D_MODEL), lambda l, i: (l, i, 0))
    return pl.pallas_call(
        _prep_kernel,
        grid=(DEPTH, steps),
        in_specs=[pl.BlockSpec((None, n_in, PREP_K), lambda l, i: (l, 0, i)), br_spec, out_spec],
        out_specs=[pl.BlockSpec((None, PREP_K, W_COLS), lambda l, i: (l, i, 0)), br_spec, out_spec],
        out_shape=[jax.ShapeDtypeStruct((DEPTH, D_MODEL, W_COLS), BF16),
                   jax.ShapeDtypeStruct(w_branch.shape, BF16), jax.ShapeDtypeStruct(w_out.shape, BF16)],
        compiler_params=pltpu.CompilerParams(
            dimension_semantics=("arbitrary", "arbitrary"), vmem_limit_bytes=VMEM_LIMIT_BYTES),
        name="in_weight_prep",
    )(w_in_t, w_branch, w_out)


def _rope(x, cos, sa, sb):
    return x * cos + pltpu.roll(x, LANES - AXIS_DIM // 2, 1) * sa + pltpu.roll(x, AXIS_DIM // 2, 1) * sb


def _group_rms(x, bd, gain):
    ms = _dot((x * x).astype(BF16), bd) * (1.0 / HEAD_DIM)
    return x * lax.rsqrt(ms + RMS_EPS) * gain


def _ones_row_tile():
    r = lax.broadcasted_iota(jnp.int32, (BF16_ROWS, LANES), 0)
    return jnp.where(r == 0, 1.0, 0.0).astype(BF16)


def _attend_chains(chains, results, depth):
    state = []
    for qs, tiles, sink_row in chains:
        m_cols = qs.shape[0]
        if sink_row is None:
            m = jnp.full((1, m_cols), -jnp.inf, F32)
            acc = jnp.zeros((VT_ROWS, m_cols), F32)
        else:
            m = sink_row
            r = lax.broadcasted_iota(jnp.int32, (VT_ROWS, m_cols), 0)
            acc = jnp.where(r == HEAD_DIM, 1.0, 0.0)
        state.append([m, acc])
    order = [(t, c) for t in range(max(len(tiles) for _, tiles, _ in chains))
             for c, (_, tiles, _) in enumerate(chains) if t < len(tiles)]

    def issue(t, c):
        qs, tiles, _ = chains[c]
        tile = tiles[t]()
        return tile, _dot_tb(tile[0], qs)

    pending = [issue(*item) for item in order[:depth]]
    yield
    for k, (t, c) in enumerate(order):
        (_, v_t, valid), s = pending.pop(0)
        if k + depth < len(order):
            pending.append(issue(*order[k + depth]))
        if valid is not None:
            s = jnp.where(valid, s, -jnp.inf)
        m, acc = state[c]
        m_new = jnp.maximum(m, jnp.max(s, axis=0, keepdims=True))
        p = jnp.exp2(s - m_new).astype(BF16)
        state[c] = [m_new, acc * jnp.exp2(m - m_new) + _dot(v_t, p)]
        if k + 1 == len(order) or order[k + 1][0] != t:
            yield
    results.extend(acc[0:HEAD_DIM, :] / acc[HEAD_DIM:HEAD_DIM + 1, :] for _, acc in state)


def _interleave(weighted):
    live = [[g, n] for g, n in weighted]
    while live:
        for item in list(live):
            for _ in range(item[1]):
                try:
                    next(item[0])
                except StopIteration:
                    live.remove(item)
                    break
        yield


def _run(gen):
    for _ in gen:
        pass


def _layer_kernel(lat, T, layer, *refs):
    _run(_sequence(lat, T, layer, *refs))


def _sequence(lat, T, layer, *refs):
    refs = list(refs)
    x_ref, mod_ref, w_ref, wb_ref, wo_ref = refs[:5]
    g0_ref, g1_ref, gb_ref, lng_ref, lnb_ref = refs[5:10]
    bd_ref, tril_ref, triu_ref, sink_ref = refs[10:N_SHARED]
    pos = N_SHARED
    if lat:
        cos_ref, sa_ref, sb_ref = refs[pos:pos + 3]
        cka_ref, cva_ref, ckw_ref, cvw_ref, c0_ref, n0_ref, m0_ref = refs[pos + 3:pos + 10]
        pos += 10
        w_hbm = refs[pos]
        y_ref = refs[pos + 1]
        pos += 2
    else:
        if layer > 0:
            pos += CTX_ALIASED
        x_next_ref = refs[pos]
        pos += 1
        y_ref, ka_o, va_o, kw_o, vw_o, c_o, n_o, m_o = refs[pos:pos + 8]
        pos += 8
        if layer == 0:
            for full in (ka_o, va_o, kw_o, vw_o, c_o):
                full[1:] = jnp.zeros((DEPTH - 1,) + full.shape[1:], F32)
            ka_o, va_o, kw_o, vw_o, c_o = (full.at[0] for full in (ka_o, va_o, kw_o, vw_o, c_o))
    (u_s, qa_s, qw_s, ka_s, vat_s, kw_s, vwt_s, qm_s, km_s, vmt_s, om_s, g_s,
     ya_s, yw_s, hmt_s, c_s, m_s) = refs[pos:pos + 17]
    pos += 17
    if not lat:
        step = pl.program_id(0)
        u_next_s = u_s.at[(step + 1) % 2]
        u_s = u_s.at[step % 2]
    wt_s = None
    if lat:
        kwc_s, vwct_s = refs[pos:pos + 2]
        wb_hbm, wo_hbm = wb_ref, wo_ref
        wt_s, wb_ref, wo_ref, w_sem = refs[pos + 2:pos + 6]

        first_step = pl.program_id(0) == 0
        late_copies = (
            pltpu.make_async_copy(w_hbm.at[layer, :, C_ZW:W_COLS], wt_s, w_sem.at[0]),
            pltpu.make_async_copy(wb_hbm.at[layer], wb_ref, w_sem.at[1]),
            pltpu.make_async_copy(wo_hbm.at[layer], wo_ref, w_sem.at[2]),
        )

        @pl.when(first_step)
        def _():
            for cp in late_copies:
                cp.start()

    row_block = min(T, ROW_BLOCK)
    n_rows = T // row_block
    n_chunks = T // CHUNK
    chunks_per_block = row_block // CHUNK
    shift = mod_ref[0:1, :]
    scale = mod_ref[1:2, :]
    gate = mod_ref[2:3, :]
    ones_tile = _ones_row_tile()

    norm_rows = 64

    def row_slice(i, j, n):
        r0 = i * row_block + j
        return pl.ds(r0 if isinstance(r0, int) else pl.multiple_of(r0, n), n)

    def phase1_norm(i, src=x_ref, dst=u_s):
        for j in range(0, row_block, norm_rows):
            piece = row_slice(i, j, norm_rows)
            dst[piece, :] = (_layer_norm(src[piece, :]) * (1.0 + scale) + shift).astype(BF16)
            yield

    def phase1_proj(i):
        R = row_block
        rows = row_slice(i, 0, R)
        u = u_s[rows, :]

        def proj(c0, width):
            return _dot(u, w_ref[:, c0:c0 + width])

        if lat:
            cos = cos_ref[rows, :]
            sa = sa_ref[rows, :]
            sb = sb_ref[rows, :]

        def rope_wide(v):
            if not lat:
                return v
            slabs = [_rope(v[:, j:j + LANES], cos, sa, sb) for j in range(0, v.shape[1], LANES)]
            return slabs[0] if len(slabs) == 1 else jnp.concatenate(slabs, axis=1)

        qscale = LOG2E * HEAD_DIM ** -0.5
        qa = rope_wide(_group_rms(proj(C_QA, 512), bd_ref[...], g0_ref[...])) * qscale
        for h in range(N_HEADS):
            qa_s[h, rows, :] = qa[:, h * HEAD_DIM:(h + 1) * HEAD_DIM].astype(BF16)
        yield
        ka_n = _group_rms(proj(C_KA, LANES), bd_ref[0:LANES, 0:LANES], g1_ref[...])
        ka = rope_wide(ka_n)
        va = proj(C_VA, LANES)
        qw = rope_wide(proj(C_QW, 512)) * qscale
        for h in range(N_HEADS):
            qw_s[h, rows, :] = qw[:, h * HEAD_DIM:(h + 1) * HEAD_DIM].astype(BF16)
        yield
        kw_raw = proj(C_KW, LANES)
        kw = rope_wide(kw_raw)
        vw = proj(C_VW, LANES)
        for kv in range(N_KV):
            sl = slice(kv * HEAD_DIM, (kv + 1) * HEAD_DIM)
            ka_s[kv, rows, :] = ka[:, sl].astype(BF16)
            kw_s[kv, rows, :] = kw[:, sl].astype(BF16)
        yield
        qm_s[rows, :] = proj(C_QM, 256).astype(BF16)
        km_s[rows, :] = (proj(C_KM, 256) * (ML_DK ** -0.5)).astype(BF16)
        yield
        vm = proj(C_VM, 512)
        yield
        om_s[rows, :] = _sigmoid(proj(C_OM, 512)).astype(BF16)
        g_s[rows, :] = proj(C_GT, LANES) + gb_ref[...]
        yield
        for j in range(chunks_per_block):
            cj = i * chunks_per_block + j
            cr = slice(j * CHUNK, (j + 1) * CHUNK)
            va_t = jnp.transpose(va[cr, :])
            vw_t = jnp.transpose(vw[cr, :])
            for kv in range(N_KV):
                hd = slice(kv * HEAD_DIM, (kv + 1) * HEAD_DIM)
                vat_s[cj, kv, 0:HEAD_DIM, :] = va_t[hd, :].astype(BF16)
                vat_s[cj, kv, HEAD_DIM:VT_ROWS, :] = ones_tile
                vwt_s[cj, kv, 0:HEAD_DIM, :] = vw_t[hd, :].astype(BF16)
                vwt_s[cj, kv, HEAD_DIM:VT_ROWS, :] = ones_tile
            if not lat:
                tc = slice(i * row_block + j * CHUNK, i * row_block + (j + 1) * CHUNK)
                ka_t = jnp.transpose(ka_n[cr, :])
                kw_t = jnp.transpose(kw_raw[cr, :])
                for kv in range(N_KV):
                    hd = slice(kv * HEAD_DIM, (kv + 1) * HEAD_DIM)
                    ka_o[kv, :, tc] = ka_t[hd, :]
                    va_o[kv, :, tc] = va_t[hd, :]
                    kw_o[kv, :, tc] = kw_t[hd, :]
                    vw_o[kv, :, tc] = vw_t[hd, :]
            for h in range(ML_HEADS):
                vmt_s[cj, h, 0:ML_DV, :] = jnp.transpose(vm[cr, h * ML_DV:(h + 1) * ML_DV]).astype(BF16)
                vmt_s[cj, h, ML_DV:ST_ROWS, :] = ones_tile
            yield

    if not lat:
        @pl.when(step == 0)
        def _():
            _run(phase1_norm(0))
        yield from phase1_proj(0)
    else:
        def body1(i, carry):
            _run(phase1_norm(i))
            _run(phase1_proj(i))
            return carry
        lax.fori_loop(0, n_rows, body1, 0)

    if lat:
        for j in range(PAST_LEN // CHUNK):
            cols = slice(j * CHUNK, (j + 1) * CHUNK)
            ka_c = jnp.transpose(jnp.concatenate([cka_ref[kv, :, cols] for kv in range(N_KV)], axis=0))
            kw_c = jnp.transpose(jnp.concatenate([ckw_ref[kv, :, cols] for kv in range(N_KV)], axis=0))
            for kv in range(N_KV):
                hd = slice(kv * HEAD_DIM, (kv + 1) * HEAD_DIM)
                ka_s[kv, T + j * CHUNK:T + (j + 1) * CHUNK, :] = ka_c[:, hd].astype(BF16)
                kwc_s[kv, cols, :] = kw_c[:, hd].astype(BF16)
                vat_s[n_chunks + j, kv, 0:HEAD_DIM, :] = cva_ref[kv, :, cols].astype(BF16)
                vat_s[n_chunks + j, kv, HEAD_DIM:VT_ROWS, :] = ones_tile
                vwct_s[kv, 0:HEAD_DIM, cols] = cvw_ref[kv, :, cols].astype(BF16)
                vwct_s[kv, HEAD_DIM:VT_ROWS, cols] = ones_tile

    tq = CHUNK
    scores_in_flight = 2 if lat else 2 * N_KV
    m_cols = N_GROUP * tq

    def sink_row(kv):
        col = lax.broadcasted_iota(jnp.int32, (1, m_cols), 1)
        row = jnp.full((1, m_cols), sink_ref[layer * N_HEADS + kv * N_GROUP], F32)
        for g in range(1, N_GROUP):
            row = jnp.where(col >= g * tq, sink_ref[layer * N_HEADS + kv * N_GROUP + g], row)
        return row * LOG2E

    def load_q(q_s, kv, q0):
        return q_s[kv * N_GROUP:(kv + 1) * N_GROUP, pl.ds(q0, tq), :].reshape(m_cols, HEAD_DIM)

    def store_heads(dst, q0, kv, o_t):
        for p in range(N_GROUP // 2):
            blk = jnp.concatenate([o_t[:, (2 * p) * tq:(2 * p + 1) * tq],
                                   o_t[:, (2 * p + 1) * tq:(2 * p + 2) * tq]], axis=0)
            c0 = (kv * N_GROUP + 2 * p) * HEAD_DIM
            dst[pl.ds(q0, tq), c0:c0 + LANES] = jnp.transpose(blk).astype(dst.dtype)

    def key_tiles(k_ref, vt_ref, kv, n_key_chunks, chunks_per_tile):
        def tile(c0):
            def load():
                k = k_ref[kv, c0 * CHUNK:(c0 + chunks_per_tile) * CHUNK, :]
                v_t = [vt_ref[c0 + j, kv] for j in range(chunks_per_tile)]
                return k, (v_t[0] if len(v_t) == 1 else jnp.concatenate(v_t, axis=1)), None
            return load
        return [tile(c0) for c0 in range(0, n_key_chunks, chunks_per_tile)]

    def band_tiles(kv, n, q0):
        j0 = jnp.clip(n - 1, 0, n_chunks - 3)
        w0 = pl.multiple_of(j0 * CHUNK, CHUNK)

        def band(c0, nc):
            def load():
                rows = nc * CHUNK
                kpos = w0 + c0 * CHUNK + lax.broadcasted_iota(jnp.int32, (rows, m_cols), 0)
                qpos = q0 + (lax.broadcasted_iota(jnp.int32, (rows, m_cols), 1) & (tq - 1))
                v_t = [vwt_s[j0 + c0 + j, kv] for j in range(nc)]
                return (kw_s[kv, pl.ds(w0 + c0 * CHUNK, rows), :],
                        v_t[0] if nc == 1 else jnp.concatenate(v_t, axis=1),
                        jnp.abs(kpos - qpos) <= WINDOW)
            return load
        return [band(0, 2), band(2, 1), lambda: (kwc_s[kv], vwct_s[kv], None)]

    def attend(n, q0, chunks_per_tile):
        chains, dsts = [], []
        n_key_chunks = ka_s.shape[1] // CHUNK
        for kv in range(N_KV):
            chains.append((load_q(qa_s, kv, q0), key_tiles(ka_s, vat_s, kv, n_key_chunks, chunks_per_tile), None))
            dsts.append((ya_s, kv))
        for kv in range(N_KV):
            tiles = band_tiles(kv, n, q0) if lat else key_tiles(kw_s, vwt_s, kv, n_chunks, chunks_per_tile)
            chains.append((load_q(qw_s, kv, q0), tiles, sink_row(kv)))
            dsts.append((yw_s, kv))
        results = []
        yield from _attend_chains(chains, results, scores_in_flight)
        for (dst, kv), o_t in zip(dsts, results):
            store_heads(dst, q0, kv, o_t)

    n_pairs = ML_HEADS // 2
    hmt_s[...] = jnp.zeros_like(hmt_s)
    for d in range(2):
        if lat:
            for pr in range(n_pairs):
                c_s[d * n_pairs + pr, 0:ML_DV, :] = jnp.transpose(
                    jnp.concatenate([c0_ref[d, 2 * pr], c0_ref[d, 2 * pr + 1]], axis=0))
        for h in range(ML_HEADS):
            idx = d * n_pairs + h // 2
            hl = slice((h % 2) * ML_DK, (h % 2 + 1) * ML_DK)
            c_s[idx, ML_DV:ST_ROWS, hl] = jnp.zeros((BF16_ROWS, ML_DK), F32)
            if lat:
                c_s[idx, ML_DV:ML_DV + 1, hl] = n0_ref[d, h:h + 1, :]
                m_s[d * ML_HEADS + h:d * ML_HEADS + h + 1, :] = jnp.broadcast_to(
                    m0_ref[d:d + 1, h:h + 1], (1, LANES))
            else:
                c_s[idx, 0:ML_DV, hl] = jnp.zeros((ML_DV, ML_DK), F32)
                m_s[d * ML_HEADS + h:d * ML_HEADS + h + 1, :] = jnp.zeros((1, LANES), F32)

    L = CHUNK
    s_idx = lax.broadcasted_iota(jnp.int32, (L, ML_HEADS * L), 0)
    t_idx = lax.broadcasted_iota(jnp.int32, (L, ML_HEADS * L), 1) & (L - 1)
    lane_row = lax.broadcasted_iota(jnp.int32, (1, LANES), 1)
    low_half = lax.broadcasted_iota(jnp.int32, (L, LANES), 1) < ML_DK
    ones_ll = jnp.ones((L, L), BF16)
    zeros_ll = jnp.zeros((L, L), BF16)

    def heads_row(src, r0, c0=0):
        return jnp.concatenate([src[r0 + h:r0 + h + 1, c0:c0 + L] for h in range(ML_HEADS)], axis=1)

    def block_diag_rows(x):
        zero = jnp.zeros_like(x)
        return jnp.concatenate([jnp.where(low_half, x, zero), jnp.where(low_half, zero, x)], axis=0)

    def mlstm_streams(streams):
        m_state = {d: heads_row(m_s, d * ML_HEADS) for d in sorted({d for d, _ in streams})}
        c_state = {d: [c_s[d * n_pairs + pr] for pr in range(n_pairs)] for d in m_state}
        work = []
        for d, cc in streams:
            r0 = cc * L if isinstance(cc, int) else pl.multiple_of(cc * L, L)
            rows = pl.ds(r0, L)
            g = g_s[rows, :]
            tri_r = triu_ref[...] if d == 0 else tril_ref[...]
            gi0 = 2 * ML_HEADS * d

            def row_sums():
                g_t = jnp.transpose(g)[0:N_GATES, :]
                ft_hi, ft_lo = _split(_log_sigmoid(g_t))
                tr = jnp.concatenate([tri_r, ones_ll], axis=1)
                return g_t, _dot(ft_hi, tr) + _dot(ft_lo, tr)

            if lat:
                g_t, ct = row_sums()
                r_rows = g_t[gi0:gi0 + 8, :] - pltpu.roll(ct[gi0:gi0 + 8, 0:L], ML_HEADS, 0)
                r = jnp.transpose(jnp.concatenate([r_rows, jnp.zeros((L - 8, L), F32)], axis=0))
                r_lane0 = 0
            else:
                f_hi, f_lo = _split(_log_sigmoid(g))
                tri_c = tril_ref[...] if d == 0 else triu_ref[...]
                cum = _dot(tri_c, f_hi) + _dot(tri_c, f_lo)
                r = g - pltpu.roll(cum, LANES - ML_HEADS, 1)
                r_lane0 = gi0
            rb = jnp.concatenate([jnp.broadcast_to(r[:, r_lane0 + h:r_lane0 + h + 1], (L, L))
                                  for h in range(ML_HEADS)], axis=1)
            if not lat:
                g_t, ct = row_sums()
            work.append(dict(d=d, cc=cc, rows=rows, rb=rb, g_t=g_t, ct=ct))
            yield
        for w in work:
            d = w["d"]
            valid = (s_idx <= t_idx) if d == 0 else (s_idx >= t_idx)
            gi0 = 2 * ML_HEADS * d
            gf0 = gi0 + ML_HEADS
            b_row = heads_row(w["ct"], gf0)
            b_last = heads_row(w["ct"], gf0, L)
            i_row = heads_row(w["g_t"], gi0)
            m_prev = m_state[d]
            a_row = b_row + m_prev
            dm = jnp.where(valid, b_row + w["rb"], -jnp.inf)
            mt = jnp.maximum(a_row, jnp.max(dm, axis=0, keepdims=True))
            w["p"] = jnp.exp(dm - mt)
            w["w_inter"] = jnp.exp(a_row - mt)
            w["floor"] = jnp.exp(-mt)
            g_row = b_last - b_row + i_row
            g_max = jnp.concatenate(
                [jnp.broadcast_to(jnp.max(g_row[:, h * L:(h + 1) * L], axis=-1, keepdims=True), (1, L))
                 for h in range(ML_HEADS)], axis=1)
            m_new = jnp.maximum(b_last + m_prev, g_max)
            w["ws"] = jnp.exp(g_row - m_new)
            w["wc"] = jnp.exp(b_last + m_prev - m_new)
            m_state[d] = m_new
            yield
        for w in work:
            rows, cc = w["rows"], w["cc"]
            w["pairs"] = []
            for pr in range(n_pairs):
                lanes = slice(pr * LANES, (pr + 1) * LANES)
                cols = slice(pr * 2 * L, (pr + 1) * 2 * L)
                k_pair = km_s[rows, lanes]
                q_bd = block_diag_rows(qm_s[rows, lanes])
                k_bd = block_diag_rows(k_pair)
                s_t = (_dot_tb(k_pair, q_bd) * w["p"][:, cols]).astype(BF16)
                s_bd = jnp.concatenate(
                    [jnp.concatenate([s_t[:, 0:L], zeros_ll], axis=1),
                     jnp.concatenate([zeros_ll, s_t[:, L:2 * L]], axis=1)], axis=0)
                v_t = jnp.concatenate([vmt_s[cc, 2 * pr], vmt_s[cc, 2 * pr + 1]], axis=1)
                intra = _dot(v_t, s_bd)
                wv = (v_t.astype(F32) * w["ws"][:, cols]).astype(BF16)
                w["pairs"].append((q_bd, intra, _dot(wv, k_bd)))
            yield
        for w in work:
            d, cc = w["d"], w["cc"]
            for pr, (q_bd, intra, update) in enumerate(w["pairs"]):
                cols = slice(pr * 2 * L, (pr + 1) * 2 * L)
                state = c_state[d][pr]
                tot = intra + w["w_inter"][:, cols] * _dot_tb(state.astype(BF16), q_bd)
                h_t = tot[0:ML_DV, :] / jnp.maximum(jnp.abs(tot[ML_DV:ML_DV + 1, :]), w["floor"][:, cols])
                for e in range(2):
                    hr = slice((2 * pr + e) * ML_DV, (2 * pr + e + 1) * ML_DV)
                    hmt_s[cc, hr, :] += h_t[:, e * L:(e + 1) * L]
                wc = w["wc"]
                wc_pair = jnp.where(lane_row < ML_DK, wc[:, 2 * pr * L:(2 * pr + 1) * L],
                                    wc[:, (2 * pr + 1) * L:(2 * pr + 2) * L])
                c_state[d][pr] = wc_pair * state + update
            yield
        for d in m_state:
            for h in range(ML_HEADS):
                m_s[d * ML_HEADS + h:d * ML_HEADS + h + 1, :] = m_state[d][:, h * L:(h + 1) * L]
            for pr in range(n_pairs):
                c_s[d * n_pairs + pr] = c_state[d][pr]

    def scan_streams(cs):
        return [(d, c if d == 0 else n_chunks - 1 - c) for c in cs for d in range(2)]

    if lat:
        def body2(n, carry):
            _run(_interleave([(attend(n, pl.multiple_of(n * tq, tq), 2), 1),
                              (mlstm_streams(scan_streams([n])), 1)]))
            return carry
        lax.fori_loop(0, n_chunks, body2, 0)
    else:
        def all_tiles():
            for n in range(T // tq):
                yield from attend(n, n * tq, 2)
        yield from _interleave([(all_tiles(), 1), (mlstm_streams(scan_streams(range(n_chunks))), 3)])

    if not lat:
        for d in range(2):
            for pr in range(n_pairs):
                c_t = jnp.transpose(c_s[d * n_pairs + pr, 0:ML_DV, :])
                for e in range(2):
                    c_o[d, 2 * pr + e] = c_t[e * ML_DK:(e + 1) * ML_DK, :]
            for h in range(ML_HEADS):
                idx = d * n_pairs + h // 2
                hl = slice((h % 2) * ML_DK, (h % 2 + 1) * ML_DK)
                n_o[d, h:h + 1, :] = c_s[idx, ML_DV:ML_DV + 1, hl]
                m_o[d:d + 1, h:h + 1] = m_s[d * ML_HEADS + h:d * ML_HEADS + h + 1, 0:1]

    def phase3_merge(i, out):
        rows = row_slice(i, 0, row_block)
        u = u_s[rows, :]

        def proj(c0, width):
            if wt_s is not None and c0 >= C_ZW:
                return _dot(u, wt_s[:, c0 - C_ZW:c0 - C_ZW + width])
            return _dot(u, w_ref[:, c0:c0 + width])

        hm = jnp.concatenate(
            [jnp.concatenate([jnp.transpose(hmt_s[i * chunks_per_block + j, h * ML_DV:(h + 1) * ML_DV, :])
                              for h in range(ML_HEADS)], axis=1)
             for j in range(chunks_per_block)], axis=0)
        ys = (ya_s[rows, :].astype(F32), om_s[rows, :].astype(F32) * hm, yw_s[rows, :].astype(F32))
        yield
        merged = None
        for b, (yb, zc) in enumerate(zip(ys, (C_ZA, C_ZM, C_ZW))):
            z = proj(zc, BRANCH_W)
            t = (yb * (z * _sigmoid(z))).astype(BF16)
            yield
            pb = _sigmoid(proj(C_GMERGE + b * D_MODEL, D_MODEL)) * _dot(t, wb_ref[b])
            merged = pb if merged is None else merged + pb
            yield
        out.append(_dot(merged.astype(BF16), wo_ref[...]))
        yield

    def phase3_norm(i, o):
        for j in range(0, row_block, norm_rows):
            piece = row_slice(i, j, norm_rows)
            hres = ALPHA * x_ref[piece, :] + gate * o[j:j + norm_rows, :]
            y_ref[piece, :] = _layer_norm(hres) * lng_ref[...] + lnb_ref[...]
            yield

    if lat:
        @pl.when(first_step)
        def _():
            for cp in late_copies:
                cp.wait()

    def phase3(i):
        box = []
        yield from phase3_merge(i, box)
        yield from phase3_norm(i, box[0])

    if not lat:
        box = []
        yield from _interleave([(phase3_merge(0, box), 1), (phase1_norm(0, x_next_ref, u_next_s), 1)])
        yield from phase3_norm(0, box[0])
    else:
        def body4(i, carry):
            _run(phase3(i))
            return carry
        lax.fori_loop(0, n_rows, body4, 0)


def _const_spec(shape):
    nd = len(shape)
    return pl.BlockSpec(shape, lambda b: (0,) * nd, pipeline_mode=pl.Buffered(1))


def _layer_spec(shape, layer):
    nd = len(shape) - 1
    return pl.BlockSpec((None,) + tuple(shape[1:]), lambda b: (layer,) + (0,) * nd,
                        pipeline_mode=pl.Buffered(1))


def _layer_call(lat, layer, x, mod, weights, consts, extra):
    B, T, _ = x.shape
    S = T + PAST_LEN if lat else T
    n_chunks = T // CHUNK

    seq_spec = pl.BlockSpec((None, T, D_MODEL), lambda b: (b, 0, 0))
    out_seq_spec = seq_spec
    if lat:
        mod_spec = pl.BlockSpec((None, None, 3, D_MODEL), lambda b: (layer, b + 1, 0, 0))
    else:
        mod_spec = pl.BlockSpec((None, None, 3, D_MODEL), lambda b: (layer, 0, 0, 0))
    *vmem_consts, sink = consts
    in_specs = [seq_spec, mod_spec]
    w_all, wb_all, wo_all = weights[:3]
    if lat:
        in_specs.append(
            pl.BlockSpec((None, D_MODEL, C_ZW), lambda b: (layer, 0, 0), pipeline_mode=pl.Buffered(1)))
        in_specs += [pl.BlockSpec(memory_space=pl.ANY)] * 2
        in_specs += [_layer_spec(a.shape, layer) for a in weights[3:]]
    else:
        in_specs += [_layer_spec(a.shape, layer) for a in weights]
    in_specs += [_const_spec(a.shape) for a in vmem_consts]
    in_specs.append(pl.BlockSpec(memory_space=pltpu.SMEM))
    args = [x, mod, *weights, *vmem_consts, sink]

    if lat:
        cos, sa, sb, cka, cva, ckw, cvw, sc, sn, sm = extra
        in_specs += [_const_spec(cos.shape)] * 3
        kv_spec = pl.BlockSpec((None, None, N_KV, HEAD_DIM, PAST_LEN), lambda b: (b, layer, 0, 0, 0))
        in_specs += [kv_spec] * 4
        in_specs += [
            pl.BlockSpec((None, None, 2, ML_HEADS, ML_DK, ML_DV), lambda b: (b, layer, 0, 0, 0, 0)),
            pl.BlockSpec((None, None, 2, ML_HEADS, ML_DK), lambda b: (b, layer, 0, 0, 0)),
            pl.BlockSpec((None, None, 2, ML_HEADS), lambda b: (b, layer, 0, 0)),
        ]
        args += [cos, sa, sb, cka, cva, ckw, cvw, sc, sn, sm]
        out_specs = out_seq_spec
        out_shape = jax.ShapeDtypeStruct((B, T, D_MODEL), F32)
        aliases = {}
    else:
        kv_shape = (B, DEPTH, N_KV, HEAD_DIM, T)
        c_shape = (B, DEPTH, 2, ML_HEADS, ML_DK, ML_DV)
        if layer == 0:
            aliases = {}
            ld = DEPTH
            layer_idx = 0
        else:
            first_acc = len(args)
            in_specs += [pl.BlockSpec(memory_space=pl.ANY)] * len(extra)
            args += list(extra)
            aliases = {first_acc + i: 1 + i for i in range(len(extra))}
            assert len(extra) == CTX_ALIASED
            ld = None
            layer_idx = layer
        in_specs.append(pl.BlockSpec((None, T, D_MODEL), lambda b: (jnp.minimum(b + 1, B - 1), 0, 0)))
        args.append(x)
        kv_out = pl.BlockSpec((None, ld, N_KV, HEAD_DIM, T), lambda b: (b, layer_idx, 0, 0, 0))
        out_specs = [
            seq_spec, kv_out, kv_out, kv_out, kv_out,
            pl.BlockSpec((None, ld, 2, ML_HEADS, ML_DK, ML_DV), lambda b: (b, layer_idx, 0, 0, 0, 0)),
            pl.BlockSpec((None, 2, ML_HEADS, ML_DK), lambda b: (b, 0, 0, 0)),
            pl.BlockSpec((None, 2, ML_HEADS), lambda b: (b, 0, 0)),
        ]
        out_shape = [jax.ShapeDtypeStruct((B, T, D_MODEL), F32)]
        out_shape += [jax.ShapeDtypeStruct(kv_shape, F32)] * 4 + [jax.ShapeDtypeStruct(c_shape, F32)]
        out_shape += [
            jax.ShapeDtypeStruct((B, 2, ML_HEADS, ML_DK), F32),
            jax.ShapeDtypeStruct((B, 2, ML_HEADS), F32),
        ]

    scratch = [
        ((T, D_MODEL) if lat else (2, T, D_MODEL), BF16),
        ((N_HEADS, T, HEAD_DIM), BF16),
        ((N_HEADS, T, HEAD_DIM), BF16),
        ((N_KV, S, HEAD_DIM), BF16),
        ((S // CHUNK, N_KV, VT_ROWS, CHUNK), BF16),
        ((N_KV, T, HEAD_DIM), BF16),
        ((n_chunks, N_KV, VT_ROWS, CHUNK), BF16),
        ((T, ML_HEADS * ML_DK), BF16),
        ((T, ML_HEADS * ML_DK), BF16),
        ((n_chunks, ML_HEADS, ST_ROWS, CHUNK), BF16),
        ((T, ML_HEADS * ML_DV), BF16),
        ((T, LANES), F32),
        ((T, BRANCH_W), BF16),
        ((T, BRANCH_W), BF16),
        ((n_chunks, ML_HEADS * ML_DV, CHUNK), F32),
        ((ML_HEADS, ST_ROWS, 2 * ML_DK), F32),
        ((2 * ML_HEADS, LANES), F32),
    ]
    if lat:
        scratch += [((N_KV, PAST_LEN, HEAD_DIM), BF16),
                    ((N_KV, VT_ROWS, PAST_LEN), BF16)]
    scratch = [pltpu.VMEM(shape, dtype) for shape, dtype in scratch]
    if lat:
        in_specs.append(pl.BlockSpec(memory_space=pl.ANY))
        args.append(w_all)
        scratch += [
            pltpu.VMEM((D_MODEL, W_COLS - C_ZW), BF16),
            pltpu.VMEM(tuple(wb_all.shape[1:]), BF16),
            pltpu.VMEM(tuple(wo_all.shape[1:]), BF16),
            pltpu.SemaphoreType.DMA((3,)),
        ]

    return pl.pallas_call(
        functools.partial(_layer_kernel, lat, T, layer),
        grid=(B,),
        in_specs=in_specs,
        out_specs=out_specs,
        out_shape=out_shape,
        scratch_shapes=scratch,
        input_output_aliases=aliases,
        compiler_params=pltpu.CompilerParams(
            dimension_semantics=("arbitrary",),
            vmem_limit_bytes=LATENT_VMEM_LIMIT_BYTES if lat else VMEM_LIMIT_BYTES),
        name=("latent_layer" if lat else "context_layer") + str(layer),
    )(*args)


def _rope_tables(T):
    f32 = np.float32
    rows = T // GRID_W
    row = np.repeat(np.arange(rows, dtype=f32), GRID_W)
    col = np.tile(np.arange(GRID_W, dtype=f32), rows)
    inv = (f32(ROPE_THETA) ** (-np.arange(0, AXIS_DIM, 2, dtype=f32) / f32(AXIS_DIM))).astype(f32)
    ar = row[:, None] * inv
    ac = col[:, None] * inv
    ang = np.concatenate([ar, ar, ac, ac], axis=-1)
    ang = np.concatenate([ang, ang], axis=-1)
    first_half = (np.arange(LANES) % AXIS_DIM) < (AXIS_DIM // 2)
    cos = np.cos(ang).astype(f32)
    sin = np.sin(ang).astype(f32)
    sa = np.where(first_half[None, :], -sin, f32(0.0))
    sb = np.where(first_half[None, :], f32(0.0), sin)
    return jnp.asarray(cos), jnp.asarray(sa), jnp.asarray(sb)


def _constants(sink_logit):
    grp = np.arange(N_HEADS * HEAD_DIM) // HEAD_DIM
    bd = jnp.asarray(grp[:, None] == grp[None, :], dtype=BF16)
    ti = np.arange(CHUNK)
    tril = jnp.asarray(ti[None, :] <= ti[:, None], dtype=BF16)
    triu = jnp.asarray(ti[None, :] >= ti[:, None], dtype=BF16)
    return bd, tril, triu, sink_logit.reshape(-1)


def kernel(x_prompt, x_sample, cache_attn_k, cache_attn_v, cache_win_k, cache_win_v, state_mlstm_C,
           state_mlstm_n, state_mlstm_m, c, c_ctx, w_mod, b_mod, w_in, qk_gain, sink_logit,
           mlstm_gate_bias, w_branch, w_out, ln_gain, ln_bias):
    dec_b = x_sample.shape[0]
    assert dec_b + 1 <= 8
    cond = jnp.concatenate([c_ctx[None, :], c, jnp.zeros((8 - 1 - dec_b, D_MODEL), F32)], axis=0)
    mod = _modulation(cond, w_mod, b_mod).reshape(DEPTH, 8, 3, D_MODEL)

    consts = _constants(sink_logit)
    cos, sa, sb = _rope_tables(x_sample.shape[1])
    lane_pad = jnp.zeros((DEPTH, 1, LANES - N_GATES), F32)
    weights = (
        *_prep_weights(jnp.swapaxes(w_in, 1, 2), w_branch, w_out),
        jnp.tile(qk_gain[:, 0:1, :], (1, 1, N_HEADS)),
        jnp.tile(qk_gain[:, 1:2, :], (1, 1, N_KV)),
        jnp.concatenate([mlstm_gate_bias.reshape(DEPTH, 1, N_GATES), lane_pad], axis=2),
        ln_gain[:, None, :], ln_bias[:, None, :],
    )

    ctx = None
    cached_t = [jnp.swapaxes(a, -1, -2)
                for a in (cache_attn_k, cache_attn_v, cache_win_k, cache_win_v, state_mlstm_C)]
    xp, xs = x_prompt, x_sample
    small = []
    for l in range(DEPTH):
        outs = _layer_call(False, l, xp, mod, weights, consts, ctx)
        xp = outs[0]
        ctx = tuple(outs[1:6])
        small.append(outs[6:])
        extra = (cos, sa, sb, *cached_t, state_mlstm_n, state_mlstm_m)
        xs = _layer_call(True, l, xs, mod, weights, consts, extra)
    new_n, new_m = [jnp.stack([small[l][i] for l in range(DEPTH)], axis=1) for i in range(2)]
    return (xp, xs, *[jnp.swapaxes(a, -1, -2) for a in ctx], new_n, new_m)
```
